```python
import math
import jax, jax.numpy as jnp
from jax import lax
import numpy as np

D_MODEL = 2048
BATCH = 8
SEQ = 2048
DEPTH = 2

HEAD_DIM = 128
N_HEADS = D_MODEL // HEAD_DIM
N_SB_HEADS = N_HEADS // 2
N_FOX_HEADS = N_HEADS - N_SB_HEADS
D_FF = 256 * ((8 * D_MODEL // 3 + 255) // 256)
Q_BLOCK = 128
ROPE_THETA = 500000.0
ROPE_DIMS = HEAD_DIM // 4
DILATED_PATTERNS = ((128, 1), (512, 4), (2048, 16))
RMS_EPS = 1e-6
NEG_INF = -1e30
N_EVEN = (DEPTH + 1) // 2
N_ODD = DEPTH // 2

kernel_name = "hybrid_sb_fox_dilated_macaron"


def rmsnorm(x, g):
    xf = x.astype(jnp.float32)
    y = xf * lax.rsqrt(jnp.mean(xf * xf, axis=-1, keepdims=True) + RMS_EPS)
    return (y * g.astype(jnp.float32)).astype(x.dtype)


def swiglu(x, w_gate, w_up, w_down):
    return (jax.nn.silu(x @ w_gate) * (x @ w_up)) @ w_down


def partial_rope(x, positions):
    half = ROPE_DIMS // 2
    freqs = ROPE_THETA ** (-jnp.arange(half, dtype=jnp.float32) / half)
    ang = positions[:, None] * freqs[None, :]
    cos = jnp.cos(ang)[None, :, None, :]
    sin = jnp.sin(ang)[None, :, None, :]
    xr = x[..., :ROPE_DIMS].astype(jnp.float32)
    x1, x2 = xr[..., :half], xr[..., half:]
    rot = jnp.concatenate([x1 * cos - x2 * sin, x2 * cos + x1 * sin], axis=-1)
    return jnp.concatenate([rot.astype(x.dtype), x[..., ROPE_DIMS:]], axis=-1)


def stick_breaking_attention(q, k, v):
    B, H, S, hd = q.shape
    n_blocks = S // Q_BLOCK
    scale = hd ** -0.5
    key_pos = jnp.arange(S)

    def block(i):
        qb = lax.dynamic_slice_in_dim(q, i * Q_BLOCK, Q_BLOCK, axis=2)
        z = jnp.einsum('bhqd,bhkd->bhqk', qb, k).astype(jnp.float32) * scale
        q_pos = i * Q_BLOCK + jnp.arange(Q_BLOCK)
        strict = key_pos[None, :] < q_pos[:, None]
        log_not_beta = jnp.where(strict, -jax.nn.softplus(z), 0.0)
        after = lax.cumsum(log_not_beta, axis=3, reverse=True) - log_not_beta
        w = jnp.where(strict, jnp.exp(jax.nn.log_sigmoid(z) + after), 0.0)
        return jnp.einsum('bhqk,bhkd->bhqd', w.astype(v.dtype), v)

    out = lax.map(block, jnp.arange(n_blocks))
    return jnp.moveaxis(out, 0, 2).reshape(B, H, S, hd)


def forgetting_attention(q, k, v, log_f):
    B, H, S, hd = q.shape
    n_blocks = S // Q_BLOCK
    scale = hd ** -0.5
    cum_f = lax.cumsum(log_f, axis=2)
    key_pos = jnp.arange(S)

    def block(i):
        qb = lax.dynamic_slice_in_dim(q, i * Q_BLOCK, Q_BLOCK, axis=2)
        fq = lax.dynamic_slice_in_dim(cum_f, i * Q_BLOCK, Q_BLOCK, axis=2)
        z = (jnp.einsum('bhqd,bhkd->bhqk', qb, k).astype(jnp.float32) * scale
             + fq[..., :, None] - cum_f[:, :, None, :])
        q_pos = i * Q_BLOCK + jnp.arange(Q_BLOCK)
        causal = key_pos[None, :] <= q_pos[:, None]
        p = jax.nn.softmax(jnp.where(causal, z, NEG_INF), axis=-1)
        return jnp.einsum('bhqk,bhkd->bhqd', p.astype(v.dtype), v)

    out = lax.map(block, jnp.arange(n_blocks))
    return jnp.moveaxis(out, 0, 2).reshape(B, H, S, hd)


def dilated_branch(q, k, v, window, dilation):
    B, S, H, hd = q.shape
    steps = window // dilation
    L = S // dilation
    n_blocks = -(-L // steps)
    Lp = n_blocks * steps
    scale = hd ** -0.5

    def to_sub(t):
        return t.reshape(B, L, dilation, H, hd).transpose(0, 2, 3, 1, 4)

    qs = jnp.pad(to_sub(q), ((0, 0), (0, 0), (0, 0), (0, Lp - L), (0, 0)))
    qs = qs.reshape(B, dilation, H, n_blocks, steps, hd)

    def banded(t):
        tp = jnp.pad(to_sub(t), ((0, 0), (0, 0), (0, 0), (steps, Lp - L), (0, 0)))
        prev = tp[:, :, :, :Lp].reshape(B, dilation, H, n_blocks, steps, hd)
        cur = tp[:, :, :, steps:].reshape(B, dilation, H, n_blocks, steps, hd)
        return jnp.concatenate([prev, cur], axis=4)

    kb, vb = banded(k), banded(v)
    z = jnp.einsum('brhnqd,brhnkd->brhnqk', qs, kb).astype(jnp.float32) * scale
    a = jnp.arange(steps)[:, None]
    c = jnp.arange(2 * steps)[None, :]
    dist = a + steps - c
    key_sub = jnp.arange(n_blocks)[:, None, None] * steps + c - steps
    valid = (dist >= 0) & (dist <= steps) & (key_sub >= 0)
    z = jnp.where(valid, z, NEG_INF)
    m = jnp.max(z, axis=-1, keepdims=True)
    e = jnp.exp(z - m)
    denom = jnp.sum(e, axis=-1, keepdims=True)
    o = jnp.einsum('brhnqk,brhnkd->brhnqd', (e / denom).astype(v.dtype), vb)
    lse = (m + jnp.log(denom))[..., 0]
    o = o.reshape(B, dilation, H, Lp, hd)[:, :, :, :L]
    o = o.transpose(0, 3, 1, 2, 4).reshape(B, S, H, hd)
    lse = lse.reshape(B, dilation, H, Lp)[..., :L].transpose(0, 3, 1, 2).reshape(B, S, H)
    return o, lse


def dilated_attention(q, k, v):
    outs, lses = [], []
    for window, dilation in DILATED_PATTERNS:
        o, lse = dilated_branch(q, k, v, window, dilation)
        outs.append(o)
        lses.append(lse)
    w = jax.nn.softmax(jnp.stack(lses, axis=0), axis=0)
    out = jnp.sum(w[..., None] * jnp.stack(outs, axis=0).astype(jnp.float32), axis=0)
    return out.astype(q.dtype)


def sb_fox_mixer(x, w_in, b_forget, w_out):
    B, S, _ = x.shape
    h = x @ w_in
    qkv = h[..., :3 * D_MODEL].reshape(B, S, 3, N_HEADS, HEAD_DIM).transpose(2, 0, 3, 1, 4)
    q, k, v = qkv[0], qkv[1], qkv[2]
    log_f = jax.nn.log_sigmoid(
        (h[..., 3 * D_MODEL:] + b_forget).astype(jnp.float32)).transpose(0, 2, 1)
    o_sb = stick_breaking_attention(q[:, :N_SB_HEADS], k[:, :N_SB_HEADS], v[:, :N_SB_HEADS])
    o_fox = forgetting_attention(q[:, N_SB_HEADS:], k[:, N_SB_HEADS:], v[:, N_SB_HEADS:], log_f)
    o = jnp.concatenate([o_sb, o_fox], axis=1).transpose(0, 2, 1, 3).reshape(B, S, D_MODEL)
    return o @ w_out


def dilated_mixer(x, w_qkv, w_out):
    B, S, _ = x.shape
    qkv = (x @ w_qkv).reshape(B, S, 3, N_HEADS, HEAD_DIM)
    positions = jnp.arange(S, dtype=jnp.float32)
    q = partial_rope(qkv[:, :, 0], positions)
    k = partial_rope(qkv[:, :, 1], positions)
    v = qkv[:, :, 2]
    o = dilated_attention(q, k, v).reshape(B, S, D_MODEL)
    return o @ w_out


def _fwd_setup_inputs(seed: int = 0) -> dict:
    key = jax.random.key(seed)
    ks = jax.random.split(key, 16)
    f32 = jnp.float32

    def dense(k, shape, fan_in):
        return jax.random.normal(k, shape, f32) * fan_in ** -0.5

    x = jax.random.normal(ks[0], (BATCH, SEQ, D_MODEL), f32)
    norm_g = 1.0 + 0.02 * jax.random.normal(ks[1], (DEPTH, 3, D_MODEL), f32)
    ffn1_w_gate = dense(ks[2], (DEPTH, D_MODEL, D_FF), D_MODEL)
    ffn1_w_up = dense(ks[3], (DEPTH, D_MODEL, D_FF), D_MODEL)
    ffn1_w_down = dense(ks[4], (DEPTH, D_FF, D_MODEL), D_FF)
    ffn2_w_gate = dense(ks[5], (DEPTH, D_MODEL, D_FF), D_MODEL)
    ffn2_w_up = dense(ks[6], (DEPTH, D_MODEL, D_FF), D_MODEL)
    ffn2_w_down = dense(ks[7], (DEPTH, D_FF, D_MODEL), D_FF)
    even_w_in = dense(ks[8], (N_EVEN, D_MODEL, 3 * D_MODEL + N_FOX_HEADS), D_MODEL)
    even_b_forget = 3.0 + 0.5 * jax.random.normal(ks[9], (N_EVEN, N_FOX_HEADS), f32)
    even_w_out = dense(ks[10], (N_EVEN, D_MODEL, D_MODEL), D_MODEL)
    odd_w_qkv = dense(ks[11], (N_ODD, D_MODEL, 3 * D_MODEL), D_MODEL)
    odd_w_out = dense(ks[12], (N_ODD, D_MODEL, D_MODEL), D_MODEL)
    final_norm_g = 1.0 + 0.02 * jax.random.normal(ks[13], (D_MODEL,), f32)
    return {
        "x": x, "norm_g": norm_g,
        "ffn1_w_gate": ffn1_w_gate, "ffn1_w_up": ffn1_w_up, "ffn1_w_down": ffn1_w_down,
        "ffn2_w_gate": ffn2_w_gate, "ffn2_w_up": ffn2_w_up, "ffn2_w_down": ffn2_w_down,
        "even_w_in": even_w_in, "even_b_forget": even_b_forget, "even_w_out": even_w_out,
        "odd_w_qkv": odd_w_qkv, "odd_w_out": odd_w_out, "final_norm_g": final_norm_g,
    }


def _fwd_reference(x, norm_g, ffn1_w_gate, ffn1_w_up, ffn1_w_down, ffn2_w_gate, ffn2_w_up,
              ffn2_w_down, even_w_in, even_b_forget, even_w_out, odd_w_qkv, odd_w_out,
              final_norm_g):
    for layer in range(DEPTH):
        j = layer // 2
        x = x + 0.5 * swiglu(rmsnorm(x, norm_g[layer, 0]),
                             ffn1_w_gate[layer], ffn1_w_up[layer], ffn1_w_down[layer])
        h = rmsnorm(x, norm_g[layer, 1])
        if layer % 2 == 0:
            x = x + sb_fox_mixer(h, even_w_in[j], even_b_forget[j], even_w_out[j])
        else:
            x = x + dilated_mixer(h, odd_w_qkv[j], odd_w_out[j])
        x = x + 0.5 * swiglu(rmsnorm(x, norm_g[layer, 2]),
                             ffn2_w_gate[layer], ffn2_w_up[layer], ffn2_w_down[layer])
    return rmsnorm(x, final_norm_g)


import jax as _jax
import jax.numpy as _jnp

TWIN_FORMAT = 'train_step'
FWD_PARAMS = ['x', 'norm_g', 'ffn1_w_gate', 'ffn1_w_up', 'ffn1_w_down', 'ffn2_w_gate', 'ffn2_w_up', 'ffn2_w_down', 'even_w_in', 'even_b_forget', 'even_w_out', 'odd_w_qkv', 'odd_w_out', 'final_norm_g']
TWIN_WEIGHTS = ['norm_g', 'ffn1_w_gate', 'ffn1_w_up', 'ffn1_w_down', 'ffn2_w_gate', 'ffn2_w_up', 'ffn2_w_down', 'even_w_in', 'even_b_forget', 'even_w_out', 'odd_w_qkv', 'odd_w_out', 'final_norm_g']
TWIN_DIFF_INPUT = 'x'
TWIN_INPUTS = ['x', 'norm_g', 'ffn1_w_gate', 'ffn1_w_up', 'ffn1_w_down', 'ffn2_w_gate', 'ffn2_w_up', 'ffn2_w_down', 'even_w_in', 'even_b_forget', 'even_w_out', 'odd_w_qkv', 'odd_w_out', 'final_norm_g', 'loss_target', 'm_norm_g', 'm_ffn1_w_gate', 'm_ffn1_w_up', 'm_ffn1_w_down', 'm_ffn2_w_gate', 'm_ffn2_w_up', 'm_ffn2_w_down', 'm_even_w_in', 'm_even_b_forget', 'm_even_w_out', 'm_odd_w_qkv', 'm_odd_w_out', 'm_final_norm_g', 'v_norm_g', 'v_ffn1_w_gate', 'v_ffn1_w_up', 'v_ffn1_w_down', 'v_ffn2_w_gate', 'v_ffn2_w_up', 'v_ffn2_w_down', 'v_even_w_in', 'v_even_b_forget', 'v_even_w_out', 'v_odd_w_qkv', 'v_odd_w_out', 'v_final_norm_g']
TWIN_OUTPUTS = ['loss', 'grad_x', 'grad_norm_g', 'grad_ffn1_w_gate', 'grad_ffn1_w_up', 'grad_ffn1_w_down', 'grad_ffn2_w_gate', 'grad_ffn2_w_up', 'grad_ffn2_w_down', 'grad_even_w_in', 'grad_even_b_forget', 'grad_even_w_out', 'grad_odd_w_qkv', 'grad_odd_w_out', 'grad_final_norm_g', 'delta_norm_g', 'delta_ffn1_w_gate', 'delta_ffn1_w_up', 'delta_ffn1_w_down', 'delta_ffn2_w_gate', 'delta_ffn2_w_up', 'delta_ffn2_w_down', 'delta_even_w_in', 'delta_even_b_forget', 'delta_even_w_out', 'delta_odd_w_qkv', 'delta_odd_w_out', 'delta_final_norm_g', 'new_m_norm_g', 'new_m_ffn1_w_gate', 'new_m_ffn1_w_up', 'new_m_ffn1_w_down', 'new_m_ffn2_w_gate', 'new_m_ffn2_w_up', 'new_m_ffn2_w_down', 'new_m_even_w_in', 'new_m_even_b_forget', 'new_m_even_w_out', 'new_m_odd_w_qkv', 'new_m_odd_w_out', 'new_m_final_norm_g', 'new_v_norm_g', 'new_v_ffn1_w_gate', 'new_v_ffn1_w_up', 'new_v_ffn1_w_down', 'new_v_ffn2_w_gate', 'new_v_ffn2_w_up', 'new_v_ffn2_w_down', 'new_v_even_w_in', 'new_v_even_b_forget', 'new_v_even_w_out', 'new_v_odd_w_qkv', 'new_v_odd_w_out', 'new_v_final_norm_g']
TWIN_LEAF_KINDS = {'loss': 'loss', 'grad_x': 'grad_x', 'grad_norm_g': 'grad_w', 'grad_ffn1_w_gate': 'grad_w', 'grad_ffn1_w_up': 'grad_w', 'grad_ffn1_w_down': 'grad_w', 'grad_ffn2_w_gate': 'grad_w', 'grad_ffn2_w_up': 'grad_w', 'grad_ffn2_w_down': 'grad_w', 'grad_even_w_in': 'grad_w', 'grad_even_b_forget': 'grad_w', 'grad_even_w_out': 'grad_w', 'grad_odd_w_qkv': 'grad_w', 'grad_odd_w_out': 'grad_w', 'grad_final_norm_g': 'grad_w', 'delta_norm_g': 'delta_w', 'delta_ffn1_w_gate': 'delta_w', 'delta_ffn1_w_up': 'delta_w', 'delta_ffn1_w_down': 'delta_w', 'delta_ffn2_w_gate': 'delta_w', 'delta_ffn2_w_up': 'delta_w', 'delta_ffn2_w_down': 'delta_w', 'delta_even_w_in': 'delta_w', 'delta_even_b_forget': 'delta_w', 'delta_even_w_out': 'delta_w', 'delta_odd_w_qkv': 'delta_w', 'delta_odd_w_out': 'delta_w', 'delta_final_norm_g': 'delta_w', 'new_m_norm_g': 'new_m', 'new_m_ffn1_w_gate': 'new_m', 'new_m_ffn1_w_up': 'new_m', 'new_m_ffn1_w_down': 'new_m', 'new_m_ffn2_w_gate': 'new_m', 'new_m_ffn2_w_up': 'new_m', 'new_m_ffn2_w_down': 'new_m', 'new_m_even_w_in': 'new_m', 'new_m_even_b_forget': 'new_m', 'new_m_even_w_out': 'new_m', 'new_m_odd_w_qkv': 'new_m', 'new_m_odd_w_out': 'new_m', 'new_m_final_norm_g': 'new_m', 'new_v_norm_g': 'new_v', 'new_v_ffn1_w_gate': 'new_v', 'new_v_ffn1_w_up': 'new_v', 'new_v_ffn1_w_down': 'new_v', 'new_v_ffn2_w_gate': 'new_v', 'new_v_ffn2_w_up': 'new_v', 'new_v_ffn2_w_down': 'new_v', 'new_v_even_w_in': 'new_v', 'new_v_even_b_forget': 'new_v', 'new_v_even_w_out': 'new_v', 'new_v_odd_w_qkv': 'new_v', 'new_v_odd_w_out': 'new_v', 'new_v_final_norm_g': 'new_v'}


def _forward(args):
    return _fwd_reference(*[args[k] for k in FWD_PARAMS])


def _output_shape():
    out = _jax.eval_shape(lambda: _forward(_fwd_setup_inputs(0)))
    return out.shape, out.dtype

N_MICROBATCH = 1
ADAM_LR = 0.001
ADAM_B1 = 0.9
ADAM_B2 = 0.999
ADAM_EPS = 1e-08
ADAM_WD = 0.01
ADAM_STEP = 10
PER_EXAMPLE_BATCH_AXIS = {'x': 0, 'loss_target': 0}
SHARED_INPUTS = []
_WEIGHT_DTYPES = {'norm_g': _jnp.float32, 'ffn1_w_gate': _jnp.float32, 'ffn1_w_up': _jnp.float32, 'ffn1_w_down': _jnp.float32, 'ffn2_w_gate': _jnp.float32, 'ffn2_w_up': _jnp.float32, 'ffn2_w_down': _jnp.float32, 'even_w_in': _jnp.float32, 'even_b_forget': _jnp.float32, 'even_w_out': _jnp.float32, 'odd_w_qkv': _jnp.float32, 'odd_w_out': _jnp.float32, 'final_norm_g': _jnp.float32}
MOMENT_SCALE = {'norm_g': 2.668315e-02, 'ffn1_w_gate': 1.160276e-02, 'ffn1_w_up': 1.124878e-02, 'ffn1_w_down': 1.866247e-02, 'ffn2_w_gate': 9.811825e-03, 'ffn2_w_up': 9.516739e-03, 'ffn2_w_down': 1.577834e-02, 'even_w_in': 2.236748e-02, 'even_b_forget': 1.441965e-01, 'even_w_out': 3.074068e-02, 'odd_w_qkv': 1.037608e-02, 'odd_w_out': 1.220307e-02, 'final_norm_g': 7.999063e+00}


def _to_microbatches(a, axis):
    t = _jnp.moveaxis(a, axis, 0)
    t = t.reshape((N_MICROBATCH, t.shape[0] // N_MICROBATCH) + t.shape[1:])
    return _jnp.moveaxis(t, 1, axis + 1)


def setup_inputs(seed: int = 0) -> dict:
    inp = _fwd_setup_inputs(seed)
    key = _jax.random.fold_in(_jax.random.key(seed), 7919)
    shape, _ = _output_shape()
    out = dict(inp)
    out["loss_target"] = _jax.random.normal(_jax.random.fold_in(key, 0), shape, _jnp.float32)
    for i, name in enumerate(TWIN_WEIGHTS):
        w = inp[name].astype(_jnp.float32)
        if MOMENT_SCALE is None:
            s = _jnp.sqrt(_jnp.mean(_jnp.square(w)) + 1e-30)
        else:
            s = MOMENT_SCALE[name]
        km, kv = _jax.random.split(_jax.random.fold_in(key, i + 1))
        out[name] = w
        out["m_" + name] = s * _jax.random.normal(km, w.shape, _jnp.float32)
        out["v_" + name] = (s * s) * _jax.random.uniform(kv, w.shape, _jnp.float32, 0.5, 1.5)
    if N_MICROBATCH > 1:
        for name, axis in PER_EXAMPLE_BATCH_AXIS.items():
            out[name] = _to_microbatches(out[name], axis)
    return {'x': out['x'], 'norm_g': out['norm_g'], 'ffn1_w_gate': out['ffn1_w_gate'], 'ffn1_w_up': out['ffn1_w_up'], 'ffn1_w_down': out['ffn1_w_down'], 'ffn2_w_gate': out['ffn2_w_gate'], 'ffn2_w_up': out['ffn2_w_up'], 'ffn2_w_down': out['ffn2_w_down'], 'even_w_in': out['even_w_in'], 'even_b_forget': out['even_b_forget'], 'even_w_out': out['even_w_out'], 'odd_w_qkv': out['odd_w_qkv'], 'odd_w_out': out['odd_w_out'], 'final_norm_g': out['final_norm_g'], 'loss_target': out['loss_target'], 'm_norm_g': out['m_norm_g'], 'm_ffn1_w_gate': out['m_ffn1_w_gate'], 'm_ffn1_w_up': out['m_ffn1_w_up'], 'm_ffn1_w_down': out['m_ffn1_w_down'], 'm_ffn2_w_gate': out['m_ffn2_w_gate'], 'm_ffn2_w_up': out['m_ffn2_w_up'], 'm_ffn2_w_down': out['m_ffn2_w_down'], 'm_even_w_in': out['m_even_w_in'], 'm_even_b_forget': out['m_even_b_forget'], 'm_even_w_out': out['m_even_w_out'], 'm_odd_w_qkv': out['m_odd_w_qkv'], 'm_odd_w_out': out['m_odd_w_out'], 'm_final_norm_g': out['m_final_norm_g'], 'v_norm_g': out['v_norm_g'], 'v_ffn1_w_gate': out['v_ffn1_w_gate'], 'v_ffn1_w_up': out['v_ffn1_w_up'], 'v_ffn1_w_down': out['v_ffn1_w_down'], 'v_ffn2_w_gate': out['v_ffn2_w_gate'], 'v_ffn2_w_up': out['v_ffn2_w_up'], 'v_ffn2_w_down': out['v_ffn2_w_down'], 'v_even_w_in': out['v_even_w_in'], 'v_even_b_forget': out['v_even_b_forget'], 'v_even_w_out': out['v_even_w_out'], 'v_odd_w_qkv': out['v_odd_w_qkv'], 'v_odd_w_out': out['v_odd_w_out'], 'v_final_norm_g': out['v_final_norm_g']}


def _loss(weights, diff, rest, loss_target):
    with _jax.named_scope("forward"):
        args = {**rest, TWIN_DIFF_INPUT: diff, **{k: w.astype(_WEIGHT_DTYPES[k]) for k, w in weights.items()}}
        y = _forward(args)
    with _jax.named_scope("loss_head"):
        err = _jnp.square(y.astype(_jnp.float32) - loss_target)
        return 0.5 * _jnp.sum(_jnp.mean(err, axis=-1)) if err.ndim else 0.5 * err


def _adamw(w, g, m, v):
    m = ADAM_B1 * m + (1.0 - ADAM_B1) * g
    v = ADAM_B2 * v + (1.0 - ADAM_B2) * _jnp.square(g)
    m_hat = m / (1.0 - ADAM_B1 ** ADAM_STEP)
    v_hat = v / (1.0 - ADAM_B2 ** ADAM_STEP)
    delta = -ADAM_LR * (m_hat / (_jnp.sqrt(v_hat) + ADAM_EPS) + ADAM_WD * w)
    return delta, m, v


def reference(x, norm_g, ffn1_w_gate, ffn1_w_up, ffn1_w_down, ffn2_w_gate, ffn2_w_up, ffn2_w_down, even_w_in, even_b_forget, even_w_out, odd_w_qkv, odd_w_out, final_norm_g, loss_target, m_norm_g, m_ffn1_w_gate, m_ffn1_w_up, m_ffn1_w_down, m_ffn2_w_gate, m_ffn2_w_up, m_ffn2_w_down, m_even_w_in, m_even_b_forget, m_even_w_out, m_odd_w_qkv, m_odd_w_out, m_final_norm_g, v_norm_g, v_ffn1_w_gate, v_ffn1_w_up, v_ffn1_w_down, v_ffn2_w_gate, v_ffn2_w_up, v_ffn2_w_down, v_even_w_in, v_even_b_forget, v_even_w_out, v_odd_w_qkv, v_odd_w_out, v_final_norm_g):
    given = dict(x=x, norm_g=norm_g, ffn1_w_gate=ffn1_w_gate, ffn1_w_up=ffn1_w_up, ffn1_w_down=ffn1_w_down, ffn2_w_gate=ffn2_w_gate, ffn2_w_up=ffn2_w_up, ffn2_w_down=ffn2_w_down, even_w_in=even_w_in, even_b_forget=even_b_forget, even_w_out=even_w_out, odd_w_qkv=odd_w_qkv, odd_w_out=odd_w_out, final_norm_g=final_norm_g, loss_target=loss_target, m_norm_g=m_norm_g, m_ffn1_w_gate=m_ffn1_w_gate, m_ffn1_w_up=m_ffn1_w_up, m_ffn1_w_down=m_ffn1_w_down, m_ffn2_w_gate=m_ffn2_w_gate, m_ffn2_w_up=m_ffn2_w_up, m_ffn2_w_down=m_ffn2_w_down, m_even_w_in=m_even_w_in, m_even_b_forget=m_even_b_forget, m_even_w_out=m_even_w_out, m_odd_w_qkv=m_odd_w_qkv, m_odd_w_out=m_odd_w_out, m_final_norm_g=m_final_norm_g, v_norm_g=v_norm_g, v_ffn1_w_gate=v_ffn1_w_gate, v_ffn1_w_up=v_ffn1_w_up, v_ffn1_w_down=v_ffn1_w_down, v_ffn2_w_gate=v_ffn2_w_gate, v_ffn2_w_up=v_ffn2_w_up, v_ffn2_w_down=v_ffn2_w_down, v_even_w_in=v_even_w_in, v_even_b_forget=v_even_b_forget, v_even_w_out=v_even_w_out, v_odd_w_qkv=v_odd_w_qkv, v_odd_w_out=v_odd_w_out, v_final_norm_g=v_final_norm_g)
    weights = {n: given[n] for n in TWIN_WEIGHTS}
    shared = {n: given[n] for n in SHARED_INPUTS}
    per_example = {n: given[n] for n in ['x']}
    grad_fn = _jax.value_and_grad(_loss, argnums=(0, 1))

    def one_microbatch(ex, loss_target):
        ex = dict(ex)
        diff = ex.pop(TWIN_DIFF_INPUT)
        return grad_fn(weights, diff, {**shared, **ex}, loss_target)

    if N_MICROBATCH == 1:
        loss, (grad_w, grad_x) = one_microbatch(per_example, given["loss_target"])
    else:
        def body(carry, xs):
            loss_sum, grad_sum = carry
            l_k, (gw_k, gx_k) = one_microbatch(xs[0], xs[1])
            with _jax.named_scope("update"):
                return (loss_sum + l_k, _jax.tree.map(_jnp.add, grad_sum, gw_k)), gx_k

        init = (_jnp.zeros((), _jnp.float32), _jax.tree.map(_jnp.zeros_like, weights))
        (loss, grad_w), grad_x = _jax.lax.scan(body, init, (per_example, given["loss_target"]))
    with _jax.named_scope("update"):
        delta_w, new_m, new_v = {}, {}, {}
        for n in TWIN_WEIGHTS:
            delta_w[n], new_m[n], new_v[n] = _adamw(weights[n], grad_w[n], given["m_" + n], given["v_" + n])
    return (loss, grad_x, *[grad_w[n] for n in TWIN_WEIGHTS], *[delta_w[n] for n in TWIN_WEIGHTS],
            *[new_m[n] for n in TWIN_WEIGHTS], *[new_v[n] for n in TWIN_WEIGHTS])
```

```python
import functools
import math

import jax
import jax.numpy as jnp
from jax import lax
from jax.experimental import pallas as pl
from jax.experimental.pallas import tpu as pltpu

F32 = jnp.float32
BF = jnp.bfloat16
NDEV = 8
HD = 128
QB = 128
N_HEADS = 16
N_SB = 8
SCALE = HD ** -0.5
ROPE_THETA = 500000.0
ROPE_DIMS = HD // 4
DILATED_PATTERNS = ((128, 1), (512, 4), (2048, 16))
RMS_EPS = 1e-6
NEG_INF = -1e30
ADAM_LR = 0.001
ADAM_B1 = 0.9
ADAM_B2 = 0.999
ADAM_EPS = 1e-08
ADAM_WD = 0.01
ADAM_STEP = 10
VMEM_LIMIT_V7X = 56 * 1024 * 1024
MESH = pl.DeviceIdType.MESH
ANY = pl.BlockSpec(memory_space=pl.ANY)

NT_DIMS = (((1,), (1,)), ((), ()))


def _cp(*dims):
    return pltpu.CompilerParams(dimension_semantics=dims if dims else None, vmem_limit_bytes=VMEM_LIMIT_V7X)


def _dot(a, b):
    return jnp.dot(a, b, preferred_element_type=F32)


def _dot_nt(a, b):
    return lax.dot_general(a, b, NT_DIMS, preferred_element_type=F32)


def _sds(shape, dtype):
    return jax.ShapeDtypeStruct(shape, dtype)


def _place():
    x, y, c = lax.axis_index("x"), lax.axis_index("y"), lax.axis_index("c")
    chips = [(x, y), (1 - x, y), (x, 1 - y), (1 - x, 1 - y)]
    return x, y, c, chips


def _all_gather(xs, name):
    n = len(xs)

    def body(*refs):
        x_refs, out_refs = refs[:n], refs[n:2 * n]
        send_sems, recv_sems, local_sems = refs[2 * n:]
        x, y, c, chips = _place()
        me, sibling = (x, y, c), (x, y, 1 - c)
        others = chips[1:]

        def slot(a, px, py, pc):
            return out_refs[a].at[4 * px + 2 * py + pc]

        def copy(a, k, block, to, src=None):
            return pltpu.make_async_remote_copy(
                src_ref=slot(a, *block) if src is None else src, dst_ref=slot(a, *block),
                send_sem=send_sems.at[a, k], recv_sem=recv_sems.at[a, k], device_id=to, device_id_type=MESH)

        started = []
        for a in range(n):
            mine = pltpu.make_async_copy(x_refs[a], slot(a, *me), local_sems.at[a])
            mine.start()
            first = [copy(a, 0, me, sibling, src=x_refs[a])]
            first += [copy(a, 1 + j, me, (*chip, c), src=x_refs[a]) for j, chip in enumerate(others)]
            for cp in first:
                cp.start()
            started += [mine.wait] + [cp.wait_send for cp in first]
        for a in range(n):
            for j, chip in enumerate(others):
                copy(a, 1 + j, (*chip, c), me).wait_recv()
                passed = copy(a, 4 + j, (*chip, c), sibling)
                passed.start()
                started.append(passed.wait_send)
        for a in range(n):
            copy(a, 0, sibling, me).wait_recv()
            for j, chip in enumerate(others):
                copy(a, 4 + j, (*chip, 1 - c), me).wait_recv()
        for w in started:
            w()

    return pl.pallas_call(
        body, name=name,
        out_shape=[_sds((NDEV,) + x.shape, x.dtype) for x in xs],
        in_specs=[ANY] * n, out_specs=[ANY] * n,
        scratch_shapes=[pltpu.SemaphoreType.DMA((n, 7)), pltpu.SemaphoreType.DMA((n, 7)), pltpu.SemaphoreType.DMA((n,))],
    )(*xs)


def _pair_exchange(gs, name):
    n = len(gs)

    def body(*refs):
        g_refs, a_refs = refs[:n], refs[n:2 * n]
        send_sems, recv_sems = refs[2 * n:]
        x, y, c, chips = _place()
        copies = []
        for a in range(n):
            for j, (px, py) in enumerate(chips):
                copies.append(pltpu.make_async_remote_copy(
                    src_ref=g_refs[a].at[4 * px + 2 * py + (1 - c)], dst_ref=a_refs[a].at[j],
                    send_sem=send_sems.at[a, j], recv_sem=recv_sems.at[a, j],
                    device_id=(x, y, 1 - c), device_id_type=MESH))
        for cp in copies:
            cp.start()
        for cp in copies:
            cp.wait()

    return pl.pallas_call(
        body, name=name,
        out_shape=[_sds((4,) + g.shape[1:], g.dtype) for g in gs],
        in_specs=[ANY] * n, out_specs=[ANY] * n,
        scratch_shapes=[pltpu.SemaphoreType.DMA((n, 4)), pltpu.SemaphoreType.DMA((n, 4))],
    )(*gs)


def _chip_exchange(ps, name):
    n = len(ps)

    def body(*refs):
        p_refs, b_refs = refs[:n], refs[n:2 * n]
        send_sems, recv_sems = refs[2 * n:]
        x, y, c, chips = _place()
        copies = []
        for a in range(n):
            for j, chip in enumerate(chips[1:]):
                copies.append(pltpu.make_async_remote_copy(
                    src_ref=p_refs[a].at[j], dst_ref=b_refs[a].at[j],
                    send_sem=send_sems.at[a, j], recv_sem=recv_sems.at[a, j],
                    device_id=(*chip, c), device_id_type=MESH))
        for cp in copies:
            cp.start()
        for cp in copies:
            cp.wait()

    return pl.pallas_call(
        body, name=name,
        out_shape=[_sds(p.shape, p.dtype) for p in ps],
        in_specs=[ANY] * n, out_specs=[ANY] * n,
        scratch_shapes=[pltpu.SemaphoreType.DMA((n, 3)), pltpu.SemaphoreType.DMA((n, 3))],
    )(*ps)


def _rows_tile(r):
    for t in (512, 256, 128, 64, 32, 16):
        if r % t == 0:
            return t
    return r


def _pair_sum(g, a, slots, name):
    _, r, c = g.shape
    tr = _rows_tile(r)

    def body(slots_ref, g_ref, a_ref, p_ref):
        p_ref[...] = (g_ref[...].astype(F32) + a_ref[...].astype(F32)).astype(BF)

    return pl.pallas_call(
        body, name=name,
        grid_spec=pltpu.PrefetchScalarGridSpec(
            num_scalar_prefetch=1, grid=(3, r // tr),
            in_specs=[pl.BlockSpec((None, tr, c), lambda j, i, s: (s[j + 1], i, 0)),
                      pl.BlockSpec((None, tr, c), lambda j, i, s: (j + 1, i, 0))],
            out_specs=pl.BlockSpec((None, tr, c), lambda j, i, s: (j, i, 0))),
        out_shape=_sds((3, r, c), BF), compiler_params=_cp("parallel", "parallel"),
    )(slots, g, a)


def _adamw_math(w, g, m, v):
    m = ADAM_B1 * m + (1.0 - ADAM_B1) * g
    v = ADAM_B2 * v + (1.0 - ADAM_B2) * (g * g)
    m_hat = m / (1.0 - ADAM_B1 ** ADAM_STEP)
    v_hat = v / (1.0 - ADAM_B2 ** ADAM_STEP)
    delta = -ADAM_LR * (m_hat / (jnp.sqrt(v_hat) + ADAM_EPS) + ADAM_WD * w)
    return delta, m, v


def _adamw_sharded(w, m, v, parts, slots, name):
    nl, r, c = w.shape
    tr = _rows_tile(r)
    if c * tr * 4 > (1 << 21):
        tr = max(8, tr // 2)

    def body(slots_ref, w_ref, m_ref, v_ref, *rest):
        part_refs, (g_out, d_out, m_out, v_out) = rest[:5 * nl], rest[5 * nl:]
        layer = pl.program_id(0)
        g = None
        for l in range(nl):
            s = part_refs[5 * l][...].astype(F32)
            for ref in part_refs[5 * l + 1:5 * l + 5]:
                s = s + ref[...].astype(F32)
            g = s if g is None else jnp.where(layer == l, s, g)
        delta, mn, vn = _adamw_math(w_ref[...], g, m_ref[...], v_ref[...])
        g_out[...] = g
        d_out[...] = delta
        m_out[...] = mn
        v_out[...] = vn

    def own(l):
        return lambda L, i, s: (s[0], jnp.where(L == l, i, 0), 0)

    def fixed(l, k):
        return lambda L, i, s: (k, jnp.where(L == l, i, 0), 0)

    wspec = pl.BlockSpec((None, tr, c), lambda L, i, s: (L, i, 0))
    in_specs = [wspec, wspec, wspec]
    args = [w, m, v]
    for l, (g, a, b) in enumerate(parts):
        in_specs += [pl.BlockSpec((None, tr, c), own(l)), pl.BlockSpec((None, tr, c), fixed(l, 0)),
                     pl.BlockSpec((None, tr, c), fixed(l, 0)), pl.BlockSpec((None, tr, c), fixed(l, 1)),
                     pl.BlockSpec((None, tr, c), fixed(l, 2))]
        args += [g, a, b, b, b]
    return pl.pallas_call(
        body, name=name,
        grid_spec=pltpu.PrefetchScalarGridSpec(
            num_scalar_prefetch=1, grid=(nl, r // tr), in_specs=in_specs, out_specs=[wspec] * 4),
        out_shape=[_sds(w.shape, F32)] * 4, compiler_params=_cp("arbitrary", "arbitrary"),
    )(slots, *args)


def _adamw_small(w, m, v, gathered, name):
    def body(w_ref, m_ref, v_ref, gg_ref, g_out, d_out, m_out, v_out):
        g = gg_ref[0]
        for k in range(1, NDEV):
            g = g + gg_ref[k]
        delta, mn, vn = _adamw_math(w_ref[...], g, m_ref[...], v_ref[...])
        g_out[...] = g
        d_out[...] = delta
        m_out[...] = mn
        v_out[...] = vn

    return pl.pallas_call(body, name=name, out_shape=[_sds(w.shape, F32)] * 4)(w, m, v, gathered)


def _rmsnorm_fwd(x, g, name):
    s, d = x.shape
    tm = 256

    def body(x_ref, g_ref, h_ref, ht_ref):
        xf = x_ref[...]
        y = xf * lax.rsqrt(jnp.mean(xf * xf, axis=-1, keepdims=True) + RMS_EPS)
        h = y * g_ref[...]
        h_ref[...] = h.astype(BF)
        ht_ref[...] = h.T.astype(BF)

    return pl.pallas_call(
        body, name=name, grid=(s // tm,),
        in_specs=[pl.BlockSpec((tm, d), lambda i: (i, 0)), pl.BlockSpec((1, d), lambda i: (0, 0))],
        out_specs=[pl.BlockSpec((tm, d), lambda i: (i, 0)), pl.BlockSpec((d, tm), lambda i: (0, i))],
        out_shape=[_sds((s, d), BF), _sds((d, s), BF)], compiler_params=_cp("parallel"),
    )(x, g)


def _rmsnorm_bwd(x, g, dh, dres, out_scale, name):
    s, d = x.shape
    tm = 256

    def body(x_ref, g_ref, dh_ref, dres_ref, dx_ref, dxb_ref, dxbt_ref, dg_ref):
        xf = x_ref[...]
        r = lax.rsqrt(jnp.mean(xf * xf, axis=-1, keepdims=True) + RMS_EPS)
        xhat = xf * r
        dhv = dh_ref[...]
        dxhat = dhv * g_ref[...]
        dx = dres_ref[...] + r * (dxhat - xhat * jnp.mean(dxhat * xhat, axis=-1, keepdims=True))
        dx_ref[...] = dx
        scaled = dx * out_scale
        dxb_ref[...] = scaled.astype(BF)
        dxbt_ref[...] = scaled.T.astype(BF)

        @pl.when(pl.program_id(0) == 0)
        def _():
            dg_ref[...] = jnp.zeros_like(dg_ref)

        dg_ref[...] += jnp.sum(dhv * xhat, axis=0, keepdims=True)

    row = pl.BlockSpec((tm, d), lambda i: (i, 0))
    vec = pl.BlockSpec((1, d), lambda i: (0, 0))
    return pl.pallas_call(
        body, name=name, grid=(s // tm,),
        in_specs=[row, vec, row, row],
        out_specs=[row, row, pl.BlockSpec((d, tm), lambda i: (0, i)), vec],
        out_shape=[_sds((s, d), F32), _sds((s, d), BF), _sds((d, s), BF), _sds((1, d), F32)],
        compiler_params=_cp("arbitrary"),
    )(x, g, dh, dres)


def _loss_head(x, g, target, name):
    s, d = x.shape
    tm = 256

    def body(x_ref, g_ref, t_ref, dx_ref, dxb_ref, dxbt_ref, dg_ref, loss_ref):
        xf = x_ref[...]
        r = lax.rsqrt(jnp.mean(xf * xf, axis=-1, keepdims=True) + RMS_EPS)
        xhat = xf * r
        err = xhat * g_ref[...] - t_ref[...]
        dy = err * (1.0 / d)
        dxhat = dy * g_ref[...]
        dx = r * (dxhat - xhat * jnp.mean(dxhat * xhat, axis=-1, keepdims=True))
        dx_ref[...] = dx
        half = dx * 0.5
        dxb_ref[...] = half.astype(BF)
        dxbt_ref[...] = half.T.astype(BF)

        @pl.when(pl.program_id(0) == 0)
        def _():
            dg_ref[...] = jnp.zeros_like(dg_ref)
            loss_ref[...] = jnp.zeros_like(loss_ref)

        dg_ref[...] += jnp.sum(dy * xhat, axis=0, keepdims=True)
        part = 0.5 * jnp.sum(jnp.mean(err * err, axis=-1, keepdims=True), axis=0, keepdims=True)
        lane = lax.broadcasted_iota(jnp.int32, (1, 128), 1)
        loss_ref[...] += jnp.where(lane == 0, part, 0.0)

    row = pl.BlockSpec((tm, d), lambda i: (i, 0))
    vec = pl.BlockSpec((1, d), lambda i: (0, 0))
    return pl.pallas_call(
        body, name=name, grid=(s // tm,),
        in_specs=[row, vec, row],
        out_specs=[row, row, pl.BlockSpec((d, tm), lambda i: (0, i)), vec, pl.BlockSpec((1, 128), lambda i: (0, 0))],
        out_shape=[_sds((s, d), F32), _sds((s, d), BF), _sds((d, s), BF), _sds((1, d), F32), _sds((1, 128), F32)],
        compiler_params=_cp("arbitrary"),
    )(x, g, target)


def _act_spec(tm, n, natural, order):
    if natural:
        return pl.BlockSpec((tm, n), (lambda s, i: (i, s)) if order == "si" else (lambda i, s: (i, s)))
    return pl.BlockSpec((None, tm, n), (lambda s, i: (s, i, 0)) if order == "si" else (lambda i, s: (s, i, 0)))


def _act_shape(s, n, natural, dtype):
    return _sds((s, NDEV * n), dtype) if natural else _sds((NDEV, s, n), dtype)


def _ffn_up(h, wg, wu, name):
    s, d = h.shape
    n = wg.shape[2]
    tm = 512

    def body(h_ref, wg_ref, wu_ref, g_ref, u_ref, a_ref):
        hb = h_ref[...]
        g = _dot(hb, wg_ref[...])
        u = _dot(hb, wu_ref[...])
        g_ref[...] = g.astype(BF)
        u_ref[...] = u.astype(BF)
        a_ref[...] = (g * jax.nn.sigmoid(g) * u).astype(BF)

    wsp = pl.BlockSpec((None, d, n), lambda s_, i: (s_, 0, 0))
    blk = _act_spec(tm, n, False, "si")
    return pl.pallas_call(
        body, name=name, grid=(NDEV, s // tm),
        in_specs=[pl.BlockSpec((tm, d), lambda s_, i: (i, 0)), wsp, wsp],
        out_specs=[blk] * 3, out_shape=[_act_shape(s, n, False, BF)] * 3,
        compiler_params=_cp("parallel", "parallel"),
    )(h, wg, wu)


def _ffn_down(act, wd, x, name):
    _, s, n = act.shape
    d = wd.shape[2]
    tm = 512

    def body(a_ref, w_ref, x_ref, o_ref, acc):
        k = pl.program_id(1)

        @pl.when(k == 0)
        def _():
            acc[...] = jnp.zeros_like(acc)

        acc[...] += _dot(a_ref[...], w_ref[...])

        @pl.when(k == NDEV - 1)
        def _():
            o_ref[...] = x_ref[...] + 0.5 * acc[...]

    row = pl.BlockSpec((tm, d), lambda i, k: (i, 0))
    return pl.pallas_call(
        body, name=name, grid=(s // tm, NDEV),
        in_specs=[_act_spec(tm, n, False, "is"), pl.BlockSpec((None, n, d), lambda i, k: (k, 0, 0)), row],
        out_specs=row, out_shape=_sds((s, d), F32),
        scratch_shapes=[pltpu.VMEM((tm, d), F32)], compiler_params=_cp("parallel", "arbitrary"),
    )(act, wd, x)


def _ffn_bwd_act(dyb, wd, g, u, name):
    s, d = dyb.shape
    n = wd.shape[1]
    tm = 512

    def body(dy_ref, w_ref, g_ref, u_ref, dg_ref, du_ref):
        dact = _dot_nt(dy_ref[...], w_ref[...])
        gv = g_ref[...].astype(F32)
        uv = u_ref[...].astype(F32)
        sig = jax.nn.sigmoid(gv)
        dg_ref[...] = (dact * uv * (sig * (1.0 + gv * (1.0 - sig)))).astype(BF)
        du_ref[...] = (dact * (gv * sig)).astype(BF)

    blk = _act_spec(tm, n, False, "si")
    return pl.pallas_call(
        body, name=name, grid=(NDEV, s // tm),
        in_specs=[pl.BlockSpec((tm, d), lambda s_, i: (i, 0)), pl.BlockSpec((None, n, d), lambda s_, i: (s_, 0, 0)), blk, blk],
        out_specs=[blk, blk], out_shape=[_act_shape(s, n, False, BF)] * 2,
        compiler_params=_cp("parallel", "parallel"),
    )(dyb, wd, g, u)


def _grad_rows(act_t, dyb, name):
    _, n, s = act_t.shape
    d = dyb.shape[1]
    tn = 1024

    def body(a_ref, dy_ref, o_ref):
        o_ref[...] = _dot(a_ref[...], dy_ref[...]).astype(BF)

    return pl.pallas_call(
        body, name=name, grid=(NDEV, d // tn),
        in_specs=[pl.BlockSpec((None, n, s), lambda k, j: (k, 0, 0)), pl.BlockSpec((s, tn), lambda k, j: (0, j))],
        out_specs=pl.BlockSpec((None, n, tn), lambda k, j: (k, 0, j)), out_shape=_sds((NDEV, n, d), BF),
        compiler_params=_cp("parallel", "parallel"),
    )(act_t, dyb)


def _transpose_blocked(act, name):
    _, s, n = act.shape
    tm = 512

    def body(a_ref, o_ref):
        o_ref[...] = a_ref[...].astype(F32).T.astype(BF)

    return pl.pallas_call(
        body, name=name, grid=(NDEV, s // tm),
        in_specs=[pl.BlockSpec((None, tm, n), lambda k, i: (k, i, 0))],
        out_specs=pl.BlockSpec((None, n, tm), lambda k, i: (k, 0, i)), out_shape=_sds((NDEV, n, s), BF),
        compiler_params=_cp("parallel", "parallel"),
    )(act)


def _grad_cols(ht, dxs, naturals, name):
    d, s = ht.shape
    k = len(dxs)
    ns = [dx.shape[1] // NDEV if nat else dx.shape[2] for dx, nat in zip(dxs, naturals)]
    td = 512

    def body(*refs):
        ht_ref, dx_refs, o_refs = refs[0], refs[1:1 + k], refs[1 + k:]
        hv = ht_ref[...]
        for dx_ref, o_ref in zip(dx_refs, o_refs):
            o_ref[...] = _dot(hv, dx_ref[...]).astype(BF)

    def dx_spec(n, nat):
        if nat:
            return pl.BlockSpec((s, n), lambda s_, j: (0, s_))
        return pl.BlockSpec((None, s, n), lambda s_, j: (s_, 0, 0))

    return pl.pallas_call(
        body, name=name, grid=(NDEV, d // td),
        in_specs=[pl.BlockSpec((td, s), lambda s_, j: (j, 0))] + [dx_spec(n, nat) for n, nat in zip(ns, naturals)],
        out_specs=[pl.BlockSpec((None, td, n), lambda s_, j: (s_, j, 0)) for n in ns],
        out_shape=[_sds((NDEV, d, n), BF) for n in ns], compiler_params=_cp("parallel", "parallel"),
    )(ht, *dxs)


def _dh_cols(dxs, ws, naturals, name):
    k = len(dxs)
    d = ws[0].shape[1]
    ns = [w.shape[2] for w in ws]
    s = dxs[0].shape[0] if naturals[0] else dxs[0].shape[1]
    tm = 1024

    def body(*refs):
        dx_refs, w_refs, o_ref, acc = refs[:k], refs[k:2 * k], refs[2 * k], refs[2 * k + 1]
        j = pl.program_id(1)

        @pl.when(j == 0)
        def _():
            acc[...] = jnp.zeros_like(acc)

        t = _dot_nt(dx_refs[0][...], w_refs[0][...])
        for dx_ref, w_ref in zip(dx_refs[1:], w_refs[1:]):
            t = t + _dot_nt(dx_ref[...], w_ref[...])
        acc[...] += t

        @pl.when(j == NDEV - 1)
        def _():
            o_ref[...] = acc[...]

    return pl.pallas_call(
        body, name=name, grid=(s // tm, NDEV),
        in_specs=[_act_spec(tm, n, nat, "is") for n, nat in zip(ns, naturals)]
        + [pl.BlockSpec((None, d, n), lambda i, j: (j, 0, 0)) for n in ns],
        out_specs=pl.BlockSpec((tm, d), lambda i, j: (i, 0)), out_shape=_sds((s, d), F32),
        scratch_shapes=[pltpu.VMEM((tm, d), F32)], compiler_params=_cp("parallel", "arbitrary"),
    )(*dxs, *ws)


def _mm_nn(a, b, tn, out_dtype, name, res=None, tm=512):
    m, k = a.shape
    nn = b.shape[1]

    def body(*refs):
        if res is None:
            a_ref, b_ref, o_ref = refs
            o_ref[...] = _dot(a_ref[...], b_ref[...]).astype(out_dtype)
        else:
            a_ref, b_ref, r_ref, o_ref = refs
            o_ref[...] = (r_ref[...] + _dot(a_ref[...], b_ref[...])).astype(out_dtype)

    osp = pl.BlockSpec((tm, tn), lambda j, i: (i, j))
    in_specs = [pl.BlockSpec((tm, k), lambda j, i: (i, 0)), pl.BlockSpec((k, tn), lambda j, i: (0, j))]
    args = [a, b]
    if res is not None:
        in_specs.append(osp)
        args.append(res)
    return pl.pallas_call(
        body, name=name, grid=(nn // tn, m // tm), in_specs=in_specs, out_specs=osp,
        out_shape=_sds((m, nn), out_dtype), compiler_params=_cp("parallel", "parallel"),
    )(*args)


def _mm_nt(pairs, name, tm=512, tk=512):
    m = pairs[0][0].shape[0]
    kk = pairs[0][1].shape[0]
    p = len(pairs)

    def body(*refs):
        o_ref = refs[2 * p]
        t = _dot_nt(refs[0][...], refs[1][...])
        for q in range(1, p):
            t = t + _dot_nt(refs[2 * q][...], refs[2 * q + 1][...])
        o_ref[...] = t

    in_specs, args = [], []
    for a, b in pairs:
        in_specs += [pl.BlockSpec((tm, a.shape[1]), lambda j, i: (i, 0)), pl.BlockSpec((tk, b.shape[1]), lambda j, i: (j, 0))]
        args += [a, b]
    return pl.pallas_call(
        body, name=name, grid=(kk // tk, m // tm), in_specs=in_specs,
        out_specs=pl.BlockSpec((tm, tk), lambda j, i: (i, j)), out_shape=_sds((m, kk), F32),
        compiler_params=_cp("parallel", "parallel"),
    )(*args)


def _rope_tables(s, sign):
    half = ROPE_DIMS // 2
    freqs = ROPE_THETA ** (-jnp.arange(half, dtype=F32) / half)
    ang = jnp.arange(s, dtype=F32)[:, None] * freqs[None, :]
    cos, sin = jnp.cos(ang), sign * jnp.sin(ang)
    one = jnp.ones((s, HD - ROPE_DIMS), F32)
    zero = jnp.zeros((s, HD - ROPE_DIMS), F32)
    zh = jnp.zeros((s, half), F32)
    c = jnp.concatenate([cos, cos, one], axis=1)
    sa = jnp.concatenate([-sin, zh, zero], axis=1)
    sb = jnp.concatenate([zh, sin, zero], axis=1)
    return c, sa, sb


def _rope(xv, c, sa, sb):
    return xv * c + pltpu.roll(xv, HD - ROPE_DIMS // 2, 1) * sa + pltpu.roll(xv, ROPE_DIMS // 2, 1) * sb


def _qkv_rope(h, w, tables, name):
    s, d = h.shape
    n = w.shape[2]
    per = n // HD
    tm = 512

    def body(h_ref, w_ref, c_ref, sa_ref, sb_ref, o_ref):
        shard = pl.program_id(0)
        y = _dot(h_ref[...], w_ref[...])
        c, sa, sb = c_ref[...], sa_ref[...], sb_ref[...]
        for j in range(per):
            blk = y[:, j * HD:(j + 1) * HD]
            rot = _rope(blk, c, sa, sb)
            is_qk = shard * per + j < 2 * N_HEADS
            o_ref[:, j * HD:(j + 1) * HD] = jnp.where(is_qk, rot, blk).astype(BF)

    tab = pl.BlockSpec((tm, HD), lambda s_, i: (i, 0))
    return pl.pallas_call(
        body, name=name, grid=(NDEV, s // tm),
        in_specs=[pl.BlockSpec((tm, d), lambda s_, i: (i, 0)), pl.BlockSpec((None, d, n), lambda s_, i: (s_, 0, 0)), tab, tab, tab],
        out_specs=pl.BlockSpec((tm, n), lambda s_, i: (i, s_)), out_shape=_sds((s, NDEV * n), BF),
        compiler_params=_cp("parallel", "parallel"),
    )(h, w, *tables)


def _iota2():
    return (lax.broadcasted_iota(jnp.int32, (QB, QB), 0), lax.broadcasted_iota(jnp.int32, (QB, QB), 1))


def _softplus(z):
    return jnp.maximum(z, 0.0) + jnp.log(1.0 + jnp.exp(-jnp.abs(z)))


def _tri_dot(xv, tri, left=False):
    hi = xv.astype(BF)
    r1 = xv - hi.astype(F32)
    mid = r1.astype(BF)
    lo = (r1 - mid.astype(F32)).astype(BF)
    if left:
        return _dot(tri, hi) + _dot(tri, mid) + _dot(tri, lo)
    return _dot(hi, tri) + _dot(mid, tri) + _dot(lo, tri)


def _col(ref_or_val):
    return ref_or_val[:, 0:1]


def _sb_fwd(qkv, name):
    s = qkv.shape[0]
    nb = s // QB

    def body(q_ref, k_ref, v_ref, o_ref, ot_ref, t_ref):
        i = pl.program_id(1)
        q = q_ref[...]
        row, col = _iota2()
        after_tri = (row > col).astype(BF)

        def step(jj, carry):
            acc, later = carry
            j = i - jj
            off = pl.multiple_of(j * QB, QB)
            k = k_ref[pl.ds(off, QB), :]
            v = v_ref[pl.ds(off, QB), :]
            z = _dot_nt(q, k) * SCALE
            strict = row + (i - j) * QB > col
            sp = _softplus(z)
            lnb = jnp.where(strict, -sp, 0.0)
            after = later + _tri_dot(lnb, after_tri)
            w = jnp.where(strict, jnp.exp((z - sp) + after), 0.0)
            acc = acc + _dot(w.astype(BF), v)
            return acc, later + jnp.sum(lnb, axis=1, keepdims=True)

        acc, total = lax.fori_loop(0, i + 1, step, (jnp.zeros((QB, HD), F32), jnp.zeros((QB, 1), F32)))
        o_ref[...] = acc.astype(BF)
        ot_ref[...] = acc.T.astype(BF)
        t_ref[...] = jnp.broadcast_to(total, (QB, HD))

    blk = pl.BlockSpec((QB, HD), lambda h, i: (i, h))
    return pl.pallas_call(
        body, name=name, grid=(N_SB, nb),
        in_specs=[blk, pl.BlockSpec((s, HD), lambda h, i: (0, N_HEADS + h)), pl.BlockSpec((s, HD), lambda h, i: (0, 2 * N_HEADS + h))],
        out_specs=[blk, pl.BlockSpec((HD, QB), lambda h, i: (h, i)), blk],
        out_shape=[_sds((s, N_SB * HD), BF), _sds((N_SB * HD, s), BF), _sds((s, N_SB * HD), F32)],
        compiler_params=_cp("parallel", "parallel"),
    )(qkv, qkv, qkv)


def _sb_bwd(qkv, do, total, name):
    s = qkv.shape[0]
    nb = s // QB

    def body(q_ref, k_ref, v_ref, do_ref, t_ref, dq_ref, dk_ref, dv_ref, dk_acc, dv_acc):
        i = pl.program_id(1)

        @pl.when(i == 0)
        def _():
            dk_acc[...] = jnp.zeros_like(dk_acc)
            dv_acc[...] = jnp.zeros_like(dv_acc)

        q = q_ref[...]
        dov = do_ref[...]
        tot = _col(t_ref[...])
        row, col = _iota2()
        upto_tri = (row <= col).astype(BF)
        before_tri = (row < col).astype(BF)

        def step(j, carry):
            dq, lnb_before, dl_before = carry
            off = pl.multiple_of(j * QB, QB)
            k = k_ref[pl.ds(off, QB), :]
            v = v_ref[pl.ds(off, QB), :]
            z = _dot_nt(q, k) * SCALE
            strict = row + (i - j) * QB > col
            sp = _softplus(z)
            lnb = jnp.where(strict, -sp, 0.0)
            after = tot - (lnb_before + _tri_dot(lnb, upto_tri))
            a = jnp.where(strict, jnp.exp((z - sp) + after), 0.0)
            dl = a * _dot_nt(dov, v)
            before = dl_before + _tri_dot(dl, before_tri)
            sig = jnp.exp(z - sp)
            dz = jnp.where(strict, dl * (1.0 - sig) - sig * before, 0.0) * SCALE
            dq = dq + _dot(dz.astype(BF), k)
            dk_acc[pl.ds(off, QB), :] += _dot(dz.T.astype(BF), q)
            dv_acc[pl.ds(off, QB), :] += _dot(a.T.astype(BF), dov)
            return dq, lnb_before + jnp.sum(lnb, axis=1, keepdims=True), dl_before + jnp.sum(dl, axis=1, keepdims=True)

        zero = jnp.zeros((QB, 1), F32)
        dq, _, _ = lax.fori_loop(0, i + 1, step, (jnp.zeros((QB, HD), F32), zero, zero))
        dq_ref[...] = dq.astype(BF)

        @pl.when(i == nb - 1)
        def _():
            dk_ref[...] = dk_acc[...].astype(BF)
            dv_ref[...] = dv_acc[...].astype(BF)

    blk = pl.BlockSpec((QB, HD), lambda h, i: (i, h))
    full = pl.BlockSpec((s, HD), lambda h, i: (0, h))
    return pl.pallas_call(
        body, name=name, grid=(N_SB, nb),
        in_specs=[blk, pl.BlockSpec((s, HD), lambda h, i: (0, N_HEADS + h)), pl.BlockSpec((s, HD), lambda h, i: (0, 2 * N_HEADS + h)), blk, blk],
        out_specs=[blk, full, full], out_shape=[_sds((s, N_SB * HD), BF)] * 3,
        scratch_shapes=[pltpu.VMEM((s, HD), F32), pltpu.VMEM((s, HD), F32)],
        compiler_params=_cp("parallel", "arbitrary"),
    )(qkv, qkv, qkv, do, total)


def _fgate_fwd(f, b, name):
    s = f.shape[0]
    nb = s // QB
    nfox = N_HEADS - N_SB

    def body(f_ref, b_ref, cb_ref, ct_ref):
        row, col = _iota2()
        upto = (row >= col).astype(BF)
        carry = jnp.zeros((1, HD), F32)
        for blk in range(nb):
            xv = f_ref[blk * QB:(blk + 1) * QB, :] + b_ref[...]
            logf = -_softplus(-xv)
            cum = _tri_dot(logf, upto, left=True) + carry
            carry = cum[QB - 1:QB, :]
            ct_ref[blk] = cum.T
            for h in range(nfox):
                cb_ref[blk * QB:(blk + 1) * QB, h * HD:(h + 1) * HD] = jnp.broadcast_to(cum[:, h:h + 1], (QB, HD))

    return pl.pallas_call(
        body, name=name, out_shape=[_sds((s, nfox * HD), F32), _sds((nb, HD, HD), F32)], compiler_params=_cp(),
    )(f, b)


def _fgate_bwd(dcq, dck, f, b, name):
    s = f.shape[0]
    nb = s // QB
    nfox = N_HEADS - N_SB

    def body(dcq_ref, dck_ref, f_ref, b_ref, df_ref, db_ref):
        row, col = _iota2()
        from_tri = (row <= col).astype(BF)
        lane = col
        carry = jnp.zeros((1, HD), F32)
        db = jnp.zeros((1, HD), F32)
        for blk in reversed(range(nb)):
            dcum = jnp.zeros((QB, HD), F32)
            for h in range(nfox):
                here = (slice(blk * QB, (blk + 1) * QB), slice(h * HD, (h + 1) * HD))
                dcum = jnp.where(lane == h, dcq_ref[here] - dck_ref[here], dcum)
            dlogf = _tri_dot(dcum, from_tri, left=True) + carry
            carry = dlogf[0:1, :]
            xv = f_ref[blk * QB:(blk + 1) * QB, :] + b_ref[...]
            sp = _softplus(xv)
            df = jnp.where(lane < nfox, dlogf * jnp.exp(-sp), 0.0)
            df_ref[blk * QB:(blk + 1) * QB, :] = df.astype(BF)
            db = db + jnp.sum(df, axis=0, keepdims=True)
        db_ref[...] = db

    return pl.pallas_call(
        body, name=name, out_shape=[_sds((s, HD), BF), _sds((1, HD), F32)], compiler_params=_cp(),
    )(dcq, dck, f, b)


def _fox_head_row(ct_ref, j, h):
    tile = ct_ref[j]
    sub = lax.broadcasted_iota(jnp.int32, tile.shape, 0)
    return jnp.sum(jnp.where(sub == h, tile, 0.0), axis=0, keepdims=True)


def _fox_fwd(qkv, cum_b, cum_t, name):
    s = qkv.shape[0]
    nb = s // QB
    nfox = N_HEADS - N_SB

    def body(q_ref, k_ref, v_ref, cq_ref, ct_ref, o_ref, ot_ref, lse_ref):
        h, i = pl.program_id(0), pl.program_id(1)
        q = q_ref[...]
        cq = cq_ref[...]
        row, col = _iota2()

        def step(j, carry):
            acc, m, l = carry
            off = pl.multiple_of(j * QB, QB)
            k = k_ref[pl.ds(off, QB), :]
            v = v_ref[pl.ds(off, QB), :]
            z = _dot_nt(q, k) * SCALE + cq - _fox_head_row(ct_ref, j, h)
            z = jnp.where(row + (i - j) * QB >= col, z, NEG_INF)
            m_new = jnp.maximum(m, jnp.max(z, axis=1, keepdims=True))
            alpha = jnp.exp(m - m_new)
            p = jnp.exp(z - m_new)
            l = alpha * l + jnp.sum(p, axis=1, keepdims=True)
            acc = alpha * acc + _dot(p.astype(BF), v)
            return acc, m_new, l

        acc, m, l = lax.fori_loop(0, i + 1, step, (jnp.zeros((QB, HD), F32), jnp.full((QB, 1), NEG_INF, F32), jnp.zeros((QB, 1), F32)))
        o = acc / l
        o_ref[...] = o.astype(BF)
        ot_ref[...] = o.T.astype(BF)
        lse_ref[...] = jnp.broadcast_to(m + jnp.log(l), (QB, HD))

    blk = pl.BlockSpec((QB, HD), lambda h, i: (i, h))
    return pl.pallas_call(
        body, name=name, grid=(nfox, nb),
        in_specs=[pl.BlockSpec((QB, HD), lambda h, i: (i, N_SB + h)),
                  pl.BlockSpec((s, HD), lambda h, i: (0, N_HEADS + N_SB + h)),
                  pl.BlockSpec((s, HD), lambda h, i: (0, 2 * N_HEADS + N_SB + h)),
                  blk, pl.BlockSpec((nb, 8, HD), lambda h, i: (0, 0, 0))],
        out_specs=[blk, pl.BlockSpec((HD, QB), lambda h, i: (h, i)), blk],
        out_shape=[_sds((s, nfox * HD), BF), _sds((nfox * HD, s), BF), _sds((s, nfox * HD), F32)],
        compiler_params=_cp("parallel", "parallel"),
    )(qkv, qkv, qkv, cum_b, cum_t)


def _fox_bwd(qkv, cum_b, cum_t, o, lse, do, name):
    s = qkv.shape[0]
    nb = s // QB
    nfox = N_HEADS - N_SB

    def body(q_ref, k_ref, v_ref, cq_ref, ct_ref, o_ref, lse_ref, do_ref, dq_ref, dk_ref, dv_ref, dcq_ref, dc_ref, dk_acc, dv_acc, dc_acc):
        h, i = pl.program_id(0), pl.program_id(1)

        @pl.when(i == 0)
        def _():
            dk_acc[...] = jnp.zeros_like(dk_acc)
            dv_acc[...] = jnp.zeros_like(dv_acc)
            dc_acc[...] = jnp.zeros_like(dc_acc)

        q = q_ref[...]
        cq = cq_ref[...]
        dov = do_ref[...]
        lse_c = _col(lse_ref[...])
        delta = jnp.sum(dov.astype(F32) * o_ref[...].astype(F32), axis=1, keepdims=True)
        row, col = _iota2()
        ones = jnp.ones((QB, HD), BF)

        def step(j, carry):
            dq, over_keys = carry
            off = pl.multiple_of(j * QB, QB)
            k = k_ref[pl.ds(off, QB), :]
            v = v_ref[pl.ds(off, QB), :]
            z = _dot_nt(q, k) * SCALE + cq - _fox_head_row(ct_ref, j, h)
            p = jnp.where(row + (i - j) * QB >= col, jnp.exp(z - lse_c), 0.0)
            dz = p * (_dot_nt(dov, v) - delta)
            dzt = dz.T
            dq = dq + _dot((dz * SCALE).astype(BF), k)
            dk_acc[pl.ds(off, QB), :] += _dot((dzt * SCALE).astype(BF), q)
            dv_acc[pl.ds(off, QB), :] += _dot(p.T.astype(BF), dov)
            dc_acc[pl.ds(off, QB), :] += _tri_dot(dzt, ones)
            return dq, over_keys + jnp.sum(dz, axis=1, keepdims=True)

        dq, over_keys = lax.fori_loop(0, i + 1, step, (jnp.zeros((QB, HD), F32), jnp.zeros((QB, 1), F32)))
        dq_ref[...] = dq.astype(BF)
        dcq_ref[...] = jnp.broadcast_to(over_keys, (QB, HD))

        @pl.when(i == nb - 1)
        def _():
            dk_ref[...] = dk_acc[...].astype(BF)
            dv_ref[...] = dv_acc[...].astype(BF)
            dc_ref[...] = dc_acc[...]

    blk = pl.BlockSpec((QB, HD), lambda h, i: (i, h))
    full = pl.BlockSpec((s, HD), lambda h, i: (0, h))
    return pl.pallas_call(
        body, name=name, grid=(nfox, nb),
        in_specs=[pl.BlockSpec((QB, HD), lambda h, i: (i, N_SB + h)),
                  pl.BlockSpec((s, HD), lambda h, i: (0, N_HEADS + N_SB + h)),
                  pl.BlockSpec((s, HD), lambda h, i: (0, 2 * N_HEADS + N_SB + h)),
                  blk, pl.BlockSpec((nb, 8, HD), lambda h, i: (0, 0, 0)), blk, blk,
                  pl.BlockSpec((QB, HD), lambda h, i: (i, N_SB + h))],
        out_specs=[blk, full, full, blk, full],
        out_shape=[_sds((s, nfox * HD), BF)] * 3 + [_sds((s, nfox * HD), F32)] * 2,
        scratch_shapes=[pltpu.VMEM((s, HD), F32)] * 3,
        compiler_params=_cp("parallel", "arbitrary"),
    )(qkv, qkv, qkv, cum_b, cum_t, o, lse, do)


def _dil_cols(d):
    per = 3 * N_HEADS

    def qcol(c):
        return (c // N_HEADS) * per + c % N_HEADS

    return qcol, (lambda c: qcol(c) + N_HEADS), (lambda c: qcol(c) + 2 * N_HEADS)


def _dil_logits(q, kc, kp, n):
    row, col = _iota2()
    zc = _dot_nt(q, kc) * SCALE
    zp = _dot_nt(q, kp) * SCALE
    mc = col <= row
    mp = jnp.logical_and(col >= row, n >= 1)
    return zc, zp, mc, mp


def _dil_fwd(qkv, d, name):
    s, width = qkv.shape
    ll = s // d
    nb = ll // QB
    view = qkv.reshape(ll, d * width)
    qcol, kcol, vcol = _dil_cols(d)

    def body(q_ref, kc_ref, kp_ref, vc_ref, vp_ref, o_ref, lse_ref):
        n = pl.program_id(1)
        zc, zp, mc, mp = _dil_logits(q_ref[...], kc_ref[...], kp_ref[...], n)
        zc = jnp.where(mc, zc, NEG_INF)
        zp = jnp.where(mp, zp, NEG_INF)
        m = jnp.maximum(jnp.max(zc, axis=1, keepdims=True), jnp.max(zp, axis=1, keepdims=True))
        ec = jnp.exp(zc - m)
        ep = jnp.where(mp, jnp.exp(zp - m), 0.0)
        l = jnp.sum(ec, axis=1, keepdims=True) + jnp.sum(ep, axis=1, keepdims=True)
        o_ref[...] = _dot((ec / l).astype(BF), vc_ref[...]) + _dot((ep / l).astype(BF), vp_ref[...])
        lse_ref[...] = jnp.broadcast_to(m + jnp.log(l), (QB, HD))

    def cur(f):
        return pl.BlockSpec((QB, HD), lambda c, n: (n, f(c)))

    def prev(f):
        return pl.BlockSpec((QB, HD), lambda c, n: (jnp.maximum(n - 1, 0), f(c)))

    out = pl.BlockSpec((QB, HD), lambda c, n: (n, c))
    o, lse = pl.pallas_call(
        body, name=name, grid=(d * N_HEADS, nb),
        in_specs=[cur(qcol), cur(kcol), prev(kcol), cur(vcol), prev(vcol)],
        out_specs=[out, out], out_shape=[_sds((ll, d * N_HEADS * HD), F32)] * 2,
        compiler_params=_cp("parallel", "parallel"),
    )(view, view, view, view, view)
    return o.reshape(s, N_HEADS * HD), lse.reshape(s, N_HEADS * HD)


def _dil_combine(os_, lses, name):
    s, w = os_[0].shape
    tm, tn = 256, 512
    k = len(os_)

    def body(*refs):
        o_refs, l_refs, (out_ref, outt_ref, g_ref) = refs[:k], refs[k:2 * k], refs[2 * k:]
        ls = [r[...] for r in l_refs]
        m = functools.reduce(jnp.maximum, ls)
        es = [jnp.exp(l - m) for l in ls]
        tot = functools.reduce(lambda a, b: a + b, es)
        out = functools.reduce(lambda a, b: a + b, [(e / tot) * r[...] for e, r in zip(es, o_refs)])
        out_ref[...] = out.astype(BF)
        outt_ref[...] = out.T.astype(BF)
        g_ref[...] = m + jnp.log(tot)

    blk = pl.BlockSpec((tm, tn), lambda i, j: (i, j))
    return pl.pallas_call(
        body, name=name, grid=(s // tm, w // tn), in_specs=[blk] * (2 * k),
        out_specs=[blk, pl.BlockSpec((tn, tm), lambda i, j: (j, i)), blk],
        out_shape=[_sds((s, w), BF), _sds((w, s), BF), _sds((s, w), F32)],
        compiler_params=_cp("parallel", "parallel"),
    )(*os_, *lses)


def _dil_bwd(qkv, out, glse, do, d, name):
    s, width = qkv.shape
    ll = s // d
    nb = ll // QB
    view = qkv.reshape(ll, d * width)
    ow = N_HEADS * HD
    outv, gv, dov = out.reshape(ll, d * ow), glse.reshape(ll, d * ow), do.reshape(ll, d * ow)
    qcol, kcol, vcol = _dil_cols(d)

    def body(q_ref, kc_ref, kp_ref, vc_ref, vp_ref, out_ref, g_ref, do_ref, dq_ref, dk_ref, dv_ref):
        n = pl.program_id(1)

        @pl.when(n == 0)
        def _():
            dk_ref[...] = jnp.zeros_like(dk_ref)
            dv_ref[...] = jnp.zeros_like(dv_ref)

        q, kc, kp = q_ref[...], kc_ref[...], kp_ref[...]
        dov_ = do_ref[...]
        zc, zp, mc, mp = _dil_logits(q, kc, kp, n)
        g = _col(g_ref[...])
        delta = jnp.sum(dov_.astype(F32) * out_ref[...].astype(F32), axis=1, keepdims=True)
        pc = jnp.where(mc, jnp.exp(zc - g), 0.0)
        pp = jnp.where(mp, jnp.exp(zp - g), 0.0)
        dzc = pc * (_dot_nt(dov_, vc_ref[...]) - delta) * SCALE
        dzp = pp * (_dot_nt(dov_, vp_ref[...]) - delta) * SCALE
        dq_ref[...] = _dot(dzc.astype(BF), kc) + _dot(dzp.astype(BF), kp)
        offc = pl.multiple_of(n * QB, QB)
        offp = pl.multiple_of(jnp.maximum(n - 1, 0) * QB, QB)
        dk_ref[pl.ds(offc, QB), :] += _dot(dzc.T.astype(BF), q)
        dv_ref[pl.ds(offc, QB), :] += _dot(pc.T.astype(BF), dov_)
        dk_ref[pl.ds(offp, QB), :] += _dot(dzp.T.astype(BF), q)
        dv_ref[pl.ds(offp, QB), :] += _dot(pp.T.astype(BF), dov_)

    def cur(f):
        return pl.BlockSpec((QB, HD), lambda c, n: (n, f(c)))

    def prev(f):
        return pl.BlockSpec((QB, HD), lambda c, n: (jnp.maximum(n - 1, 0), f(c)))

    blk = pl.BlockSpec((QB, HD), lambda c, n: (n, c))
    full = pl.BlockSpec((ll, HD), lambda c, n: (0, c))
    dq, dk, dv = pl.pallas_call(
        body, name=name, grid=(d * N_HEADS, nb),
        in_specs=[cur(qcol), cur(kcol), prev(kcol), cur(vcol), prev(vcol), blk, blk, blk],
        out_specs=[blk, full, full], out_shape=[_sds((ll, d * ow), F32)] * 3,
        compiler_params=_cp("parallel", "arbitrary"),
    )(view, view, view, view, view, outv, gv, dov)
    return dq.reshape(s, ow), dk.reshape(s, ow), dv.reshape(s, ow)


def _dil_grad_sum(parts, tables, name):
    s, w = parts[0].shape
    tm = 512
    k = len(parts)

    def body(*refs):
        p_refs = refs[:k]
        o_ref = refs[-1]
        t = p_refs[0][...]
        for r in p_refs[1:]:
            t = t + r[...]
        if tables is not None:
            c_ref, sa_ref, sb_ref = refs[k:k + 3]
            t = _rope(t, c_ref[...], sa_ref[...], sb_ref[...])
        o_ref[...] = t.astype(BF)

    blk = pl.BlockSpec((tm, HD), lambda i, j: (i, j))
    tab = pl.BlockSpec((tm, HD), lambda i, j: (i, 0))
    extra = [] if tables is None else list(tables)
    return pl.pallas_call(
        body, name=name, grid=(s // tm, w // HD), in_specs=[blk] * k + [tab] * len(extra), out_specs=blk,
        out_shape=_sds((s, w), BF), compiler_params=_cp("parallel", "parallel"),
    )(*parts, *extra)


def _swiglu_fwd(x, gnorm, w, tag):
    h, ht = _rmsnorm_fwd(x, gnorm, f"norm_{tag}")
    g, u, act = _ffn_up(h, w["gate"], w["up"], f"ffn_up_{tag}")
    y = _ffn_down(act, w["down"], x, f"ffn_down_{tag}")
    return y, (x, ht, g, u, act)


def _swiglu_bwd(saved, gnorm, w, dy, dyb_half, out_scale, tag):
    x, ht, g, u, act = saved
    dg, du = _ffn_bwd_act(dyb_half, w["down"], g, u, f"ffn_bwd_act_{tag}")
    d_down = _grad_rows(_transpose_blocked(act, f"act_t_{tag}"), dyb_half, f"ffn_bwd_wd_{tag}")
    d_gate, d_up = _grad_cols(ht, [dg, du], [False, False], f"ffn_bwd_wgu_{tag}")
    dh = _dh_cols([dg, du], [w["gate"], w["up"]], [False, False], f"ffn_bwd_dh_{tag}")
    dx, dxb, dxbt, dgn = _rmsnorm_bwd(x, gnorm, dh, dy, out_scale, f"norm_bwd_{tag}")
    return (dx, dxb, dxbt), dgn, {"gate": d_gate, "up": d_up, "down": d_down}


def kernel(x, norm_g, ffn1_w_gate, ffn1_w_up, ffn1_w_down, ffn2_w_gate, ffn2_w_up, ffn2_w_down, even_w_in, even_b_forget, even_w_out, odd_w_qkv, odd_w_out, final_norm_g, loss_target, m_norm_g, m_ffn1_w_gate, m_ffn1_w_up, m_ffn1_w_down, m_ffn2_w_gate, m_ffn2_w_up, m_ffn2_w_down, m_even_w_in, m_even_b_forget, m_even_w_out, m_odd_w_qkv, m_odd_w_out, m_final_norm_g, v_norm_g, v_ffn1_w_gate, v_ffn1_w_up, v_ffn1_w_down, v_ffn2_w_gate, v_ffn2_w_up, v_ffn2_w_down, v_even_w_in, v_even_b_forget, v_even_w_out, v_odd_w_qkv, v_odd_w_out, v_final_norm_g):
    s, d = x.shape[1], x.shape[2]
    nfox = N_HEADS - N_SB
    ax, ay, ac = lax.axis_index("x"), lax.axis_index("y"), lax.axis_index("c")
    me = 4 * ax + 2 * ay + ac
    slots = jnp.stack([4 * px + 2 * py + ac for px, py in [(ax, ay), (1 - ax, ay), (ax, 1 - ay), (1 - ax, 1 - ay)]]).astype(jnp.int32)
    x0 = x.reshape(s, d)
    target = loss_target.reshape(s, d)

    def bf(w):
        return w.astype(BF)

    l0 = _all_gather(
        [bf(ffn1_w_gate[0]), bf(ffn1_w_up[0]), bf(ffn1_w_down[0]), bf(ffn2_w_gate[0]), bf(ffn2_w_up[0]), bf(ffn2_w_down[0]),
         bf(even_w_in[0]), bf(even_w_out[0]), norm_g.reshape(6, d // NDEV)], "gather_layer0")
    l1 = _all_gather(
        [bf(ffn1_w_gate[1]), bf(ffn1_w_up[1]), bf(ffn1_w_down[1]), bf(ffn2_w_gate[1]), bf(ffn2_w_up[1]), bf(ffn2_w_down[1]),
         bf(odd_w_qkv[0]), bf(odd_w_out[0])], "gather_layer1")
    wf = [[{"gate": l[0], "up": l[1], "down": l[2]}, {"gate": l[3], "up": l[4], "down": l[5]}] for l in (l0, l1)]
    gn = jnp.transpose(l0[8], (1, 0, 2)).reshape(6, 1, d)
    w_in_nat = jnp.transpose(l0[6], (1, 0, 2)).reshape(d, -1)
    w_qkv_e = w_in_nat[:, :3 * d]
    w_f = jnp.pad(w_in_nat[:, 3 * d:], ((0, 0), (0, HD - nfox)))
    w_out_e = l0[7].reshape(d, d)
    w_qkv_o = l1[6]
    w_out_o = l1[7].reshape(d, d)
    b_pad = jnp.pad(even_b_forget, ((0, 0), (0, HD - nfox)))
    gfin = final_norm_g.reshape(1, d)

    x1, sv_f1_0 = _swiglu_fwd(x0, gn[0], wf[0][0], "l0a")
    h_e, ht_e = _rmsnorm_fwd(x1, gn[1], "norm_l0m")
    qkv_e = _mm_nn(h_e, w_qkv_e, 768, BF, "even_qkv")
    f_e = _mm_nn(h_e, w_f, HD, F32, "even_fgate")
    o_sb, ot_sb, tot_sb = _sb_fwd(qkv_e, "sb_fwd")
    cum_b, cum_t = _fgate_fwd(f_e, b_pad, "fgate_fwd")
    o_fox, ot_fox, lse_fox = _fox_fwd(qkv_e, cum_b, cum_t, "fox_fwd")
    o_e = jnp.concatenate([o_sb, o_fox], axis=1)
    ot_e = jnp.concatenate([ot_sb, ot_fox], axis=0)
    x2 = _mm_nn(o_e, w_out_e, 1024, F32, "even_out", res=x1)
    x3, sv_f2_0 = _swiglu_fwd(x2, gn[2], wf[0][1], "l0b")

    x4, sv_f1_1 = _swiglu_fwd(x3, gn[3], wf[1][0], "l1a")
    h_o, ht_o = _rmsnorm_fwd(x4, gn[4], "norm_l1m")
    qkv_o = _qkv_rope(h_o, w_qkv_o, _rope_tables(s, 1.0), "odd_qkv")
    pats = [_dil_fwd(qkv_o, dil, f"dil_fwd_{dil}") for _, dil in DILATED_PATTERNS]
    o_o, ot_o, glse = _dil_combine([p[0] for p in pats], [p[1] for p in pats], "dil_combine")
    x5 = _mm_nn(o_o, w_out_o, 1024, F32, "odd_out", res=x4)
    x6, sv_f2_1 = _swiglu_fwd(x5, gn[5], wf[1][1], "l1b")

    dx6, dx6b, _, d_gfin, loss_part = _loss_head(x6, gfin, target, "loss_head")

    (dx5, dx5b, dx5bt), dgn5, gw_f2_1 = _swiglu_bwd(sv_f2_1, gn[5], wf[1][1], dx6, dx6b, 1.0, "l1b")
    d_wout_o = _mm_nn(ot_o, dx5b, 1024, BF, "odd_out_dw")
    do_o = _mm_nt([(dx5b, w_out_o)], "odd_out_do").astype(BF)
    inv = _rope_tables(s, -1.0)
    grads = [_dil_bwd(qkv_o, o_o, glse, do_o, dil, f"dil_bwd_{dil}") for _, dil in DILATED_PATTERNS]
    dq = _dil_grad_sum([g_[0] for g_ in grads], inv, "dil_dq")
    dk = _dil_grad_sum([g_[1] for g_ in grads], inv, "dil_dk")
    dv = _dil_grad_sum([g_[2] for g_ in grads], None, "dil_dv")
    dqkv_o = jnp.concatenate([dq, dk, dv], axis=1)
    (d_wqkv_o,) = _grad_cols(ht_o, [dqkv_o], [True], "odd_qkv_dw")
    dh_o = _dh_cols([dqkv_o], [w_qkv_o], [True], "odd_qkv_dh")
    dx4, dx4b, _, dgn4 = _rmsnorm_bwd(x4, gn[4], dh_o, dx5, 0.5, "norm_bwd_l1m")
    (dx3, dx3b, _), dgn3, gw_f1_1 = _swiglu_bwd(sv_f1_1, gn[3], wf[1][0], dx4, dx4b, 0.5, "l1a")

    (dx2, dx2b, dx2bt), dgn2, gw_f2_0 = _swiglu_bwd(sv_f2_0, gn[2], wf[0][1], dx3, dx3b, 1.0, "l0b")
    d_wout_e = _mm_nn(ot_e, dx2b, 1024, BF, "even_out_dw")
    do_e = _mm_nt([(dx2b, w_out_e)], "even_out_do").astype(BF)
    dq_sb, dk_sb, dv_sb = _sb_bwd(qkv_e, do_e, tot_sb, "sb_bwd")
    dq_fx, dk_fx, dv_fx, dcq, dck = _fox_bwd(qkv_e, cum_b, cum_t, o_fox, lse_fox, do_e, "fox_bwd")
    df, db_part = _fgate_bwd(dcq, dck, f_e, b_pad, "fgate_bwd")
    dqkv_e = jnp.concatenate([dq_sb, dq_fx, dk_sb, dk_fx, dv_sb, dv_fx], axis=1)
    d_wqkv_e = _mm_nn(ht_e, dqkv_e, 768, BF, "even_qkv_dw")
    d_wf = _mm_nn(ht_e, df, HD, BF, "even_fgate_dw")
    dh_e = _mm_nt([(dqkv_e, w_qkv_e), (df, w_f)], "even_in_dh")
    dx1, dx1b, _, dgn1 = _rmsnorm_bwd(x1, gn[1], dh_e, dx2, 0.5, "norm_bwd_l0m")
    (dx0, _, _), dgn0, gw_f1_0 = _swiglu_bwd(sv_f1_0, gn[0], wf[0][0], dx1, dx1b, 1.0, "l0a")

    d_win_nat = jnp.concatenate([d_wqkv_e, d_wf[:, :nfox]], axis=1)
    d_win = jnp.transpose(d_win_nat.reshape(d, NDEV, -1), (1, 0, 2))

    def reduce_layer(gs, tag):
        gs = [g_ if g_.ndim == 3 else g_.reshape(NDEV, g_.shape[0] // NDEV, g_.shape[1]) for g_ in gs]
        a_s = _pair_exchange(gs, f"pair_exchange_{tag}")
        ps = [_pair_sum(g_, a_, slots, f"pair_sum_{tag}_{k}") for k, (g_, a_) in enumerate(zip(gs, a_s))]
        bs = _chip_exchange(ps, f"chip_exchange_{tag}")
        return list(zip(gs, a_s, bs))

    r1 = reduce_layer([gw_f1_1["gate"], gw_f1_1["up"], gw_f1_1["down"], gw_f2_1["gate"], gw_f2_1["up"], gw_f2_1["down"],
                       d_wqkv_o, d_wout_o], "l1")
    r0 = reduce_layer([gw_f1_0["gate"], gw_f1_0["up"], gw_f1_0["down"], gw_f2_0["gate"], gw_f2_0["up"], gw_f2_0["down"],
                       d_win, d_wout_e], "l0")

    res = {}
    names = ["ffn1_w_gate", "ffn1_w_up", "ffn1_w_down", "ffn2_w_gate", "ffn2_w_up", "ffn2_w_down"]
    ws = [ffn1_w_gate, ffn1_w_up, ffn1_w_down, ffn2_w_gate, ffn2_w_up, ffn2_w_down]
    ms = [m_ffn1_w_gate, m_ffn1_w_up, m_ffn1_w_down, m_ffn2_w_gate, m_ffn2_w_up, m_ffn2_w_down]
    vs = [v_ffn1_w_gate, v_ffn1_w_up, v_ffn1_w_down, v_ffn2_w_gate, v_ffn2_w_up, v_ffn2_w_down]
    for k, nm in enumerate(names):
        res[nm] = _adamw_sharded(ws[k], ms[k], vs[k], [r0[k], r1[k]], slots, f"adamw_{nm}")
    res["even_w_in"] = _adamw_sharded(even_w_in, m_even_w_in, v_even_w_in, [r0[6]], slots, "adamw_even_w_in")
    res["even_w_out"] = _adamw_sharded(even_w_out, m_even_w_out, v_even_w_out, [r0[7]], slots, "adamw_even_w_out")
    res["odd_w_qkv"] = _adamw_sharded(odd_w_qkv, m_odd_w_qkv, v_odd_w_qkv, [r1[6]], slots, "adamw_odd_w_qkv")
    res["odd_w_out"] = _adamw_sharded(odd_w_out, m_odd_w_out, v_odd_w_out, [r1[7]], slots, "adamw_odd_w_out")

    dnorm = jnp.concatenate([dgn0, dgn1, dgn2, dgn3, dgn4, dgn5], axis=0)
    nsm = d // NDEV
    small_rows = (6 * d + d + 2 * HD) // HD
    pad_rows = -small_rows % 8
    part = jnp.concatenate([dnorm.reshape(-1), d_gfin.reshape(-1), db_part.reshape(-1), loss_part.reshape(-1),
                            jnp.zeros((pad_rows * HD,), F32)]).reshape(small_rows + pad_rows, HD)
    (gathered,) = _all_gather([part], "gather_small")

    def pack(ng, bfg, fg):
        full = lax.dynamic_update_slice(jnp.zeros((6, d), F32), ng.reshape(6, nsm), (0, me * nsm))
        return jnp.concatenate([full.reshape(-1), fg.reshape(-1), jnp.pad(bfg.reshape(-1), (0, HD - nfox)),
                                jnp.zeros((HD + pad_rows * HD,), F32)]).reshape(small_rows + pad_rows, HD)

    sm = _adamw_small(pack(norm_g, even_b_forget, final_norm_g), pack(m_norm_g, m_even_b_forget, m_final_norm_g),
                      pack(v_norm_g, v_even_b_forget, v_final_norm_g), gathered, "adamw_small")

    def unpack(t):
        flat = t.reshape(-1)
        ng = lax.dynamic_slice(flat[:6 * d].reshape(6, d), (0, me * nsm), (6, nsm)).reshape(norm_g.shape)
        fg = flat[6 * d:7 * d].reshape(final_norm_g.shape)
        bfg = flat[7 * d:7 * d + nfox].reshape(even_b_forget.shape)
        return ng, bfg, fg

    sm_g, sm_d, sm_m, sm_v = [unpack(t) for t in sm]
    loss = sm[0].reshape(-1)[7 * d + HD]

    order = ["norm_g", "ffn1_w_gate", "ffn1_w_up", "ffn1_w_down", "ffn2_w_gate", "ffn2_w_up", "ffn2_w_down", "even_w_in",
             "even_b_forget", "even_w_out", "odd_w_qkv", "odd_w_out", "final_norm_g"]
    outs = [loss, dx0.reshape(x.shape)]
    for k in range(4):
        smk = [sm_g, sm_d, sm_m, sm_v][k]
        for nm in order:
            if nm == "norm_g":
                outs.append(smk[0])
            elif nm == "even_b_forget":
                outs.append(smk[1])
            elif nm == "final_norm_g":
                outs.append(smk[2])
            else:
                outs.append(res[nm][k])
    return tuple(outs)
```

```python
import functools
import math

import jax
import jax.numpy as jnp
from jax import lax
from jax.experimental import pallas as pl
from jax.experimental.pallas import tpu as pltpu

F32 = jnp.float32
BF = jnp.bfloat16
NDEV = 8
HD = 128
QB = 128
N_HEADS = 16
N_SB = 8
SCALE = HD ** -0.5
ROPE_THETA = 500000.0
ROPE_DIMS = HD // 4
DILATED_PATTERNS = ((128, 1), (512, 4), (2048, 16))
RMS_EPS = 1e-6
NEG_INF = -1e30
ADAM_LR = 0.001
ADAM_B1 = 0.9
ADAM_B2 = 0.999
ADAM_EPS = 1e-08
ADAM_WD = 0.01
ADAM_STEP = 10
VMEM_LIMIT_V7X = 56 * 1024 * 1024
MESH = pl.DeviceIdType.MESH
ANY = pl.BlockSpec(memory_space=pl.ANY)

NT_DIMS = (((1,), (1,)), ((), ()))


def _cp(*dims):
    return pltpu.CompilerParams(dimension_semantics=dims if dims else None, vmem_limit_bytes=VMEM_LIMIT_V7X)


def _dot(a, b):
    return jnp.dot(a, b, preferred_element_type=F32)


def _dot_nt(a, b):
    return lax.dot_general(a, b, NT_DIMS, preferred_element_type=F32)


def _sds(shape, dtype):
    return jax.ShapeDtypeStruct(shape, dtype)


def _place():
    x, y, c = lax.axis_index("x"), lax.axis_index("y"), lax.axis_index("c")
    chips = [(x, y), (1 - x, y), (x, 1 - y), (1 - x, 1 - y)]
    return x, y, c, chips


def _all_gather(xs, name):
    n = len(xs)

    def body(*refs):
        x_refs, out_refs = refs[:n], refs[n:2 * n]
        send_sems, recv_sems, local_sems = refs[2 * n:]
        x, y, c, chips = _place()
        me, sibling = (x, y, c), (x, y, 1 - c)
        others = chips[1:]

        def slot(a, px, py, pc):
            return out_refs[a].at[4 * px + 2 * py + pc]

        def copy(a, k, block, to, src=None):
            return pltpu.make_async_remote_copy(
                src_ref=slot(a, *block) if src is None else src, dst_ref=slot(a, *block),
                send_sem=send_sems.at[a, k], recv_sem=recv_sems.at[a, k], device_id=to, device_id_type=MESH)

        started = []
        for a in range(n):
            mine = pltpu.make_async_copy(x_refs[a], slot(a, *me), local_sems.at[a])
            mine.start()
            first = [copy(a, 0, me, sibling, src=x_refs[a])]
            first += [copy(a, 1 + j, me, (*chip, c), src=x_refs[a]) for j, chip in enumerate(others)]
            for cp in first:
                cp.start()
            started += [mine.wait] + [cp.wait_send for cp in first]
        for a in range(n):
            for j, chip in enumerate(others):
                copy(a, 1 + j, (*chip, c), me).wait_recv()
                passed = copy(a, 4 + j, (*chip, c), sibling)
                passed.start()
                started.append(passed.wait_send)
        for a in range(n):
            copy(a, 0, sibling, me).wait_recv()
            for j, chip in enumerate(others):
                copy(a, 4 + j, (*chip, 1 - c), me).wait_recv()
        for w in started:
            w()

    return pl.pallas_call(
        body, name=name,
        out_shape=[_sds((NDEV,) + x.shape, x.dtype) for x in xs],
        in_specs=[ANY] * n, out_specs=[ANY] * n,
        scratch_shapes=[pltpu.SemaphoreType.DMA((n, 7)), pltpu.SemaphoreType.DMA((n, 7)), pltpu.SemaphoreType.DMA((n,))],
    )(*xs)


def _pair_exchange(gs, name):
    n = len(gs)

    def body(*refs):
        g_refs, a_refs = refs[:n], refs[n:2 * n]
        send_sems, recv_sems = refs[2 * n:]
        x, y, c, chips = _place()
        copies = []
        for a in range(n):
            for j, (px, py) in enumerate(chips):
                copies.append(pltpu.make_async_remote_copy(
                    src_ref=g_refs[a].at[4 * px + 2 * py + (1 - c)], dst_ref=a_refs[a].at[j],
                    send_sem=send_sems.at[a, j], recv_sem=recv_sems.at[a, j],
                    device_id=(x, y, 1 - c), device_id_type=MESH))
        for cp in copies:
            cp.start()
        for cp in copies:
            cp.wait()

    return pl.pallas_call(
        body, name=name,
        out_shape=[_sds((4,) + g.shape[1:], g.dtype) for g in gs],
        in_specs=[ANY] * n, out_specs=[ANY] * n,
        scratch_shapes=[pltpu.SemaphoreType.DMA((n, 4)), pltpu.SemaphoreType.DMA((n, 4))],
    )(*gs)


HBM = pl.BlockSpec(memory_space=pltpu.HBM)
SEM = pl.BlockSpec(memory_space=pltpu.SEMAPHORE)
EFFECT = pltpu.SideEffectType.DATAFLOW_SIDE_EFFECTING
TOKEN = _sds((8, 128), F32)
TOKEN_SPEC = pl.BlockSpec((8, 128), lambda *_: (0, 0))


def _in_hbm(x):
    return pltpu.with_memory_space_constraint(x, pltpu.HBM)


def _ignore_deps(body, n_in, n_deps):
    if not n_deps:
        return body
    return lambda *refs: body(*refs[:n_in], *refs[n_in + n_deps:])


def _slot_of(px, py, pc):
    return 4 * px + 2 * py + pc


def _gather_start(xs, me, deps, name):
    n = len(xs)
    lands = [lax.dynamic_update_slice(lax.empty((NDEV,) + x.shape, x.dtype), x[None], (me,) + (0,) * x.ndim) for x in xs]

    def body(*refs):
        x_refs, land_refs = refs[:n], refs[n:2 * n]
        send, recv_ici, recv_sib = refs[2 * n:2 * n + 3]
        token = refs[4 * n + 3]
        x, y, c, chips = _place()
        for a in range(n):
            dst = land_refs[a].at[_slot_of(x, y, c)]
            pltpu.make_async_remote_copy(src_ref=x_refs[a], dst_ref=dst, send_sem=send.at[4 * a], recv_sem=recv_sib.at[a],
                                         device_id=(x, y, 1 - c), device_id_type=MESH).start()
            for j, chip in enumerate(chips[1:]):
                pltpu.make_async_remote_copy(src_ref=x_refs[a], dst_ref=dst, send_sem=send.at[4 * a + 1 + j], recv_sem=recv_ici.at[3 * a + j],
                                             device_id=(*chip, c), device_id_type=MESH).start()
        token[...] = jnp.zeros_like(token)

    outs = pl.pallas_call(
        _ignore_deps(body, 2 * n, len(deps)), name=name,
        out_shape=(pltpu.SemaphoreType.DMA((4 * n,)), pltpu.SemaphoreType.DMA((3 * n,)), pltpu.SemaphoreType.DMA((n,)),
                   *[pltpu.HBM(x.shape, x.dtype) for x in xs], *[pltpu.HBM(l.shape, l.dtype) for l in lands], TOKEN),
        in_specs=[HBM] * (2 * n) + [TOKEN_SPEC] * len(deps),
        out_specs=(SEM, SEM, SEM, *[HBM] * (2 * n), pl.BlockSpec(memory_space=pltpu.VMEM)),
        input_output_aliases={a: 3 + a for a in range(2 * n)},
        compiler_params=pltpu.CompilerParams(has_side_effects=EFFECT),
    )(*[_in_hbm(x) for x in xs], *[_in_hbm(l) for l in lands], *deps)
    send, recv_ici, recv_sib = outs[:3]
    return dict(send=send, recv_ici=recv_ici, recv_sib=recv_sib, xs=list(outs[3:3 + n]), lands=list(outs[3 + n:3 + 2 * n]), token=outs[-1])


def _gather_forward(st, after, name):
    n = len(st["lands"])

    def body(*refs):
        land_refs, recv_ici = refs[:n], refs[n]
        send2, recv2 = refs[n + 2], refs[n + 3]
        x, y, c, chips = _place()
        for a in range(n):
            for j, chip in enumerate(chips[1:]):
                blk = land_refs[a].at[_slot_of(*chip, c)]
                pltpu.make_async_remote_copy(src_ref=blk, dst_ref=blk, send_sem=send2.at[3 * a + j], recv_sem=recv_ici.at[3 * a + j],
                                             device_id=(*chip, c), device_id_type=MESH).wait_recv()
                pltpu.make_async_remote_copy(src_ref=blk, dst_ref=blk, send_sem=send2.at[3 * a + j], recv_sem=recv2.at[3 * a + j],
                                             device_id=(x, y, 1 - c), device_id_type=MESH).start()

    outs = pl.pallas_call(
        body, name=name,
        out_shape=(pltpu.SemaphoreType.DMA((3 * n,)), pltpu.SemaphoreType.DMA((3 * n,)), *[pltpu.HBM(l.shape, l.dtype) for l in st["lands"]]),
        in_specs=[HBM] * n + [SEM, pl.BlockSpec(memory_space=pl.ANY)],
        out_specs=(SEM, SEM, *[HBM] * n),
        input_output_aliases={a: 2 + a for a in range(n)},
        compiler_params=pltpu.CompilerParams(has_side_effects=EFFECT),
    )(*st["lands"], st["recv_ici"], after)
    return dict(st, send2=outs[0], recv2=outs[1], lands=list(outs[2:]))


def _gather_finish(st, name):
    n = len(st["lands"])

    def body(*refs):
        x_refs, land_refs = refs[:n], refs[n:2 * n]
        send, recv_sib, send2, recv2 = refs[2 * n:2 * n + 4]
        x, y, c, chips = _place()
        for a in range(n):
            mine = land_refs[a].at[_slot_of(x, y, c)]
            theirs = land_refs[a].at[_slot_of(x, y, 1 - c)]
            for k in range(4):
                pltpu.make_async_remote_copy(src_ref=x_refs[a], dst_ref=mine, send_sem=send.at[4 * a + k], recv_sem=recv_sib.at[a],
                                             device_id=(x, y, 1 - c), device_id_type=MESH).wait_send()
            pltpu.make_async_remote_copy(src_ref=x_refs[a], dst_ref=theirs, send_sem=send.at[4 * a], recv_sem=recv_sib.at[a],
                                         device_id=(x, y, 1 - c), device_id_type=MESH).wait_recv()
            for j, chip in enumerate(chips[1:]):
                sent = land_refs[a].at[_slot_of(*chip, c)]
                got = land_refs[a].at[_slot_of(*chip, 1 - c)]
                pltpu.make_async_remote_copy(src_ref=sent, dst_ref=sent, send_sem=send2.at[3 * a + j], recv_sem=recv2.at[3 * a + j],
                                             device_id=(x, y, 1 - c), device_id_type=MESH).wait_send()
                pltpu.make_async_remote_copy(src_ref=got, dst_ref=got, send_sem=send2.at[3 * a + j], recv_sem=recv2.at[3 * a + j],
                                             device_id=(x, y, 1 - c), device_id_type=MESH).wait_recv()

    outs = pl.pallas_call(
        body, name=name,
        out_shape=tuple(pltpu.HBM(v.shape, v.dtype) for v in st["xs"] + st["lands"]),
        in_specs=[HBM] * (2 * n) + [SEM] * 4, out_specs=tuple([HBM] * (2 * n)),
        input_output_aliases={a: a for a in range(2 * n)},
        compiler_params=pltpu.CompilerParams(has_side_effects=EFFECT),
    )(*st["xs"], *st["lands"], st["send"], st["recv_sib"], st["send2"], st["recv2"])
    return list(outs[n:])


def _chip_start(ps, name):
    n = len(ps)
    lands = [lax.empty(p.shape, p.dtype) for p in ps]

    def body(*refs):
        p_refs, b_refs = refs[:n], refs[n:2 * n]
        send, recv = refs[2 * n], refs[2 * n + 1]
        token = refs[4 * n + 2]
        x, y, c, chips = _place()
        for a in range(n):
            for j, chip in enumerate(chips[1:]):
                pltpu.make_async_remote_copy(src_ref=p_refs[a].at[j], dst_ref=b_refs[a].at[j], send_sem=send.at[3 * a + j], recv_sem=recv.at[3 * a + j],
                                             device_id=(*chip, c), device_id_type=MESH).start()
        token[...] = jnp.zeros_like(token)

    outs = pl.pallas_call(
        body, name=name,
        out_shape=(pltpu.SemaphoreType.DMA((3 * n,)), pltpu.SemaphoreType.DMA((3 * n,)),
                   *[pltpu.HBM(p.shape, p.dtype) for p in ps], *[pltpu.HBM(p.shape, p.dtype) for p in ps], TOKEN),
        in_specs=[HBM] * (2 * n), out_specs=(SEM, SEM, *[HBM] * (2 * n), pl.BlockSpec(memory_space=pltpu.VMEM)),
        input_output_aliases={a: 2 + a for a in range(2 * n)},
        compiler_params=pltpu.CompilerParams(has_side_effects=EFFECT),
    )(*[_in_hbm(p) for p in ps], *[_in_hbm(l) for l in lands])
    return dict(send=outs[0], recv=outs[1], ps=list(outs[2:2 + n]), lands=list(outs[2 + n:2 + 2 * n]), token=outs[-1])


def _chip_finish(st, after, name):
    n = len(st["ps"])

    def body(*refs):
        p_refs, b_refs = refs[:n], refs[n:2 * n]
        send, recv = refs[2 * n], refs[2 * n + 1]
        x, y, c, chips = _place()
        for a in range(n):
            for j, chip in enumerate(chips[1:]):
                cp = pltpu.make_async_remote_copy(src_ref=p_refs[a].at[j], dst_ref=b_refs[a].at[j], send_sem=send.at[3 * a + j], recv_sem=recv.at[3 * a + j],
                                                  device_id=(*chip, c), device_id_type=MESH)
                cp.wait_send()
                cp.wait_recv()

    outs = pl.pallas_call(
        body, name=name,
        out_shape=tuple(pltpu.HBM(v.shape, v.dtype) for v in st["ps"] + st["lands"]),
        in_specs=[HBM] * (2 * n) + [SEM, SEM, pl.BlockSpec(memory_space=pl.ANY)], out_specs=tuple([HBM] * (2 * n)),
        input_output_aliases={a: a for a in range(2 * n)},
        compiler_params=pltpu.CompilerParams(has_side_effects=EFFECT),
    )(*st["ps"], *st["lands"], st["send"], st["recv"], after)
    return list(outs[n:])


def _rows_tile(r):
    for t in (512, 256, 128, 64, 32, 16):
        if r % t == 0:
            return t
    return r


def _pair_sum(g, a, slots, name):
    _, r, c = g.shape
    tr = _rows_tile(r)

    def body(slots_ref, g_ref, a_ref, p_ref):
        p_ref[...] = (g_ref[...].astype(F32) + a_ref[...].astype(F32)).astype(BF)

    return pl.pallas_call(
        body, name=name,
        grid_spec=pltpu.PrefetchScalarGridSpec(
            num_scalar_prefetch=1, grid=(3, r // tr),
            in_specs=[pl.BlockSpec((None, tr, c), lambda j, i, s: (s[j + 1], i, 0)),
                      pl.BlockSpec((None, tr, c), lambda j, i, s: (j + 1, i, 0))],
            out_specs=pl.BlockSpec((None, tr, c), lambda j, i, s: (j, i, 0))),
        out_shape=_sds((3, r, c), BF), compiler_params=_cp("parallel", "parallel"),
    )(slots, g, a)


def _adamw_math(w, g, m, v):
    m = ADAM_B1 * m + (1.0 - ADAM_B1) * g
    v = ADAM_B2 * v + (1.0 - ADAM_B2) * (g * g)
    m_hat = m / (1.0 - ADAM_B1 ** ADAM_STEP)
    v_hat = v / (1.0 - ADAM_B2 ** ADAM_STEP)
    delta = -ADAM_LR * (m_hat / (jnp.sqrt(v_hat) + ADAM_EPS) + ADAM_WD * w)
    return delta, m, v


def _adamw_sharded(w, m, v, parts, slots, name):
    nl, r, c = w.shape
    tr = _rows_tile(r)
    if c * tr * 4 > (1 << 21):
        tr = max(8, tr // 2)

    def body(slots_ref, w_ref, m_ref, v_ref, *rest):
        part_refs, (g_out, d_out, m_out, v_out) = rest[:5 * nl], rest[5 * nl:]
        layer = pl.program_id(0)
        g = None
        for l in range(nl):
            s = part_refs[5 * l][...].astype(F32)
            for ref in part_refs[5 * l + 1:5 * l + 5]:
                s = s + ref[...].astype(F32)
            g = s if g is None else jnp.where(layer == l, s, g)
        delta, mn, vn = _adamw_math(w_ref[...], g, m_ref[...], v_ref[...])
        g_out[...] = g
        d_out[...] = delta
        m_out[...] = mn
        v_out[...] = vn

    def own(l):
        return lambda L, i, s: (s[0], jnp.where(L == l, i, 0), 0)

    def fixed(l, k):
        return lambda L, i, s: (k, jnp.where(L == l, i, 0), 0)

    wspec = pl.BlockSpec((None, tr, c), lambda L, i, s: (L, i, 0))
    in_specs = [wspec, wspec, wspec]
    args = [w, m, v]
    for l, (g, a, b) in enumerate(parts):
        in_specs += [pl.BlockSpec((None, tr, c), own(l)), pl.BlockSpec((None, tr, c), fixed(l, 0)),
                     pl.BlockSpec((None, tr, c), fixed(l, 0)), pl.BlockSpec((None, tr, c), fixed(l, 1)),
                     pl.BlockSpec((None, tr, c), fixed(l, 2))]
        args += [g, a, b, b, b]
    return pl.pallas_call(
        body, name=name,
        grid_spec=pltpu.PrefetchScalarGridSpec(
            num_scalar_prefetch=1, grid=(nl, r // tr), in_specs=in_specs, out_specs=[wspec] * 4),
        out_shape=[_sds(w.shape, F32)] * 4, compiler_params=_cp("arbitrary", "arbitrary"),
    )(slots, *args)


def _adamw_small(w, m, v, gathered, name):
    def body(w_ref, m_ref, v_ref, gg_ref, g_out, d_out, m_out, v_out):
        g = gg_ref[0]
        for k in range(1, NDEV):
            g = g + gg_ref[k]
        delta, mn, vn = _adamw_math(w_ref[...], g, m_ref[...], v_ref[...])
        g_out[...] = g
        d_out[...] = delta
        m_out[...] = mn
        v_out[...] = vn

    return pl.pallas_call(body, name=name, out_shape=[_sds(w.shape, F32)] * 4)(w, m, v, gathered)


def _rmsnorm_fwd(x, g, name, deps=()):
    s, d = x.shape
    tm = 256

    def body(x_ref, g_ref, h_ref, ht_ref):
        xf = x_ref[...]
        y = xf * lax.rsqrt(jnp.mean(xf * xf, axis=-1, keepdims=True) + RMS_EPS)
        h = y * g_ref[...]
        h_ref[...] = h.astype(BF)
        ht_ref[...] = h.T.astype(BF)

    return pl.pallas_call(
        _ignore_deps(body, 2, len(deps)), name=name, grid=(s // tm,),
        in_specs=[pl.BlockSpec((tm, d), lambda i: (i, 0)), pl.BlockSpec((1, d), lambda i: (0, 0))] + [TOKEN_SPEC] * len(deps),
        out_specs=[pl.BlockSpec((tm, d), lambda i: (i, 0)), pl.BlockSpec((d, tm), lambda i: (0, i))],
        out_shape=[_sds((s, d), BF), _sds((d, s), BF)], compiler_params=_cp("parallel"),
    )(x, g, *deps)


def _rmsnorm_bwd(x, g, dh, dres, out_scale, name):
    s, d = x.shape
    tm = 256

    def body(x_ref, g_ref, dh_ref, dres_ref, dx_ref, dxb_ref, dxbt_ref, dg_ref):
        xf = x_ref[...]
        r = lax.rsqrt(jnp.mean(xf * xf, axis=-1, keepdims=True) + RMS_EPS)
        xhat = xf * r
        dhv = dh_ref[...]
        dxhat = dhv * g_ref[...]
        dx = dres_ref[...] + r * (dxhat - xhat * jnp.mean(dxhat * xhat, axis=-1, keepdims=True))
        dx_ref[...] = dx
        scaled = dx * out_scale
        dxb_ref[...] = scaled.astype(BF)
        dxbt_ref[...] = scaled.T.astype(BF)

        @pl.when(pl.program_id(0) == 0)
        def _():
            dg_ref[...] = jnp.zeros_like(dg_ref)

        dg_ref[...] += jnp.sum(dhv * xhat, axis=0, keepdims=True)

    row = pl.BlockSpec((tm, d), lambda i: (i, 0))
    vec = pl.BlockSpec((1, d), lambda i: (0, 0))
    return pl.pallas_call(
        body, name=name, grid=(s // tm,),
        in_specs=[row, vec, row, row],
        out_specs=[row, row, pl.BlockSpec((d, tm), lambda i: (0, i)), vec],
        out_shape=[_sds((s, d), F32), _sds((s, d), BF), _sds((d, s), BF), _sds((1, d), F32)],
        compiler_params=_cp("arbitrary"),
    )(x, g, dh, dres)


def _loss_head(x, g, target, name):
    s, d = x.shape
    tm = 256

    def body(x_ref, g_ref, t_ref, dx_ref, dxb_ref, dxbt_ref, dg_ref, loss_ref):
        xf = x_ref[...]
        r = lax.rsqrt(jnp.mean(xf * xf, axis=-1, keepdims=True) + RMS_EPS)
        xhat = xf * r
        err = xhat * g_ref[...] - t_ref[...]
        dy = err * (1.0 / d)
        dxhat = dy * g_ref[...]
        dx = r * (dxhat - xhat * jnp.mean(dxhat * xhat, axis=-1, keepdims=True))
        dx_ref[...] = dx
        half = dx * 0.5
        dxb_ref[...] = half.astype(BF)
        dxbt_ref[...] = half.T.astype(BF)

        @pl.when(pl.program_id(0) == 0)
        def _():
            dg_ref[...] = jnp.zeros_like(dg_ref)
            loss_ref[...] = jnp.zeros_like(loss_ref)

        dg_ref[...] += jnp.sum(dy * xhat, axis=0, keepdims=True)
        part = 0.5 * jnp.sum(jnp.mean(err * err, axis=-1, keepdims=True), axis=0, keepdims=True)
        lane = lax.broadcasted_iota(jnp.int32, (1, 128), 1)
        loss_ref[...] += jnp.where(lane == 0, part, 0.0)

    row = pl.BlockSpec((tm, d), lambda i: (i, 0))
    vec = pl.BlockSpec((1, d), lambda i: (0, 0))
    return pl.pallas_call(
        body, name=name, grid=(s // tm,),
        in_specs=[row, vec, row],
        out_specs=[row, row, pl.BlockSpec((d, tm), lambda i: (0, i)), vec, pl.BlockSpec((1, 128), lambda i: (0, 0))],
        out_shape=[_sds((s, d), F32), _sds((s, d), BF), _sds((d, s), BF), _sds((1, d), F32), _sds((1, 128), F32)],
        compiler_params=_cp("arbitrary"),
    )(x, g, target)


def _act_spec(tm, n, natural, order):
    if natural:
        return pl.BlockSpec((tm, n), (lambda s, i: (i, s)) if order == "si" else (lambda i, s: (i, s)))
    return pl.BlockSpec((None, tm, n), (lambda s, i: (s, i, 0)) if order == "si" else (lambda i, s: (s, i, 0)))


def _act_shape(s, n, natural, dtype):
    return _sds((s, NDEV * n), dtype) if natural else _sds((NDEV, s, n), dtype)


def _ffn_up(h, wg, wu, name):
    s, d = h.shape
    n = wg.shape[2]
    tm = 512

    def body(h_ref, wg_ref, wu_ref, g_ref, u_ref, a_ref):
        hb = h_ref[...]
        g = _dot(hb, wg_ref[...])
        u = _dot(hb, wu_ref[...])
        g_ref[...] = g.astype(BF)
        u_ref[...] = u.astype(BF)
        a_ref[...] = (g * jax.nn.sigmoid(g) * u).astype(BF)

    wsp = pl.BlockSpec((None, d, n), lambda s_, i: (s_, 0, 0))
    blk = _act_spec(tm, n, False, "si")
    return pl.pallas_call(
        body, name=name, grid=(NDEV, s // tm),
        in_specs=[pl.BlockSpec((tm, d), lambda s_, i: (i, 0)), wsp, wsp],
        out_specs=[blk] * 3, out_shape=[_act_shape(s, n, False, BF)] * 3,
        compiler_params=_cp("parallel", "parallel"),
    )(h, wg, wu)


def _ffn_down(act, wd, x, name):
    _, s, n = act.shape
    d = wd.shape[2]
    tm = 512

    def body(a_ref, w_ref, x_ref, o_ref, acc):
        k = pl.program_id(1)

        @pl.when(k == 0)
        def _():
            acc[...] = jnp.zeros_like(acc)

        acc[...] += _dot(a_ref[...], w_ref[...])

        @pl.when(k == NDEV - 1)
        def _():
            o_ref[...] = x_ref[...] + 0.5 * acc[...]

    row = pl.BlockSpec((tm, d), lambda i, k: (i, 0))
    return pl.pallas_call(
        body, name=name, grid=(s // tm, NDEV),
        in_specs=[_act_spec(tm, n, False, "is"), pl.BlockSpec((None, n, d), lambda i, k: (k, 0, 0)), row],
        out_specs=row, out_shape=_sds((s, d), F32),
        scratch_shapes=[pltpu.VMEM((tm, d), F32)], compiler_params=_cp("parallel", "arbitrary"),
    )(act, wd, x)


def _ffn_bwd_act(dyb, wd, g, u, name, deps=()):
    s, d = dyb.shape
    n = wd.shape[1]
    tm = 512

    def body(dy_ref, w_ref, g_ref, u_ref, dg_ref, du_ref):
        dact = _dot_nt(dy_ref[...], w_ref[...])
        gv = g_ref[...].astype(F32)
        uv = u_ref[...].astype(F32)
        sig = jax.nn.sigmoid(gv)
        dg_ref[...] = (dact * uv * (sig * (1.0 + gv * (1.0 - sig)))).astype(BF)
        du_ref[...] = (dact * (gv * sig)).astype(BF)

    blk = _act_spec(tm, n, False, "si")
    return pl.pallas_call(
        _ignore_deps(body, 4, len(deps)), name=name, grid=(NDEV, s // tm),
        in_specs=[pl.BlockSpec((tm, d), lambda s_, i: (i, 0)), pl.BlockSpec((None, n, d), lambda s_, i: (s_, 0, 0)), blk, blk]
        + [TOKEN_SPEC] * len(deps),
        out_specs=[blk, blk], out_shape=[_act_shape(s, n, False, BF)] * 2,
        compiler_params=_cp("parallel", "parallel"),
    )(dyb, wd, g, u, *deps)


def _grad_rows(act_t, dyb, name):
    _, n, s = act_t.shape
    d = dyb.shape[1]
    tn = 1024

    def body(a_ref, dy_ref, o_ref):
        o_ref[...] = _dot(a_ref[...], dy_ref[...]).astype(BF)

    return pl.pallas_call(
        body, name=name, grid=(NDEV, d // tn),
        in_specs=[pl.BlockSpec((None, n, s), lambda k, j: (k, 0, 0)), pl.BlockSpec((s, tn), lambda k, j: (0, j))],
        out_specs=pl.BlockSpec((None, n, tn), lambda k, j: (k, 0, j)), out_shape=_sds((NDEV, n, d), BF),
        compiler_params=_cp("parallel", "parallel"),
    )(act_t, dyb)


def _transpose_blocked(act, name):
    _, s, n = act.shape
    tm = 512

    def body(a_ref, o_ref):
        o_ref[...] = a_ref[...].astype(F32).T.astype(BF)

    return pl.pallas_call(
        body, name=name, grid=(NDEV, s // tm),
        in_specs=[pl.BlockSpec((None, tm, n), lambda k, i: (k, i, 0))],
        out_specs=pl.BlockSpec((None, n, tm), lambda k, i: (k, 0, i)), out_shape=_sds((NDEV, n, s), BF),
        compiler_params=_cp("parallel", "parallel"),
    )(act)


def _grad_cols(ht, dxs, naturals, name):
    d, s = ht.shape
    k = len(dxs)
    ns = [dx.shape[1] // NDEV if nat else dx.shape[2] for dx, nat in zip(dxs, naturals)]
    td = 512

    def body(*refs):
        ht_ref, dx_refs, o_refs = refs[0], refs[1:1 + k], refs[1 + k:]
        hv = ht_ref[...]
        for dx_ref, o_ref in zip(dx_refs, o_refs):
            o_ref[...] = _dot(hv, dx_ref[...]).astype(BF)

    def dx_spec(n, nat):
        if nat:
            return pl.BlockSpec((s, n), lambda s_, j: (0, s_))
        return pl.BlockSpec((None, s, n), lambda s_, j: (s_, 0, 0))

    return pl.pallas_call(
        body, name=name, grid=(NDEV, d // td),
        in_specs=[pl.BlockSpec((td, s), lambda s_, j: (j, 0))] + [dx_spec(n, nat) for n, nat in zip(ns, naturals)],
        out_specs=[pl.BlockSpec((None, td, n), lambda s_, j: (s_, j, 0)) for n in ns],
        out_shape=[_sds((NDEV, d, n), BF) for n in ns], compiler_params=_cp("parallel", "parallel"),
    )(ht, *dxs)


def _dh_cols(dxs, ws, naturals, name):
    k = len(dxs)
    d = ws[0].shape[1]
    ns = [w.shape[2] for w in ws]
    s = dxs[0].shape[0] if naturals[0] else dxs[0].shape[1]
    tm = 512

    def body(*refs):
        dx_refs, w_refs, o_ref, acc = refs[:k], refs[k:2 * k], refs[2 * k], refs[2 * k + 1]
        j = pl.program_id(1)

        @pl.when(j == 0)
        def _():
            acc[...] = jnp.zeros_like(acc)

        t = _dot_nt(dx_refs[0][...], w_refs[0][...])
        for dx_ref, w_ref in zip(dx_refs[1:], w_refs[1:]):
            t = t + _dot_nt(dx_ref[...], w_ref[...])
        acc[...] += t

        @pl.when(j == NDEV - 1)
        def _():
            o_ref[...] = acc[...]

    return pl.pallas_call(
        body, name=name, grid=(s // tm, NDEV),
        in_specs=[_act_spec(tm, n, nat, "is") for n, nat in zip(ns, naturals)]
        + [pl.BlockSpec((None, d, n), lambda i, j: (j, 0, 0)) for n in ns],
        out_specs=pl.BlockSpec((tm, d), lambda i, j: (i, 0)), out_shape=_sds((s, d), F32),
        scratch_shapes=[pltpu.VMEM((tm, d), F32)], compiler_params=_cp("parallel", "arbitrary"),
    )(*dxs, *ws)


def _mm_nn(a, b, tn, out_dtype, name, res=None, tm=512):
    m, k = a.shape
    nn = b.shape[1]

    def body(*refs):
        if res is None:
            a_ref, b_ref, o_ref = refs
            o_ref[...] = _dot(a_ref[...], b_ref[...]).astype(out_dtype)
        else:
            a_ref, b_ref, r_ref, o_ref = refs
            o_ref[...] = (r_ref[...] + _dot(a_ref[...], b_ref[...])).astype(out_dtype)

    osp = pl.BlockSpec((tm, tn), lambda j, i: (i, j))
    in_specs = [pl.BlockSpec((tm, k), lambda j, i: (i, 0)), pl.BlockSpec((k, tn), lambda j, i: (0, j))]
    args = [a, b]
    if res is not None:
        in_specs.append(osp)
        args.append(res)
    return pl.pallas_call(
        body, name=name, grid=(nn // tn, m // tm), in_specs=in_specs, out_specs=osp,
        out_shape=_sds((m, nn), out_dtype), compiler_params=_cp("parallel", "parallel"),
    )(*args)


def _mm_nt(pairs, name, tm=512, tk=512, deps=()):
    m = pairs[0][0].shape[0]
    kk = pairs[0][1].shape[0]
    p = len(pairs)

    def body(*refs):
        o_ref = refs[2 * p]
        t = _dot_nt(refs[0][...], refs[1][...])
        for q in range(1, p):
            t = t + _dot_nt(refs[2 * q][...], refs[2 * q + 1][...])
        o_ref[...] = t

    in_specs, args = [], []
    for a, b in pairs:
        in_specs += [pl.BlockSpec((tm, a.shape[1]), lambda j, i: (i, 0)), pl.BlockSpec((tk, b.shape[1]), lambda j, i: (j, 0))]
        args += [a, b]
    return pl.pallas_call(
        _ignore_deps(body, 2 * p, len(deps)), name=name, grid=(kk // tk, m // tm), in_specs=in_specs + [TOKEN_SPEC] * len(deps),
        out_specs=pl.BlockSpec((tm, tk), lambda j, i: (i, j)), out_shape=_sds((m, kk), F32),
        compiler_params=_cp("parallel", "parallel"),
    )(*args, *deps)


def _rope_tables(s, sign):
    half = ROPE_DIMS // 2
    freqs = ROPE_THETA ** (-jnp.arange(half, dtype=F32) / half)
    ang = jnp.arange(s, dtype=F32)[:, None] * freqs[None, :]
    cos, sin = jnp.cos(ang), sign * jnp.sin(ang)
    one = jnp.ones((s, HD - ROPE_DIMS), F32)
    zero = jnp.zeros((s, HD - ROPE_DIMS), F32)
    zh = jnp.zeros((s, half), F32)
    c = jnp.concatenate([cos, cos, one], axis=1)
    sa = jnp.concatenate([-sin, zh, zero], axis=1)
    sb = jnp.concatenate([zh, sin, zero], axis=1)
    return c, sa, sb


def _rope(xv, c, sa, sb):
    return xv * c + pltpu.roll(xv, HD - ROPE_DIMS // 2, 1) * sa + pltpu.roll(xv, ROPE_DIMS // 2, 1) * sb


def _qkv_rope(h, w, tables, name):
    s, d = h.shape
    n = w.shape[2]
    per = n // HD
    tm = 512

    def body(h_ref, w_ref, c_ref, sa_ref, sb_ref, o_ref):
        shard = pl.program_id(0)
        y = _dot(h_ref[...], w_ref[...])
        c, sa, sb = c_ref[...], sa_ref[...], sb_ref[...]
        for j in range(per):
            blk = y[:, j * HD:(j + 1) * HD]
            rot = _rope(blk, c, sa, sb)
            is_qk = shard * per + j < 2 * N_HEADS
            o_ref[:, j * HD:(j + 1) * HD] = jnp.where(is_qk, rot, blk).astype(BF)

    tab = pl.BlockSpec((tm, HD), lambda s_, i: (i, 0))
    return pl.pallas_call(
        body, name=name, grid=(NDEV, s // tm),
        in_specs=[pl.BlockSpec((tm, d), lambda s_, i: (i, 0)), pl.BlockSpec((None, d, n), lambda s_, i: (s_, 0, 0)), tab, tab, tab],
        out_specs=pl.BlockSpec((tm, n), lambda s_, i: (i, s_)), out_shape=_sds((s, NDEV * n), BF),
        compiler_params=_cp("parallel", "parallel"),
    )(h, w, *tables)


def _iota2():
    return (lax.broadcasted_iota(jnp.int32, (QB, QB), 0), lax.broadcasted_iota(jnp.int32, (QB, QB), 1))


def _softplus(z):
    return jnp.maximum(z, 0.0) + jnp.log(1.0 + jnp.exp(-jnp.abs(z)))


def _tri_dot(xv, tri, left=False):
    hi = xv.astype(BF)
    r1 = xv - hi.astype(F32)
    mid = r1.astype(BF)
    lo = (r1 - mid.astype(F32)).astype(BF)
    if left:
        return _dot(tri, hi) + _dot(tri, mid) + _dot(tri, lo)
    return _dot(hi, tri) + _dot(mid, tri) + _dot(lo, tri)


def _col(ref_or_val):
    return ref_or_val[:, 0:1]


def _sb_fwd(qkv, name):
    s = qkv.shape[0]
    nb = s // QB

    def body(q_ref, k_ref, v_ref, o_ref, ot_ref, t_ref):
        i = pl.program_id(1)
        q = q_ref[...]
        row, col = _iota2()
        after_tri = (row > col).astype(BF)

        def step(jj, carry):
            acc, later = carry
            j = i - jj
            off = pl.multiple_of(j * QB, QB)
            k = k_ref[pl.ds(off, QB), :]
            v = v_ref[pl.ds(off, QB), :]
            z = _dot_nt(q, k) * SCALE
            strict = row + (i - j) * QB > col
            sp = _softplus(z)
            lnb = jnp.where(strict, -sp, 0.0)
            after = later + _tri_dot(lnb, after_tri)
            w = jnp.where(strict, jnp.exp((z - sp) + after), 0.0)
            acc = acc + _dot(w.astype(BF), v)
            return acc, later + jnp.sum(lnb, axis=1, keepdims=True)

        acc, total = lax.fori_loop(0, i + 1, step, (jnp.zeros((QB, HD), F32), jnp.zeros((QB, 1), F32)))
        o_ref[...] = acc.astype(BF)
        ot_ref[...] = acc.T.astype(BF)
        t_ref[...] = jnp.broadcast_to(total, (QB, HD))

    blk = pl.BlockSpec((QB, HD), lambda h, i: (i, h))
    return pl.pallas_call(
        body, name=name, grid=(N_SB, nb),
        in_specs=[blk, pl.BlockSpec((s, HD), lambda h, i: (0, N_HEADS + h)), pl.BlockSpec((s, HD), lambda h, i: (0, 2 * N_HEADS + h))],
        out_specs=[blk, pl.BlockSpec((HD, QB), lambda h, i: (h, i)), blk],
        out_shape=[_sds((s, N_SB * HD), BF), _sds((N_SB * HD, s), BF), _sds((s, N_SB * HD), F32)],
        compiler_params=_cp("parallel", "parallel"),
    )(qkv, qkv, qkv)


def _sb_bwd(qkv, do, total, name):
    s = qkv.shape[0]
    nb = s // QB

    def body(q_ref, k_ref, v_ref, do_ref, t_ref, dq_ref, dk_ref, dv_ref, dk_acc, dv_acc):
        i = pl.program_id(1)

        @pl.when(i == 0)
        def _():
            dk_acc[...] = jnp.zeros_like(dk_acc)
            dv_acc[...] = jnp.zeros_like(dv_acc)

        q = q_ref[...]
        dov = do_ref[...]
        tot = _col(t_ref[...])
        row, col = _iota2()
        upto_tri = (row <= col).astype(BF)
        before_tri = (row < col).astype(BF)

        def step(j, carry):
            dq, lnb_before, dl_before = carry
            off = pl.multiple_of(j * QB, QB)
            k = k_ref[pl.ds(off, QB), :]
            v = v_ref[pl.ds(off, QB), :]
            z = _dot_nt(q, k) * SCALE
            strict = row + (i - j) * QB > col
            sp = _softplus(z)
            lnb = jnp.where(strict, -sp, 0.0)
            after = tot - (lnb_before + _tri_dot(lnb, upto_tri))
            a = jnp.where(strict, jnp.exp((z - sp) + after), 0.0)
            dl = a * _dot_nt(dov, v)
            before = dl_before + _tri_dot(dl, before_tri)
            sig = jnp.exp(z - sp)
            dz = jnp.where(strict, dl * (1.0 - sig) - sig * before, 0.0) * SCALE
            dq = dq + _dot(dz.astype(BF), k)
            dk_acc[pl.ds(off, QB), :] += _dot(dz.T.astype(BF), q)
            dv_acc[pl.ds(off, QB), :] += _dot(a.T.astype(BF), dov)
            return dq, lnb_before + jnp.sum(lnb, axis=1, keepdims=True), dl_before + jnp.sum(dl, axis=1, keepdims=True)

        zero = jnp.zeros((QB, 1), F32)
        dq, _, _ = lax.fori_loop(0, i + 1, step, (jnp.zeros((QB, HD), F32), zero, zero))
        dq_ref[...] = dq.astype(BF)

        @pl.when(i == nb - 1)
        def _():
            dk_ref[...] = dk_acc[...].astype(BF)
            dv_ref[...] = dv_acc[...].astype(BF)

    blk = pl.BlockSpec((QB, HD), lambda h, i: (i, h))
    full = pl.BlockSpec((s, HD), lambda h, i: (0, h))
    return pl.pallas_call(
        body, name=name, grid=(N_SB, nb),
        in_specs=[blk, pl.BlockSpec((s, HD), lambda h, i: (0, N_HEADS + h)), pl.BlockSpec((s, HD), lambda h, i: (0, 2 * N_HEADS + h)), blk, blk],
        out_specs=[blk, full, full], out_shape=[_sds((s, N_SB * HD), BF)] * 3,
        scratch_shapes=[pltpu.VMEM((s, HD), F32), pltpu.VMEM((s, HD), F32)],
        compiler_params=_cp("parallel", "arbitrary"),
    )(qkv, qkv, qkv, do, total)


def _fgate_fwd(f, b, name):
    s = f.shape[0]
    nb = s // QB
    nfox = N_HEADS - N_SB

    def body(f_ref, b_ref, cb_ref, ct_ref):
        row, col = _iota2()
        upto = (row >= col).astype(BF)
        carry = jnp.zeros((1, HD), F32)
        for blk in range(nb):
            xv = f_ref[blk * QB:(blk + 1) * QB, :] + b_ref[...]
            logf = -_softplus(-xv)
            cum = _tri_dot(logf, upto, left=True) + carry
            carry = cum[QB - 1:QB, :]
            ct_ref[blk] = cum.T
            for h in range(nfox):
                cb_ref[blk * QB:(blk + 1) * QB, h * HD:(h + 1) * HD] = jnp.broadcast_to(cum[:, h:h + 1], (QB, HD))

    return pl.pallas_call(
        body, name=name, out_shape=[_sds((s, nfox * HD), F32), _sds((nb, HD, HD), F32)], compiler_params=_cp(),
    )(f, b)


def _fgate_bwd(dcq, dck, f, b, name):
    s = f.shape[0]
    nb = s // QB
    nfox = N_HEADS - N_SB

    def body(dcq_ref, dck_ref, f_ref, b_ref, df_ref, db_ref):
        row, col = _iota2()
        from_tri = (row <= col).astype(BF)
        lane = col
        carry = jnp.zeros((1, HD), F32)
        db = jnp.zeros((1, HD), F32)
        for blk in reversed(range(nb)):
            dcum = jnp.zeros((QB, HD), F32)
            for h in range(nfox):
                here = (slice(blk * QB, (blk + 1) * QB), slice(h * HD, (h + 1) * HD))
                dcum = jnp.where(lane == h, dcq_ref[here] - dck_ref[here], dcum)
            dlogf = _tri_dot(dcum, from_tri, left=True) + carry
            carry = dlogf[0:1, :]
            xv = f_ref[blk * QB:(blk + 1) * QB, :] + b_ref[...]
            sp = _softplus(xv)
            df = jnp.where(lane < nfox, dlogf * jnp.exp(-sp), 0.0)
            df_ref[blk * QB:(blk + 1) * QB, :] = df.astype(BF)
            db = db + jnp.sum(df, axis=0, keepdims=True)
        db_ref[...] = db

    return pl.pallas_call(
        body, name=name, out_shape=[_sds((s, HD), BF), _sds((1, HD), F32)], compiler_params=_cp(),
    )(dcq, dck, f, b)


def _fox_head_row(ct_ref, j, h):
    tile = ct_ref[j]
    sub = lax.broadcasted_iota(jnp.int32, tile.shape, 0)
    return jnp.sum(jnp.where(sub == h, tile, 0.0), axis=0, keepdims=True)


def _fox_fwd(qkv, cum_b, cum_t, name):
    s = qkv.shape[0]
    nb = s // QB
    nfox = N_HEADS - N_SB

    def body(q_ref, k_ref, v_ref, cq_ref, ct_ref, o_ref, ot_ref, lse_ref):
        h, i = pl.program_id(0), pl.program_id(1)
        q = q_ref[...]
        cq = cq_ref[...]
        row, col = _iota2()

        def step(j, carry):
            acc, m, l = carry
            off = pl.multiple_of(j * QB, QB)
            k = k_ref[pl.ds(off, QB), :]
            v = v_ref[pl.ds(off, QB), :]
            z = _dot_nt(q, k) * SCALE + cq - _fox_head_row(ct_ref, j, h)
            z = jnp.where(row + (i - j) * QB >= col, z, NEG_INF)
            m_new = jnp.maximum(m, jnp.max(z, axis=1, keepdims=True))
            alpha = jnp.exp(m - m_new)
            p = jnp.exp(z - m_new)
            l = alpha * l + jnp.sum(p, axis=1, keepdims=True)
            acc = alpha * acc + _dot(p.astype(BF), v)
            return acc, m_new, l

        acc, m, l = lax.fori_loop(0, i + 1, step, (jnp.zeros((QB, HD), F32), jnp.full((QB, 1), NEG_INF, F32), jnp.zeros((QB, 1), F32)))
        o = acc / l
        o_ref[...] = o.astype(BF)
        ot_ref[...] = o.T.astype(BF)
        lse_ref[...] = jnp.broadcast_to(m + jnp.log(l), (QB, HD))

    blk = pl.BlockSpec((QB, HD), lambda h, i: (i, h))
    return pl.pallas_call(
        body, name=name, grid=(nfox, nb),
        in_specs=[pl.BlockSpec((QB, HD), lambda h, i: (i, N_SB + h)),
                  pl.BlockSpec((s, HD), lambda h, i: (0, N_HEADS + N_SB + h)),
                  pl.BlockSpec((s, HD), lambda h, i: (0, 2 * N_HEADS + N_SB + h)),
                  blk, pl.BlockSpec((nb, 8, HD), lambda h, i: (0, 0, 0))],
        out_specs=[blk, pl.BlockSpec((HD, QB), lambda h, i: (h, i)), blk],
        out_shape=[_sds((s, nfox * HD), BF), _sds((nfox * HD, s), BF), _sds((s, nfox * HD), F32)],
        compiler_params=_cp("parallel", "parallel"),
    )(qkv, qkv, qkv, cum_b, cum_t)


def _fox_bwd(qkv, cum_b, cum_t, o, lse, do, name):
    s = qkv.shape[0]
    nb = s // QB
    nfox = N_HEADS - N_SB

    def body(q_ref, k_ref, v_ref, cq_ref, ct_ref, o_ref, lse_ref, do_ref, dq_ref, dk_ref, dv_ref, dcq_ref, dc_ref, dk_acc, dv_acc, dc_acc):
        h, i = pl.program_id(0), pl.program_id(1)

        @pl.when(i == 0)
        def _():
            dk_acc[...] = jnp.zeros_like(dk_acc)
            dv_acc[...] = jnp.zeros_like(dv_acc)
            dc_acc[...] = jnp.zeros_like(dc_acc)

        q = q_ref[...]
        cq = cq_ref[...]
        dov = do_ref[...]
        lse_c = _col(lse_ref[...])
        delta = jnp.sum(dov.astype(F32) * o_ref[...].astype(F32), axis=1, keepdims=True)
        row, col = _iota2()
        ones = jnp.ones((QB, HD), BF)

        def step(j, carry):
            dq, over_keys = carry
            off = pl.multiple_of(j * QB, QB)
            k = k_ref[pl.ds(off, QB), :]
            v = v_ref[pl.ds(off, QB), :]
            z = _dot_nt(q, k) * SCALE + cq - _fox_head_row(ct_ref, j, h)
            p = jnp.where(row + (i - j) * QB >= col, jnp.exp(z - lse_c), 0.0)
            dz = p * (_dot_nt(dov, v) - delta)
            dzt = dz.T
            dq = dq + _dot((dz * SCALE).astype(BF), k)
            dk_acc[pl.ds(off, QB), :] += _dot((dzt * SCALE).astype(BF), q)
            dv_acc[pl.ds(off, QB), :] += _dot(p.T.astype(BF), dov)
            dc_acc[pl.ds(off, QB), :] += _tri_dot(dzt, ones)
            return dq, over_keys + jnp.sum(dz, axis=1, keepdims=True)

        dq, over_keys = lax.fori_loop(0, i + 1, step, (jnp.zeros((QB, HD), F32), jnp.zeros((QB, 1), F32)))
        dq_ref[...] = dq.astype(BF)
        dcq_ref[...] = jnp.broadcast_to(over_keys, (QB, HD))

        @pl.when(i == nb - 1)
        def _():
            dk_ref[...] = dk_acc[...].astype(BF)
            dv_ref[...] = dv_acc[...].astype(BF)
            dc_ref[...] = dc_acc[...]

    blk = pl.BlockSpec((QB, HD), lambda h, i: (i, h))
    full = pl.BlockSpec((s, HD), lambda h, i: (0, h))
    return pl.pallas_call(
        body, name=name, grid=(nfox, nb),
        in_specs=[pl.BlockSpec((QB, HD), lambda h, i: (i, N_SB + h)),
                  pl.BlockSpec((s, HD), lambda h, i: (0, N_HEADS + N_SB + h)),
                  pl.BlockSpec((s, HD), lambda h, i: (0, 2 * N_HEADS + N_SB + h)),
                  blk, pl.BlockSpec((nb, 8, HD), lambda h, i: (0, 0, 0)), blk, blk,
                  pl.BlockSpec((QB, HD), lambda h, i: (i, N_SB + h))],
        out_specs=[blk, full, full, blk, full],
        out_shape=[_sds((s, nfox * HD), BF)] * 3 + [_sds((s, nfox * HD), F32)] * 2,
        scratch_shapes=[pltpu.VMEM((s, HD), F32)] * 3,
        compiler_params=_cp("parallel", "arbitrary"),
    )(qkv, qkv, qkv, cum_b, cum_t, o, lse, do)


def _dil_cols(d):
    per = 3 * N_HEADS

    def qcol(c):
        return (c // N_HEADS) * per + c % N_HEADS

    return qcol, (lambda c: qcol(c) + N_HEADS), (lambda c: qcol(c) + 2 * N_HEADS)


def _dil_logits(q, kc, kp, n):
    row, col = _iota2()
    zc = _dot_nt(q, kc) * SCALE
    zp = _dot_nt(q, kp) * SCALE
    mc = col <= row
    mp = jnp.logical_and(col >= row, n >= 1)
    return zc, zp, mc, mp


def _dil_fwd(qkv, d, name):
    s, width = qkv.shape
    ll = s // d
    nb = ll // QB
    view = qkv.reshape(ll, d * width)
    qcol, kcol, vcol = _dil_cols(d)

    def body(q_ref, kc_ref, kp_ref, vc_ref, vp_ref, o_ref, lse_ref):
        n = pl.program_id(1)
        zc, zp, mc, mp = _dil_logits(q_ref[...], kc_ref[...], kp_ref[...], n)
        zc = jnp.where(mc, zc, NEG_INF)
        zp = jnp.where(mp, zp, NEG_INF)
        m = jnp.maximum(jnp.max(zc, axis=1, keepdims=True), jnp.max(zp, axis=1, keepdims=True))
        ec = jnp.exp(zc - m)
        ep = jnp.where(mp, jnp.exp(zp - m), 0.0)
        l = jnp.sum(ec, axis=1, keepdims=True) + jnp.sum(ep, axis=1, keepdims=True)
        o_ref[...] = _dot((ec / l).astype(BF), vc_ref[...]) + _dot((ep / l).astype(BF), vp_ref[...])
        lse_ref[...] = jnp.broadcast_to(m + jnp.log(l), (QB, HD))

    def cur(f):
        return pl.BlockSpec((QB, HD), lambda c, n: (n, f(c)))

    def prev(f):
        return pl.BlockSpec((QB, HD), lambda c, n: (jnp.maximum(n - 1, 0), f(c)))

    out = pl.BlockSpec((QB, HD), lambda c, n: (n, c))
    o, lse = pl.pallas_call(
        body, name=name, grid=(d * N_HEADS, nb),
        in_specs=[cur(qcol), cur(kcol), prev(kcol), cur(vcol), prev(vcol)],
        out_specs=[out, out], out_shape=[_sds((ll, d * N_HEADS * HD), F32)] * 2,
        compiler_params=_cp("parallel", "parallel"),
    )(view, view, view, view, view)
    return o.reshape(s, N_HEADS * HD), lse.reshape(s, N_HEADS * HD)


def _dil_combine(os_, lses, name):
    s, w = os_[0].shape
    tm, tn = 256, 512
    k = len(os_)

    def body(*refs):
        o_refs, l_refs, (out_ref, outt_ref, g_ref) = refs[:k], refs[k:2 * k], refs[2 * k:]
        ls = [r[...] for r in l_refs]
        m = functools.reduce(jnp.maximum, ls)
        es = [jnp.exp(l - m) for l in ls]
        tot = functools.reduce(lambda a, b: a + b, es)
        out = functools.reduce(lambda a, b: a + b, [(e / tot) * r[...] for e, r in zip(es, o_refs)])
        out_ref[...] = out.astype(BF)
        outt_ref[...] = out.T.astype(BF)
        g_ref[...] = m + jnp.log(tot)

    blk = pl.BlockSpec((tm, tn), lambda i, j: (i, j))
    return pl.pallas_call(
        body, name=name, grid=(s // tm, w // tn), in_specs=[blk] * (2 * k),
        out_specs=[blk, pl.BlockSpec((tn, tm), lambda i, j: (j, i)), blk],
        out_shape=[_sds((s, w), BF), _sds((w, s), BF), _sds((s, w), F32)],
        compiler_params=_cp("parallel", "parallel"),
    )(*os_, *lses)


def _dil_bwd(qkv, out, glse, do, d, name):
    s, width = qkv.shape
    ll = s // d
    nb = ll // QB
    view = qkv.reshape(ll, d * width)
    ow = N_HEADS * HD
    outv, gv, dov = out.reshape(ll, d * ow), glse.reshape(ll, d * ow), do.reshape(ll, d * ow)
    qcol, kcol, vcol = _dil_cols(d)

    def body(q_ref, kc_ref, kp_ref, vc_ref, vp_ref, out_ref, g_ref, do_ref, dq_ref, dk_ref, dv_ref):
        n = pl.program_id(1)

        @pl.when(n == 0)
        def _():
            dk_ref[...] = jnp.zeros_like(dk_ref)
            dv_ref[...] = jnp.zeros_like(dv_ref)

        q, kc, kp = q_ref[...], kc_ref[...], kp_ref[...]
        dov_ = do_ref[...]
        zc, zp, mc, mp = _dil_logits(q, kc, kp, n)
        g = _col(g_ref[...])
        delta = jnp.sum(dov_.astype(F32) * out_ref[...].astype(F32), axis=1, keepdims=True)
        pc = jnp.where(mc, jnp.exp(zc - g), 0.0)
        pp = jnp.where(mp, jnp.exp(zp - g), 0.0)
        dzc = pc * (_dot_nt(dov_, vc_ref[...]) - delta) * SCALE
        dzp = pp * (_dot_nt(dov_, vp_ref[...]) - delta) * SCALE
        dq_ref[...] = _dot(dzc.astype(BF), kc) + _dot(dzp.astype(BF), kp)
        offc = pl.multiple_of(n * QB, QB)
        offp = pl.multiple_of(jnp.maximum(n - 1, 0) * QB, QB)
        dk_ref[pl.ds(offc, QB), :] += _dot(dzc.T.astype(BF), q)
        dv_ref[pl.ds(offc, QB), :] += _dot(pc.T.astype(BF), dov_)
        dk_ref[pl.ds(offp, QB), :] += _dot(dzp.T.astype(BF), q)
        dv_ref[pl.ds(offp, QB), :] += _dot(pp.T.astype(BF), dov_)

    def cur(f):
        return pl.BlockSpec((QB, HD), lambda c, n: (n, f(c)))

    def prev(f):
        return pl.BlockSpec((QB, HD), lambda c, n: (jnp.maximum(n - 1, 0), f(c)))

    blk = pl.BlockSpec((QB, HD), lambda c, n: (n, c))
    full = pl.BlockSpec((ll, HD), lambda c, n: (0, c))
    dq, dk, dv = pl.pallas_call(
        body, name=name, grid=(d * N_HEADS, nb),
        in_specs=[cur(qcol), cur(kcol), prev(kcol), cur(vcol), prev(vcol), blk, blk, blk],
        out_specs=[blk, full, full], out_shape=[_sds((ll, d * ow), F32)] * 3,
        compiler_params=_cp("parallel", "arbitrary"),
    )(view, view, view, view, view, outv, gv, dov)
    return dq.reshape(s, ow), dk.reshape(s, ow), dv.reshape(s, ow)


def _dil_grad_sum(parts, tables, name):
    s, w = parts[0].shape
    tm = 512
    k = len(parts)

    def body(*refs):
        p_refs = refs[:k]
        o_ref = refs[-1]
        t = p_refs[0][...]
        for r in p_refs[1:]:
            t = t + r[...]
        if tables is not None:
            c_ref, sa_ref, sb_ref = refs[k:k + 3]
            t = _rope(t, c_ref[...], sa_ref[...], sb_ref[...])
        o_ref[...] = t.astype(BF)

    blk = pl.BlockSpec((tm, HD), lambda i, j: (i, j))
    tab = pl.BlockSpec((tm, HD), lambda i, j: (i, 0))
    extra = [] if tables is None else list(tables)
    return pl.pallas_call(
        body, name=name, grid=(s // tm, w // HD), in_specs=[blk] * k + [tab] * len(extra), out_specs=blk,
        out_shape=_sds((s, w), BF), compiler_params=_cp("parallel", "parallel"),
    )(*parts, *extra)


def _swiglu_fwd(x, gnorm, w, tag, deps=()):
    h, ht = _rmsnorm_fwd(x, gnorm, f"norm_{tag}", deps)
    g, u, act = _ffn_up(h, w["gate"], w["up"], f"ffn_up_{tag}")
    y = _ffn_down(act, w["down"], x, f"ffn_down_{tag}")
    return y, (x, ht, g, u, act)


def _swiglu_bwd(saved, gnorm, w, dy, dyb_half, out_scale, tag, deps=()):
    x, ht, g, u, act = saved
    dg, du = _ffn_bwd_act(dyb_half, w["down"], g, u, f"ffn_bwd_act_{tag}", deps)
    d_down = _grad_rows(_transpose_blocked(act, f"act_t_{tag}"), dyb_half, f"ffn_bwd_wd_{tag}")
    d_gate, d_up = _grad_cols(ht, [dg, du], [False, False], f"ffn_bwd_wgu_{tag}")
    dh = _dh_cols([dg, du], [w["gate"], w["up"]], [False, False], f"ffn_bwd_dh_{tag}")
    dx, dxb, dxbt, dgn = _rmsnorm_bwd(x, gnorm, dh, dy, out_scale, f"norm_bwd_{tag}")
    return (dx, dxb, dxbt), dgn, {"gate": d_gate, "up": d_up, "down": d_down}


def kernel(x, norm_g, ffn1_w_gate, ffn1_w_up, ffn1_w_down, ffn2_w_gate, ffn2_w_up, ffn2_w_down, even_w_in, even_b_forget, even_w_out, odd_w_qkv, odd_w_out, final_norm_g, loss_target, m_norm_g, m_ffn1_w_gate, m_ffn1_w_up, m_ffn1_w_down, m_ffn2_w_gate, m_ffn2_w_up, m_ffn2_w_down, m_even_w_in, m_even_b_forget, m_even_w_out, m_odd_w_qkv, m_odd_w_out, m_final_norm_g, v_norm_g, v_ffn1_w_gate, v_ffn1_w_up, v_ffn1_w_down, v_ffn2_w_gate, v_ffn2_w_up, v_ffn2_w_down, v_even_w_in, v_even_b_forget, v_even_w_out, v_odd_w_qkv, v_odd_w_out, v_final_norm_g):
    s, d = x.shape[1], x.shape[2]
    nfox = N_HEADS - N_SB
    ax, ay, ac = lax.axis_index("x"), lax.axis_index("y"), lax.axis_index("c")
    me = 4 * ax + 2 * ay + ac
    slots = jnp.stack([4 * px + 2 * py + ac for px, py in [(ax, ay), (1 - ax, ay), (ax, 1 - ay), (1 - ax, 1 - ay)]]).astype(jnp.int32)
    x0 = x.reshape(s, d)
    target = loss_target.reshape(s, d)

    def bf(w):
        return w.astype(BF)

    groups = [
        [bf(ffn1_w_gate[0]), bf(ffn1_w_up[0]), bf(ffn1_w_down[0]), norm_g.reshape(6, d // NDEV)],
        [bf(even_w_in[0]), bf(even_w_out[0])],
        [bf(ffn2_w_gate[0]), bf(ffn2_w_up[0]), bf(ffn2_w_down[0])],
        [bf(ffn1_w_gate[1]), bf(ffn1_w_up[1]), bf(ffn1_w_down[1])],
        [bf(odd_w_qkv[0]), bf(odd_w_out[0])],
        [bf(ffn2_w_gate[1]), bf(ffn2_w_up[1]), bf(ffn2_w_down[1])],
    ]
    started = []
    for k, grp in enumerate(groups):
        started.append(_gather_start(grp, me, [started[-1]["token"]] if started else [], f"gather_start_{k}"))
    all_started = [started[-1]["token"]]

    def gathered(k, after):
        return _gather_finish(_gather_forward(started[k], after, f"gather_forward_{k}"), f"gather_finish_{k}")

    def ffn_weights(ws_):
        return {"gate": ws_[0], "up": ws_[1], "down": ws_[2]}

    b_pad = jnp.pad(even_b_forget, ((0, 0), (0, HD - nfox)))
    gfin = final_norm_g.reshape(1, d)

    g0 = gathered(0, x0)
    gn = jnp.transpose(g0[3], (1, 0, 2)).reshape(6, 1, d)
    wf = [[ffn_weights(g0), None], [None, None]]
    x1, sv_f1_0 = _swiglu_fwd(x0, gn[0], wf[0][0], "l0a", all_started)
    g1 = gathered(1, x1)
    w_in_nat = jnp.transpose(g1[0], (1, 0, 2)).reshape(d, -1)
    w_qkv_e = w_in_nat[:, :3 * d]
    w_f = jnp.pad(w_in_nat[:, 3 * d:], ((0, 0), (0, HD - nfox)))
    w_out_e = g1[1].reshape(d, d)
    h_e, ht_e = _rmsnorm_fwd(x1, gn[1], "norm_l0m")
    qkv_e = _mm_nn(h_e, w_qkv_e, 768, BF, "even_qkv")
    f_e = _mm_nn(h_e, w_f, HD, F32, "even_fgate")
    o_sb, ot_sb, tot_sb = _sb_fwd(qkv_e, "sb_fwd")
    cum_b, cum_t = _fgate_fwd(f_e, b_pad, "fgate_fwd")
    o_fox, ot_fox, lse_fox = _fox_fwd(qkv_e, cum_b, cum_t, "fox_fwd")
    o_e = jnp.concatenate([o_sb, o_fox], axis=1)
    ot_e = jnp.concatenate([ot_sb, ot_fox], axis=0)
    x2 = _mm_nn(o_e, w_out_e, 1024, F32, "even_out", res=x1)
    wf[0][1] = ffn_weights(gathered(2, x2))
    x3, sv_f2_0 = _swiglu_fwd(x2, gn[2], wf[0][1], "l0b")

    wf[1][0] = ffn_weights(gathered(3, x3))
    x4, sv_f1_1 = _swiglu_fwd(x3, gn[3], wf[1][0], "l1a")
    g4 = gathered(4, x4)
    w_qkv_o = g4[0]
    w_out_o = g4[1].reshape(d, d)
    h_o, ht_o = _rmsnorm_fwd(x4, gn[4], "norm_l1m")
    qkv_o = _qkv_rope(h_o, w_qkv_o, _rope_tables(s, 1.0), "odd_qkv")
    pats = [_dil_fwd(qkv_o, dil, f"dil_fwd_{dil}") for _, dil in DILATED_PATTERNS]
    o_o, ot_o, glse = _dil_combine([p[0] for p in pats], [p[1] for p in pats], "dil_combine")
    x5 = _mm_nn(o_o, w_out_o, 1024, F32, "odd_out", res=x4)
    wf[1][1] = ffn_weights(gathered(5, x5))
    x6, sv_f2_1 = _swiglu_fwd(x5, gn[5], wf[1][1], "l1b")

    def reduce_start(gs, tag):
        gs = [g_ if g_.ndim == 3 else g_.reshape(NDEV, g_.shape[0] // NDEV, g_.shape[1]) for g_ in gs]
        a_s = _pair_exchange(gs, f"pair_exchange_{tag}")
        ps = [_pair_sum(g_, a_, slots, f"pair_sum_{tag}_{k}") for k, (g_, a_) in enumerate(zip(gs, a_s))]
        return gs, a_s, _chip_start(ps, f"chip_start_{tag}")

    def ffn_grads(gw):
        return [gw["gate"], gw["up"], gw["down"]]

    dx6, dx6b, _, d_gfin, loss_part = _loss_head(x6, gfin, target, "loss_head")

    (dx5, dx5b, dx5bt), dgn5, gw_f2_1 = _swiglu_bwd(sv_f2_1, gn[5], wf[1][1], dx6, dx6b, 1.0, "l1b")
    red_l1b = reduce_start(ffn_grads(gw_f2_1), "l1b")
    d_wout_o = _mm_nn(ot_o, dx5b, 1024, BF, "odd_out_dw")
    do_o = _mm_nt([(dx5b, w_out_o)], "odd_out_do", deps=[red_l1b[2]["token"]]).astype(BF)
    inv = _rope_tables(s, -1.0)
    grads = [_dil_bwd(qkv_o, o_o, glse, do_o, dil, f"dil_bwd_{dil}") for _, dil in DILATED_PATTERNS]
    dq = _dil_grad_sum([g_[0] for g_ in grads], inv, "dil_dq")
    dk = _dil_grad_sum([g_[1] for g_ in grads], inv, "dil_dk")
    dv = _dil_grad_sum([g_[2] for g_ in grads], None, "dil_dv")
    dqkv_o = jnp.concatenate([dq, dk, dv], axis=1)
    (d_wqkv_o,) = _grad_cols(ht_o, [dqkv_o], [True], "odd_qkv_dw")
    dh_o = _dh_cols([dqkv_o], [w_qkv_o], [True], "odd_qkv_dh")
    dx4, dx4b, _, dgn4 = _rmsnorm_bwd(x4, gn[4], dh_o, dx5, 0.5, "norm_bwd_l1m")
    red_l1m = reduce_start([d_wqkv_o, d_wout_o], "l1m")
    (dx3, dx3b, _), dgn3, gw_f1_1 = _swiglu_bwd(sv_f1_1, gn[3], wf[1][0], dx4, dx4b, 0.5, "l1a", [red_l1m[2]["token"]])
    red_l1a = reduce_start(ffn_grads(gw_f1_1), "l1a")

    (dx2, dx2b, dx2bt), dgn2, gw_f2_0 = _swiglu_bwd(sv_f2_0, gn[2], wf[0][1], dx3, dx3b, 1.0, "l0b", [red_l1a[2]["token"]])
    red_l0b = reduce_start(ffn_grads(gw_f2_0), "l0b")
    d_wout_e = _mm_nn(ot_e, dx2b, 1024, BF, "even_out_dw")
    do_e = _mm_nt([(dx2b, w_out_e)], "even_out_do", deps=[red_l0b[2]["token"]]).astype(BF)
    dq_sb, dk_sb, dv_sb = _sb_bwd(qkv_e, do_e, tot_sb, "sb_bwd")
    dq_fx, dk_fx, dv_fx, dcq, dck = _fox_bwd(qkv_e, cum_b, cum_t, o_fox, lse_fox, do_e, "fox_bwd")
    df, db_part = _fgate_bwd(dcq, dck, f_e, b_pad, "fgate_bwd")
    dqkv_e = jnp.concatenate([dq_sb, dq_fx, dk_sb, dk_fx, dv_sb, dv_fx], axis=1)
    d_wqkv_e = _mm_nn(ht_e, dqkv_e, 768, BF, "even_qkv_dw")
    d_wf = _mm_nn(ht_e, df, HD, BF, "even_fgate_dw")
    dh_e = _mm_nt([(dqkv_e, w_qkv_e), (df, w_f)], "even_in_dh")
    dx1, dx1b, _, dgn1 = _rmsnorm_bwd(x1, gn[1], dh_e, dx2, 0.5, "norm_bwd_l0m")
    d_win_nat = jnp.concatenate([d_wqkv_e, d_wf[:, :nfox]], axis=1)
    d_win = jnp.transpose(d_win_nat.reshape(d, NDEV, -1), (1, 0, 2))
    red_l0m = reduce_start([d_win, d_wout_e], "l0m")
    (dx0, _, _), dgn0, gw_f1_0 = _swiglu_bwd(sv_f1_0, gn[0], wf[0][0], dx1, dx1b, 1.0, "l0a", [red_l0m[2]["token"]])
    red_l0a = reduce_start(ffn_grads(gw_f1_0), "l0a")

    def reduce_finish(red, tag):
        gs, a_s, st = red
        return list(zip(gs, a_s, _chip_finish(st, dx0, f"chip_finish_{tag}")))

    f_l1b, f_l1m, f_l1a = reduce_finish(red_l1b, "l1b"), reduce_finish(red_l1m, "l1m"), reduce_finish(red_l1a, "l1a")
    f_l0b, f_l0m, f_l0a = reduce_finish(red_l0b, "l0b"), reduce_finish(red_l0m, "l0m"), reduce_finish(red_l0a, "l0a")
    r0 = f_l0a + f_l0b + f_l0m
    r1 = f_l1a + f_l1b + f_l1m

    res = {}
    names = ["ffn1_w_gate", "ffn1_w_up", "ffn1_w_down", "ffn2_w_gate", "ffn2_w_up", "ffn2_w_down"]
    ws = [ffn1_w_gate, ffn1_w_up, ffn1_w_down, ffn2_w_gate, ffn2_w_up, ffn2_w_down]
    ms = [m_ffn1_w_gate, m_ffn1_w_up, m_ffn1_w_down, m_ffn2_w_gate, m_ffn2_w_up, m_ffn2_w_down]
    vs = [v_ffn1_w_gate, v_ffn1_w_up, v_ffn1_w_down, v_ffn2_w_gate, v_ffn2_w_up, v_ffn2_w_down]
    for k, nm in enumerate(names):
        res[nm] = _adamw_sharded(ws[k], ms[k], vs[k], [r0[k], r1[k]], slots, f"adamw_{nm}")
    res["even_w_in"] = _adamw_sharded(even_w_in, m_even_w_in, v_even_w_in, [r0[6]], slots, "adamw_even_w_in")
    res["even_w_out"] = _adamw_sharded(even_w_out, m_even_w_out, v_even_w_out, [r0[7]], slots, "adamw_even_w_out")
    res["odd_w_qkv"] = _adamw_sharded(odd_w_qkv, m_odd_w_qkv, v_odd_w_qkv, [r1[6]], slots, "adamw_odd_w_qkv")
    res["odd_w_out"] = _adamw_sharded(odd_w_out, m_odd_w_out, v_odd_w_out, [r1[7]], slots, "adamw_odd_w_out")

    dnorm = jnp.concatenate([dgn0, dgn1, dgn2, dgn3, dgn4, dgn5], axis=0)
    nsm = d // NDEV
    small_rows = (6 * d + d + 2 * HD) // HD
    pad_rows = -small_rows % 8
    part = jnp.concatenate([dnorm.reshape(-1), d_gfin.reshape(-1), db_part.reshape(-1), loss_part.reshape(-1),
                            jnp.zeros((pad_rows * HD,), F32)]).reshape(small_rows + pad_rows, HD)
    (gathered,) = _all_gather([part], "gather_small")

    def pack(ng, bfg, fg):
        full = lax.dynamic_update_slice(jnp.zeros((6, d), F32), ng.reshape(6, nsm), (0, me * nsm))
        return jnp.concatenate([full.reshape(-1), fg.reshape(-1), jnp.pad(bfg.reshape(-1), (0, HD - nfox)),
                                jnp.zeros((HD + pad_rows * HD,), F32)]).reshape(small_rows + pad_rows, HD)

    sm = _adamw_small(pack(norm_g, even_b_forget, final_norm_g), pack(m_norm_g, m_even_b_forget, m_final_norm_g),
                      pack(v_norm_g, v_even_b_forget, v_final_norm_g), gathered, "adamw_small")

    def unpack(t):
        flat = t.reshape(-1)
        ng = lax.dynamic_slice(flat[:6 * d].reshape(6, d), (0, me * nsm), (6, nsm)).reshape(norm_g.shape)
        fg = flat[6 * d:7 * d].reshape(final_norm_g.shape)
        bfg = flat[7 * d:7 * d + nfox].reshape(even_b_forget.shape)
        return ng, bfg, fg

    sm_g, sm_d, sm_m, sm_v = [unpack(t) for t in sm]
    loss = sm[0].reshape(-1)[7 * d + HD]

    order = ["norm_g", "ffn1_w_gate", "ffn1_w_up", "ffn1_w_down", "ffn2_w_gate", "ffn2_w_up", "ffn2_w_down", "even_w_in",
             "even_b_forget", "even_w_out", "odd_w_qkv", "odd_w_out", "final_norm_g"]
    outs = [loss, dx0.reshape(x.shape)]
    for k in range(4):
        smk = [sm_g, sm_d, sm_m, sm_v][k]
        for nm in order:
            if nm == "norm_g":
                outs.append(smk[0])
            elif nm == "even_b_forget":
                outs.append(smk[1])
            elif nm == "final_norm_g":
                outs.append(smk[2])
            else:
                outs.append(res[nm][k])
    return tuple(outs)
```

```python
import functools
import math

import jax
import jax.numpy as jnp
from jax import lax
from jax.experimental import pallas as pl
from jax.experimental.pallas import tpu as pltpu

F32 = jnp.float32
BF = jnp.bfloat16
NDEV = 8
HD = 128
QB = 128
N_HEADS = 16
N_SB = 8
SCALE = HD ** -0.5
ROPE_THETA = 500000.0
ROPE_DIMS = HD // 4
DILATED_PATTERNS = ((128, 1), (512, 4), (2048, 16))
RMS_EPS = 1e-6
NEG_INF = -1e30
ADAM_LR = 0.001
ADAM_B1 = 0.9
ADAM_B2 = 0.999
ADAM_EPS = 1e-08
ADAM_WD = 0.01
ADAM_STEP = 10
VMEM_LIMIT_V7X = 56 * 1024 * 1024
MESH = pl.DeviceIdType.MESH
ANY = pl.BlockSpec(memory_space=pl.ANY)

NT_DIMS = (((1,), (1,)), ((), ()))


def _cp(*dims):
    return pltpu.CompilerParams(dimension_semantics=dims if dims else None, vmem_limit_bytes=VMEM_LIMIT_V7X)


def _dot(a, b):
    return jnp.dot(a, b, preferred_element_type=F32)


def _dot_nt(a, b):
    return lax.dot_general(a, b, NT_DIMS, preferred_element_type=F32)


def _sds(shape, dtype):
    return jax.ShapeDtypeStruct(shape, dtype)


def _place():
    x, y, c = lax.axis_index("x"), lax.axis_index("y"), lax.axis_index("c")
    chips = [(x, y), (1 - x, y), (x, 1 - y), (1 - x, 1 - y)]
    return x, y, c, chips


def _all_gather(xs, name):
    n = len(xs)

    def body(*refs):
        x_refs, out_refs = refs[:n], refs[n:2 * n]
        send_sems, recv_sems, local_sems = refs[2 * n:]
        x, y, c, chips = _place()
        me, sibling = (x, y, c), (x, y, 1 - c)
        others = chips[1:]

        def slot(a, px, py, pc):
            return out_refs[a].at[4 * px + 2 * py + pc]

        def copy(a, k, block, to, src=None):
            return pltpu.make_async_remote_copy(
                src_ref=slot(a, *block) if src is None else src, dst_ref=slot(a, *block),
                send_sem=send_sems.at[a, k], recv_sem=recv_sems.at[a, k], device_id=to, device_id_type=MESH)

        started = []
        for a in range(n):
            mine = pltpu.make_async_copy(x_refs[a], slot(a, *me), local_sems.at[a])
            mine.start()
            first = [copy(a, 0, me, sibling, src=x_refs[a])]
            first += [copy(a, 1 + j, me, (*chip, c), src=x_refs[a]) for j, chip in enumerate(others)]
            for cp in first:
                cp.start()
            started += [mine.wait] + [cp.wait_send for cp in first]
        for a in range(n):
            for j, chip in enumerate(others):
                copy(a, 1 + j, (*chip, c), me).wait_recv()
                passed = copy(a, 4 + j, (*chip, c), sibling)
                passed.start()
                started.append(passed.wait_send)
        for a in range(n):
            copy(a, 0, sibling, me).wait_recv()
            for j, chip in enumerate(others):
                copy(a, 4 + j, (*chip, 1 - c), me).wait_recv()
        for w in started:
            w()

    return pl.pallas_call(
        body, name=name,
        out_shape=[_sds((NDEV,) + x.shape, x.dtype) for x in xs],
        in_specs=[ANY] * n, out_specs=[ANY] * n,
        scratch_shapes=[pltpu.SemaphoreType.DMA((n, 7)), pltpu.SemaphoreType.DMA((n, 7)), pltpu.SemaphoreType.DMA((n,))],
    )(*xs)


def _pair_exchange(gs, name):
    n = len(gs)

    def body(*refs):
        g_refs, a_refs = refs[:n], refs[n:2 * n]
        send_sems, recv_sems = refs[2 * n:]
        x, y, c, chips = _place()
        copies = []
        for a in range(n):
            for j, (px, py) in enumerate(chips):
                copies.append(pltpu.make_async_remote_copy(
                    src_ref=g_refs[a].at[4 * px + 2 * py + (1 - c)], dst_ref=a_refs[a].at[j],
                    send_sem=send_sems.at[a, j], recv_sem=recv_sems.at[a, j],
                    device_id=(x, y, 1 - c), device_id_type=MESH))
        for cp in copies:
            cp.start()
        for cp in copies:
            cp.wait()

    return pl.pallas_call(
        body, name=name,
        out_shape=[_sds((4,) + g.shape[1:], g.dtype) for g in gs],
        in_specs=[ANY] * n, out_specs=[ANY] * n,
        scratch_shapes=[pltpu.SemaphoreType.DMA((n, 4)), pltpu.SemaphoreType.DMA((n, 4))],
    )(*gs)


HBM = pl.BlockSpec(memory_space=pltpu.HBM)
SEM = pl.BlockSpec(memory_space=pltpu.SEMAPHORE)
EFFECT = pltpu.SideEffectType.DATAFLOW_SIDE_EFFECTING
TOKEN = _sds((8, 128), F32)
TOKEN_SPEC = pl.BlockSpec((8, 128), lambda *_: (0, 0))


def _in_hbm(x):
    return pltpu.with_memory_space_constraint(x, pltpu.HBM)


def _ignore_deps(body, n_in, n_deps):
    if not n_deps:
        return body
    return lambda *refs: body(*refs[:n_in], *refs[n_in + n_deps:])


def _slot_of(px, py, pc):
    return 4 * px + 2 * py + pc


def _gather_start(xs, me, deps, name):
    n = len(xs)
    lands = [lax.dynamic_update_slice(lax.empty((NDEV,) + x.shape, x.dtype), x[None], (me,) + (0,) * x.ndim) for x in xs]

    def body(*refs):
        x_refs, land_refs = refs[:n], refs[n:2 * n]
        send, recv_ici, recv_sib = refs[2 * n:2 * n + 3]
        token = refs[4 * n + 3]
        x, y, c, chips = _place()
        for a in range(n):
            dst = land_refs[a].at[_slot_of(x, y, c)]
            pltpu.make_async_remote_copy(src_ref=x_refs[a], dst_ref=dst, send_sem=send.at[4 * a], recv_sem=recv_sib.at[a],
                                         device_id=(x, y, 1 - c), device_id_type=MESH).start()
            for j, chip in enumerate(chips[1:]):
                pltpu.make_async_remote_copy(src_ref=x_refs[a], dst_ref=dst, send_sem=send.at[4 * a + 1 + j], recv_sem=recv_ici.at[3 * a + j],
                                             device_id=(*chip, c), device_id_type=MESH).start()
        token[...] = jnp.zeros_like(token)

    outs = pl.pallas_call(
        _ignore_deps(body, 2 * n, len(deps)), name=name,
        out_shape=(pltpu.SemaphoreType.DMA((4 * n,)), pltpu.SemaphoreType.DMA((3 * n,)), pltpu.SemaphoreType.DMA((n,)),
                   *[pltpu.HBM(x.shape, x.dtype) for x in xs], *[pltpu.HBM(l.shape, l.dtype) for l in lands], TOKEN),
        in_specs=[HBM] * (2 * n) + [TOKEN_SPEC] * len(deps),
        out_specs=(SEM, SEM, SEM, *[HBM] * (2 * n), pl.BlockSpec(memory_space=pltpu.VMEM)),
        input_output_aliases={a: 3 + a for a in range(2 * n)},
        compiler_params=pltpu.CompilerParams(has_side_effects=EFFECT),
    )(*[_in_hbm(x) for x in xs], *[_in_hbm(l) for l in lands], *deps)
    send, recv_ici, recv_sib = outs[:3]
    return dict(send=send, recv_ici=recv_ici, recv_sib=recv_sib, xs=list(outs[3:3 + n]), lands=list(outs[3 + n:3 + 2 * n]), token=outs[-1])


def _gather_forward(st, after, name):
    n = len(st["lands"])

    def body(*refs):
        land_refs, recv_ici = refs[:n], refs[n]
        send2, recv2 = refs[n + 2], refs[n + 3]
        x, y, c, chips = _place()
        for a in range(n):
            for j, chip in enumerate(chips[1:]):
                blk = land_refs[a].at[_slot_of(*chip, c)]
                pltpu.make_async_remote_copy(src_ref=blk, dst_ref=blk, send_sem=send2.at[3 * a + j], recv_sem=recv_ici.at[3 * a + j],
                                             device_id=(*chip, c), device_id_type=MESH).wait_recv()
                pltpu.make_async_remote_copy(src_ref=blk, dst_ref=blk, send_sem=send2.at[3 * a + j], recv_sem=recv2.at[3 * a + j],
                                             device_id=(x, y, 1 - c), device_id_type=MESH).start()

    outs = pl.pallas_call(
        body, name=name,
        out_shape=(pltpu.SemaphoreType.DMA((3 * n,)), pltpu.SemaphoreType.DMA((3 * n,)), *[pltpu.HBM(l.shape, l.dtype) for l in st["lands"]]),
        in_specs=[HBM] * n + [SEM, pl.BlockSpec(memory_space=pl.ANY)],
        out_specs=(SEM, SEM, *[HBM] * n),
        input_output_aliases={a: 2 + a for a in range(n)},
        compiler_params=pltpu.CompilerParams(has_side_effects=EFFECT),
    )(*st["lands"], st["recv_ici"], after)
    return dict(st, send2=outs[0], recv2=outs[1], lands=list(outs[2:]))


def _gather_finish(st, name):
    n = len(st["lands"])

    def body(*refs):
        x_refs, land_refs = refs[:n], refs[n:2 * n]
        send, recv_sib, send2, recv2 = refs[2 * n:2 * n + 4]
        x, y, c, chips = _place()
        for a in range(n):
            mine = land_refs[a].at[_slot_of(x, y, c)]
            theirs = land_refs[a].at[_slot_of(x, y, 1 - c)]
            for k in range(4):
                pltpu.make_async_remote_copy(src_ref=x_refs[a], dst_ref=mine, send_sem=send.at[4 * a + k], recv_sem=recv_sib.at[a],
                                             device_id=(x, y, 1 - c), device_id_type=MESH).wait_send()
            pltpu.make_async_remote_copy(src_ref=x_refs[a], dst_ref=theirs, send_sem=send.at[4 * a], recv_sem=recv_sib.at[a],
                                         device_id=(x, y, 1 - c), device_id_type=MESH).wait_recv()
            for j, chip in enumerate(chips[1:]):
                sent = land_refs[a].at[_slot_of(*chip, c)]
                got = land_refs[a].at[_slot_of(*chip, 1 - c)]
                pltpu.make_async_remote_copy(src_ref=sent, dst_ref=sent, send_sem=send2.at[3 * a + j], recv_sem=recv2.at[3 * a + j],
                                             device_id=(x, y, 1 - c), device_id_type=MESH).wait_send()
                pltpu.make_async_remote_copy(src_ref=got, dst_ref=got, send_sem=send2.at[3 * a + j], recv_sem=recv2.at[3 * a + j],
                                             device_id=(x, y, 1 - c), device_id_type=MESH).wait_recv()

    outs = pl.pallas_call(
        body, name=name,
        out_shape=tuple(pltpu.HBM(v.shape, v.dtype) for v in st["xs"] + st["lands"]),
        in_specs=[HBM] * (2 * n) + [SEM] * 4, out_specs=tuple([HBM] * (2 * n)),
        input_output_aliases={a: a for a in range(2 * n)},
        compiler_params=pltpu.CompilerParams(has_side_effects=EFFECT),
    )(*st["xs"], *st["lands"], st["send"], st["recv_sib"], st["send2"], st["recv2"])
    return list(outs[n:])


def _chip_start(ps, name):
    n = len(ps)
    lands = [lax.empty(p.shape, p.dtype) for p in ps]

    def body(*refs):
        p_refs, b_refs = refs[:n], refs[n:2 * n]
        send, recv = refs[2 * n], refs[2 * n + 1]
        token = refs[4 * n + 2]
        x, y, c, chips = _place()
        for a in range(n):
            for j, chip in enumerate(chips[1:]):
                pltpu.make_async_remote_copy(src_ref=p_refs[a].at[j], dst_ref=b_refs[a].at[j], send_sem=send.at[3 * a + j], recv_sem=recv.at[3 * a + j],
                                             device_id=(*chip, c), device_id_type=MESH).start()
        token[...] = jnp.zeros_like(token)

    outs = pl.pallas_call(
        body, name=name,
        out_shape=(pltpu.SemaphoreType.DMA((3 * n,)), pltpu.SemaphoreType.DMA((3 * n,)),
                   *[pltpu.HBM(p.shape, p.dtype) for p in ps], *[pltpu.HBM(p.shape, p.dtype) for p in ps], TOKEN),
        in_specs=[HBM] * (2 * n), out_specs=(SEM, SEM, *[HBM] * (2 * n), pl.BlockSpec(memory_space=pltpu.VMEM)),
        input_output_aliases={a: 2 + a for a in range(2 * n)},
        compiler_params=pltpu.CompilerParams(has_side_effects=EFFECT),
    )(*[_in_hbm(p) for p in ps], *[_in_hbm(l) for l in lands])
    return dict(send=outs[0], recv=outs[1], ps=list(outs[2:2 + n]), lands=list(outs[2 + n:2 + 2 * n]), token=outs[-1])


def _chip_finish(st, after, name):
    n = len(st["ps"])

    def body(*refs):
        p_refs, b_refs = refs[:n], refs[n:2 * n]
        send, recv = refs[2 * n], refs[2 * n + 1]
        x, y, c, chips = _place()
        for a in range(n):
            for j, chip in enumerate(chips[1:]):
                cp = pltpu.make_async_remote_copy(src_ref=p_refs[a].at[j], dst_ref=b_refs[a].at[j], send_sem=send.at[3 * a + j], recv_sem=recv.at[3 * a + j],
                                                  device_id=(*chip, c), device_id_type=MESH)
                cp.wait_send()
                cp.wait_recv()

    outs = pl.pallas_call(
        body, name=name,
        out_shape=tuple(pltpu.HBM(v.shape, v.dtype) for v in st["ps"] + st["lands"]),
        in_specs=[HBM] * (2 * n) + [SEM, SEM, pl.BlockSpec(memory_space=pl.ANY)], out_specs=tuple([HBM] * (2 * n)),
        input_output_aliases={a: a for a in range(2 * n)},
        compiler_params=pltpu.CompilerParams(has_side_effects=EFFECT),
    )(*st["ps"], *st["lands"], st["send"], st["recv"], after)
    return list(outs[n:])


def _rows_tile(r):
    for t in (512, 256, 128, 64, 32, 16):
        if r % t == 0:
            return t
    return r


def _pair_sum(g, a, slots, name):
    _, r, c = g.shape
    tr = _rows_tile(r)

    def body(slots_ref, g_ref, a_ref, p_ref):
        p_ref[...] = (g_ref[...].astype(F32) + a_ref[...].astype(F32)).astype(BF)

    return pl.pallas_call(
        body, name=name,
        grid_spec=pltpu.PrefetchScalarGridSpec(
            num_scalar_prefetch=1, grid=(3, r // tr),
            in_specs=[pl.BlockSpec((None, tr, c), lambda j, i, s: (s[j + 1], i, 0)),
                      pl.BlockSpec((None, tr, c), lambda j, i, s: (j + 1, i, 0))],
            out_specs=pl.BlockSpec((None, tr, c), lambda j, i, s: (j, i, 0))),
        out_shape=_sds((3, r, c), BF), compiler_params=_cp("parallel", "parallel"),
    )(slots, g, a)


def _adamw_math(w, g, m, v):
    m = ADAM_B1 * m + (1.0 - ADAM_B1) * g
    v = ADAM_B2 * v + (1.0 - ADAM_B2) * (g * g)
    m_hat = m / (1.0 - ADAM_B1 ** ADAM_STEP)
    v_hat = v / (1.0 - ADAM_B2 ** ADAM_STEP)
    delta = -ADAM_LR * (m_hat / (jnp.sqrt(v_hat) + ADAM_EPS) + ADAM_WD * w)
    return delta, m, v


def _adamw_sharded(w, m, v, parts, slots, name):
    nl, r, c = w.shape
    tr = _rows_tile(r)
    if c * tr * 4 > (1 << 21):
        tr = max(8, tr // 2)

    def body(slots_ref, w_ref, m_ref, v_ref, *rest):
        part_refs, (g_out, d_out, m_out, v_out) = rest[:5 * nl], rest[5 * nl:]
        layer = pl.program_id(0)
        g = None
        for l in range(nl):
            s = part_refs[5 * l][...].astype(F32)
            for ref in part_refs[5 * l + 1:5 * l + 5]:
                s = s + ref[...].astype(F32)
            g = s if g is None else jnp.where(layer == l, s, g)
        delta, mn, vn = _adamw_math(w_ref[...], g, m_ref[...], v_ref[...])
        g_out[...] = g
        d_out[...] = delta
        m_out[...] = mn
        v_out[...] = vn

    def own(l):
        return lambda L, i, s: (s[0], jnp.where(L == l, i, 0), 0)

    def fixed(l, k):
        return lambda L, i, s: (k, jnp.where(L == l, i, 0), 0)

    wspec = pl.BlockSpec((None, tr, c), lambda L, i, s: (L, i, 0))
    in_specs = [wspec, wspec, wspec]
    args = [w, m, v]
    for l, (g, a, b) in enumerate(parts):
        in_specs += [pl.BlockSpec((None, tr, c), own(l)), pl.BlockSpec((None, tr, c), fixed(l, 0)),
                     pl.BlockSpec((None, tr, c), fixed(l, 0)), pl.BlockSpec((None, tr, c), fixed(l, 1)),
                     pl.BlockSpec((None, tr, c), fixed(l, 2))]
        args += [g, a, b, b, b]
    return pl.pallas_call(
        body, name=name,
        grid_spec=pltpu.PrefetchScalarGridSpec(
            num_scalar_prefetch=1, grid=(nl, r // tr), in_specs=in_specs, out_specs=[wspec] * 4),
        out_shape=[_sds(w.shape, F32)] * 4, compiler_params=_cp("arbitrary", "arbitrary"),
    )(slots, *args)


def _adamw_small(w, m, v, gathered, name):
    def body(w_ref, m_ref, v_ref, gg_ref, g_out, d_out, m_out, v_out):
        g = gg_ref[0]
        for k in range(1, NDEV):
            g = g + gg_ref[k]
        delta, mn, vn = _adamw_math(w_ref[...], g, m_ref[...], v_ref[...])
        g_out[...] = g
        d_out[...] = delta
        m_out[...] = mn
        v_out[...] = vn

    return pl.pallas_call(body, name=name, out_shape=[_sds(w.shape, F32)] * 4)(w, m, v, gathered)


def _rmsnorm_fwd(x, g, name, deps=()):
    s, d = x.shape
    tm = 256

    def body(x_ref, g_ref, h_ref, ht_ref):
        xf = x_ref[...]
        y = xf * lax.rsqrt(jnp.mean(xf * xf, axis=-1, keepdims=True) + RMS_EPS)
        h = y * g_ref[...]
        h_ref[...] = h.astype(BF)
        ht_ref[...] = h.T.astype(BF)

    return pl.pallas_call(
        _ignore_deps(body, 2, len(deps)), name=name, grid=(s // tm,),
        in_specs=[pl.BlockSpec((tm, d), lambda i: (i, 0)), pl.BlockSpec((1, d), lambda i: (0, 0))] + [TOKEN_SPEC] * len(deps),
        out_specs=[pl.BlockSpec((tm, d), lambda i: (i, 0)), pl.BlockSpec((d, tm), lambda i: (0, i))],
        out_shape=[_sds((s, d), BF), _sds((d, s), BF)], compiler_params=_cp("parallel"),
    )(x, g, *deps)


def _rmsnorm_bwd(x, g, dh, dres, out_scale, name):
    s, d = x.shape
    tm = 256

    def body(x_ref, g_ref, dh_ref, dres_ref, dx_ref, dxb_ref, dxbt_ref, dg_ref):
        xf = x_ref[...]
        r = lax.rsqrt(jnp.mean(xf * xf, axis=-1, keepdims=True) + RMS_EPS)
        xhat = xf * r
        dhv = dh_ref[...]
        dxhat = dhv * g_ref[...]
        dx = dres_ref[...] + r * (dxhat - xhat * jnp.mean(dxhat * xhat, axis=-1, keepdims=True))
        dx_ref[...] = dx
        scaled = dx * out_scale
        dxb_ref[...] = scaled.astype(BF)
        dxbt_ref[...] = scaled.T.astype(BF)

        @pl.when(pl.program_id(0) == 0)
        def _():
            dg_ref[...] = jnp.zeros_like(dg_ref)

        dg_ref[...] += jnp.sum(dhv * xhat, axis=0, keepdims=True)

    row = pl.BlockSpec((tm, d), lambda i: (i, 0))
    vec = pl.BlockSpec((1, d), lambda i: (0, 0))
    return pl.pallas_call(
        body, name=name, grid=(s // tm,),
        in_specs=[row, vec, row, row],
        out_specs=[row, row, pl.BlockSpec((d, tm), lambda i: (0, i)), vec],
        out_shape=[_sds((s, d), F32), _sds((s, d), BF), _sds((d, s), BF), _sds((1, d), F32)],
        compiler_params=_cp("arbitrary"),
    )(x, g, dh, dres)


def _loss_head(x, g, target, name):
    s, d = x.shape
    tm = 256

    def body(x_ref, g_ref, t_ref, dx_ref, dxb_ref, dxbt_ref, dg_ref, loss_ref):
        xf = x_ref[...]
        r = lax.rsqrt(jnp.mean(xf * xf, axis=-1, keepdims=True) + RMS_EPS)
        xhat = xf * r
        err = xhat * g_ref[...] - t_ref[...]
        dy = err * (1.0 / d)
        dxhat = dy * g_ref[...]
        dx = r * (dxhat - xhat * jnp.mean(dxhat * xhat, axis=-1, keepdims=True))
        dx_ref[...] = dx
        half = dx * 0.5
        dxb_ref[...] = half.astype(BF)
        dxbt_ref[...] = half.T.astype(BF)

        @pl.when(pl.program_id(0) == 0)
        def _():
            dg_ref[...] = jnp.zeros_like(dg_ref)
            loss_ref[...] = jnp.zeros_like(loss_ref)

        dg_ref[...] += jnp.sum(dy * xhat, axis=0, keepdims=True)
        part = 0.5 * jnp.sum(jnp.mean(err * err, axis=-1, keepdims=True), axis=0, keepdims=True)
        lane = lax.broadcasted_iota(jnp.int32, (1, 128), 1)
        loss_ref[...] += jnp.where(lane == 0, part, 0.0)

    row = pl.BlockSpec((tm, d), lambda i: (i, 0))
    vec = pl.BlockSpec((1, d), lambda i: (0, 0))
    return pl.pallas_call(
        body, name=name, grid=(s // tm,),
        in_specs=[row, vec, row],
        out_specs=[row, row, pl.BlockSpec((d, tm), lambda i: (0, i)), vec, pl.BlockSpec((1, 128), lambda i: (0, 0))],
        out_shape=[_sds((s, d), F32), _sds((s, d), BF), _sds((d, s), BF), _sds((1, d), F32), _sds((1, 128), F32)],
        compiler_params=_cp("arbitrary"),
    )(x, g, target)


def _act_spec(tm, n, natural, order):
    if natural:
        return pl.BlockSpec((tm, n), (lambda s, i: (i, s)) if order == "si" else (lambda i, s: (i, s)))
    return pl.BlockSpec((None, tm, n), (lambda s, i: (s, i, 0)) if order == "si" else (lambda i, s: (s, i, 0)))


def _act_shape(s, n, natural, dtype):
    return _sds((s, NDEV * n), dtype) if natural else _sds((NDEV, s, n), dtype)


def _ffn_up(h, wg, wu, name):
    s, d = h.shape
    n = wg.shape[2]
    tm = 512

    def body(h_ref, wg_ref, wu_ref, g_ref, u_ref, a_ref):
        hb = h_ref[...]
        g = _dot(hb, wg_ref[...])
        u = _dot(hb, wu_ref[...])
        g_ref[...] = g.astype(BF)
        u_ref[...] = u.astype(BF)
        a_ref[...] = (g * jax.nn.sigmoid(g) * u).astype(BF)

    wsp = pl.BlockSpec((None, d, n), lambda s_, i: (s_, 0, 0))
    blk = _act_spec(tm, n, False, "si")
    return pl.pallas_call(
        body, name=name, grid=(NDEV, s // tm),
        in_specs=[pl.BlockSpec((tm, d), lambda s_, i: (i, 0)), wsp, wsp],
        out_specs=[blk] * 3, out_shape=[_act_shape(s, n, False, BF)] * 3,
        compiler_params=_cp("parallel", "parallel"),
    )(h, wg, wu)


def _ffn_down(act, wd, x, name):
    _, s, n = act.shape
    d = wd.shape[2]
    tm = 512

    def body(a_ref, w_ref, x_ref, o_ref, acc):
        k = pl.program_id(1)

        @pl.when(k == 0)
        def _():
            acc[...] = jnp.zeros_like(acc)

        acc[...] += _dot(a_ref[...], w_ref[...])

        @pl.when(k == NDEV - 1)
        def _():
            o_ref[...] = x_ref[...] + 0.5 * acc[...]

    row = pl.BlockSpec((tm, d), lambda i, k: (i, 0))
    return pl.pallas_call(
        body, name=name, grid=(s // tm, NDEV),
        in_specs=[_act_spec(tm, n, False, "is"), pl.BlockSpec((None, n, d), lambda i, k: (k, 0, 0)), row],
        out_specs=row, out_shape=_sds((s, d), F32),
        scratch_shapes=[pltpu.VMEM((tm, d), F32)], compiler_params=_cp("parallel", "arbitrary"),
    )(act, wd, x)


def _ffn_bwd_act(dyb, wd, g, u, name, deps=()):
    s, d = dyb.shape
    n = wd.shape[1]
    tm = 512

    def body(dy_ref, w_ref, g_ref, u_ref, dg_ref, du_ref):
        dact = _dot_nt(dy_ref[...], w_ref[...])
        gv = g_ref[...].astype(F32)
        uv = u_ref[...].astype(F32)
        sig = jax.nn.sigmoid(gv)
        dg_ref[...] = (dact * uv * (sig * (1.0 + gv * (1.0 - sig)))).astype(BF)
        du_ref[...] = (dact * (gv * sig)).astype(BF)

    blk = _act_spec(tm, n, False, "si")
    return pl.pallas_call(
        _ignore_deps(body, 4, len(deps)), name=name, grid=(NDEV, s // tm),
        in_specs=[pl.BlockSpec((tm, d), lambda s_, i: (i, 0)), pl.BlockSpec((None, n, d), lambda s_, i: (s_, 0, 0)), blk, blk]
        + [TOKEN_SPEC] * len(deps),
        out_specs=[blk, blk], out_shape=[_act_shape(s, n, False, BF)] * 2,
        compiler_params=_cp("parallel", "parallel"),
    )(dyb, wd, g, u, *deps)


def _grad_rows(act_t, dyb, name):
    _, n, s = act_t.shape
    d = dyb.shape[1]
    tn = 1024

    def body(a_ref, dy_ref, o_ref):
        o_ref[...] = _dot(a_ref[...], dy_ref[...]).astype(BF)

    return pl.pallas_call(
        body, name=name, grid=(NDEV, d // tn),
        in_specs=[pl.BlockSpec((None, n, s), lambda k, j: (k, 0, 0)), pl.BlockSpec((s, tn), lambda k, j: (0, j))],
        out_specs=pl.BlockSpec((None, n, tn), lambda k, j: (k, 0, j)), out_shape=_sds((NDEV, n, d), BF),
        compiler_params=_cp("parallel", "parallel"),
    )(act_t, dyb)


def _transpose_blocked(act, name):
    _, s, n = act.shape
    tm = 512

    def body(a_ref, o_ref):
        o_ref[...] = a_ref[...].astype(F32).T.astype(BF)

    return pl.pallas_call(
        body, name=name, grid=(NDEV, s // tm),
        in_specs=[pl.BlockSpec((None, tm, n), lambda k, i: (k, i, 0))],
        out_specs=pl.BlockSpec((None, n, tm), lambda k, i: (k, 0, i)), out_shape=_sds((NDEV, n, s), BF),
        compiler_params=_cp("parallel", "parallel"),
    )(act)


def _grad_cols(ht, dxs, naturals, name):
    d, s = ht.shape
    k = len(dxs)
    ns = [dx.shape[1] // NDEV if nat else dx.shape[2] for dx, nat in zip(dxs, naturals)]
    td = 512

    def body(*refs):
        ht_ref, dx_refs, o_refs = refs[0], refs[1:1 + k], refs[1 + k:]
        hv = ht_ref[...]
        for dx_ref, o_ref in zip(dx_refs, o_refs):
            o_ref[...] = _dot(hv, dx_ref[...]).astype(BF)

    def dx_spec(n, nat):
        if nat:
            return pl.BlockSpec((s, n), lambda s_, j: (0, s_))
        return pl.BlockSpec((None, s, n), lambda s_, j: (s_, 0, 0))

    return pl.pallas_call(
        body, name=name, grid=(NDEV, d // td),
        in_specs=[pl.BlockSpec((td, s), lambda s_, j: (j, 0))] + [dx_spec(n, nat) for n, nat in zip(ns, naturals)],
        out_specs=[pl.BlockSpec((None, td, n), lambda s_, j: (s_, j, 0)) for n in ns],
        out_shape=[_sds((NDEV, d, n), BF) for n in ns], compiler_params=_cp("parallel", "parallel"),
    )(ht, *dxs)


def _dh_cols(dxs, ws, naturals, name):
    k = len(dxs)
    d = ws[0].shape[1]
    ns = [w.shape[2] for w in ws]
    s = dxs[0].shape[0] if naturals[0] else dxs[0].shape[1]
    tm = 512

    def body(*refs):
        dx_refs, w_refs, o_ref, acc = refs[:k], refs[k:2 * k], refs[2 * k], refs[2 * k + 1]
        j = pl.program_id(1)

        @pl.when(j == 0)
        def _():
            acc[...] = jnp.zeros_like(acc)

        t = _dot_nt(dx_refs[0][...], w_refs[0][...])
        for dx_ref, w_ref in zip(dx_refs[1:], w_refs[1:]):
            t = t + _dot_nt(dx_ref[...], w_ref[...])
        acc[...] += t

        @pl.when(j == NDEV - 1)
        def _():
            o_ref[...] = acc[...]

    return pl.pallas_call(
        body, name=name, grid=(s // tm, NDEV),
        in_specs=[_act_spec(tm, n, nat, "is") for n, nat in zip(ns, naturals)]
        + [pl.BlockSpec((None, d, n), lambda i, j: (j, 0, 0)) for n in ns],
        out_specs=pl.BlockSpec((tm, d), lambda i, j: (i, 0)), out_shape=_sds((s, d), F32),
        scratch_shapes=[pltpu.VMEM((tm, d), F32)], compiler_params=_cp("parallel", "arbitrary"),
    )(*dxs, *ws)


def _mm_nn(a, b, tn, out_dtype, name, res=None, tm=512):
    m, k = a.shape
    nn = b.shape[1]

    def body(*refs):
        if res is None:
            a_ref, b_ref, o_ref = refs
            o_ref[...] = _dot(a_ref[...], b_ref[...]).astype(out_dtype)
        else:
            a_ref, b_ref, r_ref, o_ref = refs
            o_ref[...] = (r_ref[...] + _dot(a_ref[...], b_ref[...])).astype(out_dtype)

    osp = pl.BlockSpec((tm, tn), lambda j, i: (i, j))
    in_specs = [pl.BlockSpec((tm, k), lambda j, i: (i, 0)), pl.BlockSpec((k, tn), lambda j, i: (0, j))]
    args = [a, b]
    if res is not None:
        in_specs.append(osp)
        args.append(res)
    return pl.pallas_call(
        body, name=name, grid=(nn // tn, m // tm), in_specs=in_specs, out_specs=osp,
        out_shape=_sds((m, nn), out_dtype), compiler_params=_cp("parallel", "parallel"),
    )(*args)


def _mm_nt(pairs, name, out_dtype=F32, tm=512, tk=512, deps=()):
    m = pairs[0][0].shape[0]
    kk = pairs[0][1].shape[0]
    p = len(pairs)

    def body(*refs):
        o_ref = refs[2 * p]
        t = _dot_nt(refs[0][...], refs[1][...])
        for q in range(1, p):
            t = t + _dot_nt(refs[2 * q][...], refs[2 * q + 1][...])
        o_ref[...] = t.astype(out_dtype)

    in_specs, args = [], []
    for a, b in pairs:
        in_specs += [pl.BlockSpec((tm, a.shape[1]), lambda j, i: (i, 0)), pl.BlockSpec((tk, b.shape[1]), lambda j, i: (j, 0))]
        args += [a, b]
    return pl.pallas_call(
        _ignore_deps(body, 2 * p, len(deps)), name=name, grid=(kk // tk, m // tm), in_specs=in_specs + [TOKEN_SPEC] * len(deps),
        out_specs=pl.BlockSpec((tm, tk), lambda j, i: (i, j)), out_shape=_sds((m, kk), out_dtype),
        compiler_params=_cp("parallel", "parallel"),
    )(*args, *deps)


def _rope_tables(s, sign):
    half = ROPE_DIMS // 2
    freqs = ROPE_THETA ** (-jnp.arange(half, dtype=F32) / half)
    ang = jnp.arange(s, dtype=F32)[:, None] * freqs[None, :]
    cos, sin = jnp.cos(ang), sign * jnp.sin(ang)
    one = jnp.ones((s, HD - ROPE_DIMS), F32)
    zero = jnp.zeros((s, HD - ROPE_DIMS), F32)
    zh = jnp.zeros((s, half), F32)
    c = jnp.concatenate([cos, cos, one], axis=1)
    sa = jnp.concatenate([-sin, zh, zero], axis=1)
    sb = jnp.concatenate([zh, sin, zero], axis=1)
    return c, sa, sb


def _rope(xv, c, sa, sb):
    return xv * c + pltpu.roll(xv, HD - ROPE_DIMS // 2, 1) * sa + pltpu.roll(xv, ROPE_DIMS // 2, 1) * sb


def _qkv_rope(h, w, tables, name):
    s, d = h.shape
    n = w.shape[2]
    per = n // HD
    tm = 512

    def body(h_ref, w_ref, c_ref, sa_ref, sb_ref, o_ref):
        shard = pl.program_id(0)
        y = _dot(h_ref[...], w_ref[...])
        c, sa, sb = c_ref[...], sa_ref[...], sb_ref[...]
        for j in range(per):
            blk = y[:, j * HD:(j + 1) * HD]
            rot = _rope(blk, c, sa, sb)
            is_qk = shard * per + j < 2 * N_HEADS
            o_ref[:, j * HD:(j + 1) * HD] = jnp.where(is_qk, rot, blk).astype(BF)

    tab = pl.BlockSpec((tm, HD), lambda s_, i: (i, 0))
    return pl.pallas_call(
        body, name=name, grid=(NDEV, s // tm),
        in_specs=[pl.BlockSpec((tm, d), lambda s_, i: (i, 0)), pl.BlockSpec((None, d, n), lambda s_, i: (s_, 0, 0)), tab, tab, tab],
        out_specs=pl.BlockSpec((tm, n), lambda s_, i: (i, s_)), out_shape=_sds((s, NDEV * n), BF),
        compiler_params=_cp("parallel", "parallel"),
    )(h, w, *tables)


def _iota2():
    return (lax.broadcasted_iota(jnp.int32, (QB, QB), 0), lax.broadcasted_iota(jnp.int32, (QB, QB), 1))


def _softplus(z):
    return jnp.maximum(z, 0.0) + jnp.log(1.0 + jnp.exp(-jnp.abs(z)))


def _tri_dot(xv, tri, left=False):
    hi = xv.astype(BF)
    r1 = xv - hi.astype(F32)
    mid = r1.astype(BF)
    lo = (r1 - mid.astype(F32)).astype(BF)
    if left:
        return _dot(tri, hi) + _dot(tri, mid) + _dot(tri, lo)
    return _dot(hi, tri) + _dot(mid, tri) + _dot(lo, tri)


def _col(ref_or_val):
    return ref_or_val[:, 0:1]


KT = 4 * QB


def _iota_tile():
    return (lax.broadcasted_iota(jnp.int32, (QB, KT), 0), lax.broadcasted_iota(jnp.int32, (QB, KT), 1))


def _scan_matrix(keep):
    tri = keep(*_iota2()).astype(BF)
    return jnp.concatenate([tri, tri], axis=0)


def _scan_dot(xv, tri2):
    hi = xv.astype(BF)
    lo = (xv - hi.astype(F32)).astype(BF)
    return _dot(jnp.concatenate([hi, lo], axis=1), tri2)


def _blocks(xv):
    return [xv[:, b * QB:(b + 1) * QB] for b in range(KT // QB)]


def _sb_fwd(qkv, name):
    s = qkv.shape[0]
    nb = s // QB

    def body(q_ref, k_ref, v_ref, o_ref, ot_ref, t_ref):
        i = pl.program_id(1)
        q = q_ref[...]
        row, col = _iota_tile()
        later_keys = _scan_matrix(lambda j, s_: j > s_)
        last = i // (KT // QB)

        def step(tt, carry):
            acc, later = carry
            t = last - tt
            off = pl.multiple_of(t * KT, KT)
            k = k_ref[pl.ds(off, KT), :]
            v = v_ref[pl.ds(off, KT), :]
            z = _dot_nt(q, k) * SCALE
            strict = row + (i * QB - t * KT) > col
            sp = _softplus(z)
            lnb = jnp.where(strict, -sp, 0.0)
            afters = []
            for xb in reversed(_blocks(lnb)):
                afters.append(later + _scan_dot(xb, later_keys))
                later = later + jnp.sum(xb, axis=1, keepdims=True)
            after = jnp.concatenate(afters[::-1], axis=1)
            w = jnp.where(strict, jnp.exp((z - sp) + after), 0.0)
            return acc + _dot(w.astype(BF), v), later

        acc, total = lax.fori_loop(0, last + 1, step, (jnp.zeros((QB, HD), F32), jnp.zeros((QB, 1), F32)))
        o_ref[...] = acc.astype(BF)
        ot_ref[...] = acc.T.astype(BF)
        t_ref[...] = jnp.broadcast_to(total, (QB, HD))

    blk = pl.BlockSpec((QB, HD), lambda h, i: (i, h))
    return pl.pallas_call(
        body, name=name, grid=(N_SB, nb),
        in_specs=[blk, pl.BlockSpec((s, HD), lambda h, i: (0, N_HEADS + h)), pl.BlockSpec((s, HD), lambda h, i: (0, 2 * N_HEADS + h))],
        out_specs=[blk, pl.BlockSpec((HD, QB), lambda h, i: (h, i)), blk],
        out_shape=[_sds((s, N_SB * HD), BF), _sds((N_SB * HD, s), BF), _sds((s, N_SB * HD), F32)],
        compiler_params=_cp("parallel", "parallel"),
    )(qkv, qkv, qkv)


def _sb_bwd(qkv, do, total, name):
    s = qkv.shape[0]
    nb = s // QB

    def body(q_ref, k_ref, v_ref, do_ref, t_ref, dq_ref, dk_ref, dv_ref, dk_acc, dv_acc):
        i = pl.program_id(1)

        @pl.when(i == 0)
        def _():
            dk_acc[...] = jnp.zeros_like(dk_acc)
            dv_acc[...] = jnp.zeros_like(dv_acc)

        q = q_ref[...]
        dov = do_ref[...]
        tot = _col(t_ref[...])
        row, col = _iota_tile()
        keys_upto = _scan_matrix(lambda j, s_: j <= s_)
        keys_before = _scan_matrix(lambda j, s_: j < s_)

        def step(t, carry):
            dq, lnb_before, dl_before = carry
            off = pl.multiple_of(t * KT, KT)
            k = k_ref[pl.ds(off, KT), :]
            v = v_ref[pl.ds(off, KT), :]
            z = _dot_nt(q, k) * SCALE
            strict = row + (i * QB - t * KT) > col
            sp = _softplus(z)
            lnb = jnp.where(strict, -sp, 0.0)
            afters = []
            for xb in _blocks(lnb):
                afters.append(tot - (lnb_before + _scan_dot(xb, keys_upto)))
                lnb_before = lnb_before + jnp.sum(xb, axis=1, keepdims=True)
            a = jnp.where(strict, jnp.exp((z - sp) + jnp.concatenate(afters, axis=1)), 0.0)
            dl = a * _dot_nt(dov, v)
            befores = []
            for xb in _blocks(dl):
                befores.append(dl_before + _scan_dot(xb, keys_before))
                dl_before = dl_before + jnp.sum(xb, axis=1, keepdims=True)
            sig = jnp.exp(z - sp)
            dz = jnp.where(strict, dl * (1.0 - sig) - sig * jnp.concatenate(befores, axis=1), 0.0) * SCALE
            dq = dq + _dot(dz.astype(BF), k)
            dk_acc[pl.ds(off, KT), :] += _dot(dz.T.astype(BF), q)
            dv_acc[pl.ds(off, KT), :] += _dot(a.T.astype(BF), dov)
            return dq, lnb_before, dl_before

        zero = jnp.zeros((QB, 1), F32)
        dq, _, _ = lax.fori_loop(0, i // (KT // QB) + 1, step, (jnp.zeros((QB, HD), F32), zero, zero))
        dq_ref[...] = dq.astype(BF)

        @pl.when(i == nb - 1)
        def _():
            dk_ref[...] = dk_acc[...].astype(BF)
            dv_ref[...] = dv_acc[...].astype(BF)

    blk = pl.BlockSpec((QB, HD), lambda h, i: (i, h))
    full = pl.BlockSpec((s, HD), lambda h, i: (0, h))
    return pl.pallas_call(
        body, name=name, grid=(N_SB, nb),
        in_specs=[blk, pl.BlockSpec((s, HD), lambda h, i: (0, N_HEADS + h)), pl.BlockSpec((s, HD), lambda h, i: (0, 2 * N_HEADS + h)), blk, blk],
        out_specs=[blk, full, full], out_shape=[_sds((s, N_SB * HD), BF)] * 3,
        scratch_shapes=[pltpu.VMEM((s, HD), F32), pltpu.VMEM((s, HD), F32)],
        compiler_params=_cp("parallel", "arbitrary"),
    )(qkv, qkv, qkv, do, total)


def _fgate_fwd(f, b, name):
    s = f.shape[0]
    nb = s // QB
    nfox = N_HEADS - N_SB

    def body(f_ref, b_ref, cb_ref, ct_ref):
        row, col = _iota2()
        upto = (row >= col).astype(BF)
        carry = jnp.zeros((1, HD), F32)
        for blk in range(nb):
            xv = f_ref[blk * QB:(blk + 1) * QB, :] + b_ref[...]
            logf = -_softplus(-xv)
            cum = _tri_dot(logf, upto, left=True) + carry
            carry = cum[QB - 1:QB, :]
            ct_ref[blk] = cum.T
            for h in range(nfox):
                cb_ref[blk * QB:(blk + 1) * QB, h * HD:(h + 1) * HD] = jnp.broadcast_to(cum[:, h:h + 1], (QB, HD))

    return pl.pallas_call(
        body, name=name, out_shape=[_sds((s, nfox * HD), F32), _sds((nb, HD, HD), F32)], compiler_params=_cp(),
    )(f, b)


def _fgate_bwd(dcq, dck, f, b, name):
    s = f.shape[0]
    nb = s // QB
    nfox = N_HEADS - N_SB

    def body(dcq_ref, dck_ref, f_ref, b_ref, df_ref, db_ref):
        row, col = _iota2()
        from_tri = (row <= col).astype(BF)
        lane = col
        carry = jnp.zeros((1, HD), F32)
        db = jnp.zeros((1, HD), F32)
        for blk in reversed(range(nb)):
            dcum = jnp.zeros((QB, HD), F32)
            for h in range(nfox):
                here = (slice(blk * QB, (blk + 1) * QB), slice(h * HD, (h + 1) * HD))
                dcum = jnp.where(lane == h, dcq_ref[here] - dck_ref[here], dcum)
            dlogf = _tri_dot(dcum, from_tri, left=True) + carry
            carry = dlogf[0:1, :]
            xv = f_ref[blk * QB:(blk + 1) * QB, :] + b_ref[...]
            sp = _softplus(xv)
            df = jnp.where(lane < nfox, dlogf * jnp.exp(-sp), 0.0)
            df_ref[blk * QB:(blk + 1) * QB, :] = df.astype(BF)
            db = db + jnp.sum(df, axis=0, keepdims=True)
        db_ref[...] = db

    return pl.pallas_call(
        body, name=name, out_shape=[_sds((s, HD), BF), _sds((1, HD), F32)], compiler_params=_cp(),
    )(dcq, dck, f, b)


def _fox_head_row(ct_ref, j, h):
    tile = ct_ref[j]
    sub = lax.broadcasted_iota(jnp.int32, tile.shape, 0)
    return jnp.sum(jnp.where(sub == h, tile, 0.0), axis=0, keepdims=True)


def _fox_tile_row(ct_ref, t, h):
    nsub = KT // QB
    return jnp.concatenate([_fox_head_row(ct_ref, t * nsub + b, h) for b in range(nsub)], axis=1)


def _fox_fwd(qkv, cum_b, cum_t, name):
    s = qkv.shape[0]
    nb = s // QB
    nfox = N_HEADS - N_SB

    def body(q_ref, k_ref, v_ref, cq_ref, ct_ref, o_ref, ot_ref, lse_ref):
        h, i = pl.program_id(0), pl.program_id(1)
        q = q_ref[...]
        cq = _col(cq_ref[...])
        row, col = _iota_tile()

        def step(t, carry):
            acc, m, l = carry
            off = pl.multiple_of(t * KT, KT)
            k = k_ref[pl.ds(off, KT), :]
            v = v_ref[pl.ds(off, KT), :]
            z = _dot_nt(q, k) * SCALE + cq - _fox_tile_row(ct_ref, t, h)
            z = jnp.where(row + (i * QB - t * KT) >= col, z, NEG_INF)
            m_new = jnp.maximum(m, jnp.max(z, axis=1, keepdims=True))
            alpha = jnp.exp(m - m_new)
            p = jnp.exp(z - m_new)
            l = alpha * l + jnp.sum(p, axis=1, keepdims=True)
            acc = alpha * acc + _dot(p.astype(BF), v)
            return acc, m_new, l

        acc, m, l = lax.fori_loop(0, i // (KT // QB) + 1, step,
                                  (jnp.zeros((QB, HD), F32), jnp.full((QB, 1), NEG_INF, F32), jnp.zeros((QB, 1), F32)))
        o = acc / l
        o_ref[...] = o.astype(BF)
        ot_ref[...] = o.T.astype(BF)
        lse_ref[...] = jnp.broadcast_to(m + jnp.log(l), (QB, HD))

    blk = pl.BlockSpec((QB, HD), lambda h, i: (i, h))
    return pl.pallas_call(
        body, name=name, grid=(nfox, nb),
        in_specs=[pl.BlockSpec((QB, HD), lambda h, i: (i, N_SB + h)),
                  pl.BlockSpec((s, HD), lambda h, i: (0, N_HEADS + N_SB + h)),
                  pl.BlockSpec((s, HD), lambda h, i: (0, 2 * N_HEADS + N_SB + h)),
                  blk, pl.BlockSpec((nb, 8, HD), lambda h, i: (0, 0, 0))],
        out_specs=[blk, pl.BlockSpec((HD, QB), lambda h, i: (h, i)), blk],
        out_shape=[_sds((s, nfox * HD), BF), _sds((nfox * HD, s), BF), _sds((s, nfox * HD), F32)],
        compiler_params=_cp("parallel", "parallel"),
    )(qkv, qkv, qkv, cum_b, cum_t)


def _fox_bwd(qkv, cum_b, cum_t, o, lse, do, name):
    s = qkv.shape[0]
    nb = s // QB
    nfox = N_HEADS - N_SB

    def body(q_ref, k_ref, v_ref, cq_ref, ct_ref, o_ref, lse_ref, do_ref, dq_ref, dk_ref, dv_ref, dcq_ref, dc_ref, dk_acc, dv_acc, dc_acc):
        h, i = pl.program_id(0), pl.program_id(1)

        @pl.when(i == 0)
        def _():
            dk_acc[...] = jnp.zeros_like(dk_acc)
            dv_acc[...] = jnp.zeros_like(dv_acc)
            dc_acc[...] = jnp.zeros_like(dc_acc)

        q = q_ref[...]
        cq = _col(cq_ref[...])
        dov = do_ref[...]
        lse_c = _col(lse_ref[...])
        delta = jnp.sum(dov.astype(F32) * o_ref[...].astype(F32), axis=1, keepdims=True)
        row, col = _iota_tile()
        ones = jnp.ones((QB, HD), BF)

        def step(t, carry):
            dq, over_keys = carry
            off = pl.multiple_of(t * KT, KT)
            k = k_ref[pl.ds(off, KT), :]
            v = v_ref[pl.ds(off, KT), :]
            z = _dot_nt(q, k) * SCALE + cq - _fox_tile_row(ct_ref, t, h)
            p = jnp.where(row + (i * QB - t * KT) >= col, jnp.exp(z - lse_c), 0.0)
            dz = p * (_dot_nt(dov, v) - delta)
            dzt = dz.T
            dq = dq + _dot((dz * SCALE).astype(BF), k)
            dk_acc[pl.ds(off, KT), :] += _dot((dzt * SCALE).astype(BF), q)
            dv_acc[pl.ds(off, KT), :] += _dot(p.T.astype(BF), dov)
            dc_acc[pl.ds(off, KT), :] += _tri_dot(dzt, ones)
            return dq, over_keys + jnp.sum(dz, axis=1, keepdims=True)

        dq, over_keys = lax.fori_loop(0, i // (KT // QB) + 1, step, (jnp.zeros((QB, HD), F32), jnp.zeros((QB, 1), F32)))
        dq_ref[...] = dq.astype(BF)
        dcq_ref[...] = jnp.broadcast_to(over_keys, (QB, HD))

        @pl.when(i == nb - 1)
        def _():
            dk_ref[...] = dk_acc[...].astype(BF)
            dv_ref[...] = dv_acc[...].astype(BF)
            dc_ref[...] = dc_acc[...]

    blk = pl.BlockSpec((QB, HD), lambda h, i: (i, h))
    full = pl.BlockSpec((s, HD), lambda h, i: (0, h))
    return pl.pallas_call(
        body, name=name, grid=(nfox, nb),
        in_specs=[pl.BlockSpec((QB, HD), lambda h, i: (i, N_SB + h)),
                  pl.BlockSpec((s, HD), lambda h, i: (0, N_HEADS + N_SB + h)),
                  pl.BlockSpec((s, HD), lambda h, i: (0, 2 * N_HEADS + N_SB + h)),
                  blk, pl.BlockSpec((nb, 8, HD), lambda h, i: (0, 0, 0)), blk, blk,
                  pl.BlockSpec((QB, HD), lambda h, i: (i, N_SB + h))],
        out_specs=[blk, full, full, blk, full],
        out_shape=[_sds((s, nfox * HD), BF)] * 3 + [_sds((s, nfox * HD), F32)] * 2,
        scratch_shapes=[pltpu.VMEM((s, HD), F32)] * 3,
        compiler_params=_cp("parallel", "arbitrary"),
    )(qkv, qkv, qkv, cum_b, cum_t, o, lse, do)


def _dil_rows(r, n, d):
    if d == 1:
        return pl.ds(pl.multiple_of(n * QB, QB), QB)
    return pl.ds(r + n * (QB * d), QB, stride=d)


def _dilated_fwd(qkv, name):
    s = qkv.shape[0]
    npat = len(DILATED_PATTERNS)
    chunk = 256

    def body(q_ref, k_ref, v_ref, out_ref, outt_ref, g_ref, qf, kf, vf, *per_pattern):
        o_s, l_s = per_pattern[:npat], per_pattern[npat:]
        qf[...] = q_ref[...].astype(F32)
        kf[...] = k_ref[...].astype(F32)
        vf[...] = v_ref[...].astype(F32)
        row, col = _iota2()
        for p, (_, d) in enumerate(DILATED_PATTERNS):
            nb = s // d // QB

            def blk(idx, carry, p=p, d=d, nb=nb):
                r, n = idx // nb, idx % nb
                cur = _dil_rows(r, n, d)
                q = qf[cur, :].astype(BF)
                zc = jnp.where(col <= row, _dot_nt(q, kf[cur, :].astype(BF)) * SCALE, NEG_INF)
                m = jnp.max(zc, axis=1, keepdims=True)
                if nb > 1:
                    prv = _dil_rows(r, jnp.maximum(n - 1, 0), d)
                    mp = jnp.logical_and(col >= row, n >= 1)
                    zp = jnp.where(mp, _dot_nt(q, kf[prv, :].astype(BF)) * SCALE, NEG_INF)
                    m = jnp.maximum(m, jnp.max(zp, axis=1, keepdims=True))
                ec = jnp.exp(zc - m)
                l = jnp.sum(ec, axis=1, keepdims=True)
                if nb > 1:
                    ep = jnp.where(mp, jnp.exp(zp - m), 0.0)
                    l = l + jnp.sum(ep, axis=1, keepdims=True)
                o = _dot((ec / l).astype(BF), vf[cur, :].astype(BF))
                if nb > 1:
                    o = o + _dot((ep / l).astype(BF), vf[prv, :].astype(BF))
                o_s[p][cur, :] = o
                l_s[p][cur, :] = jnp.broadcast_to(m + jnp.log(l), (QB, HD))
                return carry

            lax.fori_loop(0, s // QB, blk, 0, unroll=2)
        for c0 in range(0, s, chunk):
            rows = slice(c0, c0 + chunk)
            ls = [l_s[p][rows, :] for p in range(npat)]
            m = functools.reduce(jnp.maximum, ls)
            es = [jnp.exp(l - m) for l in ls]
            tot = functools.reduce(lambda a, b: a + b, es)
            out = functools.reduce(lambda a, b: a + b, [(e / tot) * o_s[p][rows, :] for p, e in enumerate(es)])
            out_ref[rows, :] = out.astype(BF)
            outt_ref[:, rows] = out.T.astype(BF)
            g_ref[rows, :] = m + jnp.log(tot)

    full = pl.BlockSpec((s, HD), lambda h: (0, h))
    return pl.pallas_call(
        body, name=name, grid=(N_HEADS,),
        in_specs=[full, pl.BlockSpec((s, HD), lambda h: (0, N_HEADS + h)), pl.BlockSpec((s, HD), lambda h: (0, 2 * N_HEADS + h))],
        out_specs=[full, pl.BlockSpec((HD, s), lambda h: (h, 0)), full],
        out_shape=[_sds((s, N_HEADS * HD), BF), _sds((N_HEADS * HD, s), BF), _sds((s, N_HEADS * HD), F32)],
        scratch_shapes=[pltpu.VMEM((s, HD), F32)] * (3 + 2 * npat),
        compiler_params=_cp("parallel"),
    )(qkv, qkv, qkv)


def _dilated_bwd(qkv, out, glse, do, tables, name):
    s = qkv.shape[0]
    chunk = 256

    def body(q_ref, k_ref, v_ref, out_ref, g_ref, do_ref, c_ref, sa_ref, sb_ref, dq_ref, dk_ref, dv_ref,
             qf, kf, vf, dof, dl_s, dq_a, dk_a, dv_a):
        qf[...] = q_ref[...].astype(F32)
        kf[...] = k_ref[...].astype(F32)
        vf[...] = v_ref[...].astype(F32)
        for c0 in range(0, s, chunk):
            rows = slice(c0, c0 + chunk)
            dov = do_ref[rows, :].astype(F32)
            dof[rows, :] = dov
            dl_s[rows, :] = jnp.broadcast_to(jnp.sum(dov * out_ref[rows, :].astype(F32), axis=1, keepdims=True), (chunk, HD))
        dq_a[...] = jnp.zeros_like(dq_a)
        dk_a[...] = jnp.zeros_like(dk_a)
        dv_a[...] = jnp.zeros_like(dv_a)
        row, col = _iota2()
        for _, d in DILATED_PATTERNS:
            nb = s // d // QB

            def blk(idx, carry, d=d, nb=nb):
                r, n = idx // nb, idx % nb
                cur = _dil_rows(r, n, d)
                q = qf[cur, :].astype(BF)
                kc = kf[cur, :].astype(BF)
                dov = dof[cur, :].astype(BF)
                g = _col(g_ref[cur, :])
                delta = _col(dl_s[cur, :])
                pc = jnp.where(col <= row, jnp.exp(_dot_nt(q, kc) * SCALE - g), 0.0)
                dzc = pc * (_dot_nt(dov, vf[cur, :].astype(BF)) - delta) * SCALE
                dq = _dot(dzc.astype(BF), kc)
                if nb > 1:
                    prv = _dil_rows(r, jnp.maximum(n - 1, 0), d)
                    kp = kf[prv, :].astype(BF)
                    mp = jnp.logical_and(col >= row, n >= 1)
                    pp = jnp.where(mp, jnp.exp(_dot_nt(q, kp) * SCALE - g), 0.0)
                    dzp = pp * (_dot_nt(dov, vf[prv, :].astype(BF)) - delta) * SCALE
                    dq = dq + _dot(dzp.astype(BF), kp)
                dq_a[cur, :] += dq
                dk_a[cur, :] += _dot(dzc.T.astype(BF), q)
                dv_a[cur, :] += _dot(pc.T.astype(BF), dov)
                if nb > 1:
                    dk_a[prv, :] += _dot(dzp.T.astype(BF), q)
                    dv_a[prv, :] += _dot(pp.T.astype(BF), dov)
                return carry

            lax.fori_loop(0, s // QB, blk, 0)
        for c0 in range(0, s, chunk):
            rows = slice(c0, c0 + chunk)
            c, sa, sb = c_ref[rows, :], sa_ref[rows, :], sb_ref[rows, :]
            dq_ref[rows, :] = _rope(dq_a[rows, :], c, sa, sb).astype(BF)
            dk_ref[rows, :] = _rope(dk_a[rows, :], c, sa, sb).astype(BF)
            dv_ref[rows, :] = dv_a[rows, :].astype(BF)

    full = pl.BlockSpec((s, HD), lambda h: (0, h))
    tab = pl.BlockSpec((s, HD), lambda h: (0, 0))
    return pl.pallas_call(
        body, name=name, grid=(N_HEADS,),
        in_specs=[full, pl.BlockSpec((s, HD), lambda h: (0, N_HEADS + h)), pl.BlockSpec((s, HD), lambda h: (0, 2 * N_HEADS + h)),
                  full, full, full, tab, tab, tab],
        out_specs=[full, full, full], out_shape=[_sds((s, N_HEADS * HD), BF)] * 3,
        scratch_shapes=[pltpu.VMEM((s, HD), F32)] * 8,
        compiler_params=_cp("parallel"),
    )(qkv, qkv, qkv, out, glse, do, *tables)


def _swiglu_fwd(x, gnorm, w, tag, deps=()):
    h, ht = _rmsnorm_fwd(x, gnorm, f"norm_{tag}", deps)
    g, u, act = _ffn_up(h, w["gate"], w["up"], f"ffn_up_{tag}")
    y = _ffn_down(act, w["down"], x, f"ffn_down_{tag}")
    return y, (x, ht, g, u, act)


def _swiglu_bwd(saved, gnorm, w, dy, dyb_half, out_scale, tag, deps=()):
    x, ht, g, u, act = saved
    dg, du = _ffn_bwd_act(dyb_half, w["down"], g, u, f"ffn_bwd_act_{tag}", deps)
    d_down = _grad_rows(_transpose_blocked(act, f"act_t_{tag}"), dyb_half, f"ffn_bwd_wd_{tag}")
    d_gate, d_up = _grad_cols(ht, [dg, du], [False, False], f"ffn_bwd_wgu_{tag}")
    dh = _dh_cols([dg, du], [w["gate"], w["up"]], [False, False], f"ffn_bwd_dh_{tag}")
    dx, dxb, dxbt, dgn = _rmsnorm_bwd(x, gnorm, dh, dy, out_scale, f"norm_bwd_{tag}")
    return (dx, dxb, dxbt), dgn, {"gate": d_gate, "up": d_up, "down": d_down}


def kernel(x, norm_g, ffn1_w_gate, ffn1_w_up, ffn1_w_down, ffn2_w_gate, ffn2_w_up, ffn2_w_down, even_w_in, even_b_forget, even_w_out, odd_w_qkv, odd_w_out, final_norm_g, loss_target, m_norm_g, m_ffn1_w_gate, m_ffn1_w_up, m_ffn1_w_down, m_ffn2_w_gate, m_ffn2_w_up, m_ffn2_w_down, m_even_w_in, m_even_b_forget, m_even_w_out, m_odd_w_qkv, m_odd_w_out, m_final_norm_g, v_norm_g, v_ffn1_w_gate, v_ffn1_w_up, v_ffn1_w_down, v_ffn2_w_gate, v_ffn2_w_up, v_ffn2_w_down, v_even_w_in, v_even_b_forget, v_even_w_out, v_odd_w_qkv, v_odd_w_out, v_final_norm_g):
    s, d = x.shape[1], x.shape[2]
    nfox = N_HEADS - N_SB
    ax, ay, ac = lax.axis_index("x"), lax.axis_index("y"), lax.axis_index("c")
    me = 4 * ax + 2 * ay + ac
    slots = jnp.stack([4 * px + 2 * py + ac for px, py in [(ax, ay), (1 - ax, ay), (ax, 1 - ay), (1 - ax, 1 - ay)]]).astype(jnp.int32)
    x0 = x.reshape(s, d)
    target = loss_target.reshape(s, d)

    def bf(w):
        return w.astype(BF)

    groups = [
        [bf(ffn1_w_gate[0]), bf(ffn1_w_up[0]), bf(ffn1_w_down[0]), norm_g.reshape(6, d // NDEV)],
        [bf(even_w_in[0]), bf(even_w_out[0])],
        [bf(ffn2_w_gate[0]), bf(ffn2_w_up[0]), bf(ffn2_w_down[0])],
        [bf(ffn1_w_gate[1]), bf(ffn1_w_up[1]), bf(ffn1_w_down[1])],
        [bf(odd_w_qkv[0]), bf(odd_w_out[0])],
        [bf(ffn2_w_gate[1]), bf(ffn2_w_up[1]), bf(ffn2_w_down[1])],
    ]
    started = []
    for k, grp in enumerate(groups):
        started.append(_gather_start(grp, me, [started[-1]["token"]] if started else [], f"gather_start_{k}"))
    all_started = [started[-1]["token"]]

    def gathered(k, after):
        return _gather_finish(_gather_forward(started[k], after, f"gather_forward_{k}"), f"gather_finish_{k}")

    def ffn_weights(ws_):
        return {"gate": ws_[0], "up": ws_[1], "down": ws_[2]}

    b_pad = jnp.pad(even_b_forget, ((0, 0), (0, HD - nfox)))
    gfin = final_norm_g.reshape(1, d)

    g0 = gathered(0, x0)
    gn = jnp.transpose(g0[3], (1, 0, 2)).reshape(6, 1, d)
    wf = [[ffn_weights(g0), None], [None, None]]
    x1, sv_f1_0 = _swiglu_fwd(x0, gn[0], wf[0][0], "l0a", all_started)
    g1 = gathered(1, x1)
    w_in_nat = jnp.transpose(g1[0], (1, 0, 2)).reshape(d, -1)
    w_qkv_e = w_in_nat[:, :3 * d]
    w_f = jnp.pad(w_in_nat[:, 3 * d:], ((0, 0), (0, HD - nfox)))
    w_out_e = g1[1].reshape(d, d)
    h_e, ht_e = _rmsnorm_fwd(x1, gn[1], "norm_l0m")
    qkv_e = _mm_nn(h_e, w_qkv_e, 768, BF, "even_qkv")
    f_e = _mm_nn(h_e, w_f, HD, F32, "even_fgate")
    o_sb, ot_sb, tot_sb = _sb_fwd(qkv_e, "sb_fwd")
    cum_b, cum_t = _fgate_fwd(f_e, b_pad, "fgate_fwd")
    o_fox, ot_fox, lse_fox = _fox_fwd(qkv_e, cum_b, cum_t, "fox_fwd")
    o_e = jnp.concatenate([o_sb, o_fox], axis=1)
    ot_e = jnp.concatenate([ot_sb, ot_fox], axis=0)
    x2 = _mm_nn(o_e, w_out_e, 1024, F32, "even_out", res=x1)
    wf[0][1] = ffn_weights(gathered(2, x2))
    x3, sv_f2_0 = _swiglu_fwd(x2, gn[2], wf[0][1], "l0b")

    wf[1][0] = ffn_weights(gathered(3, x3))
    x4, sv_f1_1 = _swiglu_fwd(x3, gn[3], wf[1][0], "l1a")
    g4 = gathered(4, x4)
    w_qkv_o = g4[0]
    w_out_o = g4[1].reshape(d, d)
    h_o, ht_o = _rmsnorm_fwd(x4, gn[4], "norm_l1m")
    qkv_o = _qkv_rope(h_o, w_qkv_o, _rope_tables(s, 1.0), "odd_qkv")
    o_o, ot_o, glse = _dilated_fwd(qkv_o, "dilated_fwd")
    x5 = _mm_nn(o_o, w_out_o, 1024, F32, "odd_out", res=x4)
    wf[1][1] = ffn_weights(gathered(5, x5))
    x6, sv_f2_1 = _swiglu_fwd(x5, gn[5], wf[1][1], "l1b")

    def reduce_start(gs, tag):
        gs = [g_ if g_.ndim == 3 else g_.reshape(NDEV, g_.shape[0] // NDEV, g_.shape[1]) for g_ in gs]
        a_s = _pair_exchange(gs, f"pair_exchange_{tag}")
        ps = [_pair_sum(g_, a_, slots, f"pair_sum_{tag}_{k}") for k, (g_, a_) in enumerate(zip(gs, a_s))]
        return gs, a_s, _chip_start(ps, f"chip_start_{tag}")

    def ffn_grads(gw):
        return [gw["gate"], gw["up"], gw["down"]]

    dx6, dx6b, _, d_gfin, loss_part = _loss_head(x6, gfin, target, "loss_head")

    (dx5, dx5b, dx5bt), dgn5, gw_f2_1 = _swiglu_bwd(sv_f2_1, gn[5], wf[1][1], dx6, dx6b, 1.0, "l1b")
    red_l1b = reduce_start(ffn_grads(gw_f2_1), "l1b")
    d_wout_o = _mm_nn(ot_o, dx5b, 1024, BF, "odd_out_dw")
    do_o = _mm_nt([(dx5b, w_out_o)], "odd_out_do", BF, deps=[red_l1b[2]["token"]])
    dqkv_o = jnp.concatenate(_dilated_bwd(qkv_o, o_o, glse, do_o, _rope_tables(s, -1.0), "dilated_bwd"), axis=1)
    (d_wqkv_o,) = _grad_cols(ht_o, [dqkv_o], [True], "odd_qkv_dw")
    dh_o = _dh_cols([dqkv_o], [w_qkv_o], [True], "odd_qkv_dh")
    dx4, dx4b, _, dgn4 = _rmsnorm_bwd(x4, gn[4], dh_o, dx5, 0.5, "norm_bwd_l1m")
    red_l1m = reduce_start([d_wqkv_o, d_wout_o], "l1m")
    (dx3, dx3b, _), dgn3, gw_f1_1 = _swiglu_bwd(sv_f1_1, gn[3], wf[1][0], dx4, dx4b, 0.5, "l1a", [red_l1m[2]["token"]])
    red_l1a = reduce_start(ffn_grads(gw_f1_1), "l1a")

    (dx2, dx2b, dx2bt), dgn2, gw_f2_0 = _swiglu_bwd(sv_f2_0, gn[2], wf[0][1], dx3, dx3b, 1.0, "l0b", [red_l1a[2]["token"]])
    red_l0b = reduce_start(ffn_grads(gw_f2_0), "l0b")
    d_wout_e = _mm_nn(ot_e, dx2b, 1024, BF, "even_out_dw")
    do_e = _mm_nt([(dx2b, w_out_e)], "even_out_do", BF, deps=[red_l0b[2]["token"]])
    dq_sb, dk_sb, dv_sb = _sb_bwd(qkv_e, do_e, tot_sb, "sb_bwd")
    dq_fx, dk_fx, dv_fx, dcq, dck = _fox_bwd(qkv_e, cum_b, cum_t, o_fox, lse_fox, do_e, "fox_bwd")
    df, db_part = _fgate_bwd(dcq, dck, f_e, b_pad, "fgate_bwd")
    dqkv_e = jnp.concatenate([dq_sb, dq_fx, dk_sb, dk_fx, dv_sb, dv_fx], axis=1)
    d_wqkv_e = _mm_nn(ht_e, dqkv_e, 768, BF, "even_qkv_dw")
    d_wf = _mm_nn(ht_e, df, HD, BF, "even_fgate_dw")
    dh_e = _mm_nt([(dqkv_e, w_qkv_e), (df, w_f)], "even_in_dh")
    dx1, dx1b, _, dgn1 = _rmsnorm_bwd(x1, gn[1], dh_e, dx2, 0.5, "norm_bwd_l0m")
    d_win_nat = jnp.concatenate([d_wqkv_e, d_wf[:, :nfox]], axis=1)
    d_win = jnp.transpose(d_win_nat.reshape(d, NDEV, -1), (1, 0, 2))
    red_l0m = reduce_start([d_win, d_wout_e], "l0m")
    (dx0, _, _), dgn0, gw_f1_0 = _swiglu_bwd(sv_f1_0, gn[0], wf[0][0], dx1, dx1b, 1.0, "l0a", [red_l0m[2]["token"]])
    red_l0a = reduce_start(ffn_grads(gw_f1_0), "l0a")

    def reduce_finish(red, tag, after):
        gs, a_s, st = red
        return list(zip(gs, a_s, _chip_finish(st, after, f"chip_finish_{tag}")))

    f_l1b, f_l1m, f_l1a = reduce_finish(red_l1b, "l1b", dx0), reduce_finish(red_l1m, "l1m", dx0), reduce_finish(red_l1a, "l1a", dx0)
    f_l0b, f_l0m = reduce_finish(red_l0b, "l0b", dx0), reduce_finish(red_l0m, "l0m", dx0)

    res = {}
    res["even_w_in"] = _adamw_sharded(even_w_in, m_even_w_in, v_even_w_in, [f_l0m[0]], slots, "adamw_even_w_in")
    res["even_w_out"] = _adamw_sharded(even_w_out, m_even_w_out, v_even_w_out, [f_l0m[1]], slots, "adamw_even_w_out")
    res["odd_w_qkv"] = _adamw_sharded(odd_w_qkv, m_odd_w_qkv, v_odd_w_qkv, [f_l1m[0]], slots, "adamw_odd_w_qkv")
    res["odd_w_out"] = _adamw_sharded(odd_w_out, m_odd_w_out, v_odd_w_out, [f_l1m[1]], slots, "adamw_odd_w_out")
    names = ["ffn2_w_gate", "ffn2_w_up", "ffn2_w_down", "ffn1_w_gate", "ffn1_w_up", "ffn1_w_down"]
    ws = [ffn2_w_gate, ffn2_w_up, ffn2_w_down, ffn1_w_gate, ffn1_w_up, ffn1_w_down]
    ms = [m_ffn2_w_gate, m_ffn2_w_up, m_ffn2_w_down, m_ffn1_w_gate, m_ffn1_w_up, m_ffn1_w_down]
    vs = [v_ffn2_w_gate, v_ffn2_w_up, v_ffn2_w_down, v_ffn1_w_gate, v_ffn1_w_up, v_ffn1_w_down]
    for k in range(3):
        res[names[k]] = _adamw_sharded(ws[k], ms[k], vs[k], [f_l0b[k], f_l1b[k]], slots, f"adamw_{names[k]}")
    f_l0a = reduce_finish(red_l0a, "l0a", res["ffn2_w_down"][1])
    for k in range(3, 6):
        res[names[k]] = _adamw_sharded(ws[k], ms[k], vs[k], [f_l0a[k - 3], f_l1a[k - 3]], slots, f"adamw_{names[k]}")

    dnorm = jnp.concatenate([dgn0, dgn1, dgn2, dgn3, dgn4, dgn5], axis=0)
    nsm = d // NDEV
    small_rows = (6 * d + d + 2 * HD) // HD
    pad_rows = -small_rows % 8
    part = jnp.concatenate([dnorm.reshape(-1), d_gfin.reshape(-1), db_part.reshape(-1), loss_part.reshape(-1),
                            jnp.zeros((pad_rows * HD,), F32)]).reshape(small_rows + pad_rows, HD)
    (gathered,) = _all_gather([part], "gather_small")

    def pack(ng, bfg, fg):
        full = lax.dynamic_update_slice(jnp.zeros((6, d), F32), ng.reshape(6, nsm), (0, me * nsm))
        return jnp.concatenate([full.reshape(-1), fg.reshape(-1), jnp.pad(bfg.reshape(-1), (0, HD - nfox)),
                                jnp.zeros((HD + pad_rows * HD,), F32)]).reshape(small_rows + pad_rows, HD)

    sm = _adamw_small(pack(norm_g, even_b_forget, final_norm_g), pack(m_norm_g, m_even_b_forget, m_final_norm_g),
                      pack(v_norm_g, v_even_b_forget, v_final_norm_g), gathered, "adamw_small")

    def unpack(t):
        flat = t.reshape(-1)
        ng = lax.dynamic_slice(flat[:6 * d].reshape(6, d), (0, me * nsm), (6, nsm)).reshape(norm_g.shape)
        fg = flat[6 * d:7 * d].reshape(final_norm_g.shape)
        bfg = flat[7 * d:7 * d + nfox].reshape(even_b_forget.shape)
        return ng, bfg, fg

    sm_g, sm_d, sm_m, sm_v = [unpack(t) for t in sm]
    loss = sm[0].reshape(-1)[7 * d + HD]

    order = ["norm_g", "ffn1_w_gate", "ffn1_w_up", "ffn1_w_down", "ffn2_w_gate", "ffn2_w_up", "ffn2_w_down", "even_w_in",
             "even_b_forget", "even_w_out", "odd_w_qkv", "odd_w_out", "final_norm_g"]
    outs = [loss, dx0.reshape(x.shape)]
    for k in range(4):
        smk = [sm_g, sm_d, sm_m, sm_v][k]
        for nm in order:
            if nm == "norm_g":
                outs.append(smk[0])
            elif nm == "even_b_forget":
                outs.append(smk[1])
            elif nm == "final_norm_g":
                outs.append(smk[2])
            else:
                outs.append(res[nm][k])
    return tuple(outs)
```

```python
import functools
import math

import jax
import jax.numpy as jnp
from jax import lax
from jax.experimental import pallas as pl
from jax.experimental.pallas import tpu as pltpu

F32 = jnp.float32
BF = jnp.bfloat16
NDEV = 8
HD = 128
QB = 128
N_HEADS = 16
N_SB = 8
SCALE = HD ** -0.5
ROPE_THETA = 500000.0
ROPE_DIMS = HD // 4
DILATED_PATTERNS = ((128, 1), (512, 4), (2048, 16))
RMS_EPS = 1e-6
NEG_INF = -1e30
ADAM_LR = 0.001
ADAM_B1 = 0.9
ADAM_B2 = 0.999
ADAM_EPS = 1e-08
ADAM_WD = 0.01
ADAM_STEP = 10
VMEM_LIMIT_V7X = 56 * 1024 * 1024
MESH = pl.DeviceIdType.MESH
ANY = pl.BlockSpec(memory_space=pl.ANY)

NT_DIMS = (((1,), (1,)), ((), ()))


def _cp(*dims):
    return pltpu.CompilerParams(dimension_semantics=dims if dims else None, vmem_limit_bytes=VMEM_LIMIT_V7X)


def _dot(a, b):
    return jnp.dot(a, b, preferred_element_type=F32)


def _dot_nt(a, b):
    return lax.dot_general(a, b, NT_DIMS, preferred_element_type=F32)


def _sds(shape, dtype):
    return jax.ShapeDtypeStruct(shape, dtype)


def _place():
    x, y, c = lax.axis_index("x"), lax.axis_index("y"), lax.axis_index("c")
    chips = [(x, y), (1 - x, y), (x, 1 - y), (1 - x, 1 - y)]
    return x, y, c, chips


def _all_gather(xs, name):
    n = len(xs)

    def body(*refs):
        x_refs, out_refs = refs[:n], refs[n:2 * n]
        send_sems, recv_sems, local_sems = refs[2 * n:]
        x, y, c, chips = _place()
        me, sibling = (x, y, c), (x, y, 1 - c)
        others = chips[1:]

        def slot(a, px, py, pc):
            return out_refs[a].at[4 * px + 2 * py + pc]

        def copy(a, k, block, to, src=None):
            return pltpu.make_async_remote_copy(
                src_ref=slot(a, *block) if src is None else src, dst_ref=slot(a, *block),
                send_sem=send_sems.at[a, k], recv_sem=recv_sems.at[a, k], device_id=to, device_id_type=MESH)

        started = []
        for a in range(n):
            mine = pltpu.make_async_copy(x_refs[a], slot(a, *me), local_sems.at[a])
            mine.start()
            first = [copy(a, 0, me, sibling, src=x_refs[a])]
            first += [copy(a, 1 + j, me, (*chip, c), src=x_refs[a]) for j, chip in enumerate(others)]
            for cp in first:
                cp.start()
            started += [mine.wait] + [cp.wait_send for cp in first]
        for a in range(n):
            for j, chip in enumerate(others):
                copy(a, 1 + j, (*chip, c), me).wait_recv()
                passed = copy(a, 4 + j, (*chip, c), sibling)
                passed.start()
                started.append(passed.wait_send)
        for a in range(n):
            copy(a, 0, sibling, me).wait_recv()
            for j, chip in enumerate(others):
                copy(a, 4 + j, (*chip, 1 - c), me).wait_recv()
        for w in started:
            w()

    return pl.pallas_call(
        body, name=name,
        out_shape=[_sds((NDEV,) + x.shape, x.dtype) for x in xs],
        in_specs=[ANY] * n, out_specs=[ANY] * n,
        scratch_shapes=[pltpu.SemaphoreType.DMA((n, 7)), pltpu.SemaphoreType.DMA((n, 7)), pltpu.SemaphoreType.DMA((n,))],
    )(*xs)


def _pair_exchange(gs, name):
    n = len(gs)

    def body(*refs):
        g_refs, a_refs = refs[:n], refs[n:2 * n]
        send_sems, recv_sems = refs[2 * n:]
        x, y, c, chips = _place()
        copies = []
        for a in range(n):
            for j, (px, py) in enumerate(chips):
                copies.append(pltpu.make_async_remote_copy(
                    src_ref=g_refs[a].at[4 * px + 2 * py + (1 - c)], dst_ref=a_refs[a].at[j],
                    send_sem=send_sems.at[a, j], recv_sem=recv_sems.at[a, j],
                    device_id=(x, y, 1 - c), device_id_type=MESH))
        for cp in copies:
            cp.start()
        for cp in copies:
            cp.wait()

    return pl.pallas_call(
        body, name=name,
        out_shape=[_sds((4,) + g.shape[1:], g.dtype) for g in gs],
        in_specs=[ANY] * n, out_specs=[ANY] * n,
        scratch_shapes=[pltpu.SemaphoreType.DMA((n, 4)), pltpu.SemaphoreType.DMA((n, 4))],
    )(*gs)


HBM = pl.BlockSpec(memory_space=pltpu.HBM)
SEM = pl.BlockSpec(memory_space=pltpu.SEMAPHORE)
EFFECT = pltpu.SideEffectType.DATAFLOW_SIDE_EFFECTING
TOKEN = _sds((8, 128), F32)
TOKEN_SPEC = pl.BlockSpec((8, 128), lambda *_: (0, 0))


def _in_hbm(x):
    return pltpu.with_memory_space_constraint(x, pltpu.HBM)


def _ignore_deps(body, n_in, n_deps):
    if not n_deps:
        return body
    return lambda *refs: body(*refs[:n_in], *refs[n_in + n_deps:])


def _slot_of(px, py, pc):
    return 4 * px + 2 * py + pc


def _gather_start(xs, me, deps, name):
    n = len(xs)
    lands = [lax.dynamic_update_slice(lax.empty((NDEV,) + x.shape, x.dtype), x[None], (me,) + (0,) * x.ndim) for x in xs]

    def body(*refs):
        x_refs, land_refs = refs[:n], refs[n:2 * n]
        send, recv_ici, recv_sib = refs[2 * n:2 * n + 3]
        token = refs[4 * n + 3]
        x, y, c, chips = _place()
        for a in range(n):
            dst = land_refs[a].at[_slot_of(x, y, c)]
            pltpu.make_async_remote_copy(src_ref=x_refs[a], dst_ref=dst, send_sem=send.at[4 * a], recv_sem=recv_sib.at[a],
                                         device_id=(x, y, 1 - c), device_id_type=MESH).start()
            for j, chip in enumerate(chips[1:]):
                pltpu.make_async_remote_copy(src_ref=x_refs[a], dst_ref=dst, send_sem=send.at[4 * a + 1 + j], recv_sem=recv_ici.at[3 * a + j],
                                             device_id=(*chip, c), device_id_type=MESH).start()
        token[...] = jnp.zeros_like(token)

    outs = pl.pallas_call(
        _ignore_deps(body, 2 * n, len(deps)), name=name,
        out_shape=(pltpu.SemaphoreType.DMA((4 * n,)), pltpu.SemaphoreType.DMA((3 * n,)), pltpu.SemaphoreType.DMA((n,)),
                   *[pltpu.HBM(x.shape, x.dtype) for x in xs], *[pltpu.HBM(l.shape, l.dtype) for l in lands], TOKEN),
        in_specs=[HBM] * (2 * n) + [TOKEN_SPEC] * len(deps),
        out_specs=(SEM, SEM, SEM, *[HBM] * (2 * n), pl.BlockSpec(memory_space=pltpu.VMEM)),
        input_output_aliases={a: 3 + a for a in range(2 * n)},
        compiler_params=pltpu.CompilerParams(has_side_effects=EFFECT),
    )(*[_in_hbm(x) for x in xs], *[_in_hbm(l) for l in lands], *deps)
    send, recv_ici, recv_sib = outs[:3]
    return dict(send=send, recv_ici=recv_ici, recv_sib=recv_sib, xs=list(outs[3:3 + n]), lands=list(outs[3 + n:3 + 2 * n]), token=outs[-1])


def _gather_forward(st, after, name):
    n = len(st["lands"])

    def body(*refs):
        land_refs, recv_ici = refs[:n], refs[n]
        send2, recv2 = refs[n + 2], refs[n + 3]
        x, y, c, chips = _place()
        for a in range(n):
            for j, chip in enumerate(chips[1:]):
                blk = land_refs[a].at[_slot_of(*chip, c)]
                pltpu.make_async_remote_copy(src_ref=blk, dst_ref=blk, send_sem=send2.at[3 * a + j], recv_sem=recv_ici.at[3 * a + j],
                                             device_id=(*chip, c), device_id_type=MESH).wait_recv()
                pltpu.make_async_remote_copy(src_ref=blk, dst_ref=blk, send_sem=send2.at[3 * a + j], recv_sem=recv2.at[3 * a + j],
                                             device_id=(x, y, 1 - c), device_id_type=MESH).start()

    outs = pl.pallas_call(
        body, name=name,
        out_shape=(pltpu.SemaphoreType.DMA((3 * n,)), pltpu.SemaphoreType.DMA((3 * n,)), *[pltpu.HBM(l.shape, l.dtype) for l in st["lands"]]),
        in_specs=[HBM] * n + [SEM, pl.BlockSpec(memory_space=pl.ANY)],
        out_specs=(SEM, SEM, *[HBM] * n),
        input_output_aliases={a: 2 + a for a in range(n)},
        compiler_params=pltpu.CompilerParams(has_side_effects=EFFECT),
    )(*st["lands"], st["recv_ici"], after)
    return dict(st, send2=outs[0], recv2=outs[1], lands=list(outs[2:]))


def _gather_finish(st, name):
    n = len(st["lands"])

    def body(*refs):
        x_refs, land_refs = refs[:n], refs[n:2 * n]
        send, recv_sib, send2, recv2 = refs[2 * n:2 * n + 4]
        x, y, c, chips = _place()
        for a in range(n):
            mine = land_refs[a].at[_slot_of(x, y, c)]
            theirs = land_refs[a].at[_slot_of(x, y, 1 - c)]
            for k in range(4):
                pltpu.make_async_remote_copy(src_ref=x_refs[a], dst_ref=mine, send_sem=send.at[4 * a + k], recv_sem=recv_sib.at[a],
                                             device_id=(x, y, 1 - c), device_id_type=MESH).wait_send()
            pltpu.make_async_remote_copy(src_ref=x_refs[a], dst_ref=theirs, send_sem=send.at[4 * a], recv_sem=recv_sib.at[a],
                                         device_id=(x, y, 1 - c), device_id_type=MESH).wait_recv()
            for j, chip in enumerate(chips[1:]):
                sent = land_refs[a].at[_slot_of(*chip, c)]
                got = land_refs[a].at[_slot_of(*chip, 1 - c)]
                pltpu.make_async_remote_copy(src_ref=sent, dst_ref=sent, send_sem=send2.at[3 * a + j], recv_sem=recv2.at[3 * a + j],
                                             device_id=(x, y, 1 - c), device_id_type=MESH).wait_send()
                pltpu.make_async_remote_copy(src_ref=got, dst_ref=got, send_sem=send2.at[3 * a + j], recv_sem=recv2.at[3 * a + j],
                                             device_id=(x, y, 1 - c), device_id_type=MESH).wait_recv()

    outs = pl.pallas_call(
        body, name=name,
        out_shape=tuple(pltpu.HBM(v.shape, v.dtype) for v in st["xs"] + st["lands"]),
        in_specs=[HBM] * (2 * n) + [SEM] * 4, out_specs=tuple([HBM] * (2 * n)),
        input_output_aliases={a: a for a in range(2 * n)},
        compiler_params=pltpu.CompilerParams(has_side_effects=EFFECT),
    )(*st["xs"], *st["lands"], st["send"], st["recv_sib"], st["send2"], st["recv2"])
    return list(outs[n:])


def _chip_start(ps, name):
    n = len(ps)
    lands = [lax.empty(p.shape, p.dtype) for p in ps]

    def body(*refs):
        p_refs, b_refs = refs[:n], refs[n:2 * n]
        send, recv = refs[2 * n], refs[2 * n + 1]
        token = refs[4 * n + 2]
        x, y, c, chips = _place()
        for a in range(n):
            for j, chip in enumerate(chips[1:]):
                pltpu.make_async_remote_copy(src_ref=p_refs[a].at[j], dst_ref=b_refs[a].at[j], send_sem=send.at[3 * a + j], recv_sem=recv.at[3 * a + j],
                                             device_id=(*chip, c), device_id_type=MESH).start()
        token[...] = jnp.zeros_like(token)

    outs = pl.pallas_call(
        body, name=name,
        out_shape=(pltpu.SemaphoreType.DMA((3 * n,)), pltpu.SemaphoreType.DMA((3 * n,)),
                   *[pltpu.HBM(p.shape, p.dtype) for p in ps], *[pltpu.HBM(p.shape, p.dtype) for p in ps], TOKEN),
        in_specs=[HBM] * (2 * n), out_specs=(SEM, SEM, *[HBM] * (2 * n), pl.BlockSpec(memory_space=pltpu.VMEM)),
        input_output_aliases={a: 2 + a for a in range(2 * n)},
        compiler_params=pltpu.CompilerParams(has_side_effects=EFFECT),
    )(*[_in_hbm(p) for p in ps], *[_in_hbm(l) for l in lands])
    return dict(send=outs[0], recv=outs[1], ps=list(outs[2:2 + n]), lands=list(outs[2 + n:2 + 2 * n]), token=outs[-1])


def _chip_finish(st, after, name):
    n = len(st["ps"])

    def body(*refs):
        p_refs, b_refs = refs[:n], refs[n:2 * n]
        send, recv = refs[2 * n], refs[2 * n + 1]
        x, y, c, chips = _place()
        for a in range(n):
            for j, chip in enumerate(chips[1:]):
                cp = pltpu.make_async_remote_copy(src_ref=p_refs[a].at[j], dst_ref=b_refs[a].at[j], send_sem=send.at[3 * a + j], recv_sem=recv.at[3 * a + j],
                                                  device_id=(*chip, c), device_id_type=MESH)
                cp.wait_send()
                cp.wait_recv()

    outs = pl.pallas_call(
        body, name=name,
        out_shape=tuple(pltpu.HBM(v.shape, v.dtype) for v in st["ps"] + st["lands"]),
        in_specs=[HBM] * (2 * n) + [SEM, SEM, pl.BlockSpec(memory_space=pl.ANY)], out_specs=tuple([HBM] * (2 * n)),
        input_output_aliases={a: a for a in range(2 * n)},
        compiler_params=pltpu.CompilerParams(has_side_effects=EFFECT),
    )(*st["ps"], *st["lands"], st["send"], st["recv"], after)
    return list(outs[n:])


def _rows_tile(r):
    for t in (512, 256, 128, 64, 32, 16):
        if r % t == 0:
            return t
    return r


def _pair_sum(g, a, slots, name):
    _, r, c = g.shape
    tr = _rows_tile(r)

    def body(slots_ref, g_ref, a_ref, p_ref):
        p_ref[...] = (g_ref[...].astype(F32) + a_ref[...].astype(F32)).astype(BF)

    return pl.pallas_call(
        body, name=name,
        grid_spec=pltpu.PrefetchScalarGridSpec(
            num_scalar_prefetch=1, grid=(3, r // tr),
            in_specs=[pl.BlockSpec((None, tr, c), lambda j, i, s: (s[j + 1], i, 0)),
                      pl.BlockSpec((None, tr, c), lambda j, i, s: (j + 1, i, 0))],
            out_specs=pl.BlockSpec((None, tr, c), lambda j, i, s: (j, i, 0))),
        out_shape=_sds((3, r, c), BF), compiler_params=_cp("parallel", "parallel"),
    )(slots, g, a)


def _adamw_math(w, g, m, v):
    m = ADAM_B1 * m + (1.0 - ADAM_B1) * g
    v = ADAM_B2 * v + (1.0 - ADAM_B2) * (g * g)
    m_hat = m / (1.0 - ADAM_B1 ** ADAM_STEP)
    v_hat = v / (1.0 - ADAM_B2 ** ADAM_STEP)
    delta = -ADAM_LR * (m_hat / (jnp.sqrt(v_hat) + ADAM_EPS) + ADAM_WD * w)
    return delta, m, v


def _adamw_sharded(w, m, v, parts, slots, name, transposed=False):
    nl = w.shape[0]
    r, c = parts[0][0].shape[1:]
    tr = _rows_tile(r)
    if c * tr * 4 > (1 << 21) and not transposed:
        tr = max(8, tr // 2)

    def body(slots_ref, w_ref, m_ref, v_ref, *rest):
        part_refs, (g_out, d_out, m_out, v_out) = rest[:5 * nl], rest[5 * nl:]
        layer = pl.program_id(0)
        g = None
        for l in range(nl):
            s = part_refs[5 * l][...].astype(F32)
            for ref in part_refs[5 * l + 1:5 * l + 5]:
                s = s + ref[...].astype(F32)
            g = s if g is None else jnp.where(layer == l, s, g)
        if transposed:
            g = g.T
        delta, mn, vn = _adamw_math(w_ref[...], g, m_ref[...], v_ref[...])
        g_out[...] = g
        d_out[...] = delta
        m_out[...] = mn
        v_out[...] = vn

    def own(l):
        return lambda L, i, s: (s[0], jnp.where(L == l, i, 0), 0)

    def fixed(l, k):
        return lambda L, i, s: (k, jnp.where(L == l, i, 0), 0)

    if transposed:
        wspec = pl.BlockSpec((None, c, tr), lambda L, i, s: (L, 0, i))
    else:
        wspec = pl.BlockSpec((None, tr, c), lambda L, i, s: (L, i, 0))
    in_specs = [wspec, wspec, wspec]
    args = [w, m, v]
    for l, (g, a, b) in enumerate(parts):
        in_specs += [pl.BlockSpec((None, tr, c), own(l)), pl.BlockSpec((None, tr, c), fixed(l, 0)),
                     pl.BlockSpec((None, tr, c), fixed(l, 0)), pl.BlockSpec((None, tr, c), fixed(l, 1)),
                     pl.BlockSpec((None, tr, c), fixed(l, 2))]
        args += [g, a, b, b, b]
    return pl.pallas_call(
        body, name=name,
        grid_spec=pltpu.PrefetchScalarGridSpec(
            num_scalar_prefetch=1, grid=(nl, r // tr), in_specs=in_specs, out_specs=[wspec] * 4),
        out_shape=[_sds(w.shape, F32)] * 4, compiler_params=_cp("arbitrary", "arbitrary"),
    )(slots, *args)


def _adamw_small(w, m, v, gathered, name):
    def body(w_ref, m_ref, v_ref, gg_ref, g_out, d_out, m_out, v_out):
        g = gg_ref[0]
        for k in range(1, NDEV):
            g = g + gg_ref[k]
        delta, mn, vn = _adamw_math(w_ref[...], g, m_ref[...], v_ref[...])
        g_out[...] = g
        d_out[...] = delta
        m_out[...] = mn
        v_out[...] = vn

    return pl.pallas_call(body, name=name, out_shape=[_sds(w.shape, F32)] * 4)(w, m, v, gathered)


def _rmsnorm_fwd(x, g, name, deps=()):
    s, d = x.shape
    tm = 256

    def body(x_ref, g_ref, h_ref, ht_ref):
        xf = x_ref[...]
        y = xf * lax.rsqrt(jnp.mean(xf * xf, axis=-1, keepdims=True) + RMS_EPS)
        h = y * g_ref[...]
        h_ref[...] = h.astype(BF)
        ht_ref[...] = h.T.astype(BF)

    return pl.pallas_call(
        _ignore_deps(body, 2, len(deps)), name=name, grid=(s // tm,),
        in_specs=[pl.BlockSpec((tm, d), lambda i: (i, 0)), pl.BlockSpec((1, d), lambda i: (0, 0))] + [TOKEN_SPEC] * len(deps),
        out_specs=[pl.BlockSpec((tm, d), lambda i: (i, 0)), pl.BlockSpec((d, tm), lambda i: (0, i))],
        out_shape=[_sds((s, d), BF), _sds((d, s), BF)], compiler_params=_cp("parallel"),
    )(x, g, *deps)


def _rmsnorm_bwd(x, g, dh, dres, out_scale, name):
    s, d = x.shape
    tm = 256

    def body(x_ref, g_ref, dh_ref, dres_ref, dx_ref, dxb_ref, dxbt_ref, dg_ref):
        xf = x_ref[...]
        r = lax.rsqrt(jnp.mean(xf * xf, axis=-1, keepdims=True) + RMS_EPS)
        xhat = xf * r
        dhv = dh_ref[...]
        dxhat = dhv * g_ref[...]
        dx = dres_ref[...] + r * (dxhat - xhat * jnp.mean(dxhat * xhat, axis=-1, keepdims=True))
        dx_ref[...] = dx
        scaled = dx * out_scale
        dxb_ref[...] = scaled.astype(BF)
        dxbt_ref[...] = scaled.T.astype(BF)

        @pl.when(pl.program_id(0) == 0)
        def _():
            dg_ref[...] = jnp.zeros_like(dg_ref)

        dg_ref[...] += jnp.sum(dhv * xhat, axis=0, keepdims=True)

    row = pl.BlockSpec((tm, d), lambda i: (i, 0))
    vec = pl.BlockSpec((1, d), lambda i: (0, 0))
    return pl.pallas_call(
        body, name=name, grid=(s // tm,),
        in_specs=[row, vec, row, row],
        out_specs=[row, row, pl.BlockSpec((d, tm), lambda i: (0, i)), vec],
        out_shape=[_sds((s, d), F32), _sds((s, d), BF), _sds((d, s), BF), _sds((1, d), F32)],
        compiler_params=_cp("arbitrary"),
    )(x, g, dh, dres)


def _loss_head(x, g, target, name):
    s, d = x.shape
    tm = 256

    def body(x_ref, g_ref, t_ref, dx_ref, dxb_ref, dxbt_ref, dg_ref, loss_ref):
        xf = x_ref[...]
        r = lax.rsqrt(jnp.mean(xf * xf, axis=-1, keepdims=True) + RMS_EPS)
        xhat = xf * r
        err = xhat * g_ref[...] - t_ref[...]
        dy = err * (1.0 / d)
        dxhat = dy * g_ref[...]
        dx = r * (dxhat - xhat * jnp.mean(dxhat * xhat, axis=-1, keepdims=True))
        dx_ref[...] = dx
        half = dx * 0.5
        dxb_ref[...] = half.astype(BF)
        dxbt_ref[...] = half.T.astype(BF)

        @pl.when(pl.program_id(0) == 0)
        def _():
            dg_ref[...] = jnp.zeros_like(dg_ref)
            loss_ref[...] = jnp.zeros_like(loss_ref)

        dg_ref[...] += jnp.sum(dy * xhat, axis=0, keepdims=True)
        part = 0.5 * jnp.sum(jnp.mean(err * err, axis=-1, keepdims=True), axis=0, keepdims=True)
        lane = lax.broadcasted_iota(jnp.int32, (1, 128), 1)
        loss_ref[...] += jnp.where(lane == 0, part, 0.0)

    row = pl.BlockSpec((tm, d), lambda i: (i, 0))
    vec = pl.BlockSpec((1, d), lambda i: (0, 0))
    return pl.pallas_call(
        body, name=name, grid=(s // tm,),
        in_specs=[row, vec, row],
        out_specs=[row, row, pl.BlockSpec((d, tm), lambda i: (0, i)), vec, pl.BlockSpec((1, 128), lambda i: (0, 0))],
        out_shape=[_sds((s, d), F32), _sds((s, d), BF), _sds((d, s), BF), _sds((1, d), F32), _sds((1, 128), F32)],
        compiler_params=_cp("arbitrary"),
    )(x, g, target)


def _act_spec(tm, n, natural, order):
    if natural:
        return pl.BlockSpec((tm, n), (lambda s, i: (i, s)) if order == "si" else (lambda i, s: (i, s)))
    return pl.BlockSpec((None, tm, n), (lambda s, i: (s, i, 0)) if order == "si" else (lambda i, s: (s, i, 0)))


def _act_shape(s, n, natural, dtype):
    return _sds((s, NDEV * n), dtype) if natural else _sds((NDEV, s, n), dtype)


def _ffn_up(h, wg, wu, name):
    s, d = h.shape
    n = wg.shape[2]
    tm = 512

    def body(h_ref, wg_ref, wu_ref, g_ref, u_ref, a_ref):
        hb = h_ref[...]
        g = _dot(hb, wg_ref[...])
        u = _dot(hb, wu_ref[...])
        g_ref[...] = g.astype(BF)
        u_ref[...] = u.astype(BF)
        a_ref[...] = (g * jax.nn.sigmoid(g) * u).astype(BF)

    wsp = pl.BlockSpec((None, d, n), lambda s_, i: (s_, 0, 0))
    blk = _act_spec(tm, n, False, "si")
    return pl.pallas_call(
        body, name=name, grid=(NDEV, s // tm),
        in_specs=[pl.BlockSpec((tm, d), lambda s_, i: (i, 0)), wsp, wsp],
        out_specs=[blk] * 3, out_shape=[_act_shape(s, n, False, BF)] * 3,
        compiler_params=_cp("parallel", "parallel"),
    )(h, wg, wu)


def _ffn_down(act, wd, x, name):
    _, s, n = act.shape
    d = wd.shape[2]
    tm = 512

    def body(a_ref, w_ref, x_ref, o_ref, acc):
        k = pl.program_id(1)

        @pl.when(k == 0)
        def _():
            acc[...] = jnp.zeros_like(acc)

        acc[...] += _dot(a_ref[...], w_ref[...])

        @pl.when(k == NDEV - 1)
        def _():
            o_ref[...] = x_ref[...] + 0.5 * acc[...]

    row = pl.BlockSpec((tm, d), lambda i, k: (i, 0))
    return pl.pallas_call(
        body, name=name, grid=(s // tm, NDEV),
        in_specs=[_act_spec(tm, n, False, "is"), pl.BlockSpec((None, n, d), lambda i, k: (k, 0, 0)), row],
        out_specs=row, out_shape=_sds((s, d), F32),
        scratch_shapes=[pltpu.VMEM((tm, d), F32)], compiler_params=_cp("parallel", "arbitrary"),
    )(act, wd, x)


def _ffn_bwd_act(dyb, wd, g, u, name, deps=()):
    s, d = dyb.shape
    n = wd.shape[1]
    tm = 512

    def body(dy_ref, w_ref, g_ref, u_ref, dg_ref, du_ref):
        dact = _dot_nt(dy_ref[...], w_ref[...])
        gv = g_ref[...].astype(F32)
        uv = u_ref[...].astype(F32)
        sig = jax.nn.sigmoid(gv)
        dg_ref[...] = (dact * uv * (sig * (1.0 + gv * (1.0 - sig)))).astype(BF)
        du_ref[...] = (dact * (gv * sig)).astype(BF)

    blk = _act_spec(tm, n, False, "si")
    return pl.pallas_call(
        _ignore_deps(body, 4, len(deps)), name=name, grid=(NDEV, s // tm),
        in_specs=[pl.BlockSpec((tm, d), lambda s_, i: (i, 0)), pl.BlockSpec((None, n, d), lambda s_, i: (s_, 0, 0)), blk, blk]
        + [TOKEN_SPEC] * len(deps),
        out_specs=[blk, blk], out_shape=[_act_shape(s, n, False, BF)] * 2,
        compiler_params=_cp("parallel", "parallel"),
    )(dyb, wd, g, u, *deps)


def _grad_rows(act_t, dyb, name):
    _, n, s = act_t.shape
    d = dyb.shape[1]
    tn = 1024

    def body(a_ref, dy_ref, o_ref):
        o_ref[...] = _dot(a_ref[...], dy_ref[...]).astype(BF)

    return pl.pallas_call(
        body, name=name, grid=(NDEV, d // tn),
        in_specs=[pl.BlockSpec((None, n, s), lambda k, j: (k, 0, 0)), pl.BlockSpec((s, tn), lambda k, j: (0, j))],
        out_specs=pl.BlockSpec((None, n, tn), lambda k, j: (k, 0, j)), out_shape=_sds((NDEV, n, d), BF),
        compiler_params=_cp("parallel", "parallel"),
    )(act_t, dyb)


def _transpose_blocked(act, name):
    _, s, n = act.shape
    tm = 512

    def body(a_ref, o_ref):
        o_ref[...] = a_ref[...].astype(F32).T.astype(BF)

    return pl.pallas_call(
        body, name=name, grid=(NDEV, s // tm),
        in_specs=[pl.BlockSpec((None, tm, n), lambda k, i: (k, i, 0))],
        out_specs=pl.BlockSpec((None, n, tm), lambda k, i: (k, 0, i)), out_shape=_sds((NDEV, n, s), BF),
        compiler_params=_cp("parallel", "parallel"),
    )(act)


def _grad_cols(ht, dxs, naturals, name):
    d, s = ht.shape
    k = len(dxs)
    ns = [dx.shape[1] // NDEV if nat else dx.shape[2] for dx, nat in zip(dxs, naturals)]
    td = 512

    def body(*refs):
        ht_ref, dx_refs, o_refs = refs[0], refs[1:1 + k], refs[1 + k:]
        hv = ht_ref[...]
        for dx_ref, o_ref in zip(dx_refs, o_refs):
            o_ref[...] = _dot(hv, dx_ref[...]).astype(BF)

    def dx_spec(n, nat):
        if nat:
            return pl.BlockSpec((s, n), lambda s_, j: (0, s_))
        return pl.BlockSpec((None, s, n), lambda s_, j: (s_, 0, 0))

    return pl.pallas_call(
        body, name=name, grid=(NDEV, d // td),
        in_specs=[pl.BlockSpec((td, s), lambda s_, j: (j, 0))] + [dx_spec(n, nat) for n, nat in zip(ns, naturals)],
        out_specs=[pl.BlockSpec((None, td, n), lambda s_, j: (s_, j, 0)) for n in ns],
        out_shape=[_sds((NDEV, d, n), BF) for n in ns], compiler_params=_cp("parallel", "parallel"),
    )(ht, *dxs)


def _dh_cols(dxs, ws, naturals, name, deps=()):
    k = len(dxs)
    d = ws[0].shape[1]
    ns = [w.shape[2] for w in ws]
    s = dxs[0].shape[0] if naturals[0] else dxs[0].shape[1]
    tm = 512

    def body(*refs):
        dx_refs, w_refs, o_ref, acc = refs[:k], refs[k:2 * k], refs[2 * k], refs[2 * k + 1]
        j = pl.program_id(1)

        @pl.when(j == 0)
        def _():
            acc[...] = jnp.zeros_like(acc)

        t = _dot_nt(dx_refs[0][...], w_refs[0][...])
        for dx_ref, w_ref in zip(dx_refs[1:], w_refs[1:]):
            t = t + _dot_nt(dx_ref[...], w_ref[...])
        acc[...] += t

        @pl.when(j == NDEV - 1)
        def _():
            o_ref[...] = acc[...]

    return pl.pallas_call(
        _ignore_deps(body, 2 * k, len(deps)), name=name, grid=(s // tm, NDEV),
        in_specs=[_act_spec(tm, n, nat, "is") for n, nat in zip(ns, naturals)]
        + [pl.BlockSpec((None, d, n), lambda i, j: (j, 0, 0)) for n in ns] + [TOKEN_SPEC] * len(deps),
        out_specs=pl.BlockSpec((tm, d), lambda i, j: (i, 0)), out_shape=_sds((s, d), F32),
        scratch_shapes=[pltpu.VMEM((tm, d), F32)], compiler_params=_cp("parallel", "arbitrary"),
    )(*dxs, *ws, *deps)


def _mm_nn(a, b, tn, out_dtype, name, res=None, tm=512):
    m, k = a.shape
    nn = b.shape[1]

    def body(*refs):
        if res is None:
            a_ref, b_ref, o_ref = refs
            o_ref[...] = _dot(a_ref[...], b_ref[...]).astype(out_dtype)
        else:
            a_ref, b_ref, r_ref, o_ref = refs
            o_ref[...] = (r_ref[...] + _dot(a_ref[...], b_ref[...])).astype(out_dtype)

    osp = pl.BlockSpec((tm, tn), lambda j, i: (i, j))
    in_specs = [pl.BlockSpec((tm, k), lambda j, i: (i, 0)), pl.BlockSpec((k, tn), lambda j, i: (0, j))]
    args = [a, b]
    if res is not None:
        in_specs.append(osp)
        args.append(res)
    return pl.pallas_call(
        body, name=name, grid=(nn // tn, m // tm), in_specs=in_specs, out_specs=osp,
        out_shape=_sds((m, nn), out_dtype), compiler_params=_cp("parallel", "parallel"),
    )(*args)


def _mm_nt(pairs, name, out_dtype=F32, tm=512, tk=512, deps=()):
    m = pairs[0][0].shape[0]
    kk = pairs[0][1].shape[0]
    p = len(pairs)

    def body(*refs):
        o_ref = refs[2 * p]
        t = _dot_nt(refs[0][...], refs[1][...])
        for q in range(1, p):
            t = t + _dot_nt(refs[2 * q][...], refs[2 * q + 1][...])
        o_ref[...] = t.astype(out_dtype)

    in_specs, args = [], []
    for a, b in pairs:
        in_specs += [pl.BlockSpec((tm, a.shape[1]), lambda j, i: (i, 0)), pl.BlockSpec((tk, b.shape[1]), lambda j, i: (j, 0))]
        args += [a, b]
    return pl.pallas_call(
        _ignore_deps(body, 2 * p, len(deps)), name=name, grid=(kk // tk, m // tm), in_specs=in_specs + [TOKEN_SPEC] * len(deps),
        out_specs=pl.BlockSpec((tm, tk), lambda j, i: (i, j)), out_shape=_sds((m, kk), out_dtype),
        compiler_params=_cp("parallel", "parallel"),
    )(*args, *deps)


def _rope_tables(s, sign):
    half = ROPE_DIMS // 2
    freqs = ROPE_THETA ** (-jnp.arange(half, dtype=F32) / half)
    ang = jnp.arange(s, dtype=F32)[:, None] * freqs[None, :]
    cos, sin = jnp.cos(ang), sign * jnp.sin(ang)
    one = jnp.ones((s, HD - ROPE_DIMS), F32)
    zero = jnp.zeros((s, HD - ROPE_DIMS), F32)
    zh = jnp.zeros((s, half), F32)
    c = jnp.concatenate([cos, cos, one], axis=1)
    sa = jnp.concatenate([-sin, zh, zero], axis=1)
    sb = jnp.concatenate([zh, sin, zero], axis=1)
    return c, sa, sb


def _rope(xv, c, sa, sb):
    return xv * c + pltpu.roll(xv, HD - ROPE_DIMS // 2, 1) * sa + pltpu.roll(xv, ROPE_DIMS // 2, 1) * sb


def _qkv_rope(h, w, tables, name):
    s, d = h.shape
    n = w.shape[2]
    per = n // HD
    tm = 512

    def body(h_ref, w_ref, c_ref, sa_ref, sb_ref, o_ref):
        shard = pl.program_id(0)
        y = _dot(h_ref[...], w_ref[...])
        c, sa, sb = c_ref[...], sa_ref[...], sb_ref[...]
        for j in range(per):
            blk = y[:, j * HD:(j + 1) * HD]
            rot = _rope(blk, c, sa, sb)
            is_qk = shard * per + j < 2 * N_HEADS
            o_ref[:, j * HD:(j + 1) * HD] = jnp.where(is_qk, rot, blk).astype(BF)

    tab = pl.BlockSpec((tm, HD), lambda s_, i: (i, 0))
    return pl.pallas_call(
        body, name=name, grid=(NDEV, s // tm),
        in_specs=[pl.BlockSpec((tm, d), lambda s_, i: (i, 0)), pl.BlockSpec((None, d, n), lambda s_, i: (s_, 0, 0)), tab, tab, tab],
        out_specs=pl.BlockSpec((tm, n), lambda s_, i: (i, s_)), out_shape=_sds((s, NDEV * n), BF),
        compiler_params=_cp("parallel", "parallel"),
    )(h, w, *tables)


def _iota2():
    return (lax.broadcasted_iota(jnp.int32, (QB, QB), 0), lax.broadcasted_iota(jnp.int32, (QB, QB), 1))


def _softplus(z):
    return jnp.maximum(z, 0.0) + jnp.log(1.0 + jnp.exp(-jnp.abs(z)))


def _tri_dot(xv, tri, left=False):
    hi = xv.astype(BF)
    r1 = xv - hi.astype(F32)
    mid = r1.astype(BF)
    lo = (r1 - mid.astype(F32)).astype(BF)
    if left:
        return _dot(tri, hi) + _dot(tri, mid) + _dot(tri, lo)
    return _dot(hi, tri) + _dot(mid, tri) + _dot(lo, tri)


def _col(ref_or_val):
    return ref_or_val[:, 0:1]


KT = 4 * QB


def _iota_tile():
    return (lax.broadcasted_iota(jnp.int32, (QB, KT), 0), lax.broadcasted_iota(jnp.int32, (QB, KT), 1))


def _scan_matrix(keep):
    tri = keep(*_iota2()).astype(BF)
    return jnp.concatenate([tri, tri], axis=0)


def _scan_dot(xv, tri2):
    hi = xv.astype(BF)
    lo = (xv - hi.astype(F32)).astype(BF)
    return _dot(jnp.concatenate([hi, lo], axis=1), tri2)


def _blocks(xv):
    return [xv[:, b * QB:(b + 1) * QB] for b in range(KT // QB)]


def _sb_fwd(qkv, name):
    s = qkv.shape[0]
    nb = s // QB

    def body(q_ref, k_ref, v_ref, o_ref, ot_ref, t_ref):
        i = pl.program_id(1)
        q = q_ref[...]
        row, col = _iota_tile()
        later_keys = _scan_matrix(lambda j, s_: j > s_)
        last = i // (KT // QB)

        def step(tt, carry):
            acc, later = carry
            t = last - tt
            off = pl.multiple_of(t * KT, KT)
            k = k_ref[pl.ds(off, KT), :]
            v = v_ref[pl.ds(off, KT), :]
            z = _dot_nt(q, k) * SCALE
            strict = row + (i * QB - t * KT) > col
            sp = _softplus(z)
            lnb = jnp.where(strict, -sp, 0.0)
            afters = []
            for xb in reversed(_blocks(lnb)):
                afters.append(later + _scan_dot(xb, later_keys))
                later = later + jnp.sum(xb, axis=1, keepdims=True)
            after = jnp.concatenate(afters[::-1], axis=1)
            w = jnp.where(strict, jnp.exp((z - sp) + after), 0.0)
            return acc + _dot(w.astype(BF), v), later

        acc, total = lax.fori_loop(0, last + 1, step, (jnp.zeros((QB, HD), F32), jnp.zeros((QB, 1), F32)))
        o_ref[...] = acc.astype(BF)
        ot_ref[...] = acc.T.astype(BF)
        t_ref[...] = jnp.broadcast_to(total, (QB, HD))

    blk = pl.BlockSpec((QB, HD), lambda h, i: (i, h))
    return pl.pallas_call(
        body, name=name, grid=(N_SB, nb),
        in_specs=[blk, pl.BlockSpec((s, HD), lambda h, i: (0, N_HEADS + h)), pl.BlockSpec((s, HD), lambda h, i: (0, 2 * N_HEADS + h))],
        out_specs=[blk, pl.BlockSpec((HD, QB), lambda h, i: (h, i)), blk],
        out_shape=[_sds((s, N_SB * HD), BF), _sds((N_SB * HD, s), BF), _sds((s, N_SB * HD), F32)],
        compiler_params=_cp("parallel", "parallel"),
    )(qkv, qkv, qkv)


def _sb_bwd(qkv, do, total, name):
    s = qkv.shape[0]
    nb = s // QB

    def body(q_ref, k_ref, v_ref, do_ref, t_ref, dq_ref, dk_ref, dv_ref, dk_acc, dv_acc):
        i = pl.program_id(1)

        @pl.when(i == 0)
        def _():
            dk_acc[...] = jnp.zeros_like(dk_acc)
            dv_acc[...] = jnp.zeros_like(dv_acc)

        q = q_ref[...]
        dov = do_ref[...]
        tot = _col(t_ref[...])
        row, col = _iota_tile()
        keys_upto = _scan_matrix(lambda j, s_: j <= s_)
        keys_before = _scan_matrix(lambda j, s_: j < s_)

        def step(t, carry):
            dq, lnb_before, dl_before = carry
            off = pl.multiple_of(t * KT, KT)
            k = k_ref[pl.ds(off, KT), :]
            v = v_ref[pl.ds(off, KT), :]
            z = _dot_nt(q, k) * SCALE
            strict = row + (i * QB - t * KT) > col
            sp = _softplus(z)
            lnb = jnp.where(strict, -sp, 0.0)
            afters = []
            for xb in _blocks(lnb):
                afters.append(tot - (lnb_before + _scan_dot(xb, keys_upto)))
                lnb_before = lnb_before + jnp.sum(xb, axis=1, keepdims=True)
            a = jnp.where(strict, jnp.exp((z - sp) + jnp.concatenate(afters, axis=1)), 0.0)
            dl = a * _dot_nt(dov, v)
            befores = []
            for xb in _blocks(dl):
                befores.append(dl_before + _scan_dot(xb, keys_before))
                dl_before = dl_before + jnp.sum(xb, axis=1, keepdims=True)
            sig = jnp.exp(z - sp)
            dz = jnp.where(strict, dl * (1.0 - sig) - sig * jnp.concatenate(befores, axis=1), 0.0) * SCALE
            dq = dq + _dot(dz.astype(BF), k)
            dk_acc[pl.ds(off, KT), :] += _dot(dz.T.astype(BF), q)
            dv_acc[pl.ds(off, KT), :] += _dot(a.T.astype(BF), dov)
            return dq, lnb_before, dl_before

        zero = jnp.zeros((QB, 1), F32)
        dq, _, _ = lax.fori_loop(0, i // (KT // QB) + 1, step, (jnp.zeros((QB, HD), F32), zero, zero))
        dq_ref[...] = dq.astype(BF)

        @pl.when(i == nb - 1)
        def _():
            dk_ref[...] = dk_acc[...].astype(BF)
            dv_ref[...] = dv_acc[...].astype(BF)

    blk = pl.BlockSpec((QB, HD), lambda h, i: (i, h))
    full = pl.BlockSpec((s, HD), lambda h, i: (0, h))
    return pl.pallas_call(
        body, name=name, grid=(N_SB, nb),
        in_specs=[blk, pl.BlockSpec((s, HD), lambda h, i: (0, N_HEADS + h)), pl.BlockSpec((s, HD), lambda h, i: (0, 2 * N_HEADS + h)), blk, blk],
        out_specs=[blk, full, full], out_shape=[_sds((s, N_SB * HD), BF)] * 3,
        scratch_shapes=[pltpu.VMEM((s, HD), F32), pltpu.VMEM((s, HD), F32)],
        compiler_params=_cp("parallel", "arbitrary"),
    )(qkv, qkv, qkv, do, total)


def _fgate_fwd(f, b, name):
    s = f.shape[0]
    nb = s // QB
    nfox = N_HEADS - N_SB

    def body(f_ref, b_ref, cb_ref, ct_ref):
        row, col = _iota2()
        upto = (row >= col).astype(BF)
        carry = jnp.zeros((1, HD), F32)
        for blk in range(nb):
            xv = f_ref[blk * QB:(blk + 1) * QB, :] + b_ref[...]
            logf = -_softplus(-xv)
            cum = _tri_dot(logf, upto, left=True) + carry
            carry = cum[QB - 1:QB, :]
            ct_ref[blk] = cum.T
            for h in range(nfox):
                cb_ref[blk * QB:(blk + 1) * QB, h * HD:(h + 1) * HD] = jnp.broadcast_to(cum[:, h:h + 1], (QB, HD))

    return pl.pallas_call(
        body, name=name, out_shape=[_sds((s, nfox * HD), F32), _sds((nb, HD, HD), F32)], compiler_params=_cp(),
    )(f, b)


def _fgate_bwd(dcq, dck, f, b, name):
    s = f.shape[0]
    nb = s // QB
    nfox = N_HEADS - N_SB

    def body(dcq_ref, dck_ref, f_ref, b_ref, df_ref, db_ref):
        row, col = _iota2()
        from_tri = (row <= col).astype(BF)
        lane = col
        carry = jnp.zeros((1, HD), F32)
        db = jnp.zeros((1, HD), F32)
        for blk in reversed(range(nb)):
            dcum = jnp.zeros((QB, HD), F32)
            for h in range(nfox):
                here = (slice(blk * QB, (blk + 1) * QB), slice(h * HD, (h + 1) * HD))
                dcum = jnp.where(lane == h, dcq_ref[here] - dck_ref[here], dcum)
            dlogf = _tri_dot(dcum, from_tri, left=True) + carry
            carry = dlogf[0:1, :]
            xv = f_ref[blk * QB:(blk + 1) * QB, :] + b_ref[...]
            sp = _softplus(xv)
            df = jnp.where(lane < nfox, dlogf * jnp.exp(-sp), 0.0)
            df_ref[blk * QB:(blk + 1) * QB, :] = df.astype(BF)
            db = db + jnp.sum(df, axis=0, keepdims=True)
        db_ref[...] = db

    return pl.pallas_call(
        body, name=name, out_shape=[_sds((s, HD), BF), _sds((1, HD), F32)], compiler_params=_cp(),
    )(dcq, dck, f, b)


def _fox_head_row(ct_ref, j, h):
    tile = ct_ref[j]
    sub = lax.broadcasted_iota(jnp.int32, tile.shape, 0)
    return jnp.sum(jnp.where(sub == h, tile, 0.0), axis=0, keepdims=True)


def _fox_tile_row(ct_ref, t, h):
    nsub = KT // QB
    return jnp.concatenate([_fox_head_row(ct_ref, t * nsub + b, h) for b in range(nsub)], axis=1)


def _fox_fwd(qkv, cum_b, cum_t, name):
    s = qkv.shape[0]
    nb = s // QB
    nfox = N_HEADS - N_SB

    def body(q_ref, k_ref, v_ref, cq_ref, ct_ref, o_ref, ot_ref, lse_ref):
        h, i = pl.program_id(0), pl.program_id(1)
        q = q_ref[...]
        cq = _col(cq_ref[...])
        row, col = _iota_tile()

        def step(t, carry):
            acc, m, l = carry
            off = pl.multiple_of(t * KT, KT)
            k = k_ref[pl.ds(off, KT), :]
            v = v_ref[pl.ds(off, KT), :]
            z = _dot_nt(q, k) * SCALE + cq - _fox_tile_row(ct_ref, t, h)
            z = jnp.where(row + (i * QB - t * KT) >= col, z, NEG_INF)
            m_new = jnp.maximum(m, jnp.max(z, axis=1, keepdims=True))
            alpha = jnp.exp(m - m_new)
            p = jnp.exp(z - m_new)
            l = alpha * l + jnp.sum(p, axis=1, keepdims=True)
            acc = alpha * acc + _dot(p.astype(BF), v)
            return acc, m_new, l

        acc, m, l = lax.fori_loop(0, i // (KT // QB) + 1, step,
                                  (jnp.zeros((QB, HD), F32), jnp.full((QB, 1), NEG_INF, F32), jnp.zeros((QB, 1), F32)))
        o = acc / l
        o_ref[...] = o.astype(BF)
        ot_ref[...] = o.T.astype(BF)
        lse_ref[...] = jnp.broadcast_to(m + jnp.log(l), (QB, HD))

    blk = pl.BlockSpec((QB, HD), lambda h, i: (i, h))
    return pl.pallas_call(
        body, name=name, grid=(nfox, nb),
        in_specs=[pl.BlockSpec((QB, HD), lambda h, i: (i, N_SB + h)),
                  pl.BlockSpec((s, HD), lambda h, i: (0, N_HEADS + N_SB + h)),
                  pl.BlockSpec((s, HD), lambda h, i: (0, 2 * N_HEADS + N_SB + h)),
                  blk, pl.BlockSpec((nb, 8, HD), lambda h, i: (0, 0, 0))],
        out_specs=[blk, pl.BlockSpec((HD, QB), lambda h, i: (h, i)), blk],
        out_shape=[_sds((s, nfox * HD), BF), _sds((nfox * HD, s), BF), _sds((s, nfox * HD), F32)],
        compiler_params=_cp("parallel", "parallel"),
    )(qkv, qkv, qkv, cum_b, cum_t)


def _fox_bwd(qkv, cum_b, cum_t, o, lse, do, name):
    s = qkv.shape[0]
    nb = s // QB
    nfox = N_HEADS - N_SB

    def body(q_ref, k_ref, v_ref, cq_ref, ct_ref, o_ref, lse_ref, do_ref, dq_ref, dk_ref, dv_ref, dcq_ref, dc_ref, dk_acc, dv_acc, dc_acc):
        h, i = pl.program_id(0), pl.program_id(1)

        @pl.when(i == 0)
        def _():
            dk_acc[...] = jnp.zeros_like(dk_acc)
            dv_acc[...] = jnp.zeros_like(dv_acc)
            dc_acc[...] = jnp.zeros_like(dc_acc)

        q = q_ref[...]
        cq = _col(cq_ref[...])
        dov = do_ref[...]
        lse_c = _col(lse_ref[...])
        delta = jnp.sum(dov.astype(F32) * o_ref[...].astype(F32), axis=1, keepdims=True)
        row, col = _iota_tile()
        ones = jnp.ones((QB, HD), BF)

        def step(t, carry):
            dq, over_keys = carry
            off = pl.multiple_of(t * KT, KT)
            k = k_ref[pl.ds(off, KT), :]
            v = v_ref[pl.ds(off, KT), :]
            z = _dot_nt(q, k) * SCALE + cq - _fox_tile_row(ct_ref, t, h)
            p = jnp.where(row + (i * QB - t * KT) >= col, jnp.exp(z - lse_c), 0.0)
            dz = p * (_dot_nt(dov, v) - delta)
            dzt = dz.T
            dq = dq + _dot((dz * SCALE).astype(BF), k)
            dk_acc[pl.ds(off, KT), :] += _dot((dzt * SCALE).astype(BF), q)
            dv_acc[pl.ds(off, KT), :] += _dot(p.T.astype(BF), dov)
            dc_acc[pl.ds(off, KT), :] += _tri_dot(dzt, ones)
            return dq, over_keys + jnp.sum(dz, axis=1, keepdims=True)

        dq, over_keys = lax.fori_loop(0, i // (KT // QB) + 1, step, (jnp.zeros((QB, HD), F32), jnp.zeros((QB, 1), F32)))
        dq_ref[...] = dq.astype(BF)
        dcq_ref[...] = jnp.broadcast_to(over_keys, (QB, HD))

        @pl.when(i == nb - 1)
        def _():
            dk_ref[...] = dk_acc[...].astype(BF)
            dv_ref[...] = dv_acc[...].astype(BF)
            dc_ref[...] = dc_acc[...]

    blk = pl.BlockSpec((QB, HD), lambda h, i: (i, h))
    full = pl.BlockSpec((s, HD), lambda h, i: (0, h))
    return pl.pallas_call(
        body, name=name, grid=(nfox, nb),
        in_specs=[pl.BlockSpec((QB, HD), lambda h, i: (i, N_SB + h)),
                  pl.BlockSpec((s, HD), lambda h, i: (0, N_HEADS + N_SB + h)),
                  pl.BlockSpec((s, HD), lambda h, i: (0, 2 * N_HEADS + N_SB + h)),
                  blk, pl.BlockSpec((nb, 8, HD), lambda h, i: (0, 0, 0)), blk, blk,
                  pl.BlockSpec((QB, HD), lambda h, i: (i, N_SB + h))],
        out_specs=[blk, full, full, blk, full],
        out_shape=[_sds((s, nfox * HD), BF)] * 3 + [_sds((s, nfox * HD), F32)] * 2,
        scratch_shapes=[pltpu.VMEM((s, HD), F32)] * 3,
        compiler_params=_cp("parallel", "arbitrary"),
    )(qkv, qkv, qkv, cum_b, cum_t, o, lse, do)


def _dil_rows(r, n, d):
    if d == 1:
        return pl.ds(pl.multiple_of(n * QB, QB), QB)
    return pl.ds(r + n * (QB * d), QB, stride=d)


def _dilated_fwd(qkv, name):
    s = qkv.shape[0]
    npat = len(DILATED_PATTERNS)
    chunk = 256

    def body(q_ref, k_ref, v_ref, out_ref, outt_ref, g_ref, qf, kf, vf, *per_pattern):
        o_s, l_s = per_pattern[:npat], per_pattern[npat:]
        qf[...] = q_ref[...].astype(F32)
        kf[...] = k_ref[...].astype(F32)
        vf[...] = v_ref[...].astype(F32)
        row, col = _iota2()
        for p, (_, d) in enumerate(DILATED_PATTERNS):
            nb = s // d // QB

            def blk(idx, carry, p=p, d=d, nb=nb):
                r, n = idx // nb, idx % nb
                cur = _dil_rows(r, n, d)
                q = qf[cur, :].astype(BF)
                zc = jnp.where(col <= row, _dot_nt(q, kf[cur, :].astype(BF)) * SCALE, NEG_INF)
                m = jnp.max(zc, axis=1, keepdims=True)
                if nb > 1:
                    prv = _dil_rows(r, jnp.maximum(n - 1, 0), d)
                    mp = jnp.logical_and(col >= row, n >= 1)
                    zp = jnp.where(mp, _dot_nt(q, kf[prv, :].astype(BF)) * SCALE, NEG_INF)
                    m = jnp.maximum(m, jnp.max(zp, axis=1, keepdims=True))
                ec = jnp.exp(zc - m)
                l = jnp.sum(ec, axis=1, keepdims=True)
                if nb > 1:
                    ep = jnp.where(mp, jnp.exp(zp - m), 0.0)
                    l = l + jnp.sum(ep, axis=1, keepdims=True)
                o = _dot((ec / l).astype(BF), vf[cur, :].astype(BF))
                if nb > 1:
                    o = o + _dot((ep / l).astype(BF), vf[prv, :].astype(BF))
                o_s[p][cur, :] = o
                l_s[p][cur, :] = jnp.broadcast_to(m + jnp.log(l), (QB, HD))
                return carry

            lax.fori_loop(0, s // QB, blk, 0, unroll=4)
        for c0 in range(0, s, chunk):
            rows = slice(c0, c0 + chunk)
            ls = [l_s[p][rows, :] for p in range(npat)]
            m = functools.reduce(jnp.maximum, ls)
            es = [jnp.exp(l - m) for l in ls]
            tot = functools.reduce(lambda a, b: a + b, es)
            out = functools.reduce(lambda a, b: a + b, [(e / tot) * o_s[p][rows, :] for p, e in enumerate(es)])
            out_ref[rows, :] = out.astype(BF)
            outt_ref[:, rows] = out.T.astype(BF)
            g_ref[rows, :] = m + jnp.log(tot)

    full = pl.BlockSpec((s, HD), lambda h: (0, h))
    return pl.pallas_call(
        body, name=name, grid=(N_HEADS,),
        in_specs=[full, pl.BlockSpec((s, HD), lambda h: (0, N_HEADS + h)), pl.BlockSpec((s, HD), lambda h: (0, 2 * N_HEADS + h))],
        out_specs=[full, pl.BlockSpec((HD, s), lambda h: (h, 0)), full],
        out_shape=[_sds((s, N_HEADS * HD), BF), _sds((N_HEADS * HD, s), BF), _sds((s, N_HEADS * HD), F32)],
        scratch_shapes=[pltpu.VMEM((s, HD), F32)] * (3 + 2 * npat),
        compiler_params=_cp("parallel"),
    )(qkv, qkv, qkv)


def _dilated_bwd(qkv, out, glse, do, tables, name):
    s = qkv.shape[0]
    chunk = 256

    def body(q_ref, k_ref, v_ref, out_ref, g_ref, do_ref, c_ref, sa_ref, sb_ref, dq_ref, dk_ref, dv_ref,
             qf, kf, vf, dof, dl_s, dq_a, dk_a, dv_a):
        qf[...] = q_ref[...].astype(F32)
        kf[...] = k_ref[...].astype(F32)
        vf[...] = v_ref[...].astype(F32)
        for c0 in range(0, s, chunk):
            rows = slice(c0, c0 + chunk)
            dov = do_ref[rows, :].astype(F32)
            dof[rows, :] = dov
            dl_s[rows, :] = jnp.broadcast_to(jnp.sum(dov * out_ref[rows, :].astype(F32), axis=1, keepdims=True), (chunk, HD))
        dq_a[...] = jnp.zeros_like(dq_a)
        dk_a[...] = jnp.zeros_like(dk_a)
        dv_a[...] = jnp.zeros_like(dv_a)
        row, col = _iota2()
        for _, d in DILATED_PATTERNS:
            nb = s // d // QB

            def blk(idx, carry, d=d, nb=nb):
                r, n = idx // nb, idx % nb
                cur = _dil_rows(r, n, d)
                q = qf[cur, :].astype(BF)
                kc = kf[cur, :].astype(BF)
                dov = dof[cur, :].astype(BF)
                g = _col(g_ref[cur, :])
                delta = _col(dl_s[cur, :])
                pc = jnp.where(col <= row, jnp.exp(_dot_nt(q, kc) * SCALE - g), 0.0)
                dzc = pc * (_dot_nt(dov, vf[cur, :].astype(BF)) - delta) * SCALE
                dq = _dot(dzc.astype(BF), kc)
                if nb > 1:
                    prv = _dil_rows(r, jnp.maximum(n - 1, 0), d)
                    kp = kf[prv, :].astype(BF)
                    mp = jnp.logical_and(col >= row, n >= 1)
                    pp = jnp.where(mp, jnp.exp(_dot_nt(q, kp) * SCALE - g), 0.0)
                    dzp = pp * (_dot_nt(dov, vf[prv, :].astype(BF)) - delta) * SCALE
                    dq = dq + _dot(dzp.astype(BF), kp)
                dq_a[cur, :] += dq
                dk_a[cur, :] += _dot(dzc.T.astype(BF), q)
                dv_a[cur, :] += _dot(pc.T.astype(BF), dov)
                if nb > 1:
                    dk_a[prv, :] += _dot(dzp.T.astype(BF), q)
                    dv_a[prv, :] += _dot(pp.T.astype(BF), dov)
                return carry

            lax.fori_loop(0, s // QB, blk, 0, unroll=2)
        for c0 in range(0, s, chunk):
            rows = slice(c0, c0 + chunk)
            c, sa, sb = c_ref[rows, :], sa_ref[rows, :], sb_ref[rows, :]
            dq_ref[rows, :] = _rope(dq_a[rows, :], c, sa, sb).astype(BF)
            dk_ref[rows, :] = _rope(dk_a[rows, :], c, sa, sb).astype(BF)
            dv_ref[rows, :] = dv_a[rows, :].astype(BF)

    full = pl.BlockSpec((s, HD), lambda h: (0, h))
    tab = pl.BlockSpec((s, HD), lambda h: (0, 0))
    return pl.pallas_call(
        body, name=name, grid=(N_HEADS,),
        in_specs=[full, pl.BlockSpec((s, HD), lambda h: (0, N_HEADS + h)), pl.BlockSpec((s, HD), lambda h: (0, 2 * N_HEADS + h)),
                  full, full, full, tab, tab, tab],
        out_specs=[full, full, full], out_shape=[_sds((s, N_HEADS * HD), BF)] * 3,
        scratch_shapes=[pltpu.VMEM((s, HD), F32)] * 8,
        compiler_params=_cp("parallel"),
    )(qkv, qkv, qkv, out, glse, do, *tables)


def _swiglu_fwd(x, gnorm, w, tag, deps=()):
    h, ht = _rmsnorm_fwd(x, gnorm, f"norm_{tag}", deps)
    g, u, act = _ffn_up(h, w["gate"], w["up"], f"ffn_up_{tag}")
    y = _ffn_down(act, w["down"], x, f"ffn_down_{tag}")
    return y, (x, ht, g, u, act)


def _swiglu_bwd(saved, gnorm, w, dy, dyb_half, out_scale, tag, deps=(), on_grads=None):
    x, ht, g, u, act = saved
    dg, du = _ffn_bwd_act(dyb_half, w["down"], g, u, f"ffn_bwd_act_{tag}", deps)
    d_down = _grad_rows(_transpose_blocked(act, f"act_t_{tag}"), dyb_half, f"ffn_bwd_wd_{tag}")
    d_gate, d_up = _grad_cols(ht, [dg, du], [False, False], f"ffn_bwd_wgu_{tag}")
    gw = {"gate": d_gate, "up": d_up, "down": d_down}
    tokens = on_grads(gw) if on_grads else ()
    dh = _dh_cols([dg, du], [w["gate"], w["up"]], [False, False], f"ffn_bwd_dh_{tag}", tokens)
    dx, dxb, dxbt, dgn = _rmsnorm_bwd(x, gnorm, dh, dy, out_scale, f"norm_bwd_{tag}")
    return (dx, dxb, dxbt), dgn, gw


def kernel(x, norm_g, ffn1_w_gate, ffn1_w_up, ffn1_w_down, ffn2_w_gate, ffn2_w_up, ffn2_w_down, even_w_in, even_b_forget, even_w_out, odd_w_qkv, odd_w_out, final_norm_g, loss_target, m_norm_g, m_ffn1_w_gate, m_ffn1_w_up, m_ffn1_w_down, m_ffn2_w_gate, m_ffn2_w_up, m_ffn2_w_down, m_even_w_in, m_even_b_forget, m_even_w_out, m_odd_w_qkv, m_odd_w_out, m_final_norm_g, v_norm_g, v_ffn1_w_gate, v_ffn1_w_up, v_ffn1_w_down, v_ffn2_w_gate, v_ffn2_w_up, v_ffn2_w_down, v_even_w_in, v_even_b_forget, v_even_w_out, v_odd_w_qkv, v_odd_w_out, v_final_norm_g):
    s, d = x.shape[1], x.shape[2]
    nfox = N_HEADS - N_SB
    ax, ay, ac = lax.axis_index("x"), lax.axis_index("y"), lax.axis_index("c")
    me = 4 * ax + 2 * ay + ac
    slots = jnp.stack([4 * px + 2 * py + ac for px, py in [(ax, ay), (1 - ax, ay), (ax, 1 - ay), (1 - ax, 1 - ay)]]).astype(jnp.int32)
    x0 = x.reshape(s, d)
    target = loss_target.reshape(s, d)

    def bf(w):
        return w.astype(BF)

    groups = [
        [bf(ffn1_w_gate[0]), bf(ffn1_w_up[0]), bf(ffn1_w_down[0]), norm_g.reshape(6, d // NDEV)],
        [bf(even_w_in[0]), bf(even_w_out[0])],
        [bf(ffn2_w_gate[0]), bf(ffn2_w_up[0]), bf(ffn2_w_down[0])],
        [bf(ffn1_w_gate[1]), bf(ffn1_w_up[1]), bf(ffn1_w_down[1])],
        [bf(odd_w_qkv[0]), bf(odd_w_out[0])],
        [bf(ffn2_w_gate[1]), bf(ffn2_w_up[1]), bf(ffn2_w_down[1])],
    ]
    started = []
    for k, grp in enumerate(groups):
        started.append(_gather_start(grp, me, [started[-1]["token"]] if started else [], f"gather_start_{k}"))
    all_started = [started[-1]["token"]]

    def gathered(k, after):
        return _gather_finish(_gather_forward(started[k], after, f"gather_forward_{k}"), f"gather_finish_{k}")

    def ffn_weights(ws_):
        return {"gate": ws_[0], "up": ws_[1], "down": ws_[2]}

    b_pad = jnp.pad(even_b_forget, ((0, 0), (0, HD - nfox)))
    gfin = final_norm_g.reshape(1, d)

    g0 = gathered(0, x0)
    gn = jnp.transpose(g0[3], (1, 0, 2)).reshape(6, 1, d)
    wf = [[ffn_weights(g0), None], [None, None]]
    x1, sv_f1_0 = _swiglu_fwd(x0, gn[0], wf[0][0], "l0a", all_started)
    g1 = gathered(1, x1)
    w_in_nat = jnp.transpose(g1[0], (1, 0, 2)).reshape(d, -1)
    w_qkv_e = w_in_nat[:, :3 * d]
    w_f = jnp.pad(w_in_nat[:, 3 * d:], ((0, 0), (0, HD - nfox)))
    w_out_e = g1[1].reshape(d, d)
    h_e, ht_e = _rmsnorm_fwd(x1, gn[1], "norm_l0m")
    qkv_e = _mm_nn(h_e, w_qkv_e, 768, BF, "even_qkv")
    f_e = _mm_nn(h_e, w_f, HD, F32, "even_fgate")
    o_sb, ot_sb, tot_sb = _sb_fwd(qkv_e, "sb_fwd")
    cum_b, cum_t = _fgate_fwd(f_e, b_pad, "fgate_fwd")
    o_fox, ot_fox, lse_fox = _fox_fwd(qkv_e, cum_b, cum_t, "fox_fwd")
    o_e = jnp.concatenate([o_sb, o_fox], axis=1)
    ot_e = jnp.concatenate([ot_sb, ot_fox], axis=0)
    x2 = _mm_nn(o_e, w_out_e, 1024, F32, "even_out", res=x1)
    wf[0][1] = ffn_weights(gathered(2, x2))
    x3, sv_f2_0 = _swiglu_fwd(x2, gn[2], wf[0][1], "l0b")

    wf[1][0] = ffn_weights(gathered(3, x3))
    x4, sv_f1_1 = _swiglu_fwd(x3, gn[3], wf[1][0], "l1a")
    g4 = gathered(4, x4)
    w_qkv_o = g4[0]
    w_out_o = g4[1].reshape(d, d)
    h_o, ht_o = _rmsnorm_fwd(x4, gn[4], "norm_l1m")
    qkv_o = _qkv_rope(h_o, w_qkv_o, _rope_tables(s, 1.0), "odd_qkv")
    o_o, ot_o, glse = _dilated_fwd(qkv_o, "dilated_fwd")
    x5 = _mm_nn(o_o, w_out_o, 1024, F32, "odd_out", res=x4)
    wf[1][1] = ffn_weights(gathered(5, x5))
    x6, sv_f2_1 = _swiglu_fwd(x5, gn[5], wf[1][1], "l1b")

    def reduce_start(gs, tag):
        gs = [g_ if g_.ndim == 3 else g_.reshape(NDEV, g_.shape[0] // NDEV, g_.shape[1]) for g_ in gs]
        a_s = _pair_exchange(gs, f"pair_exchange_{tag}")
        ps = [_pair_sum(g_, a_, slots, f"pair_sum_{tag}_{k}") for k, (g_, a_) in enumerate(zip(gs, a_s))]
        return gs, a_s, _chip_start(ps, f"chip_start_{tag}")

    red = {}

    def reduce_ffn(tag):
        def on_grads(gw):
            red[tag] = reduce_start([gw["gate"], gw["up"], gw["down"]], tag)
            return [red[tag][2]["token"]]
        return on_grads

    dx6, dx6b, _, d_gfin, loss_part = _loss_head(x6, gfin, target, "loss_head")

    (dx5, dx5b, dx5bt), dgn5, _ = _swiglu_bwd(sv_f2_1, gn[5], wf[1][1], dx6, dx6b, 1.0, "l1b", on_grads=reduce_ffn("l1b"))
    d_wout_o = _mm_nn(ot_o, dx5b, 1024, BF, "odd_out_dw")
    do_o = _mm_nt([(dx5b, w_out_o)], "odd_out_do", BF)
    dqkv_o = jnp.concatenate(_dilated_bwd(qkv_o, o_o, glse, do_o, _rope_tables(s, -1.0), "dilated_bwd"), axis=1)
    (d_wqkv_o,) = _grad_cols(ht_o, [dqkv_o], [True], "odd_qkv_dw")
    dh_o = _dh_cols([dqkv_o], [w_qkv_o], [True], "odd_qkv_dh")
    dx4, dx4b, _, dgn4 = _rmsnorm_bwd(x4, gn[4], dh_o, dx5, 0.5, "norm_bwd_l1m")
    red["l1m"] = reduce_start([d_wqkv_o, d_wout_o], "l1m")
    (dx3, dx3b, _), dgn3, _ = _swiglu_bwd(sv_f1_1, gn[3], wf[1][0], dx4, dx4b, 0.5, "l1a", [red["l1m"][2]["token"]],
                                         on_grads=reduce_ffn("l1a"))

    (dx2, dx2b, dx2bt), dgn2, _ = _swiglu_bwd(sv_f2_0, gn[2], wf[0][1], dx3, dx3b, 1.0, "l0b", on_grads=reduce_ffn("l0b"))
    d_wout_e = _mm_nn(ot_e, dx2b, 1024, BF, "even_out_dw")
    do_e = _mm_nt([(dx2b, w_out_e)], "even_out_do", BF)
    dq_sb, dk_sb, dv_sb = _sb_bwd(qkv_e, do_e, tot_sb, "sb_bwd")
    dq_fx, dk_fx, dv_fx, dcq, dck = _fox_bwd(qkv_e, cum_b, cum_t, o_fox, lse_fox, do_e, "fox_bwd")
    df, db_part = _fgate_bwd(dcq, dck, f_e, b_pad, "fgate_bwd")
    dqkv_e = jnp.concatenate([dq_sb, dq_fx, dk_sb, dk_fx, dv_sb, dv_fx], axis=1)
    d_wqkv_e = _mm_nn(ht_e, dqkv_e, 768, BF, "even_qkv_dw")
    d_wf = _mm_nn(ht_e, df, HD, BF, "even_fgate_dw")
    dh_e = _mm_nt([(dqkv_e, w_qkv_e), (df, w_f)], "even_in_dh")
    dx1, dx1b, _, dgn1 = _rmsnorm_bwd(x1, gn[1], dh_e, dx2, 0.5, "norm_bwd_l0m")
    d_win_nat = jnp.concatenate([d_wqkv_e, d_wf[:, :nfox]], axis=1)
    d_win = jnp.transpose(d_win_nat.reshape(d, NDEV, -1), (1, 0, 2))
    red["l0m"] = reduce_start([d_win, d_wout_e], "l0m")
    (dx0, _, _), dgn0, _ = _swiglu_bwd(sv_f1_0, gn[0], wf[0][0], dx1, dx1b, 1.0, "l0a", [red["l0m"][2]["token"]],
                                      on_grads=reduce_ffn("l0a"))

    def reduce_finish(red, tag, after):
        gs, a_s, st = red
        return list(zip(gs, a_s, _chip_finish(st, after, f"chip_finish_{tag}")))

    f_l1b, f_l1m, f_l1a = (reduce_finish(red[t], t, dx0) for t in ("l1b", "l1m", "l1a"))
    f_l0b, f_l0m = (reduce_finish(red[t], t, dx0) for t in ("l0b", "l0m"))

    def update(w_, m_, v_, parts, nm):
        if w_.shape[2] % 128 == 0:
            return _adamw_sharded(w_, m_, v_, parts, slots, f"adamw_{nm}")
        outs = _adamw_sharded(jnp.swapaxes(w_, 1, 2), jnp.swapaxes(m_, 1, 2), jnp.swapaxes(v_, 1, 2), parts, slots,
                              f"adamw_{nm}", transposed=True)
        return [jnp.swapaxes(o, 1, 2) for o in outs]

    res = {}
    res["even_w_in"] = update(even_w_in, m_even_w_in, v_even_w_in, [f_l0m[0]], "even_w_in")
    res["even_w_out"] = _adamw_sharded(even_w_out, m_even_w_out, v_even_w_out, [f_l0m[1]], slots, "adamw_even_w_out")
    res["odd_w_qkv"] = _adamw_sharded(odd_w_qkv, m_odd_w_qkv, v_odd_w_qkv, [f_l1m[0]], slots, "adamw_odd_w_qkv")
    res["odd_w_out"] = _adamw_sharded(odd_w_out, m_odd_w_out, v_odd_w_out, [f_l1m[1]], slots, "adamw_odd_w_out")
    names = ["ffn2_w_gate", "ffn2_w_up", "ffn2_w_down", "ffn1_w_gate", "ffn1_w_up", "ffn1_w_down"]
    ws = [ffn2_w_gate, ffn2_w_up, ffn2_w_down, ffn1_w_gate, ffn1_w_up, ffn1_w_down]
    ms = [m_ffn2_w_gate, m_ffn2_w_up, m_ffn2_w_down, m_ffn1_w_gate, m_ffn1_w_up, m_ffn1_w_down]
    vs = [v_ffn2_w_gate, v_ffn2_w_up, v_ffn2_w_down, v_ffn1_w_gate, v_ffn1_w_up, v_ffn1_w_down]
    for k in range(3):
        res[names[k]] = update(ws[k], ms[k], vs[k], [f_l0b[k], f_l1b[k]], names[k])
    f_l0a = reduce_finish(red["l0a"], "l0a", res["ffn2_w_down"][1])
    for k in range(3, 6):
        res[names[k]] = update(ws[k], ms[k], vs[k], [f_l0a[k - 3], f_l1a[k - 3]], names[k])

    dnorm = jnp.concatenate([dgn0, dgn1, dgn2, dgn3, dgn4, dgn5], axis=0)
    nsm = d // NDEV
    small_rows = (6 * d + d + 2 * HD) // HD
    pad_rows = -small_rows % 8
    part = jnp.concatenate([dnorm.reshape(-1), d_gfin.reshape(-1), db_part.reshape(-1), loss_part.reshape(-1),
                            jnp.zeros((pad_rows * HD,), F32)]).reshape(small_rows + pad_rows, HD)
    (gathered,) = _all_gather([part], "gather_small")

    def pack(ng, bfg, fg):
        full = lax.dynamic_update_slice(jnp.zeros((6, d), F32), ng.reshape(6, nsm), (0, me * nsm))
        return jnp.concatenate([full.reshape(-1), fg.reshape(-1), jnp.pad(bfg.reshape(-1), (0, HD - nfox)),
                                jnp.zeros((HD + pad_rows * HD,), F32)]).reshape(small_rows + pad_rows, HD)

    sm = _adamw_small(pack(norm_g, even_b_forget, final_norm_g), pack(m_norm_g, m_even_b_forget, m_final_norm_g),
                      pack(v_norm_g, v_even_b_forget, v_final_norm_g), gathered, "adamw_small")

    def unpack(t):
        flat = t.reshape(-1)
        ng = lax.dynamic_slice(flat[:6 * d].reshape(6, d), (0, me * nsm), (6, nsm)).reshape(norm_g.shape)
        fg = flat[6 * d:7 * d].reshape(final_norm_g.shape)
        bfg = flat[7 * d:7 * d + nfox].reshape(even_b_forget.shape)
        return ng, bfg, fg

    sm_g, sm_d, sm_m, sm_v = [unpack(t) for t in sm]
    loss = sm[0].reshape(-1)[7 * d + HD]

    order = ["norm_g", "ffn1_w_gate", "ffn1_w_up", "ffn1_w_down", "ffn2_w_gate", "ffn2_w_up", "ffn2_w_down", "even_w_in",
             "even_b_forget", "even_w_out", "odd_w_qkv", "odd_w_out", "final_norm_g"]
    outs = [loss, dx0.reshape(x.shape)]
    for k in range(4):
        smk = [sm_g, sm_d, sm_m, sm_v][k]
        for nm in order:
            if nm == "norm_g":
                outs.append(smk[0])
            elif nm == "even_b_forget":
                outs.append(smk[1])
            elif nm == "final_norm_g":
                outs.append(smk[2])
            else:
                outs.append(res[nm][k])
    return tuple(outs)
```

```python
import functools
import math

import jax
import jax.numpy as jnp
from jax import lax
from jax.experimental import pallas as pl
from jax.experimental.pallas import tpu as pltpu

F32 = jnp.float32
BF = jnp.bfloat16
NDEV = 8
HD = 128
QB = 128
N_HEADS = 16
N_SB = 8
SCALE = HD ** -0.5
ROPE_THETA = 500000.0
ROPE_DIMS = HD // 4
DILATED_PATTERNS = ((128, 1), (512, 4), (2048, 16))
RMS_EPS = 1e-6
NEG_INF = -1e30
ADAM_LR = 0.001
ADAM_B1 = 0.9
ADAM_B2 = 0.999
ADAM_EPS = 1e-08
ADAM_WD = 0.01
ADAM_STEP = 10
VMEM_LIMIT_V7X = 56 * 1024 * 1024
MESH = pl.DeviceIdType.MESH
ANY = pl.BlockSpec(memory_space=pl.ANY)

NT_DIMS = (((1,), (1,)), ((), ()))


def _cp(*dims):
    return pltpu.CompilerParams(dimension_semantics=dims if dims else None, vmem_limit_bytes=VMEM_LIMIT_V7X)


def _dot(a, b):
    return jnp.dot(a, b, preferred_element_type=F32)


def _dot_nt(a, b):
    return lax.dot_general(a, b, NT_DIMS, preferred_element_type=F32)


def _sds(shape, dtype):
    return jax.ShapeDtypeStruct(shape, dtype)


def _place():
    x, y, c = lax.axis_index("x"), lax.axis_index("y"), lax.axis_index("c")
    chips = [(x, y), (1 - x, y), (x, 1 - y), (1 - x, 1 - y)]
    return x, y, c, chips


def _all_gather(xs, name):
    n = len(xs)

    def body(*refs):
        x_refs, out_refs = refs[:n], refs[n:2 * n]
        send_sems, recv_sems, local_sems = refs[2 * n:]
        x, y, c, chips = _place()
        me, sibling = (x, y, c), (x, y, 1 - c)
        others = chips[1:]

        def slot(a, px, py, pc):
            return out_refs[a].at[4 * px + 2 * py + pc]

        def copy(a, k, block, to, src=None):
            return pltpu.make_async_remote_copy(
                src_ref=slot(a, *block) if src is None else src, dst_ref=slot(a, *block),
                send_sem=send_sems.at[a, k], recv_sem=recv_sems.at[a, k], device_id=to, device_id_type=MESH)

        started = []
        for a in range(n):
            mine = pltpu.make_async_copy(x_refs[a], slot(a, *me), local_sems.at[a])
            mine.start()
            first = [copy(a, 0, me, sibling, src=x_refs[a])]
            first += [copy(a, 1 + j, me, (*chip, c), src=x_refs[a]) for j, chip in enumerate(others)]
            for cp in first:
                cp.start()
            started += [mine.wait] + [cp.wait_send for cp in first]
        for a in range(n):
            for j, chip in enumerate(others):
                copy(a, 1 + j, (*chip, c), me).wait_recv()
                passed = copy(a, 4 + j, (*chip, c), sibling)
                passed.start()
                started.append(passed.wait_send)
        for a in range(n):
            copy(a, 0, sibling, me).wait_recv()
            for j, chip in enumerate(others):
                copy(a, 4 + j, (*chip, 1 - c), me).wait_recv()
        for w in started:
            w()

    return pl.pallas_call(
        body, name=name,
        out_shape=[_sds((NDEV,) + x.shape, x.dtype) for x in xs],
        in_specs=[ANY] * n, out_specs=[ANY] * n,
        scratch_shapes=[pltpu.SemaphoreType.DMA((n, 7)), pltpu.SemaphoreType.DMA((n, 7)), pltpu.SemaphoreType.DMA((n,))],
    )(*xs)


def _pair_exchange(gs, name):
    n = len(gs)

    def body(*refs):
        g_refs, a_refs = refs[:n], refs[n:2 * n]
        send_sems, recv_sems = refs[2 * n:]
        x, y, c, chips = _place()
        copies = []
        for a in range(n):
            for j, (px, py) in enumerate(chips):
                copies.append(pltpu.make_async_remote_copy(
                    src_ref=g_refs[a].at[4 * px + 2 * py + (1 - c)], dst_ref=a_refs[a].at[j],
                    send_sem=send_sems.at[a, j], recv_sem=recv_sems.at[a, j],
                    device_id=(x, y, 1 - c), device_id_type=MESH))
        for cp in copies:
            cp.start()
        for cp in copies:
            cp.wait()

    return pl.pallas_call(
        body, name=name,
        out_shape=[_sds((4,) + g.shape[1:], g.dtype) for g in gs],
        in_specs=[ANY] * n, out_specs=[ANY] * n,
        scratch_shapes=[pltpu.SemaphoreType.DMA((n, 4)), pltpu.SemaphoreType.DMA((n, 4))],
    )(*gs)


HBM = pl.BlockSpec(memory_space=pltpu.HBM)
SEM = pl.BlockSpec(memory_space=pltpu.SEMAPHORE)
EFFECT = pltpu.SideEffectType.DATAFLOW_SIDE_EFFECTING
TOKEN = _sds((8, 128), F32)
TOKEN_SPEC = pl.BlockSpec((8, 128), lambda *_: (0, 0))


def _in_hbm(x):
    return pltpu.with_memory_space_constraint(x, pltpu.HBM)


def _ignore_deps(body, n_in, n_deps):
    if not n_deps:
        return body
    return lambda *refs: body(*refs[:n_in], *refs[n_in + n_deps:])


def _slot_of(px, py, pc):
    return 4 * px + 2 * py + pc


def _gather_start(xs, me, deps, name):
    n = len(xs)
    lands = [lax.dynamic_update_slice(lax.empty((NDEV,) + x.shape, x.dtype), x[None], (me,) + (0,) * x.ndim) for x in xs]

    def body(*refs):
        x_refs, land_refs = refs[:n], refs[n:2 * n]
        send, recv_ici, recv_sib = refs[2 * n:2 * n + 3]
        token = refs[4 * n + 3]
        x, y, c, chips = _place()
        for a in range(n):
            dst = land_refs[a].at[_slot_of(x, y, c)]
            pltpu.make_async_remote_copy(src_ref=x_refs[a], dst_ref=dst, send_sem=send.at[4 * a], recv_sem=recv_sib.at[a],
                                         device_id=(x, y, 1 - c), device_id_type=MESH).start()
            for j, chip in enumerate(chips[1:]):
                pltpu.make_async_remote_copy(src_ref=x_refs[a], dst_ref=dst, send_sem=send.at[4 * a + 1 + j], recv_sem=recv_ici.at[3 * a + j],
                                             device_id=(*chip, c), device_id_type=MESH).start()
        token[...] = jnp.zeros_like(token)

    outs = pl.pallas_call(
        _ignore_deps(body, 2 * n, len(deps)), name=name,
        out_shape=(pltpu.SemaphoreType.DMA((4 * n,)), pltpu.SemaphoreType.DMA((3 * n,)), pltpu.SemaphoreType.DMA((n,)),
                   *[pltpu.HBM(x.shape, x.dtype) for x in xs], *[pltpu.HBM(l.shape, l.dtype) for l in lands], TOKEN),
        in_specs=[HBM] * (2 * n) + [TOKEN_SPEC] * len(deps),
        out_specs=(SEM, SEM, SEM, *[HBM] * (2 * n), pl.BlockSpec(memory_space=pltpu.VMEM)),
        input_output_aliases={a: 3 + a for a in range(2 * n)},
        compiler_params=pltpu.CompilerParams(has_side_effects=EFFECT),
    )(*[_in_hbm(x) for x in xs], *[_in_hbm(l) for l in lands], *deps)
    send, recv_ici, recv_sib = outs[:3]
    return dict(send=send, recv_ici=recv_ici, recv_sib=recv_sib, xs=list(outs[3:3 + n]), lands=list(outs[3 + n:3 + 2 * n]), token=outs[-1])


def _gather_forward(st, after, name):
    n = len(st["lands"])

    def body(*refs):
        land_refs, recv_ici = refs[:n], refs[n]
        send2, recv2 = refs[n + 2], refs[n + 3]
        x, y, c, chips = _place()
        for a in range(n):
            for j, chip in enumerate(chips[1:]):
                blk = land_refs[a].at[_slot_of(*chip, c)]
                pltpu.make_async_remote_copy(src_ref=blk, dst_ref=blk, send_sem=send2.at[3 * a + j], recv_sem=recv_ici.at[3 * a + j],
                                             device_id=(*chip, c), device_id_type=MESH).wait_recv()
                pltpu.make_async_remote_copy(src_ref=blk, dst_ref=blk, send_sem=send2.at[3 * a + j], recv_sem=recv2.at[3 * a + j],
                                             device_id=(x, y, 1 - c), device_id_type=MESH).start()

    outs = pl.pallas_call(
        body, name=name,
        out_shape=(pltpu.SemaphoreType.DMA((3 * n,)), pltpu.SemaphoreType.DMA((3 * n,)), *[pltpu.HBM(l.shape, l.dtype) for l in st["lands"]]),
        in_specs=[HBM] * n + [SEM, pl.BlockSpec(memory_space=pl.ANY)],
        out_specs=(SEM, SEM, *[HBM] * n),
        input_output_aliases={a: 2 + a for a in range(n)},
        compiler_params=pltpu.CompilerParams(has_side_effects=EFFECT),
    )(*st["lands"], st["recv_ici"], after)
    return dict(st, send2=outs[0], recv2=outs[1], lands=list(outs[2:]))


def _gather_finish(st, name):
    n = len(st["lands"])

    def body(*refs):
        x_refs, land_refs = refs[:n], refs[n:2 * n]
        send, recv_sib, send2, recv2 = refs[2 * n:2 * n + 4]
        x, y, c, chips = _place()
        for a in range(n):
            mine = land_refs[a].at[_slot_of(x, y, c)]
            theirs = land_refs[a].at[_slot_of(x, y, 1 - c)]
            for k in range(4):
                pltpu.make_async_remote_copy(src_ref=x_refs[a], dst_ref=mine, send_sem=send.at[4 * a + k], recv_sem=recv_sib.at[a],
                                             device_id=(x, y, 1 - c), device_id_type=MESH).wait_send()
            pltpu.make_async_remote_copy(src_ref=x_refs[a], dst_ref=theirs, send_sem=send.at[4 * a], recv_sem=recv_sib.at[a],
                                         device_id=(x, y, 1 - c), device_id_type=MESH).wait_recv()
            for j, chip in enumerate(chips[1:]):
                sent = land_refs[a].at[_slot_of(*chip, c)]
                got = land_refs[a].at[_slot_of(*chip, 1 - c)]
                pltpu.make_async_remote_copy(src_ref=sent, dst_ref=sent, send_sem=send2.at[3 * a + j], recv_sem=recv2.at[3 * a + j],
                                             device_id=(x, y, 1 - c), device_id_type=MESH).wait_send()
                pltpu.make_async_remote_copy(src_ref=got, dst_ref=got, send_sem=send2.at[3 * a + j], recv_sem=recv2.at[3 * a + j],
                                             device_id=(x, y, 1 - c), device_id_type=MESH).wait_recv()

    outs = pl.pallas_call(
        body, name=name,
        out_shape=tuple(pltpu.HBM(v.shape, v.dtype) for v in st["xs"] + st["lands"]),
        in_specs=[HBM] * (2 * n) + [SEM] * 4, out_specs=tuple([HBM] * (2 * n)),
        input_output_aliases={a: a for a in range(2 * n)},
        compiler_params=pltpu.CompilerParams(has_side_effects=EFFECT),
    )(*st["xs"], *st["lands"], st["send"], st["recv_sib"], st["send2"], st["recv2"])
    return list(outs[n:])


def _pair_start(gs, name):
    n = len(gs)
    lands = [lax.empty((4,) + g.shape[1:], g.dtype) for g in gs]

    def body(*refs):
        g_refs, a_refs = refs[:n], refs[n:2 * n]
        send, recv = refs[2 * n], refs[2 * n + 1]
        token = refs[4 * n + 2]
        x, y, c, chips = _place()
        for a in range(n):
            for j, (px, py) in enumerate(chips):
                pltpu.make_async_remote_copy(src_ref=g_refs[a].at[_slot_of(px, py, 1 - c)], dst_ref=a_refs[a].at[j],
                                             send_sem=send.at[4 * a + j], recv_sem=recv.at[4 * a + j],
                                             device_id=(x, y, 1 - c), device_id_type=MESH).start()
        token[...] = jnp.zeros_like(token)

    outs = pl.pallas_call(
        body, name=name,
        out_shape=(pltpu.SemaphoreType.DMA((4 * n,)), pltpu.SemaphoreType.DMA((4 * n,)),
                   *[pltpu.HBM(g.shape, g.dtype) for g in gs], *[pltpu.HBM(l.shape, l.dtype) for l in lands], TOKEN),
        in_specs=[HBM] * (2 * n), out_specs=(SEM, SEM, *[HBM] * (2 * n), pl.BlockSpec(memory_space=pltpu.VMEM)),
        input_output_aliases={a: 2 + a for a in range(2 * n)},
        compiler_params=pltpu.CompilerParams(has_side_effects=EFFECT),
    )(*[_in_hbm(g) for g in gs], *[_in_hbm(l) for l in lands])
    return dict(send=outs[0], recv=outs[1], gs=list(outs[2:2 + n]), lands=list(outs[2 + n:2 + 2 * n]), token=outs[-1])


def _pair_finish(st, after, name):
    n = len(st["gs"])

    def body(*refs):
        g_refs, a_refs = refs[:n], refs[n:2 * n]
        send, recv = refs[2 * n], refs[2 * n + 1]
        x, y, c, chips = _place()
        for a in range(n):
            for j, (px, py) in enumerate(chips):
                cp = pltpu.make_async_remote_copy(src_ref=g_refs[a].at[_slot_of(px, py, 1 - c)], dst_ref=a_refs[a].at[j],
                                                  send_sem=send.at[4 * a + j], recv_sem=recv.at[4 * a + j],
                                                  device_id=(x, y, 1 - c), device_id_type=MESH)
                cp.wait_send()
                cp.wait_recv()

    outs = pl.pallas_call(
        body, name=name,
        out_shape=tuple(pltpu.HBM(v.shape, v.dtype) for v in st["gs"] + st["lands"]),
        in_specs=[HBM] * (2 * n) + [SEM, SEM, pl.BlockSpec(memory_space=pl.ANY)], out_specs=tuple([HBM] * (2 * n)),
        input_output_aliases={a: a for a in range(2 * n)},
        compiler_params=pltpu.CompilerParams(has_side_effects=EFFECT),
    )(*st["gs"], *st["lands"], st["send"], st["recv"], after)
    return list(outs[:n]), list(outs[n:])


def _chip_start(ps, name):
    n = len(ps)
    lands = [lax.empty(p.shape, p.dtype) for p in ps]

    def body(*refs):
        p_refs, b_refs = refs[:n], refs[n:2 * n]
        send, recv = refs[2 * n], refs[2 * n + 1]
        token = refs[4 * n + 2]
        x, y, c, chips = _place()
        for a in range(n):
            for j, chip in enumerate(chips[1:]):
                pltpu.make_async_remote_copy(src_ref=p_refs[a].at[j], dst_ref=b_refs[a].at[j], send_sem=send.at[3 * a + j], recv_sem=recv.at[3 * a + j],
                                             device_id=(*chip, c), device_id_type=MESH).start()
        token[...] = jnp.zeros_like(token)

    outs = pl.pallas_call(
        body, name=name,
        out_shape=(pltpu.SemaphoreType.DMA((3 * n,)), pltpu.SemaphoreType.DMA((3 * n,)),
                   *[pltpu.HBM(p.shape, p.dtype) for p in ps], *[pltpu.HBM(p.shape, p.dtype) for p in ps], TOKEN),
        in_specs=[HBM] * (2 * n), out_specs=(SEM, SEM, *[HBM] * (2 * n), pl.BlockSpec(memory_space=pltpu.VMEM)),
        input_output_aliases={a: 2 + a for a in range(2 * n)},
        compiler_params=pltpu.CompilerParams(has_side_effects=EFFECT),
    )(*[_in_hbm(p) for p in ps], *[_in_hbm(l) for l in lands])
    return dict(send=outs[0], recv=outs[1], ps=list(outs[2:2 + n]), lands=list(outs[2 + n:2 + 2 * n]), token=outs[-1])


def _chip_finish(st, after, name):
    n = len(st["ps"])

    def body(*refs):
        p_refs, b_refs = refs[:n], refs[n:2 * n]
        send, recv = refs[2 * n], refs[2 * n + 1]
        x, y, c, chips = _place()
        for a in range(n):
            for j, chip in enumerate(chips[1:]):
                cp = pltpu.make_async_remote_copy(src_ref=p_refs[a].at[j], dst_ref=b_refs[a].at[j], send_sem=send.at[3 * a + j], recv_sem=recv.at[3 * a + j],
                                                  device_id=(*chip, c), device_id_type=MESH)
                cp.wait_send()
                cp.wait_recv()

    outs = pl.pallas_call(
        body, name=name,
        out_shape=tuple(pltpu.HBM(v.shape, v.dtype) for v in st["ps"] + st["lands"]),
        in_specs=[HBM] * (2 * n) + [SEM, SEM, pl.BlockSpec(memory_space=pl.ANY)], out_specs=tuple([HBM] * (2 * n)),
        input_output_aliases={a: a for a in range(2 * n)},
        compiler_params=pltpu.CompilerParams(has_side_effects=EFFECT),
    )(*st["ps"], *st["lands"], st["send"], st["recv"], after)
    return list(outs[n:])


def _rows_tile(r):
    for t in (512, 256, 128, 64, 32, 16):
        if r % t == 0:
            return t
    return r


def _pair_sum(g, a, slots, name):
    _, r, c = g.shape
    tr = _rows_tile(r)

    def body(slots_ref, g_ref, a_ref, p_ref):
        p_ref[...] = (g_ref[...].astype(F32) + a_ref[...].astype(F32)).astype(BF)

    return pl.pallas_call(
        body, name=name,
        grid_spec=pltpu.PrefetchScalarGridSpec(
            num_scalar_prefetch=1, grid=(3, r // tr),
            in_specs=[pl.BlockSpec((None, tr, c), lambda j, i, s: (s[j + 1], i, 0)),
                      pl.BlockSpec((None, tr, c), lambda j, i, s: (j + 1, i, 0))],
            out_specs=pl.BlockSpec((None, tr, c), lambda j, i, s: (j, i, 0))),
        out_shape=_sds((3, r, c), BF), compiler_params=_cp("parallel", "parallel"),
    )(slots, g, a)


def _adamw_math(w, g, m, v):
    m = ADAM_B1 * m + (1.0 - ADAM_B1) * g
    v = ADAM_B2 * v + (1.0 - ADAM_B2) * (g * g)
    m_hat = m / (1.0 - ADAM_B1 ** ADAM_STEP)
    v_hat = v / (1.0 - ADAM_B2 ** ADAM_STEP)
    delta = -ADAM_LR * (m_hat / (jnp.sqrt(v_hat) + ADAM_EPS) + ADAM_WD * w)
    return delta, m, v


def _adamw_sharded(w, m, v, parts, slots, name, transposed=False):
    nl = w.shape[0]
    r, c = parts[0][0].shape[1:]
    tr = _rows_tile(r)
    if c * tr * 4 > (1 << 21) and not transposed:
        tr = max(8, tr // 2)

    def body(slots_ref, w_ref, m_ref, v_ref, *rest):
        part_refs, (g_out, d_out, m_out, v_out) = rest[:5 * nl], rest[5 * nl:]
        layer = pl.program_id(0)
        g = None
        for l in range(nl):
            s = part_refs[5 * l][...].astype(F32)
            for ref in part_refs[5 * l + 1:5 * l + 5]:
                s = s + ref[...].astype(F32)
            g = s if g is None else jnp.where(layer == l, s, g)
        if transposed:
            g = g.T
        delta, mn, vn = _adamw_math(w_ref[...], g, m_ref[...], v_ref[...])
        g_out[...] = g
        d_out[...] = delta
        m_out[...] = mn
        v_out[...] = vn

    def own(l):
        return lambda L, i, s: (s[0], jnp.where(L == l, i, 0), 0)

    def fixed(l, k):
        return lambda L, i, s: (k, jnp.where(L == l, i, 0), 0)

    if transposed:
        wspec = pl.BlockSpec((None, c, tr), lambda L, i, s: (L, 0, i))
    else:
        wspec = pl.BlockSpec((None, tr, c), lambda L, i, s: (L, i, 0))
    in_specs = [wspec, wspec, wspec]
    args = [w, m, v]
    for l, (g, a, b) in enumerate(parts):
        in_specs += [pl.BlockSpec((None, tr, c), own(l)), pl.BlockSpec((None, tr, c), fixed(l, 0)),
                     pl.BlockSpec((None, tr, c), fixed(l, 0)), pl.BlockSpec((None, tr, c), fixed(l, 1)),
                     pl.BlockSpec((None, tr, c), fixed(l, 2))]
        args += [g, a, b, b, b]
    return pl.pallas_call(
        body, name=name,
        grid_spec=pltpu.PrefetchScalarGridSpec(
            num_scalar_prefetch=1, grid=(nl, r // tr), in_specs=in_specs, out_specs=[wspec] * 4),
        out_shape=[_sds(w.shape, F32)] * 4, compiler_params=_cp("arbitrary", "arbitrary"),
    )(slots, *args)


def _adamw_small(w, m, v, gathered, name):
    def body(w_ref, m_ref, v_ref, gg_ref, g_out, d_out, m_out, v_out):
        g = gg_ref[0]
        for k in range(1, NDEV):
            g = g + gg_ref[k]
        delta, mn, vn = _adamw_math(w_ref[...], g, m_ref[...], v_ref[...])
        g_out[...] = g
        d_out[...] = delta
        m_out[...] = mn
        v_out[...] = vn

    return pl.pallas_call(body, name=name, out_shape=[_sds(w.shape, F32)] * 4)(w, m, v, gathered)


def _rmsnorm_fwd(x, g, name, deps=()):
    s, d = x.shape
    tm = 256

    def body(x_ref, g_ref, h_ref, ht_ref):
        xf = x_ref[...]
        y = xf * lax.rsqrt(jnp.mean(xf * xf, axis=-1, keepdims=True) + RMS_EPS)
        h = y * g_ref[...]
        h_ref[...] = h.astype(BF)
        ht_ref[...] = h.T.astype(BF)

    return pl.pallas_call(
        _ignore_deps(body, 2, len(deps)), name=name, grid=(s // tm,),
        in_specs=[pl.BlockSpec((tm, d), lambda i: (i, 0)), pl.BlockSpec((1, d), lambda i: (0, 0))] + [TOKEN_SPEC] * len(deps),
        out_specs=[pl.BlockSpec((tm, d), lambda i: (i, 0)), pl.BlockSpec((d, tm), lambda i: (0, i))],
        out_shape=[_sds((s, d), BF), _sds((d, s), BF)], compiler_params=_cp("parallel"),
    )(x, g, *deps)


def _rmsnorm_bwd(x, g, dh, dres, out_scale, name):
    s, d = x.shape
    tm = 256

    def body(x_ref, g_ref, dh_ref, dres_ref, dx_ref, dxb_ref, dxbt_ref, dg_ref):
        xf = x_ref[...]
        r = lax.rsqrt(jnp.mean(xf * xf, axis=-1, keepdims=True) + RMS_EPS)
        xhat = xf * r
        dhv = dh_ref[...]
        dxhat = dhv * g_ref[...]
        dx = dres_ref[...] + r * (dxhat - xhat * jnp.mean(dxhat * xhat, axis=-1, keepdims=True))
        dx_ref[...] = dx
        scaled = dx * out_scale
        dxb_ref[...] = scaled.astype(BF)
        dxbt_ref[...] = scaled.T.astype(BF)

        @pl.when(pl.program_id(0) == 0)
        def _():
            dg_ref[...] = jnp.zeros_like(dg_ref)

        dg_ref[...] += jnp.sum(dhv * xhat, axis=0, keepdims=True)

    row = pl.BlockSpec((tm, d), lambda i: (i, 0))
    vec = pl.BlockSpec((1, d), lambda i: (0, 0))
    return pl.pallas_call(
        body, name=name, grid=(s // tm,),
        in_specs=[row, vec, row, row],
        out_specs=[row, row, pl.BlockSpec((d, tm), lambda i: (0, i)), vec],
        out_shape=[_sds((s, d), F32), _sds((s, d), BF), _sds((d, s), BF), _sds((1, d), F32)],
        compiler_params=_cp("arbitrary"),
    )(x, g, dh, dres)


def _loss_head(x, g, target, name):
    s, d = x.shape
    tm = 256

    def body(x_ref, g_ref, t_ref, dx_ref, dxb_ref, dxbt_ref, dg_ref, loss_ref):
        xf = x_ref[...]
        r = lax.rsqrt(jnp.mean(xf * xf, axis=-1, keepdims=True) + RMS_EPS)
        xhat = xf * r
        err = xhat * g_ref[...] - t_ref[...]
        dy = err * (1.0 / d)
        dxhat = dy * g_ref[...]
        dx = r * (dxhat - xhat * jnp.mean(dxhat * xhat, axis=-1, keepdims=True))
        dx_ref[...] = dx
        half = dx * 0.5
        dxb_ref[...] = half.astype(BF)
        dxbt_ref[...] = half.T.astype(BF)

        @pl.when(pl.program_id(0) == 0)
        def _():
            dg_ref[...] = jnp.zeros_like(dg_ref)
            loss_ref[...] = jnp.zeros_like(loss_ref)

        dg_ref[...] += jnp.sum(dy * xhat, axis=0, keepdims=True)
        part = 0.5 * jnp.sum(jnp.mean(err * err, axis=-1, keepdims=True), axis=0, keepdims=True)
        lane = lax.broadcasted_iota(jnp.int32, (1, 128), 1)
        loss_ref[...] += jnp.where(lane == 0, part, 0.0)

    row = pl.BlockSpec((tm, d), lambda i: (i, 0))
    vec = pl.BlockSpec((1, d), lambda i: (0, 0))
    return pl.pallas_call(
        body, name=name, grid=(s // tm,),
        in_specs=[row, vec, row],
        out_specs=[row, row, pl.BlockSpec((d, tm), lambda i: (0, i)), vec, pl.BlockSpec((1, 128), lambda i: (0, 0))],
        out_shape=[_sds((s, d), F32), _sds((s, d), BF), _sds((d, s), BF), _sds((1, d), F32), _sds((1, 128), F32)],
        compiler_params=_cp("arbitrary"),
    )(x, g, target)


def _act_spec(tm, n, natural, order):
    if natural:
        return pl.BlockSpec((tm, n), (lambda s, i: (i, s)) if order == "si" else (lambda i, s: (i, s)))
    return pl.BlockSpec((None, tm, n), (lambda s, i: (s, i, 0)) if order == "si" else (lambda i, s: (s, i, 0)))


def _act_shape(s, n, natural, dtype):
    return _sds((s, NDEV * n), dtype) if natural else _sds((NDEV, s, n), dtype)


def _ffn_up(h, wg, wu, name):
    s, d = h.shape
    n = wg.shape[2]
    tm = 512

    def body(h_ref, wg_ref, wu_ref, g_ref, u_ref, a_ref, at_ref):
        hb = h_ref[...]
        g = _dot(hb, wg_ref[...])
        u = _dot(hb, wu_ref[...])
        g_ref[...] = g.astype(BF)
        u_ref[...] = u.astype(BF)
        act = g * jax.nn.sigmoid(g) * u
        a_ref[...] = act.astype(BF)
        at_ref[...] = act.T.astype(BF)

    wsp = pl.BlockSpec((None, d, n), lambda s_, i: (s_, 0, 0))
    blk = _act_spec(tm, n, False, "si")
    return pl.pallas_call(
        body, name=name, grid=(NDEV, s // tm),
        in_specs=[pl.BlockSpec((tm, d), lambda s_, i: (i, 0)), wsp, wsp],
        out_specs=[blk] * 3 + [pl.BlockSpec((None, n, tm), lambda s_, i: (s_, 0, i))],
        out_shape=[_act_shape(s, n, False, BF)] * 3 + [_sds((NDEV, n, s), BF)],
        compiler_params=_cp("parallel", "parallel"),
    )(h, wg, wu)


def _ffn_down(act, wd, x, name):
    _, s, n = act.shape
    d = wd.shape[2]
    tm = 512

    def body(a_ref, w_ref, x_ref, o_ref, acc):
        k = pl.program_id(1)

        @pl.when(k == 0)
        def _():
            acc[...] = jnp.zeros_like(acc)

        acc[...] += _dot(a_ref[...], w_ref[...])

        @pl.when(k == NDEV - 1)
        def _():
            o_ref[...] = x_ref[...] + 0.5 * acc[...]

    row = pl.BlockSpec((tm, d), lambda i, k: (i, 0))
    return pl.pallas_call(
        body, name=name, grid=(s // tm, NDEV),
        in_specs=[_act_spec(tm, n, False, "is"), pl.BlockSpec((None, n, d), lambda i, k: (k, 0, 0)), row],
        out_specs=row, out_shape=_sds((s, d), F32),
        scratch_shapes=[pltpu.VMEM((tm, d), F32)], compiler_params=_cp("parallel", "arbitrary"),
    )(act, wd, x)


def _ffn_bwd_act(dyb, wd, g, u, name, deps=()):
    s, d = dyb.shape
    n = wd.shape[1]
    tm = 512

    def body(dy_ref, w_ref, g_ref, u_ref, dg_ref, du_ref):
        dact = _dot_nt(dy_ref[...], w_ref[...])
        gv = g_ref[...].astype(F32)
        uv = u_ref[...].astype(F32)
        sig = jax.nn.sigmoid(gv)
        dg_ref[...] = (dact * uv * (sig * (1.0 + gv * (1.0 - sig)))).astype(BF)
        du_ref[...] = (dact * (gv * sig)).astype(BF)

    blk = _act_spec(tm, n, False, "si")
    return pl.pallas_call(
        _ignore_deps(body, 4, len(deps)), name=name, grid=(NDEV, s // tm),
        in_specs=[pl.BlockSpec((tm, d), lambda s_, i: (i, 0)), pl.BlockSpec((None, n, d), lambda s_, i: (s_, 0, 0)), blk, blk]
        + [TOKEN_SPEC] * len(deps),
        out_specs=[blk, blk], out_shape=[_act_shape(s, n, False, BF)] * 2,
        compiler_params=_cp("parallel", "parallel"),
    )(dyb, wd, g, u, *deps)


def _grad_rows(act_t, dyb, name):
    _, n, s = act_t.shape
    d = dyb.shape[1]
    tn = 1024

    def body(a_ref, dy_ref, o_ref):
        o_ref[...] = _dot(a_ref[...], dy_ref[...]).astype(BF)

    return pl.pallas_call(
        body, name=name, grid=(NDEV, d // tn),
        in_specs=[pl.BlockSpec((None, n, s), lambda k, j: (k, 0, 0)), pl.BlockSpec((s, tn), lambda k, j: (0, j))],
        out_specs=pl.BlockSpec((None, n, tn), lambda k, j: (k, 0, j)), out_shape=_sds((NDEV, n, d), BF),
        compiler_params=_cp("parallel", "parallel"),
    )(act_t, dyb)


def _grad_cols(ht, dxs, naturals, name, deps=()):
    d, s = ht.shape
    k = len(dxs)
    ns = [dx.shape[1] // NDEV if nat else dx.shape[2] for dx, nat in zip(dxs, naturals)]
    td = 512

    def body(*refs):
        ht_ref, dx_refs, o_refs = refs[0], refs[1:1 + k], refs[1 + k:]
        hv = ht_ref[...]
        for dx_ref, o_ref in zip(dx_refs, o_refs):
            o_ref[...] = _dot(hv, dx_ref[...]).astype(BF)

    def dx_spec(n, nat):
        if nat:
            return pl.BlockSpec((s, n), lambda s_, j: (0, s_))
        return pl.BlockSpec((None, s, n), lambda s_, j: (s_, 0, 0))

    return pl.pallas_call(
        _ignore_deps(body, 1 + k, len(deps)), name=name, grid=(NDEV, d // td),
        in_specs=[pl.BlockSpec((td, s), lambda s_, j: (j, 0))] + [dx_spec(n, nat) for n, nat in zip(ns, naturals)]
        + [TOKEN_SPEC] * len(deps),
        out_specs=[pl.BlockSpec((None, td, n), lambda s_, j: (s_, j, 0)) for n in ns],
        out_shape=[_sds((NDEV, d, n), BF) for n in ns], compiler_params=_cp("parallel", "parallel"),
    )(ht, *dxs, *deps)


def _dh_cols(dxs, ws, naturals, name, deps=()):
    k = len(dxs)
    d = ws[0].shape[1]
    ns = [w.shape[2] for w in ws]
    s = dxs[0].shape[0] if naturals[0] else dxs[0].shape[1]
    tm = 512

    def body(*refs):
        dx_refs, w_refs, o_ref, acc = refs[:k], refs[k:2 * k], refs[2 * k], refs[2 * k + 1]
        j = pl.program_id(1)

        @pl.when(j == 0)
        def _():
            acc[...] = jnp.zeros_like(acc)

        t = _dot_nt(dx_refs[0][...], w_refs[0][...])
        for dx_ref, w_ref in zip(dx_refs[1:], w_refs[1:]):
            t = t + _dot_nt(dx_ref[...], w_ref[...])
        acc[...] += t

        @pl.when(j == NDEV - 1)
        def _():
            o_ref[...] = acc[...]

    return pl.pallas_call(
        _ignore_deps(body, 2 * k, len(deps)), name=name, grid=(s // tm, NDEV),
        in_specs=[_act_spec(tm, n, nat, "is") for n, nat in zip(ns, naturals)]
        + [pl.BlockSpec((None, d, n), lambda i, j: (j, 0, 0)) for n in ns] + [TOKEN_SPEC] * len(deps),
        out_specs=pl.BlockSpec((tm, d), lambda i, j: (i, 0)), out_shape=_sds((s, d), F32),
        scratch_shapes=[pltpu.VMEM((tm, d), F32)], compiler_params=_cp("parallel", "arbitrary"),
    )(*dxs, *ws, *deps)


def _mm_nn(a, b, tn, out_dtype, name, res=None, tm=512):
    m, k = a.shape
    nn = b.shape[1]

    def body(*refs):
        if res is None:
            a_ref, b_ref, o_ref = refs
            o_ref[...] = _dot(a_ref[...], b_ref[...]).astype(out_dtype)
        else:
            a_ref, b_ref, r_ref, o_ref = refs
            o_ref[...] = (r_ref[...] + _dot(a_ref[...], b_ref[...])).astype(out_dtype)

    osp = pl.BlockSpec((tm, tn), lambda j, i: (i, j))
    in_specs = [pl.BlockSpec((tm, k), lambda j, i: (i, 0)), pl.BlockSpec((k, tn), lambda j, i: (0, j))]
    args = [a, b]
    if res is not None:
        in_specs.append(osp)
        args.append(res)
    return pl.pallas_call(
        body, name=name, grid=(nn // tn, m // tm), in_specs=in_specs, out_specs=osp,
        out_shape=_sds((m, nn), out_dtype), compiler_params=_cp("parallel", "parallel"),
    )(*args)


def _mm_nt(pairs, name, out_dtype=F32, tm=512, tk=512, deps=()):
    m = pairs[0][0].shape[0]
    kk = pairs[0][1].shape[0]
    p = len(pairs)

    def body(*refs):
        o_ref = refs[2 * p]
        t = _dot_nt(refs[0][...], refs[1][...])
        for q in range(1, p):
            t = t + _dot_nt(refs[2 * q][...], refs[2 * q + 1][...])
        o_ref[...] = t.astype(out_dtype)

    in_specs, args = [], []
    for a, b in pairs:
        in_specs += [pl.BlockSpec((tm, a.shape[1]), lambda j, i: (i, 0)), pl.BlockSpec((tk, b.shape[1]), lambda j, i: (j, 0))]
        args += [a, b]
    return pl.pallas_call(
        _ignore_deps(body, 2 * p, len(deps)), name=name, grid=(kk // tk, m // tm), in_specs=in_specs + [TOKEN_SPEC] * len(deps),
        out_specs=pl.BlockSpec((tm, tk), lambda j, i: (i, j)), out_shape=_sds((m, kk), out_dtype),
        compiler_params=_cp("parallel", "parallel"),
    )(*args, *deps)


def _rope_tables(s, sign):
    half = ROPE_DIMS // 2
    freqs = ROPE_THETA ** (-jnp.arange(half, dtype=F32) / half)
    ang = jnp.arange(s, dtype=F32)[:, None] * freqs[None, :]
    cos, sin = jnp.cos(ang), sign * jnp.sin(ang)
    one = jnp.ones((s, HD - ROPE_DIMS), F32)
    zero = jnp.zeros((s, HD - ROPE_DIMS), F32)
    zh = jnp.zeros((s, half), F32)
    c = jnp.concatenate([cos, cos, one], axis=1)
    sa = jnp.concatenate([-sin, zh, zero], axis=1)
    sb = jnp.concatenate([zh, sin, zero], axis=1)
    return c, sa, sb


def _rope(xv, c, sa, sb):
    return xv * c + pltpu.roll(xv, HD - ROPE_DIMS // 2, 1) * sa + pltpu.roll(xv, ROPE_DIMS // 2, 1) * sb


def _qkv_rope(h, w, tables, name):
    s, d = h.shape
    n = w.shape[2]
    per = n // HD
    tm = 512

    def body(h_ref, w_ref, c_ref, sa_ref, sb_ref, o_ref):
        shard = pl.program_id(0)
        y = _dot(h_ref[...], w_ref[...])
        c, sa, sb = c_ref[...], sa_ref[...], sb_ref[...]
        for j in range(per):
            blk = y[:, j * HD:(j + 1) * HD]
            rot = _rope(blk, c, sa, sb)
            is_qk = shard * per + j < 2 * N_HEADS
            o_ref[:, j * HD:(j + 1) * HD] = jnp.where(is_qk, rot, blk).astype(BF)

    tab = pl.BlockSpec((tm, HD), lambda s_, i: (i, 0))
    return pl.pallas_call(
        body, name=name, grid=(NDEV, s // tm),
        in_specs=[pl.BlockSpec((tm, d), lambda s_, i: (i, 0)), pl.BlockSpec((None, d, n), lambda s_, i: (s_, 0, 0)), tab, tab, tab],
        out_specs=pl.BlockSpec((tm, n), lambda s_, i: (i, s_)), out_shape=_sds((s, NDEV * n), BF),
        compiler_params=_cp("parallel", "parallel"),
    )(h, w, *tables)


def _iota2():
    return (lax.broadcasted_iota(jnp.int32, (QB, QB), 0), lax.broadcasted_iota(jnp.int32, (QB, QB), 1))


def _softplus(z):
    return jnp.maximum(z, 0.0) + jnp.log(1.0 + jnp.exp(-jnp.abs(z)))


def _tri_dot(xv, tri, left=False):
    hi = xv.astype(BF)
    r1 = xv - hi.astype(F32)
    mid = r1.astype(BF)
    lo = (r1 - mid.astype(F32)).astype(BF)
    if left:
        return _dot(tri, hi) + _dot(tri, mid) + _dot(tri, lo)
    return _dot(hi, tri) + _dot(mid, tri) + _dot(lo, tri)


def _col(ref_or_val):
    return ref_or_val[:, 0:1]


KT = 4 * QB


def _iota_tile():
    return (lax.broadcasted_iota(jnp.int32, (QB, KT), 0), lax.broadcasted_iota(jnp.int32, (QB, KT), 1))


def _scan_matrix(keep):
    tri = keep(*_iota2()).astype(BF)
    return jnp.concatenate([tri, tri], axis=0)


def _scan_dot(xv, tri2):
    hi = xv.astype(BF)
    lo = (xv - hi.astype(F32)).astype(BF)
    return _dot(jnp.concatenate([hi, lo], axis=1), tri2)


def _blocks(xv):
    return [xv[:, b * QB:(b + 1) * QB] for b in range(KT // QB)]


def _sb_fwd(qkv, name):
    s = qkv.shape[0]
    nb = s // QB

    def body(q_ref, k_ref, v_ref, o_ref, ot_ref, t_ref):
        i = pl.program_id(1)
        q = q_ref[...]
        row, col = _iota_tile()
        later_keys = _scan_matrix(lambda j, s_: j > s_)
        last = i // (KT // QB)

        def step(tt, carry):
            acc, later = carry
            t = last - tt
            off = pl.multiple_of(t * KT, KT)
            k = k_ref[pl.ds(off, KT), :]
            v = v_ref[pl.ds(off, KT), :]
            z = _dot_nt(q, k) * SCALE
            strict = row + (i * QB - t * KT) > col
            sp = _softplus(z)
            lnb = jnp.where(strict, -sp, 0.0)
            afters = []
            for xb in reversed(_blocks(lnb)):
                afters.append(later + _scan_dot(xb, later_keys))
                later = later + jnp.sum(xb, axis=1, keepdims=True)
            after = jnp.concatenate(afters[::-1], axis=1)
            w = jnp.where(strict, jnp.exp((z - sp) + after), 0.0)
            return acc + _dot(w.astype(BF), v), later

        acc, total = lax.fori_loop(0, last + 1, step, (jnp.zeros((QB, HD), F32), jnp.zeros((QB, 1), F32)))
        o_ref[...] = acc.astype(BF)
        ot_ref[...] = acc.T.astype(BF)
        t_ref[...] = jnp.broadcast_to(total, (QB, HD))

    blk = pl.BlockSpec((QB, HD), lambda h, i: (i, h))
    return pl.pallas_call(
        body, name=name, grid=(N_SB, nb),
        in_specs=[blk, pl.BlockSpec((s, HD), lambda h, i: (0, N_HEADS + h)), pl.BlockSpec((s, HD), lambda h, i: (0, 2 * N_HEADS + h))],
        out_specs=[blk, pl.BlockSpec((HD, QB), lambda h, i: (h, i)), blk],
        out_shape=[_sds((s, N_SB * HD), BF), _sds((N_SB * HD, s), BF), _sds((s, N_SB * HD), F32)],
        compiler_params=_cp("parallel", "parallel"),
    )(qkv, qkv, qkv)


def _sb_bwd(qkv, do, total, name):
    s = qkv.shape[0]
    nb = s // QB

    def body(q_ref, k_ref, v_ref, do_ref, t_ref, dq_ref, dk_ref, dv_ref, dk_acc, dv_acc):
        i = pl.program_id(1)

        @pl.when(i == 0)
        def _():
            dk_acc[...] = jnp.zeros_like(dk_acc)
            dv_acc[...] = jnp.zeros_like(dv_acc)

        q = q_ref[...]
        dov = do_ref[...]
        tot = _col(t_ref[...])
        row, col = _iota_tile()
        keys_upto = _scan_matrix(lambda j, s_: j <= s_)
        keys_before = _scan_matrix(lambda j, s_: j < s_)

        def step(t, carry):
            dq, lnb_before, dl_before = carry
            off = pl.multiple_of(t * KT, KT)
            k = k_ref[pl.ds(off, KT), :]
            v = v_ref[pl.ds(off, KT), :]
            z = _dot_nt(q, k) * SCALE
            strict = row + (i * QB - t * KT) > col
            sp = _softplus(z)
            lnb = jnp.where(strict, -sp, 0.0)
            afters = []
            for xb in _blocks(lnb):
                afters.append(tot - (lnb_before + _scan_dot(xb, keys_upto)))
                lnb_before = lnb_before + jnp.sum(xb, axis=1, keepdims=True)
            a = jnp.where(strict, jnp.exp((z - sp) + jnp.concatenate(afters, axis=1)), 0.0)
            dl = a * _dot_nt(dov, v)
            befores = []
            for xb in _blocks(dl):
                befores.append(dl_before + _scan_dot(xb, keys_before))
                dl_before = dl_before + jnp.sum(xb, axis=1, keepdims=True)
            sig = jnp.exp(z - sp)
            dz = jnp.where(strict, dl * (1.0 - sig) - sig * jnp.concatenate(befores, axis=1), 0.0) * SCALE
            dq = dq + _dot(dz.astype(BF), k)
            dk_acc[pl.ds(off, KT), :] += _dot(dz.T.astype(BF), q)
            dv_acc[pl.ds(off, KT), :] += _dot(a.T.astype(BF), dov)
            return dq, lnb_before, dl_before

        zero = jnp.zeros((QB, 1), F32)
        dq, _, _ = lax.fori_loop(0, i // (KT // QB) + 1, step, (jnp.zeros((QB, HD), F32), zero, zero))
        dq_ref[...] = dq.astype(BF)

        @pl.when(i == nb - 1)
        def _():
            dk_ref[...] = dk_acc[...].astype(BF)
            dv_ref[...] = dv_acc[...].astype(BF)

    blk = pl.BlockSpec((QB, HD), lambda h, i: (i, h))
    full = pl.BlockSpec((s, HD), lambda h, i: (0, h))
    return pl.pallas_call(
        body, name=name, grid=(N_SB, nb),
        in_specs=[blk, pl.BlockSpec((s, HD), lambda h, i: (0, N_HEADS + h)), pl.BlockSpec((s, HD), lambda h, i: (0, 2 * N_HEADS + h)), blk, blk],
        out_specs=[blk, full, full], out_shape=[_sds((s, N_SB * HD), BF)] * 3,
        scratch_shapes=[pltpu.VMEM((s, HD), F32), pltpu.VMEM((s, HD), F32)],
        compiler_params=_cp("parallel", "arbitrary"),
    )(qkv, qkv, qkv, do, total)


def _fgate_fwd(f, b, name):
    s = f.shape[0]
    nb = s // QB
    nfox = N_HEADS - N_SB

    def body(f_ref, b_ref, cb_ref, ct_ref):
        row, col = _iota2()
        upto = (row >= col).astype(BF)
        carry = jnp.zeros((1, HD), F32)
        for blk in range(nb):
            xv = f_ref[blk * QB:(blk + 1) * QB, :] + b_ref[...]
            logf = -_softplus(-xv)
            cum = _tri_dot(logf, upto, left=True) + carry
            carry = cum[QB - 1:QB, :]
            ct_ref[blk] = cum.T
            for h in range(nfox):
                cb_ref[blk * QB:(blk + 1) * QB, h * HD:(h + 1) * HD] = jnp.broadcast_to(cum[:, h:h + 1], (QB, HD))

    return pl.pallas_call(
        body, name=name, out_shape=[_sds((s, nfox * HD), F32), _sds((nb, HD, HD), F32)], compiler_params=_cp(),
    )(f, b)


def _fgate_bwd(dcq, dck, f, b, name):
    s = f.shape[0]
    nb = s // QB
    nfox = N_HEADS - N_SB

    def body(dcq_ref, dck_ref, f_ref, b_ref, df_ref, db_ref):
        row, col = _iota2()
        from_tri = (row <= col).astype(BF)
        lane = col
        carry = jnp.zeros((1, HD), F32)
        db = jnp.zeros((1, HD), F32)
        for blk in reversed(range(nb)):
            dcum = jnp.zeros((QB, HD), F32)
            for h in range(nfox):
                here = (slice(blk * QB, (blk + 1) * QB), slice(h * HD, (h + 1) * HD))
                dcum = jnp.where(lane == h, dcq_ref[here] - dck_ref[here], dcum)
            dlogf = _tri_dot(dcum, from_tri, left=True) + carry
            carry = dlogf[0:1, :]
            xv = f_ref[blk * QB:(blk + 1) * QB, :] + b_ref[...]
            sp = _softplus(xv)
            df = jnp.where(lane < nfox, dlogf * jnp.exp(-sp), 0.0)
            df_ref[blk * QB:(blk + 1) * QB, :] = df.astype(BF)
            db = db + jnp.sum(df, axis=0, keepdims=True)
        db_ref[...] = db

    return pl.pallas_call(
        body, name=name, out_shape=[_sds((s, HD), BF), _sds((1, HD), F32)], compiler_params=_cp(),
    )(dcq, dck, f, b)


def _fox_head_row(ct_ref, j, h):
    tile = ct_ref[j]
    sub = lax.broadcasted_iota(jnp.int32, tile.shape, 0)
    return jnp.sum(jnp.where(sub == h, tile, 0.0), axis=0, keepdims=True)


def _fox_tile_row(ct_ref, t, h):
    nsub = KT // QB
    return jnp.concatenate([_fox_head_row(ct_ref, t * nsub + b, h) for b in range(nsub)], axis=1)


def _fox_fwd(qkv, cum_b, cum_t, name):
    s = qkv.shape[0]
    nb = s // QB
    nfox = N_HEADS - N_SB

    def body(q_ref, k_ref, v_ref, cq_ref, ct_ref, o_ref, ot_ref, lse_ref):
        h, i = pl.program_id(0), pl.program_id(1)
        q = q_ref[...]
        cq = _col(cq_ref[...])
        row, col = _iota_tile()

        def step(t, carry):
            acc, m, l = carry
            off = pl.multiple_of(t * KT, KT)
            k = k_ref[pl.ds(off, KT), :]
            v = v_ref[pl.ds(off, KT), :]
            z = _dot_nt(q, k) * SCALE + cq - _fox_tile_row(ct_ref, t, h)
            z = jnp.where(row + (i * QB - t * KT) >= col, z, NEG_INF)
            m_new = jnp.maximum(m, jnp.max(z, axis=1, keepdims=True))
            alpha = jnp.exp(m - m_new)
            p = jnp.exp(z - m_new)
            l = alpha * l + jnp.sum(p, axis=1, keepdims=True)
            acc = alpha * acc + _dot(p.astype(BF), v)
            return acc, m_new, l

        acc, m, l = lax.fori_loop(0, i // (KT // QB) + 1, step,
                                  (jnp.zeros((QB, HD), F32), jnp.full((QB, 1), NEG_INF, F32), jnp.zeros((QB, 1), F32)))
        o = acc / l
        o_ref[...] = o.astype(BF)
        ot_ref[...] = o.T.astype(BF)
        lse_ref[...] = jnp.broadcast_to(m + jnp.log(l), (QB, HD))

    blk = pl.BlockSpec((QB, HD), lambda h, i: (i, h))
    return pl.pallas_call(
        body, name=name, grid=(nfox, nb),
        in_specs=[pl.BlockSpec((QB, HD), lambda h, i: (i, N_SB + h)),
                  pl.BlockSpec((s, HD), lambda h, i: (0, N_HEADS + N_SB + h)),
                  pl.BlockSpec((s, HD), lambda h, i: (0, 2 * N_HEADS + N_SB + h)),
                  blk, pl.BlockSpec((nb, 8, HD), lambda h, i: (0, 0, 0))],
        out_specs=[blk, pl.BlockSpec((HD, QB), lambda h, i: (h, i)), blk],
        out_shape=[_sds((s, nfox * HD), BF), _sds((nfox * HD, s), BF), _sds((s, nfox * HD), F32)],
        compiler_params=_cp("parallel", "parallel"),
    )(qkv, qkv, qkv, cum_b, cum_t)


def _fox_bwd(qkv, cum_b, cum_t, o, lse, do, name):
    s = qkv.shape[0]
    nb = s // QB
    nfox = N_HEADS - N_SB

    def body(q_ref, k_ref, v_ref, cq_ref, ct_ref, o_ref, lse_ref, do_ref, dq_ref, dk_ref, dv_ref, dcq_ref, dc_ref, dk_acc, dv_acc, dc_acc):
        h, i = pl.program_id(0), pl.program_id(1)

        @pl.when(i == 0)
        def _():
            dk_acc[...] = jnp.zeros_like(dk_acc)
            dv_acc[...] = jnp.zeros_like(dv_acc)
            dc_acc[...] = jnp.zeros_like(dc_acc)

        q = q_ref[...]
        cq = _col(cq_ref[...])
        dov = do_ref[...]
        lse_c = _col(lse_ref[...])
        delta = jnp.sum(dov.astype(F32) * o_ref[...].astype(F32), axis=1, keepdims=True)
        row, col = _iota_tile()
        ones = jnp.ones((QB, HD), BF)

        def step(t, carry):
            dq, over_keys = carry
            off = pl.multiple_of(t * KT, KT)
            k = k_ref[pl.ds(off, KT), :]
            v = v_ref[pl.ds(off, KT), :]
            z = _dot_nt(q, k) * SCALE + cq - _fox_tile_row(ct_ref, t, h)
            p = jnp.where(row + (i * QB - t * KT) >= col, jnp.exp(z - lse_c), 0.0)
            dz = p * (_dot_nt(dov, v) - delta)
            dzt = dz.T
            dq = dq + _dot((dz * SCALE).astype(BF), k)
            dk_acc[pl.ds(off, KT), :] += _dot((dzt * SCALE).astype(BF), q)
            dv_acc[pl.ds(off, KT), :] += _dot(p.T.astype(BF), dov)
            dc_acc[pl.ds(off, KT), :] += _tri_dot(dzt, ones)
            return dq, over_keys + jnp.sum(dz, axis=1, keepdims=True)

        dq, over_keys = lax.fori_loop(0, i // (KT // QB) + 1, step, (jnp.zeros((QB, HD), F32), jnp.zeros((QB, 1), F32)))
        dq_ref[...] = dq.astype(BF)
        dcq_ref[...] = jnp.broadcast_to(over_keys, (QB, HD))

        @pl.when(i == nb - 1)
        def _():
            dk_ref[...] = dk_acc[...].astype(BF)
            dv_ref[...] = dv_acc[...].astype(BF)
            dc_ref[...] = dc_acc[...]

    blk = pl.BlockSpec((QB, HD), lambda h, i: (i, h))
    full = pl.BlockSpec((s, HD), lambda h, i: (0, h))
    return pl.pallas_call(
        body, name=name, grid=(nfox, nb),
        in_specs=[pl.BlockSpec((QB, HD), lambda h, i: (i, N_SB + h)),
                  pl.BlockSpec((s, HD), lambda h, i: (0, N_HEADS + N_SB + h)),
                  pl.BlockSpec((s, HD), lambda h, i: (0, 2 * N_HEADS + N_SB + h)),
                  blk, pl.BlockSpec((nb, 8, HD), lambda h, i: (0, 0, 0)), blk, blk,
                  pl.BlockSpec((QB, HD), lambda h, i: (i, N_SB + h))],
        out_specs=[blk, full, full, blk, full],
        out_shape=[_sds((s, nfox * HD), BF)] * 3 + [_sds((s, nfox * HD), F32)] * 2,
        scratch_shapes=[pltpu.VMEM((s, HD), F32)] * 3,
        compiler_params=_cp("parallel", "arbitrary"),
    )(qkv, qkv, qkv, cum_b, cum_t, o, lse, do)


def _dil_rows(r, n, d):
    if d == 1:
        return pl.ds(pl.multiple_of(n * QB, QB), QB)
    return pl.ds(r + n * (QB * d), QB, stride=d)


def _dilated_fwd(qkv, name):
    s = qkv.shape[0]
    npat = len(DILATED_PATTERNS)
    chunk = 256

    def body(q_ref, k_ref, v_ref, out_ref, outt_ref, g_ref, qf, kf, vf, *per_pattern):
        o_s, l_s = per_pattern[:npat], per_pattern[npat:]
        qf[...] = q_ref[...].astype(F32)
        kf[...] = k_ref[...].astype(F32)
        vf[...] = v_ref[...].astype(F32)
        row, col = _iota2()
        for p, (_, d) in enumerate(DILATED_PATTERNS):
            nb = s // d // QB

            def blk(idx, carry, p=p, d=d, nb=nb):
                r, n = idx // nb, idx % nb
                cur = _dil_rows(r, n, d)
                q = qf[cur, :].astype(BF)
                zc = jnp.where(col <= row, _dot_nt(q, kf[cur, :].astype(BF)) * SCALE, NEG_INF)
                m = jnp.max(zc, axis=1, keepdims=True)
                if nb > 1:
                    prv = _dil_rows(r, jnp.maximum(n - 1, 0), d)
                    mp = jnp.logical_and(col >= row, n >= 1)
                    zp = jnp.where(mp, _dot_nt(q, kf[prv, :].astype(BF)) * SCALE, NEG_INF)
                    m = jnp.maximum(m, jnp.max(zp, axis=1, keepdims=True))
                ec = jnp.exp(zc - m)
                l = jnp.sum(ec, axis=1, keepdims=True)
                if nb > 1:
                    ep = jnp.where(mp, jnp.exp(zp - m), 0.0)
                    l = l + jnp.sum(ep, axis=1, keepdims=True)
                o = _dot((ec / l).astype(BF), vf[cur, :].astype(BF))
                if nb > 1:
                    o = o + _dot((ep / l).astype(BF), vf[prv, :].astype(BF))
                o_s[p][cur, :] = o
                l_s[p][cur, :] = jnp.broadcast_to(m + jnp.log(l), (QB, HD))
                return carry

            lax.fori_loop(0, s // QB, blk, 0, unroll=4)
        for c0 in range(0, s, chunk):
            rows = slice(c0, c0 + chunk)
            ls = [l_s[p][rows, :] for p in range(npat)]
            m = functools.reduce(jnp.maximum, ls)
            es = [jnp.exp(l - m) for l in ls]
            tot = functools.reduce(lambda a, b: a + b, es)
            out = functools.reduce(lambda a, b: a + b, [(e / tot) * o_s[p][rows, :] for p, e in enumerate(es)])
            out_ref[rows, :] = out.astype(BF)
            outt_ref[:, rows] = out.T.astype(BF)
            g_ref[rows, :] = m + jnp.log(tot)

    full = pl.BlockSpec((s, HD), lambda h: (0, h))
    return pl.pallas_call(
        body, name=name, grid=(N_HEADS,),
        in_specs=[full, pl.BlockSpec((s, HD), lambda h: (0, N_HEADS + h)), pl.BlockSpec((s, HD), lambda h: (0, 2 * N_HEADS + h))],
        out_specs=[full, pl.BlockSpec((HD, s), lambda h: (h, 0)), full],
        out_shape=[_sds((s, N_HEADS * HD), BF), _sds((N_HEADS * HD, s), BF), _sds((s, N_HEADS * HD), F32)],
        scratch_shapes=[pltpu.VMEM((s, HD), F32)] * (3 + 2 * npat),
        compiler_params=_cp("parallel"),
    )(qkv, qkv, qkv)


def _dilated_bwd(qkv, out, glse, do, tables, name):
    s = qkv.shape[0]
    chunk = 256

    def body(q_ref, k_ref, v_ref, out_ref, g_ref, do_ref, c_ref, sa_ref, sb_ref, dq_ref, dk_ref, dv_ref,
             qf, kf, vf, dof, dl_s, dq_a, dk_a, dv_a):
        qf[...] = q_ref[...].astype(F32)
        kf[...] = k_ref[...].astype(F32)
        vf[...] = v_ref[...].astype(F32)
        for c0 in range(0, s, chunk):
            rows = slice(c0, c0 + chunk)
            dov = do_ref[rows, :].astype(F32)
            dof[rows, :] = dov
            dl_s[rows, :] = jnp.broadcast_to(jnp.sum(dov * out_ref[rows, :].astype(F32), axis=1, keepdims=True), (chunk, HD))
        dq_a[...] = jnp.zeros_like(dq_a)
        dk_a[...] = jnp.zeros_like(dk_a)
        dv_a[...] = jnp.zeros_like(dv_a)
        row, col = _iota2()
        for _, d in DILATED_PATTERNS:
            nb = s // d // QB

            def blk(idx, carry, d=d, nb=nb):
                r, n = idx // nb, idx % nb
                cur = _dil_rows(r, n, d)
                q = qf[cur, :].astype(BF)
                kc = kf[cur, :].astype(BF)
                dov = dof[cur, :].astype(BF)
                g = _col(g_ref[cur, :])
                delta = _col(dl_s[cur, :])
                pc = jnp.where(col <= row, jnp.exp(_dot_nt(q, kc) * SCALE - g), 0.0)
                dzc = pc * (_dot_nt(dov, vf[cur, :].astype(BF)) - delta) * SCALE
                dq = _dot(dzc.astype(BF), kc)
                if nb > 1:
                    prv = _dil_rows(r, jnp.maximum(n - 1, 0), d)
                    kp = kf[prv, :].astype(BF)
                    mp = jnp.logical_and(col >= row, n >= 1)
                    pp = jnp.where(mp, jnp.exp(_dot_nt(q, kp) * SCALE - g), 0.0)
                    dzp = pp * (_dot_nt(dov, vf[prv, :].astype(BF)) - delta) * SCALE
                    dq = dq + _dot(dzp.astype(BF), kp)
                dq_a[cur, :] += dq
                dk_a[cur, :] += _dot(dzc.T.astype(BF), q)
                dv_a[cur, :] += _dot(pc.T.astype(BF), dov)
                if nb > 1:
                    dk_a[prv, :] += _dot(dzp.T.astype(BF), q)
                    dv_a[prv, :] += _dot(pp.T.astype(BF), dov)
                return carry

            lax.fori_loop(0, s // QB, blk, 0, unroll=2)
        for c0 in range(0, s, chunk):
            rows = slice(c0, c0 + chunk)
            c, sa, sb = c_ref[rows, :], sa_ref[rows, :], sb_ref[rows, :]
            dq_ref[rows, :] = _rope(dq_a[rows, :], c, sa, sb).astype(BF)
            dk_ref[rows, :] = _rope(dk_a[rows, :], c, sa, sb).astype(BF)
            dv_ref[rows, :] = dv_a[rows, :].astype(BF)

    full = pl.BlockSpec((s, HD), lambda h: (0, h))
    tab = pl.BlockSpec((s, HD), lambda h: (0, 0))
    return pl.pallas_call(
        body, name=name, grid=(N_HEADS,),
        in_specs=[full, pl.BlockSpec((s, HD), lambda h: (0, N_HEADS + h)), pl.BlockSpec((s, HD), lambda h: (0, 2 * N_HEADS + h)),
                  full, full, full, tab, tab, tab],
        out_specs=[full, full, full], out_shape=[_sds((s, N_HEADS * HD), BF)] * 3,
        scratch_shapes=[pltpu.VMEM((s, HD), F32)] * 8,
        compiler_params=_cp("parallel"),
    )(qkv, qkv, qkv, out, glse, do, *tables)


def _swiglu_fwd(x, gnorm, w, tag, deps=()):
    h, ht = _rmsnorm_fwd(x, gnorm, f"norm_{tag}", deps)
    g, u, act, act_t = _ffn_up(h, w["gate"], w["up"], f"ffn_up_{tag}")
    y = _ffn_down(act, w["down"], x, f"ffn_down_{tag}")
    return y, (x, ht, g, u, act_t)


def _swiglu_bwd(saved, gnorm, w, dy, dyb_half, out_scale, tag, deps=(), on_down=None, on_grads=None):
    x, ht, g, u, act_t = saved
    dg, du = _ffn_bwd_act(dyb_half, w["down"], g, u, f"ffn_bwd_act_{tag}", deps)
    d_down = _grad_rows(act_t, dyb_half, f"ffn_bwd_wd_{tag}")
    tokens = list(on_down(d_down)) if on_down else []
    d_gate, d_up = _grad_cols(ht, [dg, du], [False, False], f"ffn_bwd_wgu_{tag}", tokens)
    gw = {"gate": d_gate, "up": d_up, "down": d_down}
    tokens = list(on_grads(gw)) if on_grads else []
    dh = _dh_cols([dg, du], [w["gate"], w["up"]], [False, False], f"ffn_bwd_dh_{tag}", tokens)
    dx, dxb, dxbt, dgn = _rmsnorm_bwd(x, gnorm, dh, dy, out_scale, f"norm_bwd_{tag}")
    return (dx, dxb, dxbt), dgn, gw


def kernel(x, norm_g, ffn1_w_gate, ffn1_w_up, ffn1_w_down, ffn2_w_gate, ffn2_w_up, ffn2_w_down, even_w_in, even_b_forget, even_w_out, odd_w_qkv, odd_w_out, final_norm_g, loss_target, m_norm_g, m_ffn1_w_gate, m_ffn1_w_up, m_ffn1_w_down, m_ffn2_w_gate, m_ffn2_w_up, m_ffn2_w_down, m_even_w_in, m_even_b_forget, m_even_w_out, m_odd_w_qkv, m_odd_w_out, m_final_norm_g, v_norm_g, v_ffn1_w_gate, v_ffn1_w_up, v_ffn1_w_down, v_ffn2_w_gate, v_ffn2_w_up, v_ffn2_w_down, v_even_w_in, v_even_b_forget, v_even_w_out, v_odd_w_qkv, v_odd_w_out, v_final_norm_g):
    s, d = x.shape[1], x.shape[2]
    nfox = N_HEADS - N_SB
    ax, ay, ac = lax.axis_index("x"), lax.axis_index("y"), lax.axis_index("c")
    me = 4 * ax + 2 * ay + ac
    slots = jnp.stack([4 * px + 2 * py + ac for px, py in [(ax, ay), (1 - ax, ay), (ax, 1 - ay), (1 - ax, 1 - ay)]]).astype(jnp.int32)
    x0 = x.reshape(s, d)
    target = loss_target.reshape(s, d)

    def bf(w):
        return w.astype(BF)

    groups = [
        [bf(ffn1_w_gate[0]), bf(ffn1_w_up[0]), bf(ffn1_w_down[0]), norm_g.reshape(6, d // NDEV)],
        [bf(even_w_in[0]), bf(even_w_out[0])],
        [bf(ffn2_w_gate[0]), bf(ffn2_w_up[0]), bf(ffn2_w_down[0])],
        [bf(ffn1_w_gate[1]), bf(ffn1_w_up[1]), bf(ffn1_w_down[1])],
        [bf(odd_w_qkv[0]), bf(odd_w_out[0])],
        [bf(ffn2_w_gate[1]), bf(ffn2_w_up[1]), bf(ffn2_w_down[1])],
    ]
    started = []
    for k, grp in enumerate(groups):
        started.append(_gather_start(grp, me, [started[-1]["token"]] if started else [], f"gather_start_{k}"))
    all_started = [started[-1]["token"]]

    def gathered(k, after):
        return _gather_finish(_gather_forward(started[k], after, f"gather_forward_{k}"), f"gather_finish_{k}")

    def ffn_weights(ws_):
        return {"gate": ws_[0], "up": ws_[1], "down": ws_[2]}

    b_pad = jnp.pad(even_b_forget, ((0, 0), (0, HD - nfox)))
    gfin = final_norm_g.reshape(1, d)

    g0 = gathered(0, x0)
    gn = jnp.transpose(g0[3], (1, 0, 2)).reshape(6, 1, d)
    wf = [[ffn_weights(g0), None], [None, None]]
    x1, sv_f1_0 = _swiglu_fwd(x0, gn[0], wf[0][0], "l0a", all_started)
    g1 = gathered(1, x1)
    w_in_nat = jnp.transpose(g1[0], (1, 0, 2)).reshape(d, -1)
    w_qkv_e = w_in_nat[:, :3 * d]
    w_f = jnp.pad(w_in_nat[:, 3 * d:], ((0, 0), (0, HD - nfox)))
    w_out_e = g1[1].reshape(d, d)
    h_e, ht_e = _rmsnorm_fwd(x1, gn[1], "norm_l0m")
    qkv_e = _mm_nn(h_e, w_qkv_e, 768, BF, "even_qkv")
    f_e = _mm_nn(h_e, w_f, HD, F32, "even_fgate")
    o_sb, ot_sb, tot_sb = _sb_fwd(qkv_e, "sb_fwd")
    cum_b, cum_t = _fgate_fwd(f_e, b_pad, "fgate_fwd")
    o_fox, ot_fox, lse_fox = _fox_fwd(qkv_e, cum_b, cum_t, "fox_fwd")
    o_e = jnp.concatenate([o_sb, o_fox], axis=1)
    ot_e = jnp.concatenate([ot_sb, ot_fox], axis=0)
    x2 = _mm_nn(o_e, w_out_e, 1024, F32, "even_out", res=x1)
    wf[0][1] = ffn_weights(gathered(2, x2))
    x3, sv_f2_0 = _swiglu_fwd(x2, gn[2], wf[0][1], "l0b")

    wf[1][0] = ffn_weights(gathered(3, x3))
    x4, sv_f1_1 = _swiglu_fwd(x3, gn[3], wf[1][0], "l1a")
    g4 = gathered(4, x4)
    w_qkv_o = g4[0]
    w_out_o = g4[1].reshape(d, d)
    h_o, ht_o = _rmsnorm_fwd(x4, gn[4], "norm_l1m")
    qkv_o = _qkv_rope(h_o, w_qkv_o, _rope_tables(s, 1.0), "odd_qkv")
    o_o, ot_o, glse = _dilated_fwd(qkv_o, "dilated_fwd")
    x5 = _mm_nn(o_o, w_out_o, 1024, F32, "odd_out", res=x4)
    wf[1][1] = ffn_weights(gathered(5, x5))
    x6, sv_f2_1 = _swiglu_fwd(x5, gn[5], wf[1][1], "l1b")

    def chip_sums(gs, a_s, tag):
        ps = [_pair_sum(g_, a_, slots, f"pair_sum_{tag}_{k}") for k, (g_, a_) in enumerate(zip(gs, a_s))]
        return gs, a_s, _chip_start(ps, f"chip_start_{tag}")

    def as_slices(gs):
        return [g_ if g_.ndim == 3 else g_.reshape(NDEV, g_.shape[0] // NDEV, g_.shape[1]) for g_ in gs]

    def reduce_start(gs, tag):
        gs = as_slices(gs)
        return chip_sums(gs, _pair_exchange(gs, f"pair_exchange_{tag}"), tag)

    red, crossing = {}, {}

    def cross(gs, tag):
        crossing[tag] = _pair_start(as_slices(gs), f"pair_start_{tag}")
        return [crossing[tag]["token"]]

    def reduce_behind_dh(tag):
        return lambda gw: cross([gw["gate"], gw["up"], gw["down"]], tag)

    def reduce_after(tag, after):
        red[tag] = chip_sums(*_pair_finish(crossing[tag], after, f"pair_finish_{tag}"), tag)
        return [red[tag][2]["token"]]

    def reduce_now(tag, names):
        def hook(gw):
            red[tag] = reduce_start([gw[nm] for nm in names] if names else [gw], tag)
            return [red[tag][2]["token"]]
        return hook

    dx6, dx6b, _, d_gfin, loss_part = _loss_head(x6, gfin, target, "loss_head")

    (dx5, dx5b, dx5bt), dgn5, _ = _swiglu_bwd(sv_f2_1, gn[5], wf[1][1], dx6, dx6b, 1.0, "l1b", on_grads=reduce_behind_dh("l1b"))
    d_wout_o = _mm_nn(ot_o, dx5b, 1024, BF, "odd_out_dw")
    do_o = _mm_nt([(dx5b, w_out_o)], "odd_out_do", BF, deps=reduce_after("l1b", dx5))
    dqkv_o = jnp.concatenate(_dilated_bwd(qkv_o, o_o, glse, do_o, _rope_tables(s, -1.0), "dilated_bwd"), axis=1)
    (d_wqkv_o,) = _grad_cols(ht_o, [dqkv_o], [True], "odd_qkv_dw")
    dh_o = _dh_cols([dqkv_o], [w_qkv_o], [True], "odd_qkv_dh", cross([d_wqkv_o, d_wout_o], "l1m"))
    dx4, dx4b, _, dgn4 = _rmsnorm_bwd(x4, gn[4], dh_o, dx5, 0.5, "norm_bwd_l1m")
    (dx3, dx3b, _), dgn3, _ = _swiglu_bwd(sv_f1_1, gn[3], wf[1][0], dx4, dx4b, 0.5, "l1a", reduce_after("l1m", dx4),
                                         on_grads=reduce_behind_dh("l1a"))

    (dx2, dx2b, dx2bt), dgn2, _ = _swiglu_bwd(sv_f2_0, gn[2], wf[0][1], dx3, dx3b, 1.0, "l0b", reduce_after("l1a", dx3),
                                             on_grads=reduce_behind_dh("l0b"))
    d_wout_e = _mm_nn(ot_e, dx2b, 1024, BF, "even_out_dw")
    do_e = _mm_nt([(dx2b, w_out_e)], "even_out_do", BF, deps=reduce_after("l0b", dx2))
    dq_sb, dk_sb, dv_sb = _sb_bwd(qkv_e, do_e, tot_sb, "sb_bwd")
    dq_fx, dk_fx, dv_fx, dcq, dck = _fox_bwd(qkv_e, cum_b, cum_t, o_fox, lse_fox, do_e, "fox_bwd")
    df, db_part = _fgate_bwd(dcq, dck, f_e, b_pad, "fgate_bwd")
    dqkv_e = jnp.concatenate([dq_sb, dq_fx, dk_sb, dk_fx, dv_sb, dv_fx], axis=1)
    d_wqkv_e = _mm_nn(ht_e, dqkv_e, 768, BF, "even_qkv_dw")
    d_wf = _mm_nn(ht_e, df, HD, BF, "even_fgate_dw")
    d_win_nat = jnp.concatenate([d_wqkv_e, d_wf[:, :nfox]], axis=1)
    d_win = jnp.transpose(d_win_nat.reshape(d, NDEV, -1), (1, 0, 2))
    dh_e = _mm_nt([(dqkv_e, w_qkv_e), (df, w_f)], "even_in_dh", deps=cross([d_win, d_wout_e], "l0m"))
    dx1, dx1b, _, dgn1 = _rmsnorm_bwd(x1, gn[1], dh_e, dx2, 0.5, "norm_bwd_l0m")
    (dx0, _, _), dgn0, _ = _swiglu_bwd(sv_f1_0, gn[0], wf[0][0], dx1, dx1b, 1.0, "l0a", reduce_after("l0m", dx1),
                                      on_down=reduce_now("l0a_down", None), on_grads=reduce_now("l0a_gu", ["gate", "up"]))

    def reduce_finish(red, tag, after):
        gs, a_s, st = red
        return list(zip(gs, a_s, _chip_finish(st, after, f"chip_finish_{tag}")))

    f_l1b, f_l1m, f_l1a = (reduce_finish(red[t], t, dx0) for t in ("l1b", "l1m", "l1a"))
    f_l0b, f_l0m = (reduce_finish(red[t], t, dx0) for t in ("l0b", "l0m"))

    def update(w_, m_, v_, parts, nm):
        if w_.shape[2] % 128 == 0:
            return _adamw_sharded(w_, m_, v_, parts, slots, f"adamw_{nm}")
        outs = _adamw_sharded(jnp.swapaxes(w_, 1, 2), jnp.swapaxes(m_, 1, 2), jnp.swapaxes(v_, 1, 2), parts, slots,
                              f"adamw_{nm}", transposed=True)
        return [jnp.swapaxes(o, 1, 2) for o in outs]

    res = {}
    res["even_w_in"] = update(even_w_in, m_even_w_in, v_even_w_in, [f_l0m[0]], "even_w_in")
    res["even_w_out"] = _adamw_sharded(even_w_out, m_even_w_out, v_even_w_out, [f_l0m[1]], slots, "adamw_even_w_out")
    res["odd_w_qkv"] = _adamw_sharded(odd_w_qkv, m_odd_w_qkv, v_odd_w_qkv, [f_l1m[0]], slots, "adamw_odd_w_qkv")
    res["odd_w_out"] = _adamw_sharded(odd_w_out, m_odd_w_out, v_odd_w_out, [f_l1m[1]], slots, "adamw_odd_w_out")
    names = ["ffn2_w_gate", "ffn2_w_up", "ffn2_w_down", "ffn1_w_gate", "ffn1_w_up", "ffn1_w_down"]
    ws = [ffn2_w_gate, ffn2_w_up, ffn2_w_down, ffn1_w_gate, ffn1_w_up, ffn1_w_down]
    ms = [m_ffn2_w_gate, m_ffn2_w_up, m_ffn2_w_down, m_ffn1_w_gate, m_ffn1_w_up, m_ffn1_w_down]
    vs = [v_ffn2_w_gate, v_ffn2_w_up, v_ffn2_w_down, v_ffn1_w_gate, v_ffn1_w_up, v_ffn1_w_down]
    for k in range(3):
        res[names[k]] = update(ws[k], ms[k], vs[k], [f_l0b[k], f_l1b[k]], names[k])
    f_l0a = (reduce_finish(red["l0a_gu"], "l0a_gu", res["ffn2_w_down"][1])
             + reduce_finish(red["l0a_down"], "l0a_down", res["ffn2_w_down"][1]))
    for k in range(3, 6):
        res[names[k]] = update(ws[k], ms[k], vs[k], [f_l0a[k - 3], f_l1a[k - 3]], names[k])

    dnorm = jnp.concatenate([dgn0, dgn1, dgn2, dgn3, dgn4, dgn5], axis=0)
    nsm = d // NDEV
    small_rows = (6 * d + d + 2 * HD) // HD
    pad_rows = -small_rows % 8
    part = jnp.concatenate([dnorm.reshape(-1), d_gfin.reshape(-1), db_part.reshape(-1), loss_part.reshape(-1),
                            jnp.zeros((pad_rows * HD,), F32)]).reshape(small_rows + pad_rows, HD)
    (gathered,) = _all_gather([part], "gather_small")

    def pack(ng, bfg, fg):
        full = lax.dynamic_update_slice(jnp.zeros((6, d), F32), ng.reshape(6, nsm), (0, me * nsm))
        return jnp.concatenate([full.reshape(-1), fg.reshape(-1), jnp.pad(bfg.reshape(-1), (0, HD - nfox)),
                                jnp.zeros((HD + pad_rows * HD,), F32)]).reshape(small_rows + pad_rows, HD)

    sm = _adamw_small(pack(norm_g, even_b_forget, final_norm_g), pack(m_norm_g, m_even_b_forget, m_final_norm_g),
                      pack(v_norm_g, v_even_b_forget, v_final_norm_g), gathered, "adamw_small")

    def unpack(t):
        flat = t.reshape(-1)
        ng = lax.dynamic_slice(flat[:6 * d].reshape(6, d), (0, me * nsm), (6, nsm)).reshape(norm_g.shape)
        fg = flat[6 * d:7 * d].reshape(final_norm_g.shape)
        bfg = flat[7 * d:7 * d + nfox].reshape(even_b_forget.shape)
        return ng, bfg, fg

    sm_g, sm_d, sm_m, sm_v = [unpack(t) for t in sm]
    loss = sm[0].reshape(-1)[7 * d + HD]

    order = ["norm_g", "ffn1_w_gate", "ffn1_w_up", "ffn1_w_down", "ffn2_w_gate", "ffn2_w_up", "ffn2_w_down", "even_w_in",
             "even_b_forget", "even_w_out", "odd_w_qkv", "odd_w_out", "final_norm_g"]
    outs = [loss, dx0.reshape(x.shape)]
    for k in range(4):
        smk = [sm_g, sm_d, sm_m, sm_v][k]
        for nm in order:
            if nm == "norm_g":
                outs.append(smk[0])
            elif nm == "even_b_forget":
                outs.append(smk[1])
            elif nm == "final_norm_g":
                outs.append(smk[2])
            else:
                outs.append(res[nm][k])
    return tuple(outs)
```

```python
import functools
import math

import jax
import jax.numpy as jnp
from jax import lax
from jax.experimental import pallas as pl
from jax.experimental.pallas import tpu as pltpu

F32 = jnp.float32
BF = jnp.bfloat16
NDEV = 8
HD = 128
QB = 128
N_HEADS = 16
N_SB = 8
SCALE = HD ** -0.5
ROPE_THETA = 500000.0
ROPE_DIMS = HD // 4
DILATED_PATTERNS = ((128, 1), (512, 4), (2048, 16))
RMS_EPS = 1e-6
NEG_INF = -1e30
ADAM_LR = 0.001
ADAM_B1 = 0.9
ADAM_B2 = 0.999
ADAM_EPS = 1e-08
ADAM_WD = 0.01
ADAM_STEP = 10
VMEM_LIMIT_V7X = 56 * 1024 * 1024
MESH = pl.DeviceIdType.MESH
ANY = pl.BlockSpec(memory_space=pl.ANY)

NT_DIMS = (((1,), (1,)), ((), ()))


def _cp(*dims):
    return pltpu.CompilerParams(dimension_semantics=dims if dims else None, vmem_limit_bytes=VMEM_LIMIT_V7X)


def _dot(a, b):
    return jnp.dot(a, b, preferred_element_type=F32)


def _dot_nt(a, b):
    return lax.dot_general(a, b, NT_DIMS, preferred_element_type=F32)


def _sds(shape, dtype):
    return jax.ShapeDtypeStruct(shape, dtype)


def _place():
    x, y, c = lax.axis_index("x"), lax.axis_index("y"), lax.axis_index("c")
    chips = [(x, y), (1 - x, y), (x, 1 - y), (1 - x, 1 - y)]
    return x, y, c, chips


def _all_gather(xs, name):
    n = len(xs)

    def body(*refs):
        x_refs, out_refs = refs[:n], refs[n:2 * n]
        send_sems, recv_sems, local_sems = refs[2 * n:]
        x, y, c, chips = _place()
        me, sibling = (x, y, c), (x, y, 1 - c)
        others = chips[1:]

        def slot(a, px, py, pc):
            return out_refs[a].at[4 * px + 2 * py + pc]

        def copy(a, k, block, to, src=None):
            return pltpu.make_async_remote_copy(
                src_ref=slot(a, *block) if src is None else src, dst_ref=slot(a, *block),
                send_sem=send_sems.at[a, k], recv_sem=recv_sems.at[a, k], device_id=to, device_id_type=MESH)

        started = []
        for a in range(n):
            mine = pltpu.make_async_copy(x_refs[a], slot(a, *me), local_sems.at[a])
            mine.start()
            first = [copy(a, 0, me, sibling, src=x_refs[a])]
            first += [copy(a, 1 + j, me, (*chip, c), src=x_refs[a]) for j, chip in enumerate(others)]
            for cp in first:
                cp.start()
            started += [mine.wait] + [cp.wait_send for cp in first]
        for a in range(n):
            for j, chip in enumerate(others):
                copy(a, 1 + j, (*chip, c), me).wait_recv()
                passed = copy(a, 4 + j, (*chip, c), sibling)
                passed.start()
                started.append(passed.wait_send)
        for a in range(n):
            copy(a, 0, sibling, me).wait_recv()
            for j, chip in enumerate(others):
                copy(a, 4 + j, (*chip, 1 - c), me).wait_recv()
        for w in started:
            w()

    return pl.pallas_call(
        body, name=name,
        out_shape=[_sds((NDEV,) + x.shape, x.dtype) for x in xs],
        in_specs=[ANY] * n, out_specs=[ANY] * n,
        scratch_shapes=[pltpu.SemaphoreType.DMA((n, 7)), pltpu.SemaphoreType.DMA((n, 7)), pltpu.SemaphoreType.DMA((n,))],
    )(*xs)


def _pair_exchange(gs, name):
    n = len(gs)

    def body(*refs):
        g_refs, a_refs = refs[:n], refs[n:2 * n]
        send_sems, recv_sems = refs[2 * n:]
        x, y, c, chips = _place()
        copies = []
        for a in range(n):
            for j, (px, py) in enumerate(chips):
                copies.append(pltpu.make_async_remote_copy(
                    src_ref=g_refs[a].at[4 * px + 2 * py + (1 - c)], dst_ref=a_refs[a].at[j],
                    send_sem=send_sems.at[a, j], recv_sem=recv_sems.at[a, j],
                    device_id=(x, y, 1 - c), device_id_type=MESH))
        for cp in copies:
            cp.start()
        for cp in copies:
            cp.wait()

    return pl.pallas_call(
        body, name=name,
        out_shape=[_sds((4,) + g.shape[1:], g.dtype) for g in gs],
        in_specs=[ANY] * n, out_specs=[ANY] * n,
        scratch_shapes=[pltpu.SemaphoreType.DMA((n, 4)), pltpu.SemaphoreType.DMA((n, 4))],
    )(*gs)


HBM = pl.BlockSpec(memory_space=pltpu.HBM)
SEM = pl.BlockSpec(memory_space=pltpu.SEMAPHORE)
EFFECT = pltpu.SideEffectType.DATAFLOW_SIDE_EFFECTING
TOKEN = _sds((8, 128), F32)
TOKEN_SPEC = pl.BlockSpec((8, 128), lambda *_: (0, 0))


def _in_hbm(x):
    return pltpu.with_memory_space_constraint(x, pltpu.HBM)


def _ignore_deps(body, n_in, n_deps):
    if not n_deps:
        return body
    return lambda *refs: body(*refs[:n_in], *refs[n_in + n_deps:])


def _slot_of(px, py, pc):
    return 4 * px + 2 * py + pc


def _gather_start(xs, me, deps, name):
    n = len(xs)
    lands = [lax.dynamic_update_slice(lax.empty((NDEV,) + x.shape, x.dtype), x[None], (me,) + (0,) * x.ndim) for x in xs]

    def body(*refs):
        x_refs, land_refs = refs[:n], refs[n:2 * n]
        send, recv_ici, recv_sib = refs[2 * n:2 * n + 3]
        token = refs[4 * n + 3]
        x, y, c, chips = _place()
        for a in range(n):
            dst = land_refs[a].at[_slot_of(x, y, c)]
            pltpu.make_async_remote_copy(src_ref=x_refs[a], dst_ref=dst, send_sem=send.at[4 * a], recv_sem=recv_sib.at[a],
                                         device_id=(x, y, 1 - c), device_id_type=MESH).start()
            for j, chip in enumerate(chips[1:]):
                pltpu.make_async_remote_copy(src_ref=x_refs[a], dst_ref=dst, send_sem=send.at[4 * a + 1 + j], recv_sem=recv_ici.at[3 * a + j],
                                             device_id=(*chip, c), device_id_type=MESH).start()
        token[...] = jnp.zeros_like(token)

    outs = pl.pallas_call(
        _ignore_deps(body, 2 * n, len(deps)), name=name,
        out_shape=(pltpu.SemaphoreType.DMA((4 * n,)), pltpu.SemaphoreType.DMA((3 * n,)), pltpu.SemaphoreType.DMA((n,)),
                   *[pltpu.HBM(x.shape, x.dtype) for x in xs], *[pltpu.HBM(l.shape, l.dtype) for l in lands], TOKEN),
        in_specs=[HBM] * (2 * n) + [TOKEN_SPEC] * len(deps),
        out_specs=(SEM, SEM, SEM, *[HBM] * (2 * n), pl.BlockSpec(memory_space=pltpu.VMEM)),
        input_output_aliases={a: 3 + a for a in range(2 * n)},
        compiler_params=pltpu.CompilerParams(has_side_effects=EFFECT),
    )(*[_in_hbm(x) for x in xs], *[_in_hbm(l) for l in lands], *deps)
    send, recv_ici, recv_sib = outs[:3]
    return dict(send=send, recv_ici=recv_ici, recv_sib=recv_sib, xs=list(outs[3:3 + n]), lands=list(outs[3 + n:3 + 2 * n]), token=outs[-1])


def _gather_forward(st, after, name):
    n = len(st["lands"])

    def body(*refs):
        land_refs, recv_ici = refs[:n], refs[n]
        send2, recv2 = refs[n + 2], refs[n + 3]
        x, y, c, chips = _place()
        for a in range(n):
            for j, chip in enumerate(chips[1:]):
                blk = land_refs[a].at[_slot_of(*chip, c)]
                pltpu.make_async_remote_copy(src_ref=blk, dst_ref=blk, send_sem=send2.at[3 * a + j], recv_sem=recv_ici.at[3 * a + j],
                                             device_id=(*chip, c), device_id_type=MESH).wait_recv()
                pltpu.make_async_remote_copy(src_ref=blk, dst_ref=blk, send_sem=send2.at[3 * a + j], recv_sem=recv2.at[3 * a + j],
                                             device_id=(x, y, 1 - c), device_id_type=MESH).start()

    outs = pl.pallas_call(
        body, name=name,
        out_shape=(pltpu.SemaphoreType.DMA((3 * n,)), pltpu.SemaphoreType.DMA((3 * n,)), *[pltpu.HBM(l.shape, l.dtype) for l in st["lands"]]),
        in_specs=[HBM] * n + [SEM, pl.BlockSpec(memory_space=pl.ANY)],
        out_specs=(SEM, SEM, *[HBM] * n),
        input_output_aliases={a: 2 + a for a in range(n)},
        compiler_params=pltpu.CompilerParams(has_side_effects=EFFECT),
    )(*st["lands"], st["recv_ici"], after)
    return dict(st, send2=outs[0], recv2=outs[1], lands=list(outs[2:]))


def _gather_finish(st, after, name):
    n = len(st["lands"])

    def body(*refs):
        x_refs, land_refs = refs[:n], refs[n:2 * n]
        send, recv_sib, send2, recv2 = refs[2 * n:2 * n + 4]
        x, y, c, chips = _place()
        for a in range(n):
            mine = land_refs[a].at[_slot_of(x, y, c)]
            theirs = land_refs[a].at[_slot_of(x, y, 1 - c)]
            for k in range(4):
                pltpu.make_async_remote_copy(src_ref=x_refs[a], dst_ref=mine, send_sem=send.at[4 * a + k], recv_sem=recv_sib.at[a],
                                             device_id=(x, y, 1 - c), device_id_type=MESH).wait_send()
            pltpu.make_async_remote_copy(src_ref=x_refs[a], dst_ref=theirs, send_sem=send.at[4 * a], recv_sem=recv_sib.at[a],
                                         device_id=(x, y, 1 - c), device_id_type=MESH).wait_recv()
            for j, chip in enumerate(chips[1:]):
                sent = land_refs[a].at[_slot_of(*chip, c)]
                got = land_refs[a].at[_slot_of(*chip, 1 - c)]
                pltpu.make_async_remote_copy(src_ref=sent, dst_ref=sent, send_sem=send2.at[3 * a + j], recv_sem=recv2.at[3 * a + j],
                                             device_id=(x, y, 1 - c), device_id_type=MESH).wait_send()
                pltpu.make_async_remote_copy(src_ref=got, dst_ref=got, send_sem=send2.at[3 * a + j], recv_sem=recv2.at[3 * a + j],
                                             device_id=(x, y, 1 - c), device_id_type=MESH).wait_recv()

    outs = pl.pallas_call(
        body, name=name,
        out_shape=tuple(pltpu.HBM(v.shape, v.dtype) for v in st["xs"] + st["lands"]),
        in_specs=[HBM] * (2 * n) + [SEM] * 4 + [pl.BlockSpec(memory_space=pl.ANY)], out_specs=tuple([HBM] * (2 * n)),
        input_output_aliases={a: a for a in range(2 * n)},
        compiler_params=pltpu.CompilerParams(has_side_effects=EFFECT),
    )(*st["xs"], *st["lands"], st["send"], st["recv_sib"], st["send2"], st["recv2"], after)
    return list(outs[n:])


def _pair_start(gs, name):
    n = len(gs)
    lands = [lax.empty((4,) + g.shape[1:], g.dtype) for g in gs]

    def body(*refs):
        g_refs, a_refs = refs[:n], refs[n:2 * n]
        send, recv = refs[2 * n], refs[2 * n + 1]
        token = refs[4 * n + 2]
        x, y, c, chips = _place()
        for a in range(n):
            for j, (px, py) in enumerate(chips):
                pltpu.make_async_remote_copy(src_ref=g_refs[a].at[_slot_of(px, py, 1 - c)], dst_ref=a_refs[a].at[j],
                                             send_sem=send.at[4 * a + j], recv_sem=recv.at[4 * a + j],
                                             device_id=(x, y, 1 - c), device_id_type=MESH).start()
        token[...] = jnp.zeros_like(token)

    outs = pl.pallas_call(
        body, name=name,
        out_shape=(pltpu.SemaphoreType.DMA((4 * n,)), pltpu.SemaphoreType.DMA((4 * n,)),
                   *[pltpu.HBM(g.shape, g.dtype) for g in gs], *[pltpu.HBM(l.shape, l.dtype) for l in lands], TOKEN),
        in_specs=[HBM] * (2 * n), out_specs=(SEM, SEM, *[HBM] * (2 * n), pl.BlockSpec(memory_space=pltpu.VMEM)),
        input_output_aliases={a: 2 + a for a in range(2 * n)},
        compiler_params=pltpu.CompilerParams(has_side_effects=EFFECT),
    )(*[_in_hbm(g) for g in gs], *[_in_hbm(l) for l in lands])
    return dict(send=outs[0], recv=outs[1], gs=list(outs[2:2 + n]), lands=list(outs[2 + n:2 + 2 * n]), token=outs[-1])


def _pair_finish(st, after, name):
    n = len(st["gs"])

    def body(*refs):
        g_refs, a_refs = refs[:n], refs[n:2 * n]
        send, recv = refs[2 * n], refs[2 * n + 1]
        x, y, c, chips = _place()
        for a in range(n):
            for j, (px, py) in enumerate(chips):
                cp = pltpu.make_async_remote_copy(src_ref=g_refs[a].at[_slot_of(px, py, 1 - c)], dst_ref=a_refs[a].at[j],
                                                  send_sem=send.at[4 * a + j], recv_sem=recv.at[4 * a + j],
                                                  device_id=(x, y, 1 - c), device_id_type=MESH)
                cp.wait_send()
                cp.wait_recv()

    outs = pl.pallas_call(
        body, name=name,
        out_shape=tuple(pltpu.HBM(v.shape, v.dtype) for v in st["gs"] + st["lands"]),
        in_specs=[HBM] * (2 * n) + [SEM, SEM, pl.BlockSpec(memory_space=pl.ANY)], out_specs=tuple([HBM] * (2 * n)),
        input_output_aliases={a: a for a in range(2 * n)},
        compiler_params=pltpu.CompilerParams(has_side_effects=EFFECT),
    )(*st["gs"], *st["lands"], st["send"], st["recv"], after)
    return list(outs[:n]), list(outs[n:])


def _chip_start(ps, name):
    n = len(ps)
    lands = [lax.empty(p.shape, p.dtype) for p in ps]

    def body(*refs):
        p_refs, b_refs = refs[:n], refs[n:2 * n]
        send, recv = refs[2 * n], refs[2 * n + 1]
        token = refs[4 * n + 2]
        x, y, c, chips = _place()
        for a in range(n):
            for j, chip in enumerate(chips[1:]):
                pltpu.make_async_remote_copy(src_ref=p_refs[a].at[j], dst_ref=b_refs[a].at[j], send_sem=send.at[3 * a + j], recv_sem=recv.at[3 * a + j],
                                             device_id=(*chip, c), device_id_type=MESH).start()
        token[...] = jnp.zeros_like(token)

    outs = pl.pallas_call(
        body, name=name,
        out_shape=(pltpu.SemaphoreType.DMA((3 * n,)), pltpu.SemaphoreType.DMA((3 * n,)),
                   *[pltpu.HBM(p.shape, p.dtype) for p in ps], *[pltpu.HBM(p.shape, p.dtype) for p in ps], TOKEN),
        in_specs=[HBM] * (2 * n), out_specs=(SEM, SEM, *[HBM] * (2 * n), pl.BlockSpec(memory_space=pltpu.VMEM)),
        input_output_aliases={a: 2 + a for a in range(2 * n)},
        compiler_params=pltpu.CompilerParams(has_side_effects=EFFECT),
    )(*[_in_hbm(p) for p in ps], *[_in_hbm(l) for l in lands])
    return dict(send=outs[0], recv=outs[1], ps=list(outs[2:2 + n]), lands=list(outs[2 + n:2 + 2 * n]), token=outs[-1])


def _chip_finish(st, after, name):
    n = len(st["ps"])

    def body(*refs):
        p_refs, b_refs = refs[:n], refs[n:2 * n]
        send, recv = refs[2 * n], refs[2 * n + 1]
        x, y, c, chips = _place()
        for a in range(n):
            for j, chip in enumerate(chips[1:]):
                cp = pltpu.make_async_remote_copy(src_ref=p_refs[a].at[j], dst_ref=b_refs[a].at[j], send_sem=send.at[3 * a + j], recv_sem=recv.at[3 * a + j],
                                                  device_id=(*chip, c), device_id_type=MESH)
                cp.wait_send()
                cp.wait_recv()

    outs = pl.pallas_call(
        body, name=name,
        out_shape=tuple(pltpu.HBM(v.shape, v.dtype) for v in st["ps"] + st["lands"]),
        in_specs=[HBM] * (2 * n) + [SEM, SEM, pl.BlockSpec(memory_space=pl.ANY)], out_specs=tuple([HBM] * (2 * n)),
        input_output_aliases={a: a for a in range(2 * n)},
        compiler_params=pltpu.CompilerParams(has_side_effects=EFFECT),
    )(*st["ps"], *st["lands"], st["send"], st["recv"], after)
    return list(outs[n:])


def _rows_tile(r):
    for t in (512, 256, 128, 64, 32, 16):
        if r % t == 0:
            return t
    return r


def _pair_sum(g, a, slots, name):
    _, r, c = g.shape
    tr = _rows_tile(r)

    def body(slots_ref, g_ref, a_ref, p_ref):
        p_ref[...] = (g_ref[...].astype(F32) + a_ref[...].astype(F32)).astype(BF)

    return pl.pallas_call(
        body, name=name,
        grid_spec=pltpu.PrefetchScalarGridSpec(
            num_scalar_prefetch=1, grid=(3, r // tr),
            in_specs=[pl.BlockSpec((None, tr, c), lambda j, i, s: (s[j + 1], i, 0)),
                      pl.BlockSpec((None, tr, c), lambda j, i, s: (j + 1, i, 0))],
            out_specs=pl.BlockSpec((None, tr, c), lambda j, i, s: (j, i, 0))),
        out_shape=_sds((3, r, c), BF), compiler_params=_cp("parallel", "parallel"),
    )(slots, g, a)


def _adamw_math(w, g, m, v):
    m = ADAM_B1 * m + (1.0 - ADAM_B1) * g
    v = ADAM_B2 * v + (1.0 - ADAM_B2) * (g * g)
    m_hat = m / (1.0 - ADAM_B1 ** ADAM_STEP)
    v_hat = v / (1.0 - ADAM_B2 ** ADAM_STEP)
    delta = -ADAM_LR * (m_hat / (jnp.sqrt(v_hat) + ADAM_EPS) + ADAM_WD * w)
    return delta, m, v


def _adamw_sharded(w, m, v, parts, slots, name, transposed=False):
    nl = w.shape[0]
    r, c = parts[0][0].shape[1:]
    tr = _rows_tile(r)
    if c * tr * 4 > (1 << 21) and not transposed:
        tr = max(8, tr // 2)

    def body(slots_ref, w_ref, m_ref, v_ref, *rest):
        part_refs, (g_out, d_out, m_out, v_out) = rest[:5 * nl], rest[5 * nl:]
        layer = pl.program_id(0)
        g = None
        for l in range(nl):
            s = part_refs[5 * l][...].astype(F32)
            for ref in part_refs[5 * l + 1:5 * l + 5]:
                s = s + ref[...].astype(F32)
            g = s if g is None else jnp.where(layer == l, s, g)
        if transposed:
            g = g.T
        delta, mn, vn = _adamw_math(w_ref[...], g, m_ref[...], v_ref[...])
        g_out[...] = g
        d_out[...] = delta
        m_out[...] = mn
        v_out[...] = vn

    def own(l):
        return lambda L, i, s: (s[0], jnp.where(L == l, i, 0), 0)

    def fixed(l, k):
        return lambda L, i, s: (k, jnp.where(L == l, i, 0), 0)

    if transposed:
        wspec = pl.BlockSpec((None, c, tr), lambda L, i, s: (L, 0, i))
    else:
        wspec = pl.BlockSpec((None, tr, c), lambda L, i, s: (L, i, 0))
    in_specs = [wspec, wspec, wspec]
    args = [w, m, v]
    for l, (g, a, b) in enumerate(parts):
        in_specs += [pl.BlockSpec((None, tr, c), own(l)), pl.BlockSpec((None, tr, c), fixed(l, 0)),
                     pl.BlockSpec((None, tr, c), fixed(l, 0)), pl.BlockSpec((None, tr, c), fixed(l, 1)),
                     pl.BlockSpec((None, tr, c), fixed(l, 2))]
        args += [g, a, b, b, b]
    return pl.pallas_call(
        body, name=name,
        grid_spec=pltpu.PrefetchScalarGridSpec(
            num_scalar_prefetch=1, grid=(nl, r // tr), in_specs=in_specs, out_specs=[wspec] * 4),
        out_shape=[_sds(w.shape, F32)] * 4, compiler_params=_cp("arbitrary", "arbitrary"),
    )(slots, *args)


def _adamw_small(w, m, v, gathered, name):
    def body(w_ref, m_ref, v_ref, gg_ref, g_out, d_out, m_out, v_out):
        g = gg_ref[0]
        for k in range(1, NDEV):
            g = g + gg_ref[k]
        delta, mn, vn = _adamw_math(w_ref[...], g, m_ref[...], v_ref[...])
        g_out[...] = g
        d_out[...] = delta
        m_out[...] = mn
        v_out[...] = vn

    return pl.pallas_call(body, name=name, out_shape=[_sds(w.shape, F32)] * 4)(w, m, v, gathered)


def _rmsnorm_fwd(x, g, name, deps=()):
    s, d = x.shape
    tm = 256

    def body(x_ref, g_ref, h_ref, ht_ref):
        xf = x_ref[...]
        y = xf * lax.rsqrt(jnp.mean(xf * xf, axis=-1, keepdims=True) + RMS_EPS)
        h = y * g_ref[...]
        h_ref[...] = h.astype(BF)
        ht_ref[...] = h.T.astype(BF)

    return pl.pallas_call(
        _ignore_deps(body, 2, len(deps)), name=name, grid=(s // tm,),
        in_specs=[pl.BlockSpec((tm, d), lambda i: (i, 0)), pl.BlockSpec((1, d), lambda i: (0, 0))] + [TOKEN_SPEC] * len(deps),
        out_specs=[pl.BlockSpec((tm, d), lambda i: (i, 0)), pl.BlockSpec((d, tm), lambda i: (0, i))],
        out_shape=[_sds((s, d), BF), _sds((d, s), BF)], compiler_params=_cp("parallel"),
    )(x, g, *deps)


def _rmsnorm_bwd(x, g, dh, dres, out_scale, name):
    s, d = x.shape
    tm = 256

    def body(x_ref, g_ref, dh_ref, dres_ref, dx_ref, dxb_ref, dxbt_ref, dg_ref):
        xf = x_ref[...]
        r = lax.rsqrt(jnp.mean(xf * xf, axis=-1, keepdims=True) + RMS_EPS)
        xhat = xf * r
        dhv = dh_ref[...]
        dxhat = dhv * g_ref[...]
        dx = dres_ref[...] + r * (dxhat - xhat * jnp.mean(dxhat * xhat, axis=-1, keepdims=True))
        dx_ref[...] = dx
        scaled = dx * out_scale
        dxb_ref[...] = scaled.astype(BF)
        dxbt_ref[...] = scaled.T.astype(BF)

        @pl.when(pl.program_id(0) == 0)
        def _():
            dg_ref[...] = jnp.zeros_like(dg_ref)

        dg_ref[...] += jnp.sum(dhv * xhat, axis=0, keepdims=True)

    row = pl.BlockSpec((tm, d), lambda i: (i, 0))
    vec = pl.BlockSpec((1, d), lambda i: (0, 0))
    return pl.pallas_call(
        body, name=name, grid=(s // tm,),
        in_specs=[row, vec, row, row],
        out_specs=[row, row, pl.BlockSpec((d, tm), lambda i: (0, i)), vec],
        out_shape=[_sds((s, d), F32), _sds((s, d), BF), _sds((d, s), BF), _sds((1, d), F32)],
        compiler_params=_cp("arbitrary"),
    )(x, g, dh, dres)


def _loss_head(x, g, target, name):
    s, d = x.shape
    tm = 256

    def body(x_ref, g_ref, t_ref, dx_ref, dxb_ref, dxbt_ref, dg_ref, loss_ref):
        xf = x_ref[...]
        r = lax.rsqrt(jnp.mean(xf * xf, axis=-1, keepdims=True) + RMS_EPS)
        xhat = xf * r
        err = xhat * g_ref[...] - t_ref[...]
        dy = err * (1.0 / d)
        dxhat = dy * g_ref[...]
        dx = r * (dxhat - xhat * jnp.mean(dxhat * xhat, axis=-1, keepdims=True))
        dx_ref[...] = dx
        half = dx * 0.5
        dxb_ref[...] = half.astype(BF)
        dxbt_ref[...] = half.T.astype(BF)

        @pl.when(pl.program_id(0) == 0)
        def _():
            dg_ref[...] = jnp.zeros_like(dg_ref)
            loss_ref[...] = jnp.zeros_like(loss_ref)

        dg_ref[...] += jnp.sum(dy * xhat, axis=0, keepdims=True)
        part = 0.5 * jnp.sum(jnp.mean(err * err, axis=-1, keepdims=True), axis=0, keepdims=True)
        lane = lax.broadcasted_iota(jnp.int32, (1, 128), 1)
        loss_ref[...] += jnp.where(lane == 0, part, 0.0)

    row = pl.BlockSpec((tm, d), lambda i: (i, 0))
    vec = pl.BlockSpec((1, d), lambda i: (0, 0))
    return pl.pallas_call(
        body, name=name, grid=(s // tm,),
        in_specs=[row, vec, row],
        out_specs=[row, row, pl.BlockSpec((d, tm), lambda i: (0, i)), vec, pl.BlockSpec((1, 128), lambda i: (0, 0))],
        out_shape=[_sds((s, d), F32), _sds((s, d), BF), _sds((d, s), BF), _sds((1, d), F32), _sds((1, 128), F32)],
        compiler_params=_cp("arbitrary"),
    )(x, g, target)


def _act_spec(tm, n, natural, order):
    if natural:
        return pl.BlockSpec((tm, n), (lambda s, i: (i, s)) if order == "si" else (lambda i, s: (i, s)))
    return pl.BlockSpec((None, tm, n), (lambda s, i: (s, i, 0)) if order == "si" else (lambda i, s: (s, i, 0)))


def _act_shape(s, n, natural, dtype):
    return _sds((s, NDEV * n), dtype) if natural else _sds((NDEV, s, n), dtype)


def _ffn_up(h, wg, wu, name):
    s, d = h.shape
    n = wg.shape[2]
    tm = 512

    def body(h_ref, wg_ref, wu_ref, g_ref, u_ref, a_ref, at_ref):
        hb = h_ref[...]
        g = _dot(hb, wg_ref[...])
        u = _dot(hb, wu_ref[...])
        g_ref[...] = g.astype(BF)
        u_ref[...] = u.astype(BF)
        act = g * jax.nn.sigmoid(g) * u
        a_ref[...] = act.astype(BF)
        at_ref[...] = act.T.astype(BF)

    wsp = pl.BlockSpec((None, d, n), lambda s_, i: (s_, 0, 0))
    blk = _act_spec(tm, n, False, "si")
    return pl.pallas_call(
        body, name=name, grid=(NDEV, s // tm),
        in_specs=[pl.BlockSpec((tm, d), lambda s_, i: (i, 0)), wsp, wsp],
        out_specs=[blk] * 3 + [pl.BlockSpec((None, n, tm), lambda s_, i: (s_, 0, i))],
        out_shape=[_act_shape(s, n, False, BF)] * 3 + [_sds((NDEV, n, s), BF)],
        compiler_params=_cp("parallel", "parallel"),
    )(h, wg, wu)


def _ffn_down(act, wd, x, name):
    _, s, n = act.shape
    d = wd.shape[2]
    tm = 512

    def body(a_ref, w_ref, x_ref, o_ref, acc):
        k = pl.program_id(1)

        @pl.when(k == 0)
        def _():
            acc[...] = jnp.zeros_like(acc)

        acc[...] += _dot(a_ref[...], w_ref[...])

        @pl.when(k == NDEV - 1)
        def _():
            o_ref[...] = x_ref[...] + 0.5 * acc[...]

    row = pl.BlockSpec((tm, d), lambda i, k: (i, 0))
    return pl.pallas_call(
        body, name=name, grid=(s // tm, NDEV),
        in_specs=[_act_spec(tm, n, False, "is"), pl.BlockSpec((None, n, d), lambda i, k: (k, 0, 0)), row],
        out_specs=row, out_shape=_sds((s, d), F32),
        scratch_shapes=[pltpu.VMEM((tm, d), F32)], compiler_params=_cp("parallel", "arbitrary"),
    )(act, wd, x)


def _ffn_bwd_act(dyb, wd, g, u, name, deps=()):
    s, d = dyb.shape
    n = wd.shape[1]
    tm = 512

    def body(dy_ref, w_ref, g_ref, u_ref, dg_ref, du_ref):
        dact = _dot_nt(dy_ref[...], w_ref[...])
        gv = g_ref[...].astype(F32)
        uv = u_ref[...].astype(F32)
        sig = jax.nn.sigmoid(gv)
        dg_ref[...] = (dact * uv * (sig * (1.0 + gv * (1.0 - sig)))).astype(BF)
        du_ref[...] = (dact * (gv * sig)).astype(BF)

    blk = _act_spec(tm, n, False, "si")
    return pl.pallas_call(
        _ignore_deps(body, 4, len(deps)), name=name, grid=(NDEV, s // tm),
        in_specs=[pl.BlockSpec((tm, d), lambda s_, i: (i, 0)), pl.BlockSpec((None, n, d), lambda s_, i: (s_, 0, 0)), blk, blk]
        + [TOKEN_SPEC] * len(deps),
        out_specs=[blk, blk], out_shape=[_act_shape(s, n, False, BF)] * 2,
        compiler_params=_cp("parallel", "parallel"),
    )(dyb, wd, g, u, *deps)


def _grad_rows(act_t, dyb, name):
    _, n, s = act_t.shape
    d = dyb.shape[1]
    tn = 1024

    def body(a_ref, dy_ref, o_ref):
        o_ref[...] = _dot(a_ref[...], dy_ref[...]).astype(BF)

    return pl.pallas_call(
        body, name=name, grid=(NDEV, d // tn),
        in_specs=[pl.BlockSpec((None, n, s), lambda k, j: (k, 0, 0)), pl.BlockSpec((s, tn), lambda k, j: (0, j))],
        out_specs=pl.BlockSpec((None, n, tn), lambda k, j: (k, 0, j)), out_shape=_sds((NDEV, n, d), BF),
        compiler_params=_cp("parallel", "parallel"),
    )(act_t, dyb)


def _grad_cols(ht, dxs, naturals, name, deps=()):
    d, s = ht.shape
    k = len(dxs)
    ns = [dx.shape[1] // NDEV if nat else dx.shape[2] for dx, nat in zip(dxs, naturals)]
    td = 512

    def body(*refs):
        ht_ref, dx_refs, o_refs = refs[0], refs[1:1 + k], refs[1 + k:]
        hv = ht_ref[...]
        for dx_ref, o_ref in zip(dx_refs, o_refs):
            o_ref[...] = _dot(hv, dx_ref[...]).astype(BF)

    def dx_spec(n, nat):
        if nat:
            return pl.BlockSpec((s, n), lambda s_, j: (0, s_))
        return pl.BlockSpec((None, s, n), lambda s_, j: (s_, 0, 0))

    return pl.pallas_call(
        _ignore_deps(body, 1 + k, len(deps)), name=name, grid=(NDEV, d // td),
        in_specs=[pl.BlockSpec((td, s), lambda s_, j: (j, 0))] + [dx_spec(n, nat) for n, nat in zip(ns, naturals)]
        + [TOKEN_SPEC] * len(deps),
        out_specs=[pl.BlockSpec((None, td, n), lambda s_, j: (s_, j, 0)) for n in ns],
        out_shape=[_sds((NDEV, d, n), BF) for n in ns], compiler_params=_cp("parallel", "parallel"),
    )(ht, *dxs, *deps)


def _dh_cols(dxs, ws, naturals, name, deps=()):
    k = len(dxs)
    d = ws[0].shape[1]
    ns = [w.shape[2] for w in ws]
    s = dxs[0].shape[0] if naturals[0] else dxs[0].shape[1]
    tm = 512

    def body(*refs):
        dx_refs, w_refs, o_ref, acc = refs[:k], refs[k:2 * k], refs[2 * k], refs[2 * k + 1]
        j = pl.program_id(1)

        @pl.when(j == 0)
        def _():
            acc[...] = jnp.zeros_like(acc)

        t = _dot_nt(dx_refs[0][...], w_refs[0][...])
        for dx_ref, w_ref in zip(dx_refs[1:], w_refs[1:]):
            t = t + _dot_nt(dx_ref[...], w_ref[...])
        acc[...] += t

        @pl.when(j == NDEV - 1)
        def _():
            o_ref[...] = acc[...]

    return pl.pallas_call(
        _ignore_deps(body, 2 * k, len(deps)), name=name, grid=(s // tm, NDEV),
        in_specs=[_act_spec(tm, n, nat, "is") for n, nat in zip(ns, naturals)]
        + [pl.BlockSpec((None, d, n), lambda i, j: (j, 0, 0)) for n in ns] + [TOKEN_SPEC] * len(deps),
        out_specs=pl.BlockSpec((tm, d), lambda i, j: (i, 0)), out_shape=_sds((s, d), F32),
        scratch_shapes=[pltpu.VMEM((tm, d), F32)], compiler_params=_cp("parallel", "arbitrary"),
    )(*dxs, *ws, *deps)


def _mm_nn(a, b, tn, out_dtype, name, res=None, tm=512):
    m, k = a.shape
    nn = b.shape[1]

    def body(*refs):
        if res is None:
            a_ref, b_ref, o_ref = refs
            o_ref[...] = _dot(a_ref[...], b_ref[...]).astype(out_dtype)
        else:
            a_ref, b_ref, r_ref, o_ref = refs
            o_ref[...] = (r_ref[...] + _dot(a_ref[...], b_ref[...])).astype(out_dtype)

    osp = pl.BlockSpec((tm, tn), lambda j, i: (i, j))
    in_specs = [pl.BlockSpec((tm, k), lambda j, i: (i, 0)), pl.BlockSpec((k, tn), lambda j, i: (0, j))]
    args = [a, b]
    if res is not None:
        in_specs.append(osp)
        args.append(res)
    return pl.pallas_call(
        body, name=name, grid=(nn // tn, m // tm), in_specs=in_specs, out_specs=osp,
        out_shape=_sds((m, nn), out_dtype), compiler_params=_cp("parallel", "parallel"),
    )(*args)


def _mm_nt(pairs, name, out_dtype=F32, tm=512, tk=512, deps=()):
    m = pairs[0][0].shape[0]
    kk = pairs[0][1].shape[0]
    p = len(pairs)

    def body(*refs):
        o_ref = refs[2 * p]
        t = _dot_nt(refs[0][...], refs[1][...])
        for q in range(1, p):
            t = t + _dot_nt(refs[2 * q][...], refs[2 * q + 1][...])
        o_ref[...] = t.astype(out_dtype)

    in_specs, args = [], []
    for a, b in pairs:
        in_specs += [pl.BlockSpec((tm, a.shape[1]), lambda j, i: (i, 0)), pl.BlockSpec((tk, b.shape[1]), lambda j, i: (j, 0))]
        args += [a, b]
    return pl.pallas_call(
        _ignore_deps(body, 2 * p, len(deps)), name=name, grid=(kk // tk, m // tm), in_specs=in_specs + [TOKEN_SPEC] * len(deps),
        out_specs=pl.BlockSpec((tm, tk), lambda j, i: (i, j)), out_shape=_sds((m, kk), out_dtype),
        compiler_params=_cp("parallel", "parallel"),
    )(*args, *deps)


def _rope_tables(s, sign):
    half = ROPE_DIMS // 2
    freqs = ROPE_THETA ** (-jnp.arange(half, dtype=F32) / half)
    ang = jnp.arange(s, dtype=F32)[:, None] * freqs[None, :]
    cos, sin = jnp.cos(ang), sign * jnp.sin(ang)
    one = jnp.ones((s, HD - ROPE_DIMS), F32)
    zero = jnp.zeros((s, HD - ROPE_DIMS), F32)
    zh = jnp.zeros((s, half), F32)
    c = jnp.concatenate([cos, cos, one], axis=1)
    sa = jnp.concatenate([-sin, zh, zero], axis=1)
    sb = jnp.concatenate([zh, sin, zero], axis=1)
    return c, sa, sb


def _rope(xv, c, sa, sb):
    return xv * c + pltpu.roll(xv, HD - ROPE_DIMS // 2, 1) * sa + pltpu.roll(xv, ROPE_DIMS // 2, 1) * sb


def _qkv_rope(h, w, tables, name):
    s, d = h.shape
    n = w.shape[2]
    per = n // HD
    tm = 512

    def body(h_ref, w_ref, c_ref, sa_ref, sb_ref, o_ref):
        shard = pl.program_id(0)
        y = _dot(h_ref[...], w_ref[...])
        c, sa, sb = c_ref[...], sa_ref[...], sb_ref[...]
        for j in range(per):
            blk = y[:, j * HD:(j + 1) * HD]
            rot = _rope(blk, c, sa, sb)
            is_qk = shard * per + j < 2 * N_HEADS
            o_ref[:, j * HD:(j + 1) * HD] = jnp.where(is_qk, rot, blk).astype(BF)

    tab = pl.BlockSpec((tm, HD), lambda s_, i: (i, 0))
    return pl.pallas_call(
        body, name=name, grid=(NDEV, s // tm),
        in_specs=[pl.BlockSpec((tm, d), lambda s_, i: (i, 0)), pl.BlockSpec((None, d, n), lambda s_, i: (s_, 0, 0)), tab, tab, tab],
        out_specs=pl.BlockSpec((tm, n), lambda s_, i: (i, s_)), out_shape=_sds((s, NDEV * n), BF),
        compiler_params=_cp("parallel", "parallel"),
    )(h, w, *tables)


def _iota2():
    return (lax.broadcasted_iota(jnp.int32, (QB, QB), 0), lax.broadcasted_iota(jnp.int32, (QB, QB), 1))


def _softplus(z):
    return jnp.maximum(z, 0.0) + jnp.log(1.0 + jnp.exp(-jnp.abs(z)))


def _tri_dot(xv, tri, left=False):
    hi = xv.astype(BF)
    r1 = xv - hi.astype(F32)
    mid = r1.astype(BF)
    lo = (r1 - mid.astype(F32)).astype(BF)
    if left:
        return _dot(tri, hi) + _dot(tri, mid) + _dot(tri, lo)
    return _dot(hi, tri) + _dot(mid, tri) + _dot(lo, tri)


def _col(ref_or_val):
    return ref_or_val[:, 0:1]


KT = 4 * QB


def _iota_tile():
    return (lax.broadcasted_iota(jnp.int32, (QB, KT), 0), lax.broadcasted_iota(jnp.int32, (QB, KT), 1))


def _scan_matrix(keep):
    tri = keep(*_iota2()).astype(BF)
    return jnp.concatenate([tri, tri], axis=0)


def _scan_dot(xv, tri2):
    hi = xv.astype(BF)
    lo = (xv - hi.astype(F32)).astype(BF)
    return _dot(jnp.concatenate([hi, lo], axis=1), tri2)


def _blocks(xv):
    return [xv[:, b * QB:(b + 1) * QB] for b in range(KT // QB)]


def _sb_fwd(qkv, name):
    s = qkv.shape[0]
    nb = s // QB

    def body(q_ref, k_ref, v_ref, o_ref, ot_ref, t_ref):
        i = pl.program_id(1)
        q = q_ref[...]
        row, col = _iota_tile()
        later_keys = _scan_matrix(lambda j, s_: j > s_)
        last = i // (KT // QB)

        def step(tt, carry):
            acc, later = carry
            t = last - tt
            off = pl.multiple_of(t * KT, KT)
            k = k_ref[pl.ds(off, KT), :]
            v = v_ref[pl.ds(off, KT), :]
            z = _dot_nt(q, k) * SCALE
            strict = row + (i * QB - t * KT) > col
            sp = _softplus(z)
            lnb = jnp.where(strict, -sp, 0.0)
            afters = []
            for xb in reversed(_blocks(lnb)):
                afters.append(later + _scan_dot(xb, later_keys))
                later = later + jnp.sum(xb, axis=1, keepdims=True)
            after = jnp.concatenate(afters[::-1], axis=1)
            w = jnp.where(strict, jnp.exp((z - sp) + after), 0.0)
            return acc + _dot(w.astype(BF), v), later

        acc, total = lax.fori_loop(0, last + 1, step, (jnp.zeros((QB, HD), F32), jnp.zeros((QB, 1), F32)))
        o_ref[...] = acc.astype(BF)
        ot_ref[...] = acc.T.astype(BF)
        t_ref[...] = jnp.broadcast_to(total, (QB, HD))

    blk = pl.BlockSpec((QB, HD), lambda h, i: (i, h))
    return pl.pallas_call(
        body, name=name, grid=(N_SB, nb),
        in_specs=[blk, pl.BlockSpec((s, HD), lambda h, i: (0, N_HEADS + h)), pl.BlockSpec((s, HD), lambda h, i: (0, 2 * N_HEADS + h))],
        out_specs=[blk, pl.BlockSpec((HD, QB), lambda h, i: (h, i)), blk],
        out_shape=[_sds((s, N_SB * HD), BF), _sds((N_SB * HD, s), BF), _sds((s, N_SB * HD), F32)],
        compiler_params=_cp("parallel", "parallel"),
    )(qkv, qkv, qkv)


def _sb_bwd(qkv, do, total, name):
    s = qkv.shape[0]
    nb = s // QB

    def body(q_ref, k_ref, v_ref, do_ref, t_ref, dq_ref, dk_ref, dv_ref, dk_acc, dv_acc):
        i = pl.program_id(1)

        @pl.when(i == 0)
        def _():
            dk_acc[...] = jnp.zeros_like(dk_acc)
            dv_acc[...] = jnp.zeros_like(dv_acc)

        q = q_ref[...]
        dov = do_ref[...]
        tot = _col(t_ref[...])
        row, col = _iota_tile()
        keys_upto = _scan_matrix(lambda j, s_: j <= s_)
        keys_before = _scan_matrix(lambda j, s_: j < s_)

        def step(t, carry):
            dq, lnb_before, dl_before = carry
            off = pl.multiple_of(t * KT, KT)
            k = k_ref[pl.ds(off, KT), :]
            v = v_ref[pl.ds(off, KT), :]
            z = _dot_nt(q, k) * SCALE
            strict = row + (i * QB - t * KT) > col
            sp = _softplus(z)
            lnb = jnp.where(strict, -sp, 0.0)
            afters = []
            for xb in _blocks(lnb):
                afters.append(tot - (lnb_before + _scan_dot(xb, keys_upto)))
                lnb_before = lnb_before + jnp.sum(xb, axis=1, keepdims=True)
            a = jnp.where(strict, jnp.exp((z - sp) + jnp.concatenate(afters, axis=1)), 0.0)
            dl = a * _dot_nt(dov, v)
            befores = []
            for xb in _blocks(dl):
                befores.append(dl_before + _scan_dot(xb, keys_before))
                dl_before = dl_before + jnp.sum(xb, axis=1, keepdims=True)
            sig = jnp.exp(z - sp)
            dz = jnp.where(strict, dl * (1.0 - sig) - sig * jnp.concatenate(befores, axis=1), 0.0) * SCALE
            dq = dq + _dot(dz.astype(BF), k)
            dk_acc[pl.ds(off, KT), :] += _dot(dz.T.astype(BF), q)
            dv_acc[pl.ds(off, KT), :] += _dot(a.T.astype(BF), dov)
            return dq, lnb_before, dl_before

        zero = jnp.zeros((QB, 1), F32)
        dq, _, _ = lax.fori_loop(0, i // (KT // QB) + 1, step, (jnp.zeros((QB, HD), F32), zero, zero))
        dq_ref[...] = dq.astype(BF)

        @pl.when(i == nb - 1)
        def _():
            dk_ref[...] = dk_acc[...].astype(BF)
            dv_ref[...] = dv_acc[...].astype(BF)

    blk = pl.BlockSpec((QB, HD), lambda h, i: (i, h))
    full = pl.BlockSpec((s, HD), lambda h, i: (0, h))
    return pl.pallas_call(
        body, name=name, grid=(N_SB, nb),
        in_specs=[blk, pl.BlockSpec((s, HD), lambda h, i: (0, N_HEADS + h)), pl.BlockSpec((s, HD), lambda h, i: (0, 2 * N_HEADS + h)), blk, blk],
        out_specs=[blk, full, full], out_shape=[_sds((s, N_SB * HD), BF)] * 3,
        scratch_shapes=[pltpu.VMEM((s, HD), F32), pltpu.VMEM((s, HD), F32)],
        compiler_params=_cp("parallel", "arbitrary"),
    )(qkv, qkv, qkv, do, total)


def _fgate_fwd(f, b, name):
    s = f.shape[0]
    nb = s // QB
    nfox = N_HEADS - N_SB

    def body(f_ref, b_ref, cb_ref, ct_ref):
        row, col = _iota2()
        upto = (row >= col).astype(BF)
        carry = jnp.zeros((1, HD), F32)
        for blk in range(nb):
            xv = f_ref[blk * QB:(blk + 1) * QB, :] + b_ref[...]
            logf = -_softplus(-xv)
            cum = _tri_dot(logf, upto, left=True) + carry
            carry = cum[QB - 1:QB, :]
            ct_ref[blk] = cum.T
            for h in range(nfox):
                cb_ref[blk * QB:(blk + 1) * QB, h * HD:(h + 1) * HD] = jnp.broadcast_to(cum[:, h:h + 1], (QB, HD))

    return pl.pallas_call(
        body, name=name, out_shape=[_sds((s, nfox * HD), F32), _sds((nb, HD, HD), F32)], compiler_params=_cp(),
    )(f, b)


def _fgate_bwd(dcq, dck, f, b, name):
    s = f.shape[0]
    nb = s // QB
    nfox = N_HEADS - N_SB

    def body(dcq_ref, dck_ref, f_ref, b_ref, df_ref, db_ref):
        row, col = _iota2()
        from_tri = (row <= col).astype(BF)
        lane = col
        carry = jnp.zeros((1, HD), F32)
        db = jnp.zeros((1, HD), F32)
        for blk in reversed(range(nb)):
            dcum = jnp.zeros((QB, HD), F32)
            for h in range(nfox):
                here = (slice(blk * QB, (blk + 1) * QB), slice(h * HD, (h + 1) * HD))
                dcum = jnp.where(lane == h, dcq_ref[here] - dck_ref[here], dcum)
            dlogf = _tri_dot(dcum, from_tri, left=True) + carry
            carry = dlogf[0:1, :]
            xv = f_ref[blk * QB:(blk + 1) * QB, :] + b_ref[...]
            sp = _softplus(xv)
            df = jnp.where(lane < nfox, dlogf * jnp.exp(-sp), 0.0)
            df_ref[blk * QB:(blk + 1) * QB, :] = df.astype(BF)
            db = db + jnp.sum(df, axis=0, keepdims=True)
        db_ref[...] = db

    return pl.pallas_call(
        body, name=name, out_shape=[_sds((s, HD), BF), _sds((1, HD), F32)], compiler_params=_cp(),
    )(dcq, dck, f, b)


def _fox_head_row(ct_ref, j, h):
    tile = ct_ref[j]
    sub = lax.broadcasted_iota(jnp.int32, tile.shape, 0)
    return jnp.sum(jnp.where(sub == h, tile, 0.0), axis=0, keepdims=True)


def _fox_tile_row(ct_ref, t, h):
    nsub = KT // QB
    return jnp.concatenate([_fox_head_row(ct_ref, t * nsub + b, h) for b in range(nsub)], axis=1)


def _fox_fwd(qkv, cum_b, cum_t, name):
    s = qkv.shape[0]
    nb = s // QB
    nfox = N_HEADS - N_SB

    def body(q_ref, k_ref, v_ref, cq_ref, ct_ref, o_ref, ot_ref, lse_ref):
        h, i = pl.program_id(0), pl.program_id(1)
        q = q_ref[...]
        cq = _col(cq_ref[...])
        row, col = _iota_tile()

        def step(t, carry):
            acc, m, l = carry
            off = pl.multiple_of(t * KT, KT)
            k = k_ref[pl.ds(off, KT), :]
            v = v_ref[pl.ds(off, KT), :]
            z = _dot_nt(q, k) * SCALE + cq - _fox_tile_row(ct_ref, t, h)
            z = jnp.where(row + (i * QB - t * KT) >= col, z, NEG_INF)
            m_new = jnp.maximum(m, jnp.max(z, axis=1, keepdims=True))
            alpha = jnp.exp(m - m_new)
            p = jnp.exp(z - m_new)
            l = alpha * l + jnp.sum(p, axis=1, keepdims=True)
            acc = alpha * acc + _dot(p.astype(BF), v)
            return acc, m_new, l

        acc, m, l = lax.fori_loop(0, i // (KT // QB) + 1, step,
                                  (jnp.zeros((QB, HD), F32), jnp.full((QB, 1), NEG_INF, F32), jnp.zeros((QB, 1), F32)))
        o = acc / l
        o_ref[...] = o.astype(BF)
        ot_ref[...] = o.T.astype(BF)
        lse_ref[...] = jnp.broadcast_to(m + jnp.log(l), (QB, HD))

    blk = pl.BlockSpec((QB, HD), lambda h, i: (i, h))
    return pl.pallas_call(
        body, name=name, grid=(nfox, nb),
        in_specs=[pl.BlockSpec((QB, HD), lambda h, i: (i, N_SB + h)),
                  pl.BlockSpec((s, HD), lambda h, i: (0, N_HEADS + N_SB + h)),
                  pl.BlockSpec((s, HD), lambda h, i: (0, 2 * N_HEADS + N_SB + h)),
                  blk, pl.BlockSpec((nb, 8, HD), lambda h, i: (0, 0, 0))],
        out_specs=[blk, pl.BlockSpec((HD, QB), lambda h, i: (h, i)), blk],
        out_shape=[_sds((s, nfox * HD), BF), _sds((nfox * HD, s), BF), _sds((s, nfox * HD), F32)],
        compiler_params=_cp("parallel", "parallel"),
    )(qkv, qkv, qkv, cum_b, cum_t)


def _fox_bwd(qkv, cum_b, cum_t, o, lse, do, name):
    s = qkv.shape[0]
    nb = s // QB
    nfox = N_HEADS - N_SB

    def body(q_ref, k_ref, v_ref, cq_ref, ct_ref, o_ref, lse_ref, do_ref, dq_ref, dk_ref, dv_ref, dcq_ref, dc_ref, dk_acc, dv_acc, dc_acc):
        h, i = pl.program_id(0), pl.program_id(1)

        @pl.when(i == 0)
        def _():
            dk_acc[...] = jnp.zeros_like(dk_acc)
            dv_acc[...] = jnp.zeros_like(dv_acc)
            dc_acc[...] = jnp.zeros_like(dc_acc)

        q = q_ref[...]
        cq = _col(cq_ref[...])
        dov = do_ref[...]
        lse_c = _col(lse_ref[...])
        delta = jnp.sum(dov.astype(F32) * o_ref[...].astype(F32), axis=1, keepdims=True)
        row, col = _iota_tile()
        ones = jnp.ones((QB, HD), BF)

        def step(t, carry):
            dq, over_keys = carry
            off = pl.multiple_of(t * KT, KT)
            k = k_ref[pl.ds(off, KT), :]
            v = v_ref[pl.ds(off, KT), :]
            z = _dot_nt(q, k) * SCALE + cq - _fox_tile_row(ct_ref, t, h)
            p = jnp.where(row + (i * QB - t * KT) >= col, jnp.exp(z - lse_c), 0.0)
            dz = p * (_dot_nt(dov, v) - delta)
            dzt = dz.T
            dq = dq + _dot((dz * SCALE).astype(BF), k)
            dk_acc[pl.ds(off, KT), :] += _dot((dzt * SCALE).astype(BF), q)
            dv_acc[pl.ds(off, KT), :] += _dot(p.T.astype(BF), dov)
            dc_acc[pl.ds(off, KT), :] += _tri_dot(dzt, ones)
            return dq, over_keys + jnp.sum(dz, axis=1, keepdims=True)

        dq, over_keys = lax.fori_loop(0, i // (KT // QB) + 1, step, (jnp.zeros((QB, HD), F32), jnp.zeros((QB, 1), F32)))
        dq_ref[...] = dq.astype(BF)
        dcq_ref[...] = jnp.broadcast_to(over_keys, (QB, HD))

        @pl.when(i == nb - 1)
        def _():
            dk_ref[...] = dk_acc[...].astype(BF)
            dv_ref[...] = dv_acc[...].astype(BF)
            dc_ref[...] = dc_acc[...]

    blk = pl.BlockSpec((QB, HD), lambda h, i: (i, h))
    full = pl.BlockSpec((s, HD), lambda h, i: (0, h))
    return pl.pallas_call(
        body, name=name, grid=(nfox, nb),
        in_specs=[pl.BlockSpec((QB, HD), lambda h, i: (i, N_SB + h)),
                  pl.BlockSpec((s, HD), lambda h, i: (0, N_HEADS + N_SB + h)),
                  pl.BlockSpec((s, HD), lambda h, i: (0, 2 * N_HEADS + N_SB + h)),
                  blk, pl.BlockSpec((nb, 8, HD), lambda h, i: (0, 0, 0)), blk, blk,
                  pl.BlockSpec((QB, HD), lambda h, i: (i, N_SB + h))],
        out_specs=[blk, full, full, blk, full],
        out_shape=[_sds((s, nfox * HD), BF)] * 3 + [_sds((s, nfox * HD), F32)] * 2,
        scratch_shapes=[pltpu.VMEM((s, HD), F32)] * 3,
        compiler_params=_cp("parallel", "arbitrary"),
    )(qkv, qkv, qkv, cum_b, cum_t, o, lse, do)


GB = 4
DIL_PAD = QB * 16


def _dil_group(g, d, off=0, shift=0):
    if d == 1:
        return [pl.ds(pl.multiple_of(off + (g * GB + shift) * QB, QB), GB * QB)]
    if d == 4:
        return [pl.ds(off + g + shift * QB * d, GB * QB, stride=d)]
    assert d == 16 and shift == 0
    return [pl.ds(off + g * GB + b, QB, stride=d) for b in range(GB)]


def _dil_load(ref, g, d, off=0, shift=0):
    parts = [ref[sl, :] for sl in _dil_group(g, d, off, shift)]
    rows = parts[0] if len(parts) == 1 else jnp.concatenate(parts, axis=0)
    return rows.reshape(GB, QB, HD)


def _dil_store(ref, g, d, val, off=0, shift=0, add=False):
    rows = val.reshape(GB * QB, HD)
    slices = _dil_group(g, d, off, shift)
    for b, sl in enumerate(slices):
        piece = rows if len(slices) == 1 else rows[b * QB:(b + 1) * QB]
        if add:
            ref[sl, :] += piece
        else:
            ref[sl, :] = piece


def _bdot_nt(a, b):
    return lax.dot_general(a, b, (((2,), (2,)), ((0,), (0,))), preferred_element_type=F32)


def _bdot(a, b):
    return lax.dot_general(a, b, (((2,), (1,)), ((0,), (0,))), preferred_element_type=F32)


def _bdot_tn(a, b):
    return lax.dot_general(jnp.swapaxes(a, 1, 2).astype(BF), b, (((2,), (1,)), ((0,), (0,))), preferred_element_type=F32)


def _dil_masks(g, d, nb):
    row = lax.broadcasted_iota(jnp.int32, (GB, QB, QB), 1)
    col = lax.broadcasted_iota(jnp.int32, (GB, QB, QB), 2)
    blk = lax.broadcasted_iota(jnp.int32, (GB, QB, QB), 0) + (g * GB if d == 1 else 0)
    return col <= row, jnp.logical_and(col >= row, blk >= 1) if nb > 1 else None


def _dilated_fwd(qkv, name):
    s = qkv.shape[0]
    npat = len(DILATED_PATTERNS)
    chunk = 256

    def body(q_ref, k_ref, v_ref, out_ref, outt_ref, g_ref, qf, kf, vf, *per_pattern):
        o_s, l_s = per_pattern[:npat], per_pattern[npat:]
        qf[...] = q_ref[...].astype(F32)
        for dst, src in ((kf, k_ref), (vf, v_ref)):
            dst[0:DIL_PAD, :] = jnp.zeros((DIL_PAD, HD), F32)
            dst[DIL_PAD:, :] = src[...].astype(F32)
        for p, (_, d) in enumerate(DILATED_PATTERNS):
            nb = s // d // QB

            def grp(g, carry, p=p, d=d, nb=nb):
                mc, mp = _dil_masks(g, d, nb)
                q = _dil_load(qf, g, d).astype(BF)
                zc = jnp.where(mc, _bdot_nt(q, _dil_load(kf, g, d, DIL_PAD).astype(BF)) * SCALE, NEG_INF)
                m = jnp.max(zc, axis=2, keepdims=True)
                if nb > 1:
                    zp = jnp.where(mp, _bdot_nt(q, _dil_load(kf, g, d, DIL_PAD, -1).astype(BF)) * SCALE, NEG_INF)
                    m = jnp.maximum(m, jnp.max(zp, axis=2, keepdims=True))
                ec = jnp.exp(zc - m)
                l = jnp.sum(ec, axis=2, keepdims=True)
                if nb > 1:
                    ep = jnp.where(mp, jnp.exp(zp - m), 0.0)
                    l = l + jnp.sum(ep, axis=2, keepdims=True)
                o = _bdot((ec / l).astype(BF), _dil_load(vf, g, d, DIL_PAD).astype(BF))
                if nb > 1:
                    o = o + _bdot((ep / l).astype(BF), _dil_load(vf, g, d, DIL_PAD, -1).astype(BF))
                _dil_store(o_s[p], g, d, o)
                _dil_store(l_s[p], g, d, jnp.broadcast_to(m + jnp.log(l), (GB, QB, HD)))
                return carry

            lax.fori_loop(0, s // (QB * GB), grp, 0)
        for c0 in range(0, s, chunk):
            rows = slice(c0, c0 + chunk)
            ls = [l_s[p][rows, :] for p in range(npat)]
            m = functools.reduce(jnp.maximum, ls)
            es = [jnp.exp(l - m) for l in ls]
            tot = functools.reduce(lambda a, b: a + b, es)
            out = functools.reduce(lambda a, b: a + b, [(e / tot) * o_s[p][rows, :] for p, e in enumerate(es)])
            out_ref[rows, :] = out.astype(BF)
            outt_ref[:, rows] = out.T.astype(BF)
            g_ref[rows, :] = m + jnp.log(tot)

    full = pl.BlockSpec((s, HD), lambda h: (0, h))
    return pl.pallas_call(
        body, name=name, grid=(N_HEADS,),
        in_specs=[full, pl.BlockSpec((s, HD), lambda h: (0, N_HEADS + h)), pl.BlockSpec((s, HD), lambda h: (0, 2 * N_HEADS + h))],
        out_specs=[full, pl.BlockSpec((HD, s), lambda h: (h, 0)), full],
        out_shape=[_sds((s, N_HEADS * HD), BF), _sds((N_HEADS * HD, s), BF), _sds((s, N_HEADS * HD), F32)],
        scratch_shapes=[pltpu.VMEM((s, HD), F32)] + [pltpu.VMEM((s + DIL_PAD, HD), F32)] * 2 + [pltpu.VMEM((s, HD), F32)] * (2 * npat),
        compiler_params=_cp("parallel"),
    )(qkv, qkv, qkv)


def _dilated_bwd(qkv, out, glse, do, tables, name):
    s = qkv.shape[0]
    chunk = 256

    def body(q_ref, k_ref, v_ref, out_ref, g_ref, do_ref, c_ref, sa_ref, sb_ref, dq_ref, dk_ref, dv_ref,
             qf, kf, vf, dof, dl_s, dq_a, dk_a, dv_a):
        qf[...] = q_ref[...].astype(F32)
        for dst, src in ((kf, k_ref), (vf, v_ref)):
            dst[0:DIL_PAD, :] = jnp.zeros((DIL_PAD, HD), F32)
            dst[DIL_PAD:, :] = src[...].astype(F32)
        for c0 in range(0, s, chunk):
            rows = slice(c0, c0 + chunk)
            dov = do_ref[rows, :].astype(F32)
            dof[rows, :] = dov
            dl_s[rows, :] = jnp.broadcast_to(jnp.sum(dov * out_ref[rows, :].astype(F32), axis=1, keepdims=True), (chunk, HD))
        dq_a[...] = jnp.zeros_like(dq_a)
        dk_a[...] = jnp.zeros_like(dk_a)
        dv_a[...] = jnp.zeros_like(dv_a)
        for _, d in DILATED_PATTERNS:
            nb = s // d // QB

            def grp(g, carry, d=d, nb=nb):
                mc, mp = _dil_masks(g, d, nb)
                q = _dil_load(qf, g, d).astype(BF)
                kc = _dil_load(kf, g, d, DIL_PAD).astype(BF)
                dov = _dil_load(dof, g, d).astype(BF)
                lse = _dil_load(g_ref, g, d)[:, :, 0:1]
                delta = _dil_load(dl_s, g, d)[:, :, 0:1]
                pc = jnp.where(mc, jnp.exp(_bdot_nt(q, kc) * SCALE - lse), 0.0)
                dzc = pc * (_bdot_nt(dov, _dil_load(vf, g, d, DIL_PAD).astype(BF)) - delta) * SCALE
                dq = _bdot(dzc.astype(BF), kc)
                if nb > 1:
                    kp = _dil_load(kf, g, d, DIL_PAD, -1).astype(BF)
                    pp = jnp.where(mp, jnp.exp(_bdot_nt(q, kp) * SCALE - lse), 0.0)
                    dzp = pp * (_bdot_nt(dov, _dil_load(vf, g, d, DIL_PAD, -1).astype(BF)) - delta) * SCALE
                    dq = dq + _bdot(dzp.astype(BF), kp)
                _dil_store(dq_a, g, d, dq, add=True)
                _dil_store(dk_a, g, d, _bdot_tn(dzc, q), DIL_PAD, add=True)
                _dil_store(dv_a, g, d, _bdot_tn(pc, dov), DIL_PAD, add=True)
                if nb > 1:
                    _dil_store(dk_a, g, d, _bdot_tn(dzp, q), DIL_PAD, -1, add=True)
                    _dil_store(dv_a, g, d, _bdot_tn(pp, dov), DIL_PAD, -1, add=True)
                return carry

            lax.fori_loop(0, s // (QB * GB), grp, 0)
        for c0 in range(0, s, chunk):
            rows = slice(c0, c0 + chunk)
            padded = slice(DIL_PAD + c0, DIL_PAD + c0 + chunk)
            c, sa, sb = c_ref[rows, :], sa_ref[rows, :], sb_ref[rows, :]
            dq_ref[rows, :] = _rope(dq_a[rows, :], c, sa, sb).astype(BF)
            dk_ref[rows, :] = _rope(dk_a[padded, :], c, sa, sb).astype(BF)
            dv_ref[rows, :] = dv_a[padded, :].astype(BF)

    full = pl.BlockSpec((s, HD), lambda h: (0, h))
    tab = pl.BlockSpec((s, HD), lambda h: (0, 0))
    return pl.pallas_call(
        body, name=name, grid=(N_HEADS,),
        in_specs=[full, pl.BlockSpec((s, HD), lambda h: (0, N_HEADS + h)), pl.BlockSpec((s, HD), lambda h: (0, 2 * N_HEADS + h)),
                  full, full, full, tab, tab, tab],
        out_specs=[full, full, full], out_shape=[_sds((s, N_HEADS * HD), BF)] * 3,
        scratch_shapes=[pltpu.VMEM((s, HD), F32)] + [pltpu.VMEM((s + DIL_PAD, HD), F32)] * 2 + [pltpu.VMEM((s, HD), F32)] * 3
        + [pltpu.VMEM((s + DIL_PAD, HD), F32)] * 2,
        compiler_params=_cp("parallel"),
    )(qkv, qkv, qkv, out, glse, do, *tables)


def _swiglu_fwd(x, gnorm, w, tag, deps=()):
    h, ht = _rmsnorm_fwd(x, gnorm, f"norm_{tag}", deps)
    g, u, act, act_t = _ffn_up(h, w["gate"], w["up"], f"ffn_up_{tag}")
    y = _ffn_down(act, w["down"], x, f"ffn_down_{tag}")
    return y, (x, ht, g, u, act_t)


def _swiglu_bwd(saved, gnorm, w, dy, dyb_half, out_scale, tag, deps=(), on_down=None, on_grads=None):
    x, ht, g, u, act_t = saved
    dg, du = _ffn_bwd_act(dyb_half, w["down"], g, u, f"ffn_bwd_act_{tag}", deps)
    d_down = _grad_rows(act_t, dyb_half, f"ffn_bwd_wd_{tag}")
    tokens = list(on_down(d_down)) if on_down else []
    d_gate, d_up = _grad_cols(ht, [dg, du], [False, False], f"ffn_bwd_wgu_{tag}", tokens)
    gw = {"gate": d_gate, "up": d_up, "down": d_down}
    tokens = list(on_grads(gw)) if on_grads else []
    dh = _dh_cols([dg, du], [w["gate"], w["up"]], [False, False], f"ffn_bwd_dh_{tag}", tokens)
    dx, dxb, dxbt, dgn = _rmsnorm_bwd(x, gnorm, dh, dy, out_scale, f"norm_bwd_{tag}")
    return (dx, dxb, dxbt), dgn, gw


def kernel(x, norm_g, ffn1_w_gate, ffn1_w_up, ffn1_w_down, ffn2_w_gate, ffn2_w_up, ffn2_w_down, even_w_in, even_b_forget, even_w_out, odd_w_qkv, odd_w_out, final_norm_g, loss_target, m_norm_g, m_ffn1_w_gate, m_ffn1_w_up, m_ffn1_w_down, m_ffn2_w_gate, m_ffn2_w_up, m_ffn2_w_down, m_even_w_in, m_even_b_forget, m_even_w_out, m_odd_w_qkv, m_odd_w_out, m_final_norm_g, v_norm_g, v_ffn1_w_gate, v_ffn1_w_up, v_ffn1_w_down, v_ffn2_w_gate, v_ffn2_w_up, v_ffn2_w_down, v_even_w_in, v_even_b_forget, v_even_w_out, v_odd_w_qkv, v_odd_w_out, v_final_norm_g):
    s, d = x.shape[1], x.shape[2]
    nfox = N_HEADS - N_SB
    ax, ay, ac = lax.axis_index("x"), lax.axis_index("y"), lax.axis_index("c")
    me = 4 * ax + 2 * ay + ac
    slots = jnp.stack([4 * px + 2 * py + ac for px, py in [(ax, ay), (1 - ax, ay), (ax, 1 - ay), (1 - ax, 1 - ay)]]).astype(jnp.int32)
    x0 = x.reshape(s, d)
    target = loss_target.reshape(s, d)

    def bf(w):
        return w.astype(BF)

    groups = [
        [bf(ffn1_w_gate[0]), bf(ffn1_w_up[0]), bf(ffn1_w_down[0]), norm_g.reshape(6, d // NDEV)],
        [bf(even_w_in[0]), bf(even_w_out[0])],
        [bf(ffn2_w_gate[0]), bf(ffn2_w_up[0]), bf(ffn2_w_down[0])],
        [bf(ffn1_w_gate[1]), bf(ffn1_w_up[1]), bf(ffn1_w_down[1])],
        [bf(odd_w_qkv[0]), bf(odd_w_out[0])],
        [bf(ffn2_w_gate[1]), bf(ffn2_w_up[1]), bf(ffn2_w_down[1])],
    ]
    started = []
    for k, grp in enumerate(groups):
        started.append(_gather_start(grp, me, [started[-1]["token"]] if started else [], f"gather_start_{k}"))
    all_started = [started[-1]["token"]]

    def gathered(k, after, early=None):
        return _gather_finish(_gather_forward(started[k], after if early is None else early, f"gather_forward_{k}"), after, f"gather_finish_{k}")

    def ffn_weights(ws_):
        return {"gate": ws_[0], "up": ws_[1], "down": ws_[2]}

    b_pad = jnp.pad(even_b_forget, ((0, 0), (0, HD - nfox)))
    gfin = final_norm_g.reshape(1, d)

    g0 = gathered(0, x0)
    gn = jnp.transpose(g0[3], (1, 0, 2)).reshape(6, 1, d)
    wf = [[ffn_weights(g0), None], [None, None]]
    x1, sv_f1_0 = _swiglu_fwd(x0, gn[0], wf[0][0], "l0a", all_started)
    g1 = gathered(1, x1)
    w_in_nat = jnp.transpose(g1[0], (1, 0, 2)).reshape(d, -1)
    w_qkv_e = w_in_nat[:, :3 * d]
    w_f = jnp.pad(w_in_nat[:, 3 * d:], ((0, 0), (0, HD - nfox)))
    w_out_e = g1[1].reshape(d, d)
    h_e, ht_e = _rmsnorm_fwd(x1, gn[1], "norm_l0m")
    qkv_e = _mm_nn(h_e, w_qkv_e, 768, BF, "even_qkv")
    f_e = _mm_nn(h_e, w_f, HD, F32, "even_fgate")
    o_sb, ot_sb, tot_sb = _sb_fwd(qkv_e, "sb_fwd")
    cum_b, cum_t = _fgate_fwd(f_e, b_pad, "fgate_fwd")
    o_fox, ot_fox, lse_fox = _fox_fwd(qkv_e, cum_b, cum_t, "fox_fwd")
    o_e = jnp.concatenate([o_sb, o_fox], axis=1)
    ot_e = jnp.concatenate([ot_sb, ot_fox], axis=0)
    x2 = _mm_nn(o_e, w_out_e, 1024, F32, "even_out", res=x1)
    wf[0][1] = ffn_weights(gathered(2, x2, early=o_fox))
    x3, sv_f2_0 = _swiglu_fwd(x2, gn[2], wf[0][1], "l0b")

    wf[1][0] = ffn_weights(gathered(3, x3))
    x4, sv_f1_1 = _swiglu_fwd(x3, gn[3], wf[1][0], "l1a")
    g4 = gathered(4, x4, early=sv_f1_1[2])
    w_qkv_o = g4[0]
    w_out_o = g4[1].reshape(d, d)
    h_o, ht_o = _rmsnorm_fwd(x4, gn[4], "norm_l1m")
    qkv_o = _qkv_rope(h_o, w_qkv_o, _rope_tables(s, 1.0), "odd_qkv")
    o_o, ot_o, glse = _dilated_fwd(qkv_o, "dilated_fwd")
    x5 = _mm_nn(o_o, w_out_o, 1024, F32, "odd_out", res=x4)
    wf[1][1] = ffn_weights(gathered(5, x5, early=o_o))
    x6, sv_f2_1 = _swiglu_fwd(x5, gn[5], wf[1][1], "l1b")

    def chip_sums(gs, a_s, tag):
        ps = [_pair_sum(g_, a_, slots, f"pair_sum_{tag}_{k}") for k, (g_, a_) in enumerate(zip(gs, a_s))]
        return gs, a_s, _chip_start(ps, f"chip_start_{tag}")

    def as_slices(gs):
        return [g_ if g_.ndim == 3 else g_.reshape(NDEV, g_.shape[0] // NDEV, g_.shape[1]) for g_ in gs]

    def reduce_start(gs, tag):
        gs = as_slices(gs)
        return chip_sums(gs, _pair_exchange(gs, f"pair_exchange_{tag}"), tag)

    red, crossing = {}, {}

    def cross(gs, tag):
        crossing[tag] = _pair_start(as_slices(gs), f"pair_start_{tag}")
        return [crossing[tag]["token"]]

    def reduce_behind_dh(tag):
        return lambda gw: cross([gw["gate"], gw["up"], gw["down"]], tag)

    def reduce_after(tag, after):
        red[tag] = chip_sums(*_pair_finish(crossing[tag], after, f"pair_finish_{tag}"), tag)
        return [red[tag][2]["token"]]

    def reduce_now(tag, names):
        def hook(gw):
            red[tag] = reduce_start([gw[nm] for nm in names] if names else [gw], tag)
            return [red[tag][2]["token"]]
        return hook

    dx6, dx6b, _, d_gfin, loss_part = _loss_head(x6, gfin, target, "loss_head")

    (dx5, dx5b, dx5bt), dgn5, _ = _swiglu_bwd(sv_f2_1, gn[5], wf[1][1], dx6, dx6b, 1.0, "l1b", on_grads=reduce_behind_dh("l1b"))
    d_wout_o = _mm_nn(ot_o, dx5b, 1024, BF, "odd_out_dw")
    do_o = _mm_nt([(dx5b, w_out_o)], "odd_out_do", BF, deps=reduce_after("l1b", dx5))
    dqkv_o = jnp.concatenate(_dilated_bwd(qkv_o, o_o, glse, do_o, _rope_tables(s, -1.0), "dilated_bwd"), axis=1)
    (d_wqkv_o,) = _grad_cols(ht_o, [dqkv_o], [True], "odd_qkv_dw")
    dh_o = _dh_cols([dqkv_o], [w_qkv_o], [True], "odd_qkv_dh", cross([d_wqkv_o, d_wout_o], "l1m"))
    dx4, dx4b, _, dgn4 = _rmsnorm_bwd(x4, gn[4], dh_o, dx5, 0.5, "norm_bwd_l1m")
    (dx3, dx3b, _), dgn3, _ = _swiglu_bwd(sv_f1_1, gn[3], wf[1][0], dx4, dx4b, 0.5, "l1a", reduce_after("l1m", dx4),
                                         on_grads=reduce_behind_dh("l1a"))

    (dx2, dx2b, dx2bt), dgn2, _ = _swiglu_bwd(sv_f2_0, gn[2], wf[0][1], dx3, dx3b, 1.0, "l0b", reduce_after("l1a", dx3),
                                             on_grads=reduce_behind_dh("l0b"))
    d_wout_e = _mm_nn(ot_e, dx2b, 1024, BF, "even_out_dw")
    do_e = _mm_nt([(dx2b, w_out_e)], "even_out_do", BF, deps=reduce_after("l0b", dx2))
    dq_sb, dk_sb, dv_sb = _sb_bwd(qkv_e, do_e, tot_sb, "sb_bwd")
    dq_fx, dk_fx, dv_fx, dcq, dck = _fox_bwd(qkv_e, cum_b, cum_t, o_fox, lse_fox, do_e, "fox_bwd")
    df, db_part = _fgate_bwd(dcq, dck, f_e, b_pad, "fgate_bwd")
    dqkv_e = jnp.concatenate([dq_sb, dq_fx, dk_sb, dk_fx, dv_sb, dv_fx], axis=1)
    d_wqkv_e = _mm_nn(ht_e, dqkv_e, 768, BF, "even_qkv_dw")
    d_wf = _mm_nn(ht_e, df, HD, BF, "even_fgate_dw")
    d_win_nat = jnp.concatenate([d_wqkv_e, d_wf[:, :nfox]], axis=1)
    d_win = jnp.transpose(d_win_nat.reshape(d, NDEV, -1), (1, 0, 2))
    dh_e = _mm_nt([(dqkv_e, w_qkv_e), (df, w_f)], "even_in_dh", deps=cross([d_win, d_wout_e], "l0m"))
    dx1, dx1b, _, dgn1 = _rmsnorm_bwd(x1, gn[1], dh_e, dx2, 0.5, "norm_bwd_l0m")
    (dx0, _, _), dgn0, _ = _swiglu_bwd(sv_f1_0, gn[0], wf[0][0], dx1, dx1b, 1.0, "l0a", reduce_after("l0m", dx1),
                                      on_down=reduce_now("l0a_down", None), on_grads=reduce_now("l0a_gu", ["gate", "up"]))

    def reduce_finish(red, tag, after):
        gs, a_s, st = red
        return list(zip(gs, a_s, _chip_finish(st, after, f"chip_finish_{tag}")))

    f_l1b, f_l1m, f_l1a = (reduce_finish(red[t], t, dx0) for t in ("l1b", "l1m", "l1a"))
    f_l0b, f_l0m = (reduce_finish(red[t], t, dx0) for t in ("l0b", "l0m"))

    def update(w_, m_, v_, parts, nm):
        if w_.shape[2] % 128 == 0:
            return _adamw_sharded(w_, m_, v_, parts, slots, f"adamw_{nm}")
        outs = _adamw_sharded(jnp.swapaxes(w_, 1, 2), jnp.swapaxes(m_, 1, 2), jnp.swapaxes(v_, 1, 2), parts, slots,
                              f"adamw_{nm}", transposed=True)
        return [jnp.swapaxes(o, 1, 2) for o in outs]

    res = {}
    res["even_w_in"] = update(even_w_in, m_even_w_in, v_even_w_in, [f_l0m[0]], "even_w_in")
    res["even_w_out"] = _adamw_sharded(even_w_out, m_even_w_out, v_even_w_out, [f_l0m[1]], slots, "adamw_even_w_out")
    res["odd_w_qkv"] = _adamw_sharded(odd_w_qkv, m_odd_w_qkv, v_odd_w_qkv, [f_l1m[0]], slots, "adamw_odd_w_qkv")
    res["odd_w_out"] = _adamw_sharded(odd_w_out, m_odd_w_out, v_odd_w_out, [f_l1m[1]], slots, "adamw_odd_w_out")
    names = ["ffn2_w_gate", "ffn2_w_up", "ffn2_w_down", "ffn1_w_gate", "ffn1_w_up", "ffn1_w_down"]
    ws = [ffn2_w_gate, ffn2_w_up, ffn2_w_down, ffn1_w_gate, ffn1_w_up, ffn1_w_down]
    ms = [m_ffn2_w_gate, m_ffn2_w_up, m_ffn2_w_down, m_ffn1_w_gate, m_ffn1_w_up, m_ffn1_w_down]
    vs = [v_ffn2_w_gate, v_ffn2_w_up, v_ffn2_w_down, v_ffn1_w_gate, v_ffn1_w_up, v_ffn1_w_down]
    for k in range(3):
        res[names[k]] = update(ws[k], ms[k], vs[k], [f_l0b[k], f_l1b[k]], names[k])
    f_l0a = (reduce_finish(red["l0a_gu"], "l0a_gu", res["ffn2_w_down"][1])
             + reduce_finish(red["l0a_down"], "l0a_down", res["ffn2_w_down"][1]))
    for k in range(3, 6):
        res[names[k]] = update(ws[k], ms[k], vs[k], [f_l0a[k - 3], f_l1a[k - 3]], names[k])

    dnorm = jnp.concatenate([dgn0, dgn1, dgn2, dgn3, dgn4, dgn5], axis=0)
    nsm = d // NDEV
    small_rows = (6 * d + d + 2 * HD) // HD
    pad_rows = -small_rows % 8
    part = jnp.concatenate([dnorm.reshape(-1), d_gfin.reshape(-1), db_part.reshape(-1), loss_part.reshape(-1),
                            jnp.zeros((pad_rows * HD,), F32)]).reshape(small_rows + pad_rows, HD)
    (gathered,) = _all_gather([part], "gather_small")

    def pack(ng, bfg, fg):
        full = lax.dynamic_update_slice(jnp.zeros((6, d), F32), ng.reshape(6, nsm), (0, me * nsm))
        return jnp.concatenate([full.reshape(-1), fg.reshape(-1), jnp.pad(bfg.reshape(-1), (0, HD - nfox)),
                                jnp.zeros((HD + pad_rows * HD,), F32)]).reshape(small_rows + pad_rows, HD)

    sm = _adamw_small(pack(norm_g, even_b_forget, final_norm_g), pack(m_norm_g, m_even_b_forget, m_final_norm_g),
                      pack(v_norm_g, v_even_b_forget, v_final_norm_g), gathered, "adamw_small")

    def unpack(t):
        flat = t.reshape(-1)
        ng = lax.dynamic_slice(flat[:6 * d].reshape(6, d), (0, me * nsm), (6, nsm)).reshape(norm_g.shape)
        fg = flat[6 * d:7 * d].reshape(final_norm_g.shape)
        bfg = flat[7 * d:7 * d + nfox].reshape(even_b_forget.shape)
        return ng, bfg, fg

    sm_g, sm_d, sm_m, sm_v = [unpack(t) for t in sm]
    loss = sm[0].reshape(-1)[7 * d + HD]

    order = ["norm_g", "ffn1_w_gate", "ffn1_w_up", "ffn1_w_down", "ffn2_w_gate", "ffn2_w_up", "ffn2_w_down", "even_w_in",
             "even_b_forget", "even_w_out", "odd_w_qkv", "odd_w_out", "final_norm_g"]
    outs = [loss, dx0.reshape(x.shape)]
    for k in range(4):
        smk = [sm_g, sm_d, sm_m, sm_v][k]
        for nm in order:
            if nm == "norm_g":
                outs.append(smk[0])
            elif nm == "even_b_forget":
                outs.append(smk[1])
            elif nm == "final_norm_g":
                outs.append(smk[2])
            else:
                outs.append(res[nm][k])
    return tuple(outs)
```

```python
import functools
import math

import jax
import jax.numpy as jnp
from jax import lax
from jax.experimental import pallas as pl
from jax.experimental.pallas import tpu as pltpu

F32 = jnp.float32
BF = jnp.bfloat16
NDEV = 8
HD = 128
QB = 128
N_HEADS = 16
N_SB = 8
SCALE = HD ** -0.5
ROPE_THETA = 500000.0
ROPE_DIMS = HD // 4
DILATED_PATTERNS = ((128, 1), (512, 4), (2048, 16))
RMS_EPS = 1e-6
NEG_INF = -1e30
ADAM_LR = 0.001
ADAM_B1 = 0.9
ADAM_B2 = 0.999
ADAM_EPS = 1e-08
ADAM_WD = 0.01
ADAM_STEP = 10
VMEM_LIMIT_V7X = 56 * 1024 * 1024
MESH = pl.DeviceIdType.MESH
ANY = pl.BlockSpec(memory_space=pl.ANY)

NT_DIMS = (((1,), (1,)), ((), ()))


def _cp(*dims):
    return pltpu.CompilerParams(dimension_semantics=dims if dims else None, vmem_limit_bytes=VMEM_LIMIT_V7X)


def _dot(a, b):
    return jnp.dot(a, b, preferred_element_type=F32)


def _dot_nt(a, b):
    return lax.dot_general(a, b, NT_DIMS, preferred_element_type=F32)


def _sds(shape, dtype):
    return jax.ShapeDtypeStruct(shape, dtype)


def _place():
    x, y, c = lax.axis_index("x"), lax.axis_index("y"), lax.axis_index("c")
    chips = [(x, y), (1 - x, y), (x, 1 - y), (1 - x, 1 - y)]
    return x, y, c, chips


def _all_gather(xs, name):
    n = len(xs)

    def body(*refs):
        x_refs, out_refs = refs[:n], refs[n:2 * n]
        send_sems, recv_sems, local_sems = refs[2 * n:]
        x, y, c, chips = _place()
        me, sibling = (x, y, c), (x, y, 1 - c)
        others = chips[1:]

        def slot(a, px, py, pc):
            return out_refs[a].at[4 * px + 2 * py + pc]

        def copy(a, k, block, to, src=None):
            return pltpu.make_async_remote_copy(
                src_ref=slot(a, *block) if src is None else src, dst_ref=slot(a, *block),
                send_sem=send_sems.at[a, k], recv_sem=recv_sems.at[a, k], device_id=to, device_id_type=MESH)

        started = []
        for a in range(n):
            mine = pltpu.make_async_copy(x_refs[a], slot(a, *me), local_sems.at[a])
            mine.start()
            first = [copy(a, 0, me, sibling, src=x_refs[a])]
            first += [copy(a, 1 + j, me, (*chip, c), src=x_refs[a]) for j, chip in enumerate(others)]
            for cp in first:
                cp.start()
            started += [mine.wait] + [cp.wait_send for cp in first]
        for a in range(n):
            for j, chip in enumerate(others):
                copy(a, 1 + j, (*chip, c), me).wait_recv()
                passed = copy(a, 4 + j, (*chip, c), sibling)
                passed.start()
                started.append(passed.wait_send)
        for a in range(n):
            copy(a, 0, sibling, me).wait_recv()
            for j, chip in enumerate(others):
                copy(a, 4 + j, (*chip, 1 - c), me).wait_recv()
        for w in started:
            w()

    return pl.pallas_call(
        body, name=name,
        out_shape=[_sds((NDEV,) + x.shape, x.dtype) for x in xs],
        in_specs=[ANY] * n, out_specs=[ANY] * n,
        scratch_shapes=[pltpu.SemaphoreType.DMA((n, 7)), pltpu.SemaphoreType.DMA((n, 7)), pltpu.SemaphoreType.DMA((n,))],
    )(*xs)


def _pair_exchange(gs, name):
    n = len(gs)

    def body(*refs):
        g_refs, a_refs = refs[:n], refs[n:2 * n]
        send_sems, recv_sems = refs[2 * n:]
        x, y, c, chips = _place()
        copies = []
        for a in range(n):
            for j, (px, py) in enumerate(chips):
                copies.append(pltpu.make_async_remote_copy(
                    src_ref=g_refs[a].at[4 * px + 2 * py + (1 - c)], dst_ref=a_refs[a].at[j],
                    send_sem=send_sems.at[a, j], recv_sem=recv_sems.at[a, j],
                    device_id=(x, y, 1 - c), device_id_type=MESH))
        for cp in copies:
            cp.start()
        for cp in copies:
            cp.wait()

    return pl.pallas_call(
        body, name=name,
        out_shape=[_sds((4,) + g.shape[1:], g.dtype) for g in gs],
        in_specs=[ANY] * n, out_specs=[ANY] * n,
        scratch_shapes=[pltpu.SemaphoreType.DMA((n, 4)), pltpu.SemaphoreType.DMA((n, 4))],
    )(*gs)


HBM = pl.BlockSpec(memory_space=pltpu.HBM)
SEM = pl.BlockSpec(memory_space=pltpu.SEMAPHORE)
EFFECT = pltpu.SideEffectType.DATAFLOW_SIDE_EFFECTING
TOKEN = _sds((8, 128), F32)
TOKEN_SPEC = pl.BlockSpec((8, 128), lambda *_: (0, 0))


def _in_hbm(x):
    return pltpu.with_memory_space_constraint(x, pltpu.HBM)


def _ignore_deps(body, n_in, n_deps):
    if not n_deps:
        return body
    return lambda *refs: body(*refs[:n_in], *refs[n_in + n_deps:])


def _slot_of(px, py, pc):
    return 4 * px + 2 * py + pc


def _gather_start(xs, me, deps, name):
    n = len(xs)
    lands = [lax.dynamic_update_slice(lax.empty((NDEV,) + x.shape, x.dtype), x[None], (me,) + (0,) * x.ndim) for x in xs]

    def body(*refs):
        x_refs, land_refs = refs[:n], refs[n:2 * n]
        send, recv_ici, recv_sib = refs[2 * n:2 * n + 3]
        token = refs[4 * n + 3]
        x, y, c, chips = _place()
        for a in range(n):
            dst = land_refs[a].at[_slot_of(x, y, c)]
            pltpu.make_async_remote_copy(src_ref=x_refs[a], dst_ref=dst, send_sem=send.at[4 * a], recv_sem=recv_sib.at[a],
                                         device_id=(x, y, 1 - c), device_id_type=MESH).start()
            for j, chip in enumerate(chips[1:]):
                pltpu.make_async_remote_copy(src_ref=x_refs[a], dst_ref=dst, send_sem=send.at[4 * a + 1 + j], recv_sem=recv_ici.at[3 * a + j],
                                             device_id=(*chip, c), device_id_type=MESH).start()
        token[...] = jnp.zeros_like(token)

    outs = pl.pallas_call(
        _ignore_deps(body, 2 * n, len(deps)), name=name,
        out_shape=(pltpu.SemaphoreType.DMA((4 * n,)), pltpu.SemaphoreType.DMA((3 * n,)), pltpu.SemaphoreType.DMA((n,)),
                   *[pltpu.HBM(x.shape, x.dtype) for x in xs], *[pltpu.HBM(l.shape, l.dtype) for l in lands], TOKEN),
        in_specs=[HBM] * (2 * n) + [TOKEN_SPEC] * len(deps),
        out_specs=(SEM, SEM, SEM, *[HBM] * (2 * n), pl.BlockSpec(memory_space=pltpu.VMEM)),
        input_output_aliases={a: 3 + a for a in range(2 * n)},
        compiler_params=pltpu.CompilerParams(has_side_effects=EFFECT),
    )(*[_in_hbm(x) for x in xs], *[_in_hbm(l) for l in lands], *deps)
    send, recv_ici, recv_sib = outs[:3]
    return dict(send=send, recv_ici=recv_ici, recv_sib=recv_sib, xs=list(outs[3:3 + n]), lands=list(outs[3 + n:3 + 2 * n]), token=outs[-1])


def _gather_forward(st, after, name):
    n = len(st["lands"])

    def body(*refs):
        land_refs, recv_ici = refs[:n], refs[n]
        send2, recv2 = refs[n + 2], refs[n + 3]
        x, y, c, chips = _place()
        for a in range(n):
            for j, chip in enumerate(chips[1:]):
                blk = land_refs[a].at[_slot_of(*chip, c)]
                pltpu.make_async_remote_copy(src_ref=blk, dst_ref=blk, send_sem=send2.at[3 * a + j], recv_sem=recv_ici.at[3 * a + j],
                                             device_id=(*chip, c), device_id_type=MESH).wait_recv()
                pltpu.make_async_remote_copy(src_ref=blk, dst_ref=blk, send_sem=send2.at[3 * a + j], recv_sem=recv2.at[3 * a + j],
                                             device_id=(x, y, 1 - c), device_id_type=MESH).start()

    outs = pl.pallas_call(
        body, name=name,
        out_shape=(pltpu.SemaphoreType.DMA((3 * n,)), pltpu.SemaphoreType.DMA((3 * n,)), *[pltpu.HBM(l.shape, l.dtype) for l in st["lands"]]),
        in_specs=[HBM] * n + [SEM, pl.BlockSpec(memory_space=pl.ANY)],
        out_specs=(SEM, SEM, *[HBM] * n),
        input_output_aliases={a: 2 + a for a in range(n)},
        compiler_params=pltpu.CompilerParams(has_side_effects=EFFECT),
    )(*st["lands"], st["recv_ici"], after)
    return dict(st, send2=outs[0], recv2=outs[1], lands=list(outs[2:]))


def _gather_finish(st, after, name):
    n = len(st["lands"])

    def body(*refs):
        x_refs, land_refs = refs[:n], refs[n:2 * n]
        send, recv_sib, send2, recv2 = refs[2 * n:2 * n + 4]
        x, y, c, chips = _place()
        for a in range(n):
            mine = land_refs[a].at[_slot_of(x, y, c)]
            theirs = land_refs[a].at[_slot_of(x, y, 1 - c)]
            for k in range(4):
                pltpu.make_async_remote_copy(src_ref=x_refs[a], dst_ref=mine, send_sem=send.at[4 * a + k], recv_sem=recv_sib.at[a],
                                             device_id=(x, y, 1 - c), device_id_type=MESH).wait_send()
            pltpu.make_async_remote_copy(src_ref=x_refs[a], dst_ref=theirs, send_sem=send.at[4 * a], recv_sem=recv_sib.at[a],
                                         device_id=(x, y, 1 - c), device_id_type=MESH).wait_recv()
            for j, chip in enumerate(chips[1:]):
                sent = land_refs[a].at[_slot_of(*chip, c)]
                got = land_refs[a].at[_slot_of(*chip, 1 - c)]
                pltpu.make_async_remote_copy(src_ref=sent, dst_ref=sent, send_sem=send2.at[3 * a + j], recv_sem=recv2.at[3 * a + j],
                                             device_id=(x, y, 1 - c), device_id_type=MESH).wait_send()
                pltpu.make_async_remote_copy(src_ref=got, dst_ref=got, send_sem=send2.at[3 * a + j], recv_sem=recv2.at[3 * a + j],
                                             device_id=(x, y, 1 - c), device_id_type=MESH).wait_recv()

    outs = pl.pallas_call(
        body, name=name,
        out_shape=tuple(pltpu.HBM(v.shape, v.dtype) for v in st["xs"] + st["lands"]),
        in_specs=[HBM] * (2 * n) + [SEM] * 4 + [pl.BlockSpec(memory_space=pl.ANY)], out_specs=tuple([HBM] * (2 * n)),
        input_output_aliases={a: a for a in range(2 * n)},
        compiler_params=pltpu.CompilerParams(has_side_effects=EFFECT),
    )(*st["xs"], *st["lands"], st["send"], st["recv_sib"], st["send2"], st["recv2"], after)
    return list(outs[n:])


def _pair_start(gs, name):
    n = len(gs)
    lands = [lax.empty((4,) + g.shape[1:], g.dtype) for g in gs]

    def body(*refs):
        g_refs, a_refs = refs[:n], refs[n:2 * n]
        send, recv = refs[2 * n], refs[2 * n + 1]
        token = refs[4 * n + 2]
        x, y, c, chips = _place()
        for a in range(n):
            for j, (px, py) in enumerate(chips):
                pltpu.make_async_remote_copy(src_ref=g_refs[a].at[_slot_of(px, py, 1 - c)], dst_ref=a_refs[a].at[j],
                                             send_sem=send.at[4 * a + j], recv_sem=recv.at[4 * a + j],
                                             device_id=(x, y, 1 - c), device_id_type=MESH).start()
        token[...] = jnp.zeros_like(token)

    outs = pl.pallas_call(
        body, name=name,
        out_shape=(pltpu.SemaphoreType.DMA((4 * n,)), pltpu.SemaphoreType.DMA((4 * n,)),
                   *[pltpu.HBM(g.shape, g.dtype) for g in gs], *[pltpu.HBM(l.shape, l.dtype) for l in lands], TOKEN),
        in_specs=[HBM] * (2 * n), out_specs=(SEM, SEM, *[HBM] * (2 * n), pl.BlockSpec(memory_space=pltpu.VMEM)),
        input_output_aliases={a: 2 + a for a in range(2 * n)},
        compiler_params=pltpu.CompilerParams(has_side_effects=EFFECT),
    )(*[_in_hbm(g) for g in gs], *[_in_hbm(l) for l in lands])
    return dict(send=outs[0], recv=outs[1], gs=list(outs[2:2 + n]), lands=list(outs[2 + n:2 + 2 * n]), token=outs[-1])


def _pair_finish(st, after, name):
    n = len(st["gs"])

    def body(*refs):
        g_refs, a_refs = refs[:n], refs[n:2 * n]
        send, recv = refs[2 * n], refs[2 * n + 1]
        x, y, c, chips = _place()
        for a in range(n):
            for j, (px, py) in enumerate(chips):
                cp = pltpu.make_async_remote_copy(src_ref=g_refs[a].at[_slot_of(px, py, 1 - c)], dst_ref=a_refs[a].at[j],
                                                  send_sem=send.at[4 * a + j], recv_sem=recv.at[4 * a + j],
                                                  device_id=(x, y, 1 - c), device_id_type=MESH)
                cp.wait_send()
                cp.wait_recv()

    outs = pl.pallas_call(
        body, name=name,
        out_shape=tuple(pltpu.HBM(v.shape, v.dtype) for v in st["gs"] + st["lands"]),
        in_specs=[HBM] * (2 * n) + [SEM, SEM, pl.BlockSpec(memory_space=pl.ANY)], out_specs=tuple([HBM] * (2 * n)),
        input_output_aliases={a: a for a in range(2 * n)},
        compiler_params=pltpu.CompilerParams(has_side_effects=EFFECT),
    )(*st["gs"], *st["lands"], st["send"], st["recv"], after)
    return list(outs[:n]), list(outs[n:])


def _chip_start(ps, name):
    n = len(ps)
    lands = [lax.empty(p.shape, p.dtype) for p in ps]

    def body(*refs):
        p_refs, b_refs = refs[:n], refs[n:2 * n]
        send, recv = refs[2 * n], refs[2 * n + 1]
        token = refs[4 * n + 2]
        x, y, c, chips = _place()
        for a in range(n):
            for j, chip in enumerate(chips[1:]):
                pltpu.make_async_remote_copy(src_ref=p_refs[a].at[j], dst_ref=b_refs[a].at[j], send_sem=send.at[3 * a + j], recv_sem=recv.at[3 * a + j],
                                             device_id=(*chip, c), device_id_type=MESH).start()
        token[...] = jnp.zeros_like(token)

    outs = pl.pallas_call(
        body, name=name,
        out_shape=(pltpu.SemaphoreType.DMA((3 * n,)), pltpu.SemaphoreType.DMA((3 * n,)),
                   *[pltpu.HBM(p.shape, p.dtype) for p in ps], *[pltpu.HBM(p.shape, p.dtype) for p in ps], TOKEN),
        in_specs=[HBM] * (2 * n), out_specs=(SEM, SEM, *[HBM] * (2 * n), pl.BlockSpec(memory_space=pltpu.VMEM)),
        input_output_aliases={a: 2 + a for a in range(2 * n)},
        compiler_params=pltpu.CompilerParams(has_side_effects=EFFECT),
    )(*[_in_hbm(p) for p in ps], *[_in_hbm(l) for l in lands])
    return dict(send=outs[0], recv=outs[1], ps=list(outs[2:2 + n]), lands=list(outs[2 + n:2 + 2 * n]), token=outs[-1])


def _chip_finish(st, after, name):
    n = len(st["ps"])

    def body(*refs):
        p_refs, b_refs = refs[:n], refs[n:2 * n]
        send, recv = refs[2 * n], refs[2 * n + 1]
        x, y, c, chips = _place()
        for a in range(n):
            for j, chip in enumerate(chips[1:]):
                cp = pltpu.make_async_remote_copy(src_ref=p_refs[a].at[j], dst_ref=b_refs[a].at[j], send_sem=send.at[3 * a + j], recv_sem=recv.at[3 * a + j],
                                                  device_id=(*chip, c), device_id_type=MESH)
                cp.wait_send()
                cp.wait_recv()

    outs = pl.pallas_call(
        body, name=name,
        out_shape=tuple(pltpu.HBM(v.shape, v.dtype) for v in st["ps"] + st["lands"]),
        in_specs=[HBM] * (2 * n) + [SEM, SEM, pl.BlockSpec(memory_space=pl.ANY)], out_specs=tuple([HBM] * (2 * n)),
        input_output_aliases={a: a for a in range(2 * n)},
        compiler_params=pltpu.CompilerParams(has_side_effects=EFFECT),
    )(*st["ps"], *st["lands"], st["send"], st["recv"], after)
    return list(outs[n:])


def _rows_tile(r):
    for t in (512, 256, 128, 64, 32, 16):
        if r % t == 0:
            return t
    return r


def _pair_sum(g, a, slots, name):
    _, r, c = g.shape
    tr = _rows_tile(r)

    def body(slots_ref, g_ref, a_ref, p_ref):
        p_ref[...] = (g_ref[...].astype(F32) + a_ref[...].astype(F32)).astype(BF)

    return pl.pallas_call(
        body, name=name,
        grid_spec=pltpu.PrefetchScalarGridSpec(
            num_scalar_prefetch=1, grid=(3, r // tr),
            in_specs=[pl.BlockSpec((None, tr, c), lambda j, i, s: (s[j + 1], i, 0)),
                      pl.BlockSpec((None, tr, c), lambda j, i, s: (j + 1, i, 0))],
            out_specs=pl.BlockSpec((None, tr, c), lambda j, i, s: (j, i, 0))),
        out_shape=_sds((3, r, c), BF), compiler_params=_cp("parallel", "parallel"),
    )(slots, g, a)


def _adamw_math(w, g, m, v):
    m = ADAM_B1 * m + (1.0 - ADAM_B1) * g
    v = ADAM_B2 * v + (1.0 - ADAM_B2) * (g * g)
    m_hat = m / (1.0 - ADAM_B1 ** ADAM_STEP)
    v_hat = v / (1.0 - ADAM_B2 ** ADAM_STEP)
    delta = -ADAM_LR * (m_hat / (jnp.sqrt(v_hat) + ADAM_EPS) + ADAM_WD * w)
    return delta, m, v


def _adamw_sharded(w, m, v, parts, slots, name, transposed=False):
    nl = w.shape[0]
    r, c = parts[0][0].shape[1:]
    tr = _rows_tile(r)
    if c * tr * 4 > (1 << 21) and not transposed:
        tr = max(8, tr // 2)

    def body(slots_ref, w_ref, m_ref, v_ref, *rest):
        part_refs, (g_out, d_out, m_out, v_out) = rest[:5 * nl], rest[5 * nl:]
        layer = pl.program_id(0)
        g = None
        for l in range(nl):
            s = part_refs[5 * l][...].astype(F32)
            for ref in part_refs[5 * l + 1:5 * l + 5]:
                s = s + ref[...].astype(F32)
            g = s if g is None else jnp.where(layer == l, s, g)
        if transposed:
            g = g.T
        delta, mn, vn = _adamw_math(w_ref[...], g, m_ref[...], v_ref[...])
        g_out[...] = g
        d_out[...] = delta
        m_out[...] = mn
        v_out[...] = vn

    def own(l):
        return lambda L, i, s: (s[0], jnp.where(L == l, i, 0), 0)

    def fixed(l, k):
        return lambda L, i, s: (k, jnp.where(L == l, i, 0), 0)

    if transposed:
        wspec = pl.BlockSpec((None, c, tr), lambda L, i, s: (L, 0, i))
    else:
        wspec = pl.BlockSpec((None, tr, c), lambda L, i, s: (L, i, 0))
    in_specs = [wspec, wspec, wspec]
    args = [w, m, v]
    for l, (g, a, b) in enumerate(parts):
        in_specs += [pl.BlockSpec((None, tr, c), own(l)), pl.BlockSpec((None, tr, c), fixed(l, 0)),
                     pl.BlockSpec((None, tr, c), fixed(l, 0)), pl.BlockSpec((None, tr, c), fixed(l, 1)),
                     pl.BlockSpec((None, tr, c), fixed(l, 2))]
        args += [g, a, b, b, b]
    return pl.pallas_call(
        body, name=name,
        grid_spec=pltpu.PrefetchScalarGridSpec(
            num_scalar_prefetch=1, grid=(nl, r // tr), in_specs=in_specs, out_specs=[wspec] * 4),
        out_shape=[_sds(w.shape, F32)] * 4, compiler_params=_cp("arbitrary", "arbitrary"),
    )(slots, *args)


def _adamw_small(w, m, v, gathered, name):
    def body(w_ref, m_ref, v_ref, gg_ref, g_out, d_out, m_out, v_out):
        g = gg_ref[0]
        for k in range(1, NDEV):
            g = g + gg_ref[k]
        delta, mn, vn = _adamw_math(w_ref[...], g, m_ref[...], v_ref[...])
        g_out[...] = g
        d_out[...] = delta
        m_out[...] = mn
        v_out[...] = vn

    return pl.pallas_call(body, name=name, out_shape=[_sds(w.shape, F32)] * 4)(w, m, v, gathered)


def _rmsnorm_fwd(x, g, name, deps=()):
    s, d = x.shape
    tm = 256

    def body(x_ref, g_ref, h_ref, ht_ref):
        xf = x_ref[...]
        y = xf * lax.rsqrt(jnp.mean(xf * xf, axis=-1, keepdims=True) + RMS_EPS)
        h = y * g_ref[...]
        h_ref[...] = h.astype(BF)
        ht_ref[...] = h.T.astype(BF)

    return pl.pallas_call(
        _ignore_deps(body, 2, len(deps)), name=name, grid=(s // tm,),
        in_specs=[pl.BlockSpec((tm, d), lambda i: (i, 0)), pl.BlockSpec((1, d), lambda i: (0, 0))] + [TOKEN_SPEC] * len(deps),
        out_specs=[pl.BlockSpec((tm, d), lambda i: (i, 0)), pl.BlockSpec((d, tm), lambda i: (0, i))],
        out_shape=[_sds((s, d), BF), _sds((d, s), BF)], compiler_params=_cp("parallel"),
    )(x, g, *deps)


def _rmsnorm_bwd(x, g, dh, dres, out_scale, name):
    s, d = x.shape
    tm = 256

    def body(x_ref, g_ref, dh_ref, dres_ref, dx_ref, dxb_ref, dxbt_ref, dg_ref):
        xf = x_ref[...]
        r = lax.rsqrt(jnp.mean(xf * xf, axis=-1, keepdims=True) + RMS_EPS)
        xhat = xf * r
        dhv = dh_ref[...]
        dxhat = dhv * g_ref[...]
        dx = dres_ref[...] + r * (dxhat - xhat * jnp.mean(dxhat * xhat, axis=-1, keepdims=True))
        dx_ref[...] = dx
        scaled = dx * out_scale
        dxb_ref[...] = scaled.astype(BF)
        dxbt_ref[...] = scaled.T.astype(BF)

        @pl.when(pl.program_id(0) == 0)
        def _():
            dg_ref[...] = jnp.zeros_like(dg_ref)

        dg_ref[...] += jnp.sum(dhv * xhat, axis=0, keepdims=True)

    row = pl.BlockSpec((tm, d), lambda i: (i, 0))
    vec = pl.BlockSpec((1, d), lambda i: (0, 0))
    return pl.pallas_call(
        body, name=name, grid=(s // tm,),
        in_specs=[row, vec, row, row],
        out_specs=[row, row, pl.BlockSpec((d, tm), lambda i: (0, i)), vec],
        out_shape=[_sds((s, d), F32), _sds((s, d), BF), _sds((d, s), BF), _sds((1, d), F32)],
        compiler_params=_cp("arbitrary"),
    )(x, g, dh, dres)


def _loss_head(x, g, target, name):
    s, d = x.shape
    tm = 256

    def body(x_ref, g_ref, t_ref, dx_ref, dxb_ref, dxbt_ref, dg_ref, loss_ref):
        xf = x_ref[...]
        r = lax.rsqrt(jnp.mean(xf * xf, axis=-1, keepdims=True) + RMS_EPS)
        xhat = xf * r
        err = xhat * g_ref[...] - t_ref[...]
        dy = err * (1.0 / d)
        dxhat = dy * g_ref[...]
        dx = r * (dxhat - xhat * jnp.mean(dxhat * xhat, axis=-1, keepdims=True))
        dx_ref[...] = dx
        half = dx * 0.5
        dxb_ref[...] = half.astype(BF)
        dxbt_ref[...] = half.T.astype(BF)

        @pl.when(pl.program_id(0) == 0)
        def _():
            dg_ref[...] = jnp.zeros_like(dg_ref)
            loss_ref[...] = jnp.zeros_like(loss_ref)

        dg_ref[...] += jnp.sum(dy * xhat, axis=0, keepdims=True)
        part = 0.5 * jnp.sum(jnp.mean(err * err, axis=-1, keepdims=True), axis=0, keepdims=True)
        lane = lax.broadcasted_iota(jnp.int32, (1, 128), 1)
        loss_ref[...] += jnp.where(lane == 0, part, 0.0)

    row = pl.BlockSpec((tm, d), lambda i: (i, 0))
    vec = pl.BlockSpec((1, d), lambda i: (0, 0))
    return pl.pallas_call(
        body, name=name, grid=(s // tm,),
        in_specs=[row, vec, row],
        out_specs=[row, row, pl.BlockSpec((d, tm), lambda i: (0, i)), vec, pl.BlockSpec((1, 128), lambda i: (0, 0))],
        out_shape=[_sds((s, d), F32), _sds((s, d), BF), _sds((d, s), BF), _sds((1, d), F32), _sds((1, 128), F32)],
        compiler_params=_cp("arbitrary"),
    )(x, g, target)


def _act_spec(tm, n, natural, order):
    if natural:
        return pl.BlockSpec((tm, n), (lambda s, i: (i, s)) if order == "si" else (lambda i, s: (i, s)))
    return pl.BlockSpec((None, tm, n), (lambda s, i: (s, i, 0)) if order == "si" else (lambda i, s: (s, i, 0)))


def _act_shape(s, n, natural, dtype):
    return _sds((s, NDEV * n), dtype) if natural else _sds((NDEV, s, n), dtype)


def _ffn_up(h, wg, wu, name):
    s, d = h.shape
    n = wg.shape[2]
    tm = 1024

    def body(h_ref, wg_ref, wu_ref, g_ref, u_ref, a_ref, at_ref):
        hb = h_ref[...]
        g = _dot(hb, wg_ref[...])
        u = _dot(hb, wu_ref[...])
        g_ref[...] = g.astype(BF)
        u_ref[...] = u.astype(BF)
        act = g * jax.nn.sigmoid(g) * u
        a_ref[...] = act.astype(BF)
        at_ref[...] = act.T.astype(BF)

    wsp = pl.BlockSpec((None, d, n), lambda s_, i: (s_, 0, 0))
    blk = _act_spec(tm, n, False, "si")
    return pl.pallas_call(
        body, name=name, grid=(NDEV, s // tm),
        in_specs=[pl.BlockSpec((tm, d), lambda s_, i: (i, 0)), wsp, wsp],
        out_specs=[blk] * 3 + [pl.BlockSpec((None, n, tm), lambda s_, i: (s_, 0, i))],
        out_shape=[_act_shape(s, n, False, BF)] * 3 + [_sds((NDEV, n, s), BF)],
        compiler_params=_cp("parallel", "parallel"),
    )(h, wg, wu)


def _ffn_down(act, wd, x, name):
    _, s, n = act.shape
    d = wd.shape[2]
    tm = 512

    def body(a_ref, w_ref, x_ref, o_ref, acc):
        k = pl.program_id(1)

        @pl.when(k == 0)
        def _():
            acc[...] = jnp.zeros_like(acc)

        acc[...] += _dot(a_ref[...], w_ref[...])

        @pl.when(k == NDEV - 1)
        def _():
            o_ref[...] = x_ref[...] + 0.5 * acc[...]

    row = pl.BlockSpec((tm, d), lambda i, k: (i, 0))
    return pl.pallas_call(
        body, name=name, grid=(s // tm, NDEV),
        in_specs=[_act_spec(tm, n, False, "is"), pl.BlockSpec((None, n, d), lambda i, k: (k, 0, 0)), row],
        out_specs=row, out_shape=_sds((s, d), F32),
        scratch_shapes=[pltpu.VMEM((tm, d), F32)], compiler_params=_cp("parallel", "arbitrary"),
    )(act, wd, x)


def _ffn_bwd_act(dyb, wd, g, u, name, deps=()):
    s, d = dyb.shape
    n = wd.shape[1]
    tm = 1024

    def body(dy_ref, w_ref, g_ref, u_ref, dg_ref, du_ref):
        dact = _dot_nt(dy_ref[...], w_ref[...])
        gv = g_ref[...].astype(F32)
        uv = u_ref[...].astype(F32)
        sig = jax.nn.sigmoid(gv)
        dg_ref[...] = (dact * uv * (sig * (1.0 + gv * (1.0 - sig)))).astype(BF)
        du_ref[...] = (dact * (gv * sig)).astype(BF)

    blk = _act_spec(tm, n, False, "si")
    return pl.pallas_call(
        _ignore_deps(body, 4, len(deps)), name=name, grid=(NDEV, s // tm),
        in_specs=[pl.BlockSpec((tm, d), lambda s_, i: (i, 0)), pl.BlockSpec((None, n, d), lambda s_, i: (s_, 0, 0)), blk, blk]
        + [TOKEN_SPEC] * len(deps),
        out_specs=[blk, blk], out_shape=[_act_shape(s, n, False, BF)] * 2,
        compiler_params=_cp("parallel", "parallel"),
    )(dyb, wd, g, u, *deps)


def _grad_rows(act_t, dyb, name):
    _, n, s = act_t.shape
    d = dyb.shape[1]
    tn = 1024

    def body(a_ref, dy_ref, o_ref):
        o_ref[...] = _dot(a_ref[...], dy_ref[...]).astype(BF)

    return pl.pallas_call(
        body, name=name, grid=(NDEV, d // tn),
        in_specs=[pl.BlockSpec((None, n, s), lambda k, j: (k, 0, 0)), pl.BlockSpec((s, tn), lambda k, j: (0, j))],
        out_specs=pl.BlockSpec((None, n, tn), lambda k, j: (k, 0, j)), out_shape=_sds((NDEV, n, d), BF),
        compiler_params=_cp("parallel", "parallel"),
    )(act_t, dyb)


def _grad_cols(ht, dxs, naturals, name, deps=()):
    d, s = ht.shape
    k = len(dxs)
    ns = [dx.shape[1] // NDEV if nat else dx.shape[2] for dx, nat in zip(dxs, naturals)]
    td = 512

    def body(*refs):
        ht_ref, dx_refs, o_refs = refs[0], refs[1:1 + k], refs[1 + k:]
        hv = ht_ref[...]
        for dx_ref, o_ref in zip(dx_refs, o_refs):
            o_ref[...] = _dot(hv, dx_ref[...]).astype(BF)

    def dx_spec(n, nat):
        if nat:
            return pl.BlockSpec((s, n), lambda s_, j: (0, s_))
        return pl.BlockSpec((None, s, n), lambda s_, j: (s_, 0, 0))

    return pl.pallas_call(
        _ignore_deps(body, 1 + k, len(deps)), name=name, grid=(NDEV, d // td),
        in_specs=[pl.BlockSpec((td, s), lambda s_, j: (j, 0))] + [dx_spec(n, nat) for n, nat in zip(ns, naturals)]
        + [TOKEN_SPEC] * len(deps),
        out_specs=[pl.BlockSpec((None, td, n), lambda s_, j: (s_, j, 0)) for n in ns],
        out_shape=[_sds((NDEV, d, n), BF) for n in ns], compiler_params=_cp("parallel", "parallel"),
    )(ht, *dxs, *deps)


def _dh_cols(dxs, ws, naturals, name, deps=()):
    k = len(dxs)
    d = ws[0].shape[1]
    ns = [w.shape[2] for w in ws]
    s = dxs[0].shape[0] if naturals[0] else dxs[0].shape[1]
    tm = 512

    def body(*refs):
        dx_refs, w_refs, o_ref, acc = refs[:k], refs[k:2 * k], refs[2 * k], refs[2 * k + 1]
        j = pl.program_id(1)

        @pl.when(j == 0)
        def _():
            acc[...] = jnp.zeros_like(acc)

        t = _dot_nt(dx_refs[0][...], w_refs[0][...])
        for dx_ref, w_ref in zip(dx_refs[1:], w_refs[1:]):
            t = t + _dot_nt(dx_ref[...], w_ref[...])
        acc[...] += t

        @pl.when(j == NDEV - 1)
        def _():
            o_ref[...] = acc[...]

    return pl.pallas_call(
        _ignore_deps(body, 2 * k, len(deps)), name=name, grid=(s // tm, NDEV),
        in_specs=[_act_spec(tm, n, nat, "is") for n, nat in zip(ns, naturals)]
        + [pl.BlockSpec((None, d, n), lambda i, j: (j, 0, 0)) for n in ns] + [TOKEN_SPEC] * len(deps),
        out_specs=pl.BlockSpec((tm, d), lambda i, j: (i, 0)), out_shape=_sds((s, d), F32),
        scratch_shapes=[pltpu.VMEM((tm, d), F32)], compiler_params=_cp("parallel", "arbitrary"),
    )(*dxs, *ws, *deps)


def _mm_nn(a, b, tn, out_dtype, name, res=None, tm=512):
    m, k = a.shape
    nn = b.shape[1]

    def body(*refs):
        if res is None:
            a_ref, b_ref, o_ref = refs
            o_ref[...] = _dot(a_ref[...], b_ref[...]).astype(out_dtype)
        else:
            a_ref, b_ref, r_ref, o_ref = refs
            o_ref[...] = (r_ref[...] + _dot(a_ref[...], b_ref[...])).astype(out_dtype)

    osp = pl.BlockSpec((tm, tn), lambda j, i: (i, j))
    in_specs = [pl.BlockSpec((tm, k), lambda j, i: (i, 0)), pl.BlockSpec((k, tn), lambda j, i: (0, j))]
    args = [a, b]
    if res is not None:
        in_specs.append(osp)
        args.append(res)
    return pl.pallas_call(
        body, name=name, grid=(nn // tn, m // tm), in_specs=in_specs, out_specs=osp,
        out_shape=_sds((m, nn), out_dtype), compiler_params=_cp("parallel", "parallel"),
    )(*args)


def _mm_nt(pairs, name, out_dtype=F32, tm=512, tk=512, deps=()):
    m = pairs[0][0].shape[0]
    kk = pairs[0][1].shape[0]
    p = len(pairs)

    def body(*refs):
        o_ref = refs[2 * p]
        t = _dot_nt(refs[0][...], refs[1][...])
        for q in range(1, p):
            t = t + _dot_nt(refs[2 * q][...], refs[2 * q + 1][...])
        o_ref[...] = t.astype(out_dtype)

    in_specs, args = [], []
    for a, b in pairs:
        in_specs += [pl.BlockSpec((tm, a.shape[1]), lambda j, i: (i, 0)), pl.BlockSpec((tk, b.shape[1]), lambda j, i: (j, 0))]
        args += [a, b]
    return pl.pallas_call(
        _ignore_deps(body, 2 * p, len(deps)), name=name, grid=(kk // tk, m // tm), in_specs=in_specs + [TOKEN_SPEC] * len(deps),
        out_specs=pl.BlockSpec((tm, tk), lambda j, i: (i, j)), out_shape=_sds((m, kk), out_dtype),
        compiler_params=_cp("parallel", "parallel"),
    )(*args, *deps)


def _rope_tables(s, sign):
    half = ROPE_DIMS // 2
    freqs = ROPE_THETA ** (-jnp.arange(half, dtype=F32) / half)
    ang = jnp.arange(s, dtype=F32)[:, None] * freqs[None, :]
    cos, sin = jnp.cos(ang), sign * jnp.sin(ang)
    one = jnp.ones((s, HD - ROPE_DIMS), F32)
    zero = jnp.zeros((s, HD - ROPE_DIMS), F32)
    zh = jnp.zeros((s, half), F32)
    c = jnp.concatenate([cos, cos, one], axis=1)
    sa = jnp.concatenate([-sin, zh, zero], axis=1)
    sb = jnp.concatenate([zh, sin, zero], axis=1)
    return c, sa, sb


def _rope(xv, c, sa, sb):
    return xv * c + pltpu.roll(xv, HD - ROPE_DIMS // 2, 1) * sa + pltpu.roll(xv, ROPE_DIMS // 2, 1) * sb


def _qkv_rope(h, w, tables, name):
    s, d = h.shape
    n = w.shape[2]
    per = n // HD
    tm = 512

    def body(h_ref, w_ref, c_ref, sa_ref, sb_ref, o_ref):
        shard = pl.program_id(0)
        y = _dot(h_ref[...], w_ref[...])
        c, sa, sb = c_ref[...], sa_ref[...], sb_ref[...]
        for j in range(per):
            blk = y[:, j * HD:(j + 1) * HD]
            rot = _rope(blk, c, sa, sb)
            is_qk = shard * per + j < 2 * N_HEADS
            o_ref[:, j * HD:(j + 1) * HD] = jnp.where(is_qk, rot, blk).astype(BF)

    tab = pl.BlockSpec((tm, HD), lambda s_, i: (i, 0))
    return pl.pallas_call(
        body, name=name, grid=(NDEV, s // tm),
        in_specs=[pl.BlockSpec((tm, d), lambda s_, i: (i, 0)), pl.BlockSpec((None, d, n), lambda s_, i: (s_, 0, 0)), tab, tab, tab],
        out_specs=pl.BlockSpec((tm, n), lambda s_, i: (i, s_)), out_shape=_sds((s, NDEV * n), BF),
        compiler_params=_cp("parallel", "parallel"),
    )(h, w, *tables)


def _iota2():
    return (lax.broadcasted_iota(jnp.int32, (QB, QB), 0), lax.broadcasted_iota(jnp.int32, (QB, QB), 1))


def _softplus(z):
    return jnp.maximum(z, 0.0) + jnp.log(1.0 + jnp.exp(-jnp.abs(z)))


def _tri_dot(xv, tri, left=False):
    hi = xv.astype(BF)
    r1 = xv - hi.astype(F32)
    mid = r1.astype(BF)
    lo = (r1 - mid.astype(F32)).astype(BF)
    if left:
        return _dot(tri, hi) + _dot(tri, mid) + _dot(tri, lo)
    return _dot(hi, tri) + _dot(mid, tri) + _dot(lo, tri)


def _col(ref_or_val):
    return ref_or_val[:, 0:1]


KT = 4 * QB
QQ = 2 * QB


def _iota_tile():
    return (lax.broadcasted_iota(jnp.int32, (QQ, KT), 0), lax.broadcasted_iota(jnp.int32, (QQ, KT), 1))


def _scan_matrix(keep):
    tri = keep(*_iota2()).astype(BF)
    return jnp.concatenate([tri, tri], axis=0)


def _scan_dot(xv, tri2):
    hi = xv.astype(BF)
    lo = (xv - hi.astype(F32)).astype(BF)
    return _dot(jnp.concatenate([hi, lo], axis=1), tri2)


def _blocks(xv):
    return [xv[:, b * QB:(b + 1) * QB] for b in range(KT // QB)]


def _sb_fwd(qkv, name):
    s = qkv.shape[0]
    nb = s // QB

    def body(q_ref, k_ref, v_ref, o_ref, ot_ref, t_ref):
        i = pl.program_id(1)
        q = q_ref[...]
        row, col = _iota_tile()
        later_keys = _scan_matrix(lambda j, s_: j > s_)
        last = (i * QQ + QQ - 1) // KT

        def step(tt, carry):
            acc, later = carry
            t = last - tt
            off = pl.multiple_of(t * KT, KT)
            k = k_ref[pl.ds(off, KT), :]
            v = v_ref[pl.ds(off, KT), :]
            z = _dot_nt(q, k) * SCALE
            strict = row + (i * QQ - t * KT) > col
            sp = _softplus(z)
            lnb = jnp.where(strict, -sp, 0.0)
            afters = []
            for xb in reversed(_blocks(lnb)):
                afters.append(later + _scan_dot(xb, later_keys))
                later = later + jnp.sum(xb, axis=1, keepdims=True)
            after = jnp.concatenate(afters[::-1], axis=1)
            w = jnp.where(strict, jnp.exp((z - sp) + after), 0.0)
            return acc + _dot(w.astype(BF), v), later

        acc, total = lax.fori_loop(0, last + 1, step, (jnp.zeros((QQ, HD), F32), jnp.zeros((QQ, 1), F32)))
        o_ref[...] = acc.astype(BF)
        ot_ref[...] = acc.T.astype(BF)
        t_ref[...] = jnp.broadcast_to(total, (QQ, HD))

    blk = pl.BlockSpec((QQ, HD), lambda h, i: (i, h))
    return pl.pallas_call(
        body, name=name, grid=(N_SB, s // QQ),
        in_specs=[blk, pl.BlockSpec((s, HD), lambda h, i: (0, N_HEADS + h)), pl.BlockSpec((s, HD), lambda h, i: (0, 2 * N_HEADS + h))],
        out_specs=[blk, pl.BlockSpec((HD, QQ), lambda h, i: (h, i)), blk],
        out_shape=[_sds((s, N_SB * HD), BF), _sds((N_SB * HD, s), BF), _sds((s, N_SB * HD), F32)],
        compiler_params=_cp("parallel", "parallel"),
    )(qkv, qkv, qkv)


def _sb_bwd(qkv, do, total, name):
    s = qkv.shape[0]
    nb = s // QB

    def body(q_ref, k_ref, v_ref, do_ref, t_ref, dq_ref, dk_ref, dv_ref, dk_acc, dv_acc):
        i = pl.program_id(1)

        @pl.when(i == 0)
        def _():
            dk_acc[...] = jnp.zeros_like(dk_acc)
            dv_acc[...] = jnp.zeros_like(dv_acc)

        q = q_ref[...]
        dov = do_ref[...]
        tot = _col(t_ref[...])
        row, col = _iota_tile()
        keys_upto = _scan_matrix(lambda j, s_: j <= s_)
        keys_before = _scan_matrix(lambda j, s_: j < s_)

        def step(t, carry):
            dq, lnb_before, dl_before = carry
            off = pl.multiple_of(t * KT, KT)
            k = k_ref[pl.ds(off, KT), :]
            v = v_ref[pl.ds(off, KT), :]
            z = _dot_nt(q, k) * SCALE
            strict = row + (i * QQ - t * KT) > col
            sp = _softplus(z)
            lnb = jnp.where(strict, -sp, 0.0)
            afters = []
            for xb in _blocks(lnb):
                afters.append(tot - (lnb_before + _scan_dot(xb, keys_upto)))
                lnb_before = lnb_before + jnp.sum(xb, axis=1, keepdims=True)
            a = jnp.where(strict, jnp.exp((z - sp) + jnp.concatenate(afters, axis=1)), 0.0)
            dl = a * _dot_nt(dov, v)
            befores = []
            for xb in _blocks(dl):
                befores.append(dl_before + _scan_dot(xb, keys_before))
                dl_before = dl_before + jnp.sum(xb, axis=1, keepdims=True)
            sig = jnp.exp(z - sp)
            dz = jnp.where(strict, dl * (1.0 - sig) - sig * jnp.concatenate(befores, axis=1), 0.0) * SCALE
            dq = dq + _dot(dz.astype(BF), k)
            dk_acc[pl.ds(off, KT), :] += _dot(dz.T.astype(BF), q)
            dv_acc[pl.ds(off, KT), :] += _dot(a.T.astype(BF), dov)
            return dq, lnb_before, dl_before

        zero = jnp.zeros((QQ, 1), F32)
        dq, _, _ = lax.fori_loop(0, (i * QQ + QQ - 1) // KT + 1, step, (jnp.zeros((QQ, HD), F32), zero, zero))
        dq_ref[...] = dq.astype(BF)

        @pl.when(i == s // QQ - 1)
        def _():
            dk_ref[...] = dk_acc[...].astype(BF)
            dv_ref[...] = dv_acc[...].astype(BF)

    blk = pl.BlockSpec((QQ, HD), lambda h, i: (i, h))
    full = pl.BlockSpec((s, HD), lambda h, i: (0, h))
    return pl.pallas_call(
        body, name=name, grid=(N_SB, s // QQ),
        in_specs=[blk, pl.BlockSpec((s, HD), lambda h, i: (0, N_HEADS + h)), pl.BlockSpec((s, HD), lambda h, i: (0, 2 * N_HEADS + h)), blk, blk],
        out_specs=[blk, full, full], out_shape=[_sds((s, N_SB * HD), BF)] * 3,
        scratch_shapes=[pltpu.VMEM((s, HD), F32), pltpu.VMEM((s, HD), F32)],
        compiler_params=_cp("parallel", "arbitrary"),
    )(qkv, qkv, qkv, do, total)


def _fgate_fwd(f, b, name):
    s = f.shape[0]
    nb = s // QB
    nfox = N_HEADS - N_SB

    def body(f_ref, b_ref, cb_ref, ct_ref):
        row, col = _iota2()
        upto = (row >= col).astype(BF)
        carry = jnp.zeros((1, HD), F32)
        for blk in range(nb):
            xv = f_ref[blk * QB:(blk + 1) * QB, :] + b_ref[...]
            logf = -_softplus(-xv)
            cum = _tri_dot(logf, upto, left=True) + carry
            carry = cum[QB - 1:QB, :]
            ct_ref[blk] = cum.T
            for h in range(nfox):
                cb_ref[blk * QB:(blk + 1) * QB, h * HD:(h + 1) * HD] = jnp.broadcast_to(cum[:, h:h + 1], (QB, HD))

    return pl.pallas_call(
        body, name=name, out_shape=[_sds((s, nfox * HD), F32), _sds((nb, HD, HD), F32)], compiler_params=_cp(),
    )(f, b)


def _fgate_bwd(dcq, dck, f, b, name):
    s = f.shape[0]
    nb = s // QB
    nfox = N_HEADS - N_SB

    def body(dcq_ref, dck_ref, f_ref, b_ref, df_ref, db_ref):
        row, col = _iota2()
        from_tri = (row <= col).astype(BF)
        lane = col
        carry = jnp.zeros((1, HD), F32)
        db = jnp.zeros((1, HD), F32)
        for blk in reversed(range(nb)):
            dcum = jnp.zeros((QB, HD), F32)
            for h in range(nfox):
                here = (slice(blk * QB, (blk + 1) * QB), slice(h * HD, (h + 1) * HD))
                dcum = jnp.where(lane == h, dcq_ref[here] - dck_ref[here], dcum)
            dlogf = _tri_dot(dcum, from_tri, left=True) + carry
            carry = dlogf[0:1, :]
            xv = f_ref[blk * QB:(blk + 1) * QB, :] + b_ref[...]
            sp = _softplus(xv)
            df = jnp.where(lane < nfox, dlogf * jnp.exp(-sp), 0.0)
            df_ref[blk * QB:(blk + 1) * QB, :] = df.astype(BF)
            db = db + jnp.sum(df, axis=0, keepdims=True)
        db_ref[...] = db

    return pl.pallas_call(
        body, name=name, out_shape=[_sds((s, HD), BF), _sds((1, HD), F32)], compiler_params=_cp(),
    )(dcq, dck, f, b)


def _fox_head_row(ct_ref, j, h):
    tile = ct_ref[j]
    sub = lax.broadcasted_iota(jnp.int32, tile.shape, 0)
    return jnp.sum(jnp.where(sub == h, tile, 0.0), axis=0, keepdims=True)


def _fox_tile_row(ct_ref, t, h):
    nsub = KT // QB
    return jnp.concatenate([_fox_head_row(ct_ref, t * nsub + b, h) for b in range(nsub)], axis=1)


def _fox_fwd(qkv, cum_b, cum_t, name):
    s = qkv.shape[0]
    nb = s // QB
    nfox = N_HEADS - N_SB

    def body(q_ref, k_ref, v_ref, cq_ref, ct_ref, o_ref, ot_ref, lse_ref):
        h, i = pl.program_id(0), pl.program_id(1)
        q = q_ref[...]
        cq = _col(cq_ref[...])
        row, col = _iota_tile()

        def step(t, carry):
            acc, m, l = carry
            off = pl.multiple_of(t * KT, KT)
            k = k_ref[pl.ds(off, KT), :]
            v = v_ref[pl.ds(off, KT), :]
            z = _dot_nt(q, k) * SCALE + cq - _fox_tile_row(ct_ref, t, h)
            z = jnp.where(row + (i * QQ - t * KT) >= col, z, NEG_INF)
            m_new = jnp.maximum(m, jnp.max(z, axis=1, keepdims=True))
            alpha = jnp.exp(m - m_new)
            p = jnp.exp(z - m_new)
            l = alpha * l + jnp.sum(p, axis=1, keepdims=True)
            acc = alpha * acc + _dot(p.astype(BF), v)
            return acc, m_new, l

        acc, m, l = lax.fori_loop(0, (i * QQ + QQ - 1) // KT + 1, step,
                                  (jnp.zeros((QQ, HD), F32), jnp.full((QQ, 1), NEG_INF, F32), jnp.zeros((QQ, 1), F32)))
        o = acc / l
        o_ref[...] = o.astype(BF)
        ot_ref[...] = o.T.astype(BF)
        lse_ref[...] = jnp.broadcast_to(m + jnp.log(l), (QQ, HD))

    blk = pl.BlockSpec((QQ, HD), lambda h, i: (i, h))
    return pl.pallas_call(
        body, name=name, grid=(nfox, s // QQ),
        in_specs=[pl.BlockSpec((QQ, HD), lambda h, i: (i, N_SB + h)),
                  pl.BlockSpec((s, HD), lambda h, i: (0, N_HEADS + N_SB + h)),
                  pl.BlockSpec((s, HD), lambda h, i: (0, 2 * N_HEADS + N_SB + h)),
                  blk, pl.BlockSpec((nb, 8, HD), lambda h, i: (0, 0, 0))],
        out_specs=[blk, pl.BlockSpec((HD, QQ), lambda h, i: (h, i)), blk],
        out_shape=[_sds((s, nfox * HD), BF), _sds((nfox * HD, s), BF), _sds((s, nfox * HD), F32)],
        compiler_params=_cp("parallel", "parallel"),
    )(qkv, qkv, qkv, cum_b, cum_t)


def _fox_bwd(qkv, cum_b, cum_t, o, lse, do, name):
    s = qkv.shape[0]
    nb = s // QB
    nfox = N_HEADS - N_SB

    def body(q_ref, k_ref, v_ref, cq_ref, ct_ref, o_ref, lse_ref, do_ref, dq_ref, dk_ref, dv_ref, dcq_ref, dc_ref, dk_acc, dv_acc, dc_acc):
        h, i = pl.program_id(0), pl.program_id(1)

        @pl.when(i == 0)
        def _():
            dk_acc[...] = jnp.zeros_like(dk_acc)
            dv_acc[...] = jnp.zeros_like(dv_acc)
            dc_acc[...] = jnp.zeros_like(dc_acc)

        q = q_ref[...]
        cq = _col(cq_ref[...])
        dov = do_ref[...]
        lse_c = _col(lse_ref[...])
        delta = jnp.sum(dov.astype(F32) * o_ref[...].astype(F32), axis=1, keepdims=True)
        row, col = _iota_tile()
        ones = jnp.ones((QQ, HD), BF)

        def step(t, carry):
            dq, over_keys = carry
            off = pl.multiple_of(t * KT, KT)
            k = k_ref[pl.ds(off, KT), :]
            v = v_ref[pl.ds(off, KT), :]
            z = _dot_nt(q, k) * SCALE + cq - _fox_tile_row(ct_ref, t, h)
            p = jnp.where(row + (i * QQ - t * KT) >= col, jnp.exp(z - lse_c), 0.0)
            dz = p * (_dot_nt(dov, v) - delta)
            dzt = dz.T
            dq = dq + _dot((dz * SCALE).astype(BF), k)
            dk_acc[pl.ds(off, KT), :] += _dot((dzt * SCALE).astype(BF), q)
            dv_acc[pl.ds(off, KT), :] += _dot(p.T.astype(BF), dov)
            dc_acc[pl.ds(off, KT), :] += _tri_dot(dzt, ones)
            return dq, over_keys + jnp.sum(dz, axis=1, keepdims=True)

        dq, over_keys = lax.fori_loop(0, (i * QQ + QQ - 1) // KT + 1, step, (jnp.zeros((QQ, HD), F32), jnp.zeros((QQ, 1), F32)))
        dq_ref[...] = dq.astype(BF)
        dcq_ref[...] = jnp.broadcast_to(over_keys, (QQ, HD))

        @pl.when(i == s // QQ - 1)
        def _():
            dk_ref[...] = dk_acc[...].astype(BF)
            dv_ref[...] = dv_acc[...].astype(BF)
            dc_ref[...] = dc_acc[...]

    blk = pl.BlockSpec((QQ, HD), lambda h, i: (i, h))
    full = pl.BlockSpec((s, HD), lambda h, i: (0, h))
    return pl.pallas_call(
        body, name=name, grid=(nfox, s // QQ),
        in_specs=[pl.BlockSpec((QQ, HD), lambda h, i: (i, N_SB + h)),
                  pl.BlockSpec((s, HD), lambda h, i: (0, N_HEADS + N_SB + h)),
                  pl.BlockSpec((s, HD), lambda h, i: (0, 2 * N_HEADS + N_SB + h)),
                  blk, pl.BlockSpec((nb, 8, HD), lambda h, i: (0, 0, 0)), blk, blk,
                  pl.BlockSpec((QQ, HD), lambda h, i: (i, N_SB + h))],
        out_specs=[blk, full, full, blk, full],
        out_shape=[_sds((s, nfox * HD), BF)] * 3 + [_sds((s, nfox * HD), F32)] * 2,
        scratch_shapes=[pltpu.VMEM((s, HD), F32)] * 3,
        compiler_params=_cp("parallel", "arbitrary"),
    )(qkv, qkv, qkv, cum_b, cum_t, o, lse, do)


GB = 4
DIL_PAD = QB * 16


def _dil_group(g, d, off=0, shift=0):
    if d == 1:
        return [pl.ds(pl.multiple_of(off + (g * GB + shift) * QB, QB), GB * QB)]
    if d == 4:
        return [pl.ds(off + g + shift * QB * d, GB * QB, stride=d)]
    assert d == 16 and shift == 0
    return [pl.ds(off + g * GB + b, QB, stride=d) for b in range(GB)]


def _dil_load(ref, g, d, off=0, shift=0):
    parts = [ref[sl, :] for sl in _dil_group(g, d, off, shift)]
    rows = parts[0] if len(parts) == 1 else jnp.concatenate(parts, axis=0)
    return rows.reshape(GB, QB, HD)


def _dil_store(ref, g, d, val, off=0, shift=0, add=False):
    rows = val.reshape(GB * QB, HD)
    slices = _dil_group(g, d, off, shift)
    for b, sl in enumerate(slices):
        piece = rows if len(slices) == 1 else rows[b * QB:(b + 1) * QB]
        if add:
            ref[sl, :] += piece
        else:
            ref[sl, :] = piece


def _bdot_nt(a, b):
    return lax.dot_general(a, b, (((2,), (2,)), ((0,), (0,))), preferred_element_type=F32)


def _bdot(a, b):
    return lax.dot_general(a, b, (((2,), (1,)), ((0,), (0,))), preferred_element_type=F32)


def _bdot_tn(a, b):
    return lax.dot_general(jnp.swapaxes(a, 1, 2).astype(BF), b, (((2,), (1,)), ((0,), (0,))), preferred_element_type=F32)


def _dil_masks(g, d, nb):
    row = lax.broadcasted_iota(jnp.int32, (GB, QB, QB), 1)
    col = lax.broadcasted_iota(jnp.int32, (GB, QB, QB), 2)
    blk = lax.broadcasted_iota(jnp.int32, (GB, QB, QB), 0) + (g * GB if d == 1 else 0)
    return col <= row, jnp.logical_and(col >= row, blk >= 1) if nb > 1 else None


def _dilated_fwd(qkv, name):
    s = qkv.shape[0]
    npat = len(DILATED_PATTERNS)
    chunk = 256

    def body(q_ref, k_ref, v_ref, out_ref, outt_ref, g_ref, qf, kf, vf, *per_pattern):
        o_s, l_s = per_pattern[:npat], per_pattern[npat:]
        qf[...] = q_ref[...].astype(F32)
        for dst, src in ((kf, k_ref), (vf, v_ref)):
            dst[0:DIL_PAD, :] = jnp.zeros((DIL_PAD, HD), F32)
            dst[DIL_PAD:, :] = src[...].astype(F32)
        for p, (_, d) in enumerate(DILATED_PATTERNS):
            nb = s // d // QB

            def grp(g, carry, p=p, d=d, nb=nb):
                mc, mp = _dil_masks(g, d, nb)
                q = _dil_load(qf, g, d).astype(BF)
                zc = jnp.where(mc, _bdot_nt(q, _dil_load(kf, g, d, DIL_PAD).astype(BF)) * SCALE, NEG_INF)
                m = jnp.max(zc, axis=2, keepdims=True)
                if nb > 1:
                    zp = jnp.where(mp, _bdot_nt(q, _dil_load(kf, g, d, DIL_PAD, -1).astype(BF)) * SCALE, NEG_INF)
                    m = jnp.maximum(m, jnp.max(zp, axis=2, keepdims=True))
                ec = jnp.exp(zc - m)
                l = jnp.sum(ec, axis=2, keepdims=True)
                if nb > 1:
                    ep = jnp.where(mp, jnp.exp(zp - m), 0.0)
                    l = l + jnp.sum(ep, axis=2, keepdims=True)
                o = _bdot((ec / l).astype(BF), _dil_load(vf, g, d, DIL_PAD).astype(BF))
                if nb > 1:
                    o = o + _bdot((ep / l).astype(BF), _dil_load(vf, g, d, DIL_PAD, -1).astype(BF))
                _dil_store(o_s[p], g, d, o)
                _dil_store(l_s[p], g, d, jnp.broadcast_to(m + jnp.log(l), (GB, QB, HD)))
                return carry

            lax.fori_loop(0, s // (QB * GB), grp, 0)
        for c0 in range(0, s, chunk):
            rows = slice(c0, c0 + chunk)
            ls = [l_s[p][rows, :] for p in range(npat)]
            m = functools.reduce(jnp.maximum, ls)
            es = [jnp.exp(l - m) for l in ls]
            tot = functools.reduce(lambda a, b: a + b, es)
            out = functools.reduce(lambda a, b: a + b, [(e / tot) * o_s[p][rows, :] for p, e in enumerate(es)])
            out_ref[rows, :] = out.astype(BF)
            outt_ref[:, rows] = out.T.astype(BF)
            g_ref[rows, :] = m + jnp.log(tot)

    full = pl.BlockSpec((s, HD), lambda h: (0, h))
    return pl.pallas_call(
        body, name=name, grid=(N_HEADS,),
        in_specs=[full, pl.BlockSpec((s, HD), lambda h: (0, N_HEADS + h)), pl.BlockSpec((s, HD), lambda h: (0, 2 * N_HEADS + h))],
        out_specs=[full, pl.BlockSpec((HD, s), lambda h: (h, 0)), full],
        out_shape=[_sds((s, N_HEADS * HD), BF), _sds((N_HEADS * HD, s), BF), _sds((s, N_HEADS * HD), F32)],
        scratch_shapes=[pltpu.VMEM((s, HD), F32)] + [pltpu.VMEM((s + DIL_PAD, HD), F32)] * 2 + [pltpu.VMEM((s, HD), F32)] * (2 * npat),
        compiler_params=_cp("parallel"),
    )(qkv, qkv, qkv)


def _dilated_bwd(qkv, out, glse, do, tables, name):
    s = qkv.shape[0]
    chunk = 256

    def body(q_ref, k_ref, v_ref, out_ref, g_ref, do_ref, c_ref, sa_ref, sb_ref, dq_ref, dk_ref, dv_ref,
             qf, kf, vf, dof, dl_s, dq_a, dk_a, dv_a):
        qf[...] = q_ref[...].astype(F32)
        for dst, src in ((kf, k_ref), (vf, v_ref)):
            dst[0:DIL_PAD, :] = jnp.zeros((DIL_PAD, HD), F32)
            dst[DIL_PAD:, :] = src[...].astype(F32)
        for c0 in range(0, s, chunk):
            rows = slice(c0, c0 + chunk)
            dov = do_ref[rows, :].astype(F32)
            dof[rows, :] = dov
            dl_s[rows, :] = jnp.broadcast_to(jnp.sum(dov * out_ref[rows, :].astype(F32), axis=1, keepdims=True), (chunk, HD))
        dq_a[...] = jnp.zeros_like(dq_a)
        dk_a[...] = jnp.zeros_like(dk_a)
        dv_a[...] = jnp.zeros_like(dv_a)
        for _, d in DILATED_PATTERNS:
            nb = s // d // QB

            def grp(g, carry, d=d, nb=nb):
                mc, mp = _dil_masks(g, d, nb)
                q = _dil_load(qf, g, d).astype(BF)
                kc = _dil_load(kf, g, d, DIL_PAD).astype(BF)
                dov = _dil_load(dof, g, d).astype(BF)
                lse = _dil_load(g_ref, g, d)[:, :, 0:1]
                delta = _dil_load(dl_s, g, d)[:, :, 0:1]
                pc = jnp.where(mc, jnp.exp(_bdot_nt(q, kc) * SCALE - lse), 0.0)
                dzc = pc * (_bdot_nt(dov, _dil_load(vf, g, d, DIL_PAD).astype(BF)) - delta) * SCALE
                dq = _bdot(dzc.astype(BF), kc)
                if nb > 1:
                    kp = _dil_load(kf, g, d, DIL_PAD, -1).astype(BF)
                    pp = jnp.where(mp, jnp.exp(_bdot_nt(q, kp) * SCALE - lse), 0.0)
                    dzp = pp * (_bdot_nt(dov, _dil_load(vf, g, d, DIL_PAD, -1).astype(BF)) - delta) * SCALE
                    dq = dq + _bdot(dzp.astype(BF), kp)
                _dil_store(dq_a, g, d, dq, add=True)
                _dil_store(dk_a, g, d, _bdot_tn(dzc, q), DIL_PAD, add=True)
                _dil_store(dv_a, g, d, _bdot_tn(pc, dov), DIL_PAD, add=True)
                if nb > 1:
                    _dil_store(dk_a, g, d, _bdot_tn(dzp, q), DIL_PAD, -1, add=True)
                    _dil_store(dv_a, g, d, _bdot_tn(pp, dov), DIL_PAD, -1, add=True)
                return carry

            lax.fori_loop(0, s // (QB * GB), grp, 0)
        for c0 in range(0, s, chunk):
            rows = slice(c0, c0 + chunk)
            padded = slice(DIL_PAD + c0, DIL_PAD + c0 + chunk)
            c, sa, sb = c_ref[rows, :], sa_ref[rows, :], sb_ref[rows, :]
            dq_ref[rows, :] = _rope(dq_a[rows, :], c, sa, sb).astype(BF)
            dk_ref[rows, :] = _rope(dk_a[padded, :], c, sa, sb).astype(BF)
            dv_ref[rows, :] = dv_a[padded, :].astype(BF)

    full = pl.BlockSpec((s, HD), lambda h: (0, h))
    tab = pl.BlockSpec((s, HD), lambda h: (0, 0))
    return pl.pallas_call(
        body, name=name, grid=(N_HEADS,),
        in_specs=[full, pl.BlockSpec((s, HD), lambda h: (0, N_HEADS + h)), pl.BlockSpec((s, HD), lambda h: (0, 2 * N_HEADS + h)),
                  full, full, full, tab, tab, tab],
        out_specs=[full, full, full], out_shape=[_sds((s, N_HEADS * HD), BF)] * 3,
        scratch_shapes=[pltpu.VMEM((s, HD), F32)] + [pltpu.VMEM((s + DIL_PAD, HD), F32)] * 2 + [pltpu.VMEM((s, HD), F32)] * 3
        + [pltpu.VMEM((s + DIL_PAD, HD), F32)] * 2,
        compiler_params=_cp("parallel"),
    )(qkv, qkv, qkv, out, glse, do, *tables)


def _swiglu_fwd(x, gnorm, w, tag, deps=()):
    h, ht = _rmsnorm_fwd(x, gnorm, f"norm_{tag}", deps)
    g, u, act, act_t = _ffn_up(h, w["gate"], w["up"], f"ffn_up_{tag}")
    y = _ffn_down(act, w["down"], x, f"ffn_down_{tag}")
    return y, (x, ht, g, u, act_t)


def _swiglu_bwd(saved, gnorm, w, dy, dyb_half, out_scale, tag, deps=(), on_down=None, on_grads=None):
    x, ht, g, u, act_t = saved
    dg, du = _ffn_bwd_act(dyb_half, w["down"], g, u, f"ffn_bwd_act_{tag}", deps)
    d_down = _grad_rows(act_t, dyb_half, f"ffn_bwd_wd_{tag}")
    tokens = list(on_down(d_down)) if on_down else []
    d_gate, d_up = _grad_cols(ht, [dg, du], [False, False], f"ffn_bwd_wgu_{tag}", tokens)
    gw = {"gate": d_gate, "up": d_up, "down": d_down}
    tokens = list(on_grads(gw)) if on_grads else []
    dh = _dh_cols([dg, du], [w["gate"], w["up"]], [False, False], f"ffn_bwd_dh_{tag}", tokens)
    dx, dxb, dxbt, dgn = _rmsnorm_bwd(x, gnorm, dh, dy, out_scale, f"norm_bwd_{tag}")
    return (dx, dxb, dxbt), dgn, gw


def kernel(x, norm_g, ffn1_w_gate, ffn1_w_up, ffn1_w_down, ffn2_w_gate, ffn2_w_up, ffn2_w_down, even_w_in, even_b_forget, even_w_out, odd_w_qkv, odd_w_out, final_norm_g, loss_target, m_norm_g, m_ffn1_w_gate, m_ffn1_w_up, m_ffn1_w_down, m_ffn2_w_gate, m_ffn2_w_up, m_ffn2_w_down, m_even_w_in, m_even_b_forget, m_even_w_out, m_odd_w_qkv, m_odd_w_out, m_final_norm_g, v_norm_g, v_ffn1_w_gate, v_ffn1_w_up, v_ffn1_w_down, v_ffn2_w_gate, v_ffn2_w_up, v_ffn2_w_down, v_even_w_in, v_even_b_forget, v_even_w_out, v_odd_w_qkv, v_odd_w_out, v_final_norm_g):
    s, d = x.shape[1], x.shape[2]
    nfox = N_HEADS - N_SB
    ax, ay, ac = lax.axis_index("x"), lax.axis_index("y"), lax.axis_index("c")
    me = 4 * ax + 2 * ay + ac
    slots = jnp.stack([4 * px + 2 * py + ac for px, py in [(ax, ay), (1 - ax, ay), (ax, 1 - ay), (1 - ax, 1 - ay)]]).astype(jnp.int32)
    x0 = x.reshape(s, d)
    target = loss_target.reshape(s, d)

    def bf(w):
        return w.astype(BF)

    groups = [
        [bf(ffn1_w_gate[0]), bf(ffn1_w_up[0]), bf(ffn1_w_down[0]), norm_g.reshape(6, d // NDEV)],
        [bf(even_w_in[0]), bf(even_w_out[0])],
        [bf(ffn2_w_gate[0]), bf(ffn2_w_up[0]), bf(ffn2_w_down[0])],
        [bf(ffn1_w_gate[1]), bf(ffn1_w_up[1]), bf(ffn1_w_down[1])],
        [bf(odd_w_qkv[0]), bf(odd_w_out[0])],
        [bf(ffn2_w_gate[1]), bf(ffn2_w_up[1]), bf(ffn2_w_down[1])],
    ]
    started = []
    for k, grp in enumerate(groups):
        started.append(_gather_start(grp, me, [started[-1]["token"]] if started else [], f"gather_start_{k}"))
    all_started = [started[-1]["token"]]

    def gathered(k, after, early=None):
        return _gather_finish(_gather_forward(started[k], after if early is None else early, f"gather_forward_{k}"), after, f"gather_finish_{k}")

    def ffn_weights(ws_):
        return {"gate": ws_[0], "up": ws_[1], "down": ws_[2]}

    b_pad = jnp.pad(even_b_forget, ((0, 0), (0, HD - nfox)))
    gfin = final_norm_g.reshape(1, d)

    g0 = gathered(0, x0)
    gn = jnp.transpose(g0[3], (1, 0, 2)).reshape(6, 1, d)
    wf = [[ffn_weights(g0), None], [None, None]]
    x1, sv_f1_0 = _swiglu_fwd(x0, gn[0], wf[0][0], "l0a", all_started)
    g1 = gathered(1, x1)
    w_in_nat = jnp.transpose(g1[0], (1, 0, 2)).reshape(d, -1)
    w_qkv_e = w_in_nat[:, :3 * d]
    w_f = jnp.pad(w_in_nat[:, 3 * d:], ((0, 0), (0, HD - nfox)))
    w_out_e = g1[1].reshape(d, d)
    h_e, ht_e = _rmsnorm_fwd(x1, gn[1], "norm_l0m")
    qkv_e = _mm_nn(h_e, w_qkv_e, 768, BF, "even_qkv")
    f_e = _mm_nn(h_e, w_f, HD, F32, "even_fgate")
    o_sb, ot_sb, tot_sb = _sb_fwd(qkv_e, "sb_fwd")
    cum_b, cum_t = _fgate_fwd(f_e, b_pad, "fgate_fwd")
    o_fox, ot_fox, lse_fox = _fox_fwd(qkv_e, cum_b, cum_t, "fox_fwd")
    o_e = jnp.concatenate([o_sb, o_fox], axis=1)
    ot_e = jnp.concatenate([ot_sb, ot_fox], axis=0)
    x2 = _mm_nn(o_e, w_out_e, 1024, F32, "even_out", res=x1)
    wf[0][1] = ffn_weights(gathered(2, x2, early=o_fox))
    x3, sv_f2_0 = _swiglu_fwd(x2, gn[2], wf[0][1], "l0b")

    wf[1][0] = ffn_weights(gathered(3, x3))
    x4, sv_f1_1 = _swiglu_fwd(x3, gn[3], wf[1][0], "l1a")
    g4 = gathered(4, x4, early=sv_f1_1[2])
    w_qkv_o = g4[0]
    w_out_o = g4[1].reshape(d, d)
    h_o, ht_o = _rmsnorm_fwd(x4, gn[4], "norm_l1m")
    qkv_o = _qkv_rope(h_o, w_qkv_o, _rope_tables(s, 1.0), "odd_qkv")
    o_o, ot_o, glse = _dilated_fwd(qkv_o, "dilated_fwd")
    x5 = _mm_nn(o_o, w_out_o, 1024, F32, "odd_out", res=x4)
    wf[1][1] = ffn_weights(gathered(5, x5, early=o_o))
    x6, sv_f2_1 = _swiglu_fwd(x5, gn[5], wf[1][1], "l1b")

    def chip_sums(gs, a_s, tag):
        ps = [_pair_sum(g_, a_, slots, f"pair_sum_{tag}_{k}") for k, (g_, a_) in enumerate(zip(gs, a_s))]
        return gs, a_s, _chip_start(ps, f"chip_start_{tag}")

    def as_slices(gs):
        return [g_ if g_.ndim == 3 else g_.reshape(NDEV, g_.shape[0] // NDEV, g_.shape[1]) for g_ in gs]

    def reduce_start(gs, tag):
        gs = as_slices(gs)
        return chip_sums(gs, _pair_exchange(gs, f"pair_exchange_{tag}"), tag)

    red, crossing = {}, {}

    def cross(gs, tag):
        crossing[tag] = _pair_start(as_slices(gs), f"pair_start_{tag}")
        return [crossing[tag]["token"]]

    def reduce_behind_dh(tag):
        return lambda gw: cross([gw["gate"], gw["up"], gw["down"]], tag)

    def reduce_after(tag, after):
        red[tag] = chip_sums(*_pair_finish(crossing[tag], after, f"pair_finish_{tag}"), tag)
        return [red[tag][2]["token"]]

    def reduce_now(tag, names):
        def hook(gw):
            red[tag] = reduce_start([gw[nm] for nm in names] if names else [gw], tag)
            return [red[tag][2]["token"]]
        return hook

    dx6, dx6b, _, d_gfin, loss_part = _loss_head(x6, gfin, target, "loss_head")

    (dx5, dx5b, dx5bt), dgn5, _ = _swiglu_bwd(sv_f2_1, gn[5], wf[1][1], dx6, dx6b, 1.0, "l1b", on_grads=reduce_behind_dh("l1b"))
    d_wout_o = _mm_nn(ot_o, dx5b, 1024, BF, "odd_out_dw")
    do_o = _mm_nt([(dx5b, w_out_o)], "odd_out_do", BF, deps=reduce_after("l1b", dx5))
    dqkv_o = jnp.concatenate(_dilated_bwd(qkv_o, o_o, glse, do_o, _rope_tables(s, -1.0), "dilated_bwd"), axis=1)
    (d_wqkv_o,) = _grad_cols(ht_o, [dqkv_o], [True], "odd_qkv_dw")
    dh_o = _dh_cols([dqkv_o], [w_qkv_o], [True], "odd_qkv_dh", cross([d_wqkv_o, d_wout_o], "l1m"))
    dx4, dx4b, _, dgn4 = _rmsnorm_bwd(x4, gn[4], dh_o, dx5, 0.5, "norm_bwd_l1m")
    (dx3, dx3b, _), dgn3, _ = _swiglu_bwd(sv_f1_1, gn[3], wf[1][0], dx4, dx4b, 0.5, "l1a", reduce_after("l1m", dx4),
                                         on_grads=reduce_behind_dh("l1a"))

    (dx2, dx2b, dx2bt), dgn2, _ = _swiglu_bwd(sv_f2_0, gn[2], wf[0][1], dx3, dx3b, 1.0, "l0b", reduce_after("l1a", dx3),
                                             on_grads=reduce_behind_dh("l0b"))
    d_wout_e = _mm_nn(ot_e, dx2b, 1024, BF, "even_out_dw")
    do_e = _mm_nt([(dx2b, w_out_e)], "even_out_do", BF, deps=reduce_after("l0b", dx2))
    dq_sb, dk_sb, dv_sb = _sb_bwd(qkv_e, do_e, tot_sb, "sb_bwd")
    dq_fx, dk_fx, dv_fx, dcq, dck = _fox_bwd(qkv_e, cum_b, cum_t, o_fox, lse_fox, do_e, "fox_bwd")
    df, db_part = _fgate_bwd(dcq, dck, f_e, b_pad, "fgate_bwd")
    dqkv_e = jnp.concatenate([dq_sb, dq_fx, dk_sb, dk_fx, dv_sb, dv_fx], axis=1)
    d_wqkv_e = _mm_nn(ht_e, dqkv_e, 768, BF, "even_qkv_dw")
    d_wf = _mm_nn(ht_e, df, HD, BF, "even_fgate_dw")
    d_win_nat = jnp.concatenate([d_wqkv_e, d_wf[:, :nfox]], axis=1)
    d_win = jnp.transpose(d_win_nat.reshape(d, NDEV, -1), (1, 0, 2))
    dh_e = _mm_nt([(dqkv_e, w_qkv_e), (df, w_f)], "even_in_dh", deps=cross([d_win, d_wout_e], "l0m"))
    dx1, dx1b, _, dgn1 = _rmsnorm_bwd(x1, gn[1], dh_e, dx2, 0.5, "norm_bwd_l0m")
    (dx0, _, _), dgn0, _ = _swiglu_bwd(sv_f1_0, gn[0], wf[0][0], dx1, dx1b, 1.0, "l0a", reduce_after("l0m", dx1),
                                      on_down=reduce_now("l0a_down", None), on_grads=reduce_now("l0a_gu", ["gate", "up"]))

    def reduce_finish(red, tag, after):
        gs, a_s, st = red
        return list(zip(gs, a_s, _chip_finish(st, after, f"chip_finish_{tag}")))

    f_l1b, f_l1m, f_l1a = (reduce_finish(red[t], t, dx0) for t in ("l1b", "l1m", "l1a"))
    f_l0b, f_l0m = (reduce_finish(red[t], t, dx0) for t in ("l0b", "l0m"))

    def update(w_, m_, v_, parts, nm):
        if w_.shape[2] % 128 == 0:
            return _adamw_sharded(w_, m_, v_, parts, slots, f"adamw_{nm}")
        outs = _adamw_sharded(jnp.swapaxes(w_, 1, 2), jnp.swapaxes(m_, 1, 2), jnp.swapaxes(v_, 1, 2), parts, slots,
                              f"adamw_{nm}", transposed=True)
        return [jnp.swapaxes(o, 1, 2) for o in outs]

    res = {}
    res["even_w_in"] = update(even_w_in, m_even_w_in, v_even_w_in, [f_l0m[0]], "even_w_in")
    res["even_w_out"] = _adamw_sharded(even_w_out, m_even_w_out, v_even_w_out, [f_l0m[1]], slots, "adamw_even_w_out")
    res["odd_w_qkv"] = _adamw_sharded(odd_w_qkv, m_odd_w_qkv, v_odd_w_qkv, [f_l1m[0]], slots, "adamw_odd_w_qkv")
    res["odd_w_out"] = _adamw_sharded(odd_w_out, m_odd_w_out, v_odd_w_out, [f_l1m[1]], slots, "adamw_odd_w_out")
    names = ["ffn2_w_gate", "ffn2_w_up", "ffn2_w_down", "ffn1_w_gate", "ffn1_w_up", "ffn1_w_down"]
    ws = [ffn2_w_gate, ffn2_w_up, ffn2_w_down, ffn1_w_gate, ffn1_w_up, ffn1_w_down]
    ms = [m_ffn2_w_gate, m_ffn2_w_up, m_ffn2_w_down, m_ffn1_w_gate, m_ffn1_w_up, m_ffn1_w_down]
    vs = [v_ffn2_w_gate, v_ffn2_w_up, v_ffn2_w_down, v_ffn1_w_gate, v_ffn1_w_up, v_ffn1_w_down]
    for k in range(3):
        res[names[k]] = update(ws[k], ms[k], vs[k], [f_l0b[k], f_l1b[k]], names[k])
    f_l0a = (reduce_finish(red["l0a_gu"], "l0a_gu", res["ffn2_w_down"][1])
             + reduce_finish(red["l0a_down"], "l0a_down", res["ffn2_w_down"][1]))
    for k in range(3, 6):
        res[names[k]] = update(ws[k], ms[k], vs[k], [f_l0a[k - 3], f_l1a[k - 3]], names[k])

    dnorm = jnp.concatenate([dgn0, dgn1, dgn2, dgn3, dgn4, dgn5], axis=0)
    nsm = d // NDEV
    small_rows = (6 * d + d + 2 * HD) // HD
    pad_rows = -small_rows % 8
    part = jnp.concatenate([dnorm.reshape(-1), d_gfin.reshape(-1), db_part.reshape(-1), loss_part.reshape(-1),
                            jnp.zeros((pad_rows * HD,), F32)]).reshape(small_rows + pad_rows, HD)
    (gathered,) = _all_gather([part], "gather_small")

    def pack(ng, bfg, fg):
        full = lax.dynamic_update_slice(jnp.zeros((6, d), F32), ng.reshape(6, nsm), (0, me * nsm))
        return jnp.concatenate([full.reshape(-1), fg.reshape(-1), jnp.pad(bfg.reshape(-1), (0, HD - nfox)),
                                jnp.zeros((HD + pad_rows * HD,), F32)]).reshape(small_rows + pad_rows, HD)

    sm = _adamw_small(pack(norm_g, even_b_forget, final_norm_g), pack(m_norm_g, m_even_b_forget, m_final_norm_g),
                      pack(v_norm_g, v_even_b_forget, v_final_norm_g), gathered, "adamw_small")

    def unpack(t):
        flat = t.reshape(-1)
        ng = lax.dynamic_slice(flat[:6 * d].reshape(6, d), (0, me * nsm), (6, nsm)).reshape(norm_g.shape)
        fg = flat[6 * d:7 * d].reshape(final_norm_g.shape)
        bfg = flat[7 * d:7 * d + nfox].reshape(even_b_forget.shape)
        return ng, bfg, fg

    sm_g, sm_d, sm_m, sm_v = [unpack(t) for t in sm]
    loss = sm[0].reshape(-1)[7 * d + HD]

    order = ["norm_g", "ffn1_w_gate", "ffn1_w_up", "ffn1_w_down", "ffn2_w_gate", "ffn2_w_up", "ffn2_w_down", "even_w_in",
             "even_b_forget", "even_w_out", "odd_w_qkv", "odd_w_out", "final_norm_g"]
    outs = [loss, dx0.reshape(x.shape)]
    for k in range(4):
        smk = [sm_g, sm_d, sm_m, sm_v][k]
        for nm in order:
            if nm == "norm_g":
                outs.append(smk[0])
            elif nm == "even_b_forget":
                outs.append(smk[1])
            elif nm == "final_norm_g":
                outs.append(smk[2])
            else:
                outs.append(res[nm][k])
    return tuple(outs)
```

```python
import functools

import jax
import jax.numpy as jnp
import numpy as np
from jax import lax
from jax.experimental import pallas as pl
from jax.experimental.pallas import tpu as pltpu

F32 = jnp.float32
BF = jnp.bfloat16
NDEV = 8
HD = 128
QB = 128
N_HEADS = 16
N_SB = 8
SCALE = HD ** -0.5
ROPE_THETA = 500000.0
ROPE_DIMS = HD // 4
DILATED_PATTERNS = ((128, 1), (512, 4), (2048, 16))
RMS_EPS = 1e-6
NEG_INF = -1e30
ADAM_LR = 0.001
ADAM_B1 = 0.9
ADAM_B2 = 0.999
ADAM_EPS = 1e-08
ADAM_WD = 0.01
ADAM_STEP = 10
VMEM_LIMIT_V7X = 56 * 1024 * 1024
MESH = pl.DeviceIdType.MESH
ANY = pl.BlockSpec(memory_space=pl.ANY)

NT_DIMS = (((1,), (1,)), ((), ()))


def _cp(*dims):
    return pltpu.CompilerParams(dimension_semantics=dims if dims else None, vmem_limit_bytes=VMEM_LIMIT_V7X)


def _dot(a, b):
    return jnp.dot(a, b, preferred_element_type=F32)


def _dot_nt(a, b):
    return lax.dot_general(a, b, NT_DIMS, preferred_element_type=F32)


def _sds(shape, dtype):
    return jax.ShapeDtypeStruct(shape, dtype)


def _place():
    x, y, c = lax.axis_index("x"), lax.axis_index("y"), lax.axis_index("c")
    chips = [(x, y), (1 - x, y), (x, 1 - y), (1 - x, 1 - y)]
    return x, y, c, chips


def _all_gather(xs, name):
    n = len(xs)

    def body(*refs):
        x_refs, out_refs = refs[:n], refs[n:2 * n]
        send_sems, recv_sems, local_sems = refs[2 * n:]
        x, y, c, chips = _place()
        me, sibling = (x, y, c), (x, y, 1 - c)
        others = chips[1:]

        def slot(a, px, py, pc):
            return out_refs[a].at[4 * px + 2 * py + pc]

        def copy(a, k, block, to, src=None):
            return pltpu.make_async_remote_copy(
                src_ref=slot(a, *block) if src is None else src, dst_ref=slot(a, *block),
                send_sem=send_sems.at[a, k], recv_sem=recv_sems.at[a, k], device_id=to, device_id_type=MESH)

        started = []
        for a in range(n):
            mine = pltpu.make_async_copy(x_refs[a], slot(a, *me), local_sems.at[a])
            mine.start()
            first = [copy(a, 0, me, sibling, src=x_refs[a])]
            first += [copy(a, 1 + j, me, (*chip, c), src=x_refs[a]) for j, chip in enumerate(others)]
            for cp in first:
                cp.start()
            started += [mine.wait] + [cp.wait_send for cp in first]
        for a in range(n):
            for j, chip in enumerate(others):
                copy(a, 1 + j, (*chip, c), me).wait_recv()
                passed = copy(a, 4 + j, (*chip, c), sibling)
                passed.start()
                started.append(passed.wait_send)
        for a in range(n):
            copy(a, 0, sibling, me).wait_recv()
            for j, chip in enumerate(others):
                copy(a, 4 + j, (*chip, 1 - c), me).wait_recv()
        for w in started:
            w()

    return pl.pallas_call(
        body, name=name,
        out_shape=[_sds((NDEV,) + x.shape, x.dtype) for x in xs],
        in_specs=[ANY] * n, out_specs=[ANY] * n,
        scratch_shapes=[pltpu.SemaphoreType.DMA((n, 7)), pltpu.SemaphoreType.DMA((n, 7)), pltpu.SemaphoreType.DMA((n,))],
    )(*xs)


def _pair_exchange(gs, name):
    n = len(gs)

    def body(*refs):
        g_refs, a_refs = refs[:n], refs[n:2 * n]
        send_sems, recv_sems = refs[2 * n:]
        x, y, c, chips = _place()
        copies = []
        for a in range(n):
            for j, (px, py) in enumerate(chips):
                copies.append(pltpu.make_async_remote_copy(
                    src_ref=g_refs[a].at[4 * px + 2 * py + (1 - c)], dst_ref=a_refs[a].at[j],
                    send_sem=send_sems.at[a, j], recv_sem=recv_sems.at[a, j],
                    device_id=(x, y, 1 - c), device_id_type=MESH))
        for cp in copies:
            cp.start()
        for cp in copies:
            cp.wait()

    return pl.pallas_call(
        body, name=name,
        out_shape=[_sds((4,) + g.shape[1:], g.dtype) for g in gs],
        in_specs=[ANY] * n, out_specs=[ANY] * n,
        scratch_shapes=[pltpu.SemaphoreType.DMA((n, 4)), pltpu.SemaphoreType.DMA((n, 4))],
    )(*gs)


HBM = pl.BlockSpec(memory_space=pltpu.HBM)
SEM = pl.BlockSpec(memory_space=pltpu.SEMAPHORE)
EFFECT = pltpu.SideEffectType.DATAFLOW_SIDE_EFFECTING
TOKEN = _sds((8, 128), F32)
TOKEN_SPEC = pl.BlockSpec((8, 128), lambda *_: (0, 0))


def _in_hbm(x):
    return pltpu.with_memory_space_constraint(x, pltpu.HBM)


def _ignore_deps(body, n_in, n_deps):
    if not n_deps:
        return body
    return lambda *refs: body(*refs[:n_in], *refs[n_in + n_deps:])


def _slot_of(px, py, pc):
    return 4 * px + 2 * py + pc


def _gather_start(xs, me, deps, name):
    n = len(xs)
    lands = [lax.dynamic_update_slice(lax.empty((NDEV,) + x.shape, x.dtype), x[None], (me,) + (0,) * x.ndim) for x in xs]

    def body(*refs):
        x_refs, land_refs = refs[:n], refs[n:2 * n]
        send, recv_ici, recv_sib = refs[2 * n:2 * n + 3]
        token = refs[4 * n + 3]
        x, y, c, chips = _place()
        for a in range(n):
            dst = land_refs[a].at[_slot_of(x, y, c)]
            pltpu.make_async_remote_copy(src_ref=x_refs[a], dst_ref=dst, send_sem=send.at[4 * a], recv_sem=recv_sib.at[a],
                                         device_id=(x, y, 1 - c), device_id_type=MESH).start()
            for j, chip in enumerate(chips[1:]):
                pltpu.make_async_remote_copy(src_ref=x_refs[a], dst_ref=dst, send_sem=send.at[4 * a + 1 + j], recv_sem=recv_ici.at[3 * a + j],
                                             device_id=(*chip, c), device_id_type=MESH).start()
        token[...] = jnp.zeros_like(token)

    outs = pl.pallas_call(
        _ignore_deps(body, 2 * n, len(deps)), name=name,
        out_shape=(pltpu.SemaphoreType.DMA((4 * n,)), pltpu.SemaphoreType.DMA((3 * n,)), pltpu.SemaphoreType.DMA((n,)),
                   *[pltpu.HBM(x.shape, x.dtype) for x in xs], *[pltpu.HBM(l.shape, l.dtype) for l in lands], TOKEN),
        in_specs=[HBM] * (2 * n) + [TOKEN_SPEC] * len(deps),
        out_specs=(SEM, SEM, SEM, *[HBM] * (2 * n), pl.BlockSpec(memory_space=pltpu.VMEM)),
        input_output_aliases={a: 3 + a for a in range(2 * n)},
        compiler_params=pltpu.CompilerParams(has_side_effects=EFFECT),
    )(*[_in_hbm(x) for x in xs], *[_in_hbm(l) for l in lands], *deps)
    send, recv_ici, recv_sib = outs[:3]
    return dict(send=send, recv_ici=recv_ici, recv_sib=recv_sib, xs=list(outs[3:3 + n]), lands=list(outs[3 + n:3 + 2 * n]), token=outs[-1])


def _gather_forward(st, after, name):
    n = len(st["lands"])

    def body(*refs):
        land_refs, recv_ici = refs[:n], refs[n]
        send2, recv2 = refs[n + 2], refs[n + 3]
        x, y, c, chips = _place()
        for a in range(n):
            for j, chip in enumerate(chips[1:]):
                blk = land_refs[a].at[_slot_of(*chip, c)]
                pltpu.make_async_remote_copy(src_ref=blk, dst_ref=blk, send_sem=send2.at[3 * a + j], recv_sem=recv_ici.at[3 * a + j],
                                             device_id=(*chip, c), device_id_type=MESH).wait_recv()
                pltpu.make_async_remote_copy(src_ref=blk, dst_ref=blk, send_sem=send2.at[3 * a + j], recv_sem=recv2.at[3 * a + j],
                                             device_id=(x, y, 1 - c), device_id_type=MESH).start()

    outs = pl.pallas_call(
        body, name=name,
        out_shape=(pltpu.SemaphoreType.DMA((3 * n,)), pltpu.SemaphoreType.DMA((3 * n,)), *[pltpu.HBM(l.shape, l.dtype) for l in st["lands"]]),
        in_specs=[HBM] * n + [SEM, pl.BlockSpec(memory_space=pl.ANY)],
        out_specs=(SEM, SEM, *[HBM] * n),
        input_output_aliases={a: 2 + a for a in range(n)},
        compiler_params=pltpu.CompilerParams(has_side_effects=EFFECT),
    )(*st["lands"], st["recv_ici"], after)
    return dict(st, send2=outs[0], recv2=outs[1], lands=list(outs[2:]))


def _gather_finish(st, after, name):
    n = len(st["lands"])

    def body(*refs):
        x_refs, land_refs = refs[:n], refs[n:2 * n]
        send, recv_sib, send2, recv2 = refs[2 * n:2 * n + 4]
        x, y, c, chips = _place()
        for a in range(n):
            mine = land_refs[a].at[_slot_of(x, y, c)]
            theirs = land_refs[a].at[_slot_of(x, y, 1 - c)]
            for k in range(4):
                pltpu.make_async_remote_copy(src_ref=x_refs[a], dst_ref=mine, send_sem=send.at[4 * a + k], recv_sem=recv_sib.at[a],
                                             device_id=(x, y, 1 - c), device_id_type=MESH).wait_send()
            pltpu.make_async_remote_copy(src_ref=x_refs[a], dst_ref=theirs, send_sem=send.at[4 * a], recv_sem=recv_sib.at[a],
                                         device_id=(x, y, 1 - c), device_id_type=MESH).wait_recv()
            for j, chip in enumerate(chips[1:]):
                sent = land_refs[a].at[_slot_of(*chip, c)]
                got = land_refs[a].at[_slot_of(*chip, 1 - c)]
                pltpu.make_async_remote_copy(src_ref=sent, dst_ref=sent, send_sem=send2.at[3 * a + j], recv_sem=recv2.at[3 * a + j],
                                             device_id=(x, y, 1 - c), device_id_type=MESH).wait_send()
                pltpu.make_async_remote_copy(src_ref=got, dst_ref=got, send_sem=send2.at[3 * a + j], recv_sem=recv2.at[3 * a + j],
                                             device_id=(x, y, 1 - c), device_id_type=MESH).wait_recv()

    outs = pl.pallas_call(
        body, name=name,
        out_shape=tuple(pltpu.HBM(v.shape, v.dtype) for v in st["xs"] + st["lands"]),
        in_specs=[HBM] * (2 * n) + [SEM] * 4 + [pl.BlockSpec(memory_space=pl.ANY)], out_specs=tuple([HBM] * (2 * n)),
        input_output_aliases={a: a for a in range(2 * n)},
        compiler_params=pltpu.CompilerParams(has_side_effects=EFFECT),
    )(*st["xs"], *st["lands"], st["send"], st["recv_sib"], st["send2"], st["recv2"], after)
    return list(outs[n:])


def _pair_start(gs, name):
    n = len(gs)
    lands = [lax.empty((4,) + g.shape[1:], g.dtype) for g in gs]

    def body(*refs):
        g_refs, a_refs = refs[:n], refs[n:2 * n]
        send, recv = refs[2 * n], refs[2 * n + 1]
        token = refs[4 * n + 2]
        x, y, c, chips = _place()
        for a in range(n):
            for j, (px, py) in enumerate(chips):
                pltpu.make_async_remote_copy(src_ref=g_refs[a].at[_slot_of(px, py, 1 - c)], dst_ref=a_refs[a].at[j],
                                             send_sem=send.at[4 * a + j], recv_sem=recv.at[4 * a + j],
                                             device_id=(x, y, 1 - c), device_id_type=MESH).start()
        token[...] = jnp.zeros_like(token)

    outs = pl.pallas_call(
        body, name=name,
        out_shape=(pltpu.SemaphoreType.DMA((4 * n,)), pltpu.SemaphoreType.DMA((4 * n,)),
                   *[pltpu.HBM(g.shape, g.dtype) for g in gs], *[pltpu.HBM(l.shape, l.dtype) for l in lands], TOKEN),
        in_specs=[HBM] * (2 * n), out_specs=(SEM, SEM, *[HBM] * (2 * n), pl.BlockSpec(memory_space=pltpu.VMEM)),
        input_output_aliases={a: 2 + a for a in range(2 * n)},
        compiler_params=pltpu.CompilerParams(has_side_effects=EFFECT),
    )(*[_in_hbm(g) for g in gs], *[_in_hbm(l) for l in lands])
    return dict(send=outs[0], recv=outs[1], gs=list(outs[2:2 + n]), lands=list(outs[2 + n:2 + 2 * n]), token=outs[-1])


def _pair_finish(st, after, name):
    n = len(st["gs"])

    def body(*refs):
        g_refs, a_refs = refs[:n], refs[n:2 * n]
        send, recv = refs[2 * n], refs[2 * n + 1]
        x, y, c, chips = _place()
        for a in range(n):
            for j, (px, py) in enumerate(chips):
                cp = pltpu.make_async_remote_copy(src_ref=g_refs[a].at[_slot_of(px, py, 1 - c)], dst_ref=a_refs[a].at[j],
                                                  send_sem=send.at[4 * a + j], recv_sem=recv.at[4 * a + j],
                                                  device_id=(x, y, 1 - c), device_id_type=MESH)
                cp.wait_send()
                cp.wait_recv()

    outs = pl.pallas_call(
        body, name=name,
        out_shape=tuple(pltpu.HBM(v.shape, v.dtype) for v in st["gs"] + st["lands"]),
        in_specs=[HBM] * (2 * n) + [SEM, SEM, pl.BlockSpec(memory_space=pl.ANY)], out_specs=tuple([HBM] * (2 * n)),
        input_output_aliases={a: a for a in range(2 * n)},
        compiler_params=pltpu.CompilerParams(has_side_effects=EFFECT),
    )(*st["gs"], *st["lands"], st["send"], st["recv"], after)
    return list(outs[:n]), list(outs[n:])


def _chip_start(ps, name):
    n = len(ps)
    lands = [lax.empty(p.shape, p.dtype) for p in ps]

    def body(*refs):
        p_refs, b_refs = refs[:n], refs[n:2 * n]
        send, recv = refs[2 * n], refs[2 * n + 1]
        token = refs[4 * n + 2]
        x, y, c, chips = _place()
        for a in range(n):
            for j, chip in enumerate(chips[1:]):
                pltpu.make_async_remote_copy(src_ref=p_refs[a].at[j], dst_ref=b_refs[a].at[j], send_sem=send.at[3 * a + j], recv_sem=recv.at[3 * a + j],
                                             device_id=(*chip, c), device_id_type=MESH).start()
        token[...] = jnp.zeros_like(token)

    outs = pl.pallas_call(
        body, name=name,
        out_shape=(pltpu.SemaphoreType.DMA((3 * n,)), pltpu.SemaphoreType.DMA((3 * n,)),
                   *[pltpu.HBM(p.shape, p.dtype) for p in ps], *[pltpu.HBM(p.shape, p.dtype) for p in ps], TOKEN),
        in_specs=[HBM] * (2 * n), out_specs=(SEM, SEM, *[HBM] * (2 * n), pl.BlockSpec(memory_space=pltpu.VMEM)),
        input_output_aliases={a: 2 + a for a in range(2 * n)},
        compiler_params=pltpu.CompilerParams(has_side_effects=EFFECT),
    )(*[_in_hbm(p) for p in ps], *[_in_hbm(l) for l in lands])
    return dict(send=outs[0], recv=outs[1], ps=list(outs[2:2 + n]), lands=list(outs[2 + n:2 + 2 * n]), token=outs[-1])


def _chip_finish(st, after, name):
    n = len(st["ps"])

    def body(*refs):
        p_refs, b_refs = refs[:n], refs[n:2 * n]
        send, recv = refs[2 * n], refs[2 * n + 1]
        x, y, c, chips = _place()
        for a in range(n):
            for j, chip in enumerate(chips[1:]):
                cp = pltpu.make_async_remote_copy(src_ref=p_refs[a].at[j], dst_ref=b_refs[a].at[j], send_sem=send.at[3 * a + j], recv_sem=recv.at[3 * a + j],
                                                  device_id=(*chip, c), device_id_type=MESH)
                cp.wait_send()
                cp.wait_recv()

    outs = pl.pallas_call(
        body, name=name,
        out_shape=tuple(pltpu.HBM(v.shape, v.dtype) for v in st["ps"] + st["lands"]),
        in_specs=[HBM] * (2 * n) + [SEM, SEM, pl.BlockSpec(memory_space=pl.ANY)], out_specs=tuple([HBM] * (2 * n)),
        input_output_aliases={a: a for a in range(2 * n)},
        compiler_params=pltpu.CompilerParams(has_side_effects=EFFECT),
    )(*st["ps"], *st["lands"], st["send"], st["recv"], after)
    return list(outs[n:])


def _rows_tile(r):
    for t in (512, 256, 128, 64, 32, 16):
        if r % t == 0:
            return t
    return r


PAIR_SUM_STEPS = 4


def _pair_sum(gs, a_s, slots, name):
    n = len(gs)
    trs = [g.shape[1] // PAIR_SUM_STEPS for g in gs]

    def body(slots_ref, *refs):
        for g_ref, a_ref, p_ref in zip(refs[:n], refs[n:2 * n], refs[2 * n:]):
            p_ref[...] = (g_ref[...].astype(F32) + a_ref[...].astype(F32)).astype(BF)

    def spec(g, tr, index):
        return pl.BlockSpec((None, tr, g.shape[2]), index)

    return pl.pallas_call(
        body, name=name,
        grid_spec=pltpu.PrefetchScalarGridSpec(
            num_scalar_prefetch=1, grid=(3, PAIR_SUM_STEPS),
            in_specs=[spec(g, tr, lambda j, i, s: (s[j + 1], i, 0)) for g, tr in zip(gs, trs)]
            + [spec(g, tr, lambda j, i, s: (j + 1, i, 0)) for g, tr in zip(gs, trs)],
            out_specs=[spec(g, tr, lambda j, i, s: (j, i, 0)) for g, tr in zip(gs, trs)]),
        out_shape=[_sds((3,) + g.shape[1:], BF) for g in gs], compiler_params=_cp("parallel", "parallel"),
    )(slots, *gs, *a_s)


def _adamw_math(w, g, m, v):
    m = ADAM_B1 * m + (1.0 - ADAM_B1) * g
    v = ADAM_B2 * v + (1.0 - ADAM_B2) * (g * g)
    m_hat = m / (1.0 - ADAM_B1 ** ADAM_STEP)
    v_hat = v / (1.0 - ADAM_B2 ** ADAM_STEP)
    delta = -ADAM_LR * (m_hat / (jnp.sqrt(v_hat) + ADAM_EPS) + ADAM_WD * w)
    return delta, m, v


def _adamw_sharded(w, m, v, parts, slots, name, transposed=False):
    nl = w.shape[0]
    r, c = parts[0][0].shape[1:]
    tr = _rows_tile(r)
    if c * tr * 4 > (1 << 21) and not transposed:
        tr = max(8, tr // 2)

    def body(slots_ref, w_ref, m_ref, v_ref, *rest):
        part_refs, (g_out, d_out, m_out, v_out) = rest[:5 * nl], rest[5 * nl:]
        layer = pl.program_id(0)
        g = None
        for l in range(nl):
            s = part_refs[5 * l][...].astype(F32)
            for ref in part_refs[5 * l + 1:5 * l + 5]:
                s = s + ref[...].astype(F32)
            g = s if g is None else jnp.where(layer == l, s, g)
        if transposed:
            g = g.T
        delta, mn, vn = _adamw_math(w_ref[...], g, m_ref[...], v_ref[...])
        g_out[...] = g
        d_out[...] = delta
        m_out[...] = mn
        v_out[...] = vn

    def own(l):
        return lambda L, i, s: (s[0], jnp.where(L == l, i, 0), 0)

    def fixed(l, k):
        return lambda L, i, s: (k, jnp.where(L == l, i, 0), 0)

    if transposed:
        wspec = pl.BlockSpec((None, c, tr), lambda L, i, s: (L, 0, i))
    else:
        wspec = pl.BlockSpec((None, tr, c), lambda L, i, s: (L, i, 0))
    in_specs = [wspec, wspec, wspec]
    args = [w, m, v]
    for l, (g, a, b) in enumerate(parts):
        in_specs += [pl.BlockSpec((None, tr, c), own(l)), pl.BlockSpec((None, tr, c), fixed(l, 0)),
                     pl.BlockSpec((None, tr, c), fixed(l, 0)), pl.BlockSpec((None, tr, c), fixed(l, 1)),
                     pl.BlockSpec((None, tr, c), fixed(l, 2))]
        args += [g, a, b, b, b]
    return pl.pallas_call(
        body, name=name,
        grid_spec=pltpu.PrefetchScalarGridSpec(
            num_scalar_prefetch=1, grid=(nl, r // tr), in_specs=in_specs, out_specs=[wspec] * 4),
        out_shape=[_sds(w.shape, F32)] * 4, compiler_params=_cp("arbitrary", "arbitrary"),
    )(slots, *args)


def _adamw_small(w, m, v, gathered, name):
    def body(w_ref, m_ref, v_ref, gg_ref, g_out, d_out, m_out, v_out):
        g = gg_ref[0]
        for k in range(1, NDEV):
            g = g + gg_ref[k]
        delta, mn, vn = _adamw_math(w_ref[...], g, m_ref[...], v_ref[...])
        g_out[...] = g
        d_out[...] = delta
        m_out[...] = mn
        v_out[...] = vn

    return pl.pallas_call(body, name=name, out_shape=[_sds(w.shape, F32)] * 4)(w, m, v, gathered)


def _rmsnorm_fwd(x, g, name, deps=()):
    s, d = x.shape
    tm = 256

    def body(x_ref, g_ref, h_ref, ht_ref):
        xf = x_ref[...]
        y = xf * lax.rsqrt(jnp.mean(xf * xf, axis=-1, keepdims=True) + RMS_EPS)
        h = y * g_ref[...]
        h_ref[...] = h.astype(BF)
        ht_ref[...] = h.T.astype(BF)

    return pl.pallas_call(
        _ignore_deps(body, 2, len(deps)), name=name, grid=(s // tm,),
        in_specs=[pl.BlockSpec((tm, d), lambda i: (i, 0)), pl.BlockSpec((1, d), lambda i: (0, 0))] + [TOKEN_SPEC] * len(deps),
        out_specs=[pl.BlockSpec((tm, d), lambda i: (i, 0)), pl.BlockSpec((d, tm), lambda i: (0, i))],
        out_shape=[_sds((s, d), BF), _sds((d, s), BF)], compiler_params=_cp("parallel"),
    )(x, g, *deps)


def _rmsnorm_bwd(x, g, dh, dres, out_scale, name):
    s, d = x.shape
    tm = 256

    def body(x_ref, g_ref, dh_ref, dres_ref, dx_ref, dxb_ref, dxbt_ref, dg_ref):
        xf = x_ref[...]
        r = lax.rsqrt(jnp.mean(xf * xf, axis=-1, keepdims=True) + RMS_EPS)
        xhat = xf * r
        dhv = dh_ref[...]
        dxhat = dhv * g_ref[...]
        dx = dres_ref[...] + r * (dxhat - xhat * jnp.mean(dxhat * xhat, axis=-1, keepdims=True))
        dx_ref[...] = dx
        scaled = dx * out_scale
        dxb_ref[...] = scaled.astype(BF)
        dxbt_ref[...] = scaled.T.astype(BF)

        @pl.when(pl.program_id(0) == 0)
        def _():
            dg_ref[...] = jnp.zeros_like(dg_ref)

        dg_ref[...] += jnp.sum(dhv * xhat, axis=0, keepdims=True)

    row = pl.BlockSpec((tm, d), lambda i: (i, 0))
    vec = pl.BlockSpec((1, d), lambda i: (0, 0))
    return pl.pallas_call(
        body, name=name, grid=(s // tm,),
        in_specs=[row, vec, row, row],
        out_specs=[row, row, pl.BlockSpec((d, tm), lambda i: (0, i)), vec],
        out_shape=[_sds((s, d), F32), _sds((s, d), BF), _sds((d, s), BF), _sds((1, d), F32)],
        compiler_params=_cp("arbitrary"),
    )(x, g, dh, dres)


def _loss_head(x, g, target, name):
    s, d = x.shape
    tm = 256

    def body(x_ref, g_ref, t_ref, dx_ref, dxb_ref, dxbt_ref, dg_ref, loss_ref):
        xf = x_ref[...]
        r = lax.rsqrt(jnp.mean(xf * xf, axis=-1, keepdims=True) + RMS_EPS)
        xhat = xf * r
        err = xhat * g_ref[...] - t_ref[...]
        dy = err * (1.0 / d)
        dxhat = dy * g_ref[...]
        dx = r * (dxhat - xhat * jnp.mean(dxhat * xhat, axis=-1, keepdims=True))
        dx_ref[...] = dx
        half = dx * 0.5
        dxb_ref[...] = half.astype(BF)
        dxbt_ref[...] = half.T.astype(BF)

        @pl.when(pl.program_id(0) == 0)
        def _():
            dg_ref[...] = jnp.zeros_like(dg_ref)
            loss_ref[...] = jnp.zeros_like(loss_ref)

        dg_ref[...] += jnp.sum(dy * xhat, axis=0, keepdims=True)
        part = 0.5 * jnp.sum(jnp.mean(err * err, axis=-1, keepdims=True), axis=0, keepdims=True)
        lane = lax.broadcasted_iota(jnp.int32, (1, 128), 1)
        loss_ref[...] += jnp.where(lane == 0, part, 0.0)

    row = pl.BlockSpec((tm, d), lambda i: (i, 0))
    vec = pl.BlockSpec((1, d), lambda i: (0, 0))
    return pl.pallas_call(
        body, name=name, grid=(s // tm,),
        in_specs=[row, vec, row],
        out_specs=[row, row, pl.BlockSpec((d, tm), lambda i: (0, i)), vec, pl.BlockSpec((1, 128), lambda i: (0, 0))],
        out_shape=[_sds((s, d), F32), _sds((s, d), BF), _sds((d, s), BF), _sds((1, d), F32), _sds((1, 128), F32)],
        compiler_params=_cp("arbitrary"),
    )(x, g, target)


def _act_spec(tm, n, natural, order):
    if natural:
        return pl.BlockSpec((tm, n), (lambda s, i: (i, s)) if order == "si" else (lambda i, s: (i, s)))
    return pl.BlockSpec((None, tm, n), (lambda s, i: (s, i, 0)) if order == "si" else (lambda i, s: (s, i, 0)))


def _act_shape(s, n, natural, dtype):
    return _sds((s, NDEV * n), dtype) if natural else _sds((NDEV, s, n), dtype)


def _ffn_up(h, wg, wu, name):
    s, d = h.shape
    n = wg.shape[2]
    tm = 1024

    def body(h_ref, wg_ref, wu_ref, g_ref, u_ref, a_ref, at_ref):
        hb = h_ref[...]
        g = _dot(hb, wg_ref[...])
        u = _dot(hb, wu_ref[...])
        g_ref[...] = g.astype(BF)
        u_ref[...] = u.astype(BF)
        act = g * jax.nn.sigmoid(g) * u
        a_ref[...] = act.astype(BF)
        at_ref[...] = act.T.astype(BF)

    wsp = pl.BlockSpec((None, d, n), lambda s_, i: (s_, 0, 0))
    blk = _act_spec(tm, n, False, "si")
    return pl.pallas_call(
        body, name=name, grid=(NDEV, s // tm),
        in_specs=[pl.BlockSpec((tm, d), lambda s_, i: (i, 0)), wsp, wsp],
        out_specs=[blk] * 3 + [pl.BlockSpec((None, n, tm), lambda s_, i: (s_, 0, i))],
        out_shape=[_act_shape(s, n, False, BF)] * 3 + [_sds((NDEV, n, s), BF)],
        compiler_params=_cp("parallel", "parallel"),
    )(h, wg, wu)


def _ffn_down(act, wd, x, name):
    _, s, n = act.shape
    d = wd.shape[2]
    tm = 512

    def body(a_ref, w_ref, x_ref, o_ref, acc):
        k = pl.program_id(1)

        @pl.when(k == 0)
        def _():
            acc[...] = jnp.zeros_like(acc)

        acc[...] += _dot(a_ref[...], w_ref[...])

        @pl.when(k == NDEV - 1)
        def _():
            o_ref[...] = x_ref[...] + 0.5 * acc[...]

    row = pl.BlockSpec((tm, d), lambda i, k: (i, 0))
    return pl.pallas_call(
        body, name=name, grid=(s // tm, NDEV),
        in_specs=[_act_spec(tm, n, False, "is"), pl.BlockSpec((None, n, d), lambda i, k: (k, 0, 0)), row],
        out_specs=row, out_shape=_sds((s, d), F32),
        scratch_shapes=[pltpu.VMEM((tm, d), F32)], compiler_params=_cp("parallel", "arbitrary"),
    )(act, wd, x)


def _ffn_bwd_act(dyb, wd, g, u, name, deps=()):
    s, d = dyb.shape
    n = wd.shape[1]
    tm = 1024

    def body(dy_ref, w_ref, g_ref, u_ref, dg_ref, du_ref):
        dact = _dot_nt(dy_ref[...], w_ref[...])
        gv = g_ref[...].astype(F32)
        uv = u_ref[...].astype(F32)
        sig = jax.nn.sigmoid(gv)
        dg_ref[...] = (dact * uv * (sig * (1.0 + gv * (1.0 - sig)))).astype(BF)
        du_ref[...] = (dact * (gv * sig)).astype(BF)

    blk = _act_spec(tm, n, False, "si")
    return pl.pallas_call(
        _ignore_deps(body, 4, len(deps)), name=name, grid=(NDEV, s // tm),
        in_specs=[pl.BlockSpec((tm, d), lambda s_, i: (i, 0)), pl.BlockSpec((None, n, d), lambda s_, i: (s_, 0, 0)), blk, blk]
        + [TOKEN_SPEC] * len(deps),
        out_specs=[blk, blk], out_shape=[_act_shape(s, n, False, BF)] * 2,
        compiler_params=_cp("parallel", "parallel"),
    )(dyb, wd, g, u, *deps)


def _grad_rows(act_t, dyb, name):
    _, n, s = act_t.shape
    d = dyb.shape[1]
    tn = 2048

    def body(a_ref, dy_ref, o_ref):
        o_ref[...] = _dot(a_ref[...], dy_ref[...]).astype(BF)

    return pl.pallas_call(
        body, name=name, grid=(NDEV, d // tn),
        in_specs=[pl.BlockSpec((None, n, s), lambda k, j: (k, 0, 0)), pl.BlockSpec((s, tn), lambda k, j: (0, j))],
        out_specs=pl.BlockSpec((None, n, tn), lambda k, j: (k, 0, j)), out_shape=_sds((NDEV, n, d), BF),
        compiler_params=_cp("parallel", "parallel"),
    )(act_t, dyb)


def _grad_cols(ht, dxs, naturals, name, deps=()):
    d, s = ht.shape
    k = len(dxs)
    ns = [dx.shape[1] // NDEV if nat else dx.shape[2] for dx, nat in zip(dxs, naturals)]
    td = 1024

    def body(*refs):
        ht_ref, dx_refs, o_refs = refs[0], refs[1:1 + k], refs[1 + k:]
        hv = ht_ref[...]
        for dx_ref, o_ref in zip(dx_refs, o_refs):
            o_ref[...] = _dot(hv, dx_ref[...]).astype(BF)

    def dx_spec(n, nat):
        if nat:
            return pl.BlockSpec((s, n), lambda s_, j: (0, s_))
        return pl.BlockSpec((None, s, n), lambda s_, j: (s_, 0, 0))

    return pl.pallas_call(
        _ignore_deps(body, 1 + k, len(deps)), name=name, grid=(NDEV, d // td),
        in_specs=[pl.BlockSpec((td, s), lambda s_, j: (j, 0))] + [dx_spec(n, nat) for n, nat in zip(ns, naturals)]
        + [TOKEN_SPEC] * len(deps),
        out_specs=[pl.BlockSpec((None, td, n), lambda s_, j: (s_, j, 0)) for n in ns],
        out_shape=[_sds((NDEV, d, n), BF) for n in ns], compiler_params=_cp("parallel", "parallel"),
    )(ht, *dxs, *deps)


def _dh_cols(dxs, ws, naturals, name, deps=()):
    k = len(dxs)
    d = ws[0].shape[1]
    ns = [w.shape[2] for w in ws]
    s = dxs[0].shape[0] if naturals[0] else dxs[0].shape[1]
    tm = 512

    def body(*refs):
        dx_refs, w_refs, o_ref, acc = refs[:k], refs[k:2 * k], refs[2 * k], refs[2 * k + 1]
        j = pl.program_id(1)

        @pl.when(j == 0)
        def _():
            acc[...] = jnp.zeros_like(acc)

        t = _dot_nt(dx_refs[0][...], w_refs[0][...])
        for dx_ref, w_ref in zip(dx_refs[1:], w_refs[1:]):
            t = t + _dot_nt(dx_ref[...], w_ref[...])
        acc[...] += t

        @pl.when(j == NDEV - 1)
        def _():
            o_ref[...] = acc[...]

    return pl.pallas_call(
        _ignore_deps(body, 2 * k, len(deps)), name=name, grid=(s // tm, NDEV),
        in_specs=[_act_spec(tm, n, nat, "is") for n, nat in zip(ns, naturals)]
        + [pl.BlockSpec((None, d, n), lambda i, j: (j, 0, 0)) for n in ns] + [TOKEN_SPEC] * len(deps),
        out_specs=pl.BlockSpec((tm, d), lambda i, j: (i, 0)), out_shape=_sds((s, d), F32),
        scratch_shapes=[pltpu.VMEM((tm, d), F32)], compiler_params=_cp("parallel", "arbitrary"),
    )(*dxs, *ws, *deps)


def _mm_nn(a, b, tn, out_dtype, name, res=None, tm=1024):
    m, k = a.shape
    nn = b.shape[1]

    def body(*refs):
        if res is None:
            a_ref, b_ref, o_ref = refs
            o_ref[...] = _dot(a_ref[...], b_ref[...]).astype(out_dtype)
        else:
            a_ref, b_ref, r_ref, o_ref = refs
            o_ref[...] = (r_ref[...] + _dot(a_ref[...], b_ref[...])).astype(out_dtype)

    osp = pl.BlockSpec((tm, tn), lambda j, i: (i, j))
    in_specs = [pl.BlockSpec((tm, k), lambda j, i: (i, 0)), pl.BlockSpec((k, tn), lambda j, i: (0, j))]
    args = [a, b]
    if res is not None:
        in_specs.append(osp)
        args.append(res)
    return pl.pallas_call(
        body, name=name, grid=(nn // tn, m // tm), in_specs=in_specs, out_specs=osp,
        out_shape=_sds((m, nn), out_dtype), compiler_params=_cp("parallel", "parallel"),
    )(*args)


def _mm_nt(pairs, name, out_dtype=F32, tm=512, tk=512, deps=()):
    m = pairs[0][0].shape[0]
    kk = pairs[0][1].shape[0]
    p = len(pairs)

    def body(*refs):
        o_ref = refs[2 * p]
        t = _dot_nt(refs[0][...], refs[1][...])
        for q in range(1, p):
            t = t + _dot_nt(refs[2 * q][...], refs[2 * q + 1][...])
        o_ref[...] = t.astype(out_dtype)

    in_specs, args = [], []
    for a, b in pairs:
        in_specs += [pl.BlockSpec((tm, a.shape[1]), lambda j, i: (i, 0)), pl.BlockSpec((tk, b.shape[1]), lambda j, i: (j, 0))]
        args += [a, b]
    return pl.pallas_call(
        _ignore_deps(body, 2 * p, len(deps)), name=name, grid=(kk // tk, m // tm), in_specs=in_specs + [TOKEN_SPEC] * len(deps),
        out_specs=pl.BlockSpec((tm, tk), lambda j, i: (i, j)), out_shape=_sds((m, kk), out_dtype),
        compiler_params=_cp("parallel", "parallel"),
    )(*args, *deps)


def _rope_tables(s, sign):
    half = ROPE_DIMS // 2
    f32 = np.float32
    freqs = f32(ROPE_THETA) ** (-np.arange(half, dtype=f32) / f32(half))
    ang = np.arange(s, dtype=f32)[:, None] * freqs[None, :]
    cos, sin = np.cos(ang).astype(f32), (sign * np.sin(ang)).astype(f32)
    one = np.ones((s, HD - ROPE_DIMS), f32)
    zero = np.zeros((s, HD - ROPE_DIMS), f32)
    zh = np.zeros((s, half), f32)
    c = np.concatenate([cos, cos, one], axis=1)
    sa = np.concatenate([-sin, zh, zero], axis=1)
    sb = np.concatenate([zh, sin, zero], axis=1)
    return jnp.asarray(c), jnp.asarray(sa), jnp.asarray(sb)


def _rope(xv, c, sa, sb):
    return xv * c + pltpu.roll(xv, HD - ROPE_DIMS // 2, 1) * sa + pltpu.roll(xv, ROPE_DIMS // 2, 1) * sb


def _qkv_rope(h, w, tables, name):
    s, d = h.shape
    n = w.shape[2]
    per = n // HD
    tm = 1024

    def body(h_ref, w_ref, c_ref, sa_ref, sb_ref, o_ref):
        shard = pl.program_id(0)
        y = _dot(h_ref[...], w_ref[...])
        c, sa, sb = c_ref[...], sa_ref[...], sb_ref[...]
        for j in range(per):
            blk = y[:, j * HD:(j + 1) * HD]
            rot = _rope(blk, c, sa, sb)
            is_qk = shard * per + j < 2 * N_HEADS
            o_ref[:, j * HD:(j + 1) * HD] = jnp.where(is_qk, rot, blk).astype(BF)

    tab = pl.BlockSpec((tm, HD), lambda s_, i: (i, 0))
    return pl.pallas_call(
        body, name=name, grid=(NDEV, s // tm),
        in_specs=[pl.BlockSpec((tm, d), lambda s_, i: (i, 0)), pl.BlockSpec((None, d, n), lambda s_, i: (s_, 0, 0)), tab, tab, tab],
        out_specs=pl.BlockSpec((tm, n), lambda s_, i: (i, s_)), out_shape=_sds((s, NDEV * n), BF),
        compiler_params=_cp("parallel", "parallel"),
    )(h, w, *tables)


def _iota2():
    return (lax.broadcasted_iota(jnp.int32, (QB, QB), 0), lax.broadcasted_iota(jnp.int32, (QB, QB), 1))


def _softplus(z):
    return jnp.maximum(z, 0.0) + jnp.log(1.0 + jnp.exp(-jnp.abs(z)))


def _tri_dot(xv, tri, left=False):
    hi = xv.astype(BF)
    r1 = xv - hi.astype(F32)
    mid = r1.astype(BF)
    lo = (r1 - mid.astype(F32)).astype(BF)
    if left:
        return _dot(tri, hi) + _dot(tri, mid) + _dot(tri, lo)
    return _dot(hi, tri) + _dot(mid, tri) + _dot(lo, tri)


def _col(ref_or_val):
    return ref_or_val[:, 0:1]


KT = 4 * QB
QQ = 2 * QB


def _iota_tile():
    return (lax.broadcasted_iota(jnp.int32, (QQ, KT), 0), lax.broadcasted_iota(jnp.int32, (QQ, KT), 1))


def _scan_matrix(keep):
    tri = keep(*_iota2()).astype(BF)
    return jnp.concatenate([tri, tri], axis=0)


def _scan_dot(xv, tri2):
    hi = xv.astype(BF)
    lo = (xv - hi.astype(F32)).astype(BF)
    return _dot(jnp.concatenate([hi, lo], axis=1), tri2)


def _blocks(xv):
    return [xv[:, b * QB:(b + 1) * QB] for b in range(KT // QB)]


def _sb_fwd(qkv, name):
    s = qkv.shape[0]
    nb = s // QB

    def body(q_ref, k_ref, v_ref, o_ref, ot_ref, t_ref):
        i = pl.program_id(1)
        q = q_ref[...]
        row, col = _iota_tile()
        later_keys = _scan_matrix(lambda j, s_: j > s_)
        last = (i * QQ + QQ - 1) // KT

        def step(tt, carry):
            acc, later = carry
            t = last - tt
            off = pl.multiple_of(t * KT, KT)
            k = k_ref[pl.ds(off, KT), :]
            v = v_ref[pl.ds(off, KT), :]
            z = _dot_nt(q, k) * SCALE
            strict = row + (i * QQ - t * KT) > col
            sp = _softplus(z)
            lnb = jnp.where(strict, -sp, 0.0)
            afters = []
            for xb in reversed(_blocks(lnb)):
                afters.append(later + _scan_dot(xb, later_keys))
                later = later + jnp.sum(xb, axis=1, keepdims=True)
            after = jnp.concatenate(afters[::-1], axis=1)
            w = jnp.where(strict, jnp.exp((z - sp) + after), 0.0)
            return acc + _dot(w.astype(BF), v), later

        acc, total = lax.fori_loop(0, last + 1, step, (jnp.zeros((QQ, HD), F32), jnp.zeros((QQ, 1), F32)))
        o_ref[...] = acc.astype(BF)
        ot_ref[...] = acc.T.astype(BF)
        t_ref[...] = jnp.broadcast_to(total, (QQ, HD))

    blk = pl.BlockSpec((QQ, HD), lambda h, i: (i, h))
    return pl.pallas_call(
        body, name=name, grid=(N_SB, s // QQ),
        in_specs=[blk, pl.BlockSpec((s, HD), lambda h, i: (0, N_HEADS + h)), pl.BlockSpec((s, HD), lambda h, i: (0, 2 * N_HEADS + h))],
        out_specs=[blk, pl.BlockSpec((HD, QQ), lambda h, i: (h, i)), blk],
        out_shape=[_sds((s, N_SB * HD), BF), _sds((N_SB * HD, s), BF), _sds((s, N_SB * HD), F32)],
        compiler_params=_cp("parallel", "parallel"),
    )(qkv, qkv, qkv)


def _sb_bwd(qkv, do, total, name):
    s = qkv.shape[0]
    nb = s // QB

    def body(q_ref, k_ref, v_ref, do_ref, t_ref, dq_ref, dk_ref, dv_ref, dk_acc, dv_acc):
        i = pl.program_id(1)

        @pl.when(i == 0)
        def _():
            dk_acc[...] = jnp.zeros_like(dk_acc)
            dv_acc[...] = jnp.zeros_like(dv_acc)

        q = q_ref[...]
        dov = do_ref[...]
        tot = _col(t_ref[...])
        row, col = _iota_tile()
        keys_upto = _scan_matrix(lambda j, s_: j <= s_)
        keys_before = _scan_matrix(lambda j, s_: j < s_)

        def step(t, carry):
            dq, lnb_before, dl_before = carry
            off = pl.multiple_of(t * KT, KT)
            k = k_ref[pl.ds(off, KT), :]
            v = v_ref[pl.ds(off, KT), :]
            z = _dot_nt(q, k) * SCALE
            strict = row + (i * QQ - t * KT) > col
            sp = _softplus(z)
            lnb = jnp.where(strict, -sp, 0.0)
            afters = []
            for xb in _blocks(lnb):
                afters.append(tot - (lnb_before + _scan_dot(xb, keys_upto)))
                lnb_before = lnb_before + jnp.sum(xb, axis=1, keepdims=True)
            a = jnp.where(strict, jnp.exp((z - sp) + jnp.concatenate(afters, axis=1)), 0.0)
            dl = a * _dot_nt(dov, v)
            befores = []
            for xb in _blocks(dl):
                befores.append(dl_before + _scan_dot(xb, keys_before))
                dl_before = dl_before + jnp.sum(xb, axis=1, keepdims=True)
            sig = jnp.exp(z - sp)
            dz = jnp.where(strict, dl * (1.0 - sig) - sig * jnp.concatenate(befores, axis=1), 0.0) * SCALE
            dq = dq + _dot(dz.astype(BF), k)
            dk_acc[pl.ds(off, KT), :] += _dot(dz.T.astype(BF), q)
            dv_acc[pl.ds(off, KT), :] += _dot(a.T.astype(BF), dov)
            return dq, lnb_before, dl_before

        zero = jnp.zeros((QQ, 1), F32)
        dq, _, _ = lax.fori_loop(0, (i * QQ + QQ - 1) // KT + 1, step, (jnp.zeros((QQ, HD), F32), zero, zero))
        dq_ref[...] = dq.astype(BF)

        @pl.when(i == s // QQ - 1)
        def _():
            dk_ref[...] = dk_acc[...].astype(BF)
            dv_ref[...] = dv_acc[...].astype(BF)

    blk = pl.BlockSpec((QQ, HD), lambda h, i: (i, h))
    full = pl.BlockSpec((s, HD), lambda h, i: (0, h))
    return pl.pallas_call(
        body, name=name, grid=(N_SB, s // QQ),
        in_specs=[blk, pl.BlockSpec((s, HD), lambda h, i: (0, N_HEADS + h)), pl.BlockSpec((s, HD), lambda h, i: (0, 2 * N_HEADS + h)), blk, blk],
        out_specs=[blk, full, full], out_shape=[_sds((s, N_SB * HD), BF)] * 3,
        scratch_shapes=[pltpu.VMEM((s, HD), F32), pltpu.VMEM((s, HD), F32)],
        compiler_params=_cp("parallel", "arbitrary"),
    )(qkv, qkv, qkv, do, total)


def _fgate_fwd(f, b, name):
    s = f.shape[0]
    nb = s // QB
    nfox = N_HEADS - N_SB

    def body(f_ref, b_ref, cb_ref, ct_ref):
        row, col = _iota2()
        upto = (row >= col).astype(BF)
        carry = jnp.zeros((1, HD), F32)
        for blk in range(nb):
            xv = f_ref[blk * QB:(blk + 1) * QB, :] + b_ref[...]
            logf = -_softplus(-xv)
            cum = _tri_dot(logf, upto, left=True) + carry
            carry = cum[QB - 1:QB, :]
            ct_ref[blk] = cum.T
            for h in range(nfox):
                cb_ref[blk * QB:(blk + 1) * QB, h * HD:(h + 1) * HD] = jnp.broadcast_to(cum[:, h:h + 1], (QB, HD))

    return pl.pallas_call(
        body, name=name, out_shape=[_sds((s, nfox * HD), F32), _sds((nb, HD, HD), F32)], compiler_params=_cp(),
    )(f, b)


def _fgate_bwd(dcq, dck, f, b, name):
    s = f.shape[0]
    nb = s // QB
    nfox = N_HEADS - N_SB

    def body(dcq_ref, dck_ref, f_ref, b_ref, df_ref, db_ref):
        row, col = _iota2()
        from_tri = (row <= col).astype(BF)
        lane = col
        carry = jnp.zeros((1, HD), F32)
        db = jnp.zeros((1, HD), F32)
        for blk in reversed(range(nb)):
            dcum = jnp.zeros((QB, HD), F32)
            for h in range(nfox):
                here = (slice(blk * QB, (blk + 1) * QB), slice(h * HD, (h + 1) * HD))
                dcum = jnp.where(lane == h, dcq_ref[here] - dck_ref[here], dcum)
            dlogf = _tri_dot(dcum, from_tri, left=True) + carry
            carry = dlogf[0:1, :]
            xv = f_ref[blk * QB:(blk + 1) * QB, :] + b_ref[...]
            sp = _softplus(xv)
            df = jnp.where(lane < nfox, dlogf * jnp.exp(-sp), 0.0)
            df_ref[blk * QB:(blk + 1) * QB, :] = df.astype(BF)
            db = db + jnp.sum(df, axis=0, keepdims=True)
        db_ref[...] = db

    return pl.pallas_call(
        body, name=name, out_shape=[_sds((s, HD), BF), _sds((1, HD), F32)], compiler_params=_cp(),
    )(dcq, dck, f, b)


def _fox_head_row(ct_ref, j, h):
    tile = ct_ref[j]
    sub = lax.broadcasted_iota(jnp.int32, tile.shape, 0)
    return jnp.sum(jnp.where(sub == h, tile, 0.0), axis=0, keepdims=True)


def _fox_tile_row(ct_ref, t, h):
    nsub = KT // QB
    return jnp.concatenate([_fox_head_row(ct_ref, t * nsub + b, h) for b in range(nsub)], axis=1)


def _fox_fwd(qkv, cum_b, cum_t, name):
    s = qkv.shape[0]
    nb = s // QB
    nfox = N_HEADS - N_SB

    def body(q_ref, k_ref, v_ref, cq_ref, ct_ref, o_ref, ot_ref, lse_ref):
        h, i = pl.program_id(0), pl.program_id(1)
        q = q_ref[...]
        cq = _col(cq_ref[...])
        row, col = _iota_tile()

        def step(t, carry):
            acc, m, l = carry
            off = pl.multiple_of(t * KT, KT)
            k = k_ref[pl.ds(off, KT), :]
            v = v_ref[pl.ds(off, KT), :]
            z = _dot_nt(q, k) * SCALE + cq - _fox_tile_row(ct_ref, t, h)
            z = jnp.where(row + (i * QQ - t * KT) >= col, z, NEG_INF)
            m_new = jnp.maximum(m, jnp.max(z, axis=1, keepdims=True))
            alpha = jnp.exp(m - m_new)
            p = jnp.exp(z - m_new)
            l = alpha * l + jnp.sum(p, axis=1, keepdims=True)
            acc = alpha * acc + _dot(p.astype(BF), v)
            return acc, m_new, l

        acc, m, l = lax.fori_loop(0, (i * QQ + QQ - 1) // KT + 1, step,
                                  (jnp.zeros((QQ, HD), F32), jnp.full((QQ, 1), NEG_INF, F32), jnp.zeros((QQ, 1), F32)))
        o = acc / l
        o_ref[...] = o.astype(BF)
        ot_ref[...] = o.T.astype(BF)
        lse_ref[...] = jnp.broadcast_to(m + jnp.log(l), (QQ, HD))

    blk = pl.BlockSpec((QQ, HD), lambda h, i: (i, h))
    return pl.pallas_call(
        body, name=name, grid=(nfox, s // QQ),
        in_specs=[pl.BlockSpec((QQ, HD), lambda h, i: (i, N_SB + h)),
                  pl.BlockSpec((s, HD), lambda h, i: (0, N_HEADS + N_SB + h)),
                  pl.BlockSpec((s, HD), lambda h, i: (0, 2 * N_HEADS + N_SB + h)),
                  blk, pl.BlockSpec((nb, 8, HD), lambda h, i: (0, 0, 0))],
        out_specs=[blk, pl.BlockSpec((HD, QQ), lambda h, i: (h, i)), blk],
        out_shape=[_sds((s, nfox * HD), BF), _sds((nfox * HD, s), BF), _sds((s, nfox * HD), F32)],
        compiler_params=_cp("parallel", "parallel"),
    )(qkv, qkv, qkv, cum_b, cum_t)


def _fox_bwd(qkv, cum_b, cum_t, o, lse, do, name):
    s = qkv.shape[0]
    nb = s // QB
    nfox = N_HEADS - N_SB

    def body(q_ref, k_ref, v_ref, cq_ref, ct_ref, o_ref, lse_ref, do_ref, dq_ref, dk_ref, dv_ref, dcq_ref, dc_ref, dk_acc, dv_acc, dc_acc):
        h, i = pl.program_id(0), pl.program_id(1)

        @pl.when(i == 0)
        def _():
            dk_acc[...] = jnp.zeros_like(dk_acc)
            dv_acc[...] = jnp.zeros_like(dv_acc)
            dc_acc[...] = jnp.zeros_like(dc_acc)

        q = q_ref[...]
        cq = _col(cq_ref[...])
        dov = do_ref[...]
        lse_c = _col(lse_ref[...])
        delta = jnp.sum(dov.astype(F32) * o_ref[...].astype(F32), axis=1, keepdims=True)
        row, col = _iota_tile()
        ones = jnp.ones((QQ, HD), BF)

        def step(t, carry):
            dq, over_keys = carry
            off = pl.multiple_of(t * KT, KT)
            k = k_ref[pl.ds(off, KT), :]
            v = v_ref[pl.ds(off, KT), :]
            z = _dot_nt(q, k) * SCALE + cq - _fox_tile_row(ct_ref, t, h)
            p = jnp.where(row + (i * QQ - t * KT) >= col, jnp.exp(z - lse_c), 0.0)
            dz = p * (_dot_nt(dov, v) - delta)
            dzt = dz.T
            dq = dq + _dot((dz * SCALE).astype(BF), k)
            dk_acc[pl.ds(off, KT), :] += _dot((dzt * SCALE).astype(BF), q)
            dv_acc[pl.ds(off, KT), :] += _dot(p.T.astype(BF), dov)
            dc_acc[pl.ds(off, KT), :] += _tri_dot(dzt, ones)
            return dq, over_keys + jnp.sum(dz, axis=1, keepdims=True)

        dq, over_keys = lax.fori_loop(0, (i * QQ + QQ - 1) // KT + 1, step, (jnp.zeros((QQ, HD), F32), jnp.zeros((QQ, 1), F32)))
        dq_ref[...] = dq.astype(BF)
        dcq_ref[...] = jnp.broadcast_to(over_keys, (QQ, HD))

        @pl.when(i == s // QQ - 1)
        def _():
            dk_ref[...] = dk_acc[...].astype(BF)
            dv_ref[...] = dv_acc[...].astype(BF)
            dc_ref[...] = dc_acc[...]

    blk = pl.BlockSpec((QQ, HD), lambda h, i: (i, h))
    full = pl.BlockSpec((s, HD), lambda h, i: (0, h))
    return pl.pallas_call(
        body, name=name, grid=(nfox, s // QQ),
        in_specs=[pl.BlockSpec((QQ, HD), lambda h, i: (i, N_SB + h)),
                  pl.BlockSpec((s, HD), lambda h, i: (0, N_HEADS + N_SB + h)),
                  pl.BlockSpec((s, HD), lambda h, i: (0, 2 * N_HEADS + N_SB + h)),
                  blk, pl.BlockSpec((nb, 8, HD), lambda h, i: (0, 0, 0)), blk, blk,
                  pl.BlockSpec((QQ, HD), lambda h, i: (i, N_SB + h))],
        out_specs=[blk, full, full, blk, full],
        out_shape=[_sds((s, nfox * HD), BF)] * 3 + [_sds((s, nfox * HD), F32)] * 2,
        scratch_shapes=[pltpu.VMEM((s, HD), F32)] * 3,
        compiler_params=_cp("parallel", "arbitrary"),
    )(qkv, qkv, qkv, cum_b, cum_t, o, lse, do)


GB = 4
DIL_PAD = QB * 16


def _dil_group(g, d, off=0, shift=0):
    if d == 1:
        return [pl.ds(pl.multiple_of(off + (g * GB + shift) * QB, QB), GB * QB)]
    if d == 4:
        return [pl.ds(off + g + shift * QB * d, GB * QB, stride=d)]
    assert d == 16 and shift == 0
    return [pl.ds(off + g * GB + b, QB, stride=d) for b in range(GB)]


def _dil_load(ref, g, d, off=0, shift=0):
    parts = [ref[sl, :] for sl in _dil_group(g, d, off, shift)]
    rows = parts[0] if len(parts) == 1 else jnp.concatenate(parts, axis=0)
    return rows.reshape(GB, QB, HD)


def _dil_store(ref, g, d, val, off=0, shift=0, add=False):
    rows = val.reshape(GB * QB, HD)
    slices = _dil_group(g, d, off, shift)
    for b, sl in enumerate(slices):
        piece = rows if len(slices) == 1 else rows[b * QB:(b + 1) * QB]
        if add:
            ref[sl, :] += piece
        else:
            ref[sl, :] = piece


def _bdot_nt(a, b):
    return lax.dot_general(a, b, (((2,), (2,)), ((0,), (0,))), preferred_element_type=F32)


def _bdot(a, b):
    return lax.dot_general(a, b, (((2,), (1,)), ((0,), (0,))), preferred_element_type=F32)


def _bdot_tn(a, b):
    return lax.dot_general(jnp.swapaxes(a, 1, 2).astype(BF), b, (((2,), (1,)), ((0,), (0,))), preferred_element_type=F32)


def _dil_masks(g, d, nb):
    row = lax.broadcasted_iota(jnp.int32, (GB, QB, QB), 1)
    col = lax.broadcasted_iota(jnp.int32, (GB, QB, QB), 2)
    blk = lax.broadcasted_iota(jnp.int32, (GB, QB, QB), 0) + (g * GB if d == 1 else 0)
    return col <= row, jnp.logical_and(col >= row, blk >= 1) if nb > 1 else None


def _dilated_fwd(qkv, name):
    s = qkv.shape[0]
    npat = len(DILATED_PATTERNS)
    chunk = 256

    def body(q_ref, k_ref, v_ref, out_ref, outt_ref, g_ref, qf, kf, vf, *per_pattern):
        o_s, l_s = per_pattern[:npat], per_pattern[npat:]
        qf[...] = q_ref[...].astype(F32)
        for dst, src in ((kf, k_ref), (vf, v_ref)):
            dst[0:DIL_PAD, :] = jnp.zeros((DIL_PAD, HD), F32)
            dst[DIL_PAD:, :] = src[...].astype(F32)
        for p, (_, d) in enumerate(DILATED_PATTERNS):
            nb = s // d // QB

            def grp(g, carry, p=p, d=d, nb=nb):
                mc, mp = _dil_masks(g, d, nb)
                q = _dil_load(qf, g, d).astype(BF)
                zc = jnp.where(mc, _bdot_nt(q, _dil_load(kf, g, d, DIL_PAD).astype(BF)) * SCALE, NEG_INF)
                m = jnp.max(zc, axis=2, keepdims=True)
                if nb > 1:
                    zp = jnp.where(mp, _bdot_nt(q, _dil_load(kf, g, d, DIL_PAD, -1).astype(BF)) * SCALE, NEG_INF)
                    m = jnp.maximum(m, jnp.max(zp, axis=2, keepdims=True))
                ec = jnp.exp(zc - m)
                l = jnp.sum(ec, axis=2, keepdims=True)
                if nb > 1:
                    ep = jnp.where(mp, jnp.exp(zp - m), 0.0)
                    l = l + jnp.sum(ep, axis=2, keepdims=True)
                o = _bdot((ec / l).astype(BF), _dil_load(vf, g, d, DIL_PAD).astype(BF))
                if nb > 1:
                    o = o + _bdot((ep / l).astype(BF), _dil_load(vf, g, d, DIL_PAD, -1).astype(BF))
                _dil_store(o_s[p], g, d, o)
                _dil_store(l_s[p], g, d, jnp.broadcast_to(m + jnp.log(l), (GB, QB, HD)))
                return carry

            lax.fori_loop(0, s // (QB * GB), grp, 0)
        for c0 in range(0, s, chunk):
            rows = slice(c0, c0 + chunk)
            ls = [l_s[p][rows, :] for p in range(npat)]
            m = functools.reduce(jnp.maximum, ls)
            es = [jnp.exp(l - m) for l in ls]
            tot = functools.reduce(lambda a, b: a + b, es)
            out = functools.reduce(lambda a, b: a + b, [(e / tot) * o_s[p][rows, :] for p, e in enumerate(es)])
            out_ref[rows, :] = out.astype(BF)
            outt_ref[:, rows] = out.T.astype(BF)
            g_ref[rows, :] = m + jnp.log(tot)

    full = pl.BlockSpec((s, HD), lambda h: (0, h))
    return pl.pallas_call(
        body, name=name, grid=(N_HEADS,),
        in_specs=[full, pl.BlockSpec((s, HD), lambda h: (0, N_HEADS + h)), pl.BlockSpec((s, HD), lambda h: (0, 2 * N_HEADS + h))],
        out_specs=[full, pl.BlockSpec((HD, s), lambda h: (h, 0)), full],
        out_shape=[_sds((s, N_HEADS * HD), BF), _sds((N_HEADS * HD, s), BF), _sds((s, N_HEADS * HD), F32)],
        scratch_shapes=[pltpu.VMEM((s, HD), F32)] + [pltpu.VMEM((s + DIL_PAD, HD), F32)] * 2 + [pltpu.VMEM((s, HD), F32)] * (2 * npat),
        compiler_params=_cp("parallel"),
    )(qkv, qkv, qkv)


def _dilated_bwd(qkv, out, glse, do, tables, name):
    s = qkv.shape[0]
    chunk = 256

    def body(q_ref, k_ref, v_ref, out_ref, g_ref, do_ref, c_ref, sa_ref, sb_ref, dq_ref, dk_ref, dv_ref,
             qf, kf, vf, dof, dl_s, dq_a, dk_a, dv_a):
        qf[...] = q_ref[...].astype(F32)
        for dst, src in ((kf, k_ref), (vf, v_ref)):
            dst[0:DIL_PAD, :] = jnp.zeros((DIL_PAD, HD), F32)
            dst[DIL_PAD:, :] = src[...].astype(F32)
        for c0 in range(0, s, chunk):
            rows = slice(c0, c0 + chunk)
            dov = do_ref[rows, :].astype(F32)
            dof[rows, :] = dov
            dl_s[rows, :] = jnp.broadcast_to(jnp.sum(dov * out_ref[rows, :].astype(F32), axis=1, keepdims=True), (chunk, HD))
        dq_a[...] = jnp.zeros_like(dq_a)
        dk_a[...] = jnp.zeros_like(dk_a)
        dv_a[...] = jnp.zeros_like(dv_a)
        for _, d in DILATED_PATTERNS:
            nb = s // d // QB

            def grp(g, carry, d=d, nb=nb):
                mc, mp = _dil_masks(g, d, nb)
                q = _dil_load(qf, g, d).astype(BF)
                kc = _dil_load(kf, g, d, DIL_PAD).astype(BF)
                dov = _dil_load(dof, g, d).astype(BF)
                lse = _dil_load(g_ref, g, d)[:, :, 0:1]
                delta = _dil_load(dl_s, g, d)[:, :, 0:1]
                pc = jnp.where(mc, jnp.exp(_bdot_nt(q, kc) * SCALE - lse), 0.0)
                dzc = pc * (_bdot_nt(dov, _dil_load(vf, g, d, DIL_PAD).astype(BF)) - delta) * SCALE
                dq = _bdot(dzc.astype(BF), kc)
                if nb > 1:
                    kp = _dil_load(kf, g, d, DIL_PAD, -1).astype(BF)
                    pp = jnp.where(mp, jnp.exp(_bdot_nt(q, kp) * SCALE - lse), 0.0)
                    dzp = pp * (_bdot_nt(dov, _dil_load(vf, g, d, DIL_PAD, -1).astype(BF)) - delta) * SCALE
                    dq = dq + _bdot(dzp.astype(BF), kp)
                _dil_store(dq_a, g, d, dq, add=True)
                _dil_store(dk_a, g, d, _bdot_tn(dzc, q), DIL_PAD, add=True)
                _dil_store(dv_a, g, d, _bdot_tn(pc, dov), DIL_PAD, add=True)
                if nb > 1:
                    _dil_store(dk_a, g, d, _bdot_tn(dzp, q), DIL_PAD, -1, add=True)
                    _dil_store(dv_a, g, d, _bdot_tn(pp, dov), DIL_PAD, -1, add=True)
                return carry

            lax.fori_loop(0, s // (QB * GB), grp, 0)
        for c0 in range(0, s, chunk):
            rows = slice(c0, c0 + chunk)
            padded = slice(DIL_PAD + c0, DIL_PAD + c0 + chunk)
            c, sa, sb = c_ref[rows, :], sa_ref[rows, :], sb_ref[rows, :]
            dq_ref[rows, :] = _rope(dq_a[rows, :], c, sa, sb).astype(BF)
            dk_ref[rows, :] = _rope(dk_a[padded, :], c, sa, sb).astype(BF)
            dv_ref[rows, :] = dv_a[padded, :].astype(BF)

    full = pl.BlockSpec((s, HD), lambda h: (0, h))
    tab = pl.BlockSpec((s, HD), lambda h: (0, 0))
    return pl.pallas_call(
        body, name=name, grid=(N_HEADS,),
        in_specs=[full, pl.BlockSpec((s, HD), lambda h: (0, N_HEADS + h)), pl.BlockSpec((s, HD), lambda h: (0, 2 * N_HEADS + h)),
                  full, full, full, tab, tab, tab],
        out_specs=[full, full, full], out_shape=[_sds((s, N_HEADS * HD), BF)] * 3,
        scratch_shapes=[pltpu.VMEM((s, HD), F32)] + [pltpu.VMEM((s + DIL_PAD, HD), F32)] * 2 + [pltpu.VMEM((s, HD), F32)] * 3
        + [pltpu.VMEM((s + DIL_PAD, HD), F32)] * 2,
        compiler_params=_cp("parallel"),
    )(qkv, qkv, qkv, out, glse, do, *tables)


def _swiglu_fwd(x, gnorm, w, tag, deps=()):
    h, ht = _rmsnorm_fwd(x, gnorm, f"norm_{tag}", deps)
    g, u, act, act_t = _ffn_up(h, w["gate"], w["up"], f"ffn_up_{tag}")
    if callable(w["down"]):
        w["down"] = w["down"](g)
    y = _ffn_down(act, w["down"], x, f"ffn_down_{tag}")
    return y, (x, ht, g, u, act_t)


def _swiglu_bwd(saved, gnorm, w, dy, dyb_half, out_scale, tag, deps=(), on_down=None, on_grads=None):
    x, ht, g, u, act_t = saved
    dg, du = _ffn_bwd_act(dyb_half, w["down"], g, u, f"ffn_bwd_act_{tag}", deps)
    d_down = _grad_rows(act_t, dyb_half, f"ffn_bwd_wd_{tag}")
    tokens = list(on_down(d_down)) if on_down else []
    d_gate, d_up = _grad_cols(ht, [dg, du], [False, False], f"ffn_bwd_wgu_{tag}", tokens)
    gw = {"gate": d_gate, "up": d_up, "down": d_down}
    tokens = list(on_grads(gw)) if on_grads else []
    dh = _dh_cols([dg, du], [w["gate"], w["up"]], [False, False], f"ffn_bwd_dh_{tag}", tokens)
    dx, dxb, dxbt, dgn = _rmsnorm_bwd(x, gnorm, dh, dy, out_scale, f"norm_bwd_{tag}")
    return (dx, dxb, dxbt), dgn, gw


def kernel(x, norm_g, ffn1_w_gate, ffn1_w_up, ffn1_w_down, ffn2_w_gate, ffn2_w_up, ffn2_w_down, even_w_in, even_b_forget, even_w_out, odd_w_qkv, odd_w_out, final_norm_g, loss_target, m_norm_g, m_ffn1_w_gate, m_ffn1_w_up, m_ffn1_w_down, m_ffn2_w_gate, m_ffn2_w_up, m_ffn2_w_down, m_even_w_in, m_even_b_forget, m_even_w_out, m_odd_w_qkv, m_odd_w_out, m_final_norm_g, v_norm_g, v_ffn1_w_gate, v_ffn1_w_up, v_ffn1_w_down, v_ffn2_w_gate, v_ffn2_w_up, v_ffn2_w_down, v_even_w_in, v_even_b_forget, v_even_w_out, v_odd_w_qkv, v_odd_w_out, v_final_norm_g):
    s, d = x.shape[1], x.shape[2]
    nfox = N_HEADS - N_SB
    ax, ay, ac = lax.axis_index("x"), lax.axis_index("y"), lax.axis_index("c")
    me = 4 * ax + 2 * ay + ac
    slots = jnp.stack([4 * px + 2 * py + ac for px, py in [(ax, ay), (1 - ax, ay), (ax, 1 - ay), (1 - ax, 1 - ay)]]).astype(jnp.int32)
    x0 = x.reshape(s, d)
    target = loss_target.reshape(s, d)

    def bf(w):
        return w.astype(BF)

    groups = [
        [bf(ffn1_w_gate[0]), bf(ffn1_w_up[0]), norm_g.reshape(6, d // NDEV)],
        [bf(even_w_in[0]), bf(even_w_out[0])],
        [bf(ffn2_w_gate[0]), bf(ffn2_w_up[0]), bf(ffn2_w_down[0])],
        [bf(ffn1_w_gate[1]), bf(ffn1_w_up[1]), bf(ffn1_w_down[1])],
        [bf(odd_w_qkv[0]), bf(odd_w_out[0])],
        [bf(ffn2_w_gate[1]), bf(ffn2_w_up[1]), bf(ffn2_w_down[1])],
        [bf(ffn1_w_down[0])],
    ]
    started = [None] * len(groups)
    last_token = []
    for k in (0, 6, 1, 2, 3, 4, 5):
        started[k] = _gather_start(groups[k], me, last_token, f"gather_start_{k}")
        last_token = [started[k]["token"]]
    all_started = last_token

    def gathered(k, after, early=None):
        return _gather_finish(_gather_forward(started[k], after if early is None else early, f"gather_forward_{k}"), after, f"gather_finish_{k}")

    def ffn_weights(ws_):
        return {"gate": ws_[0], "up": ws_[1], "down": ws_[2]}

    b_pad = jnp.pad(even_b_forget, ((0, 0), (0, HD - nfox)))
    gfin = final_norm_g.reshape(1, d)

    g0 = gathered(0, x0)
    gn = jnp.transpose(g0[2], (1, 0, 2)).reshape(6, 1, d)
    wf = [[{"gate": g0[0], "up": g0[1], "down": lambda after: gathered(6, after)[0]}, None], [None, None]]
    x1, sv_f1_0 = _swiglu_fwd(x0, gn[0], wf[0][0], "l0a", all_started)
    g1 = gathered(1, x1)
    w_in_nat = jnp.transpose(g1[0], (1, 0, 2)).reshape(d, -1)
    w_qkv_e = w_in_nat[:, :3 * d]
    w_f = jnp.pad(w_in_nat[:, 3 * d:], ((0, 0), (0, HD - nfox)))
    w_out_e = g1[1].reshape(d, d)
    h_e, ht_e = _rmsnorm_fwd(x1, gn[1], "norm_l0m")
    qkv_e = _mm_nn(h_e, w_qkv_e, 768, BF, "even_qkv")
    f_e = _mm_nn(h_e, w_f, HD, F32, "even_fgate")
    o_sb, ot_sb, tot_sb = _sb_fwd(qkv_e, "sb_fwd")
    cum_b, cum_t = _fgate_fwd(f_e, b_pad, "fgate_fwd")
    o_fox, ot_fox, lse_fox = _fox_fwd(qkv_e, cum_b, cum_t, "fox_fwd")
    o_e = jnp.concatenate([o_sb, o_fox], axis=1)
    ot_e = jnp.concatenate([ot_sb, ot_fox], axis=0)
    x2 = _mm_nn(o_e, w_out_e, 1024, F32, "even_out", res=x1)
    wf[0][1] = ffn_weights(gathered(2, x2, early=o_fox))
    x3, sv_f2_0 = _swiglu_fwd(x2, gn[2], wf[0][1], "l0b")

    wf[1][0] = ffn_weights(gathered(3, x3))
    x4, sv_f1_1 = _swiglu_fwd(x3, gn[3], wf[1][0], "l1a")
    g4 = gathered(4, x4, early=sv_f1_1[2])
    w_qkv_o = g4[0]
    w_out_o = g4[1].reshape(d, d)
    h_o, ht_o = _rmsnorm_fwd(x4, gn[4], "norm_l1m")
    qkv_o = _qkv_rope(h_o, w_qkv_o, _rope_tables(s, 1.0), "odd_qkv")
    o_o, ot_o, glse = _dilated_fwd(qkv_o, "dilated_fwd")
    x5 = _mm_nn(o_o, w_out_o, 1024, F32, "odd_out", res=x4)
    wf[1][1] = ffn_weights(gathered(5, x5, early=o_o))
    x6, sv_f2_1 = _swiglu_fwd(x5, gn[5], wf[1][1], "l1b")

    def chip_sums(gs, a_s, tag):
        ps = _pair_sum(gs, a_s, slots, f"pair_sum_{tag}")
        return gs, a_s, _chip_start(ps, f"chip_start_{tag}")

    def as_slices(gs):
        return [g_ if g_.ndim == 3 else g_.reshape(NDEV, g_.shape[0] // NDEV, g_.shape[1]) for g_ in gs]

    def reduce_start(gs, tag):
        gs = as_slices(gs)
        return chip_sums(gs, _pair_exchange(gs, f"pair_exchange_{tag}"), tag)

    red, crossing = {}, {}

    def cross(gs, tag):
        crossing[tag] = _pair_start(as_slices(gs), f"pair_start_{tag}")
        return [crossing[tag]["token"]]

    def reduce_behind_dh(tag):
        return lambda gw: cross([gw["gate"], gw["up"], gw["down"]], tag)

    def reduce_after(tag, after):
        red[tag] = chip_sums(*_pair_finish(crossing[tag], after, f"pair_finish_{tag}"), tag)
        return [red[tag][2]["token"]]

    def reduce_now(tag, names):
        def hook(gw):
            red[tag] = reduce_start([gw[nm] for nm in names] if names else [gw], tag)
            return [red[tag][2]["token"]]
        return hook

    dx6, dx6b, _, d_gfin, loss_part = _loss_head(x6, gfin, target, "loss_head")

    (dx5, dx5b, dx5bt), dgn5, _ = _swiglu_bwd(sv_f2_1, gn[5], wf[1][1], dx6, dx6b, 1.0, "l1b", on_grads=reduce_behind_dh("l1b"))
    d_wout_o = _mm_nn(ot_o, dx5b, 1024, BF, "odd_out_dw")
    do_o = _mm_nt([(dx5b, w_out_o)], "odd_out_do", BF, deps=reduce_after("l1b", dx5))
    dqkv_o = jnp.concatenate(_dilated_bwd(qkv_o, o_o, glse, do_o, _rope_tables(s, -1.0), "dilated_bwd"), axis=1)
    (d_wqkv_o,) = _grad_cols(ht_o, [dqkv_o], [True], "odd_qkv_dw")
    dh_o = _dh_cols([dqkv_o], [w_qkv_o], [True], "odd_qkv_dh", cross([d_wqkv_o, d_wout_o], "l1m"))
    dx4, dx4b, _, dgn4 = _rmsnorm_bwd(x4, gn[4], dh_o, dx5, 0.5, "norm_bwd_l1m")
    (dx3, dx3b, _), dgn3, _ = _swiglu_bwd(sv_f1_1, gn[3], wf[1][0], dx4, dx4b, 0.5, "l1a", reduce_after("l1m", dx4),
                                         on_grads=reduce_behind_dh("l1a"))

    (dx2, dx2b, dx2bt), dgn2, _ = _swiglu_bwd(sv_f2_0, gn[2], wf[0][1], dx3, dx3b, 1.0, "l0b", reduce_after("l1a", dx3),
                                             on_grads=reduce_behind_dh("l0b"))
    d_wout_e = _mm_nn(ot_e, dx2b, 1024, BF, "even_out_dw")
    do_e = _mm_nt([(dx2b, w_out_e)], "even_out_do", BF, deps=reduce_after("l0b", dx2))
    dq_sb, dk_sb, dv_sb = _sb_bwd(qkv_e, do_e, tot_sb, "sb_bwd")
    dq_fx, dk_fx, dv_fx, dcq, dck = _fox_bwd(qkv_e, cum_b, cum_t, o_fox, lse_fox, do_e, "fox_bwd")
    df, db_part = _fgate_bwd(dcq, dck, f_e, b_pad, "fgate_bwd")
    dqkv_e = jnp.concatenate([dq_sb, dq_fx, dk_sb, dk_fx, dv_sb, dv_fx], axis=1)
    d_wqkv_e = _mm_nn(ht_e, dqkv_e, 768, BF, "even_qkv_dw")
    d_wf = _mm_nn(ht_e, df, HD, BF, "even_fgate_dw")
    d_win_nat = jnp.concatenate([d_wqkv_e, d_wf[:, :nfox]], axis=1)
    d_win = jnp.transpose(d_win_nat.reshape(d, NDEV, -1), (1, 0, 2))
    dh_e = _mm_nt([(dqkv_e, w_qkv_e), (df, w_f)], "even_in_dh", deps=cross([d_win, d_wout_e], "l0m"))
    dx1, dx1b, _, dgn1 = _rmsnorm_bwd(x1, gn[1], dh_e, dx2, 0.5, "norm_bwd_l0m")
    (dx0, _, _), dgn0, _ = _swiglu_bwd(sv_f1_0, gn[0], wf[0][0], dx1, dx1b, 1.0, "l0a", reduce_after("l0m", dx1),
                                      on_down=reduce_now("l0a_down", None), on_grads=reduce_now("l0a_gu", ["gate", "up"]))

    def reduce_finish(red, tag, after):
        gs, a_s, st = red
        return list(zip(gs, a_s, _chip_finish(st, after, f"chip_finish_{tag}")))

    f_l1b, f_l1m, f_l1a = (reduce_finish(red[t], t, dx0) for t in ("l1b", "l1m", "l1a"))
    f_l0b, f_l0m = (reduce_finish(red[t], t, dx0) for t in ("l0b", "l0m"))

    def update(w_, m_, v_, parts, nm):
        if w_.shape[2] % 128 == 0:
            return _adamw_sharded(w_, m_, v_, parts, slots, f"adamw_{nm}")
        outs = _adamw_sharded(jnp.swapaxes(w_, 1, 2), jnp.swapaxes(m_, 1, 2), jnp.swapaxes(v_, 1, 2), parts, slots,
                              f"adamw_{nm}", transposed=True)
        return [jnp.swapaxes(o, 1, 2) for o in outs]

    res = {}
    res["even_w_in"] = update(even_w_in, m_even_w_in, v_even_w_in, [f_l0m[0]], "even_w_in")
    res["even_w_out"] = _adamw_sharded(even_w_out, m_even_w_out, v_even_w_out, [f_l0m[1]], slots, "adamw_even_w_out")
    res["odd_w_qkv"] = _adamw_sharded(odd_w_qkv, m_odd_w_qkv, v_odd_w_qkv, [f_l1m[0]], slots, "adamw_odd_w_qkv")
    res["odd_w_out"] = _adamw_sharded(odd_w_out, m_odd_w_out, v_odd_w_out, [f_l1m[1]], slots, "adamw_odd_w_out")
    names = ["ffn2_w_gate", "ffn2_w_up", "ffn2_w_down", "ffn1_w_gate", "ffn1_w_up", "ffn1_w_down"]
    ws = [ffn2_w_gate, ffn2_w_up, ffn2_w_down, ffn1_w_gate, ffn1_w_up, ffn1_w_down]
    ms = [m_ffn2_w_gate, m_ffn2_w_up, m_ffn2_w_down, m_ffn1_w_gate, m_ffn1_w_up, m_ffn1_w_down]
    vs = [v_ffn2_w_gate, v_ffn2_w_up, v_ffn2_w_down, v_ffn1_w_gate, v_ffn1_w_up, v_ffn1_w_down]
    for k in range(3):
        res[names[k]] = update(ws[k], ms[k], vs[k], [f_l0b[k], f_l1b[k]], names[k])
    f_l0a = (reduce_finish(red["l0a_gu"], "l0a_gu", res["ffn2_w_down"][1])
             + reduce_finish(red["l0a_down"], "l0a_down", res["ffn2_w_down"][1]))
    for k in range(3, 6):
        res[names[k]] = update(ws[k], ms[k], vs[k], [f_l0a[k - 3], f_l1a[k - 3]], names[k])

    dnorm = jnp.concatenate([dgn0, dgn1, dgn2, dgn3, dgn4, dgn5], axis=0)
    nsm = d // NDEV
    small_rows = (6 * d + d + 2 * HD) // HD
    pad_rows = -small_rows % 8
    part = jnp.concatenate([dnorm.reshape(-1), d_gfin.reshape(-1), db_part.reshape(-1), loss_part.reshape(-1),
                            jnp.zeros((pad_rows * HD,), F32)]).reshape(small_rows + pad_rows, HD)
    (gathered,) = _all_gather([part], "gather_small")

    def pack(ng, bfg, fg):
        full = lax.dynamic_update_slice(jnp.zeros((6, d), F32), ng.reshape(6, nsm), (0, me * nsm))
        return jnp.concatenate([full.reshape(-1), fg.reshape(-1), jnp.pad(bfg.reshape(-1), (0, HD - nfox)),
                                jnp.zeros((HD + pad_rows * HD,), F32)]).reshape(small_rows + pad_rows, HD)

    sm = _adamw_small(pack(norm_g, even_b_forget, final_norm_g), pack(m_norm_g, m_even_b_forget, m_final_norm_g),
                      pack(v_norm_g, v_even_b_forget, v_final_norm_g), gathered, "adamw_small")

    def unpack(t):
        flat = t.reshape(-1)
        ng = lax.dynamic_slice(flat[:6 * d].reshape(6, d), (0, me * nsm), (6, nsm)).reshape(norm_g.shape)
        fg = flat[6 * d:7 * d].reshape(final_norm_g.shape)
        bfg = flat[7 * d:7 * d + nfox].reshape(even_b_forget.shape)
        return ng, bfg, fg

    sm_g, sm_d, sm_m, sm_v = [unpack(t) for t in sm]
    loss = sm[0].reshape(-1)[7 * d + HD]

    order = ["norm_g", "ffn1_w_gate", "ffn1_w_up", "ffn1_w_down", "ffn2_w_gate", "ffn2_w_up", "ffn2_w_down", "even_w_in",
             "even_b_forget", "even_w_out", "odd_w_qkv", "odd_w_out", "final_norm_g"]
    outs = [loss, dx0.reshape(x.shape)]
    for k in range(4):
        smk = [sm_g, sm_d, sm_m, sm_v][k]
        for nm in order:
            if nm == "norm_g":
                outs.append(smk[0])
            elif nm == "even_b_forget":
                outs.append(smk[1])
            elif nm == "final_norm_g":
                outs.append(smk[2])
            else:
                outs.append(res[nm][k])
    return tuple(outs)
```

```python
import functools

import jax
import jax.numpy as jnp
import numpy as np
from jax import lax
from jax.experimental import pallas as pl
from jax.experimental.pallas import tpu as pltpu

F32 = jnp.float32
BF = jnp.bfloat16
NDEV = 8
HD = 128
QB = 128
N_HEADS = 16
N_SB = 8
SCALE = HD ** -0.5
ROPE_THETA = 500000.0
ROPE_DIMS = HD // 4
DILATED_PATTERNS = ((128, 1), (512, 4), (2048, 16))
RMS_EPS = 1e-6
NEG_INF = -1e30
ADAM_LR = 0.001
ADAM_B1 = 0.9
ADAM_B2 = 0.999
ADAM_EPS = 1e-08
ADAM_WD = 0.01
ADAM_STEP = 10
VMEM_LIMIT_V7X = 56 * 1024 * 1024
MESH = pl.DeviceIdType.MESH
ANY = pl.BlockSpec(memory_space=pl.ANY)

NT_DIMS = (((1,), (1,)), ((), ()))


def _cp(*dims):
    return pltpu.CompilerParams(dimension_semantics=dims if dims else None, vmem_limit_bytes=VMEM_LIMIT_V7X)


def _dot(a, b):
    return jnp.dot(a, b, preferred_element_type=F32)


def _dot_nt(a, b):
    return lax.dot_general(a, b, NT_DIMS, preferred_element_type=F32)


def _sds(shape, dtype):
    return jax.ShapeDtypeStruct(shape, dtype)


def _place():
    x, y, c = lax.axis_index("x"), lax.axis_index("y"), lax.axis_index("c")
    chips = [(x, y), (1 - x, y), (x, 1 - y), (1 - x, 1 - y)]
    return x, y, c, chips


def _all_gather(xs, name):
    n = len(xs)

    def body(*refs):
        x_refs, out_refs = refs[:n], refs[n:2 * n]
        send_sems, recv_sems, local_sems = refs[2 * n:]
        x, y, c, chips = _place()
        me, sibling = (x, y, c), (x, y, 1 - c)
        others = chips[1:]

        def slot(a, px, py, pc):
            return out_refs[a].at[4 * px + 2 * py + pc]

        def copy(a, k, block, to, src=None):
            return pltpu.make_async_remote_copy(
                src_ref=slot(a, *block) if src is None else src, dst_ref=slot(a, *block),
                send_sem=send_sems.at[a, k], recv_sem=recv_sems.at[a, k], device_id=to, device_id_type=MESH)

        started = []
        for a in range(n):
            mine = pltpu.make_async_copy(x_refs[a], slot(a, *me), local_sems.at[a])
            mine.start()
            first = [copy(a, 0, me, sibling, src=x_refs[a])]
            first += [copy(a, 1 + j, me, (*chip, c), src=x_refs[a]) for j, chip in enumerate(others)]
            for cp in first:
                cp.start()
            started += [mine.wait] + [cp.wait_send for cp in first]
        for a in range(n):
            for j, chip in enumerate(others):
                copy(a, 1 + j, (*chip, c), me).wait_recv()
                passed = copy(a, 4 + j, (*chip, c), sibling)
                passed.start()
                started.append(passed.wait_send)
        for a in range(n):
            copy(a, 0, sibling, me).wait_recv()
            for j, chip in enumerate(others):
                copy(a, 4 + j, (*chip, 1 - c), me).wait_recv()
        for w in started:
            w()

    return pl.pallas_call(
        body, name=name,
        out_shape=[_sds((NDEV,) + x.shape, x.dtype) for x in xs],
        in_specs=[ANY] * n, out_specs=[ANY] * n,
        scratch_shapes=[pltpu.SemaphoreType.DMA((n, 7)), pltpu.SemaphoreType.DMA((n, 7)), pltpu.SemaphoreType.DMA((n,))],
    )(*xs)


def _pair_exchange(gs, name):
    n = len(gs)

    def body(*refs):
        g_refs, a_refs = refs[:n], refs[n:2 * n]
        send_sems, recv_sems = refs[2 * n:]
        x, y, c, chips = _place()
        copies = []
        for a in range(n):
            for j, (px, py) in enumerate(chips):
                copies.append(pltpu.make_async_remote_copy(
                    src_ref=g_refs[a].at[4 * px + 2 * py + (1 - c)], dst_ref=a_refs[a].at[j],
                    send_sem=send_sems.at[a, j], recv_sem=recv_sems.at[a, j],
                    device_id=(x, y, 1 - c), device_id_type=MESH))
        for cp in copies:
            cp.start()
        for cp in copies:
            cp.wait()

    return pl.pallas_call(
        body, name=name,
        out_shape=[_sds((4,) + g.shape[1:], g.dtype) for g in gs],
        in_specs=[ANY] * n, out_specs=[ANY] * n,
        scratch_shapes=[pltpu.SemaphoreType.DMA((n, 4)), pltpu.SemaphoreType.DMA((n, 4))],
    )(*gs)


HBM = pl.BlockSpec(memory_space=pltpu.HBM)
SEM = pl.BlockSpec(memory_space=pltpu.SEMAPHORE)
EFFECT = pltpu.SideEffectType.DATAFLOW_SIDE_EFFECTING
TOKEN = _sds((8, 128), F32)
TOKEN_SPEC = pl.BlockSpec((8, 128), lambda *_: (0, 0))


def _in_hbm(x):
    return pltpu.with_memory_space_constraint(x, pltpu.HBM)


def _ignore_deps(body, n_in, n_deps):
    if not n_deps:
        return body
    return lambda *refs: body(*refs[:n_in], *refs[n_in + n_deps:])


def _slot_of(px, py, pc):
    return 4 * px + 2 * py + pc


def _gather_start(xs, me, deps, name):
    n = len(xs)
    lands = [lax.dynamic_update_slice(lax.empty((NDEV,) + x.shape, x.dtype), x[None], (me,) + (0,) * x.ndim) for x in xs]

    def body(*refs):
        x_refs, land_refs = refs[:n], refs[n:2 * n]
        send, recv_ici, recv_sib = refs[2 * n:2 * n + 3]
        token = refs[4 * n + 3]
        x, y, c, chips = _place()
        for a in range(n):
            dst = land_refs[a].at[_slot_of(x, y, c)]
            pltpu.make_async_remote_copy(src_ref=x_refs[a], dst_ref=dst, send_sem=send.at[4 * a], recv_sem=recv_sib.at[a],
                                         device_id=(x, y, 1 - c), device_id_type=MESH).start()
            for j, chip in enumerate(chips[1:]):
                pltpu.make_async_remote_copy(src_ref=x_refs[a], dst_ref=dst, send_sem=send.at[4 * a + 1 + j], recv_sem=recv_ici.at[3 * a + j],
                                             device_id=(*chip, c), device_id_type=MESH).start()
        token[...] = jnp.zeros_like(token)

    outs = pl.pallas_call(
        _ignore_deps(body, 2 * n, len(deps)), name=name,
        out_shape=(pltpu.SemaphoreType.DMA((4 * n,)), pltpu.SemaphoreType.DMA((3 * n,)), pltpu.SemaphoreType.DMA((n,)),
                   *[pltpu.HBM(x.shape, x.dtype) for x in xs], *[pltpu.HBM(l.shape, l.dtype) for l in lands], TOKEN),
        in_specs=[HBM] * (2 * n) + [TOKEN_SPEC] * len(deps),
        out_specs=(SEM, SEM, SEM, *[HBM] * (2 * n), pl.BlockSpec(memory_space=pltpu.VMEM)),
        input_output_aliases={a: 3 + a for a in range(2 * n)},
        compiler_params=pltpu.CompilerParams(has_side_effects=EFFECT),
    )(*[_in_hbm(x) for x in xs], *[_in_hbm(l) for l in lands], *deps)
    send, recv_ici, recv_sib = outs[:3]
    return dict(send=send, recv_ici=recv_ici, recv_sib=recv_sib, xs=list(outs[3:3 + n]), lands=list(outs[3 + n:3 + 2 * n]), token=outs[-1])


def _gather_forward(st, after, name):
    n = len(st["lands"])

    def body(*refs):
        land_refs, recv_ici = refs[:n], refs[n]
        send2, recv2, token = refs[n + 2], refs[n + 3], refs[2 * n + 4]
        x, y, c, chips = _place()
        for a in range(n):
            for j, chip in enumerate(chips[1:]):
                blk = land_refs[a].at[_slot_of(*chip, c)]
                pltpu.make_async_remote_copy(src_ref=blk, dst_ref=blk, send_sem=send2.at[3 * a + j], recv_sem=recv_ici.at[3 * a + j],
                                             device_id=(*chip, c), device_id_type=MESH).wait_recv()
                pltpu.make_async_remote_copy(src_ref=blk, dst_ref=blk, send_sem=send2.at[3 * a + j], recv_sem=recv2.at[3 * a + j],
                                             device_id=(x, y, 1 - c), device_id_type=MESH).start()
        token[...] = jnp.zeros_like(token)

    outs = pl.pallas_call(
        body, name=name,
        out_shape=(pltpu.SemaphoreType.DMA((3 * n,)), pltpu.SemaphoreType.DMA((3 * n,)), *[pltpu.HBM(l.shape, l.dtype) for l in st["lands"]], TOKEN),
        in_specs=[HBM] * n + [SEM, pl.BlockSpec(memory_space=pl.ANY)],
        out_specs=(SEM, SEM, *[HBM] * n, pl.BlockSpec(memory_space=pltpu.VMEM)),
        input_output_aliases={a: 2 + a for a in range(n)},
        compiler_params=pltpu.CompilerParams(has_side_effects=EFFECT),
    )(*st["lands"], st["recv_ici"], after)
    return dict(st, send2=outs[0], recv2=outs[1], lands=list(outs[2:2 + n]), token=outs[-1])


def _gather_finish(st, after, name):
    n = len(st["lands"])

    def body(*refs):
        x_refs, land_refs = refs[:n], refs[n:2 * n]
        send, recv_sib, send2, recv2 = refs[2 * n:2 * n + 4]
        x, y, c, chips = _place()
        for a in range(n):
            mine = land_refs[a].at[_slot_of(x, y, c)]
            theirs = land_refs[a].at[_slot_of(x, y, 1 - c)]
            for k in range(4):
                pltpu.make_async_remote_copy(src_ref=x_refs[a], dst_ref=mine, send_sem=send.at[4 * a + k], recv_sem=recv_sib.at[a],
                                             device_id=(x, y, 1 - c), device_id_type=MESH).wait_send()
            pltpu.make_async_remote_copy(src_ref=x_refs[a], dst_ref=theirs, send_sem=send.at[4 * a], recv_sem=recv_sib.at[a],
                                         device_id=(x, y, 1 - c), device_id_type=MESH).wait_recv()
            for j, chip in enumerate(chips[1:]):
                sent = land_refs[a].at[_slot_of(*chip, c)]
                got = land_refs[a].at[_slot_of(*chip, 1 - c)]
                pltpu.make_async_remote_copy(src_ref=sent, dst_ref=sent, send_sem=send2.at[3 * a + j], recv_sem=recv2.at[3 * a + j],
                                             device_id=(x, y, 1 - c), device_id_type=MESH).wait_send()
                pltpu.make_async_remote_copy(src_ref=got, dst_ref=got, send_sem=send2.at[3 * a + j], recv_sem=recv2.at[3 * a + j],
                                             device_id=(x, y, 1 - c), device_id_type=MESH).wait_recv()

    outs = pl.pallas_call(
        body, name=name,
        out_shape=tuple(pltpu.HBM(v.shape, v.dtype) for v in st["xs"] + st["lands"]),
        in_specs=[HBM] * (2 * n) + [SEM] * 4 + [pl.BlockSpec(memory_space=pl.ANY)], out_specs=tuple([HBM] * (2 * n)),
        input_output_aliases={a: a for a in range(2 * n)},
        compiler_params=pltpu.CompilerParams(has_side_effects=EFFECT),
    )(*st["xs"], *st["lands"], st["send"], st["recv_sib"], st["send2"], st["recv2"], after)
    return list(outs[n:])


def _pair_start(gs, name):
    n = len(gs)
    lands = [lax.empty((4,) + g.shape[1:], g.dtype) for g in gs]

    def body(*refs):
        g_refs, a_refs = refs[:n], refs[n:2 * n]
        send, recv = refs[2 * n], refs[2 * n + 1]
        token = refs[4 * n + 2]
        x, y, c, chips = _place()
        for a in range(n):
            for j, (px, py) in enumerate(chips):
                pltpu.make_async_remote_copy(src_ref=g_refs[a].at[_slot_of(px, py, 1 - c)], dst_ref=a_refs[a].at[j],
                                             send_sem=send.at[4 * a + j], recv_sem=recv.at[4 * a + j],
                                             device_id=(x, y, 1 - c), device_id_type=MESH).start()
        token[...] = jnp.zeros_like(token)

    outs = pl.pallas_call(
        body, name=name,
        out_shape=(pltpu.SemaphoreType.DMA((4 * n,)), pltpu.SemaphoreType.DMA((4 * n,)),
                   *[pltpu.HBM(g.shape, g.dtype) for g in gs], *[pltpu.HBM(l.shape, l.dtype) for l in lands], TOKEN),
        in_specs=[HBM] * (2 * n), out_specs=(SEM, SEM, *[HBM] * (2 * n), pl.BlockSpec(memory_space=pltpu.VMEM)),
        input_output_aliases={a: 2 + a for a in range(2 * n)},
        compiler_params=pltpu.CompilerParams(has_side_effects=EFFECT),
    )(*[_in_hbm(g) for g in gs], *[_in_hbm(l) for l in lands])
    return dict(send=outs[0], recv=outs[1], gs=list(outs[2:2 + n]), lands=list(outs[2 + n:2 + 2 * n]), token=outs[-1])


def _pair_finish(st, after, name):
    n = len(st["gs"])

    def body(*refs):
        g_refs, a_refs = refs[:n], refs[n:2 * n]
        send, recv = refs[2 * n], refs[2 * n + 1]
        x, y, c, chips = _place()
        for a in range(n):
            for j, (px, py) in enumerate(chips):
                cp = pltpu.make_async_remote_copy(src_ref=g_refs[a].at[_slot_of(px, py, 1 - c)], dst_ref=a_refs[a].at[j],
                                                  send_sem=send.at[4 * a + j], recv_sem=recv.at[4 * a + j],
                                                  device_id=(x, y, 1 - c), device_id_type=MESH)
                cp.wait_send()
                cp.wait_recv()

    outs = pl.pallas_call(
        body, name=name,
        out_shape=tuple(pltpu.HBM(v.shape, v.dtype) for v in st["gs"] + st["lands"]),
        in_specs=[HBM] * (2 * n) + [SEM, SEM, pl.BlockSpec(memory_space=pl.ANY)], out_specs=tuple([HBM] * (2 * n)),
        input_output_aliases={a: a for a in range(2 * n)},
        compiler_params=pltpu.CompilerParams(has_side_effects=EFFECT),
    )(*st["gs"], *st["lands"], st["send"], st["recv"], after)
    return list(outs[:n]), list(outs[n:])


def _chip_start(ps, name):
    n = len(ps)
    lands = [lax.empty(p.shape, p.dtype) for p in ps]

    def body(*refs):
        p_refs, b_refs = refs[:n], refs[n:2 * n]
        send, recv = refs[2 * n], refs[2 * n + 1]
        token = refs[4 * n + 2]
        x, y, c, chips = _place()
        for a in range(n):
            for j, chip in enumerate(chips[1:]):
                pltpu.make_async_remote_copy(src_ref=p_refs[a].at[j], dst_ref=b_refs[a].at[j], send_sem=send.at[3 * a + j], recv_sem=recv.at[3 * a + j],
                                             device_id=(*chip, c), device_id_type=MESH).start()
        token[...] = jnp.zeros_like(token)

    outs = pl.pallas_call(
        body, name=name,
        out_shape=(pltpu.SemaphoreType.DMA((3 * n,)), pltpu.SemaphoreType.DMA((3 * n,)),
                   *[pltpu.HBM(p.shape, p.dtype) for p in ps], *[pltpu.HBM(p.shape, p.dtype) for p in ps], TOKEN),
        in_specs=[HBM] * (2 * n), out_specs=(SEM, SEM, *[HBM] * (2 * n), pl.BlockSpec(memory_space=pltpu.VMEM)),
        input_output_aliases={a: 2 + a for a in range(2 * n)},
        compiler_params=pltpu.CompilerParams(has_side_effects=EFFECT),
    )(*[_in_hbm(p) for p in ps], *[_in_hbm(l) for l in lands])
    return dict(send=outs[0], recv=outs[1], ps=list(outs[2:2 + n]), lands=list(outs[2 + n:2 + 2 * n]), token=outs[-1])


def _chip_finish(st, after, name):
    n = len(st["ps"])

    def body(*refs):
        p_refs, b_refs = refs[:n], refs[n:2 * n]
        send, recv = refs[2 * n], refs[2 * n + 1]
        x, y, c, chips = _place()
        for a in range(n):
            for j, chip in enumerate(chips[1:]):
                cp = pltpu.make_async_remote_copy(src_ref=p_refs[a].at[j], dst_ref=b_refs[a].at[j], send_sem=send.at[3 * a + j], recv_sem=recv.at[3 * a + j],
                                                  device_id=(*chip, c), device_id_type=MESH)
                cp.wait_send()
                cp.wait_recv()

    outs = pl.pallas_call(
        body, name=name,
        out_shape=tuple(pltpu.HBM(v.shape, v.dtype) for v in st["ps"] + st["lands"]),
        in_specs=[HBM] * (2 * n) + [SEM, SEM, pl.BlockSpec(memory_space=pl.ANY)], out_specs=tuple([HBM] * (2 * n)),
        input_output_aliases={a: a for a in range(2 * n)},
        compiler_params=pltpu.CompilerParams(has_side_effects=EFFECT),
    )(*st["ps"], *st["lands"], st["send"], st["recv"], after)
    return list(outs[n:])


def _rows_tile(r):
    for t in (512, 256, 128, 64, 32, 16):
        if r % t == 0:
            return t
    return r


PAIR_SUM_STEPS = 4


def _pair_sum(gs, a_s, slots, name):
    n = len(gs)
    trs = [g.shape[1] // PAIR_SUM_STEPS for g in gs]

    def body(slots_ref, *refs):
        for g_ref, a_ref, p_ref in zip(refs[:n], refs[n:2 * n], refs[2 * n:]):
            p_ref[...] = (g_ref[...].astype(F32) + a_ref[...].astype(F32)).astype(BF)

    def spec(g, tr, index):
        return pl.BlockSpec((None, tr, g.shape[2]), index)

    return pl.pallas_call(
        body, name=name,
        grid_spec=pltpu.PrefetchScalarGridSpec(
            num_scalar_prefetch=1, grid=(3, PAIR_SUM_STEPS),
            in_specs=[spec(g, tr, lambda j, i, s: (s[j + 1], i, 0)) for g, tr in zip(gs, trs)]
            + [spec(g, tr, lambda j, i, s: (j + 1, i, 0)) for g, tr in zip(gs, trs)],
            out_specs=[spec(g, tr, lambda j, i, s: (j, i, 0)) for g, tr in zip(gs, trs)]),
        out_shape=[_sds((3,) + g.shape[1:], BF) for g in gs], compiler_params=_cp("parallel", "parallel"),
    )(slots, *gs, *a_s)


def _adamw_math(w, g, m, v):
    m = ADAM_B1 * m + (1.0 - ADAM_B1) * g
    v = ADAM_B2 * v + (1.0 - ADAM_B2) * (g * g)
    m_hat = m / (1.0 - ADAM_B1 ** ADAM_STEP)
    v_hat = v / (1.0 - ADAM_B2 ** ADAM_STEP)
    delta = -ADAM_LR * (m_hat / (jnp.sqrt(v_hat) + ADAM_EPS) + ADAM_WD * w)
    return delta, m, v


def _adamw_sharded(w, m, v, parts, slots, name, transposed=False):
    nl = w.shape[0]
    r, c = parts[0][0].shape[1:]
    tr = _rows_tile(r)
    if c * tr * 4 > (1 << 21) and not transposed:
        tr = max(8, tr // 2)

    def body(slots_ref, w_ref, m_ref, v_ref, *rest):
        part_refs, (g_out, d_out, m_out, v_out) = rest[:5 * nl], rest[5 * nl:]
        layer = pl.program_id(0)
        g = None
        for l in range(nl):
            s = part_refs[5 * l][...].astype(F32)
            for ref in part_refs[5 * l + 1:5 * l + 5]:
                s = s + ref[...].astype(F32)
            g = s if g is None else jnp.where(layer == l, s, g)
        if transposed:
            g = g.T
        delta, mn, vn = _adamw_math(w_ref[...], g, m_ref[...], v_ref[...])
        g_out[...] = g
        d_out[...] = delta
        m_out[...] = mn
        v_out[...] = vn

    def own(l):
        return lambda L, i, s: (s[0], jnp.where(L == l, i, 0), 0)

    def fixed(l, k):
        return lambda L, i, s: (k, jnp.where(L == l, i, 0), 0)

    if transposed:
        wspec = pl.BlockSpec((None, c, tr), lambda L, i, s: (L, 0, i))
    else:
        wspec = pl.BlockSpec((None, tr, c), lambda L, i, s: (L, i, 0))
    in_specs = [wspec, wspec, wspec]
    args = [w, m, v]
    for l, (g, a, b) in enumerate(parts):
        in_specs += [pl.BlockSpec((None, tr, c), own(l)), pl.BlockSpec((None, tr, c), fixed(l, 0)),
                     pl.BlockSpec((None, tr, c), fixed(l, 0)), pl.BlockSpec((None, tr, c), fixed(l, 1)),
                     pl.BlockSpec((None, tr, c), fixed(l, 2))]
        args += [g, a, b, b, b]
    return pl.pallas_call(
        body, name=name,
        grid_spec=pltpu.PrefetchScalarGridSpec(
            num_scalar_prefetch=1, grid=(nl, r // tr), in_specs=in_specs, out_specs=[wspec] * 4),
        out_shape=[_sds(w.shape, F32)] * 4, compiler_params=_cp("arbitrary", "arbitrary"),
    )(slots, *args)


def _adamw_small(w, m, v, gathered, name):
    def body(w_ref, m_ref, v_ref, gg_ref, g_out, d_out, m_out, v_out):
        g = gg_ref[0]
        for k in range(1, NDEV):
            g = g + gg_ref[k]
        delta, mn, vn = _adamw_math(w_ref[...], g, m_ref[...], v_ref[...])
        g_out[...] = g
        d_out[...] = delta
        m_out[...] = mn
        v_out[...] = vn

    return pl.pallas_call(body, name=name, out_shape=[_sds(w.shape, F32)] * 4)(w, m, v, gathered)


def _rmsnorm_fwd(x, g, name, deps=()):
    s, d = x.shape
    tm = 256

    def body(x_ref, g_ref, h_ref, ht_ref):
        xf = x_ref[...]
        y = xf * lax.rsqrt(jnp.mean(xf * xf, axis=-1, keepdims=True) + RMS_EPS)
        h = y * g_ref[...]
        h_ref[...] = h.astype(BF)
        ht_ref[...] = h.T.astype(BF)

    return pl.pallas_call(
        _ignore_deps(body, 2, len(deps)), name=name, grid=(s // tm,),
        in_specs=[pl.BlockSpec((tm, d), lambda i: (i, 0)), pl.BlockSpec((1, d), lambda i: (0, 0))] + [TOKEN_SPEC] * len(deps),
        out_specs=[pl.BlockSpec((tm, d), lambda i: (i, 0)), pl.BlockSpec((d, tm), lambda i: (0, i))],
        out_shape=[_sds((s, d), BF), _sds((d, s), BF)], compiler_params=_cp("parallel"),
    )(x, g, *deps)


def _rmsnorm_bwd(x, g, dh, dres, out_scale, name):
    s, d = x.shape
    tm = 256

    def body(x_ref, g_ref, dh_ref, dres_ref, dx_ref, dxb_ref, dxbt_ref, dg_ref):
        xf = x_ref[...]
        r = lax.rsqrt(jnp.mean(xf * xf, axis=-1, keepdims=True) + RMS_EPS)
        xhat = xf * r
        dhv = dh_ref[...]
        dxhat = dhv * g_ref[...]
        dx = dres_ref[...] + r * (dxhat - xhat * jnp.mean(dxhat * xhat, axis=-1, keepdims=True))
        dx_ref[...] = dx
        scaled = dx * out_scale
        dxb_ref[...] = scaled.astype(BF)
        dxbt_ref[...] = scaled.T.astype(BF)

        @pl.when(pl.program_id(0) == 0)
        def _():
            dg_ref[...] = jnp.zeros_like(dg_ref)

        dg_ref[...] += jnp.sum(dhv * xhat, axis=0, keepdims=True)

    row = pl.BlockSpec((tm, d), lambda i: (i, 0))
    vec = pl.BlockSpec((1, d), lambda i: (0, 0))
    return pl.pallas_call(
        body, name=name, grid=(s // tm,),
        in_specs=[row, vec, row, row],
        out_specs=[row, row, pl.BlockSpec((d, tm), lambda i: (0, i)), vec],
        out_shape=[_sds((s, d), F32), _sds((s, d), BF), _sds((d, s), BF), _sds((1, d), F32)],
        compiler_params=_cp("arbitrary"),
    )(x, g, dh, dres)


def _loss_head(x, g, target, name):
    s, d = x.shape
    tm = 256

    def body(x_ref, g_ref, t_ref, dx_ref, dxb_ref, dxbt_ref, dg_ref, loss_ref):
        xf = x_ref[...]
        r = lax.rsqrt(jnp.mean(xf * xf, axis=-1, keepdims=True) + RMS_EPS)
        xhat = xf * r
        err = xhat * g_ref[...] - t_ref[...]
        dy = err * (1.0 / d)
        dxhat = dy * g_ref[...]
        dx = r * (dxhat - xhat * jnp.mean(dxhat * xhat, axis=-1, keepdims=True))
        dx_ref[...] = dx
        half = dx * 0.5
        dxb_ref[...] = half.astype(BF)
        dxbt_ref[...] = half.T.astype(BF)

        @pl.when(pl.program_id(0) == 0)
        def _():
            dg_ref[...] = jnp.zeros_like(dg_ref)
            loss_ref[...] = jnp.zeros_like(loss_ref)

        dg_ref[...] += jnp.sum(dy * xhat, axis=0, keepdims=True)
        part = 0.5 * jnp.sum(jnp.mean(err * err, axis=-1, keepdims=True), axis=0, keepdims=True)
        lane = lax.broadcasted_iota(jnp.int32, (1, 128), 1)
        loss_ref[...] += jnp.where(lane == 0, part, 0.0)

    row = pl.BlockSpec((tm, d), lambda i: (i, 0))
    vec = pl.BlockSpec((1, d), lambda i: (0, 0))
    return pl.pallas_call(
        body, name=name, grid=(s // tm,),
        in_specs=[row, vec, row],
        out_specs=[row, row, pl.BlockSpec((d, tm), lambda i: (0, i)), vec, pl.BlockSpec((1, 128), lambda i: (0, 0))],
        out_shape=[_sds((s, d), F32), _sds((s, d), BF), _sds((d, s), BF), _sds((1, d), F32), _sds((1, 128), F32)],
        compiler_params=_cp("arbitrary"),
    )(x, g, target)


def _act_spec(tm, n, natural, order):
    if natural:
        return pl.BlockSpec((tm, n), (lambda s, i: (i, s)) if order == "si" else (lambda i, s: (i, s)))
    return pl.BlockSpec((None, tm, n), (lambda s, i: (s, i, 0)) if order == "si" else (lambda i, s: (s, i, 0)))


def _act_shape(s, n, natural, dtype):
    return _sds((s, NDEV * n), dtype) if natural else _sds((NDEV, s, n), dtype)


def _ffn_up(h, wg, wu, name):
    s, d = h.shape
    n = wg.shape[2]
    tm = 1024

    def body(h_ref, wg_ref, wu_ref, g_ref, u_ref, a_ref, at_ref):
        hb = h_ref[...]
        g = _dot(hb, wg_ref[...])
        u = _dot(hb, wu_ref[...])
        g_ref[...] = g.astype(BF)
        u_ref[...] = u.astype(BF)
        act = g * jax.nn.sigmoid(g) * u
        a_ref[...] = act.astype(BF)
        at_ref[...] = act.T.astype(BF)

    wsp = pl.BlockSpec((None, d, n), lambda s_, i: (s_, 0, 0))
    blk = _act_spec(tm, n, False, "si")
    return pl.pallas_call(
        body, name=name, grid=(NDEV, s // tm),
        in_specs=[pl.BlockSpec((tm, d), lambda s_, i: (i, 0)), wsp, wsp],
        out_specs=[blk] * 3 + [pl.BlockSpec((None, n, tm), lambda s_, i: (s_, 0, i))],
        out_shape=[_act_shape(s, n, False, BF)] * 3 + [_sds((NDEV, n, s), BF)],
        compiler_params=_cp("parallel", "parallel"),
    )(h, wg, wu)


def _ffn_down(act, wd, x, name):
    _, s, n = act.shape
    d = wd.shape[2]
    tm = 512

    def body(a_ref, w_ref, x_ref, o_ref):
        k = pl.program_id(1)
        t = _dot(a_ref[...], w_ref[...])

        @pl.when(k == 0)
        def _():
            o_ref[...] = t

        @pl.when(jnp.logical_and(k > 0, k < NDEV - 1))
        def _():
            o_ref[...] += t

        @pl.when(k == NDEV - 1)
        def _():
            o_ref[...] = x_ref[...] + 0.5 * (o_ref[...] + t)

    row = pl.BlockSpec((tm, d), lambda i, k: (i, 0))
    return pl.pallas_call(
        body, name=name, grid=(s // tm, NDEV),
        in_specs=[_act_spec(tm, n, False, "is"), pl.BlockSpec((None, n, d), lambda i, k: (k, 0, 0)), row],
        out_specs=row, out_shape=_sds((s, d), F32), compiler_params=_cp("parallel", "arbitrary"),
    )(act, wd, x)


def _ffn_bwd_act(dyb, wd, g, u, name, deps=()):
    s, d = dyb.shape
    n = wd.shape[1]
    tm = 1024

    def body(dy_ref, w_ref, g_ref, u_ref, dg_ref, du_ref):
        dact = _dot_nt(dy_ref[...], w_ref[...])
        gv = g_ref[...].astype(F32)
        uv = u_ref[...].astype(F32)
        sig = jax.nn.sigmoid(gv)
        dg_ref[...] = (dact * uv * (sig * (1.0 + gv * (1.0 - sig)))).astype(BF)
        du_ref[...] = (dact * (gv * sig)).astype(BF)

    blk = _act_spec(tm, n, False, "si")
    return pl.pallas_call(
        _ignore_deps(body, 4, len(deps)), name=name, grid=(NDEV, s // tm),
        in_specs=[pl.BlockSpec((tm, d), lambda s_, i: (i, 0)), pl.BlockSpec((None, n, d), lambda s_, i: (s_, 0, 0)), blk, blk]
        + [TOKEN_SPEC] * len(deps),
        out_specs=[blk, blk], out_shape=[_act_shape(s, n, False, BF)] * 2,
        compiler_params=_cp("parallel", "parallel"),
    )(dyb, wd, g, u, *deps)


def _grad_rows(act_t, dyb, name):
    _, n, s = act_t.shape
    d = dyb.shape[1]
    tn = 2048

    def body(a_ref, dy_ref, o_ref):
        o_ref[...] = _dot(a_ref[...], dy_ref[...]).astype(BF)

    return pl.pallas_call(
        body, name=name, grid=(NDEV, d // tn),
        in_specs=[pl.BlockSpec((None, n, s), lambda k, j: (k, 0, 0)), pl.BlockSpec((s, tn), lambda k, j: (0, j))],
        out_specs=pl.BlockSpec((None, n, tn), lambda k, j: (k, 0, j)), out_shape=_sds((NDEV, n, d), BF),
        compiler_params=_cp("parallel", "parallel"),
    )(act_t, dyb)


def _grad_cols(ht, dxs, naturals, name, deps=()):
    d, s = ht.shape
    k = len(dxs)
    ns = [dx.shape[1] // NDEV if nat else dx.shape[2] for dx, nat in zip(dxs, naturals)]
    td = 1024

    def body(*refs):
        ht_ref, dx_refs, o_refs = refs[0], refs[1:1 + k], refs[1 + k:]
        hv = ht_ref[...]
        for dx_ref, o_ref in zip(dx_refs, o_refs):
            o_ref[...] = _dot(hv, dx_ref[...]).astype(BF)

    def dx_spec(n, nat):
        if nat:
            return pl.BlockSpec((s, n), lambda s_, j: (0, s_))
        return pl.BlockSpec((None, s, n), lambda s_, j: (s_, 0, 0))

    return pl.pallas_call(
        _ignore_deps(body, 1 + k, len(deps)), name=name, grid=(NDEV, d // td),
        in_specs=[pl.BlockSpec((td, s), lambda s_, j: (j, 0))] + [dx_spec(n, nat) for n, nat in zip(ns, naturals)]
        + [TOKEN_SPEC] * len(deps),
        out_specs=[pl.BlockSpec((None, td, n), lambda s_, j: (s_, j, 0)) for n in ns],
        out_shape=[_sds((NDEV, d, n), BF) for n in ns], compiler_params=_cp("parallel", "parallel"),
    )(ht, *dxs, *deps)


def _dh_cols(dxs, ws, naturals, name, deps=()):
    k = len(dxs)
    d = ws[0].shape[1]
    ns = [w.shape[2] for w in ws]
    s = dxs[0].shape[0] if naturals[0] else dxs[0].shape[1]
    tm = 512

    def body(*refs):
        dx_refs, w_refs, o_ref = refs[:k], refs[k:2 * k], refs[2 * k]
        j = pl.program_id(1)
        t = _dot_nt(dx_refs[0][...], w_refs[0][...])
        for dx_ref, w_ref in zip(dx_refs[1:], w_refs[1:]):
            t = t + _dot_nt(dx_ref[...], w_ref[...])

        @pl.when(j == 0)
        def _():
            o_ref[...] = t

        @pl.when(j > 0)
        def _():
            o_ref[...] += t

    return pl.pallas_call(
        _ignore_deps(body, 2 * k, len(deps)), name=name, grid=(s // tm, NDEV),
        in_specs=[_act_spec(tm, n, nat, "is") for n, nat in zip(ns, naturals)]
        + [pl.BlockSpec((None, d, n), lambda i, j: (j, 0, 0)) for n in ns] + [TOKEN_SPEC] * len(deps),
        out_specs=pl.BlockSpec((tm, d), lambda i, j: (i, 0)), out_shape=_sds((s, d), F32),
        compiler_params=_cp("parallel", "arbitrary"),
    )(*dxs, *ws, *deps)


def _mm_nn(a, b, tn, out_dtype, name, res=None, tm=1024, deps=()):
    m, k = a.shape
    nn = b.shape[1]

    def body(*refs):
        if res is None:
            a_ref, b_ref, o_ref = refs
            o_ref[...] = _dot(a_ref[...], b_ref[...]).astype(out_dtype)
        else:
            a_ref, b_ref, r_ref, o_ref = refs
            o_ref[...] = (r_ref[...] + _dot(a_ref[...], b_ref[...])).astype(out_dtype)

    osp = pl.BlockSpec((tm, tn), lambda j, i: (i, j))
    in_specs = [pl.BlockSpec((tm, k), lambda j, i: (i, 0)), pl.BlockSpec((k, tn), lambda j, i: (0, j))]
    args = [a, b]
    if res is not None:
        in_specs.append(osp)
        args.append(res)
    return pl.pallas_call(
        _ignore_deps(body, len(args), len(deps)), name=name, grid=(nn // tn, m // tm),
        in_specs=in_specs + [TOKEN_SPEC] * len(deps), out_specs=osp,
        out_shape=_sds((m, nn), out_dtype), compiler_params=_cp("parallel", "parallel"),
    )(*args, *deps)


def _mm_nt(pairs, name, out_dtype=F32, tm=512, tk=512, deps=()):
    m = pairs[0][0].shape[0]
    kk = pairs[0][1].shape[0]
    p = len(pairs)

    def body(*refs):
        o_ref = refs[2 * p]
        t = _dot_nt(refs[0][...], refs[1][...])
        for q in range(1, p):
            t = t + _dot_nt(refs[2 * q][...], refs[2 * q + 1][...])
        o_ref[...] = t.astype(out_dtype)

    in_specs, args = [], []
    for a, b in pairs:
        in_specs += [pl.BlockSpec((tm, a.shape[1]), lambda j, i: (i, 0)), pl.BlockSpec((tk, b.shape[1]), lambda j, i: (j, 0))]
        args += [a, b]
    return pl.pallas_call(
        _ignore_deps(body, 2 * p, len(deps)), name=name, grid=(kk // tk, m // tm), in_specs=in_specs + [TOKEN_SPEC] * len(deps),
        out_specs=pl.BlockSpec((tm, tk), lambda j, i: (i, j)), out_shape=_sds((m, kk), out_dtype),
        compiler_params=_cp("parallel", "parallel"),
    )(*args, *deps)


def _rope_tables(s, sign):
    half = ROPE_DIMS // 2
    f32 = np.float32
    freqs = f32(ROPE_THETA) ** (-np.arange(half, dtype=f32) / f32(half))
    ang = np.arange(s, dtype=f32)[:, None] * freqs[None, :]
    cos, sin = np.cos(ang).astype(f32), (sign * np.sin(ang)).astype(f32)
    one = np.ones((s, HD - ROPE_DIMS), f32)
    zero = np.zeros((s, HD - ROPE_DIMS), f32)
    zh = np.zeros((s, half), f32)
    c = np.concatenate([cos, cos, one], axis=1)
    sa = np.concatenate([-sin, zh, zero], axis=1)
    sb = np.concatenate([zh, sin, zero], axis=1)
    return jnp.asarray(c), jnp.asarray(sa), jnp.asarray(sb)


def _rope(xv, c, sa, sb):
    return xv * c + pltpu.roll(xv, HD - ROPE_DIMS // 2, 1) * sa + pltpu.roll(xv, ROPE_DIMS // 2, 1) * sb


def _qkv_rope(h, w, tables, name):
    s, d = h.shape
    n = w.shape[2]
    per = n // HD
    tm = 1024

    def body(h_ref, w_ref, c_ref, sa_ref, sb_ref, o_ref):
        shard = pl.program_id(0)
        y = _dot(h_ref[...], w_ref[...])
        c, sa, sb = c_ref[...], sa_ref[...], sb_ref[...]
        for j in range(per):
            blk = y[:, j * HD:(j + 1) * HD]
            rot = _rope(blk, c, sa, sb)
            is_qk = shard * per + j < 2 * N_HEADS
            o_ref[:, j * HD:(j + 1) * HD] = jnp.where(is_qk, rot, blk).astype(BF)

    tab = pl.BlockSpec((tm, HD), lambda s_, i: (i, 0))
    return pl.pallas_call(
        body, name=name, grid=(NDEV, s // tm),
        in_specs=[pl.BlockSpec((tm, d), lambda s_, i: (i, 0)), pl.BlockSpec((None, d, n), lambda s_, i: (s_, 0, 0)), tab, tab, tab],
        out_specs=pl.BlockSpec((tm, n), lambda s_, i: (i, s_)), out_shape=_sds((s, NDEV * n), BF),
        compiler_params=_cp("parallel", "parallel"),
    )(h, w, *tables)


def _iota2():
    return (lax.broadcasted_iota(jnp.int32, (QB, QB), 0), lax.broadcasted_iota(jnp.int32, (QB, QB), 1))


def _softplus(z):
    return jnp.maximum(z, 0.0) + jnp.log(1.0 + jnp.exp(-jnp.abs(z)))


def _tri_dot(xv, tri, left=False):
    hi = xv.astype(BF)
    r1 = xv - hi.astype(F32)
    mid = r1.astype(BF)
    lo = (r1 - mid.astype(F32)).astype(BF)
    if left:
        return _dot(tri, hi) + _dot(tri, mid) + _dot(tri, lo)
    return _dot(hi, tri) + _dot(mid, tri) + _dot(lo, tri)


def _col(ref_or_val):
    return ref_or_val[:, 0:1]


KT = 4 * QB
QQ = 2 * QB


def _iota_tile():
    return (lax.broadcasted_iota(jnp.int32, (QQ, KT), 0), lax.broadcasted_iota(jnp.int32, (QQ, KT), 1))


def _scan_matrix(keep):
    tri = keep(*_iota2()).astype(BF)
    return jnp.concatenate([tri, tri], axis=0)


def _scan_dot(xv, tri2):
    hi = xv.astype(BF)
    lo = (xv - hi.astype(F32)).astype(BF)
    return _dot(jnp.concatenate([hi, lo], axis=1), tri2)


def _blocks(xv):
    return [xv[:, b * QB:(b + 1) * QB] for b in range(KT // QB)]


def _sb_fwd(qkv, name):
    s = qkv.shape[0]
    nb = s // QB

    def body(q_ref, k_ref, v_ref, o_ref, ot_ref, t_ref):
        i = pl.program_id(1)
        q = q_ref[...]
        row, col = _iota_tile()
        later_keys = _scan_matrix(lambda j, s_: j > s_)
        last = (i * QQ + QQ - 1) // KT

        def step(tt, carry):
            acc, later = carry
            t = last - tt
            off = pl.multiple_of(t * KT, KT)
            k = k_ref[pl.ds(off, KT), :]
            v = v_ref[pl.ds(off, KT), :]
            z = _dot_nt(q, k) * SCALE
            strict = row + (i * QQ - t * KT) > col
            sp = _softplus(z)
            lnb = jnp.where(strict, -sp, 0.0)
            afters = []
            for xb in reversed(_blocks(lnb)):
                afters.append(later + _scan_dot(xb, later_keys))
                later = later + jnp.sum(xb, axis=1, keepdims=True)
            after = jnp.concatenate(afters[::-1], axis=1)
            w = jnp.where(strict, jnp.exp((z - sp) + after), 0.0)
            return acc + _dot(w.astype(BF), v), later

        acc, total = lax.fori_loop(0, last + 1, step, (jnp.zeros((QQ, HD), F32), jnp.zeros((QQ, 1), F32)))
        o_ref[...] = acc.astype(BF)
        ot_ref[...] = acc.T.astype(BF)
        t_ref[...] = jnp.broadcast_to(total, (QQ, HD))

    blk = pl.BlockSpec((QQ, HD), lambda h, i: (i, h))
    return pl.pallas_call(
        body, name=name, grid=(N_SB, s // QQ),
        in_specs=[blk, pl.BlockSpec((s, HD), lambda h, i: (0, N_HEADS + h)), pl.BlockSpec((s, HD), lambda h, i: (0, 2 * N_HEADS + h))],
        out_specs=[blk, pl.BlockSpec((HD, QQ), lambda h, i: (h, i)), blk],
        out_shape=[_sds((s, N_SB * HD), BF), _sds((N_SB * HD, s), BF), _sds((s, N_SB * HD), F32)],
        compiler_params=_cp("parallel", "parallel"),
    )(qkv, qkv, qkv)


def _sb_bwd(qkv, do, total, name):
    s = qkv.shape[0]
    nb = s // QB

    def body(q_ref, k_ref, v_ref, do_ref, t_ref, dq_ref, dk_ref, dv_ref, dk_acc, dv_acc):
        i = pl.program_id(1)

        @pl.when(i == 0)
        def _():
            dk_acc[...] = jnp.zeros_like(dk_acc)
            dv_acc[...] = jnp.zeros_like(dv_acc)

        q = q_ref[...]
        dov = do_ref[...]
        tot = _col(t_ref[...])
        row, col = _iota_tile()
        keys_upto = _scan_matrix(lambda j, s_: j <= s_)
        keys_before = _scan_matrix(lambda j, s_: j < s_)

        def step(t, carry):
            dq, lnb_before, dl_before = carry
            off = pl.multiple_of(t * KT, KT)
            k = k_ref[pl.ds(off, KT), :]
            v = v_ref[pl.ds(off, KT), :]
            z = _dot_nt(q, k) * SCALE
            strict = row + (i * QQ - t * KT) > col
            sp = _softplus(z)
            lnb = jnp.where(strict, -sp, 0.0)
            afters = []
            for xb in _blocks(lnb):
                afters.append(tot - (lnb_before + _scan_dot(xb, keys_upto)))
                lnb_before = lnb_before + jnp.sum(xb, axis=1, keepdims=True)
            a = jnp.where(strict, jnp.exp((z - sp) + jnp.concatenate(afters, axis=1)), 0.0)
            dl = a * _dot_nt(dov, v)
            befores = []
            for xb in _blocks(dl):
                befores.append(dl_before + _scan_dot(xb, keys_before))
                dl_before = dl_before + jnp.sum(xb, axis=1, keepdims=True)
            sig = jnp.exp(z - sp)
            dz = jnp.where(strict, dl * (1.0 - sig) - sig * jnp.concatenate(befores, axis=1), 0.0) * SCALE
            dq = dq + _dot(dz.astype(BF), k)
            dk_acc[pl.ds(off, KT), :] += _dot(dz.T.astype(BF), q)
            dv_acc[pl.ds(off, KT), :] += _dot(a.T.astype(BF), dov)
            return dq, lnb_before, dl_before

        zero = jnp.zeros((QQ, 1), F32)
        dq, _, _ = lax.fori_loop(0, (i * QQ + QQ - 1) // KT + 1, step, (jnp.zeros((QQ, HD), F32), zero, zero))
        dq_ref[...] = dq.astype(BF)

        @pl.when(i == s // QQ - 1)
        def _():
            dk_ref[...] = dk_acc[...].astype(BF)
            dv_ref[...] = dv_acc[...].astype(BF)

    blk = pl.BlockSpec((QQ, HD), lambda h, i: (i, h))
    full = pl.BlockSpec((s, HD), lambda h, i: (0, h))
    return pl.pallas_call(
        body, name=name, grid=(N_SB, s // QQ),
        in_specs=[blk, pl.BlockSpec((s, HD), lambda h, i: (0, N_HEADS + h)), pl.BlockSpec((s, HD), lambda h, i: (0, 2 * N_HEADS + h)), blk, blk],
        out_specs=[blk, full, full], out_shape=[_sds((s, N_SB * HD), BF)] * 3,
        scratch_shapes=[pltpu.VMEM((s, HD), F32), pltpu.VMEM((s, HD), F32)],
        compiler_params=_cp("parallel", "arbitrary"),
    )(qkv, qkv, qkv, do, total)


def _fgate_fwd(f, b, name):
    s = f.shape[0]
    nb = s // QB
    nfox = N_HEADS - N_SB

    def body(f_ref, b_ref, cb_ref, ct_ref):
        row, col = _iota2()
        upto = (row >= col).astype(BF)
        carry = jnp.zeros((1, HD), F32)
        for blk in range(nb):
            xv = f_ref[blk * QB:(blk + 1) * QB, :] + b_ref[...]
            logf = -_softplus(-xv)
            cum = _tri_dot(logf, upto, left=True) + carry
            carry = cum[QB - 1:QB, :]
            ct_ref[blk] = cum.T
            for h in range(nfox):
                cb_ref[blk * QB:(blk + 1) * QB, h * HD:(h + 1) * HD] = jnp.broadcast_to(cum[:, h:h + 1], (QB, HD))

    return pl.pallas_call(
        body, name=name, out_shape=[_sds((s, nfox * HD), F32), _sds((nb, HD, HD), F32)], compiler_params=_cp(),
    )(f, b)


def _fgate_bwd(dcq, dck, f, b, name):
    s = f.shape[0]
    nb = s // QB
    nfox = N_HEADS - N_SB

    def body(dcq_ref, dck_ref, f_ref, b_ref, df_ref, db_ref):
        row, col = _iota2()
        from_tri = (row <= col).astype(BF)
        lane = col
        carry = jnp.zeros((1, HD), F32)
        db = jnp.zeros((1, HD), F32)
        for blk in reversed(range(nb)):
            dcum = jnp.zeros((QB, HD), F32)
            for h in range(nfox):
                here = (slice(blk * QB, (blk + 1) * QB), slice(h * HD, (h + 1) * HD))
                dcum = jnp.where(lane == h, dcq_ref[here] - dck_ref[here], dcum)
            dlogf = _tri_dot(dcum, from_tri, left=True) + carry
            carry = dlogf[0:1, :]
            xv = f_ref[blk * QB:(blk + 1) * QB, :] + b_ref[...]
            sp = _softplus(xv)
            df = jnp.where(lane < nfox, dlogf * jnp.exp(-sp), 0.0)
            df_ref[blk * QB:(blk + 1) * QB, :] = df.astype(BF)
            db = db + jnp.sum(df, axis=0, keepdims=True)
        db_ref[...] = db

    return pl.pallas_call(
        body, name=name, out_shape=[_sds((s, HD), BF), _sds((1, HD), F32)], compiler_params=_cp(),
    )(dcq, dck, f, b)


def _fox_head_row(ct_ref, j, h):
    tile = ct_ref[j]
    sub = lax.broadcasted_iota(jnp.int32, tile.shape, 0)
    return jnp.sum(jnp.where(sub == h, tile, 0.0), axis=0, keepdims=True)


def _fox_tile_row(ct_ref, t, h):
    nsub = KT // QB
    return jnp.concatenate([_fox_head_row(ct_ref, t * nsub + b, h) for b in range(nsub)], axis=1)


def _fox_fwd(qkv, cum_b, cum_t, name):
    s = qkv.shape[0]
    nb = s // QB
    nfox = N_HEADS - N_SB

    def body(q_ref, k_ref, v_ref, cq_ref, ct_ref, o_ref, ot_ref, lse_ref):
        h, i = pl.program_id(0), pl.program_id(1)
        q = q_ref[...]
        cq = _col(cq_ref[...])
        row, col = _iota_tile()

        def step(t, carry):
            acc, m, l = carry
            off = pl.multiple_of(t * KT, KT)
            k = k_ref[pl.ds(off, KT), :]
            v = v_ref[pl.ds(off, KT), :]
            z = _dot_nt(q, k) * SCALE + cq - _fox_tile_row(ct_ref, t, h)
            z = jnp.where(row + (i * QQ - t * KT) >= col, z, NEG_INF)
            m_new = jnp.maximum(m, jnp.max(z, axis=1, keepdims=True))
            alpha = jnp.exp(m - m_new)
            p = jnp.exp(z - m_new)
            l = alpha * l + jnp.sum(p, axis=1, keepdims=True)
            acc = alpha * acc + _dot(p.astype(BF), v)
            return acc, m_new, l

        acc, m, l = lax.fori_loop(0, (i * QQ + QQ - 1) // KT + 1, step,
                                  (jnp.zeros((QQ, HD), F32), jnp.full((QQ, 1), NEG_INF, F32), jnp.zeros((QQ, 1), F32)))
        o = acc / l
        o_ref[...] = o.astype(BF)
        ot_ref[...] = o.T.astype(BF)
        lse_ref[...] = jnp.broadcast_to(m + jnp.log(l), (QQ, HD))

    blk = pl.BlockSpec((QQ, HD), lambda h, i: (i, h))
    return pl.pallas_call(
        body, name=name, grid=(nfox, s // QQ),
        in_specs=[pl.BlockSpec((QQ, HD), lambda h, i: (i, N_SB + h)),
                  pl.BlockSpec((s, HD), lambda h, i: (0, N_HEADS + N_SB + h)),
                  pl.BlockSpec((s, HD), lambda h, i: (0, 2 * N_HEADS + N_SB + h)),
                  blk, pl.BlockSpec((nb, 8, HD), lambda h, i: (0, 0, 0))],
        out_specs=[blk, pl.BlockSpec((HD, QQ), lambda h, i: (h, i)), blk],
        out_shape=[_sds((s, nfox * HD), BF), _sds((nfox * HD, s), BF), _sds((s, nfox * HD), F32)],
        compiler_params=_cp("parallel", "parallel"),
    )(qkv, qkv, qkv, cum_b, cum_t)


def _fox_bwd(qkv, cum_b, cum_t, o, lse, do, name):
    s = qkv.shape[0]
    nb = s // QB
    nfox = N_HEADS - N_SB

    def body(q_ref, k_ref, v_ref, cq_ref, ct_ref, o_ref, lse_ref, do_ref, dq_ref, dk_ref, dv_ref, dcq_ref, dc_ref, dk_acc, dv_acc, dc_acc):
        h, i = pl.program_id(0), pl.program_id(1)

        @pl.when(i == 0)
        def _():
            dk_acc[...] = jnp.zeros_like(dk_acc)
            dv_acc[...] = jnp.zeros_like(dv_acc)
            dc_acc[...] = jnp.zeros_like(dc_acc)

        q = q_ref[...]
        cq = _col(cq_ref[...])
        dov = do_ref[...]
        lse_c = _col(lse_ref[...])
        delta = jnp.sum(dov.astype(F32) * o_ref[...].astype(F32), axis=1, keepdims=True)
        row, col = _iota_tile()
        ones = jnp.ones((QQ, HD), BF)

        def step(t, carry):
            dq, over_keys = carry
            off = pl.multiple_of(t * KT, KT)
            k = k_ref[pl.ds(off, KT), :]
            v = v_ref[pl.ds(off, KT), :]
            z = _dot_nt(q, k) * SCALE + cq - _fox_tile_row(ct_ref, t, h)
            p = jnp.where(row + (i * QQ - t * KT) >= col, jnp.exp(z - lse_c), 0.0)
            dz = p * (_dot_nt(dov, v) - delta)
            dzt = dz.T
            dq = dq + _dot((dz * SCALE).astype(BF), k)
            dk_acc[pl.ds(off, KT), :] += _dot((dzt * SCALE).astype(BF), q)
            dv_acc[pl.ds(off, KT), :] += _dot(p.T.astype(BF), dov)
            dc_acc[pl.ds(off, KT), :] += _tri_dot(dzt, ones)
            return dq, over_keys + jnp.sum(dz, axis=1, keepdims=True)

        dq, over_keys = lax.fori_loop(0, (i * QQ + QQ - 1) // KT + 1, step, (jnp.zeros((QQ, HD), F32), jnp.zeros((QQ, 1), F32)))
        dq_ref[...] = dq.astype(BF)
        dcq_ref[...] = jnp.broadcast_to(over_keys, (QQ, HD))

        @pl.when(i == s // QQ - 1)
        def _():
            dk_ref[...] = dk_acc[...].astype(BF)
            dv_ref[...] = dv_acc[...].astype(BF)
            dc_ref[...] = dc_acc[...]

    blk = pl.BlockSpec((QQ, HD), lambda h, i: (i, h))
    full = pl.BlockSpec((s, HD), lambda h, i: (0, h))
    return pl.pallas_call(
        body, name=name, grid=(nfox, s // QQ),
        in_specs=[pl.BlockSpec((QQ, HD), lambda h, i: (i, N_SB + h)),
                  pl.BlockSpec((s, HD), lambda h, i: (0, N_HEADS + N_SB + h)),
                  pl.BlockSpec((s, HD), lambda h, i: (0, 2 * N_HEADS + N_SB + h)),
                  blk, pl.BlockSpec((nb, 8, HD), lambda h, i: (0, 0, 0)), blk, blk,
                  pl.BlockSpec((QQ, HD), lambda h, i: (i, N_SB + h))],
        out_specs=[blk, full, full, blk, full],
        out_shape=[_sds((s, nfox * HD), BF)] * 3 + [_sds((s, nfox * HD), F32)] * 2,
        scratch_shapes=[pltpu.VMEM((s, HD), F32)] * 3,
        compiler_params=_cp("parallel", "arbitrary"),
    )(qkv, qkv, qkv, cum_b, cum_t, o, lse, do)


GB = 4
DIL_PAD = QB * 16


def _dil_group(g, d, off=0, shift=0):
    if d == 1:
        return [pl.ds(pl.multiple_of(off + (g * GB + shift) * QB, QB), GB * QB)]
    if d == 4:
        return [pl.ds(off + g + shift * QB * d, GB * QB, stride=d)]
    assert d == 16 and shift == 0
    return [pl.ds(off + g * GB + b, QB, stride=d) for b in range(GB)]


def _dil_load(ref, g, d, off=0, shift=0):
    parts = [ref[sl, :] for sl in _dil_group(g, d, off, shift)]
    rows = parts[0] if len(parts) == 1 else jnp.concatenate(parts, axis=0)
    return rows.reshape(GB, QB, HD)


def _dil_store(ref, g, d, val, off=0, shift=0, add=False):
    rows = val.reshape(GB * QB, HD)
    slices = _dil_group(g, d, off, shift)
    for b, sl in enumerate(slices):
        piece = rows if len(slices) == 1 else rows[b * QB:(b + 1) * QB]
        if add:
            ref[sl, :] += piece
        else:
            ref[sl, :] = piece


def _bdot_nt(a, b):
    return lax.dot_general(a, b, (((2,), (2,)), ((0,), (0,))), preferred_element_type=F32)


def _bdot(a, b):
    return lax.dot_general(a, b, (((2,), (1,)), ((0,), (0,))), preferred_element_type=F32)


def _bdot_tn(a, b):
    return lax.dot_general(jnp.swapaxes(a, 1, 2).astype(BF), b, (((2,), (1,)), ((0,), (0,))), preferred_element_type=F32)


def _dil_masks(g, d, nb):
    row = lax.broadcasted_iota(jnp.int32, (GB, QB, QB), 1)
    col = lax.broadcasted_iota(jnp.int32, (GB, QB, QB), 2)
    blk = lax.broadcasted_iota(jnp.int32, (GB, QB, QB), 0) + (g * GB if d == 1 else 0)
    return col <= row, jnp.logical_and(col >= row, blk >= 1) if nb > 1 else None


def _dilated_fwd(qkv, name):
    s = qkv.shape[0]
    npat = len(DILATED_PATTERNS)
    chunk = 256

    def body(q_ref, k_ref, v_ref, out_ref, outt_ref, g_ref, qf, kf, vf, *per_pattern):
        o_s, l_s = per_pattern[:npat], per_pattern[npat:]
        qf[...] = q_ref[...].astype(F32)
        for dst, src in ((kf, k_ref), (vf, v_ref)):
            dst[0:DIL_PAD, :] = jnp.zeros((DIL_PAD, HD), F32)
            dst[DIL_PAD:, :] = src[...].astype(F32)
        for p, (_, d) in enumerate(DILATED_PATTERNS):
            nb = s // d // QB

            def grp(g, carry, p=p, d=d, nb=nb):
                mc, mp = _dil_masks(g, d, nb)
                q = _dil_load(qf, g, d).astype(BF)
                zc = jnp.where(mc, _bdot_nt(q, _dil_load(kf, g, d, DIL_PAD).astype(BF)) * SCALE, NEG_INF)
                m = jnp.max(zc, axis=2, keepdims=True)
                if nb > 1:
                    zp = jnp.where(mp, _bdot_nt(q, _dil_load(kf, g, d, DIL_PAD, -1).astype(BF)) * SCALE, NEG_INF)
                    m = jnp.maximum(m, jnp.max(zp, axis=2, keepdims=True))
                ec = jnp.exp(zc - m)
                l = jnp.sum(ec, axis=2, keepdims=True)
                if nb > 1:
                    ep = jnp.where(mp, jnp.exp(zp - m), 0.0)
                    l = l + jnp.sum(ep, axis=2, keepdims=True)
                o = _bdot((ec / l).astype(BF), _dil_load(vf, g, d, DIL_PAD).astype(BF))
                if nb > 1:
                    o = o + _bdot((ep / l).astype(BF), _dil_load(vf, g, d, DIL_PAD, -1).astype(BF))
                _dil_store(o_s[p], g, d, o)
                _dil_store(l_s[p], g, d, jnp.broadcast_to(m + jnp.log(l), (GB, QB, HD)))
                return carry

            lax.fori_loop(0, s // (QB * GB), grp, 0)
        for c0 in range(0, s, chunk):
            rows = slice(c0, c0 + chunk)
            ls = [l_s[p][rows, :] for p in range(npat)]
            m = functools.reduce(jnp.maximum, ls)
            es = [jnp.exp(l - m) for l in ls]
            tot = functools.reduce(lambda a, b: a + b, es)
            out = functools.reduce(lambda a, b: a + b, [(e / tot) * o_s[p][rows, :] for p, e in enumerate(es)])
            out_ref[rows, :] = out.astype(BF)
            outt_ref[:, rows] = out.T.astype(BF)
            g_ref[rows, :] = m + jnp.log(tot)

    full = pl.BlockSpec((s, HD), lambda h: (0, h))
    return pl.pallas_call(
        body, name=name, grid=(N_HEADS,),
        in_specs=[full, pl.BlockSpec((s, HD), lambda h: (0, N_HEADS + h)), pl.BlockSpec((s, HD), lambda h: (0, 2 * N_HEADS + h))],
        out_specs=[full, pl.BlockSpec((HD, s), lambda h: (h, 0)), full],
        out_shape=[_sds((s, N_HEADS * HD), BF), _sds((N_HEADS * HD, s), BF), _sds((s, N_HEADS * HD), F32)],
        scratch_shapes=[pltpu.VMEM((s, HD), F32)] + [pltpu.VMEM((s + DIL_PAD, HD), F32)] * 2 + [pltpu.VMEM((s, HD), F32)] * (2 * npat),
        compiler_params=_cp("parallel"),
    )(qkv, qkv, qkv)


def _dilated_bwd(qkv, out, glse, do, tables, name):
    s = qkv.shape[0]
    chunk = 256

    def body(q_ref, k_ref, v_ref, out_ref, g_ref, do_ref, c_ref, sa_ref, sb_ref, dq_ref, dk_ref, dv_ref,
             qf, kf, vf, dof, dl_s, dq_a, dk_a, dv_a):
        qf[...] = q_ref[...].astype(F32)
        for dst, src in ((kf, k_ref), (vf, v_ref)):
            dst[0:DIL_PAD, :] = jnp.zeros((DIL_PAD, HD), F32)
            dst[DIL_PAD:, :] = src[...].astype(F32)
        for c0 in range(0, s, chunk):
            rows = slice(c0, c0 + chunk)
            dov = do_ref[rows, :].astype(F32)
            dof[rows, :] = dov
            dl_s[rows, :] = jnp.broadcast_to(jnp.sum(dov * out_ref[rows, :].astype(F32), axis=1, keepdims=True), (chunk, HD))
        dq_a[...] = jnp.zeros_like(dq_a)
        dk_a[...] = jnp.zeros_like(dk_a)
        dv_a[...] = jnp.zeros_like(dv_a)
        for _, d in DILATED_PATTERNS:
            nb = s // d // QB

            def grp(g, carry, d=d, nb=nb):
                mc, mp = _dil_masks(g, d, nb)
                q = _dil_load(qf, g, d).astype(BF)
                kc = _dil_load(kf, g, d, DIL_PAD).astype(BF)
                dov = _dil_load(dof, g, d).astype(BF)
                lse = _dil_load(g_ref, g, d)[:, :, 0:1]
                delta = _dil_load(dl_s, g, d)[:, :, 0:1]
                pc = jnp.where(mc, jnp.exp(_bdot_nt(q, kc) * SCALE - lse), 0.0)
                dzc = pc * (_bdot_nt(dov, _dil_load(vf, g, d, DIL_PAD).astype(BF)) - delta) * SCALE
                dq = _bdot(dzc.astype(BF), kc)
                if nb > 1:
                    kp = _dil_load(kf, g, d, DIL_PAD, -1).astype(BF)
                    pp = jnp.where(mp, jnp.exp(_bdot_nt(q, kp) * SCALE - lse), 0.0)
                    dzp = pp * (_bdot_nt(dov, _dil_load(vf, g, d, DIL_PAD, -1).astype(BF)) - delta) * SCALE
                    dq = dq + _bdot(dzp.astype(BF), kp)
                _dil_store(dq_a, g, d, dq, add=True)
                _dil_store(dk_a, g, d, _bdot_tn(dzc, q), DIL_PAD, add=True)
                _dil_store(dv_a, g, d, _bdot_tn(pc, dov), DIL_PAD, add=True)
                if nb > 1:
                    _dil_store(dk_a, g, d, _bdot_tn(dzp, q), DIL_PAD, -1, add=True)
                    _dil_store(dv_a, g, d, _bdot_tn(pp, dov), DIL_PAD, -1, add=True)
                return carry

            lax.fori_loop(0, s // (QB * GB), grp, 0)
        for c0 in range(0, s, chunk):
            rows = slice(c0, c0 + chunk)
            padded = slice(DIL_PAD + c0, DIL_PAD + c0 + chunk)
            c, sa, sb = c_ref[rows, :], sa_ref[rows, :], sb_ref[rows, :]
            dq_ref[rows, :] = _rope(dq_a[rows, :], c, sa, sb).astype(BF)
            dk_ref[rows, :] = _rope(dk_a[padded, :], c, sa, sb).astype(BF)
            dv_ref[rows, :] = dv_a[padded, :].astype(BF)

    full = pl.BlockSpec((s, HD), lambda h: (0, h))
    tab = pl.BlockSpec((s, HD), lambda h: (0, 0))
    return pl.pallas_call(
        body, name=name, grid=(N_HEADS,),
        in_specs=[full, pl.BlockSpec((s, HD), lambda h: (0, N_HEADS + h)), pl.BlockSpec((s, HD), lambda h: (0, 2 * N_HEADS + h)),
                  full, full, full, tab, tab, tab],
        out_specs=[full, full, full], out_shape=[_sds((s, N_HEADS * HD), BF)] * 3,
        scratch_shapes=[pltpu.VMEM((s, HD), F32)] + [pltpu.VMEM((s + DIL_PAD, HD), F32)] * 2 + [pltpu.VMEM((s, HD), F32)] * 3
        + [pltpu.VMEM((s + DIL_PAD, HD), F32)] * 2,
        compiler_params=_cp("parallel"),
    )(qkv, qkv, qkv, out, glse, do, *tables)


def _swiglu_fwd(x, gnorm, w, tag, deps=()):
    h, ht = _rmsnorm_fwd(x, gnorm, f"norm_{tag}", deps)
    g, u, act, act_t = _ffn_up(h, w["gate"], w["up"], f"ffn_up_{tag}")
    if callable(w["down"]):
        w["down"] = w["down"](g)
    y = _ffn_down(act, w["down"], x, f"ffn_down_{tag}")
    return y, (x, ht, g, u, act_t)


def _swiglu_bwd(saved, gnorm, w, dy, dyb_half, out_scale, tag, deps=(), on_down=None, on_grads=None):
    x, ht, g, u, act_t = saved
    dg, du = _ffn_bwd_act(dyb_half, w["down"], g, u, f"ffn_bwd_act_{tag}", deps)
    d_down = _grad_rows(act_t, dyb_half, f"ffn_bwd_wd_{tag}")
    tokens = list(on_down(d_down)) if on_down else []
    d_gate, d_up = _grad_cols(ht, [dg, du], [False, False], f"ffn_bwd_wgu_{tag}", tokens)
    gw = {"gate": d_gate, "up": d_up, "down": d_down}
    tokens = list(on_grads(gw)) if on_grads else []
    dh = _dh_cols([dg, du], [w["gate"], w["up"]], [False, False], f"ffn_bwd_dh_{tag}", tokens)
    dx, dxb, dxbt, dgn = _rmsnorm_bwd(x, gnorm, dh, dy, out_scale, f"norm_bwd_{tag}")
    return (dx, dxb, dxbt), dgn, gw


def kernel(x, norm_g, ffn1_w_gate, ffn1_w_up, ffn1_w_down, ffn2_w_gate, ffn2_w_up, ffn2_w_down, even_w_in, even_b_forget, even_w_out, odd_w_qkv, odd_w_out, final_norm_g, loss_target, m_norm_g, m_ffn1_w_gate, m_ffn1_w_up, m_ffn1_w_down, m_ffn2_w_gate, m_ffn2_w_up, m_ffn2_w_down, m_even_w_in, m_even_b_forget, m_even_w_out, m_odd_w_qkv, m_odd_w_out, m_final_norm_g, v_norm_g, v_ffn1_w_gate, v_ffn1_w_up, v_ffn1_w_down, v_ffn2_w_gate, v_ffn2_w_up, v_ffn2_w_down, v_even_w_in, v_even_b_forget, v_even_w_out, v_odd_w_qkv, v_odd_w_out, v_final_norm_g):
    s, d = x.shape[1], x.shape[2]
    nfox = N_HEADS - N_SB
    ax, ay, ac = lax.axis_index("x"), lax.axis_index("y"), lax.axis_index("c")
    me = 4 * ax + 2 * ay + ac
    slots = jnp.stack([4 * px + 2 * py + ac for px, py in [(ax, ay), (1 - ax, ay), (ax, 1 - ay), (1 - ax, 1 - ay)]]).astype(jnp.int32)
    x0 = x.reshape(s, d)
    target = loss_target.reshape(s, d)

    def bf(w):
        return w.astype(BF)

    groups = [
        [bf(ffn1_w_gate[0]), bf(ffn1_w_up[0]), norm_g.reshape(6, d // NDEV)],
        [bf(even_w_in[0]), bf(even_w_out[0])],
        [bf(ffn2_w_gate[0]), bf(ffn2_w_up[0]), bf(ffn2_w_down[0])],
        [bf(ffn1_w_gate[1]), bf(ffn1_w_up[1]), bf(ffn1_w_down[1])],
        [bf(odd_w_qkv[0]), bf(odd_w_out[0])],
        [bf(ffn2_w_gate[1]), bf(ffn2_w_up[1]), bf(ffn2_w_down[1])],
        [bf(ffn1_w_down[0])],
    ]
    started = [None] * len(groups)
    last_token = []
    for k in (0, 6, 1, 2, 3, 4, 5):
        started[k] = _gather_start(groups[k], me, last_token, f"gather_start_{k}")
        last_token = [started[k]["token"]]
    all_started = last_token

    def forward_early(k, after):
        started[k] = _gather_forward(started[k], after, f"gather_forward_{k}")
        return [started[k]["token"]]

    def gathered(k, after):
        st = started[k] if "send2" in started[k] else _gather_forward(started[k], after, f"gather_forward_{k}")
        return _gather_finish(st, after, f"gather_finish_{k}")

    def ffn_weights(ws_):
        return {"gate": ws_[0], "up": ws_[1], "down": ws_[2]}

    b_pad = jnp.pad(even_b_forget, ((0, 0), (0, HD - nfox)))
    gfin = final_norm_g.reshape(1, d)

    g0 = gathered(0, x0)
    gn = jnp.transpose(g0[2], (1, 0, 2)).reshape(6, 1, d)
    wf = [[{"gate": g0[0], "up": g0[1], "down": lambda after: gathered(6, after)[0]}, None], [None, None]]
    x1, sv_f1_0 = _swiglu_fwd(x0, gn[0], wf[0][0], "l0a", all_started)
    g1 = gathered(1, x1)
    w_in_nat = jnp.transpose(g1[0], (1, 0, 2)).reshape(d, -1)
    w_qkv_e = w_in_nat[:, :3 * d]
    w_f = jnp.pad(w_in_nat[:, 3 * d:], ((0, 0), (0, HD - nfox)))
    w_out_e = g1[1].reshape(d, d)
    h_e, ht_e = _rmsnorm_fwd(x1, gn[1], "norm_l0m")
    qkv_e = _mm_nn(h_e, w_qkv_e, 768, BF, "even_qkv")
    f_e = _mm_nn(h_e, w_f, HD, F32, "even_fgate")
    o_sb, ot_sb, tot_sb = _sb_fwd(qkv_e, "sb_fwd")
    cum_b, cum_t = _fgate_fwd(f_e, b_pad, "fgate_fwd")
    o_fox, ot_fox, lse_fox = _fox_fwd(qkv_e, cum_b, cum_t, "fox_fwd")
    o_e = jnp.concatenate([o_sb, o_fox], axis=1)
    ot_e = jnp.concatenate([ot_sb, ot_fox], axis=0)
    x2 = _mm_nn(o_e, w_out_e, 1024, F32, "even_out", res=x1, deps=forward_early(2, o_e))
    wf[0][1] = ffn_weights(gathered(2, x2))
    x3, sv_f2_0 = _swiglu_fwd(x2, gn[2], wf[0][1], "l0b")

    wf[1][0] = ffn_weights(gathered(3, x3))
    x4, sv_f1_1 = _swiglu_fwd(x3, gn[3], wf[1][0], "l1a")
    g4 = gathered(4, x4)
    w_qkv_o = g4[0]
    w_out_o = g4[1].reshape(d, d)
    h_o, ht_o = _rmsnorm_fwd(x4, gn[4], "norm_l1m")
    qkv_o = _qkv_rope(h_o, w_qkv_o, _rope_tables(s, 1.0), "odd_qkv")
    o_o, ot_o, glse = _dilated_fwd(qkv_o, "dilated_fwd")
    x5 = _mm_nn(o_o, w_out_o, 1024, F32, "odd_out", res=x4, deps=forward_early(5, o_o))
    wf[1][1] = ffn_weights(gathered(5, x5))
    x6, sv_f2_1 = _swiglu_fwd(x5, gn[5], wf[1][1], "l1b")

    def chip_sums(gs, a_s, tag):
        ps = _pair_sum(gs, a_s, slots, f"pair_sum_{tag}")
        return gs, a_s, _chip_start(ps, f"chip_start_{tag}")

    def as_slices(gs):
        return [g_ if g_.ndim == 3 else g_.reshape(NDEV, g_.shape[0] // NDEV, g_.shape[1]) for g_ in gs]

    def reduce_start(gs, tag):
        gs = as_slices(gs)
        return chip_sums(gs, _pair_exchange(gs, f"pair_exchange_{tag}"), tag)

    red, crossing = {}, {}

    def cross(gs, tag):
        crossing[tag] = _pair_start(as_slices(gs), f"pair_start_{tag}")
        return [crossing[tag]["token"]]

    def reduce_behind_dh(tag):
        return lambda gw: cross([gw["gate"], gw["up"], gw["down"]], tag)

    def reduce_after(tag, after):
        red[tag] = chip_sums(*_pair_finish(crossing[tag], after, f"pair_finish_{tag}"), tag)
        return [red[tag][2]["token"]]

    def reduce_now(tag, names):
        def hook(gw):
            red[tag] = reduce_start([gw[nm] for nm in names] if names else [gw], tag)
            return [red[tag][2]["token"]]
        return hook

    dx6, dx6b, _, d_gfin, loss_part = _loss_head(x6, gfin, target, "loss_head")

    (dx5, dx5b, dx5bt), dgn5, _ = _swiglu_bwd(sv_f2_1, gn[5], wf[1][1], dx6, dx6b, 1.0, "l1b", on_grads=reduce_behind_dh("l1b"))
    d_wout_o = _mm_nn(ot_o, dx5b, 1024, BF, "odd_out_dw")
    do_o = _mm_nt([(dx5b, w_out_o)], "odd_out_do", BF, deps=reduce_after("l1b", dx5))
    dqkv_o = jnp.concatenate(_dilated_bwd(qkv_o, o_o, glse, do_o, _rope_tables(s, -1.0), "dilated_bwd"), axis=1)
    (d_wqkv_o,) = _grad_cols(ht_o, [dqkv_o], [True], "odd_qkv_dw")
    dh_o = _dh_cols([dqkv_o], [w_qkv_o], [True], "odd_qkv_dh", cross([d_wqkv_o, d_wout_o], "l1m"))
    dx4, dx4b, _, dgn4 = _rmsnorm_bwd(x4, gn[4], dh_o, dx5, 0.5, "norm_bwd_l1m")
    (dx3, dx3b, _), dgn3, _ = _swiglu_bwd(sv_f1_1, gn[3], wf[1][0], dx4, dx4b, 0.5, "l1a", reduce_after("l1m", dx4),
                                         on_grads=reduce_behind_dh("l1a"))

    (dx2, dx2b, dx2bt), dgn2, _ = _swiglu_bwd(sv_f2_0, gn[2], wf[0][1], dx3, dx3b, 1.0, "l0b", reduce_after("l1a", dx3),
                                             on_grads=reduce_behind_dh("l0b"))
    d_wout_e = _mm_nn(ot_e, dx2b, 1024, BF, "even_out_dw")
    do_e = _mm_nt([(dx2b, w_out_e)], "even_out_do", BF, deps=reduce_after("l0b", dx2))
    dq_sb, dk_sb, dv_sb = _sb_bwd(qkv_e, do_e, tot_sb, "sb_bwd")
    dq_fx, dk_fx, dv_fx, dcq, dck = _fox_bwd(qkv_e, cum_b, cum_t, o_fox, lse_fox, do_e, "fox_bwd")
    df, db_part = _fgate_bwd(dcq, dck, f_e, b_pad, "fgate_bwd")
    dqkv_e = jnp.concatenate([dq_sb, dq_fx, dk_sb, dk_fx, dv_sb, dv_fx], axis=1)
    d_wqkv_e = _mm_nn(ht_e, dqkv_e, 768, BF, "even_qkv_dw")
    d_wf = _mm_nn(ht_e, df, HD, BF, "even_fgate_dw")
    d_win_nat = jnp.concatenate([d_wqkv_e, d_wf[:, :nfox]], axis=1)
    d_win = jnp.transpose(d_win_nat.reshape(d, NDEV, -1), (1, 0, 2))
    dh_e = _mm_nt([(dqkv_e, w_qkv_e), (df, w_f)], "even_in_dh", deps=cross([d_win, d_wout_e], "l0m"))
    dx1, dx1b, _, dgn1 = _rmsnorm_bwd(x1, gn[1], dh_e, dx2, 0.5, "norm_bwd_l0m")
    (dx0, _, _), dgn0, _ = _swiglu_bwd(sv_f1_0, gn[0], wf[0][0], dx1, dx1b, 1.0, "l0a", reduce_after("l0m", dx1),
                                      on_down=reduce_now("l0a_down", None), on_grads=reduce_now("l0a_gu", ["gate", "up"]))

    def reduce_finish(red, tag, after):
        gs, a_s, st = red
        return list(zip(gs, a_s, _chip_finish(st, after, f"chip_finish_{tag}")))

    f_l1b, f_l1m, f_l1a = (reduce_finish(red[t], t, dx0) for t in ("l1b", "l1m", "l1a"))
    f_l0b, f_l0m = (reduce_finish(red[t], t, dx0) for t in ("l0b", "l0m"))

    def update(w_, m_, v_, parts, nm):
        if w_.shape[2] % 128 == 0:
            return _adamw_sharded(w_, m_, v_, parts, slots, f"adamw_{nm}")
        outs = _adamw_sharded(jnp.swapaxes(w_, 1, 2), jnp.swapaxes(m_, 1, 2), jnp.swapaxes(v_, 1, 2), parts, slots,
                              f"adamw_{nm}", transposed=True)
        return [jnp.swapaxes(o, 1, 2) for o in outs]

    res = {}
    res["even_w_in"] = update(even_w_in, m_even_w_in, v_even_w_in, [f_l0m[0]], "even_w_in")
    res["even_w_out"] = _adamw_sharded(even_w_out, m_even_w_out, v_even_w_out, [f_l0m[1]], slots, "adamw_even_w_out")
    res["odd_w_qkv"] = _adamw_sharded(odd_w_qkv, m_odd_w_qkv, v_odd_w_qkv, [f_l1m[0]], slots, "adamw_odd_w_qkv")
    res["odd_w_out"] = _adamw_sharded(odd_w_out, m_odd_w_out, v_odd_w_out, [f_l1m[1]], slots, "adamw_odd_w_out")
    names = ["ffn2_w_gate", "ffn2_w_up", "ffn2_w_down", "ffn1_w_gate", "ffn1_w_up", "ffn1_w_down"]
    ws = [ffn2_w_gate, ffn2_w_up, ffn2_w_down, ffn1_w_gate, ffn1_w_up, ffn1_w_down]
    ms = [m_ffn2_w_gate, m_ffn2_w_up, m_ffn2_w_down, m_ffn1_w_gate, m_ffn1_w_up, m_ffn1_w_down]
    vs = [v_ffn2_w_gate, v_ffn2_w_up, v_ffn2_w_down, v_ffn1_w_gate, v_ffn1_w_up, v_ffn1_w_down]
    for k in range(3):
        res[names[k]] = update(ws[k], ms[k], vs[k], [f_l0b[k], f_l1b[k]], names[k])
    f_l0a = (reduce_finish(red["l0a_gu"], "l0a_gu", res["ffn2_w_down"][1])
             + reduce_finish(red["l0a_down"], "l0a_down", res["ffn2_w_down"][1]))
    for k in range(3, 6):
        res[names[k]] = update(ws[k], ms[k], vs[k], [f_l0a[k - 3], f_l1a[k - 3]], names[k])

    dnorm = jnp.concatenate([dgn0, dgn1, dgn2, dgn3, dgn4, dgn5], axis=0)
    nsm = d // NDEV
    small_rows = (6 * d + d + 2 * HD) // HD
    pad_rows = -small_rows % 8
    part = jnp.concatenate([dnorm.reshape(-1), d_gfin.reshape(-1), db_part.reshape(-1), loss_part.reshape(-1),
                            jnp.zeros((pad_rows * HD,), F32)]).reshape(small_rows + pad_rows, HD)
    (gathered,) = _all_gather([part], "gather_small")

    def pack(ng, bfg, fg):
        full = lax.dynamic_update_slice(jnp.zeros((6, d), F32), ng.reshape(6, nsm), (0, me * nsm))
        return jnp.concatenate([full.reshape(-1), fg.reshape(-1), jnp.pad(bfg.reshape(-1), (0, HD - nfox)),
                                jnp.zeros((HD + pad_rows * HD,), F32)]).reshape(small_rows + pad_rows, HD)

    sm = _adamw_small(pack(norm_g, even_b_forget, final_norm_g), pack(m_norm_g, m_even_b_forget, m_final_norm_g),
                      pack(v_norm_g, v_even_b_forget, v_final_norm_g), gathered, "adamw_small")

    def unpack(t):
        flat = t.reshape(-1)
        ng = lax.dynamic_slice(flat[:6 * d].reshape(6, d), (0, me * nsm), (6, nsm)).reshape(norm_g.shape)
        fg = flat[6 * d:7 * d].reshape(final_norm_g.shape)
        bfg = flat[7 * d:7 * d + nfox].reshape(even_b_forget.shape)
        return ng, bfg, fg

    sm_g, sm_d, sm_m, sm_v = [unpack(t) for t in sm]
    loss = sm[0].reshape(-1)[7 * d + HD]

    order = ["norm_g", "ffn1_w_gate", "ffn1_w_up", "ffn1_w_down", "ffn2_w_gate", "ffn2_w_up", "ffn2_w_down", "even_w_in",
             "even_b_forget", "even_w_out", "odd_w_qkv", "odd_w_out", "final_norm_g"]
    outs = [loss, dx0.reshape(x.shape)]
    for k in range(4):
        smk = [sm_g, sm_d, sm_m, sm_v][k]
        for nm in order:
            if nm == "norm_g":
                outs.append(smk[0])
            elif nm == "even_b_forget":
                outs.append(smk[1])
            elif nm == "final_norm_g":
                outs.append(smk[2])
            else:
                outs.append(res[nm][k])
    return tuple(outs)
```

```python
import functools

import jax
import jax.numpy as jnp
import numpy as np
from jax import lax
from jax.experimental import pallas as pl
from jax.experimental.pallas import tpu as pltpu

F32 = jnp.float32
BF = jnp.bfloat16
NDEV = 8
HD = 128
QB = 128
N_HEADS = 16
N_SB = 8
SCALE = HD ** -0.5
ROPE_THETA = 500000.0
ROPE_DIMS = HD // 4
DILATED_PATTERNS = ((128, 1), (512, 4), (2048, 16))
RMS_EPS = 1e-6
NEG_INF = -1e30
ADAM_LR = 0.001
ADAM_B1 = 0.9
ADAM_B2 = 0.999
ADAM_EPS = 1e-08
ADAM_WD = 0.01
ADAM_STEP = 10
VMEM_LIMIT_V7X = 56 * 1024 * 1024
MESH = pl.DeviceIdType.MESH
ANY = pl.BlockSpec(memory_space=pl.ANY)

NT_DIMS = (((1,), (1,)), ((), ()))


def _cp(*dims):
    return pltpu.CompilerParams(dimension_semantics=dims if dims else None, vmem_limit_bytes=VMEM_LIMIT_V7X)


def _dot(a, b):
    return jnp.dot(a, b, preferred_element_type=F32)


def _dot_nt(a, b):
    return lax.dot_general(a, b, NT_DIMS, preferred_element_type=F32)


def _sds(shape, dtype):
    return jax.ShapeDtypeStruct(shape, dtype)


def _place():
    x, y, c = lax.axis_index("x"), lax.axis_index("y"), lax.axis_index("c")
    chips = [(x, y), (1 - x, y), (x, 1 - y), (1 - x, 1 - y)]
    return x, y, c, chips


def _all_gather(xs, name):
    n = len(xs)

    def body(*refs):
        x_refs, out_refs = refs[:n], refs[n:2 * n]
        send_sems, recv_sems, local_sems = refs[2 * n:]
        x, y, c, chips = _place()
        me, sibling = (x, y, c), (x, y, 1 - c)
        others = chips[1:]

        def slot(a, px, py, pc):
            return out_refs[a].at[4 * px + 2 * py + pc]

        def copy(a, k, block, to, src=None):
            return pltpu.make_async_remote_copy(
                src_ref=slot(a, *block) if src is None else src, dst_ref=slot(a, *block),
                send_sem=send_sems.at[a, k], recv_sem=recv_sems.at[a, k], device_id=to, device_id_type=MESH)

        started = []
        for a in range(n):
            mine = pltpu.make_async_copy(x_refs[a], slot(a, *me), local_sems.at[a])
            mine.start()
            first = [copy(a, 0, me, sibling, src=x_refs[a])]
            first += [copy(a, 1 + j, me, (*chip, c), src=x_refs[a]) for j, chip in enumerate(others)]
            for cp in first:
                cp.start()
            started += [mine.wait] + [cp.wait_send for cp in first]
        for a in range(n):
            for j, chip in enumerate(others):
                copy(a, 1 + j, (*chip, c), me).wait_recv()
                passed = copy(a, 4 + j, (*chip, c), sibling)
                passed.start()
                started.append(passed.wait_send)
        for a in range(n):
            copy(a, 0, sibling, me).wait_recv()
            for j, chip in enumerate(others):
                copy(a, 4 + j, (*chip, 1 - c), me).wait_recv()
        for w in started:
            w()

    return pl.pallas_call(
        body, name=name,
        out_shape=[_sds((NDEV,) + x.shape, x.dtype) for x in xs],
        in_specs=[ANY] * n, out_specs=[ANY] * n,
        scratch_shapes=[pltpu.SemaphoreType.DMA((n, 7)), pltpu.SemaphoreType.DMA((n, 7)), pltpu.SemaphoreType.DMA((n,))],
    )(*xs)


def _pair_exchange(gs, name):
    n = len(gs)

    def body(*refs):
        g_refs, a_refs = refs[:n], refs[n:2 * n]
        send_sems, recv_sems = refs[2 * n:]
        x, y, c, chips = _place()
        copies = []
        for a in range(n):
            for j, (px, py) in enumerate(chips):
                copies.append(pltpu.make_async_remote_copy(
                    src_ref=g_refs[a].at[4 * px + 2 * py + (1 - c)], dst_ref=a_refs[a].at[j],
                    send_sem=send_sems.at[a, j], recv_sem=recv_sems.at[a, j],
                    device_id=(x, y, 1 - c), device_id_type=MESH))
        for cp in copies:
            cp.start()
        for cp in copies:
            cp.wait()

    return pl.pallas_call(
        body, name=name,
        out_shape=[_sds((4,) + g.shape[1:], g.dtype) for g in gs],
        in_specs=[ANY] * n, out_specs=[ANY] * n,
        scratch_shapes=[pltpu.SemaphoreType.DMA((n, 4)), pltpu.SemaphoreType.DMA((n, 4))],
    )(*gs)


HBM = pl.BlockSpec(memory_space=pltpu.HBM)
SEM = pl.BlockSpec(memory_space=pltpu.SEMAPHORE)
EFFECT = pltpu.SideEffectType.DATAFLOW_SIDE_EFFECTING
TOKEN = _sds((8, 128), F32)
TOKEN_SPEC = pl.BlockSpec((8, 128), lambda *_: (0, 0))


def _in_hbm(x):
    return pltpu.with_memory_space_constraint(x, pltpu.HBM)


def _ignore_deps(body, n_in, n_deps):
    if not n_deps:
        return body
    return lambda *refs: body(*refs[:n_in], *refs[n_in + n_deps:])


def _slot_of(px, py, pc):
    return 4 * px + 2 * py + pc


def _gather_start(xs, me, deps, name):
    n = len(xs)
    lands = [lax.dynamic_update_slice(lax.empty((NDEV,) + x.shape, x.dtype), x[None], (me,) + (0,) * x.ndim) for x in xs]

    def body(*refs):
        x_refs, land_refs = refs[:n], refs[n:2 * n]
        send, recv_ici, recv_sib = refs[2 * n:2 * n + 3]
        token = refs[4 * n + 3]
        x, y, c, chips = _place()
        for a in range(n):
            dst = land_refs[a].at[_slot_of(x, y, c)]
            pltpu.make_async_remote_copy(src_ref=x_refs[a], dst_ref=dst, send_sem=send.at[4 * a], recv_sem=recv_sib.at[a],
                                         device_id=(x, y, 1 - c), device_id_type=MESH).start()
            for j, chip in enumerate(chips[1:]):
                pltpu.make_async_remote_copy(src_ref=x_refs[a], dst_ref=dst, send_sem=send.at[4 * a + 1 + j], recv_sem=recv_ici.at[3 * a + j],
                                             device_id=(*chip, c), device_id_type=MESH).start()
        token[...] = jnp.zeros_like(token)

    outs = pl.pallas_call(
        _ignore_deps(body, 2 * n, len(deps)), name=name,
        out_shape=(pltpu.SemaphoreType.DMA((4 * n,)), pltpu.SemaphoreType.DMA((3 * n,)), pltpu.SemaphoreType.DMA((n,)),
                   *[pltpu.HBM(x.shape, x.dtype) for x in xs], *[pltpu.HBM(l.shape, l.dtype) for l in lands], TOKEN),
        in_specs=[HBM] * (2 * n) + [TOKEN_SPEC] * len(deps),
        out_specs=(SEM, SEM, SEM, *[HBM] * (2 * n), pl.BlockSpec(memory_space=pltpu.VMEM)),
        input_output_aliases={a: 3 + a for a in range(2 * n)},
        compiler_params=pltpu.CompilerParams(has_side_effects=EFFECT),
    )(*[_in_hbm(x) for x in xs], *[_in_hbm(l) for l in lands], *deps)
    send, recv_ici, recv_sib = outs[:3]
    return dict(send=send, recv_ici=recv_ici, recv_sib=recv_sib, xs=list(outs[3:3 + n]), lands=list(outs[3 + n:3 + 2 * n]), token=outs[-1])


def _gather_forward(st, after, name):
    n = len(st["lands"])

    def body(*refs):
        land_refs, recv_ici = refs[:n], refs[n]
        send2, recv2, token = refs[n + 2], refs[n + 3], refs[2 * n + 4]
        x, y, c, chips = _place()
        for a in range(n):
            for j, chip in enumerate(chips[1:]):
                blk = land_refs[a].at[_slot_of(*chip, c)]
                pltpu.make_async_remote_copy(src_ref=blk, dst_ref=blk, send_sem=send2.at[3 * a + j], recv_sem=recv_ici.at[3 * a + j],
                                             device_id=(*chip, c), device_id_type=MESH).wait_recv()
                pltpu.make_async_remote_copy(src_ref=blk, dst_ref=blk, send_sem=send2.at[3 * a + j], recv_sem=recv2.at[3 * a + j],
                                             device_id=(x, y, 1 - c), device_id_type=MESH).start()
        token[...] = jnp.zeros_like(token)

    outs = pl.pallas_call(
        body, name=name,
        out_shape=(pltpu.SemaphoreType.DMA((3 * n,)), pltpu.SemaphoreType.DMA((3 * n,)), *[pltpu.HBM(l.shape, l.dtype) for l in st["lands"]], TOKEN),
        in_specs=[HBM] * n + [SEM, pl.BlockSpec(memory_space=pl.ANY)],
        out_specs=(SEM, SEM, *[HBM] * n, pl.BlockSpec(memory_space=pltpu.VMEM)),
        input_output_aliases={a: 2 + a for a in range(n)},
        compiler_params=pltpu.CompilerParams(has_side_effects=EFFECT),
    )(*st["lands"], st["recv_ici"], after)
    return dict(st, send2=outs[0], recv2=outs[1], lands=list(outs[2:2 + n]), token=outs[-1])


def _gather_finish(st, after, name):
    n = len(st["lands"])

    def body(*refs):
        x_refs, land_refs = refs[:n], refs[n:2 * n]
        send, recv_sib, send2, recv2 = refs[2 * n:2 * n + 4]
        x, y, c, chips = _place()
        for a in range(n):
            mine = land_refs[a].at[_slot_of(x, y, c)]
            theirs = land_refs[a].at[_slot_of(x, y, 1 - c)]
            for k in range(4):
                pltpu.make_async_remote_copy(src_ref=x_refs[a], dst_ref=mine, send_sem=send.at[4 * a + k], recv_sem=recv_sib.at[a],
                                             device_id=(x, y, 1 - c), device_id_type=MESH).wait_send()
            pltpu.make_async_remote_copy(src_ref=x_refs[a], dst_ref=theirs, send_sem=send.at[4 * a], recv_sem=recv_sib.at[a],
                                         device_id=(x, y, 1 - c), device_id_type=MESH).wait_recv()
            for j, chip in enumerate(chips[1:]):
                sent = land_refs[a].at[_slot_of(*chip, c)]
                got = land_refs[a].at[_slot_of(*chip, 1 - c)]
                pltpu.make_async_remote_copy(src_ref=sent, dst_ref=sent, send_sem=send2.at[3 * a + j], recv_sem=recv2.at[3 * a + j],
                                             device_id=(x, y, 1 - c), device_id_type=MESH).wait_send()
                pltpu.make_async_remote_copy(src_ref=got, dst_ref=got, send_sem=send2.at[3 * a + j], recv_sem=recv2.at[3 * a + j],
                                             device_id=(x, y, 1 - c), device_id_type=MESH).wait_recv()

    outs = pl.pallas_call(
        body, name=name,
        out_shape=tuple(pltpu.HBM(v.shape, v.dtype) for v in st["xs"] + st["lands"]),
        in_specs=[HBM] * (2 * n) + [SEM] * 4 + [pl.BlockSpec(memory_space=pl.ANY)], out_specs=tuple([HBM] * (2 * n)),
        input_output_aliases={a: a for a in range(2 * n)},
        compiler_params=pltpu.CompilerParams(has_side_effects=EFFECT),
    )(*st["xs"], *st["lands"], st["send"], st["recv_sib"], st["send2"], st["recv2"], after)
    return list(outs[n:])


def _pair_start(gs, name):
    n = len(gs)
    lands = [lax.empty((4,) + g.shape[1:], g.dtype) for g in gs]

    def body(*refs):
        g_refs, a_refs = refs[:n], refs[n:2 * n]
        send, recv = refs[2 * n], refs[2 * n + 1]
        token = refs[4 * n + 2]
        x, y, c, chips = _place()
        for a in range(n):
            for j, (px, py) in enumerate(chips):
                pltpu.make_async_remote_copy(src_ref=g_refs[a].at[_slot_of(px, py, 1 - c)], dst_ref=a_refs[a].at[j],
                                             send_sem=send.at[4 * a + j], recv_sem=recv.at[4 * a + j],
                                             device_id=(x, y, 1 - c), device_id_type=MESH).start()
        token[...] = jnp.zeros_like(token)

    outs = pl.pallas_call(
        body, name=name,
        out_shape=(pltpu.SemaphoreType.DMA((4 * n,)), pltpu.SemaphoreType.DMA((4 * n,)),
                   *[pltpu.HBM(g.shape, g.dtype) for g in gs], *[pltpu.HBM(l.shape, l.dtype) for l in lands], TOKEN),
        in_specs=[HBM] * (2 * n), out_specs=(SEM, SEM, *[HBM] * (2 * n), pl.BlockSpec(memory_space=pltpu.VMEM)),
        input_output_aliases={a: 2 + a for a in range(2 * n)},
        compiler_params=pltpu.CompilerParams(has_side_effects=EFFECT),
    )(*[_in_hbm(g) for g in gs], *[_in_hbm(l) for l in lands])
    return dict(send=outs[0], recv=outs[1], gs=list(outs[2:2 + n]), lands=list(outs[2 + n:2 + 2 * n]), token=outs[-1])


def _pair_finish(st, after, name):
    n = len(st["gs"])

    def body(*refs):
        g_refs, a_refs = refs[:n], refs[n:2 * n]
        send, recv = refs[2 * n], refs[2 * n + 1]
        x, y, c, chips = _place()
        for a in range(n):
            for j, (px, py) in enumerate(chips):
                cp = pltpu.make_async_remote_copy(src_ref=g_refs[a].at[_slot_of(px, py, 1 - c)], dst_ref=a_refs[a].at[j],
                                                  send_sem=send.at[4 * a + j], recv_sem=recv.at[4 * a + j],
                                                  device_id=(x, y, 1 - c), device_id_type=MESH)
                cp.wait_send()
                cp.wait_recv()

    outs = pl.pallas_call(
        body, name=name,
        out_shape=tuple(pltpu.HBM(v.shape, v.dtype) for v in st["gs"] + st["lands"]),
        in_specs=[HBM] * (2 * n) + [SEM, SEM, pl.BlockSpec(memory_space=pl.ANY)], out_specs=tuple([HBM] * (2 * n)),
        input_output_aliases={a: a for a in range(2 * n)},
        compiler_params=pltpu.CompilerParams(has_side_effects=EFFECT),
    )(*st["gs"], *st["lands"], st["send"], st["recv"], after)
    return list(outs[:n]), list(outs[n:])


def _chip_start(ps, name):
    n = len(ps)
    lands = [lax.empty(p.shape, p.dtype) for p in ps]

    def body(*refs):
        p_refs, b_refs = refs[:n], refs[n:2 * n]
        send, recv = refs[2 * n], refs[2 * n + 1]
        token = refs[4 * n + 2]
        x, y, c, chips = _place()
        for a in range(n):
            for j, chip in enumerate(chips[1:]):
                pltpu.make_async_remote_copy(src_ref=p_refs[a].at[j], dst_ref=b_refs[a].at[j], send_sem=send.at[3 * a + j], recv_sem=recv.at[3 * a + j],
                                             device_id=(*chip, c), device_id_type=MESH).start()
        token[...] = jnp.zeros_like(token)

    outs = pl.pallas_call(
        body, name=name,
        out_shape=(pltpu.SemaphoreType.DMA((3 * n,)), pltpu.SemaphoreType.DMA((3 * n,)),
                   *[pltpu.HBM(p.shape, p.dtype) for p in ps], *[pltpu.HBM(p.shape, p.dtype) for p in ps], TOKEN),
        in_specs=[HBM] * (2 * n), out_specs=(SEM, SEM, *[HBM] * (2 * n), pl.BlockSpec(memory_space=pltpu.VMEM)),
        input_output_aliases={a: 2 + a for a in range(2 * n)},
        compiler_params=pltpu.CompilerParams(has_side_effects=EFFECT),
    )(*[_in_hbm(p) for p in ps], *[_in_hbm(l) for l in lands])
    return dict(send=outs[0], recv=outs[1], ps=list(outs[2:2 + n]), lands=list(outs[2 + n:2 + 2 * n]), token=outs[-1])


def _chip_finish(st, after, name):
    n = len(st["ps"])

    def body(*refs):
        p_refs, b_refs = refs[:n], refs[n:2 * n]
        send, recv = refs[2 * n], refs[2 * n + 1]
        x, y, c, chips = _place()
        for a in range(n):
            for j, chip in enumerate(chips[1:]):
                cp = pltpu.make_async_remote_copy(src_ref=p_refs[a].at[j], dst_ref=b_refs[a].at[j], send_sem=send.at[3 * a + j], recv_sem=recv.at[3 * a + j],
                                                  device_id=(*chip, c), device_id_type=MESH)
                cp.wait_send()
                cp.wait_recv()

    outs = pl.pallas_call(
        body, name=name,
        out_shape=tuple(pltpu.HBM(v.shape, v.dtype) for v in st["ps"] + st["lands"]),
        in_specs=[HBM] * (2 * n) + [SEM, SEM, pl.BlockSpec(memory_space=pl.ANY)], out_specs=tuple([HBM] * (2 * n)),
        input_output_aliases={a: a for a in range(2 * n)},
        compiler_params=pltpu.CompilerParams(has_side_effects=EFFECT),
    )(*st["ps"], *st["lands"], st["send"], st["recv"], after)
    return list(outs[n:])


def _rows_tile(r):
    for t in (512, 256, 128, 64, 32, 16):
        if r % t == 0:
            return t
    return r


PAIR_SUM_STEPS = 4


def _pair_sum(gs, a_s, slots, name):
    n = len(gs)
    trs = [g.shape[1] // PAIR_SUM_STEPS for g in gs]

    def body(slots_ref, *refs):
        for g_ref, a_ref, p_ref in zip(refs[:n], refs[n:2 * n], refs[2 * n:]):
            p_ref[...] = (g_ref[...].astype(F32) + a_ref[...].astype(F32)).astype(BF)

    def spec(g, tr, index):
        return pl.BlockSpec((None, tr, g.shape[2]), index)

    return pl.pallas_call(
        body, name=name,
        grid_spec=pltpu.PrefetchScalarGridSpec(
            num_scalar_prefetch=1, grid=(3, PAIR_SUM_STEPS),
            in_specs=[spec(g, tr, lambda j, i, s: (s[j + 1], i, 0)) for g, tr in zip(gs, trs)]
            + [spec(g, tr, lambda j, i, s: (j + 1, i, 0)) for g, tr in zip(gs, trs)],
            out_specs=[spec(g, tr, lambda j, i, s: (j, i, 0)) for g, tr in zip(gs, trs)]),
        out_shape=[_sds((3,) + g.shape[1:], BF) for g in gs], compiler_params=_cp("parallel", "parallel"),
    )(slots, *gs, *a_s)


def _adamw_math(w, g, m, v):
    m = ADAM_B1 * m + (1.0 - ADAM_B1) * g
    v = ADAM_B2 * v + (1.0 - ADAM_B2) * (g * g)
    m_hat = m / (1.0 - ADAM_B1 ** ADAM_STEP)
    v_hat = v / (1.0 - ADAM_B2 ** ADAM_STEP)
    delta = -ADAM_LR * (m_hat / (jnp.sqrt(v_hat) + ADAM_EPS) + ADAM_WD * w)
    return delta, m, v


def _adamw_sharded(w, m, v, parts, slots, name, transposed=False):
    nl = w.shape[0]
    r, c = parts[0][0].shape[1:]
    tr = _rows_tile(r)
    if c * tr * 4 > (1 << 21) and not transposed:
        tr = max(8, tr // 2)

    def body(slots_ref, w_ref, m_ref, v_ref, *rest):
        part_refs, (g_out, d_out, m_out, v_out) = rest[:5 * nl], rest[5 * nl:]
        layer = pl.program_id(0)
        g = None
        for l in range(nl):
            s = part_refs[5 * l][...].astype(F32)
            for ref in part_refs[5 * l + 1:5 * l + 5]:
                s = s + ref[...].astype(F32)
            g = s if g is None else jnp.where(layer == l, s, g)
        if transposed:
            g = g.T
        delta, mn, vn = _adamw_math(w_ref[...], g, m_ref[...], v_ref[...])
        g_out[...] = g
        d_out[...] = delta
        m_out[...] = mn
        v_out[...] = vn

    def own(l):
        return lambda L, i, s: (s[0], jnp.where(L == l, i, 0), 0)

    def fixed(l, k):
        return lambda L, i, s: (k, jnp.where(L == l, i, 0), 0)

    if transposed:
        wspec = pl.BlockSpec((None, c, tr), lambda L, i, s: (L, 0, i))
    else:
        wspec = pl.BlockSpec((None, tr, c), lambda L, i, s: (L, i, 0))
    in_specs = [wspec, wspec, wspec]
    args = [w, m, v]
    for l, (g, a, b) in enumerate(parts):
        in_specs += [pl.BlockSpec((None, tr, c), own(l)), pl.BlockSpec((None, tr, c), fixed(l, 0)),
                     pl.BlockSpec((None, tr, c), fixed(l, 0)), pl.BlockSpec((None, tr, c), fixed(l, 1)),
                     pl.BlockSpec((None, tr, c), fixed(l, 2))]
        args += [g, a, b, b, b]
    return pl.pallas_call(
        body, name=name,
        grid_spec=pltpu.PrefetchScalarGridSpec(
            num_scalar_prefetch=1, grid=(nl, r // tr), in_specs=in_specs, out_specs=[wspec] * 4),
        out_shape=[_sds(w.shape, F32)] * 4, compiler_params=_cp("arbitrary", "arbitrary"),
    )(slots, *args)


def _adamw_small(w, m, v, gathered, name):
    def body(w_ref, m_ref, v_ref, gg_ref, g_out, d_out, m_out, v_out):
        g = gg_ref[0]
        for k in range(1, NDEV):
            g = g + gg_ref[k]
        delta, mn, vn = _adamw_math(w_ref[...], g, m_ref[...], v_ref[...])
        g_out[...] = g
        d_out[...] = delta
        m_out[...] = mn
        v_out[...] = vn

    return pl.pallas_call(body, name=name, out_shape=[_sds(w.shape, F32)] * 4)(w, m, v, gathered)


def _rmsnorm_fwd(x, g, name, deps=()):
    s, d = x.shape
    tm = 256

    def body(x_ref, g_ref, h_ref, ht_ref):
        xf = x_ref[...]
        y = xf * lax.rsqrt(jnp.mean(xf * xf, axis=-1, keepdims=True) + RMS_EPS)
        h = y * g_ref[...]
        h_ref[...] = h.astype(BF)
        ht_ref[...] = h.T.astype(BF)

    return pl.pallas_call(
        _ignore_deps(body, 2, len(deps)), name=name, grid=(s // tm,),
        in_specs=[pl.BlockSpec((tm, d), lambda i: (i, 0)), pl.BlockSpec((1, d), lambda i: (0, 0))] + [TOKEN_SPEC] * len(deps),
        out_specs=[pl.BlockSpec((tm, d), lambda i: (i, 0)), pl.BlockSpec((d, tm), lambda i: (0, i))],
        out_shape=[_sds((s, d), BF), _sds((d, s), BF)], compiler_params=_cp("parallel"),
    )(x, g, *deps)


def _rmsnorm_bwd(x, g, dh, dres, out_scale, name):
    s, d = x.shape
    tm = 256

    def body(x_ref, g_ref, dh_ref, dres_ref, dx_ref, dxb_ref, dxbt_ref, dg_ref):
        xf = x_ref[...]
        r = lax.rsqrt(jnp.mean(xf * xf, axis=-1, keepdims=True) + RMS_EPS)
        xhat = xf * r
        dhv = dh_ref[...]
        dxhat = dhv * g_ref[...]
        dx = dres_ref[...] + r * (dxhat - xhat * jnp.mean(dxhat * xhat, axis=-1, keepdims=True))
        dx_ref[...] = dx
        scaled = dx * out_scale
        dxb_ref[...] = scaled.astype(BF)
        dxbt_ref[...] = scaled.T.astype(BF)

        @pl.when(pl.program_id(0) == 0)
        def _():
            dg_ref[...] = jnp.zeros_like(dg_ref)

        dg_ref[...] += jnp.sum(dhv * xhat, axis=0, keepdims=True)

    row = pl.BlockSpec((tm, d), lambda i: (i, 0))
    vec = pl.BlockSpec((1, d), lambda i: (0, 0))
    return pl.pallas_call(
        body, name=name, grid=(s // tm,),
        in_specs=[row, vec, row, row],
        out_specs=[row, row, pl.BlockSpec((d, tm), lambda i: (0, i)), vec],
        out_shape=[_sds((s, d), F32), _sds((s, d), BF), _sds((d, s), BF), _sds((1, d), F32)],
        compiler_params=_cp("arbitrary"),
    )(x, g, dh, dres)


def _loss_head(x, g, target, name):
    s, d = x.shape
    tm = 256

    def body(x_ref, g_ref, t_ref, dx_ref, dxb_ref, dxbt_ref, dg_ref, loss_ref):
        xf = x_ref[...]
        r = lax.rsqrt(jnp.mean(xf * xf, axis=-1, keepdims=True) + RMS_EPS)
        xhat = xf * r
        err = xhat * g_ref[...] - t_ref[...]
        dy = err * (1.0 / d)
        dxhat = dy * g_ref[...]
        dx = r * (dxhat - xhat * jnp.mean(dxhat * xhat, axis=-1, keepdims=True))
        dx_ref[...] = dx
        half = dx * 0.5
        dxb_ref[...] = half.astype(BF)
        dxbt_ref[...] = half.T.astype(BF)

        @pl.when(pl.program_id(0) == 0)
        def _():
            dg_ref[...] = jnp.zeros_like(dg_ref)
            loss_ref[...] = jnp.zeros_like(loss_ref)

        dg_ref[...] += jnp.sum(dy * xhat, axis=0, keepdims=True)
        part = 0.5 * jnp.sum(jnp.mean(err * err, axis=-1, keepdims=True), axis=0, keepdims=True)
        lane = lax.broadcasted_iota(jnp.int32, (1, 128), 1)
        loss_ref[...] += jnp.where(lane == 0, part, 0.0)

    row = pl.BlockSpec((tm, d), lambda i: (i, 0))
    vec = pl.BlockSpec((1, d), lambda i: (0, 0))
    return pl.pallas_call(
        body, name=name, grid=(s // tm,),
        in_specs=[row, vec, row],
        out_specs=[row, row, pl.BlockSpec((d, tm), lambda i: (0, i)), vec, pl.BlockSpec((1, 128), lambda i: (0, 0))],
        out_shape=[_sds((s, d), F32), _sds((s, d), BF), _sds((d, s), BF), _sds((1, d), F32), _sds((1, 128), F32)],
        compiler_params=_cp("arbitrary"),
    )(x, g, target)


def _act_spec(tm, n, natural, order):
    if natural:
        return pl.BlockSpec((tm, n), (lambda s, i: (i, s)) if order == "si" else (lambda i, s: (i, s)))
    return pl.BlockSpec((None, tm, n), (lambda s, i: (s, i, 0)) if order == "si" else (lambda i, s: (s, i, 0)))


def _act_shape(s, n, natural, dtype):
    return _sds((s, NDEV * n), dtype) if natural else _sds((NDEV, s, n), dtype)


def _ffn_up(h, wg, wu, name):
    s, d = h.shape
    n = wg.shape[2]
    tm = 1024

    def body(h_ref, wg_ref, wu_ref, g_ref, u_ref, a_ref, at_ref):
        hb = h_ref[...]
        g = _dot(hb, wg_ref[...])
        u = _dot(hb, wu_ref[...])
        g_ref[...] = g.astype(BF)
        u_ref[...] = u.astype(BF)
        act = g * jax.nn.sigmoid(g) * u
        a_ref[...] = act.astype(BF)
        at_ref[...] = act.T.astype(BF)

    wsp = pl.BlockSpec((None, d, n), lambda s_, i: (s_, 0, 0))
    blk = _act_spec(tm, n, False, "si")
    return pl.pallas_call(
        body, name=name, grid=(NDEV, s // tm),
        in_specs=[pl.BlockSpec((tm, d), lambda s_, i: (i, 0)), wsp, wsp],
        out_specs=[blk] * 3 + [pl.BlockSpec((None, n, tm), lambda s_, i: (s_, 0, i))],
        out_shape=[_act_shape(s, n, False, BF)] * 3 + [_sds((NDEV, n, s), BF)],
        compiler_params=_cp("parallel", "parallel"),
    )(h, wg, wu)


def _ffn_down(act, wd, x, name):
    _, s, n = act.shape
    d = wd.shape[2]
    tm = 512

    def body(a_ref, w_ref, x_ref, o_ref, acc):
        k = pl.program_id(1)

        @pl.when(k == 0)
        def _():
            acc[...] = jnp.zeros_like(acc)

        acc[...] += _dot(a_ref[...], w_ref[...])

        @pl.when(k == NDEV - 1)
        def _():
            o_ref[...] = x_ref[...] + 0.5 * acc[...]

    row = pl.BlockSpec((tm, d), lambda i, k: (i, 0))
    return pl.pallas_call(
        body, name=name, grid=(s // tm, NDEV),
        in_specs=[_act_spec(tm, n, False, "is"), pl.BlockSpec((None, n, d), lambda i, k: (k, 0, 0)), row],
        out_specs=row, out_shape=_sds((s, d), F32),
        scratch_shapes=[pltpu.VMEM((tm, d), F32)], compiler_params=_cp("parallel", "arbitrary"),
    )(act, wd, x)


def _ffn_bwd_act(dyb, wd, g, u, name, deps=()):
    s, d = dyb.shape
    n = wd.shape[1]
    tm = 1024

    def body(dy_ref, w_ref, g_ref, u_ref, dg_ref, du_ref):
        dact = _dot_nt(dy_ref[...], w_ref[...])
        gv = g_ref[...].astype(F32)
        uv = u_ref[...].astype(F32)
        sig = jax.nn.sigmoid(gv)
        dg_ref[...] = (dact * uv * (sig * (1.0 + gv * (1.0 - sig)))).astype(BF)
        du_ref[...] = (dact * (gv * sig)).astype(BF)

    blk = _act_spec(tm, n, False, "si")
    return pl.pallas_call(
        _ignore_deps(body, 4, len(deps)), name=name, grid=(NDEV, s // tm),
        in_specs=[pl.BlockSpec((tm, d), lambda s_, i: (i, 0)), pl.BlockSpec((None, n, d), lambda s_, i: (s_, 0, 0)), blk, blk]
        + [TOKEN_SPEC] * len(deps),
        out_specs=[blk, blk], out_shape=[_act_shape(s, n, False, BF)] * 2,
        compiler_params=_cp("parallel", "parallel"),
    )(dyb, wd, g, u, *deps)


def _grad_rows(act_t, dyb, name, deps=()):
    _, n, s = act_t.shape
    d = dyb.shape[1]
    tn = 2048

    def body(a_ref, dy_ref, o_ref):
        o_ref[...] = _dot(a_ref[...], dy_ref[...]).astype(BF)

    return pl.pallas_call(
        _ignore_deps(body, 2, len(deps)), name=name, grid=(NDEV, d // tn),
        in_specs=[pl.BlockSpec((None, n, s), lambda k, j: (k, 0, 0)), pl.BlockSpec((s, tn), lambda k, j: (0, j))]
        + [TOKEN_SPEC] * len(deps),
        out_specs=pl.BlockSpec((None, n, tn), lambda k, j: (k, 0, j)), out_shape=_sds((NDEV, n, d), BF),
        compiler_params=_cp("parallel", "parallel"),
    )(act_t, dyb, *deps)


def _grad_cols(ht, dxs, naturals, name, deps=()):
    d, s = ht.shape
    k = len(dxs)
    ns = [dx.shape[1] // NDEV if nat else dx.shape[2] for dx, nat in zip(dxs, naturals)]
    td = 1024

    def body(*refs):
        ht_ref, dx_refs, o_refs = refs[0], refs[1:1 + k], refs[1 + k:]
        hv = ht_ref[...]
        for dx_ref, o_ref in zip(dx_refs, o_refs):
            o_ref[...] = _dot(hv, dx_ref[...]).astype(BF)

    def dx_spec(n, nat):
        if nat:
            return pl.BlockSpec((s, n), lambda s_, j: (0, s_))
        return pl.BlockSpec((None, s, n), lambda s_, j: (s_, 0, 0))

    return pl.pallas_call(
        _ignore_deps(body, 1 + k, len(deps)), name=name, grid=(NDEV, d // td),
        in_specs=[pl.BlockSpec((td, s), lambda s_, j: (j, 0))] + [dx_spec(n, nat) for n, nat in zip(ns, naturals)]
        + [TOKEN_SPEC] * len(deps),
        out_specs=[pl.BlockSpec((None, td, n), lambda s_, j: (s_, j, 0)) for n in ns],
        out_shape=[_sds((NDEV, d, n), BF) for n in ns], compiler_params=_cp("parallel", "parallel"),
    )(ht, *dxs, *deps)


def _dh_cols(dxs, ws, naturals, name, deps=()):
    k = len(dxs)
    d = ws[0].shape[1]
    ns = [w.shape[2] for w in ws]
    s = dxs[0].shape[0] if naturals[0] else dxs[0].shape[1]
    tm = 512

    def body(*refs):
        dx_refs, w_refs, o_ref, acc = refs[:k], refs[k:2 * k], refs[2 * k], refs[2 * k + 1]
        j = pl.program_id(1)

        @pl.when(j == 0)
        def _():
            acc[...] = jnp.zeros_like(acc)

        t = _dot_nt(dx_refs[0][...], w_refs[0][...])
        for dx_ref, w_ref in zip(dx_refs[1:], w_refs[1:]):
            t = t + _dot_nt(dx_ref[...], w_ref[...])
        acc[...] += t

        @pl.when(j == NDEV - 1)
        def _():
            o_ref[...] = acc[...]

    return pl.pallas_call(
        _ignore_deps(body, 2 * k, len(deps)), name=name, grid=(s // tm, NDEV),
        in_specs=[_act_spec(tm, n, nat, "is") for n, nat in zip(ns, naturals)]
        + [pl.BlockSpec((None, d, n), lambda i, j: (j, 0, 0)) for n in ns] + [TOKEN_SPEC] * len(deps),
        out_specs=pl.BlockSpec((tm, d), lambda i, j: (i, 0)), out_shape=_sds((s, d), F32),
        scratch_shapes=[pltpu.VMEM((tm, d), F32)], compiler_params=_cp("parallel", "arbitrary"),
    )(*dxs, *ws, *deps)


def _mm_nn(a, b, tn, out_dtype, name, res=None, tm=1024, deps=()):
    m, k = a.shape
    nn = b.shape[1]

    def body(*refs):
        if res is None:
            a_ref, b_ref, o_ref = refs
            o_ref[...] = _dot(a_ref[...], b_ref[...]).astype(out_dtype)
        else:
            a_ref, b_ref, r_ref, o_ref = refs
            o_ref[...] = (r_ref[...] + _dot(a_ref[...], b_ref[...])).astype(out_dtype)

    osp = pl.BlockSpec((tm, tn), lambda j, i: (i, j))
    in_specs = [pl.BlockSpec((tm, k), lambda j, i: (i, 0)), pl.BlockSpec((k, tn), lambda j, i: (0, j))]
    args = [a, b]
    if res is not None:
        in_specs.append(osp)
        args.append(res)
    return pl.pallas_call(
        _ignore_deps(body, len(args), len(deps)), name=name, grid=(nn // tn, m // tm),
        in_specs=in_specs + [TOKEN_SPEC] * len(deps), out_specs=osp,
        out_shape=_sds((m, nn), out_dtype), compiler_params=_cp("parallel", "parallel"),
    )(*args, *deps)


def _mm_nt(pairs, name, out_dtype=F32, tm=512, tk=512, deps=()):
    m = pairs[0][0].shape[0]
    kk = pairs[0][1].shape[0]
    p = len(pairs)

    def body(*refs):
        o_ref = refs[2 * p]
        t = _dot_nt(refs[0][...], refs[1][...])
        for q in range(1, p):
            t = t + _dot_nt(refs[2 * q][...], refs[2 * q + 1][...])
        o_ref[...] = t.astype(out_dtype)

    in_specs, args = [], []
    for a, b in pairs:
        in_specs += [pl.BlockSpec((tm, a.shape[1]), lambda j, i: (i, 0)), pl.BlockSpec((tk, b.shape[1]), lambda j, i: (j, 0))]
        args += [a, b]
    return pl.pallas_call(
        _ignore_deps(body, 2 * p, len(deps)), name=name, grid=(kk // tk, m // tm), in_specs=in_specs + [TOKEN_SPEC] * len(deps),
        out_specs=pl.BlockSpec((tm, tk), lambda j, i: (i, j)), out_shape=_sds((m, kk), out_dtype),
        compiler_params=_cp("parallel", "parallel"),
    )(*args, *deps)


def _rope_tables(s, sign):
    half = ROPE_DIMS // 2
    f32 = np.float32
    freqs = f32(ROPE_THETA) ** (-np.arange(half, dtype=f32) / f32(half))
    ang = np.arange(s, dtype=f32)[:, None] * freqs[None, :]
    cos, sin = np.cos(ang).astype(f32), (sign * np.sin(ang)).astype(f32)
    one = np.ones((s, HD - ROPE_DIMS), f32)
    zero = np.zeros((s, HD - ROPE_DIMS), f32)
    zh = np.zeros((s, half), f32)
    c = np.concatenate([cos, cos, one], axis=1)
    sa = np.concatenate([-sin, zh, zero], axis=1)
    sb = np.concatenate([zh, sin, zero], axis=1)
    return jnp.asarray(c), jnp.asarray(sa), jnp.asarray(sb)


def _rope(xv, c, sa, sb):
    return xv * c + pltpu.roll(xv, HD - ROPE_DIMS // 2, 1) * sa + pltpu.roll(xv, ROPE_DIMS // 2, 1) * sb


def _qkv_rope(h, w, tables, name):
    s, d = h.shape
    n = w.shape[2]
    per = n // HD
    tm = 1024

    def body(h_ref, w_ref, c_ref, sa_ref, sb_ref, o_ref):
        shard = pl.program_id(0)
        y = _dot(h_ref[...], w_ref[...])
        c, sa, sb = c_ref[...], sa_ref[...], sb_ref[...]
        for j in range(per):
            blk = y[:, j * HD:(j + 1) * HD]
            rot = _rope(blk, c, sa, sb)
            is_qk = shard * per + j < 2 * N_HEADS
            o_ref[:, j * HD:(j + 1) * HD] = jnp.where(is_qk, rot, blk).astype(BF)

    tab = pl.BlockSpec((tm, HD), lambda s_, i: (i, 0))
    return pl.pallas_call(
        body, name=name, grid=(NDEV, s // tm),
        in_specs=[pl.BlockSpec((tm, d), lambda s_, i: (i, 0)), pl.BlockSpec((None, d, n), lambda s_, i: (s_, 0, 0)), tab, tab, tab],
        out_specs=pl.BlockSpec((tm, n), lambda s_, i: (i, s_)), out_shape=_sds((s, NDEV * n), BF),
        compiler_params=_cp("parallel", "parallel"),
    )(h, w, *tables)


def _iota2():
    return (lax.broadcasted_iota(jnp.int32, (QB, QB), 0), lax.broadcasted_iota(jnp.int32, (QB, QB), 1))


def _softplus(z):
    return jnp.maximum(z, 0.0) + jnp.log(1.0 + jnp.exp(-jnp.abs(z)))


def _tri_dot(xv, tri, left=False):
    hi = xv.astype(BF)
    r1 = xv - hi.astype(F32)
    mid = r1.astype(BF)
    lo = (r1 - mid.astype(F32)).astype(BF)
    if left:
        return _dot(tri, hi) + _dot(tri, mid) + _dot(tri, lo)
    return _dot(hi, tri) + _dot(mid, tri) + _dot(lo, tri)


def _col(ref_or_val):
    return ref_or_val[:, 0:1]


KT = 4 * QB
QQ = 4 * QB


def _iota_tile():
    return (lax.broadcasted_iota(jnp.int32, (QQ, KT), 0), lax.broadcasted_iota(jnp.int32, (QQ, KT), 1))


def _scan_matrix(keep):
    tri = keep(*_iota2()).astype(BF)
    return jnp.concatenate([tri, tri], axis=0)


def _scan_dot(xv, tri2):
    hi = xv.astype(BF)
    lo = (xv - hi.astype(F32)).astype(BF)
    return _dot(jnp.concatenate([hi, lo], axis=1), tri2)


def _blocks(xv):
    return [xv[:, b * QB:(b + 1) * QB] for b in range(KT // QB)]


def _sb_fwd(qkv, name):
    s = qkv.shape[0]
    nb = s // QB

    def body(q_ref, k_ref, v_ref, o_ref, ot_ref, t_ref):
        i = pl.program_id(1)
        q = q_ref[...]
        row, col = _iota_tile()
        later_keys = _scan_matrix(lambda j, s_: j > s_)
        last = (i * QQ + QQ - 1) // KT

        def step(tt, carry):
            acc, later = carry
            t = last - tt
            off = pl.multiple_of(t * KT, KT)
            k = k_ref[pl.ds(off, KT), :]
            v = v_ref[pl.ds(off, KT), :]
            z = _dot_nt(q, k) * SCALE
            strict = row + (i * QQ - t * KT) > col
            sp = _softplus(z)
            lnb = jnp.where(strict, -sp, 0.0)
            afters = []
            for xb in reversed(_blocks(lnb)):
                afters.append(later + _scan_dot(xb, later_keys))
                later = later + jnp.sum(xb, axis=1, keepdims=True)
            after = jnp.concatenate(afters[::-1], axis=1)
            w = jnp.where(strict, jnp.exp((z - sp) + after), 0.0)
            return acc + _dot(w.astype(BF), v), later

        acc, total = lax.fori_loop(0, last + 1, step, (jnp.zeros((QQ, HD), F32), jnp.zeros((QQ, 1), F32)))
        o_ref[...] = acc.astype(BF)
        ot_ref[...] = acc.T.astype(BF)
        t_ref[...] = jnp.broadcast_to(total, (QQ, HD))

    blk = pl.BlockSpec((QQ, HD), lambda h, i: (i, h))
    return pl.pallas_call(
        body, name=name, grid=(N_SB, s // QQ),
        in_specs=[blk, pl.BlockSpec((s, HD), lambda h, i: (0, N_HEADS + h)), pl.BlockSpec((s, HD), lambda h, i: (0, 2 * N_HEADS + h))],
        out_specs=[blk, pl.BlockSpec((HD, QQ), lambda h, i: (h, i)), blk],
        out_shape=[_sds((s, N_SB * HD), BF), _sds((N_SB * HD, s), BF), _sds((s, N_SB * HD), F32)],
        compiler_params=_cp("parallel", "parallel"),
    )(qkv, qkv, qkv)


def _sb_bwd(qkv, do, total, name):
    s = qkv.shape[0]
    nb = s // QB

    def body(q_ref, k_ref, v_ref, do_ref, t_ref, dq_ref, dk_ref, dv_ref, dk_acc, dv_acc):
        i = pl.program_id(1)

        @pl.when(i == 0)
        def _():
            dk_acc[...] = jnp.zeros_like(dk_acc)
            dv_acc[...] = jnp.zeros_like(dv_acc)

        q = q_ref[...]
        dov = do_ref[...]
        tot = _col(t_ref[...])
        row, col = _iota_tile()
        keys_upto = _scan_matrix(lambda j, s_: j <= s_)
        keys_before = _scan_matrix(lambda j, s_: j < s_)

        def step(t, carry):
            dq, lnb_before, dl_before = carry
            off = pl.multiple_of(t * KT, KT)
            k = k_ref[pl.ds(off, KT), :]
            v = v_ref[pl.ds(off, KT), :]
            z = _dot_nt(q, k) * SCALE
            strict = row + (i * QQ - t * KT) > col
            sp = _softplus(z)
            lnb = jnp.where(strict, -sp, 0.0)
            afters = []
            for xb in _blocks(lnb):
                afters.append(tot - (lnb_before + _scan_dot(xb, keys_upto)))
                lnb_before = lnb_before + jnp.sum(xb, axis=1, keepdims=True)
            a = jnp.where(strict, jnp.exp((z - sp) + jnp.concatenate(afters, axis=1)), 0.0)
            dl = a * _dot_nt(dov, v)
            befores = []
            for xb in _blocks(dl):
                befores.append(dl_before + _scan_dot(xb, keys_before))
                dl_before = dl_before + jnp.sum(xb, axis=1, keepdims=True)
            sig = jnp.exp(z - sp)
            dz = jnp.where(strict, dl * (1.0 - sig) - sig * jnp.concatenate(befores, axis=1), 0.0) * SCALE
            dq = dq + _dot(dz.astype(BF), k)
            dk_acc[pl.ds(off, KT), :] += _dot(dz.T.astype(BF), q)
            dv_acc[pl.ds(off, KT), :] += _dot(a.T.astype(BF), dov)
            return dq, lnb_before, dl_before

        zero = jnp.zeros((QQ, 1), F32)
        dq, _, _ = lax.fori_loop(0, (i * QQ + QQ - 1) // KT + 1, step, (jnp.zeros((QQ, HD), F32), zero, zero))
        dq_ref[...] = dq.astype(BF)

        @pl.when(i == s // QQ - 1)
        def _():
            dk_ref[...] = dk_acc[...].astype(BF)
            dv_ref[...] = dv_acc[...].astype(BF)

    blk = pl.BlockSpec((QQ, HD), lambda h, i: (i, h))
    full = pl.BlockSpec((s, HD), lambda h, i: (0, h))
    return pl.pallas_call(
        body, name=name, grid=(N_SB, s // QQ),
        in_specs=[blk, pl.BlockSpec((s, HD), lambda h, i: (0, N_HEADS + h)), pl.BlockSpec((s, HD), lambda h, i: (0, 2 * N_HEADS + h)), blk, blk],
        out_specs=[blk, full, full], out_shape=[_sds((s, N_SB * HD), BF)] * 3,
        scratch_shapes=[pltpu.VMEM((s, HD), F32), pltpu.VMEM((s, HD), F32)],
        compiler_params=_cp("parallel", "arbitrary"),
    )(qkv, qkv, qkv, do, total)


def _fgate_fwd(f, b, name):
    s = f.shape[0]
    nb = s // QB
    nfox = N_HEADS - N_SB

    def body(f_ref, b_ref, cb_ref, ct_ref):
        row, col = _iota2()
        upto = (row >= col).astype(BF)
        carry = jnp.zeros((1, HD), F32)
        for blk in range(nb):
            xv = f_ref[blk * QB:(blk + 1) * QB, :] + b_ref[...]
            logf = -_softplus(-xv)
            cum = _tri_dot(logf, upto, left=True) + carry
            carry = cum[QB - 1:QB, :]
            ct_ref[blk] = cum.T
            for h in range(nfox):
                cb_ref[blk * QB:(blk + 1) * QB, h * HD:(h + 1) * HD] = jnp.broadcast_to(cum[:, h:h + 1], (QB, HD))

    return pl.pallas_call(
        body, name=name, out_shape=[_sds((s, nfox * HD), F32), _sds((nb, HD, HD), F32)], compiler_params=_cp(),
    )(f, b)


def _fgate_bwd(dcq, dck, f, b, name):
    s = f.shape[0]
    nb = s // QB
    nfox = N_HEADS - N_SB

    def body(dcq_ref, dck_ref, f_ref, b_ref, df_ref, db_ref):
        row, col = _iota2()
        from_tri = (row <= col).astype(BF)
        lane = col
        carry = jnp.zeros((1, HD), F32)
        db = jnp.zeros((1, HD), F32)
        for blk in reversed(range(nb)):
            dcum = jnp.zeros((QB, HD), F32)
            for h in range(nfox):
                here = (slice(blk * QB, (blk + 1) * QB), slice(h * HD, (h + 1) * HD))
                dcum = jnp.where(lane == h, dcq_ref[here] - dck_ref[here], dcum)
            dlogf = _tri_dot(dcum, from_tri, left=True) + carry
            carry = dlogf[0:1, :]
            xv = f_ref[blk * QB:(blk + 1) * QB, :] + b_ref[...]
            sp = _softplus(xv)
            df = jnp.where(lane < nfox, dlogf * jnp.exp(-sp), 0.0)
            df_ref[blk * QB:(blk + 1) * QB, :] = df.astype(BF)
            db = db + jnp.sum(df, axis=0, keepdims=True)
        db_ref[...] = db

    return pl.pallas_call(
        body, name=name, out_shape=[_sds((s, HD), BF), _sds((1, HD), F32)], compiler_params=_cp(),
    )(dcq, dck, f, b)


def _fox_head_row(ct_ref, j, h):
    tile = ct_ref[j]
    sub = lax.broadcasted_iota(jnp.int32, tile.shape, 0)
    return jnp.sum(jnp.where(sub == h, tile, 0.0), axis=0, keepdims=True)


def _fox_tile_row(ct_ref, t, h):
    nsub = KT // QB
    return jnp.concatenate([_fox_head_row(ct_ref, t * nsub + b, h) for b in range(nsub)], axis=1)


def _fox_fwd(qkv, cum_b, cum_t, name):
    s = qkv.shape[0]
    nb = s // QB
    nfox = N_HEADS - N_SB

    def body(q_ref, k_ref, v_ref, cq_ref, ct_ref, o_ref, ot_ref, lse_ref):
        h, i = pl.program_id(0), pl.program_id(1)
        q = q_ref[...]
        cq = _col(cq_ref[...])
        row, col = _iota_tile()

        def step(t, carry):
            acc, m, l = carry
            off = pl.multiple_of(t * KT, KT)
            k = k_ref[pl.ds(off, KT), :]
            v = v_ref[pl.ds(off, KT), :]
            z = _dot_nt(q, k) * SCALE + cq - _fox_tile_row(ct_ref, t, h)
            z = jnp.where(row + (i * QQ - t * KT) >= col, z, NEG_INF)
            m_new = jnp.maximum(m, jnp.max(z, axis=1, keepdims=True))
            alpha = jnp.exp(m - m_new)
            p = jnp.exp(z - m_new)
            l = alpha * l + jnp.sum(p, axis=1, keepdims=True)
            acc = alpha * acc + _dot(p.astype(BF), v)
            return acc, m_new, l

        acc, m, l = lax.fori_loop(0, (i * QQ + QQ - 1) // KT + 1, step,
                                  (jnp.zeros((QQ, HD), F32), jnp.full((QQ, 1), NEG_INF, F32), jnp.zeros((QQ, 1), F32)))
        o = acc / l
        o_ref[...] = o.astype(BF)
        ot_ref[...] = o.T.astype(BF)
        lse_ref[...] = jnp.broadcast_to(m + jnp.log(l), (QQ, HD))

    blk = pl.BlockSpec((QQ, HD), lambda h, i: (i, h))
    return pl.pallas_call(
        body, name=name, grid=(nfox, s // QQ),
        in_specs=[pl.BlockSpec((QQ, HD), lambda h, i: (i, N_SB + h)),
                  pl.BlockSpec((s, HD), lambda h, i: (0, N_HEADS + N_SB + h)),
                  pl.BlockSpec((s, HD), lambda h, i: (0, 2 * N_HEADS + N_SB + h)),
                  blk, pl.BlockSpec((nb, 8, HD), lambda h, i: (0, 0, 0))],
        out_specs=[blk, pl.BlockSpec((HD, QQ), lambda h, i: (h, i)), blk],
        out_shape=[_sds((s, nfox * HD), BF), _sds((nfox * HD, s), BF), _sds((s, nfox * HD), F32)],
        compiler_params=_cp("parallel", "parallel"),
    )(qkv, qkv, qkv, cum_b, cum_t)


def _fox_bwd(qkv, cum_b, cum_t, o, lse, do, name):
    s = qkv.shape[0]
    nb = s // QB
    nfox = N_HEADS - N_SB

    def body(q_ref, k_ref, v_ref, cq_ref, ct_ref, o_ref, lse_ref, do_ref, dq_ref, dk_ref, dv_ref, dcq_ref, dc_ref, dk_acc, dv_acc, dc_acc):
        h, i = pl.program_id(0), pl.program_id(1)

        @pl.when(i == 0)
        def _():
            dk_acc[...] = jnp.zeros_like(dk_acc)
            dv_acc[...] = jnp.zeros_like(dv_acc)
            dc_acc[...] = jnp.zeros_like(dc_acc)

        q = q_ref[...]
        cq = _col(cq_ref[...])
        dov = do_ref[...]
        lse_c = _col(lse_ref[...])
        delta = jnp.sum(dov.astype(F32) * o_ref[...].astype(F32), axis=1, keepdims=True)
        row, col = _iota_tile()
        ones = jnp.ones((QQ, HD), BF)

        def step(t, carry):
            dq, over_keys = carry
            off = pl.multiple_of(t * KT, KT)
            k = k_ref[pl.ds(off, KT), :]
            v = v_ref[pl.ds(off, KT), :]
            z = _dot_nt(q, k) * SCALE + cq - _fox_tile_row(ct_ref, t, h)
            p = jnp.where(row + (i * QQ - t * KT) >= col, jnp.exp(z - lse_c), 0.0)
            dz = p * (_dot_nt(dov, v) - delta)
            dzt = dz.T
            dq = dq + _dot((dz * SCALE).astype(BF), k)
            dk_acc[pl.ds(off, KT), :] += _dot((dzt * SCALE).astype(BF), q)
            dv_acc[pl.ds(off, KT), :] += _dot(p.T.astype(BF), dov)
            dc_acc[pl.ds(off, KT), :] += _tri_dot(dzt, ones)
            return dq, over_keys + jnp.sum(dz, axis=1, keepdims=True)

        dq, over_keys = lax.fori_loop(0, (i * QQ + QQ - 1) // KT + 1, step, (jnp.zeros((QQ, HD), F32), jnp.zeros((QQ, 1), F32)))
        dq_ref[...] = dq.astype(BF)
        dcq_ref[...] = jnp.broadcast_to(over_keys, (QQ, HD))

        @pl.when(i == s // QQ - 1)
        def _():
            dk_ref[...] = dk_acc[...].astype(BF)
            dv_ref[...] = dv_acc[...].astype(BF)
            dc_ref[...] = dc_acc[...]

    blk = pl.BlockSpec((QQ, HD), lambda h, i: (i, h))
    full = pl.BlockSpec((s, HD), lambda h, i: (0, h))
    return pl.pallas_call(
        body, name=name, grid=(nfox, s // QQ),
        in_specs=[pl.BlockSpec((QQ, HD), lambda h, i: (i, N_SB + h)),
                  pl.BlockSpec((s, HD), lambda h, i: (0, N_HEADS + N_SB + h)),
                  pl.BlockSpec((s, HD), lambda h, i: (0, 2 * N_HEADS + N_SB + h)),
                  blk, pl.BlockSpec((nb, 8, HD), lambda h, i: (0, 0, 0)), blk, blk,
                  pl.BlockSpec((QQ, HD), lambda h, i: (i, N_SB + h))],
        out_specs=[blk, full, full, blk, full],
        out_shape=[_sds((s, nfox * HD), BF)] * 3 + [_sds((s, nfox * HD), F32)] * 2,
        scratch_shapes=[pltpu.VMEM((s, HD), F32)] * 3,
        compiler_params=_cp("parallel", "arbitrary"),
    )(qkv, qkv, qkv, cum_b, cum_t, o, lse, do)


GB = 4
DIL_PAD = QB * 16


def _dil_group(g, d, off=0, shift=0):
    if d == 1:
        return [pl.ds(pl.multiple_of(off + (g * GB + shift) * QB, QB), GB * QB)]
    if d == 4:
        return [pl.ds(off + g + shift * QB * d, GB * QB, stride=d)]
    assert d == 16 and shift == 0
    return [pl.ds(off + g * GB + b, QB, stride=d) for b in range(GB)]


def _dil_load(ref, g, d, off=0, shift=0):
    parts = [ref[sl, :] for sl in _dil_group(g, d, off, shift)]
    rows = parts[0] if len(parts) == 1 else jnp.concatenate(parts, axis=0)
    return rows.reshape(GB, QB, HD)


def _dil_store(ref, g, d, val, off=0, shift=0, add=False):
    rows = val.reshape(GB * QB, HD)
    slices = _dil_group(g, d, off, shift)
    for b, sl in enumerate(slices):
        piece = rows if len(slices) == 1 else rows[b * QB:(b + 1) * QB]
        if add:
            ref[sl, :] += piece
        else:
            ref[sl, :] = piece


def _bdot_nt(a, b):
    return lax.dot_general(a, b, (((2,), (2,)), ((0,), (0,))), preferred_element_type=F32)


def _bdot(a, b):
    return lax.dot_general(a, b, (((2,), (1,)), ((0,), (0,))), preferred_element_type=F32)


def _bdot_tn(a, b):
    return lax.dot_general(jnp.swapaxes(a, 1, 2).astype(BF), b, (((2,), (1,)), ((0,), (0,))), preferred_element_type=F32)


def _dil_masks(g, d, nb):
    row = lax.broadcasted_iota(jnp.int32, (GB, QB, QB), 1)
    col = lax.broadcasted_iota(jnp.int32, (GB, QB, QB), 2)
    blk = lax.broadcasted_iota(jnp.int32, (GB, QB, QB), 0) + (g * GB if d == 1 else 0)
    return col <= row, jnp.logical_and(col >= row, blk >= 1) if nb > 1 else None


def _dilated_fwd(qkv, name):
    s = qkv.shape[0]
    npat = len(DILATED_PATTERNS)
    chunk = 256

    def body(q_ref, k_ref, v_ref, out_ref, outt_ref, g_ref, qf, kf, vf, *per_pattern):
        o_s, l_s = per_pattern[:npat], per_pattern[npat:]
        qf[...] = q_ref[...].astype(F32)
        for dst, src in ((kf, k_ref), (vf, v_ref)):
            dst[0:DIL_PAD, :] = jnp.zeros((DIL_PAD, HD), F32)
            dst[DIL_PAD:, :] = src[...].astype(F32)
        for p, (_, d) in enumerate(DILATED_PATTERNS):
            nb = s // d // QB

            def grp(g, carry, p=p, d=d, nb=nb):
                mc, mp = _dil_masks(g, d, nb)
                q = _dil_load(qf, g, d).astype(BF)
                zc = jnp.where(mc, _bdot_nt(q, _dil_load(kf, g, d, DIL_PAD).astype(BF)) * SCALE, NEG_INF)
                m = jnp.max(zc, axis=2, keepdims=True)
                if nb > 1:
                    zp = jnp.where(mp, _bdot_nt(q, _dil_load(kf, g, d, DIL_PAD, -1).astype(BF)) * SCALE, NEG_INF)
                    m = jnp.maximum(m, jnp.max(zp, axis=2, keepdims=True))
                ec = jnp.exp(zc - m)
                l = jnp.sum(ec, axis=2, keepdims=True)
                if nb > 1:
                    ep = jnp.where(mp, jnp.exp(zp - m), 0.0)
                    l = l + jnp.sum(ep, axis=2, keepdims=True)
                o = _bdot((ec / l).astype(BF), _dil_load(vf, g, d, DIL_PAD).astype(BF))
                if nb > 1:
                    o = o + _bdot((ep / l).astype(BF), _dil_load(vf, g, d, DIL_PAD, -1).astype(BF))
                _dil_store(o_s[p], g, d, o)
                _dil_store(l_s[p], g, d, jnp.broadcast_to(m + jnp.log(l), (GB, QB, HD)))
                return carry

            lax.fori_loop(0, s // (QB * GB), grp, 0)
        for c0 in range(0, s, chunk):
            rows = slice(c0, c0 + chunk)
            ls = [l_s[p][rows, :] for p in range(npat)]
            m = functools.reduce(jnp.maximum, ls)
            es = [jnp.exp(l - m) for l in ls]
            tot = functools.reduce(lambda a, b: a + b, es)
            out = functools.reduce(lambda a, b: a + b, [(e / tot) * o_s[p][rows, :] for p, e in enumerate(es)])
            out_ref[rows, :] = out.astype(BF)
            outt_ref[:, rows] = out.T.astype(BF)
            g_ref[rows, :] = m + jnp.log(tot)

    full = pl.BlockSpec((s, HD), lambda h: (0, h))
    return pl.pallas_call(
        body, name=name, grid=(N_HEADS,),
        in_specs=[full, pl.BlockSpec((s, HD), lambda h: (0, N_HEADS + h)), pl.BlockSpec((s, HD), lambda h: (0, 2 * N_HEADS + h))],
        out_specs=[full, pl.BlockSpec((HD, s), lambda h: (h, 0)), full],
        out_shape=[_sds((s, N_HEADS * HD), BF), _sds((N_HEADS * HD, s), BF), _sds((s, N_HEADS * HD), F32)],
        scratch_shapes=[pltpu.VMEM((s, HD), F32)] + [pltpu.VMEM((s + DIL_PAD, HD), F32)] * 2 + [pltpu.VMEM((s, HD), F32)] * (2 * npat),
        compiler_params=_cp("parallel"),
    )(qkv, qkv, qkv)


def _dilated_bwd(qkv, out, glse, do, tables, name):
    s = qkv.shape[0]
    chunk = 256

    def body(q_ref, k_ref, v_ref, out_ref, g_ref, do_ref, c_ref, sa_ref, sb_ref, dq_ref, dk_ref, dv_ref,
             qf, kf, vf, dof, dl_s, dq_a, dk_a, dv_a):
        qf[...] = q_ref[...].astype(F32)
        for dst, src in ((kf, k_ref), (vf, v_ref)):
            dst[0:DIL_PAD, :] = jnp.zeros((DIL_PAD, HD), F32)
            dst[DIL_PAD:, :] = src[...].astype(F32)
        for c0 in range(0, s, chunk):
            rows = slice(c0, c0 + chunk)
            dov = do_ref[rows, :].astype(F32)
            dof[rows, :] = dov
            dl_s[rows, :] = jnp.broadcast_to(jnp.sum(dov * out_ref[rows, :].astype(F32), axis=1, keepdims=True), (chunk, HD))
        dq_a[...] = jnp.zeros_like(dq_a)
        dk_a[...] = jnp.zeros_like(dk_a)
        dv_a[...] = jnp.zeros_like(dv_a)
        for _, d in DILATED_PATTERNS:
            nb = s // d // QB

            def grp(g, carry, d=d, nb=nb):
                mc, mp = _dil_masks(g, d, nb)
                q = _dil_load(qf, g, d).astype(BF)
                kc = _dil_load(kf, g, d, DIL_PAD).astype(BF)
                dov = _dil_load(dof, g, d).astype(BF)
                lse = _dil_load(g_ref, g, d)[:, :, 0:1]
                delta = _dil_load(dl_s, g, d)[:, :, 0:1]
                pc = jnp.where(mc, jnp.exp(_bdot_nt(q, kc) * SCALE - lse), 0.0)
                dzc = pc * (_bdot_nt(dov, _dil_load(vf, g, d, DIL_PAD).astype(BF)) - delta) * SCALE
                dq = _bdot(dzc.astype(BF), kc)
                if nb > 1:
                    kp = _dil_load(kf, g, d, DIL_PAD, -1).astype(BF)
                    pp = jnp.where(mp, jnp.exp(_bdot_nt(q, kp) * SCALE - lse), 0.0)
                    dzp = pp * (_bdot_nt(dov, _dil_load(vf, g, d, DIL_PAD, -1).astype(BF)) - delta) * SCALE
                    dq = dq + _bdot(dzp.astype(BF), kp)
                _dil_store(dq_a, g, d, dq, add=True)
                _dil_store(dk_a, g, d, _bdot_tn(dzc, q), DIL_PAD, add=True)
                _dil_store(dv_a, g, d, _bdot_tn(pc, dov), DIL_PAD, add=True)
                if nb > 1:
                    _dil_store(dk_a, g, d, _bdot_tn(dzp, q), DIL_PAD, -1, add=True)
                    _dil_store(dv_a, g, d, _bdot_tn(pp, dov), DIL_PAD, -1, add=True)
                return carry

            lax.fori_loop(0, s // (QB * GB), grp, 0)
        for c0 in range(0, s, chunk):
            rows = slice(c0, c0 + chunk)
            padded = slice(DIL_PAD + c0, DIL_PAD + c0 + chunk)
            c, sa, sb = c_ref[rows, :], sa_ref[rows, :], sb_ref[rows, :]
            dq_ref[rows, :] = _rope(dq_a[rows, :], c, sa, sb).astype(BF)
            dk_ref[rows, :] = _rope(dk_a[padded, :], c, sa, sb).astype(BF)
            dv_ref[rows, :] = dv_a[padded, :].astype(BF)

    full = pl.BlockSpec((s, HD), lambda h: (0, h))
    tab = pl.BlockSpec((s, HD), lambda h: (0, 0))
    return pl.pallas_call(
        body, name=name, grid=(N_HEADS,),
        in_specs=[full, pl.BlockSpec((s, HD), lambda h: (0, N_HEADS + h)), pl.BlockSpec((s, HD), lambda h: (0, 2 * N_HEADS + h)),
                  full, full, full, tab, tab, tab],
        out_specs=[full, full, full], out_shape=[_sds((s, N_HEADS * HD), BF)] * 3,
        scratch_shapes=[pltpu.VMEM((s, HD), F32)] + [pltpu.VMEM((s + DIL_PAD, HD), F32)] * 2 + [pltpu.VMEM((s, HD), F32)] * 3
        + [pltpu.VMEM((s + DIL_PAD, HD), F32)] * 2,
        compiler_params=_cp("parallel"),
    )(qkv, qkv, qkv, out, glse, do, *tables)


def _swiglu_fwd(x, gnorm, w, tag, deps=()):
    h, ht = _rmsnorm_fwd(x, gnorm, f"norm_{tag}", deps)
    g, u, act, act_t = _ffn_up(h, w["gate"], w["up"], f"ffn_up_{tag}")
    if callable(w["down"]):
        w["down"] = w["down"](g)
    y = _ffn_down(act, w["down"], x, f"ffn_down_{tag}")
    return y, (x, ht, g, u, act_t)


def _swiglu_bwd(saved, gnorm, w, dy, dyb_half, out_scale, tag, deps=(), on_down=None, on_grads=None):
    x, ht, g, u, act_t = saved
    dg, du = _ffn_bwd_act(dyb_half, w["down"], g, u, f"ffn_bwd_act_{tag}", deps)
    if on_down:
        d_gate, d_up = _grad_cols(ht, [dg, du], [False, False], f"ffn_bwd_wgu_{tag}")
        tokens = list(on_grads({"gate": d_gate, "up": d_up}))
        d_down = _grad_rows(act_t, dyb_half, f"ffn_bwd_wd_{tag}", tokens)
        gw = {"gate": d_gate, "up": d_up, "down": d_down}
        tokens = list(on_down(d_down))
    else:
        d_down = _grad_rows(act_t, dyb_half, f"ffn_bwd_wd_{tag}")
        d_gate, d_up = _grad_cols(ht, [dg, du], [False, False], f"ffn_bwd_wgu_{tag}")
        gw = {"gate": d_gate, "up": d_up, "down": d_down}
        tokens = list(on_grads(gw)) if on_grads else []
    dh = _dh_cols([dg, du], [w["gate"], w["up"]], [False, False], f"ffn_bwd_dh_{tag}", tokens)
    dx, dxb, dxbt, dgn = _rmsnorm_bwd(x, gnorm, dh, dy, out_scale, f"norm_bwd_{tag}")
    return (dx, dxb, dxbt), dgn, gw


def kernel(x, norm_g, ffn1_w_gate, ffn1_w_up, ffn1_w_down, ffn2_w_gate, ffn2_w_up, ffn2_w_down, even_w_in, even_b_forget, even_w_out, odd_w_qkv, odd_w_out, final_norm_g, loss_target, m_norm_g, m_ffn1_w_gate, m_ffn1_w_up, m_ffn1_w_down, m_ffn2_w_gate, m_ffn2_w_up, m_ffn2_w_down, m_even_w_in, m_even_b_forget, m_even_w_out, m_odd_w_qkv, m_odd_w_out, m_final_norm_g, v_norm_g, v_ffn1_w_gate, v_ffn1_w_up, v_ffn1_w_down, v_ffn2_w_gate, v_ffn2_w_up, v_ffn2_w_down, v_even_w_in, v_even_b_forget, v_even_w_out, v_odd_w_qkv, v_odd_w_out, v_final_norm_g):
    s, d = x.shape[1], x.shape[2]
    nfox = N_HEADS - N_SB
    ax, ay, ac = lax.axis_index("x"), lax.axis_index("y"), lax.axis_index("c")
    me = 4 * ax + 2 * ay + ac
    slots = jnp.stack([4 * px + 2 * py + ac for px, py in [(ax, ay), (1 - ax, ay), (ax, 1 - ay), (1 - ax, 1 - ay)]]).astype(jnp.int32)
    x0 = x.reshape(s, d)
    target = loss_target.reshape(s, d)

    def bf(w):
        return w.astype(BF)

    groups = [
        [bf(ffn1_w_gate[0]), bf(ffn1_w_up[0]), norm_g.reshape(6, d // NDEV)],
        [bf(even_w_in[0]), bf(even_w_out[0])],
        [bf(ffn2_w_gate[0]), bf(ffn2_w_up[0]), bf(ffn2_w_down[0])],
        [bf(ffn1_w_gate[1]), bf(ffn1_w_up[1]), bf(ffn1_w_down[1])],
        [bf(odd_w_qkv[0]), bf(odd_w_out[0])],
        [bf(ffn2_w_gate[1]), bf(ffn2_w_up[1]), bf(ffn2_w_down[1])],
        [bf(ffn1_w_down[0])],
    ]
    started = [None] * len(groups)
    last_token = []
    for k in (0, 6, 1, 2, 3, 4, 5):
        started[k] = _gather_start(groups[k], me, last_token, f"gather_start_{k}")
        last_token = [started[k]["token"]]
    all_started = last_token

    def forward_early(k, after):
        started[k] = _gather_forward(started[k], after, f"gather_forward_{k}")
        return [started[k]["token"]]

    def gathered(k, after):
        st = started[k] if "send2" in started[k] else _gather_forward(started[k], after, f"gather_forward_{k}")
        return _gather_finish(st, after, f"gather_finish_{k}")

    def ffn_weights(ws_):
        return {"gate": ws_[0], "up": ws_[1], "down": ws_[2]}

    b_pad = jnp.pad(even_b_forget, ((0, 0), (0, HD - nfox)))
    gfin = final_norm_g.reshape(1, d)

    g0 = gathered(0, x0)
    gn = jnp.transpose(g0[2], (1, 0, 2)).reshape(6, 1, d)
    wf = [[{"gate": g0[0], "up": g0[1], "down": lambda after: gathered(6, after)[0]}, None], [None, None]]
    x1, sv_f1_0 = _swiglu_fwd(x0, gn[0], wf[0][0], "l0a", all_started)
    g1 = gathered(1, x1)
    w_in_nat = jnp.transpose(g1[0], (1, 0, 2)).reshape(d, -1)
    w_qkv_e = w_in_nat[:, :3 * d]
    w_f = jnp.pad(w_in_nat[:, 3 * d:], ((0, 0), (0, HD - nfox)))
    w_out_e = g1[1].reshape(d, d)
    h_e, ht_e = _rmsnorm_fwd(x1, gn[1], "norm_l0m")
    qkv_e = _mm_nn(h_e, w_qkv_e, 768, BF, "even_qkv")
    f_e = _mm_nn(h_e, w_f, HD, F32, "even_fgate")
    o_sb, ot_sb, tot_sb = _sb_fwd(qkv_e, "sb_fwd")
    cum_b, cum_t = _fgate_fwd(f_e, b_pad, "fgate_fwd")
    o_fox, ot_fox, lse_fox = _fox_fwd(qkv_e, cum_b, cum_t, "fox_fwd")
    o_e = jnp.concatenate([o_sb, o_fox], axis=1)
    ot_e = jnp.concatenate([ot_sb, ot_fox], axis=0)
    x2 = _mm_nn(o_e, w_out_e, 1024, F32, "even_out", res=x1, deps=forward_early(2, o_e))
    wf[0][1] = ffn_weights(gathered(2, x2))
    x3, sv_f2_0 = _swiglu_fwd(x2, gn[2], wf[0][1], "l0b")

    wf[1][0] = ffn_weights(gathered(3, x3))
    x4, sv_f1_1 = _swiglu_fwd(x3, gn[3], wf[1][0], "l1a")
    g4 = gathered(4, x4)
    w_qkv_o = g4[0]
    w_out_o = g4[1].reshape(d, d)
    h_o, ht_o = _rmsnorm_fwd(x4, gn[4], "norm_l1m")
    qkv_o = _qkv_rope(h_o, w_qkv_o, _rope_tables(s, 1.0), "odd_qkv")
    o_o, ot_o, glse = _dilated_fwd(qkv_o, "dilated_fwd")
    x5 = _mm_nn(o_o, w_out_o, 1024, F32, "odd_out", res=x4)
    wf[1][1] = ffn_weights(gathered(5, x5))
    x6, sv_f2_1 = _swiglu_fwd(x5, gn[5], wf[1][1], "l1b")

    def chip_sums(gs, a_s, tag):
        ps = _pair_sum(gs, a_s, slots, f"pair_sum_{tag}")
        return gs, a_s, _chip_start(ps, f"chip_start_{tag}")

    def as_slices(gs):
        return [g_ if g_.ndim == 3 else g_.reshape(NDEV, g_.shape[0] // NDEV, g_.shape[1]) for g_ in gs]

    def reduce_start(gs, tag):
        gs = as_slices(gs)
        return chip_sums(gs, _pair_exchange(gs, f"pair_exchange_{tag}"), tag)

    red, crossing = {}, {}

    def cross(gs, tag):
        crossing[tag] = _pair_start(as_slices(gs), f"pair_start_{tag}")
        return [crossing[tag]["token"]]

    def reduce_behind_dh(tag):
        return lambda gw: cross([gw["gate"], gw["up"], gw["down"]], tag)

    def reduce_after(tag, after):
        red[tag] = chip_sums(*_pair_finish(crossing[tag], after, f"pair_finish_{tag}"), tag)
        return [red[tag][2]["token"]]

    def reduce_now(tag, names):
        def hook(gw):
            red[tag] = reduce_start([gw[nm] for nm in names] if names else [gw], tag)
            return [red[tag][2]["token"]]
        return hook

    dx6, dx6b, _, d_gfin, loss_part = _loss_head(x6, gfin, target, "loss_head")

    (dx5, dx5b, dx5bt), dgn5, _ = _swiglu_bwd(sv_f2_1, gn[5], wf[1][1], dx6, dx6b, 1.0, "l1b", on_grads=reduce_behind_dh("l1b"))
    d_wout_o = _mm_nn(ot_o, dx5b, 1024, BF, "odd_out_dw")
    do_o = _mm_nt([(dx5b, w_out_o)], "odd_out_do", BF, deps=reduce_after("l1b", dx5))
    dqkv_o = jnp.concatenate(_dilated_bwd(qkv_o, o_o, glse, do_o, _rope_tables(s, -1.0), "dilated_bwd"), axis=1)
    (d_wqkv_o,) = _grad_cols(ht_o, [dqkv_o], [True], "odd_qkv_dw")
    dh_o = _dh_cols([dqkv_o], [w_qkv_o], [True], "odd_qkv_dh", cross([d_wqkv_o, d_wout_o], "l1m"))
    dx4, dx4b, _, dgn4 = _rmsnorm_bwd(x4, gn[4], dh_o, dx5, 0.5, "norm_bwd_l1m")
    (dx3, dx3b, _), dgn3, _ = _swiglu_bwd(sv_f1_1, gn[3], wf[1][0], dx4, dx4b, 0.5, "l1a", reduce_after("l1m", dx4),
                                         on_grads=reduce_behind_dh("l1a"))

    (dx2, dx2b, dx2bt), dgn2, _ = _swiglu_bwd(sv_f2_0, gn[2], wf[0][1], dx3, dx3b, 1.0, "l0b", reduce_after("l1a", dx3),
                                             on_grads=reduce_behind_dh("l0b"))
    d_wout_e = _mm_nn(ot_e, dx2b, 1024, BF, "even_out_dw")
    do_e = _mm_nt([(dx2b, w_out_e)], "even_out_do", BF, deps=reduce_after("l0b", dx2))
    dq_sb, dk_sb, dv_sb = _sb_bwd(qkv_e, do_e, tot_sb, "sb_bwd")
    dq_fx, dk_fx, dv_fx, dcq, dck = _fox_bwd(qkv_e, cum_b, cum_t, o_fox, lse_fox, do_e, "fox_bwd")
    df, db_part = _fgate_bwd(dcq, dck, f_e, b_pad, "fgate_bwd")
    dqkv_e = jnp.concatenate([dq_sb, dq_fx, dk_sb, dk_fx, dv_sb, dv_fx], axis=1)
    d_wqkv_e = _mm_nn(ht_e, dqkv_e, 768, BF, "even_qkv_dw")
    d_wf = _mm_nn(ht_e, df, HD, BF, "even_fgate_dw")
    d_win_nat = jnp.concatenate([d_wqkv_e, d_wf[:, :nfox]], axis=1)
    d_win = jnp.transpose(d_win_nat.reshape(d, NDEV, -1), (1, 0, 2))
    dh_e = _mm_nt([(dqkv_e, w_qkv_e), (df, w_f)], "even_in_dh", deps=cross([d_win, d_wout_e], "l0m"))
    dx1, dx1b, _, dgn1 = _rmsnorm_bwd(x1, gn[1], dh_e, dx2, 0.5, "norm_bwd_l0m")
    (dx0, _, _), dgn0, _ = _swiglu_bwd(sv_f1_0, gn[0], wf[0][0], dx1, dx1b, 1.0, "l0a", reduce_after("l0m", dx1),
                                      on_down=reduce_now("l0a_down", None), on_grads=reduce_now("l0a_gu", ["gate", "up"]))

    def reduce_finish(red, tag, after):
        gs, a_s, st = red
        return list(zip(gs, a_s, _chip_finish(st, after, f"chip_finish_{tag}")))

    f_l1b, f_l1m, f_l1a = (reduce_finish(red[t], t, dx0) for t in ("l1b", "l1m", "l1a"))
    f_l0b, f_l0m = (reduce_finish(red[t], t, dx0) for t in ("l0b", "l0m"))

    def update(w_, m_, v_, parts, nm):
        if w_.shape[2] % 128 == 0:
            return _adamw_sharded(w_, m_, v_, parts, slots, f"adamw_{nm}")
        outs = _adamw_sharded(jnp.swapaxes(w_, 1, 2), jnp.swapaxes(m_, 1, 2), jnp.swapaxes(v_, 1, 2), parts, slots,
                              f"adamw_{nm}", transposed=True)
        return [jnp.swapaxes(o, 1, 2) for o in outs]

    res = {}
    res["even_w_in"] = update(even_w_in, m_even_w_in, v_even_w_in, [f_l0m[0]], "even_w_in")
    res["even_w_out"] = _adamw_sharded(even_w_out, m_even_w_out, v_even_w_out, [f_l0m[1]], slots, "adamw_even_w_out")
    res["odd_w_qkv"] = _adamw_sharded(odd_w_qkv, m_odd_w_qkv, v_odd_w_qkv, [f_l1m[0]], slots, "adamw_odd_w_qkv")
    res["odd_w_out"] = _adamw_sharded(odd_w_out, m_odd_w_out, v_odd_w_out, [f_l1m[1]], slots, "adamw_odd_w_out")
    names = ["ffn2_w_gate", "ffn2_w_up", "ffn2_w_down", "ffn1_w_gate", "ffn1_w_up", "ffn1_w_down"]
    ws = [ffn2_w_gate, ffn2_w_up, ffn2_w_down, ffn1_w_gate, ffn1_w_up, ffn1_w_down]
    ms = [m_ffn2_w_gate, m_ffn2_w_up, m_ffn2_w_down, m_ffn1_w_gate, m_ffn1_w_up, m_ffn1_w_down]
    vs = [v_ffn2_w_gate, v_ffn2_w_up, v_ffn2_w_down, v_ffn1_w_gate, v_ffn1_w_up, v_ffn1_w_down]
    for k in range(3):
        res[names[k]] = update(ws[k], ms[k], vs[k], [f_l0b[k], f_l1b[k]], names[k])
    f_l0a = (reduce_finish(red["l0a_gu"], "l0a_gu", res["ffn2_w_down"][1])
             + reduce_finish(red["l0a_down"], "l0a_down", res["ffn2_w_down"][1]))
    for k in range(3, 6):
        res[names[k]] = update(ws[k], ms[k], vs[k], [f_l0a[k - 3], f_l1a[k - 3]], names[k])

    dnorm = jnp.concatenate([dgn0, dgn1, dgn2, dgn3, dgn4, dgn5], axis=0)
    nsm = d // NDEV
    small_rows = (6 * d + d + 2 * HD) // HD
    pad_rows = -small_rows % 8
    part = jnp.concatenate([dnorm.reshape(-1), d_gfin.reshape(-1), db_part.reshape(-1), loss_part.reshape(-1),
                            jnp.zeros((pad_rows * HD,), F32)]).reshape(small_rows + pad_rows, HD)
    (gathered,) = _all_gather([part], "gather_small")

    def pack(ng, bfg, fg):
        full = lax.dynamic_update_slice(jnp.zeros((6, d), F32), ng.reshape(6, nsm), (0, me * nsm))
        return jnp.concatenate([full.reshape(-1), fg.reshape(-1), jnp.pad(bfg.reshape(-1), (0, HD - nfox)),
                                jnp.zeros((HD + pad_rows * HD,), F32)]).reshape(small_rows + pad_rows, HD)

    sm = _adamw_small(pack(norm_g, even_b_forget, final_norm_g), pack(m_norm_g, m_even_b_forget, m_final_norm_g),
                      pack(v_norm_g, v_even_b_forget, v_final_norm_g), gathered, "adamw_small")

    def unpack(t):
        flat = t.reshape(-1)
        ng = lax.dynamic_slice(flat[:6 * d].reshape(6, d), (0, me * nsm), (6, nsm)).reshape(norm_g.shape)
        fg = flat[6 * d:7 * d].reshape(final_norm_g.shape)
        bfg = flat[7 * d:7 * d + nfox].reshape(even_b_forget.shape)
        return ng, bfg, fg

    sm_g, sm_d, sm_m, sm_v = [unpack(t) for t in sm]
    loss = sm[0].reshape(-1)[7 * d + HD]

    order = ["norm_g", "ffn1_w_gate", "ffn1_w_up", "ffn1_w_down", "ffn2_w_gate", "ffn2_w_up", "ffn2_w_down", "even_w_in",
             "even_b_forget", "even_w_out", "odd_w_qkv", "odd_w_out", "final_norm_g"]
    outs = [loss, dx0.reshape(x.shape)]
    for k in range(4):
        smk = [sm_g, sm_d, sm_m, sm_v][k]
        for nm in order:
            if nm == "norm_g":
                outs.append(smk[0])
            elif nm == "even_b_forget":
                outs.append(smk[1])
            elif nm == "final_norm_g":
                outs.append(smk[2])
            else:
                outs.append(res[nm][k])
    return tuple(outs)
```

```python
import functools

import jax
import jax.numpy as jnp
import numpy as np
from jax import lax
from jax.experimental import pallas as pl
from jax.experimental.pallas import tpu as pltpu

F32 = jnp.float32
BF = jnp.bfloat16
NDEV = 8
HD = 128
QB = 128
N_HEADS = 16
N_SB = 8
SCALE = HD ** -0.5
ROPE_THETA = 500000.0
ROPE_DIMS = HD // 4
DILATED_PATTERNS = ((128, 1), (512, 4), (2048, 16))
RMS_EPS = 1e-6
NEG_INF = -1e30
ADAM_LR = 0.001
ADAM_B1 = 0.9
ADAM_B2 = 0.999
ADAM_EPS = 1e-08
ADAM_WD = 0.01
ADAM_STEP = 10
VMEM_LIMIT_V7X = 56 * 1024 * 1024
MESH = pl.DeviceIdType.MESH
ANY = pl.BlockSpec(memory_space=pl.ANY)

NT_DIMS = (((1,), (1,)), ((), ()))


def _cp(*dims):
    return pltpu.CompilerParams(dimension_semantics=dims if dims else None, vmem_limit_bytes=VMEM_LIMIT_V7X)


def _dot(a, b):
    return jnp.dot(a, b, preferred_element_type=F32)


def _dot_nt(a, b):
    return lax.dot_general(a, b, NT_DIMS, preferred_element_type=F32)


def _sds(shape, dtype):
    return jax.ShapeDtypeStruct(shape, dtype)


def _place():
    x, y, c = lax.axis_index("x"), lax.axis_index("y"), lax.axis_index("c")
    chips = [(x, y), (1 - x, y), (x, 1 - y), (1 - x, 1 - y)]
    return x, y, c, chips


def _all_gather(xs, name):
    n = len(xs)

    def body(*refs):
        x_refs, out_refs = refs[:n], refs[n:2 * n]
        send_sems, recv_sems, local_sems = refs[2 * n:]
        x, y, c, chips = _place()
        me, sibling = (x, y, c), (x, y, 1 - c)
        others = chips[1:]

        def slot(a, px, py, pc):
            return out_refs[a].at[4 * px + 2 * py + pc]

        def copy(a, k, block, to, src=None):
            return pltpu.make_async_remote_copy(
                src_ref=slot(a, *block) if src is None else src, dst_ref=slot(a, *block),
                send_sem=send_sems.at[a, k], recv_sem=recv_sems.at[a, k], device_id=to, device_id_type=MESH)

        started = []
        for a in range(n):
            mine = pltpu.make_async_copy(x_refs[a], slot(a, *me), local_sems.at[a])
            mine.start()
            first = [copy(a, 0, me, sibling, src=x_refs[a])]
            first += [copy(a, 1 + j, me, (*chip, c), src=x_refs[a]) for j, chip in enumerate(others)]
            for cp in first:
                cp.start()
            started += [mine.wait] + [cp.wait_send for cp in first]
        for a in range(n):
            for j, chip in enumerate(others):
                copy(a, 1 + j, (*chip, c), me).wait_recv()
                passed = copy(a, 4 + j, (*chip, c), sibling)
                passed.start()
                started.append(passed.wait_send)
        for a in range(n):
            copy(a, 0, sibling, me).wait_recv()
            for j, chip in enumerate(others):
                copy(a, 4 + j, (*chip, 1 - c), me).wait_recv()
        for w in started:
            w()

    return pl.pallas_call(
        body, name=name,
        out_shape=[_sds((NDEV,) + x.shape, x.dtype) for x in xs],
        in_specs=[ANY] * n, out_specs=[ANY] * n,
        scratch_shapes=[pltpu.SemaphoreType.DMA((n, 7)), pltpu.SemaphoreType.DMA((n, 7)), pltpu.SemaphoreType.DMA((n,))],
    )(*xs)


def _pair_exchange(gs, name):
    n = len(gs)

    def body(*refs):
        g_refs, a_refs = refs[:n], refs[n:2 * n]
        send_sems, recv_sems = refs[2 * n:]
        x, y, c, chips = _place()
        copies = []
        for a in range(n):
            for j, (px, py) in enumerate(chips):
                copies.append(pltpu.make_async_remote_copy(
                    src_ref=g_refs[a].at[4 * px + 2 * py + (1 - c)], dst_ref=a_refs[a].at[j],
                    send_sem=send_sems.at[a, j], recv_sem=recv_sems.at[a, j],
                    device_id=(x, y, 1 - c), device_id_type=MESH))
        for cp in copies:
            cp.start()
        for cp in copies:
            cp.wait()

    return pl.pallas_call(
        body, name=name,
        out_shape=[_sds((4,) + g.shape[1:], g.dtype) for g in gs],
        in_specs=[ANY] * n, out_specs=[ANY] * n,
        scratch_shapes=[pltpu.SemaphoreType.DMA((n, 4)), pltpu.SemaphoreType.DMA((n, 4))],
    )(*gs)


HBM = pl.BlockSpec(memory_space=pltpu.HBM)
SEM = pl.BlockSpec(memory_space=pltpu.SEMAPHORE)
EFFECT = pltpu.SideEffectType.DATAFLOW_SIDE_EFFECTING
TOKEN = _sds((8, 128), F32)
TOKEN_SPEC = pl.BlockSpec((8, 128), lambda *_: (0, 0))


def _in_hbm(x):
    return pltpu.with_memory_space_constraint(x, pltpu.HBM)


def _ignore_deps(body, n_in, n_deps):
    if not n_deps:
        return body
    return lambda *refs: body(*refs[:n_in], *refs[n_in + n_deps:])


def _slot_of(px, py, pc):
    return 4 * px + 2 * py + pc


def _gather_start(xs, me, deps, name):
    n = len(xs)
    lands = [lax.dynamic_update_slice(lax.empty((NDEV,) + x.shape, x.dtype), x[None], (me,) + (0,) * x.ndim) for x in xs]

    def body(*refs):
        x_refs, land_refs = refs[:n], refs[n:2 * n]
        send, recv_ici, recv_sib = refs[2 * n:2 * n + 3]
        token = refs[4 * n + 3]
        x, y, c, chips = _place()
        for a in range(n):
            dst = land_refs[a].at[_slot_of(x, y, c)]
            pltpu.make_async_remote_copy(src_ref=x_refs[a], dst_ref=dst, send_sem=send.at[4 * a], recv_sem=recv_sib.at[a],
                                         device_id=(x, y, 1 - c), device_id_type=MESH).start()
            for j, chip in enumerate(chips[1:]):
                pltpu.make_async_remote_copy(src_ref=x_refs[a], dst_ref=dst, send_sem=send.at[4 * a + 1 + j], recv_sem=recv_ici.at[3 * a + j],
                                             device_id=(*chip, c), device_id_type=MESH).start()
        token[...] = jnp.zeros_like(token)

    outs = pl.pallas_call(
        _ignore_deps(body, 2 * n, len(deps)), name=name,
        out_shape=(pltpu.SemaphoreType.DMA((4 * n,)), pltpu.SemaphoreType.DMA((3 * n,)), pltpu.SemaphoreType.DMA((n,)),
                   *[pltpu.HBM(x.shape, x.dtype) for x in xs], *[pltpu.HBM(l.shape, l.dtype) for l in lands], TOKEN),
        in_specs=[HBM] * (2 * n) + [TOKEN_SPEC] * len(deps),
        out_specs=(SEM, SEM, SEM, *[HBM] * (2 * n), pl.BlockSpec(memory_space=pltpu.VMEM)),
        input_output_aliases={a: 3 + a for a in range(2 * n)},
        compiler_params=pltpu.CompilerParams(has_side_effects=EFFECT),
    )(*[_in_hbm(x) for x in xs], *[_in_hbm(l) for l in lands], *deps)
    send, recv_ici, recv_sib = outs[:3]
    return dict(send=send, recv_ici=recv_ici, recv_sib=recv_sib, xs=list(outs[3:3 + n]), lands=list(outs[3 + n:3 + 2 * n]), token=outs[-1])


def _gather_forward(st, after, name):
    n = len(st["lands"])

    def body(*refs):
        land_refs, recv_ici = refs[:n], refs[n]
        send2, recv2, token = refs[n + 2], refs[n + 3], refs[2 * n + 4]
        x, y, c, chips = _place()
        for a in range(n):
            for j, chip in enumerate(chips[1:]):
                blk = land_refs[a].at[_slot_of(*chip, c)]
                pltpu.make_async_remote_copy(src_ref=blk, dst_ref=blk, send_sem=send2.at[3 * a + j], recv_sem=recv_ici.at[3 * a + j],
                                             device_id=(*chip, c), device_id_type=MESH).wait_recv()
                pltpu.make_async_remote_copy(src_ref=blk, dst_ref=blk, send_sem=send2.at[3 * a + j], recv_sem=recv2.at[3 * a + j],
                                             device_id=(x, y, 1 - c), device_id_type=MESH).start()
        token[...] = jnp.zeros_like(token)

    outs = pl.pallas_call(
        body, name=name,
        out_shape=(pltpu.SemaphoreType.DMA((3 * n,)), pltpu.SemaphoreType.DMA((3 * n,)), *[pltpu.HBM(l.shape, l.dtype) for l in st["lands"]], TOKEN),
        in_specs=[HBM] * n + [SEM, pl.BlockSpec(memory_space=pl.ANY)],
        out_specs=(SEM, SEM, *[HBM] * n, pl.BlockSpec(memory_space=pltpu.VMEM)),
        input_output_aliases={a: 2 + a for a in range(n)},
        compiler_params=pltpu.CompilerParams(has_side_effects=EFFECT),
    )(*st["lands"], st["recv_ici"], after)
    return dict(st, send2=outs[0], recv2=outs[1], lands=list(outs[2:2 + n]), token=outs[-1])


def _gather_finish(st, after, name):
    n = len(st["lands"])

    def body(*refs):
        x_refs, land_refs = refs[:n], refs[n:2 * n]
        send, recv_sib, send2, recv2 = refs[2 * n:2 * n + 4]
        x, y, c, chips = _place()
        for a in range(n):
            mine = land_refs[a].at[_slot_of(x, y, c)]
            theirs = land_refs[a].at[_slot_of(x, y, 1 - c)]
            for k in range(4):
                pltpu.make_async_remote_copy(src_ref=x_refs[a], dst_ref=mine, send_sem=send.at[4 * a + k], recv_sem=recv_sib.at[a],
                                             device_id=(x, y, 1 - c), device_id_type=MESH).wait_send()
            pltpu.make_async_remote_copy(src_ref=x_refs[a], dst_ref=theirs, send_sem=send.at[4 * a], recv_sem=recv_sib.at[a],
                                         device_id=(x, y, 1 - c), device_id_type=MESH).wait_recv()
            for j, chip in enumerate(chips[1:]):
                sent = land_refs[a].at[_slot_of(*chip, c)]
                got = land_refs[a].at[_slot_of(*chip, 1 - c)]
                pltpu.make_async_remote_copy(src_ref=sent, dst_ref=sent, send_sem=send2.at[3 * a + j], recv_sem=recv2.at[3 * a + j],
                                             device_id=(x, y, 1 - c), device_id_type=MESH).wait_send()
                pltpu.make_async_remote_copy(src_ref=got, dst_ref=got, send_sem=send2.at[3 * a + j], recv_sem=recv2.at[3 * a + j],
                                             device_id=(x, y, 1 - c), device_id_type=MESH).wait_recv()

    outs = pl.pallas_call(
        body, name=name,
        out_shape=tuple(pltpu.HBM(v.shape, v.dtype) for v in st["xs"] + st["lands"]),
        in_specs=[HBM] * (2 * n) + [SEM] * 4 + [pl.BlockSpec(memory_space=pl.ANY)], out_specs=tuple([HBM] * (2 * n)),
        input_output_aliases={a: a for a in range(2 * n)},
        compiler_params=pltpu.CompilerParams(has_side_effects=EFFECT),
    )(*st["xs"], *st["lands"], st["send"], st["recv_sib"], st["send2"], st["recv2"], after)
    return list(outs[n:])


def _pair_start(gs, name):
    n = len(gs)
    lands = [lax.empty((4,) + g.shape[1:], g.dtype) for g in gs]

    def body(*refs):
        g_refs, a_refs = refs[:n], refs[n:2 * n]
        send, recv = refs[2 * n], refs[2 * n + 1]
        token = refs[4 * n + 2]
        x, y, c, chips = _place()
        for a in range(n):
            for j, (px, py) in enumerate(chips):
                pltpu.make_async_remote_copy(src_ref=g_refs[a].at[_slot_of(px, py, 1 - c)], dst_ref=a_refs[a].at[j],
                                             send_sem=send.at[4 * a + j], recv_sem=recv.at[4 * a + j],
                                             device_id=(x, y, 1 - c), device_id_type=MESH).start()
        token[...] = jnp.zeros_like(token)

    outs = pl.pallas_call(
        body, name=name,
        out_shape=(pltpu.SemaphoreType.DMA((4 * n,)), pltpu.SemaphoreType.DMA((4 * n,)),
                   *[pltpu.HBM(g.shape, g.dtype) for g in gs], *[pltpu.HBM(l.shape, l.dtype) for l in lands], TOKEN),
        in_specs=[HBM] * (2 * n), out_specs=(SEM, SEM, *[HBM] * (2 * n), pl.BlockSpec(memory_space=pltpu.VMEM)),
        input_output_aliases={a: 2 + a for a in range(2 * n)},
        compiler_params=pltpu.CompilerParams(has_side_effects=EFFECT),
    )(*[_in_hbm(g) for g in gs], *[_in_hbm(l) for l in lands])
    return dict(send=outs[0], recv=outs[1], gs=list(outs[2:2 + n]), lands=list(outs[2 + n:2 + 2 * n]), token=outs[-1])


def _pair_finish(st, after, name):
    n = len(st["gs"])

    def body(*refs):
        g_refs, a_refs = refs[:n], refs[n:2 * n]
        send, recv = refs[2 * n], refs[2 * n + 1]
        x, y, c, chips = _place()
        for a in range(n):
            for j, (px, py) in enumerate(chips):
                cp = pltpu.make_async_remote_copy(src_ref=g_refs[a].at[_slot_of(px, py, 1 - c)], dst_ref=a_refs[a].at[j],
                                                  send_sem=send.at[4 * a + j], recv_sem=recv.at[4 * a + j],
                                                  device_id=(x, y, 1 - c), device_id_type=MESH)
                cp.wait_send()
                cp.wait_recv()

    outs = pl.pallas_call(
        body, name=name,
        out_shape=tuple(pltpu.HBM(v.shape, v.dtype) for v in st["gs"] + st["lands"]),
        in_specs=[HBM] * (2 * n) + [SEM, SEM, pl.BlockSpec(memory_space=pl.ANY)], out_specs=tuple([HBM] * (2 * n)),
        input_output_aliases={a: a for a in range(2 * n)},
        compiler_params=pltpu.CompilerParams(has_side_effects=EFFECT),
    )(*st["gs"], *st["lands"], st["send"], st["recv"], after)
    return list(outs[:n]), list(outs[n:])


def _chip_start(ps, name):
    n = len(ps)
    lands = [lax.empty(p.shape, p.dtype) for p in ps]

    def body(*refs):
        p_refs, b_refs = refs[:n], refs[n:2 * n]
        send, recv = refs[2 * n], refs[2 * n + 1]
        token = refs[4 * n + 2]
        x, y, c, chips = _place()
        for a in range(n):
            for j, chip in enumerate(chips[1:]):
                pltpu.make_async_remote_copy(src_ref=p_refs[a].at[j], dst_ref=b_refs[a].at[j], send_sem=send.at[3 * a + j], recv_sem=recv.at[3 * a + j],
                                             device_id=(*chip, c), device_id_type=MESH).start()
        token[...] = jnp.zeros_like(token)

    outs = pl.pallas_call(
        body, name=name,
        out_shape=(pltpu.SemaphoreType.DMA((3 * n,)), pltpu.SemaphoreType.DMA((3 * n,)),
                   *[pltpu.HBM(p.shape, p.dtype) for p in ps], *[pltpu.HBM(p.shape, p.dtype) for p in ps], TOKEN),
        in_specs=[HBM] * (2 * n), out_specs=(SEM, SEM, *[HBM] * (2 * n), pl.BlockSpec(memory_space=pltpu.VMEM)),
        input_output_aliases={a: 2 + a for a in range(2 * n)},
        compiler_params=pltpu.CompilerParams(has_side_effects=EFFECT),
    )(*[_in_hbm(p) for p in ps], *[_in_hbm(l) for l in lands])
    return dict(send=outs[0], recv=outs[1], ps=list(outs[2:2 + n]), lands=list(outs[2 + n:2 + 2 * n]), token=outs[-1])


def _chip_finish(st, after, name):
    n = len(st["ps"])

    def body(*refs):
        p_refs, b_refs = refs[:n], refs[n:2 * n]
        send, recv = refs[2 * n], refs[2 * n + 1]
        x, y, c, chips = _place()
        for a in range(n):
            for j, chip in enumerate(chips[1:]):
                cp = pltpu.make_async_remote_copy(src_ref=p_refs[a].at[j], dst_ref=b_refs[a].at[j], send_sem=send.at[3 * a + j], recv_sem=recv.at[3 * a + j],
                                                  device_id=(*chip, c), device_id_type=MESH)
                cp.wait_send()
                cp.wait_recv()

    outs = pl.pallas_call(
        body, name=name,
        out_shape=tuple(pltpu.HBM(v.shape, v.dtype) for v in st["ps"] + st["lands"]),
        in_specs=[HBM] * (2 * n) + [SEM, SEM, pl.BlockSpec(memory_space=pl.ANY)], out_specs=tuple([HBM] * (2 * n)),
        input_output_aliases={a: a for a in range(2 * n)},
        compiler_params=pltpu.CompilerParams(has_side_effects=EFFECT),
    )(*st["ps"], *st["lands"], st["send"], st["recv"], after)
    return list(outs[n:])


def _rows_tile(r):
    for t in (512, 256, 128, 64, 32, 16):
        if r % t == 0:
            return t
    return r


PAIR_SUM_STEPS = 4


def _pair_sum(gs, a_s, slots, name):
    n = len(gs)
    trs = [g.shape[1] // PAIR_SUM_STEPS for g in gs]

    def body(slots_ref, *refs):
        for g_ref, a_ref, p_ref in zip(refs[:n], refs[n:2 * n], refs[2 * n:]):
            p_ref[...] = (g_ref[...].astype(F32) + a_ref[...].astype(F32)).astype(BF)

    def spec(g, tr, index):
        return pl.BlockSpec((None, tr, g.shape[2]), index)

    return pl.pallas_call(
        body, name=name,
        grid_spec=pltpu.PrefetchScalarGridSpec(
            num_scalar_prefetch=1, grid=(3, PAIR_SUM_STEPS),
            in_specs=[spec(g, tr, lambda j, i, s: (s[j + 1], i, 0)) for g, tr in zip(gs, trs)]
            + [spec(g, tr, lambda j, i, s: (j + 1, i, 0)) for g, tr in zip(gs, trs)],
            out_specs=[spec(g, tr, lambda j, i, s: (j, i, 0)) for g, tr in zip(gs, trs)]),
        out_shape=[_sds((3,) + g.shape[1:], BF) for g in gs], compiler_params=_cp("parallel", "parallel"),
    )(slots, *gs, *a_s)


def _adamw_math(w, g, m, v):
    m = ADAM_B1 * m + (1.0 - ADAM_B1) * g
    v = ADAM_B2 * v + (1.0 - ADAM_B2) * (g * g)
    m_hat = m / (1.0 - ADAM_B1 ** ADAM_STEP)
    v_hat = v / (1.0 - ADAM_B2 ** ADAM_STEP)
    delta = -ADAM_LR * (m_hat / (jnp.sqrt(v_hat) + ADAM_EPS) + ADAM_WD * w)
    return delta, m, v


def _adamw_sharded(w, m, v, parts, slots, name, transposed=False):
    nl = w.shape[0]
    r, c = parts[0][0].shape[1:]
    tr = _rows_tile(r)
    if c * tr * 4 > (1 << 21) and not transposed:
        tr = max(8, tr // 2)

    def body(slots_ref, w_ref, m_ref, v_ref, *rest):
        part_refs, (g_out, d_out, m_out, v_out) = rest[:5 * nl], rest[5 * nl:]
        layer = pl.program_id(0)
        g = None
        for l in range(nl):
            s = part_refs[5 * l][...].astype(F32)
            for ref in part_refs[5 * l + 1:5 * l + 5]:
                s = s + ref[...].astype(F32)
            g = s if g is None else jnp.where(layer == l, s, g)
        if transposed:
            g = g.T
        delta, mn, vn = _adamw_math(w_ref[...], g, m_ref[...], v_ref[...])
        g_out[...] = g
        d_out[...] = delta
        m_out[...] = mn
        v_out[...] = vn

    def own(l):
        return lambda L, i, s: (s[0], jnp.where(L == l, i, 0), 0)

    def fixed(l, k):
        return lambda L, i, s: (k, jnp.where(L == l, i, 0), 0)

    if transposed:
        wspec = pl.BlockSpec((None, c, tr), lambda L, i, s: (L, 0, i))
    else:
        wspec = pl.BlockSpec((None, tr, c), lambda L, i, s: (L, i, 0))
    in_specs = [wspec, wspec, wspec]
    args = [w, m, v]
    for l, (g, a, b) in enumerate(parts):
        in_specs += [pl.BlockSpec((None, tr, c), own(l)), pl.BlockSpec((None, tr, c), fixed(l, 0)),
                     pl.BlockSpec((None, tr, c), fixed(l, 0)), pl.BlockSpec((None, tr, c), fixed(l, 1)),
                     pl.BlockSpec((None, tr, c), fixed(l, 2))]
        args += [g, a, b, b, b]
    return pl.pallas_call(
        body, name=name,
        grid_spec=pltpu.PrefetchScalarGridSpec(
            num_scalar_prefetch=1, grid=(nl, r // tr), in_specs=in_specs, out_specs=[wspec] * 4),
        out_shape=[_sds(w.shape, F32)] * 4, compiler_params=_cp("arbitrary", "arbitrary"),
    )(slots, *args)


def _adamw_small(w, m, v, gathered, name):
    def body(w_ref, m_ref, v_ref, gg_ref, g_out, d_out, m_out, v_out):
        g = gg_ref[0]
        for k in range(1, NDEV):
            g = g + gg_ref[k]
        delta, mn, vn = _adamw_math(w_ref[...], g, m_ref[...], v_ref[...])
        g_out[...] = g
        d_out[...] = delta
        m_out[...] = mn
        v_out[...] = vn

    return pl.pallas_call(body, name=name, out_shape=[_sds(w.shape, F32)] * 4)(w, m, v, gathered)


def _rmsnorm_fwd(x, g, name, deps=()):
    s, d = x.shape
    tm = 256

    def body(x_ref, g_ref, h_ref, ht_ref):
        xf = x_ref[...]
        y = xf * lax.rsqrt(jnp.mean(xf * xf, axis=-1, keepdims=True) + RMS_EPS)
        h = y * g_ref[...]
        h_ref[...] = h.astype(BF)
        ht_ref[...] = h.T.astype(BF)

    return pl.pallas_call(
        _ignore_deps(body, 2, len(deps)), name=name, grid=(s // tm,),
        in_specs=[pl.BlockSpec((tm, d), lambda i: (i, 0)), pl.BlockSpec((1, d), lambda i: (0, 0))] + [TOKEN_SPEC] * len(deps),
        out_specs=[pl.BlockSpec((tm, d), lambda i: (i, 0)), pl.BlockSpec((d, tm), lambda i: (0, i))],
        out_shape=[_sds((s, d), BF), _sds((d, s), BF)], compiler_params=_cp("parallel"),
    )(x, g, *deps)


def _rmsnorm_bwd(x, g, dh, dres, out_scale, name):
    s, d = x.shape
    tm = 256

    def body(x_ref, g_ref, dh_ref, dres_ref, dx_ref, dxb_ref, dxbt_ref, dg_ref):
        xf = x_ref[...]
        r = lax.rsqrt(jnp.mean(xf * xf, axis=-1, keepdims=True) + RMS_EPS)
        xhat = xf * r
        dhv = dh_ref[...]
        dxhat = dhv * g_ref[...]
        dx = dres_ref[...] + r * (dxhat - xhat * jnp.mean(dxhat * xhat, axis=-1, keepdims=True))
        dx_ref[...] = dx
        scaled = dx * out_scale
        dxb_ref[...] = scaled.astype(BF)
        dxbt_ref[...] = scaled.T.astype(BF)

        @pl.when(pl.program_id(0) == 0)
        def _():
            dg_ref[...] = jnp.zeros_like(dg_ref)

        dg_ref[...] += jnp.sum(dhv * xhat, axis=0, keepdims=True)

    row = pl.BlockSpec((tm, d), lambda i: (i, 0))
    vec = pl.BlockSpec((1, d), lambda i: (0, 0))
    return pl.pallas_call(
        body, name=name, grid=(s // tm,),
        in_specs=[row, vec, row, row],
        out_specs=[row, row, pl.BlockSpec((d, tm), lambda i: (0, i)), vec],
        out_shape=[_sds((s, d), F32), _sds((s, d), BF), _sds((d, s), BF), _sds((1, d), F32)],
        compiler_params=_cp("arbitrary"),
    )(x, g, dh, dres)


def _loss_head(x, g, target, name):
    s, d = x.shape
    tm = 256

    def body(x_ref, g_ref, t_ref, dx_ref, dxb_ref, dxbt_ref, dg_ref, loss_ref):
        xf = x_ref[...]
        r = lax.rsqrt(jnp.mean(xf * xf, axis=-1, keepdims=True) + RMS_EPS)
        xhat = xf * r
        err = xhat * g_ref[...] - t_ref[...]
        dy = err * (1.0 / d)
        dxhat = dy * g_ref[...]
        dx = r * (dxhat - xhat * jnp.mean(dxhat * xhat, axis=-1, keepdims=True))
        dx_ref[...] = dx
        half = dx * 0.5
        dxb_ref[...] = half.astype(BF)
        dxbt_ref[...] = half.T.astype(BF)

        @pl.when(pl.program_id(0) == 0)
        def _():
            dg_ref[...] = jnp.zeros_like(dg_ref)
            loss_ref[...] = jnp.zeros_like(loss_ref)

        dg_ref[...] += jnp.sum(dy * xhat, axis=0, keepdims=True)
        part = 0.5 * jnp.sum(jnp.mean(err * err, axis=-1, keepdims=True), axis=0, keepdims=True)
        lane = lax.broadcasted_iota(jnp.int32, (1, 128), 1)
        loss_ref[...] += jnp.where(lane == 0, part, 0.0)

    row = pl.BlockSpec((tm, d), lambda i: (i, 0))
    vec = pl.BlockSpec((1, d), lambda i: (0, 0))
    return pl.pallas_call(
        body, name=name, grid=(s // tm,),
        in_specs=[row, vec, row],
        out_specs=[row, row, pl.BlockSpec((d, tm), lambda i: (0, i)), vec, pl.BlockSpec((1, 128), lambda i: (0, 0))],
        out_shape=[_sds((s, d), F32), _sds((s, d), BF), _sds((d, s), BF), _sds((1, d), F32), _sds((1, 128), F32)],
        compiler_params=_cp("arbitrary"),
    )(x, g, target)


def _act_spec(tm, n, natural, order):
    if natural:
        return pl.BlockSpec((tm, n), (lambda s, i: (i, s)) if order == "si" else (lambda i, s: (i, s)))
    return pl.BlockSpec((None, tm, n), (lambda s, i: (s, i, 0)) if order == "si" else (lambda i, s: (s, i, 0)))


def _act_shape(s, n, natural, dtype):
    return _sds((s, NDEV * n), dtype) if natural else _sds((NDEV, s, n), dtype)


def _ffn_up(h, wg, wu, name):
    s, d = h.shape
    n = wg.shape[2]
    tm = 1024

    def body(h_ref, wg_ref, wu_ref, g_ref, u_ref, a_ref, at_ref):
        hb = h_ref[...]
        g = _dot(hb, wg_ref[...])
        u = _dot(hb, wu_ref[...])
        g_ref[...] = g.astype(BF)
        u_ref[...] = u.astype(BF)
        act = g * jax.nn.sigmoid(g) * u
        a_ref[...] = act.astype(BF)
        at_ref[...] = act.T.astype(BF)

    wsp = pl.BlockSpec((None, d, n), lambda s_, i: (s_, 0, 0))
    blk = _act_spec(tm, n, False, "si")
    return pl.pallas_call(
        body, name=name, grid=(NDEV, s // tm),
        in_specs=[pl.BlockSpec((tm, d), lambda s_, i: (i, 0)), wsp, wsp],
        out_specs=[blk] * 3 + [pl.BlockSpec((None, n, tm), lambda s_, i: (s_, 0, i))],
        out_shape=[_act_shape(s, n, False, BF)] * 3 + [_sds((NDEV, n, s), BF)],
        compiler_params=_cp("parallel", "parallel"),
    )(h, wg, wu)


def _ffn_down(act, wd, x, name):
    _, s, n = act.shape
    d = wd.shape[2]
    tm = 512

    def body(a_ref, w_ref, x_ref, o_ref, acc):
        k = pl.program_id(1)

        @pl.when(k == 0)
        def _():
            acc[...] = jnp.zeros_like(acc)

        acc[...] += _dot(a_ref[...], w_ref[...])

        @pl.when(k == NDEV - 1)
        def _():
            o_ref[...] = x_ref[...] + 0.5 * acc[...]

    row = pl.BlockSpec((tm, d), lambda i, k: (i, 0))
    return pl.pallas_call(
        body, name=name, grid=(s // tm, NDEV),
        in_specs=[_act_spec(tm, n, False, "is"), pl.BlockSpec((None, n, d), lambda i, k: (k, 0, 0)), row],
        out_specs=row, out_shape=_sds((s, d), F32),
        scratch_shapes=[pltpu.VMEM((tm, d), F32)], compiler_params=_cp("parallel", "arbitrary"),
    )(act, wd, x)


def _ffn_bwd_act(dyb, wd, g, u, name, deps=()):
    s, d = dyb.shape
    n = wd.shape[1]
    tm = 1024

    def body(dy_ref, w_ref, g_ref, u_ref, dg_ref, du_ref):
        dact = _dot_nt(dy_ref[...], w_ref[...])
        gv = g_ref[...].astype(F32)
        uv = u_ref[...].astype(F32)
        sig = jax.nn.sigmoid(gv)
        dg_ref[...] = (dact * uv * (sig * (1.0 + gv * (1.0 - sig)))).astype(BF)
        du_ref[...] = (dact * (gv * sig)).astype(BF)

    blk = _act_spec(tm, n, False, "si")
    return pl.pallas_call(
        _ignore_deps(body, 4, len(deps)), name=name, grid=(NDEV, s // tm),
        in_specs=[pl.BlockSpec((tm, d), lambda s_, i: (i, 0)), pl.BlockSpec((None, n, d), lambda s_, i: (s_, 0, 0)), blk, blk]
        + [TOKEN_SPEC] * len(deps),
        out_specs=[blk, blk], out_shape=[_act_shape(s, n, False, BF)] * 2,
        compiler_params=_cp("parallel", "parallel"),
    )(dyb, wd, g, u, *deps)


def _grad_rows(act_t, dyb, name, deps=()):
    _, n, s = act_t.shape
    d = dyb.shape[1]
    tn = 2048

    def body(a_ref, dy_ref, o_ref):
        o_ref[...] = _dot(a_ref[...], dy_ref[...]).astype(BF)

    return pl.pallas_call(
        _ignore_deps(body, 2, len(deps)), name=name, grid=(NDEV, d // tn),
        in_specs=[pl.BlockSpec((None, n, s), lambda k, j: (k, 0, 0)), pl.BlockSpec((s, tn), lambda k, j: (0, j))]
        + [TOKEN_SPEC] * len(deps),
        out_specs=pl.BlockSpec((None, n, tn), lambda k, j: (k, 0, j)), out_shape=_sds((NDEV, n, d), BF),
        compiler_params=_cp("parallel", "parallel"),
    )(act_t, dyb, *deps)


def _grad_cols(ht, dxs, naturals, name, deps=()):
    d, s = ht.shape
    k = len(dxs)
    ns = [dx.shape[1] // NDEV if nat else dx.shape[2] for dx, nat in zip(dxs, naturals)]
    td = 1024

    def body(*refs):
        ht_ref, dx_refs, o_refs = refs[0], refs[1:1 + k], refs[1 + k:]
        hv = ht_ref[...]
        for dx_ref, o_ref in zip(dx_refs, o_refs):
            o_ref[...] = _dot(hv, dx_ref[...]).astype(BF)

    def dx_spec(n, nat):
        if nat:
            return pl.BlockSpec((s, n), lambda s_, j: (0, s_))
        return pl.BlockSpec((None, s, n), lambda s_, j: (s_, 0, 0))

    return pl.pallas_call(
        _ignore_deps(body, 1 + k, len(deps)), name=name, grid=(NDEV, d // td),
        in_specs=[pl.BlockSpec((td, s), lambda s_, j: (j, 0))] + [dx_spec(n, nat) for n, nat in zip(ns, naturals)]
        + [TOKEN_SPEC] * len(deps),
        out_specs=[pl.BlockSpec((None, td, n), lambda s_, j: (s_, j, 0)) for n in ns],
        out_shape=[_sds((NDEV, d, n), BF) for n in ns], compiler_params=_cp("parallel", "parallel"),
    )(ht, *dxs, *deps)


def _dh_cols(dxs, ws, naturals, name, deps=()):
    k = len(dxs)
    d = ws[0].shape[1]
    ns = [w.shape[2] for w in ws]
    s = dxs[0].shape[0] if naturals[0] else dxs[0].shape[1]
    tm = 512

    def body(*refs):
        dx_refs, w_refs, o_ref, acc = refs[:k], refs[k:2 * k], refs[2 * k], refs[2 * k + 1]
        j = pl.program_id(1)

        @pl.when(j == 0)
        def _():
            acc[...] = jnp.zeros_like(acc)

        t = _dot_nt(dx_refs[0][...], w_refs[0][...])
        for dx_ref, w_ref in zip(dx_refs[1:], w_refs[1:]):
            t = t + _dot_nt(dx_ref[...], w_ref[...])
        acc[...] += t

        @pl.when(j == NDEV - 1)
        def _():
            o_ref[...] = acc[...]

    return pl.pallas_call(
        _ignore_deps(body, 2 * k, len(deps)), name=name, grid=(s // tm, NDEV),
        in_specs=[_act_spec(tm, n, nat, "is") for n, nat in zip(ns, naturals)]
        + [pl.BlockSpec((None, d, n), lambda i, j: (j, 0, 0)) for n in ns] + [TOKEN_SPEC] * len(deps),
        out_specs=pl.BlockSpec((tm, d), lambda i, j: (i, 0)), out_shape=_sds((s, d), F32),
        scratch_shapes=[pltpu.VMEM((tm, d), F32)], compiler_params=_cp("parallel", "arbitrary"),
    )(*dxs, *ws, *deps)


def _mm_nn(a, b, tn, out_dtype, name, res=None, tm=1024, deps=()):
    m, k = a.shape
    nn = b.shape[1]

    def body(*refs):
        if res is None:
            a_ref, b_ref, o_ref = refs
            o_ref[...] = _dot(a_ref[...], b_ref[...]).astype(out_dtype)
        else:
            a_ref, b_ref, r_ref, o_ref = refs
            o_ref[...] = (r_ref[...] + _dot(a_ref[...], b_ref[...])).astype(out_dtype)

    osp = pl.BlockSpec((tm, tn), lambda j, i: (i, j))
    in_specs = [pl.BlockSpec((tm, k), lambda j, i: (i, 0)), pl.BlockSpec((k, tn), lambda j, i: (0, j))]
    args = [a, b]
    if res is not None:
        in_specs.append(osp)
        args.append(res)
    return pl.pallas_call(
        _ignore_deps(body, len(args), len(deps)), name=name, grid=(nn // tn, m // tm),
        in_specs=in_specs + [TOKEN_SPEC] * len(deps), out_specs=osp,
        out_shape=_sds((m, nn), out_dtype), compiler_params=_cp("parallel", "parallel"),
    )(*args, *deps)


def _mm_nt(pairs, name, out_dtype=F32, tm=512, tk=512, deps=()):
    m = pairs[0][0].shape[0]
    kk = pairs[0][1].shape[0]
    p = len(pairs)

    def body(*refs):
        o_ref = refs[2 * p]
        t = _dot_nt(refs[0][...], refs[1][...])
        for q in range(1, p):
            t = t + _dot_nt(refs[2 * q][...], refs[2 * q + 1][...])
        o_ref[...] = t.astype(out_dtype)

    in_specs, args = [], []
    for a, b in pairs:
        in_specs += [pl.BlockSpec((tm, a.shape[1]), lambda j, i: (i, 0)), pl.BlockSpec((tk, b.shape[1]), lambda j, i: (j, 0))]
        args += [a, b]
    return pl.pallas_call(
        _ignore_deps(body, 2 * p, len(deps)), name=name, grid=(kk // tk, m // tm), in_specs=in_specs + [TOKEN_SPEC] * len(deps),
        out_specs=pl.BlockSpec((tm, tk), lambda j, i: (i, j)), out_shape=_sds((m, kk), out_dtype),
        compiler_params=_cp("parallel", "parallel"),
    )(*args, *deps)


def _rope_tables(s, sign):
    half = ROPE_DIMS // 2
    f32 = np.float32
    freqs = f32(ROPE_THETA) ** (-np.arange(half, dtype=f32) / f32(half))
    ang = np.arange(s, dtype=f32)[:, None] * freqs[None, :]
    cos, sin = np.cos(ang).astype(f32), (sign * np.sin(ang)).astype(f32)
    one = np.ones((s, HD - ROPE_DIMS), f32)
    zero = np.zeros((s, HD - ROPE_DIMS), f32)
    zh = np.zeros((s, half), f32)
    c = np.concatenate([cos, cos, one], axis=1)
    sa = np.concatenate([-sin, zh, zero], axis=1)
    sb = np.concatenate([zh, sin, zero], axis=1)
    return jnp.asarray(c), jnp.asarray(sa), jnp.asarray(sb)


def _rope(xv, c, sa, sb):
    return xv * c + pltpu.roll(xv, HD - ROPE_DIMS // 2, 1) * sa + pltpu.roll(xv, ROPE_DIMS // 2, 1) * sb


def _qkv_rope(h, w, tables, name):
    s, d = h.shape
    n = w.shape[2]
    per = n // HD
    tm = 1024

    def body(h_ref, w_ref, c_ref, sa_ref, sb_ref, o_ref):
        shard = pl.program_id(0)
        y = _dot(h_ref[...], w_ref[...])
        c, sa, sb = c_ref[...], sa_ref[...], sb_ref[...]
        for j in range(per):
            blk = y[:, j * HD:(j + 1) * HD]
            rot = _rope(blk, c, sa, sb)
            is_qk = shard * per + j < 2 * N_HEADS
            o_ref[:, j * HD:(j + 1) * HD] = jnp.where(is_qk, rot, blk).astype(BF)

    tab = pl.BlockSpec((tm, HD), lambda s_, i: (i, 0))
    return pl.pallas_call(
        body, name=name, grid=(NDEV, s // tm),
        in_specs=[pl.BlockSpec((tm, d), lambda s_, i: (i, 0)), pl.BlockSpec((None, d, n), lambda s_, i: (s_, 0, 0)), tab, tab, tab],
        out_specs=pl.BlockSpec((tm, n), lambda s_, i: (i, s_)), out_shape=_sds((s, NDEV * n), BF),
        compiler_params=_cp("parallel", "parallel"),
    )(h, w, *tables)


def _iota2():
    return (lax.broadcasted_iota(jnp.int32, (QB, QB), 0), lax.broadcasted_iota(jnp.int32, (QB, QB), 1))


def _softplus(z):
    return jnp.maximum(z, 0.0) + jnp.log(1.0 + jnp.exp(-jnp.abs(z)))


def _tri_dot(xv, tri, left=False):
    hi = xv.astype(BF)
    r1 = xv - hi.astype(F32)
    mid = r1.astype(BF)
    lo = (r1 - mid.astype(F32)).astype(BF)
    if left:
        return _dot(tri, hi) + _dot(tri, mid) + _dot(tri, lo)
    return _dot(hi, tri) + _dot(mid, tri) + _dot(lo, tri)


def _col(ref_or_val):
    return ref_or_val[:, 0:1]


KT = 4 * QB
QQ = 4 * QB


def _iota_tile():
    return (lax.broadcasted_iota(jnp.int32, (QQ, KT), 0), lax.broadcasted_iota(jnp.int32, (QQ, KT), 1))


def _scan_matrix(keep):
    tri = keep(*_iota2()).astype(BF)
    return jnp.concatenate([tri, tri], axis=0)


def _scan_dot(xv, tri2):
    hi = xv.astype(BF)
    lo = (xv - hi.astype(F32)).astype(BF)
    return _dot(jnp.concatenate([hi, lo], axis=1), tri2)


def _blocks(xv):
    return [xv[:, b * QB:(b + 1) * QB] for b in range(KT // QB)]


def _sb_fwd(qkv, name):
    s = qkv.shape[0]
    nb = s // QB

    def body(q_ref, k_ref, v_ref, o_ref, ot_ref, t_ref):
        i = pl.program_id(1)
        q = q_ref[...]
        row, col = _iota_tile()
        later_keys = _scan_matrix(lambda j, s_: j > s_)
        last = (i * QQ + QQ - 1) // KT

        def step(tt, carry):
            acc, later = carry
            t = last - tt
            off = pl.multiple_of(t * KT, KT)
            k = k_ref[pl.ds(off, KT), :]
            v = v_ref[pl.ds(off, KT), :]
            z = _dot_nt(q, k) * SCALE
            strict = row + (i * QQ - t * KT) > col
            sp = _softplus(z)
            lnb = jnp.where(strict, -sp, 0.0)
            afters = []
            for xb in reversed(_blocks(lnb)):
                afters.append(later + _scan_dot(xb, later_keys))
                later = later + jnp.sum(xb, axis=1, keepdims=True)
            after = jnp.concatenate(afters[::-1], axis=1)
            w = jnp.where(strict, jnp.exp((z - sp) + after), 0.0)
            return acc + _dot(w.astype(BF), v), later

        acc, total = lax.fori_loop(0, last + 1, step, (jnp.zeros((QQ, HD), F32), jnp.zeros((QQ, 1), F32)))
        o_ref[...] = acc.astype(BF)
        ot_ref[...] = acc.T.astype(BF)
        t_ref[...] = jnp.broadcast_to(total, (QQ, HD))

    blk = pl.BlockSpec((QQ, HD), lambda h, i: (i, h))
    return pl.pallas_call(
        body, name=name, grid=(N_SB, s // QQ),
        in_specs=[blk, pl.BlockSpec((s, HD), lambda h, i: (0, N_HEADS + h)), pl.BlockSpec((s, HD), lambda h, i: (0, 2 * N_HEADS + h))],
        out_specs=[blk, pl.BlockSpec((HD, QQ), lambda h, i: (h, i)), blk],
        out_shape=[_sds((s, N_SB * HD), BF), _sds((N_SB * HD, s), BF), _sds((s, N_SB * HD), F32)],
        compiler_params=_cp("parallel", "parallel"),
    )(qkv, qkv, qkv)


def _sb_bwd(qkv, do, total, name):
    s = qkv.shape[0]
    nb = s // QB

    def body(q_ref, k_ref, v_ref, do_ref, t_ref, dq_ref, dk_ref, dv_ref, dk_acc, dv_acc):
        i = pl.program_id(1)

        @pl.when(i == 0)
        def _():
            dk_acc[...] = jnp.zeros_like(dk_acc)
            dv_acc[...] = jnp.zeros_like(dv_acc)

        q = q_ref[...]
        dov = do_ref[...]
        tot = _col(t_ref[...])
        row, col = _iota_tile()
        keys_upto = _scan_matrix(lambda j, s_: j <= s_)
        keys_before = _scan_matrix(lambda j, s_: j < s_)

        def step(t, carry):
            dq, lnb_before, dl_before = carry
            off = pl.multiple_of(t * KT, KT)
            k = k_ref[pl.ds(off, KT), :]
            v = v_ref[pl.ds(off, KT), :]
            z = _dot_nt(q, k) * SCALE
            strict = row + (i * QQ - t * KT) > col
            sp = _softplus(z)
            lnb = jnp.where(strict, -sp, 0.0)
            afters = []
            for xb in _blocks(lnb):
                afters.append(tot - (lnb_before + _scan_dot(xb, keys_upto)))
                lnb_before = lnb_before + jnp.sum(xb, axis=1, keepdims=True)
            a = jnp.where(strict, jnp.exp((z - sp) + jnp.concatenate(afters, axis=1)), 0.0)
            dl = a * _dot_nt(dov, v)
            befores = []
            for xb in _blocks(dl):
                befores.append(dl_before + _scan_dot(xb, keys_before))
                dl_before = dl_before + jnp.sum(xb, axis=1, keepdims=True)
            sig = jnp.exp(z - sp)
            dz = jnp.where(strict, dl * (1.0 - sig) - sig * jnp.concatenate(befores, axis=1), 0.0) * SCALE
            dq = dq + _dot(dz.astype(BF), k)
            dk_acc[pl.ds(off, KT), :] += _dot(dz.T.astype(BF), q)
            dv_acc[pl.ds(off, KT), :] += _dot(a.T.astype(BF), dov)
            return dq, lnb_before, dl_before

        zero = jnp.zeros((QQ, 1), F32)
        dq, _, _ = lax.fori_loop(0, (i * QQ + QQ - 1) // KT + 1, step, (jnp.zeros((QQ, HD), F32), zero, zero))
        dq_ref[...] = dq.astype(BF)

        @pl.when(i == s // QQ - 1)
        def _():
            dk_ref[...] = dk_acc[...].astype(BF)
            dv_ref[...] = dv_acc[...].astype(BF)

    blk = pl.BlockSpec((QQ, HD), lambda h, i: (i, h))
    full = pl.BlockSpec((s, HD), lambda h, i: (0, h))
    return pl.pallas_call(
        body, name=name, grid=(N_SB, s // QQ),
        in_specs=[blk, pl.BlockSpec((s, HD), lambda h, i: (0, N_HEADS + h)), pl.BlockSpec((s, HD), lambda h, i: (0, 2 * N_HEADS + h)), blk, blk],
        out_specs=[blk, full, full], out_shape=[_sds((s, N_SB * HD), BF)] * 3,
        scratch_shapes=[pltpu.VMEM((s, HD), F32), pltpu.VMEM((s, HD), F32)],
        compiler_params=_cp("parallel", "arbitrary"),
    )(qkv, qkv, qkv, do, total)


def _fgate_fwd(f, b, name):
    s = f.shape[0]
    nb = s // QB
    nfox = N_HEADS - N_SB

    def body(f_ref, b_ref, cb_ref, ct_ref):
        row, col = _iota2()
        upto = (row >= col).astype(BF)
        carry = jnp.zeros((1, HD), F32)
        for blk in range(nb):
            xv = f_ref[blk * QB:(blk + 1) * QB, :] + b_ref[...]
            logf = -_softplus(-xv)
            cum = _tri_dot(logf, upto, left=True) + carry
            carry = cum[QB - 1:QB, :]
            ct_ref[blk] = cum.T
            for h in range(nfox):
                cb_ref[blk * QB:(blk + 1) * QB, h * HD:(h + 1) * HD] = jnp.broadcast_to(cum[:, h:h + 1], (QB, HD))

    return pl.pallas_call(
        body, name=name, out_shape=[_sds((s, nfox * HD), F32), _sds((nb, HD, HD), F32)], compiler_params=_cp(),
    )(f, b)


def _fgate_bwd(dcq, dck, f, b, name):
    s = f.shape[0]
    nb = s // QB
    nfox = N_HEADS - N_SB

    def body(dcq_ref, dck_ref, f_ref, b_ref, df_ref, db_ref):
        row, col = _iota2()
        from_tri = (row <= col).astype(BF)
        lane = col
        carry = jnp.zeros((1, HD), F32)
        db = jnp.zeros((1, HD), F32)
        for blk in reversed(range(nb)):
            dcum = jnp.zeros((QB, HD), F32)
            for h in range(nfox):
                here = (slice(blk * QB, (blk + 1) * QB), slice(h * HD, (h + 1) * HD))
                dcum = jnp.where(lane == h, dcq_ref[here] - dck_ref[here], dcum)
            dlogf = _tri_dot(dcum, from_tri, left=True) + carry
            carry = dlogf[0:1, :]
            xv = f_ref[blk * QB:(blk + 1) * QB, :] + b_ref[...]
            sp = _softplus(xv)
            df = jnp.where(lane < nfox, dlogf * jnp.exp(-sp), 0.0)
            df_ref[blk * QB:(blk + 1) * QB, :] = df.astype(BF)
            db = db + jnp.sum(df, axis=0, keepdims=True)
        db_ref[...] = db

    return pl.pallas_call(
        body, name=name, out_shape=[_sds((s, HD), BF), _sds((1, HD), F32)], compiler_params=_cp(),
    )(dcq, dck, f, b)


def _fox_head_row(ct_ref, j, h):
    tile = ct_ref[j]
    sub = lax.broadcasted_iota(jnp.int32, tile.shape, 0)
    return jnp.sum(jnp.where(sub == h, tile, 0.0), axis=0, keepdims=True)


def _fox_tile_row(ct_ref, t, h):
    nsub = KT // QB
    return jnp.concatenate([_fox_head_row(ct_ref, t * nsub + b, h) for b in range(nsub)], axis=1)


def _fox_fwd(qkv, cum_b, cum_t, name):
    s = qkv.shape[0]
    nb = s // QB
    nfox = N_HEADS - N_SB

    def body(q_ref, k_ref, v_ref, cq_ref, ct_ref, o_ref, ot_ref, lse_ref):
        h, i = pl.program_id(0), pl.program_id(1)
        q = q_ref[...]
        cq = _col(cq_ref[...])
        row, col = _iota_tile()

        def step(t, carry):
            acc, m, l = carry
            off = pl.multiple_of(t * KT, KT)
            k = k_ref[pl.ds(off, KT), :]
            v = v_ref[pl.ds(off, KT), :]
            z = _dot_nt(q, k) * SCALE + cq - _fox_tile_row(ct_ref, t, h)
            z = jnp.where(row + (i * QQ - t * KT) >= col, z, NEG_INF)
            m_new = jnp.maximum(m, jnp.max(z, axis=1, keepdims=True))
            alpha = jnp.exp(m - m_new)
            p = jnp.exp(z - m_new)
            l = alpha * l + jnp.sum(p, axis=1, keepdims=True)
            acc = alpha * acc + _dot(p.astype(BF), v)
            return acc, m_new, l

        acc, m, l = lax.fori_loop(0, (i * QQ + QQ - 1) // KT + 1, step,
                                  (jnp.zeros((QQ, HD), F32), jnp.full((QQ, 1), NEG_INF, F32), jnp.zeros((QQ, 1), F32)))
        o = acc / l
        o_ref[...] = o.astype(BF)
        ot_ref[...] = o.T.astype(BF)
        lse_ref[...] = jnp.broadcast_to(m + jnp.log(l), (QQ, HD))

    blk = pl.BlockSpec((QQ, HD), lambda h, i: (i, h))
    return pl.pallas_call(
        body, name=name, grid=(nfox, s // QQ),
        in_specs=[pl.BlockSpec((QQ, HD), lambda h, i: (i, N_SB + h)),
                  pl.BlockSpec((s, HD), lambda h, i: (0, N_HEADS + N_SB + h)),
                  pl.BlockSpec((s, HD), lambda h, i: (0, 2 * N_HEADS + N_SB + h)),
                  blk, pl.BlockSpec((nb, 8, HD), lambda h, i: (0, 0, 0))],
        out_specs=[blk, pl.BlockSpec((HD, QQ), lambda h, i: (h, i)), blk],
        out_shape=[_sds((s, nfox * HD), BF), _sds((nfox * HD, s), BF), _sds((s, nfox * HD), F32)],
        compiler_params=_cp("parallel", "parallel"),
    )(qkv, qkv, qkv, cum_b, cum_t)


def _fox_bwd(qkv, cum_b, cum_t, o, lse, do, name):
    s = qkv.shape[0]
    nb = s // QB
    nfox = N_HEADS - N_SB

    def body(q_ref, k_ref, v_ref, cq_ref, ct_ref, o_ref, lse_ref, do_ref, dq_ref, dk_ref, dv_ref, dcq_ref, dc_ref, dk_acc, dv_acc, dc_acc):
        h, i = pl.program_id(0), pl.program_id(1)

        @pl.when(i == 0)
        def _():
            dk_acc[...] = jnp.zeros_like(dk_acc)
            dv_acc[...] = jnp.zeros_like(dv_acc)
            dc_acc[...] = jnp.zeros_like(dc_acc)

        q = q_ref[...]
        cq = _col(cq_ref[...])
        dov = do_ref[...]
        lse_c = _col(lse_ref[...])
        delta = jnp.sum(dov.astype(F32) * o_ref[...].astype(F32), axis=1, keepdims=True)
        row, col = _iota_tile()
        ones = jnp.ones((QQ, HD), BF)

        def step(t, carry):
            dq, over_keys = carry
            off = pl.multiple_of(t * KT, KT)
            k = k_ref[pl.ds(off, KT), :]
            v = v_ref[pl.ds(off, KT), :]
            z = _dot_nt(q, k) * SCALE + cq - _fox_tile_row(ct_ref, t, h)
            p = jnp.where(row + (i * QQ - t * KT) >= col, jnp.exp(z - lse_c), 0.0)
            dz = p * (_dot_nt(dov, v) - delta)
            dzt = dz.T
            dq = dq + _dot((dz * SCALE).astype(BF), k)
            dk_acc[pl.ds(off, KT), :] += _dot((dzt * SCALE).astype(BF), q)
            dv_acc[pl.ds(off, KT), :] += _dot(p.T.astype(BF), dov)
            dc_acc[pl.ds(off, KT), :] += _tri_dot(dzt, ones)
            return dq, over_keys + jnp.sum(dz, axis=1, keepdims=True)

        dq, over_keys = lax.fori_loop(0, (i * QQ + QQ - 1) // KT + 1, step, (jnp.zeros((QQ, HD), F32), jnp.zeros((QQ, 1), F32)))
        dq_ref[...] = dq.astype(BF)
        dcq_ref[...] = jnp.broadcast_to(over_keys, (QQ, HD))

        @pl.when(i == s // QQ - 1)
        def _():
            dk_ref[...] = dk_acc[...].astype(BF)
            dv_ref[...] = dv_acc[...].astype(BF)
            dc_ref[...] = dc_acc[...]

    blk = pl.BlockSpec((QQ, HD), lambda h, i: (i, h))
    full = pl.BlockSpec((s, HD), lambda h, i: (0, h))
    return pl.pallas_call(
        body, name=name, grid=(nfox, s // QQ),
        in_specs=[pl.BlockSpec((QQ, HD), lambda h, i: (i, N_SB + h)),
                  pl.BlockSpec((s, HD), lambda h, i: (0, N_HEADS + N_SB + h)),
                  pl.BlockSpec((s, HD), lambda h, i: (0, 2 * N_HEADS + N_SB + h)),
                  blk, pl.BlockSpec((nb, 8, HD), lambda h, i: (0, 0, 0)), blk, blk,
                  pl.BlockSpec((QQ, HD), lambda h, i: (i, N_SB + h))],
        out_specs=[blk, full, full, blk, full],
        out_shape=[_sds((s, nfox * HD), BF)] * 3 + [_sds((s, nfox * HD), F32)] * 2,
        scratch_shapes=[pltpu.VMEM((s, HD), F32)] * 3,
        compiler_params=_cp("parallel", "arbitrary"),
    )(qkv, qkv, qkv, cum_b, cum_t, o, lse, do)


GB = 8
DIL_PAD = QB * 16


def _dil_group(g, d, nb, off=0, shift=0):
    if nb >= GB:
        r, n0 = (g * GB) // nb, (g * GB) % nb
        start = off + r + (n0 + shift) * QB * d
        return [pl.ds(pl.multiple_of(start, QB), GB * QB)] if d == 1 else [pl.ds(start, GB * QB, stride=d)]
    per = GB // nb
    return [pl.ds(off + g * per + e + shift * QB * d, nb * QB, stride=d) for e in range(per)]


def _dil_load(ref, g, d, nb, off=0, shift=0):
    parts = [ref[sl, :] for sl in _dil_group(g, d, nb, off, shift)]
    rows = parts[0] if len(parts) == 1 else jnp.concatenate(parts, axis=0)
    return rows.reshape(GB, QB, HD)


def _dil_store(ref, g, d, nb, val, off=0, shift=0, add=False):
    rows = val.reshape(GB * QB, HD)
    slices = _dil_group(g, d, nb, off, shift)
    size = GB * QB // len(slices)
    for b, sl in enumerate(slices):
        piece = rows if len(slices) == 1 else rows[b * size:(b + 1) * size]
        if add:
            ref[sl, :] += piece
        else:
            ref[sl, :] = piece


def _bdot_nt(a, b):
    return lax.dot_general(a, b, (((2,), (2,)), ((0,), (0,))), preferred_element_type=F32)


def _bdot(a, b):
    return lax.dot_general(a, b, (((2,), (1,)), ((0,), (0,))), preferred_element_type=F32)


def _bdot_tn(a, b):
    return lax.dot_general(jnp.swapaxes(a, 1, 2).astype(BF), b, (((2,), (1,)), ((0,), (0,))), preferred_element_type=F32)


def _dil_masks(g, d, nb):
    row = lax.broadcasted_iota(jnp.int32, (GB, QB, QB), 1)
    col = lax.broadcasted_iota(jnp.int32, (GB, QB, QB), 2)
    blk = lax.broadcasted_iota(jnp.int32, (GB, QB, QB), 0)
    blk = blk + (g * GB) % nb if nb >= GB else blk % nb
    return col <= row, jnp.logical_and(col >= row, blk >= 1) if nb > 1 else None


def _dilated_fwd(qkv, name):
    s = qkv.shape[0]
    npat = len(DILATED_PATTERNS)
    chunk = 256

    def body(q_ref, k_ref, v_ref, out_ref, outt_ref, g_ref, qf, kf, vf, *per_pattern):
        o_s, l_s = per_pattern[:npat], per_pattern[npat:]
        qf[...] = q_ref[...].astype(F32)
        for dst, src in ((kf, k_ref), (vf, v_ref)):
            dst[0:DIL_PAD, :] = jnp.zeros((DIL_PAD, HD), F32)
            dst[DIL_PAD:, :] = src[...].astype(F32)
        for p, (_, d) in enumerate(DILATED_PATTERNS):
            nb = s // d // QB

            def grp(g, carry, p=p, d=d, nb=nb):
                mc, mp = _dil_masks(g, d, nb)
                q = _dil_load(qf, g, d, nb).astype(BF)
                zc = jnp.where(mc, _bdot_nt(q, _dil_load(kf, g, d, nb, DIL_PAD).astype(BF)) * SCALE, NEG_INF)
                m = jnp.max(zc, axis=2, keepdims=True)
                if nb > 1:
                    zp = jnp.where(mp, _bdot_nt(q, _dil_load(kf, g, d, nb, DIL_PAD, -1).astype(BF)) * SCALE, NEG_INF)
                    m = jnp.maximum(m, jnp.max(zp, axis=2, keepdims=True))
                ec = jnp.exp(zc - m)
                l = jnp.sum(ec, axis=2, keepdims=True)
                if nb > 1:
                    ep = jnp.where(mp, jnp.exp(zp - m), 0.0)
                    l = l + jnp.sum(ep, axis=2, keepdims=True)
                o = _bdot((ec / l).astype(BF), _dil_load(vf, g, d, nb, DIL_PAD).astype(BF))
                if nb > 1:
                    o = o + _bdot((ep / l).astype(BF), _dil_load(vf, g, d, nb, DIL_PAD, -1).astype(BF))
                _dil_store(o_s[p], g, d, nb, o)
                _dil_store(l_s[p], g, d, nb, jnp.broadcast_to(m + jnp.log(l), (GB, QB, HD)))
                return carry

            lax.fori_loop(0, s // (QB * GB), grp, 0)
        for c0 in range(0, s, chunk):
            rows = slice(c0, c0 + chunk)
            ls = [l_s[p][rows, :] for p in range(npat)]
            m = functools.reduce(jnp.maximum, ls)
            es = [jnp.exp(l - m) for l in ls]
            tot = functools.reduce(lambda a, b: a + b, es)
            out = functools.reduce(lambda a, b: a + b, [(e / tot) * o_s[p][rows, :] for p, e in enumerate(es)])
            out_ref[rows, :] = out.astype(BF)
            outt_ref[:, rows] = out.T.astype(BF)
            g_ref[rows, :] = m + jnp.log(tot)

    full = pl.BlockSpec((s, HD), lambda h: (0, h))
    return pl.pallas_call(
        body, name=name, grid=(N_HEADS,),
        in_specs=[full, pl.BlockSpec((s, HD), lambda h: (0, N_HEADS + h)), pl.BlockSpec((s, HD), lambda h: (0, 2 * N_HEADS + h))],
        out_specs=[full, pl.BlockSpec((HD, s), lambda h: (h, 0)), full],
        out_shape=[_sds((s, N_HEADS * HD), BF), _sds((N_HEADS * HD, s), BF), _sds((s, N_HEADS * HD), F32)],
        scratch_shapes=[pltpu.VMEM((s, HD), F32)] + [pltpu.VMEM((s + DIL_PAD, HD), F32)] * 2 + [pltpu.VMEM((s, HD), F32)] * (2 * npat),
        compiler_params=_cp("parallel"),
    )(qkv, qkv, qkv)


def _dilated_bwd(qkv, out, glse, do, tables, name):
    s = qkv.shape[0]
    chunk = 256

    def body(q_ref, k_ref, v_ref, out_ref, g_ref, do_ref, c_ref, sa_ref, sb_ref, dq_ref, dk_ref, dv_ref,
             qf, kf, vf, dof, dl_s, dq_a, dk_a, dv_a):
        qf[...] = q_ref[...].astype(F32)
        for dst, src in ((kf, k_ref), (vf, v_ref)):
            dst[0:DIL_PAD, :] = jnp.zeros((DIL_PAD, HD), F32)
            dst[DIL_PAD:, :] = src[...].astype(F32)
        for c0 in range(0, s, chunk):
            rows = slice(c0, c0 + chunk)
            dov = do_ref[rows, :].astype(F32)
            dof[rows, :] = dov
            dl_s[rows, :] = jnp.broadcast_to(jnp.sum(dov * out_ref[rows, :].astype(F32), axis=1, keepdims=True), (chunk, HD))
        dq_a[...] = jnp.zeros_like(dq_a)
        dk_a[...] = jnp.zeros_like(dk_a)
        dv_a[...] = jnp.zeros_like(dv_a)
        for _, d in DILATED_PATTERNS:
            nb = s // d // QB

            def grp(g, carry, d=d, nb=nb):
                mc, mp = _dil_masks(g, d, nb)
                q = _dil_load(qf, g, d, nb).astype(BF)
                kc = _dil_load(kf, g, d, nb, DIL_PAD).astype(BF)
                dov = _dil_load(dof, g, d, nb).astype(BF)
                lse = _dil_load(g_ref, g, d, nb)[:, :, 0:1]
                delta = _dil_load(dl_s, g, d, nb)[:, :, 0:1]
                pc = jnp.where(mc, jnp.exp(_bdot_nt(q, kc) * SCALE - lse), 0.0)
                dzc = pc * (_bdot_nt(dov, _dil_load(vf, g, d, nb, DIL_PAD).astype(BF)) - delta) * SCALE
                dq = _bdot(dzc.astype(BF), kc)
                if nb > 1:
                    kp = _dil_load(kf, g, d, nb, DIL_PAD, -1).astype(BF)
                    pp = jnp.where(mp, jnp.exp(_bdot_nt(q, kp) * SCALE - lse), 0.0)
                    dzp = pp * (_bdot_nt(dov, _dil_load(vf, g, d, nb, DIL_PAD, -1).astype(BF)) - delta) * SCALE
                    dq = dq + _bdot(dzp.astype(BF), kp)
                _dil_store(dq_a, g, d, nb, dq, add=True)
                _dil_store(dk_a, g, d, nb, _bdot_tn(dzc, q), DIL_PAD, add=True)
                _dil_store(dv_a, g, d, nb, _bdot_tn(pc, dov), DIL_PAD, add=True)
                if nb > 1:
                    _dil_store(dk_a, g, d, nb, _bdot_tn(dzp, q), DIL_PAD, -1, add=True)
                    _dil_store(dv_a, g, d, nb, _bdot_tn(pp, dov), DIL_PAD, -1, add=True)
                return carry

            lax.fori_loop(0, s // (QB * GB), grp, 0)
        for c0 in range(0, s, chunk):
            rows = slice(c0, c0 + chunk)
            padded = slice(DIL_PAD + c0, DIL_PAD + c0 + chunk)
            c, sa, sb = c_ref[rows, :], sa_ref[rows, :], sb_ref[rows, :]
            dq_ref[rows, :] = _rope(dq_a[rows, :], c, sa, sb).astype(BF)
            dk_ref[rows, :] = _rope(dk_a[padded, :], c, sa, sb).astype(BF)
            dv_ref[rows, :] = dv_a[padded, :].astype(BF)

    full = pl.BlockSpec((s, HD), lambda h: (0, h))
    tab = pl.BlockSpec((s, HD), lambda h: (0, 0))
    return pl.pallas_call(
        body, name=name, grid=(N_HEADS,),
        in_specs=[full, pl.BlockSpec((s, HD), lambda h: (0, N_HEADS + h)), pl.BlockSpec((s, HD), lambda h: (0, 2 * N_HEADS + h)),
                  full, full, full, tab, tab, tab],
        out_specs=[full, full, full], out_shape=[_sds((s, N_HEADS * HD), BF)] * 3,
        scratch_shapes=[pltpu.VMEM((s, HD), F32)] + [pltpu.VMEM((s + DIL_PAD, HD), F32)] * 2 + [pltpu.VMEM((s, HD), F32)] * 3
        + [pltpu.VMEM((s + DIL_PAD, HD), F32)] * 2,
        compiler_params=_cp("parallel"),
    )(qkv, qkv, qkv, out, glse, do, *tables)


def _swiglu_fwd(x, gnorm, w, tag, deps=()):
    h, ht = _rmsnorm_fwd(x, gnorm, f"norm_{tag}", deps)
    g, u, act, act_t = _ffn_up(h, w["gate"], w["up"], f"ffn_up_{tag}")
    if callable(w["down"]):
        w["down"] = w["down"](g)
    y = _ffn_down(act, w["down"], x, f"ffn_down_{tag}")
    return y, (x, ht, g, u, act_t)


def _swiglu_bwd(saved, gnorm, w, dy, dyb_half, out_scale, tag, deps=(), on_down=None, on_grads=None):
    x, ht, g, u, act_t = saved
    dg, du = _ffn_bwd_act(dyb_half, w["down"], g, u, f"ffn_bwd_act_{tag}", deps)
    if on_down:
        d_gate, d_up = _grad_cols(ht, [dg, du], [False, False], f"ffn_bwd_wgu_{tag}")
        tokens = list(on_grads({"gate": d_gate, "up": d_up}))
        d_down = _grad_rows(act_t, dyb_half, f"ffn_bwd_wd_{tag}", tokens)
        gw = {"gate": d_gate, "up": d_up, "down": d_down}
        tokens = list(on_down(d_down))
    else:
        d_down = _grad_rows(act_t, dyb_half, f"ffn_bwd_wd_{tag}")
        d_gate, d_up = _grad_cols(ht, [dg, du], [False, False], f"ffn_bwd_wgu_{tag}")
        gw = {"gate": d_gate, "up": d_up, "down": d_down}
        tokens = list(on_grads(gw)) if on_grads else []
    dh = _dh_cols([dg, du], [w["gate"], w["up"]], [False, False], f"ffn_bwd_dh_{tag}", tokens)
    dx, dxb, dxbt, dgn = _rmsnorm_bwd(x, gnorm, dh, dy, out_scale, f"norm_bwd_{tag}")
    return (dx, dxb, dxbt), dgn, gw


def kernel(x, norm_g, ffn1_w_gate, ffn1_w_up, ffn1_w_down, ffn2_w_gate, ffn2_w_up, ffn2_w_down, even_w_in, even_b_forget, even_w_out, odd_w_qkv, odd_w_out, final_norm_g, loss_target, m_norm_g, m_ffn1_w_gate, m_ffn1_w_up, m_ffn1_w_down, m_ffn2_w_gate, m_ffn2_w_up, m_ffn2_w_down, m_even_w_in, m_even_b_forget, m_even_w_out, m_odd_w_qkv, m_odd_w_out, m_final_norm_g, v_norm_g, v_ffn1_w_gate, v_ffn1_w_up, v_ffn1_w_down, v_ffn2_w_gate, v_ffn2_w_up, v_ffn2_w_down, v_even_w_in, v_even_b_forget, v_even_w_out, v_odd_w_qkv, v_odd_w_out, v_final_norm_g):
    s, d = x.shape[1], x.shape[2]
    nfox = N_HEADS - N_SB
    ax, ay, ac = lax.axis_index("x"), lax.axis_index("y"), lax.axis_index("c")
    me = 4 * ax + 2 * ay + ac
    slots = jnp.stack([4 * px + 2 * py + ac for px, py in [(ax, ay), (1 - ax, ay), (ax, 1 - ay), (1 - ax, 1 - ay)]]).astype(jnp.int32)
    x0 = x.reshape(s, d)
    target = loss_target.reshape(s, d)

    def bf(w):
        return w.astype(BF)

    groups = [
        [bf(ffn1_w_gate[0]), bf(ffn1_w_up[0]), norm_g.reshape(6, d // NDEV)],
        [bf(even_w_in[0]), bf(even_w_out[0])],
        [bf(ffn2_w_gate[0]), bf(ffn2_w_up[0]), bf(ffn2_w_down[0])],
        [bf(ffn1_w_gate[1]), bf(ffn1_w_up[1]), bf(ffn1_w_down[1])],
        [bf(odd_w_qkv[0]), bf(odd_w_out[0])],
        [bf(ffn2_w_gate[1]), bf(ffn2_w_up[1]), bf(ffn2_w_down[1])],
        [bf(ffn1_w_down[0])],
    ]
    started = [None] * len(groups)
    last_token = []
    for k in (0, 6, 1, 2, 3, 4, 5):
        started[k] = _gather_start(groups[k], me, last_token, f"gather_start_{k}")
        last_token = [started[k]["token"]]
    all_started = last_token

    def forward_early(k, after):
        started[k] = _gather_forward(started[k], after, f"gather_forward_{k}")
        return [started[k]["token"]]

    def gathered(k, after):
        st = started[k] if "send2" in started[k] else _gather_forward(started[k], after, f"gather_forward_{k}")
        return _gather_finish(st, after, f"gather_finish_{k}")

    def ffn_weights(ws_):
        return {"gate": ws_[0], "up": ws_[1], "down": ws_[2]}

    b_pad = jnp.pad(even_b_forget, ((0, 0), (0, HD - nfox)))
    gfin = final_norm_g.reshape(1, d)

    g0 = gathered(0, x0)
    gn = jnp.transpose(g0[2], (1, 0, 2)).reshape(6, 1, d)
    wf = [[{"gate": g0[0], "up": g0[1], "down": lambda after: gathered(6, after)[0]}, None], [None, None]]
    x1, sv_f1_0 = _swiglu_fwd(x0, gn[0], wf[0][0], "l0a", all_started)
    g1 = gathered(1, x1)
    w_in_nat = jnp.transpose(g1[0], (1, 0, 2)).reshape(d, -1)
    w_qkv_e = w_in_nat[:, :3 * d]
    w_f = jnp.pad(w_in_nat[:, 3 * d:], ((0, 0), (0, HD - nfox)))
    w_out_e = g1[1].reshape(d, d)
    h_e, ht_e = _rmsnorm_fwd(x1, gn[1], "norm_l0m")
    qkv_e = _mm_nn(h_e, w_qkv_e, 768, BF, "even_qkv")
    f_e = _mm_nn(h_e, w_f, HD, F32, "even_fgate")
    o_sb, ot_sb, tot_sb = _sb_fwd(qkv_e, "sb_fwd")
    cum_b, cum_t = _fgate_fwd(f_e, b_pad, "fgate_fwd")
    o_fox, ot_fox, lse_fox = _fox_fwd(qkv_e, cum_b, cum_t, "fox_fwd")
    o_e = jnp.concatenate([o_sb, o_fox], axis=1)
    ot_e = jnp.concatenate([ot_sb, ot_fox], axis=0)
    x2 = _mm_nn(o_e, w_out_e, 1024, F32, "even_out", res=x1, deps=forward_early(2, o_e))
    wf[0][1] = ffn_weights(gathered(2, x2))
    x3, sv_f2_0 = _swiglu_fwd(x2, gn[2], wf[0][1], "l0b")

    wf[1][0] = ffn_weights(gathered(3, x3))
    x4, sv_f1_1 = _swiglu_fwd(x3, gn[3], wf[1][0], "l1a")
    g4 = gathered(4, x4)
    w_qkv_o = g4[0]
    w_out_o = g4[1].reshape(d, d)
    h_o, ht_o = _rmsnorm_fwd(x4, gn[4], "norm_l1m")
    qkv_o = _qkv_rope(h_o, w_qkv_o, _rope_tables(s, 1.0), "odd_qkv")
    o_o, ot_o, glse = _dilated_fwd(qkv_o, "dilated_fwd")
    x5 = _mm_nn(o_o, w_out_o, 1024, F32, "odd_out", res=x4)
    wf[1][1] = ffn_weights(gathered(5, x5))
    x6, sv_f2_1 = _swiglu_fwd(x5, gn[5], wf[1][1], "l1b")

    def chip_sums(gs, a_s, tag):
        ps = _pair_sum(gs, a_s, slots, f"pair_sum_{tag}")
        return gs, a_s, _chip_start(ps, f"chip_start_{tag}")

    def as_slices(gs):
        return [g_ if g_.ndim == 3 else g_.reshape(NDEV, g_.shape[0] // NDEV, g_.shape[1]) for g_ in gs]

    def reduce_start(gs, tag):
        gs = as_slices(gs)
        return chip_sums(gs, _pair_exchange(gs, f"pair_exchange_{tag}"), tag)

    red, crossing = {}, {}

    def cross(gs, tag):
        crossing[tag] = _pair_start(as_slices(gs), f"pair_start_{tag}")
        return [crossing[tag]["token"]]

    def reduce_behind_dh(tag):
        return lambda gw: cross([gw["gate"], gw["up"], gw["down"]], tag)

    def reduce_after(tag, after):
        red[tag] = chip_sums(*_pair_finish(crossing[tag], after, f"pair_finish_{tag}"), tag)
        return [red[tag][2]["token"]]

    def reduce_now(tag, names):
        def hook(gw):
            red[tag] = reduce_start([gw[nm] for nm in names] if names else [gw], tag)
            return [red[tag][2]["token"]]
        return hook

    dx6, dx6b, _, d_gfin, loss_part = _loss_head(x6, gfin, target, "loss_head")

    (dx5, dx5b, dx5bt), dgn5, _ = _swiglu_bwd(sv_f2_1, gn[5], wf[1][1], dx6, dx6b, 1.0, "l1b", on_grads=reduce_behind_dh("l1b"))
    d_wout_o = _mm_nn(ot_o, dx5b, 1024, BF, "odd_out_dw")
    do_o = _mm_nt([(dx5b, w_out_o)], "odd_out_do", BF, deps=reduce_after("l1b", dx5))
    dqkv_o = jnp.concatenate(_dilated_bwd(qkv_o, o_o, glse, do_o, _rope_tables(s, -1.0), "dilated_bwd"), axis=1)
    (d_wqkv_o,) = _grad_cols(ht_o, [dqkv_o], [True], "odd_qkv_dw")
    dh_o = _dh_cols([dqkv_o], [w_qkv_o], [True], "odd_qkv_dh", cross([d_wqkv_o, d_wout_o], "l1m"))
    dx4, dx4b, _, dgn4 = _rmsnorm_bwd(x4, gn[4], dh_o, dx5, 0.5, "norm_bwd_l1m")
    (dx3, dx3b, _), dgn3, _ = _swiglu_bwd(sv_f1_1, gn[3], wf[1][0], dx4, dx4b, 0.5, "l1a", reduce_after("l1m", dx4),
                                         on_grads=reduce_behind_dh("l1a"))

    (dx2, dx2b, dx2bt), dgn2, _ = _swiglu_bwd(sv_f2_0, gn[2], wf[0][1], dx3, dx3b, 1.0, "l0b", reduce_after("l1a", dx3),
                                             on_grads=reduce_behind_dh("l0b"))
    d_wout_e = _mm_nn(ot_e, dx2b, 1024, BF, "even_out_dw")
    do_e = _mm_nt([(dx2b, w_out_e)], "even_out_do", BF, deps=reduce_after("l0b", dx2))
    dq_sb, dk_sb, dv_sb = _sb_bwd(qkv_e, do_e, tot_sb, "sb_bwd")
    dq_fx, dk_fx, dv_fx, dcq, dck = _fox_bwd(qkv_e, cum_b, cum_t, o_fox, lse_fox, do_e, "fox_bwd")
    df, db_part = _fgate_bwd(dcq, dck, f_e, b_pad, "fgate_bwd")
    dqkv_e = jnp.concatenate([dq_sb, dq_fx, dk_sb, dk_fx, dv_sb, dv_fx], axis=1)
    d_wqkv_e = _mm_nn(ht_e, dqkv_e, 768, BF, "even_qkv_dw")
    d_wf = _mm_nn(ht_e, df, HD, BF, "even_fgate_dw")
    d_win_nat = jnp.concatenate([d_wqkv_e, d_wf[:, :nfox]], axis=1)
    d_win = jnp.transpose(d_win_nat.reshape(d, NDEV, -1), (1, 0, 2))
    dh_e = _mm_nt([(dqkv_e, w_qkv_e), (df, w_f)], "even_in_dh", deps=cross([d_win, d_wout_e], "l0m"))
    dx1, dx1b, _, dgn1 = _rmsnorm_bwd(x1, gn[1], dh_e, dx2, 0.5, "norm_bwd_l0m")
    (dx0, _, _), dgn0, _ = _swiglu_bwd(sv_f1_0, gn[0], wf[0][0], dx1, dx1b, 1.0, "l0a", reduce_after("l0m", dx1),
                                      on_down=reduce_now("l0a_down", None), on_grads=reduce_now("l0a_gu", ["gate", "up"]))

    def reduce_finish(red, tag, after):
        gs, a_s, st = red
        return list(zip(gs, a_s, _chip_finish(st, after, f"chip_finish_{tag}")))

    f_l1b, f_l1m, f_l1a = (reduce_finish(red[t], t, dx0) for t in ("l1b", "l1m", "l1a"))
    f_l0b, f_l0m = (reduce_finish(red[t], t, dx0) for t in ("l0b", "l0m"))

    def update(w_, m_, v_, parts, nm):
        if w_.shape[2] % 128 == 0:
            return _adamw_sharded(w_, m_, v_, parts, slots, f"adamw_{nm}")
        outs = _adamw_sharded(jnp.swapaxes(w_, 1, 2), jnp.swapaxes(m_, 1, 2), jnp.swapaxes(v_, 1, 2), parts, slots,
                              f"adamw_{nm}", transposed=True)
        return [jnp.swapaxes(o, 1, 2) for o in outs]

    res = {}
    res["even_w_in"] = update(even_w_in, m_even_w_in, v_even_w_in, [f_l0m[0]], "even_w_in")
    res["even_w_out"] = _adamw_sharded(even_w_out, m_even_w_out, v_even_w_out, [f_l0m[1]], slots, "adamw_even_w_out")
    res["odd_w_qkv"] = _adamw_sharded(odd_w_qkv, m_odd_w_qkv, v_odd_w_qkv, [f_l1m[0]], slots, "adamw_odd_w_qkv")
    res["odd_w_out"] = _adamw_sharded(odd_w_out, m_odd_w_out, v_odd_w_out, [f_l1m[1]], slots, "adamw_odd_w_out")
    names = ["ffn2_w_gate", "ffn2_w_up", "ffn2_w_down", "ffn1_w_gate", "ffn1_w_up", "ffn1_w_down"]
    ws = [ffn2_w_gate, ffn2_w_up, ffn2_w_down, ffn1_w_gate, ffn1_w_up, ffn1_w_down]
    ms = [m_ffn2_w_gate, m_ffn2_w_up, m_ffn2_w_down, m_ffn1_w_gate, m_ffn1_w_up, m_ffn1_w_down]
    vs = [v_ffn2_w_gate, v_ffn2_w_up, v_ffn2_w_down, v_ffn1_w_gate, v_ffn1_w_up, v_ffn1_w_down]
    for k in range(3):
        res[names[k]] = update(ws[k], ms[k], vs[k], [f_l0b[k], f_l1b[k]], names[k])
    f_l0a = (reduce_finish(red["l0a_gu"], "l0a_gu", res["ffn2_w_down"][1])
             + reduce_finish(red["l0a_down"], "l0a_down", res["ffn2_w_down"][1]))
    for k in range(3, 6):
        res[names[k]] = update(ws[k], ms[k], vs[k], [f_l0a[k - 3], f_l1a[k - 3]], names[k])

    dnorm = jnp.concatenate([dgn0, dgn1, dgn2, dgn3, dgn4, dgn5], axis=0)
    nsm = d // NDEV
    small_rows = (6 * d + d + 2 * HD) // HD
    pad_rows = -small_rows % 8
    part = jnp.concatenate([dnorm.reshape(-1), d_gfin.reshape(-1), db_part.reshape(-1), loss_part.reshape(-1),
                            jnp.zeros((pad_rows * HD,), F32)]).reshape(small_rows + pad_rows, HD)
    (gathered,) = _all_gather([part], "gather_small")

    def pack(ng, bfg, fg):
        full = lax.dynamic_update_slice(jnp.zeros((6, d), F32), ng.reshape(6, nsm), (0, me * nsm))
        return jnp.concatenate([full.reshape(-1), fg.reshape(-1), jnp.pad(bfg.reshape(-1), (0, HD - nfox)),
                                jnp.zeros((HD + pad_rows * HD,), F32)]).reshape(small_rows + pad_rows, HD)

    sm = _adamw_small(pack(norm_g, even_b_forget, final_norm_g), pack(m_norm_g, m_even_b_forget, m_final_norm_g),
                      pack(v_norm_g, v_even_b_forget, v_final_norm_g), gathered, "adamw_small")

    def unpack(t):
        flat = t.reshape(-1)
        ng = lax.dynamic_slice(flat[:6 * d].reshape(6, d), (0, me * nsm), (6, nsm)).reshape(norm_g.shape)
        fg = flat[6 * d:7 * d].reshape(final_norm_g.shape)
        bfg = flat[7 * d:7 * d + nfox].reshape(even_b_forget.shape)
        return ng, bfg, fg

    sm_g, sm_d, sm_m, sm_v = [unpack(t) for t in sm]
    loss = sm[0].reshape(-1)[7 * d + HD]

    order = ["norm_g", "ffn1_w_gate", "ffn1_w_up", "ffn1_w_down", "ffn2_w_gate", "ffn2_w_up", "ffn2_w_down", "even_w_in",
             "even_b_forget", "even_w_out", "odd_w_qkv", "odd_w_out", "final_norm_g"]
    outs = [loss, dx0.reshape(x.shape)]
    for k in range(4):
        smk = [sm_g, sm_d, sm_m, sm_v][k]
        for nm in order:
            if nm == "norm_g":
                outs.append(smk[0])
            elif nm == "even_b_forget":
                outs.append(smk[1])
            elif nm == "final_norm_g":
                outs.append(smk[2])
            else:
                outs.append(res[nm][k])
    return tuple(outs)
```

```python
import functools

import jax
import jax.numpy as jnp
import numpy as np
from jax import lax
from jax.experimental import pallas as pl
from jax.experimental.pallas import tpu as pltpu

F32 = jnp.float32
BF = jnp.bfloat16
NDEV = 8
HD = 128
QB = 128
N_HEADS = 16
N_SB = 8
SCALE = HD ** -0.5
ROPE_THETA = 500000.0
ROPE_DIMS = HD // 4
DILATED_PATTERNS = ((128, 1), (512, 4), (2048, 16))
RMS_EPS = 1e-6
NEG_INF = -1e30
ADAM_LR = 0.001
ADAM_B1 = 0.9
ADAM_B2 = 0.999
ADAM_EPS = 1e-08
ADAM_WD = 0.01
ADAM_STEP = 10
VMEM_LIMIT_V7X = 56 * 1024 * 1024
MESH = pl.DeviceIdType.MESH
ANY = pl.BlockSpec(memory_space=pl.ANY)

NT_DIMS = (((1,), (1,)), ((), ()))


def _cp(*dims):
    return pltpu.CompilerParams(dimension_semantics=dims if dims else None, vmem_limit_bytes=VMEM_LIMIT_V7X)


def _dot(a, b):
    return jnp.dot(a, b, preferred_element_type=F32)


def _dot_nt(a, b):
    return lax.dot_general(a, b, NT_DIMS, preferred_element_type=F32)


def _sds(shape, dtype):
    return jax.ShapeDtypeStruct(shape, dtype)


def _place():
    x, y, c = lax.axis_index("x"), lax.axis_index("y"), lax.axis_index("c")
    chips = [(x, y), (1 - x, y), (x, 1 - y), (1 - x, 1 - y)]
    return x, y, c, chips


def _all_gather(xs, name):
    n = len(xs)

    def body(*refs):
        x_refs, out_refs = refs[:n], refs[n:2 * n]
        send_sems, recv_sems, local_sems = refs[2 * n:]
        x, y, c, chips = _place()
        me, sibling = (x, y, c), (x, y, 1 - c)
        others = chips[1:]

        def slot(a, px, py, pc):
            return out_refs[a].at[4 * px + 2 * py + pc]

        def copy(a, k, block, to, src=None):
            return pltpu.make_async_remote_copy(
                src_ref=slot(a, *block) if src is None else src, dst_ref=slot(a, *block),
                send_sem=send_sems.at[a, k], recv_sem=recv_sems.at[a, k], device_id=to, device_id_type=MESH)

        started = []
        for a in range(n):
            mine = pltpu.make_async_copy(x_refs[a], slot(a, *me), local_sems.at[a])
            mine.start()
            first = [copy(a, 0, me, sibling, src=x_refs[a])]
            first += [copy(a, 1 + j, me, (*chip, c), src=x_refs[a]) for j, chip in enumerate(others)]
            for cp in first:
                cp.start()
            started += [mine.wait] + [cp.wait_send for cp in first]
        for a in range(n):
            for j, chip in enumerate(others):
                copy(a, 1 + j, (*chip, c), me).wait_recv()
                passed = copy(a, 4 + j, (*chip, c), sibling)
                passed.start()
                started.append(passed.wait_send)
        for a in range(n):
            copy(a, 0, sibling, me).wait_recv()
            for j, chip in enumerate(others):
                copy(a, 4 + j, (*chip, 1 - c), me).wait_recv()
        for w in started:
            w()

    return pl.pallas_call(
        body, name=name,
        out_shape=[_sds((NDEV,) + x.shape, x.dtype) for x in xs],
        in_specs=[ANY] * n, out_specs=[ANY] * n,
        scratch_shapes=[pltpu.SemaphoreType.DMA((n, 7)), pltpu.SemaphoreType.DMA((n, 7)), pltpu.SemaphoreType.DMA((n,))],
    )(*xs)


def _pair_exchange(gs, name):
    n = len(gs)

    def body(*refs):
        g_refs, a_refs = refs[:n], refs[n:2 * n]
        send_sems, recv_sems = refs[2 * n:]
        x, y, c, chips = _place()
        copies = []
        for a in range(n):
            for j, (px, py) in enumerate(chips):
                copies.append(pltpu.make_async_remote_copy(
                    src_ref=g_refs[a].at[4 * px + 2 * py + (1 - c)], dst_ref=a_refs[a].at[j],
                    send_sem=send_sems.at[a, j], recv_sem=recv_sems.at[a, j],
                    device_id=(x, y, 1 - c), device_id_type=MESH))
        for cp in copies:
            cp.start()
        for cp in copies:
            cp.wait()

    return pl.pallas_call(
        body, name=name,
        out_shape=[_sds((4,) + g.shape[1:], g.dtype) for g in gs],
        in_specs=[ANY] * n, out_specs=[ANY] * n,
        scratch_shapes=[pltpu.SemaphoreType.DMA((n, 4)), pltpu.SemaphoreType.DMA((n, 4))],
    )(*gs)


HBM = pl.BlockSpec(memory_space=pltpu.HBM)
SEM = pl.BlockSpec(memory_space=pltpu.SEMAPHORE)
EFFECT = pltpu.SideEffectType.DATAFLOW_SIDE_EFFECTING
TOKEN = _sds((8, 128), F32)
TOKEN_SPEC = pl.BlockSpec((8, 128), lambda *_: (0, 0))


def _in_hbm(x):
    return pltpu.with_memory_space_constraint(x, pltpu.HBM)


def _ignore_deps(body, n_in, n_deps):
    if not n_deps:
        return body
    return lambda *refs: body(*refs[:n_in], *refs[n_in + n_deps:])


def _slot_of(px, py, pc):
    return 4 * px + 2 * py + pc


def _gather_start(xs, me, deps, name):
    n = len(xs)
    lands = [lax.dynamic_update_slice(lax.empty((NDEV,) + x.shape, x.dtype), x[None], (me,) + (0,) * x.ndim) for x in xs]

    def body(*refs):
        x_refs, land_refs = refs[:n], refs[n:2 * n]
        send, recv_ici, recv_sib = refs[2 * n:2 * n + 3]
        token = refs[4 * n + 3]
        x, y, c, chips = _place()
        for a in range(n):
            dst = land_refs[a].at[_slot_of(x, y, c)]
            pltpu.make_async_remote_copy(src_ref=x_refs[a], dst_ref=dst, send_sem=send.at[4 * a], recv_sem=recv_sib.at[a],
                                         device_id=(x, y, 1 - c), device_id_type=MESH).start()
            for j, chip in enumerate(chips[1:]):
                pltpu.make_async_remote_copy(src_ref=x_refs[a], dst_ref=dst, send_sem=send.at[4 * a + 1 + j], recv_sem=recv_ici.at[3 * a + j],
                                             device_id=(*chip, c), device_id_type=MESH).start()
        token[...] = jnp.zeros_like(token)

    outs = pl.pallas_call(
        _ignore_deps(body, 2 * n, len(deps)), name=name,
        out_shape=(pltpu.SemaphoreType.DMA((4 * n,)), pltpu.SemaphoreType.DMA((3 * n,)), pltpu.SemaphoreType.DMA((n,)),
                   *[pltpu.HBM(x.shape, x.dtype) for x in xs], *[pltpu.HBM(l.shape, l.dtype) for l in lands], TOKEN),
        in_specs=[HBM] * (2 * n) + [TOKEN_SPEC] * len(deps),
        out_specs=(SEM, SEM, SEM, *[HBM] * (2 * n), pl.BlockSpec(memory_space=pltpu.VMEM)),
        input_output_aliases={a: 3 + a for a in range(2 * n)},
        compiler_params=pltpu.CompilerParams(has_side_effects=EFFECT),
    )(*[_in_hbm(x) for x in xs], *[_in_hbm(l) for l in lands], *deps)
    send, recv_ici, recv_sib = outs[:3]
    return dict(send=send, recv_ici=recv_ici, recv_sib=recv_sib, xs=list(outs[3:3 + n]), lands=list(outs[3 + n:3 + 2 * n]), token=outs[-1])


def _gather_forward(st, after, name):
    n = len(st["lands"])

    def body(*refs):
        land_refs, recv_ici = refs[:n], refs[n]
        send2, recv2, token = refs[n + 2], refs[n + 3], refs[2 * n + 4]
        x, y, c, chips = _place()
        for a in range(n):
            for j, chip in enumerate(chips[1:]):
                blk = land_refs[a].at[_slot_of(*chip, c)]
                pltpu.make_async_remote_copy(src_ref=blk, dst_ref=blk, send_sem=send2.at[3 * a + j], recv_sem=recv_ici.at[3 * a + j],
                                             device_id=(*chip, c), device_id_type=MESH).wait_recv()
                pltpu.make_async_remote_copy(src_ref=blk, dst_ref=blk, send_sem=send2.at[3 * a + j], recv_sem=recv2.at[3 * a + j],
                                             device_id=(x, y, 1 - c), device_id_type=MESH).start()
        token[...] = jnp.zeros_like(token)

    outs = pl.pallas_call(
        body, name=name,
        out_shape=(pltpu.SemaphoreType.DMA((3 * n,)), pltpu.SemaphoreType.DMA((3 * n,)), *[pltpu.HBM(l.shape, l.dtype) for l in st["lands"]], TOKEN),
        in_specs=[HBM] * n + [SEM, pl.BlockSpec(memory_space=pl.ANY)],
        out_specs=(SEM, SEM, *[HBM] * n, pl.BlockSpec(memory_space=pltpu.VMEM)),
        input_output_aliases={a: 2 + a for a in range(n)},
        compiler_params=pltpu.CompilerParams(has_side_effects=EFFECT),
    )(*st["lands"], st["recv_ici"], after)
    return dict(st, send2=outs[0], recv2=outs[1], lands=list(outs[2:2 + n]), token=outs[-1])


def _gather_finish(st, after, name):
    n = len(st["lands"])

    def body(*refs):
        x_refs, land_refs = refs[:n], refs[n:2 * n]
        send, recv_sib, send2, recv2 = refs[2 * n:2 * n + 4]
        x, y, c, chips = _place()
        for a in range(n):
            mine = land_refs[a].at[_slot_of(x, y, c)]
            theirs = land_refs[a].at[_slot_of(x, y, 1 - c)]
            for k in range(4):
                pltpu.make_async_remote_copy(src_ref=x_refs[a], dst_ref=mine, send_sem=send.at[4 * a + k], recv_sem=recv_sib.at[a],
                                             device_id=(x, y, 1 - c), device_id_type=MESH).wait_send()
            pltpu.make_async_remote_copy(src_ref=x_refs[a], dst_ref=theirs, send_sem=send.at[4 * a], recv_sem=recv_sib.at[a],
                                         device_id=(x, y, 1 - c), device_id_type=MESH).wait_recv()
            for j, chip in enumerate(chips[1:]):
                sent = land_refs[a].at[_slot_of(*chip, c)]
                got = land_refs[a].at[_slot_of(*chip, 1 - c)]
                pltpu.make_async_remote_copy(src_ref=sent, dst_ref=sent, send_sem=send2.at[3 * a + j], recv_sem=recv2.at[3 * a + j],
                                             device_id=(x, y, 1 - c), device_id_type=MESH).wait_send()
                pltpu.make_async_remote_copy(src_ref=got, dst_ref=got, send_sem=send2.at[3 * a + j], recv_sem=recv2.at[3 * a + j],
                                             device_id=(x, y, 1 - c), device_id_type=MESH).wait_recv()

    outs = pl.pallas_call(
        body, name=name,
        out_shape=tuple(pltpu.HBM(v.shape, v.dtype) for v in st["xs"] + st["lands"]),
        in_specs=[HBM] * (2 * n) + [SEM] * 4 + [pl.BlockSpec(memory_space=pl.ANY)], out_specs=tuple([HBM] * (2 * n)),
        input_output_aliases={a: a for a in range(2 * n)},
        compiler_params=pltpu.CompilerParams(has_side_effects=EFFECT),
    )(*st["xs"], *st["lands"], st["send"], st["recv_sib"], st["send2"], st["recv2"], after)
    return list(outs[n:])


def _pair_start(gs, name):
    n = len(gs)
    lands = [lax.empty((4,) + g.shape[1:], g.dtype) for g in gs]

    def body(*refs):
        g_refs, a_refs = refs[:n], refs[n:2 * n]
        send, recv = refs[2 * n], refs[2 * n + 1]
        token = refs[4 * n + 2]
        x, y, c, chips = _place()
        for a in range(n):
            for j, (px, py) in enumerate(chips):
                pltpu.make_async_remote_copy(src_ref=g_refs[a].at[_slot_of(px, py, 1 - c)], dst_ref=a_refs[a].at[j],
                                             send_sem=send.at[4 * a + j], recv_sem=recv.at[4 * a + j],
                                             device_id=(x, y, 1 - c), device_id_type=MESH).start()
        token[...] = jnp.zeros_like(token)

    outs = pl.pallas_call(
        body, name=name,
        out_shape=(pltpu.SemaphoreType.DMA((4 * n,)), pltpu.SemaphoreType.DMA((4 * n,)),
                   *[pltpu.HBM(g.shape, g.dtype) for g in gs], *[pltpu.HBM(l.shape, l.dtype) for l in lands], TOKEN),
        in_specs=[HBM] * (2 * n), out_specs=(SEM, SEM, *[HBM] * (2 * n), pl.BlockSpec(memory_space=pltpu.VMEM)),
        input_output_aliases={a: 2 + a for a in range(2 * n)},
        compiler_params=pltpu.CompilerParams(has_side_effects=EFFECT),
    )(*[_in_hbm(g) for g in gs], *[_in_hbm(l) for l in lands])
    return dict(send=outs[0], recv=outs[1], gs=list(outs[2:2 + n]), lands=list(outs[2 + n:2 + 2 * n]), token=outs[-1])


def _pair_finish(st, after, name):
    n = len(st["gs"])

    def body(*refs):
        g_refs, a_refs = refs[:n], refs[n:2 * n]
        send, recv = refs[2 * n], refs[2 * n + 1]
        x, y, c, chips = _place()
        for a in range(n):
            for j, (px, py) in enumerate(chips):
                cp = pltpu.make_async_remote_copy(src_ref=g_refs[a].at[_slot_of(px, py, 1 - c)], dst_ref=a_refs[a].at[j],
                                                  send_sem=send.at[4 * a + j], recv_sem=recv.at[4 * a + j],
                                                  device_id=(x, y, 1 - c), device_id_type=MESH)
                cp.wait_send()
                cp.wait_recv()

    outs = pl.pallas_call(
        body, name=name,
        out_shape=tuple(pltpu.HBM(v.shape, v.dtype) for v in st["gs"] + st["lands"]),
        in_specs=[HBM] * (2 * n) + [SEM, SEM, pl.BlockSpec(memory_space=pl.ANY)], out_specs=tuple([HBM] * (2 * n)),
        input_output_aliases={a: a for a in range(2 * n)},
        compiler_params=pltpu.CompilerParams(has_side_effects=EFFECT),
    )(*st["gs"], *st["lands"], st["send"], st["recv"], after)
    return list(outs[:n]), list(outs[n:])


def _chip_start(ps, name):
    n = len(ps)
    lands = [lax.empty(p.shape, p.dtype) for p in ps]

    def body(*refs):
        p_refs, b_refs = refs[:n], refs[n:2 * n]
        send, recv = refs[2 * n], refs[2 * n + 1]
        token = refs[4 * n + 2]
        x, y, c, chips = _place()
        for a in range(n):
            for j, chip in enumerate(chips[1:]):
                pltpu.make_async_remote_copy(src_ref=p_refs[a].at[j], dst_ref=b_refs[a].at[j], send_sem=send.at[3 * a + j], recv_sem=recv.at[3 * a + j],
                                             device_id=(*chip, c), device_id_type=MESH).start()
        token[...] = jnp.zeros_like(token)

    outs = pl.pallas_call(
        body, name=name,
        out_shape=(pltpu.SemaphoreType.DMA((3 * n,)), pltpu.SemaphoreType.DMA((3 * n,)),
                   *[pltpu.HBM(p.shape, p.dtype) for p in ps], *[pltpu.HBM(p.shape, p.dtype) for p in ps], TOKEN),
        in_specs=[HBM] * (2 * n), out_specs=(SEM, SEM, *[HBM] * (2 * n), pl.BlockSpec(memory_space=pltpu.VMEM)),
        input_output_aliases={a: 2 + a for a in range(2 * n)},
        compiler_params=pltpu.CompilerParams(has_side_effects=EFFECT),
    )(*[_in_hbm(p) for p in ps], *[_in_hbm(l) for l in lands])
    return dict(send=outs[0], recv=outs[1], ps=list(outs[2:2 + n]), lands=list(outs[2 + n:2 + 2 * n]), token=outs[-1])


def _chip_finish(st, after, name):
    n = len(st["ps"])

    def body(*refs):
        p_refs, b_refs = refs[:n], refs[n:2 * n]
        send, recv = refs[2 * n], refs[2 * n + 1]
        x, y, c, chips = _place()
        for a in range(n):
            for j, chip in enumerate(chips[1:]):
                cp = pltpu.make_async_remote_copy(src_ref=p_refs[a].at[j], dst_ref=b_refs[a].at[j], send_sem=send.at[3 * a + j], recv_sem=recv.at[3 * a + j],
                                                  device_id=(*chip, c), device_id_type=MESH)
                cp.wait_send()
                cp.wait_recv()

    outs = pl.pallas_call(
        body, name=name,
        out_shape=tuple(pltpu.HBM(v.shape, v.dtype) for v in st["ps"] + st["lands"]),
        in_specs=[HBM] * (2 * n) + [SEM, SEM, pl.BlockSpec(memory_space=pl.ANY)], out_specs=tuple([HBM] * (2 * n)),
        input_output_aliases={a: a for a in range(2 * n)},
        compiler_params=pltpu.CompilerParams(has_side_effects=EFFECT),
    )(*st["ps"], *st["lands"], st["send"], st["recv"], after)
    return list(outs[n:])


def _rows_tile(r):
    for t in (512, 256, 128, 64, 32, 16):
        if r % t == 0:
            return t
    return r


PAIR_SUM_STEPS = 4


def _pair_sum(gs, a_s, slots, name):
    n = len(gs)
    trs = [g.shape[1] // PAIR_SUM_STEPS for g in gs]

    def body(slots_ref, *refs):
        for g_ref, a_ref, p_ref in zip(refs[:n], refs[n:2 * n], refs[2 * n:]):
            p_ref[...] = (g_ref[...].astype(F32) + a_ref[...].astype(F32)).astype(BF)

    def spec(g, tr, index):
        return pl.BlockSpec((None, tr, g.shape[2]), index)

    return pl.pallas_call(
        body, name=name,
        grid_spec=pltpu.PrefetchScalarGridSpec(
            num_scalar_prefetch=1, grid=(3, PAIR_SUM_STEPS),
            in_specs=[spec(g, tr, lambda j, i, s: (s[j + 1], i, 0)) for g, tr in zip(gs, trs)]
            + [spec(g, tr, lambda j, i, s: (j + 1, i, 0)) for g, tr in zip(gs, trs)],
            out_specs=[spec(g, tr, lambda j, i, s: (j, i, 0)) for g, tr in zip(gs, trs)]),
        out_shape=[_sds((3,) + g.shape[1:], BF) for g in gs], compiler_params=_cp("parallel", "parallel"),
    )(slots, *gs, *a_s)


def _adamw_math(w, g, m, v):
    m = ADAM_B1 * m + (1.0 - ADAM_B1) * g
    v = ADAM_B2 * v + (1.0 - ADAM_B2) * (g * g)
    m_hat = m / (1.0 - ADAM_B1 ** ADAM_STEP)
    v_hat = v / (1.0 - ADAM_B2 ** ADAM_STEP)
    delta = -ADAM_LR * (m_hat / (jnp.sqrt(v_hat) + ADAM_EPS) + ADAM_WD * w)
    return delta, m, v


def _adamw_sharded(w, m, v, parts, slots, name, transposed=False):
    nl = w.shape[0]
    r, c = parts[0][0].shape[1:]
    tr = _rows_tile(r)
    if c * tr * 4 > (1 << 21) and not transposed:
        tr = max(8, tr // 2)

    def body(slots_ref, w_ref, m_ref, v_ref, *rest):
        part_refs, (g_out, d_out, m_out, v_out) = rest[:5 * nl], rest[5 * nl:]
        layer = pl.program_id(0)
        g = None
        for l in range(nl):
            s = part_refs[5 * l][...].astype(F32)
            for ref in part_refs[5 * l + 1:5 * l + 5]:
                s = s + ref[...].astype(F32)
            g = s if g is None else jnp.where(layer == l, s, g)
        if transposed:
            g = g.T
        delta, mn, vn = _adamw_math(w_ref[...], g, m_ref[...], v_ref[...])
        g_out[...] = g
        d_out[...] = delta
        m_out[...] = mn
        v_out[...] = vn

    def own(l):
        return lambda L, i, s: (s[0], jnp.where(L == l, i, 0), 0)

    def fixed(l, k):
        return lambda L, i, s: (k, jnp.where(L == l, i, 0), 0)

    if transposed:
        wspec = pl.BlockSpec((None, c, tr), lambda L, i, s: (L, 0, i))
    else:
        wspec = pl.BlockSpec((None, tr, c), lambda L, i, s: (L, i, 0))
    in_specs = [wspec, wspec, wspec]
    args = [w, m, v]
    for l, (g, a, b) in enumerate(parts):
        in_specs += [pl.BlockSpec((None, tr, c), own(l)), pl.BlockSpec((None, tr, c), fixed(l, 0)),
                     pl.BlockSpec((None, tr, c), fixed(l, 0)), pl.BlockSpec((None, tr, c), fixed(l, 1)),
                     pl.BlockSpec((None, tr, c), fixed(l, 2))]
        args += [g, a, b, b, b]
    return pl.pallas_call(
        body, name=name,
        grid_spec=pltpu.PrefetchScalarGridSpec(
            num_scalar_prefetch=1, grid=(nl, r // tr), in_specs=in_specs, out_specs=[wspec] * 4),
        out_shape=[_sds(w.shape, F32)] * 4, compiler_params=_cp("arbitrary", "arbitrary"),
    )(slots, *args)


def _adamw_small(w, m, v, gathered, name):
    def body(w_ref, m_ref, v_ref, gg_ref, g_out, d_out, m_out, v_out):
        g = gg_ref[0]
        for k in range(1, NDEV):
            g = g + gg_ref[k]
        delta, mn, vn = _adamw_math(w_ref[...], g, m_ref[...], v_ref[...])
        g_out[...] = g
        d_out[...] = delta
        m_out[...] = mn
        v_out[...] = vn

    return pl.pallas_call(body, name=name, out_shape=[_sds(w.shape, F32)] * 4)(w, m, v, gathered)


def _rmsnorm_fwd(x, g, name, deps=()):
    s, d = x.shape
    tm = 256

    def body(x_ref, g_ref, h_ref, ht_ref):
        xf = x_ref[...]
        y = xf * lax.rsqrt(jnp.mean(xf * xf, axis=-1, keepdims=True) + RMS_EPS)
        h = y * g_ref[...]
        h_ref[...] = h.astype(BF)
        ht_ref[...] = h.T.astype(BF)

    return pl.pallas_call(
        _ignore_deps(body, 2, len(deps)), name=name, grid=(s // tm,),
        in_specs=[pl.BlockSpec((tm, d), lambda i: (i, 0)), pl.BlockSpec((1, d), lambda i: (0, 0))] + [TOKEN_SPEC] * len(deps),
        out_specs=[pl.BlockSpec((tm, d), lambda i: (i, 0)), pl.BlockSpec((d, tm), lambda i: (0, i))],
        out_shape=[_sds((s, d), BF), _sds((d, s), BF)], compiler_params=_cp("parallel"),
    )(x, g, *deps)


def _rmsnorm_bwd(x, g, dh, dres, out_scale, name):
    s, d = x.shape
    tm = 256

    def body(x_ref, g_ref, dh_ref, dres_ref, dx_ref, dxb_ref, dxbt_ref, dg_ref):
        xf = x_ref[...]
        r = lax.rsqrt(jnp.mean(xf * xf, axis=-1, keepdims=True) + RMS_EPS)
        xhat = xf * r
        dhv = dh_ref[...]
        dxhat = dhv * g_ref[...]
        dx = dres_ref[...] + r * (dxhat - xhat * jnp.mean(dxhat * xhat, axis=-1, keepdims=True))
        dx_ref[...] = dx
        scaled = dx * out_scale
        dxb_ref[...] = scaled.astype(BF)
        dxbt_ref[...] = scaled.T.astype(BF)

        @pl.when(pl.program_id(0) == 0)
        def _():
            dg_ref[...] = jnp.zeros_like(dg_ref)

        dg_ref[...] += jnp.sum(dhv * xhat, axis=0, keepdims=True)

    row = pl.BlockSpec((tm, d), lambda i: (i, 0))
    vec = pl.BlockSpec((1, d), lambda i: (0, 0))
    return pl.pallas_call(
        body, name=name, grid=(s // tm,),
        in_specs=[row, vec, row, row],
        out_specs=[row, row, pl.BlockSpec((d, tm), lambda i: (0, i)), vec],
        out_shape=[_sds((s, d), F32), _sds((s, d), BF), _sds((d, s), BF), _sds((1, d), F32)],
        compiler_params=_cp("arbitrary"),
    )(x, g, dh, dres)


def _loss_head(x, g, target, name):
    s, d = x.shape
    tm = 256

    def body(x_ref, g_ref, t_ref, dx_ref, dxb_ref, dxbt_ref, dg_ref, loss_ref):
        xf = x_ref[...]
        r = lax.rsqrt(jnp.mean(xf * xf, axis=-1, keepdims=True) + RMS_EPS)
        xhat = xf * r
        err = xhat * g_ref[...] - t_ref[...]
        dy = err * (1.0 / d)
        dxhat = dy * g_ref[...]
        dx = r * (dxhat - xhat * jnp.mean(dxhat * xhat, axis=-1, keepdims=True))
        dx_ref[...] = dx
        half = dx * 0.5
        dxb_ref[...] = half.astype(BF)
        dxbt_ref[...] = half.T.astype(BF)

        @pl.when(pl.program_id(0) == 0)
        def _():
            dg_ref[...] = jnp.zeros_like(dg_ref)
            loss_ref[...] = jnp.zeros_like(loss_ref)

        dg_ref[...] += jnp.sum(dy * xhat, axis=0, keepdims=True)
        part = 0.5 * jnp.sum(jnp.mean(err * err, axis=-1, keepdims=True), axis=0, keepdims=True)
        lane = lax.broadcasted_iota(jnp.int32, (1, 128), 1)
        loss_ref[...] += jnp.where(lane == 0, part, 0.0)

    row = pl.BlockSpec((tm, d), lambda i: (i, 0))
    vec = pl.BlockSpec((1, d), lambda i: (0, 0))
    return pl.pallas_call(
        body, name=name, grid=(s // tm,),
        in_specs=[row, vec, row],
        out_specs=[row, row, pl.BlockSpec((d, tm), lambda i: (0, i)), vec, pl.BlockSpec((1, 128), lambda i: (0, 0))],
        out_shape=[_sds((s, d), F32), _sds((s, d), BF), _sds((d, s), BF), _sds((1, d), F32), _sds((1, 128), F32)],
        compiler_params=_cp("arbitrary"),
    )(x, g, target)


def _act_spec(tm, n, natural, order):
    if natural:
        return pl.BlockSpec((tm, n), (lambda s, i: (i, s)) if order == "si" else (lambda i, s: (i, s)))
    return pl.BlockSpec((None, tm, n), (lambda s, i: (s, i, 0)) if order == "si" else (lambda i, s: (s, i, 0)))


def _act_shape(s, n, natural, dtype):
    return _sds((s, NDEV * n), dtype) if natural else _sds((NDEV, s, n), dtype)


def _ffn_up(h, wg, wu, name):
    s, d = h.shape
    n = wg.shape[2]
    tm = 1024

    def body(h_ref, wg_ref, wu_ref, g_ref, u_ref, a_ref, at_ref):
        hb = h_ref[...]
        g = _dot(hb, wg_ref[...])
        u = _dot(hb, wu_ref[...])
        sig = jax.nn.sigmoid(g)
        silu = g * sig
        g_ref[...] = (u * (sig * (1.0 + g * (1.0 - sig)))).astype(BF)
        u_ref[...] = silu.astype(BF)
        act = silu * u
        a_ref[...] = act.astype(BF)
        at_ref[...] = act.T.astype(BF)

    wsp = pl.BlockSpec((None, d, n), lambda s_, i: (s_, 0, 0))
    blk = _act_spec(tm, n, False, "si")
    return pl.pallas_call(
        body, name=name, grid=(NDEV, s // tm),
        in_specs=[pl.BlockSpec((tm, d), lambda s_, i: (i, 0)), wsp, wsp],
        out_specs=[blk] * 3 + [pl.BlockSpec((None, n, tm), lambda s_, i: (s_, 0, i))],
        out_shape=[_act_shape(s, n, False, BF)] * 3 + [_sds((NDEV, n, s), BF)],
        compiler_params=_cp("parallel", "parallel"),
    )(h, wg, wu)


def _ffn_down(act, wd, x, name):
    _, s, n = act.shape
    d = wd.shape[2]
    tm = 512

    def body(a_ref, w_ref, x_ref, o_ref, acc):
        k = pl.program_id(1)

        @pl.when(k == 0)
        def _():
            acc[...] = jnp.zeros_like(acc)

        acc[...] += _dot(a_ref[...], w_ref[...])

        @pl.when(k == NDEV - 1)
        def _():
            o_ref[...] = x_ref[...] + 0.5 * acc[...]

    row = pl.BlockSpec((tm, d), lambda i, k: (i, 0))
    return pl.pallas_call(
        body, name=name, grid=(s // tm, NDEV),
        in_specs=[_act_spec(tm, n, False, "is"), pl.BlockSpec((None, n, d), lambda i, k: (k, 0, 0)), row],
        out_specs=row, out_shape=_sds((s, d), F32),
        scratch_shapes=[pltpu.VMEM((tm, d), F32)], compiler_params=_cp("parallel", "arbitrary"),
    )(act, wd, x)


def _ffn_bwd_act(dyb, wd, g, u, name, deps=()):
    s, d = dyb.shape
    n = wd.shape[1]
    tm = 1024

    def body(dy_ref, w_ref, g_ref, u_ref, dg_ref, du_ref):
        dact = _dot_nt(dy_ref[...], w_ref[...])
        dg_ref[...] = (dact * g_ref[...].astype(F32)).astype(BF)
        du_ref[...] = (dact * u_ref[...].astype(F32)).astype(BF)

    blk = _act_spec(tm, n, False, "si")
    return pl.pallas_call(
        _ignore_deps(body, 4, len(deps)), name=name, grid=(NDEV, s // tm),
        in_specs=[pl.BlockSpec((tm, d), lambda s_, i: (i, 0)), pl.BlockSpec((None, n, d), lambda s_, i: (s_, 0, 0)), blk, blk]
        + [TOKEN_SPEC] * len(deps),
        out_specs=[blk, blk], out_shape=[_act_shape(s, n, False, BF)] * 2,
        compiler_params=_cp("parallel", "parallel"),
    )(dyb, wd, g, u, *deps)


def _grad_rows(act_t, dyb, name, deps=()):
    _, n, s = act_t.shape
    d = dyb.shape[1]
    tn = 2048

    def body(a_ref, dy_ref, o_ref):
        o_ref[...] = _dot(a_ref[...], dy_ref[...]).astype(BF)

    return pl.pallas_call(
        _ignore_deps(body, 2, len(deps)), name=name, grid=(NDEV, d // tn),
        in_specs=[pl.BlockSpec((None, n, s), lambda k, j: (k, 0, 0)), pl.BlockSpec((s, tn), lambda k, j: (0, j))]
        + [TOKEN_SPEC] * len(deps),
        out_specs=pl.BlockSpec((None, n, tn), lambda k, j: (k, 0, j)), out_shape=_sds((NDEV, n, d), BF),
        compiler_params=_cp("parallel", "parallel"),
    )(act_t, dyb, *deps)


def _grad_cols(ht, dxs, naturals, name, deps=()):
    d, s = ht.shape
    k = len(dxs)
    ns = [dx.shape[1] // NDEV if nat else dx.shape[2] for dx, nat in zip(dxs, naturals)]
    td = 1024

    def body(*refs):
        ht_ref, dx_refs, o_refs = refs[0], refs[1:1 + k], refs[1 + k:]
        hv = ht_ref[...]
        for dx_ref, o_ref in zip(dx_refs, o_refs):
            o_ref[...] = _dot(hv, dx_ref[...]).astype(BF)

    def dx_spec(n, nat):
        if nat:
            return pl.BlockSpec((s, n), lambda s_, j: (0, s_))
        return pl.BlockSpec((None, s, n), lambda s_, j: (s_, 0, 0))

    return pl.pallas_call(
        _ignore_deps(body, 1 + k, len(deps)), name=name, grid=(NDEV, d // td),
        in_specs=[pl.BlockSpec((td, s), lambda s_, j: (j, 0))] + [dx_spec(n, nat) for n, nat in zip(ns, naturals)]
        + [TOKEN_SPEC] * len(deps),
        out_specs=[pl.BlockSpec((None, td, n), lambda s_, j: (s_, j, 0)) for n in ns],
        out_shape=[_sds((NDEV, d, n), BF) for n in ns], compiler_params=_cp("parallel", "parallel"),
    )(ht, *dxs, *deps)


def _dh_cols(dxs, ws, naturals, name, deps=()):
    k = len(dxs)
    d = ws[0].shape[1]
    ns = [w.shape[2] for w in ws]
    s = dxs[0].shape[0] if naturals[0] else dxs[0].shape[1]
    tm = 512

    def body(*refs):
        dx_refs, w_refs, o_ref, acc = refs[:k], refs[k:2 * k], refs[2 * k], refs[2 * k + 1]
        j = pl.program_id(1)

        @pl.when(j == 0)
        def _():
            acc[...] = jnp.zeros_like(acc)

        t = _dot_nt(dx_refs[0][...], w_refs[0][...])
        for dx_ref, w_ref in zip(dx_refs[1:], w_refs[1:]):
            t = t + _dot_nt(dx_ref[...], w_ref[...])
        acc[...] += t

        @pl.when(j == NDEV - 1)
        def _():
            o_ref[...] = acc[...]

    return pl.pallas_call(
        _ignore_deps(body, 2 * k, len(deps)), name=name, grid=(s // tm, NDEV),
        in_specs=[_act_spec(tm, n, nat, "is") for n, nat in zip(ns, naturals)]
        + [pl.BlockSpec((None, d, n), lambda i, j: (j, 0, 0)) for n in ns] + [TOKEN_SPEC] * len(deps),
        out_specs=pl.BlockSpec((tm, d), lambda i, j: (i, 0)), out_shape=_sds((s, d), F32),
        scratch_shapes=[pltpu.VMEM((tm, d), F32)], compiler_params=_cp("parallel", "arbitrary"),
    )(*dxs, *ws, *deps)


def _mm_nn(a, b, tn, out_dtype, name, res=None, tm=1024, deps=()):
    m, k = a.shape
    nn = b.shape[1]

    def body(*refs):
        if res is None:
            a_ref, b_ref, o_ref = refs
            o_ref[...] = _dot(a_ref[...], b_ref[...]).astype(out_dtype)
        else:
            a_ref, b_ref, r_ref, o_ref = refs
            o_ref[...] = (r_ref[...] + _dot(a_ref[...], b_ref[...])).astype(out_dtype)

    osp = pl.BlockSpec((tm, tn), lambda j, i: (i, j))
    in_specs = [pl.BlockSpec((tm, k), lambda j, i: (i, 0)), pl.BlockSpec((k, tn), lambda j, i: (0, j))]
    args = [a, b]
    if res is not None:
        in_specs.append(osp)
        args.append(res)
    return pl.pallas_call(
        _ignore_deps(body, len(args), len(deps)), name=name, grid=(nn // tn, m // tm),
        in_specs=in_specs + [TOKEN_SPEC] * len(deps), out_specs=osp,
        out_shape=_sds((m, nn), out_dtype), compiler_params=_cp("parallel", "parallel"),
    )(*args, *deps)


def _mm_nt(pairs, name, out_dtype=F32, tm=512, tk=512, deps=()):
    m = pairs[0][0].shape[0]
    kk = pairs[0][1].shape[0]
    p = len(pairs)

    def body(*refs):
        o_ref = refs[2 * p]
        t = _dot_nt(refs[0][...], refs[1][...])
        for q in range(1, p):
            t = t + _dot_nt(refs[2 * q][...], refs[2 * q + 1][...])
        o_ref[...] = t.astype(out_dtype)

    in_specs, args = [], []
    for a, b in pairs:
        in_specs += [pl.BlockSpec((tm, a.shape[1]), lambda j, i: (i, 0)), pl.BlockSpec((tk, b.shape[1]), lambda j, i: (j, 0))]
        args += [a, b]
    return pl.pallas_call(
        _ignore_deps(body, 2 * p, len(deps)), name=name, grid=(kk // tk, m // tm), in_specs=in_specs + [TOKEN_SPEC] * len(deps),
        out_specs=pl.BlockSpec((tm, tk), lambda j, i: (i, j)), out_shape=_sds((m, kk), out_dtype),
        compiler_params=_cp("parallel", "parallel"),
    )(*args, *deps)


def _rope_tables(s, sign):
    half = ROPE_DIMS // 2
    f32 = np.float32
    freqs = f32(ROPE_THETA) ** (-np.arange(half, dtype=f32) / f32(half))
    ang = np.arange(s, dtype=f32)[:, None] * freqs[None, :]
    cos, sin = np.cos(ang).astype(f32), (sign * np.sin(ang)).astype(f32)
    one = np.ones((s, HD - ROPE_DIMS), f32)
    zero = np.zeros((s, HD - ROPE_DIMS), f32)
    zh = np.zeros((s, half), f32)
    c = np.concatenate([cos, cos, one], axis=1)
    sa = np.concatenate([-sin, zh, zero], axis=1)
    sb = np.concatenate([zh, sin, zero], axis=1)
    return jnp.asarray(c), jnp.asarray(sa), jnp.asarray(sb)


def _rope(xv, c, sa, sb):
    return xv * c + pltpu.roll(xv, HD - ROPE_DIMS // 2, 1) * sa + pltpu.roll(xv, ROPE_DIMS // 2, 1) * sb


def _qkv_rope(h, w, tables, name):
    s, d = h.shape
    n = w.shape[2]
    per = n // HD
    tm = 1024

    def body(h_ref, w_ref, c_ref, sa_ref, sb_ref, o_ref):
        shard = pl.program_id(0)
        y = _dot(h_ref[...], w_ref[...])
        c, sa, sb = c_ref[...], sa_ref[...], sb_ref[...]
        for j in range(per):
            blk = y[:, j * HD:(j + 1) * HD]
            rot = _rope(blk, c, sa, sb)
            is_qk = shard * per + j < 2 * N_HEADS
            o_ref[:, j * HD:(j + 1) * HD] = jnp.where(is_qk, rot, blk).astype(BF)

    tab = pl.BlockSpec((tm, HD), lambda s_, i: (i, 0))
    return pl.pallas_call(
        body, name=name, grid=(NDEV, s // tm),
        in_specs=[pl.BlockSpec((tm, d), lambda s_, i: (i, 0)), pl.BlockSpec((None, d, n), lambda s_, i: (s_, 0, 0)), tab, tab, tab],
        out_specs=pl.BlockSpec((tm, n), lambda s_, i: (i, s_)), out_shape=_sds((s, NDEV * n), BF),
        compiler_params=_cp("parallel", "parallel"),
    )(h, w, *tables)


def _iota2():
    return (lax.broadcasted_iota(jnp.int32, (QB, QB), 0), lax.broadcasted_iota(jnp.int32, (QB, QB), 1))


def _softplus(z):
    return jnp.maximum(z, 0.0) + jnp.log(1.0 + jnp.exp(-jnp.abs(z)))


def _tri_dot(xv, tri, left=False):
    hi = xv.astype(BF)
    r1 = xv - hi.astype(F32)
    mid = r1.astype(BF)
    lo = (r1 - mid.astype(F32)).astype(BF)
    if left:
        return _dot(tri, hi) + _dot(tri, mid) + _dot(tri, lo)
    return _dot(hi, tri) + _dot(mid, tri) + _dot(lo, tri)


def _col(ref_or_val):
    return ref_or_val[:, 0:1]


KT = 4 * QB
QQ = 4 * QB


def _iota_tile():
    return (lax.broadcasted_iota(jnp.int32, (QQ, KT), 0), lax.broadcasted_iota(jnp.int32, (QQ, KT), 1))


def _scan_matrix(keep):
    tri = keep(*_iota2()).astype(BF)
    return jnp.concatenate([tri, tri], axis=0)


def _scan_dot(xv, tri2):
    hi = xv.astype(BF)
    lo = (xv - hi.astype(F32)).astype(BF)
    return _dot(jnp.concatenate([hi, lo], axis=1), tri2)


def _blocks(xv):
    return [xv[:, b * QB:(b + 1) * QB] for b in range(KT // QB)]


def _sb_fwd(qkv, name):
    s = qkv.shape[0]
    nb = s // QB

    def body(q_ref, k_ref, v_ref, o_ref, ot_ref, t_ref):
        i = pl.program_id(1)
        q = q_ref[...]
        row, col = _iota_tile()
        later_keys = _scan_matrix(lambda j, s_: j > s_)
        last = (i * QQ + QQ - 1) // KT

        def step(tt, carry):
            acc, later = carry
            t = last - tt
            off = pl.multiple_of(t * KT, KT)
            k = k_ref[pl.ds(off, KT), :]
            v = v_ref[pl.ds(off, KT), :]
            z = _dot_nt(q, k) * SCALE
            strict = row + (i * QQ - t * KT) > col
            sp = _softplus(z)
            lnb = jnp.where(strict, -sp, 0.0)
            afters = []
            for xb in reversed(_blocks(lnb)):
                afters.append(later + _scan_dot(xb, later_keys))
                later = later + jnp.sum(xb, axis=1, keepdims=True)
            after = jnp.concatenate(afters[::-1], axis=1)
            w = jnp.where(strict, jnp.exp((z - sp) + after), 0.0)
            return acc + _dot(w.astype(BF), v), later

        acc, total = lax.fori_loop(0, last + 1, step, (jnp.zeros((QQ, HD), F32), jnp.zeros((QQ, 1), F32)))
        o_ref[...] = acc.astype(BF)
        ot_ref[...] = acc.T.astype(BF)
        t_ref[...] = jnp.broadcast_to(total, (QQ, HD))

    blk = pl.BlockSpec((QQ, HD), lambda h, i: (i, h))
    return pl.pallas_call(
        body, name=name, grid=(N_SB, s // QQ),
        in_specs=[blk, pl.BlockSpec((s, HD), lambda h, i: (0, N_HEADS + h)), pl.BlockSpec((s, HD), lambda h, i: (0, 2 * N_HEADS + h))],
        out_specs=[blk, pl.BlockSpec((HD, QQ), lambda h, i: (h, i)), blk],
        out_shape=[_sds((s, N_SB * HD), BF), _sds((N_SB * HD, s), BF), _sds((s, N_SB * HD), F32)],
        compiler_params=_cp("parallel", "parallel"),
    )(qkv, qkv, qkv)


def _sb_bwd(qkv, do, total, name):
    s = qkv.shape[0]
    nb = s // QB

    def body(q_ref, k_ref, v_ref, do_ref, t_ref, dq_ref, dk_ref, dv_ref, dk_acc, dv_acc):
        i = pl.program_id(1)

        @pl.when(i == 0)
        def _():
            dk_acc[...] = jnp.zeros_like(dk_acc)
            dv_acc[...] = jnp.zeros_like(dv_acc)

        q = q_ref[...]
        dov = do_ref[...]
        tot = _col(t_ref[...])
        row, col = _iota_tile()
        keys_upto = _scan_matrix(lambda j, s_: j <= s_)
        keys_before = _scan_matrix(lambda j, s_: j < s_)

        def step(t, carry):
            dq, lnb_before, dl_before = carry
            off = pl.multiple_of(t * KT, KT)
            k = k_ref[pl.ds(off, KT), :]
            v = v_ref[pl.ds(off, KT), :]
            z = _dot_nt(q, k) * SCALE
            strict = row + (i * QQ - t * KT) > col
            sp = _softplus(z)
            lnb = jnp.where(strict, -sp, 0.0)
            afters = []
            for xb in _blocks(lnb):
                afters.append(tot - (lnb_before + _scan_dot(xb, keys_upto)))
                lnb_before = lnb_before + jnp.sum(xb, axis=1, keepdims=True)
            a = jnp.where(strict, jnp.exp((z - sp) + jnp.concatenate(afters, axis=1)), 0.0)
            dl = a * _dot_nt(dov, v)
            befores = []
            for xb in _blocks(dl):
                befores.append(dl_before + _scan_dot(xb, keys_before))
                dl_before = dl_before + jnp.sum(xb, axis=1, keepdims=True)
            sig = jnp.exp(z - sp)
            dz = jnp.where(strict, dl * (1.0 - sig) - sig * jnp.concatenate(befores, axis=1), 0.0) * SCALE
            dq = dq + _dot(dz.astype(BF), k)
            dk_acc[pl.ds(off, KT), :] += _dot(dz.T.astype(BF), q)
            dv_acc[pl.ds(off, KT), :] += _dot(a.T.astype(BF), dov)
            return dq, lnb_before, dl_before

        zero = jnp.zeros((QQ, 1), F32)
        dq, _, _ = lax.fori_loop(0, (i * QQ + QQ - 1) // KT + 1, step, (jnp.zeros((QQ, HD), F32), zero, zero))
        dq_ref[...] = dq.astype(BF)

        @pl.when(i == s // QQ - 1)
        def _():
            dk_ref[...] = dk_acc[...].astype(BF)
            dv_ref[...] = dv_acc[...].astype(BF)

    blk = pl.BlockSpec((QQ, HD), lambda h, i: (i, h))
    full = pl.BlockSpec((s, HD), lambda h, i: (0, h))
    return pl.pallas_call(
        body, name=name, grid=(N_SB, s // QQ),
        in_specs=[blk, pl.BlockSpec((s, HD), lambda h, i: (0, N_HEADS + h)), pl.BlockSpec((s, HD), lambda h, i: (0, 2 * N_HEADS + h)), blk, blk],
        out_specs=[blk, full, full], out_shape=[_sds((s, N_SB * HD), BF)] * 3,
        scratch_shapes=[pltpu.VMEM((s, HD), F32), pltpu.VMEM((s, HD), F32)],
        compiler_params=_cp("parallel", "arbitrary"),
    )(qkv, qkv, qkv, do, total)


def _fgate_fwd(f, b, name):
    s = f.shape[0]
    nb = s // QB
    nfox = N_HEADS - N_SB

    def body(f_ref, b_ref, cb_ref, ct_ref):
        row, col = _iota2()
        upto = (row >= col).astype(BF)
        carry = jnp.zeros((1, HD), F32)
        for blk in range(nb):
            xv = f_ref[blk * QB:(blk + 1) * QB, :] + b_ref[...]
            logf = -_softplus(-xv)
            cum = _tri_dot(logf, upto, left=True) + carry
            carry = cum[QB - 1:QB, :]
            ct_ref[blk] = cum.T
            for h in range(nfox):
                cb_ref[blk * QB:(blk + 1) * QB, h * HD:(h + 1) * HD] = jnp.broadcast_to(cum[:, h:h + 1], (QB, HD))

    return pl.pallas_call(
        body, name=name, out_shape=[_sds((s, nfox * HD), F32), _sds((nb, HD, HD), F32)], compiler_params=_cp(),
    )(f, b)


def _fgate_bwd(dcq, dck, f, b, name):
    s = f.shape[0]
    nb = s // QB
    nfox = N_HEADS - N_SB

    def body(dcq_ref, dck_ref, f_ref, b_ref, df_ref, db_ref):
        row, col = _iota2()
        from_tri = (row <= col).astype(BF)
        lane = col
        carry = jnp.zeros((1, HD), F32)
        db = jnp.zeros((1, HD), F32)
        for blk in reversed(range(nb)):
            dcum = jnp.zeros((QB, HD), F32)
            for h in range(nfox):
                here = (slice(blk * QB, (blk + 1) * QB), slice(h * HD, (h + 1) * HD))
                dcum = jnp.where(lane == h, dcq_ref[here] - dck_ref[here], dcum)
            dlogf = _tri_dot(dcum, from_tri, left=True) + carry
            carry = dlogf[0:1, :]
            xv = f_ref[blk * QB:(blk + 1) * QB, :] + b_ref[...]
            sp = _softplus(xv)
            df = jnp.where(lane < nfox, dlogf * jnp.exp(-sp), 0.0)
            df_ref[blk * QB:(blk + 1) * QB, :] = df.astype(BF)
            db = db + jnp.sum(df, axis=0, keepdims=True)
        db_ref[...] = db

    return pl.pallas_call(
        body, name=name, out_shape=[_sds((s, HD), BF), _sds((1, HD), F32)], compiler_params=_cp(),
    )(dcq, dck, f, b)


def _fox_head_row(ct_ref, j, h):
    tile = ct_ref[j]
    sub = lax.broadcasted_iota(jnp.int32, tile.shape, 0)
    return jnp.sum(jnp.where(sub == h, tile, 0.0), axis=0, keepdims=True)


def _fox_tile_row(ct_ref, t, h):
    nsub = KT // QB
    return jnp.concatenate([_fox_head_row(ct_ref, t * nsub + b, h) for b in range(nsub)], axis=1)


def _fox_fwd(qkv, cum_b, cum_t, name):
    s = qkv.shape[0]
    nb = s // QB
    nfox = N_HEADS - N_SB

    def body(q_ref, k_ref, v_ref, cq_ref, ct_ref, o_ref, ot_ref, lse_ref):
        h, i = pl.program_id(0), pl.program_id(1)
        q = q_ref[...]
        cq = _col(cq_ref[...])
        row, col = _iota_tile()

        def step(t, carry):
            acc, m, l = carry
            off = pl.multiple_of(t * KT, KT)
            k = k_ref[pl.ds(off, KT), :]
            v = v_ref[pl.ds(off, KT), :]
            z = _dot_nt(q, k) * SCALE + cq - _fox_tile_row(ct_ref, t, h)
            z = jnp.where(row + (i * QQ - t * KT) >= col, z, NEG_INF)
            m_new = jnp.maximum(m, jnp.max(z, axis=1, keepdims=True))
            alpha = jnp.exp(m - m_new)
            p = jnp.exp(z - m_new)
            l = alpha * l + jnp.sum(p, axis=1, keepdims=True)
            acc = alpha * acc + _dot(p.astype(BF), v)
            return acc, m_new, l

        acc, m, l = lax.fori_loop(0, (i * QQ + QQ - 1) // KT + 1, step,
                                  (jnp.zeros((QQ, HD), F32), jnp.full((QQ, 1), NEG_INF, F32), jnp.zeros((QQ, 1), F32)))
        o = acc / l
        o_ref[...] = o.astype(BF)
        ot_ref[...] = o.T.astype(BF)
        lse_ref[...] = jnp.broadcast_to(m + jnp.log(l), (QQ, HD))

    blk = pl.BlockSpec((QQ, HD), lambda h, i: (i, h))
    return pl.pallas_call(
        body, name=name, grid=(nfox, s // QQ),
        in_specs=[pl.BlockSpec((QQ, HD), lambda h, i: (i, N_SB + h)),
                  pl.BlockSpec((s, HD), lambda h, i: (0, N_HEADS + N_SB + h)),
                  pl.BlockSpec((s, HD), lambda h, i: (0, 2 * N_HEADS + N_SB + h)),
                  blk, pl.BlockSpec((nb, 8, HD), lambda h, i: (0, 0, 0))],
        out_specs=[blk, pl.BlockSpec((HD, QQ), lambda h, i: (h, i)), blk],
        out_shape=[_sds((s, nfox * HD), BF), _sds((nfox * HD, s), BF), _sds((s, nfox * HD), F32)],
        compiler_params=_cp("parallel", "parallel"),
    )(qkv, qkv, qkv, cum_b, cum_t)


def _fox_bwd(qkv, cum_b, cum_t, o, lse, do, name):
    s = qkv.shape[0]
    nb = s // QB
    nfox = N_HEADS - N_SB

    def body(q_ref, k_ref, v_ref, cq_ref, ct_ref, o_ref, lse_ref, do_ref, dq_ref, dk_ref, dv_ref, dcq_ref, dc_ref, dk_acc, dv_acc, dc_acc):
        h, i = pl.program_id(0), pl.program_id(1)

        @pl.when(i == 0)
        def _():
            dk_acc[...] = jnp.zeros_like(dk_acc)
            dv_acc[...] = jnp.zeros_like(dv_acc)
            dc_acc[...] = jnp.zeros_like(dc_acc)

        q = q_ref[...]
        cq = _col(cq_ref[...])
        dov = do_ref[...]
        lse_c = _col(lse_ref[...])
        delta = jnp.sum(dov.astype(F32) * o_ref[...].astype(F32), axis=1, keepdims=True)
        row, col = _iota_tile()
        ones = jnp.ones((QQ, HD), BF)

        def step(t, carry):
            dq, over_keys = carry
            off = pl.multiple_of(t * KT, KT)
            k = k_ref[pl.ds(off, KT), :]
            v = v_ref[pl.ds(off, KT), :]
            z = _dot_nt(q, k) * SCALE + cq - _fox_tile_row(ct_ref, t, h)
            p = jnp.where(row + (i * QQ - t * KT) >= col, jnp.exp(z - lse_c), 0.0)
            dz = p * (_dot_nt(dov, v) - delta)
            dzt = dz.T
            dq = dq + _dot((dz * SCALE).astype(BF), k)
            dk_acc[pl.ds(off, KT), :] += _dot((dzt * SCALE).astype(BF), q)
            dv_acc[pl.ds(off, KT), :] += _dot(p.T.astype(BF), dov)
            dc_acc[pl.ds(off, KT), :] += _tri_dot(dzt, ones)
            return dq, over_keys + jnp.sum(dz, axis=1, keepdims=True)

        dq, over_keys = lax.fori_loop(0, (i * QQ + QQ - 1) // KT + 1, step, (jnp.zeros((QQ, HD), F32), jnp.zeros((QQ, 1), F32)))
        dq_ref[...] = dq.astype(BF)
        dcq_ref[...] = jnp.broadcast_to(over_keys, (QQ, HD))

        @pl.when(i == s // QQ - 1)
        def _():
            dk_ref[...] = dk_acc[...].astype(BF)
            dv_ref[...] = dv_acc[...].astype(BF)
            dc_ref[...] = dc_acc[...]

    blk = pl.BlockSpec((QQ, HD), lambda h, i: (i, h))
    full = pl.BlockSpec((s, HD), lambda h, i: (0, h))
    return pl.pallas_call(
        body, name=name, grid=(nfox, s // QQ),
        in_specs=[pl.BlockSpec((QQ, HD), lambda h, i: (i, N_SB + h)),
                  pl.BlockSpec((s, HD), lambda h, i: (0, N_HEADS + N_SB + h)),
                  pl.BlockSpec((s, HD), lambda h, i: (0, 2 * N_HEADS + N_SB + h)),
                  blk, pl.BlockSpec((nb, 8, HD), lambda h, i: (0, 0, 0)), blk, blk,
                  pl.BlockSpec((QQ, HD), lambda h, i: (i, N_SB + h))],
        out_specs=[blk, full, full, blk, full],
        out_shape=[_sds((s, nfox * HD), BF)] * 3 + [_sds((s, nfox * HD), F32)] * 2,
        scratch_shapes=[pltpu.VMEM((s, HD), F32)] * 3,
        compiler_params=_cp("parallel", "arbitrary"),
    )(qkv, qkv, qkv, cum_b, cum_t, o, lse, do)


GB = 8
DIL_PAD = QB * 16


def _dil_group(g, d, nb, off=0, shift=0):
    if nb >= GB:
        r, n0 = (g * GB) // nb, (g * GB) % nb
        start = off + r + (n0 + shift) * QB * d
        return [pl.ds(pl.multiple_of(start, QB), GB * QB)] if d == 1 else [pl.ds(start, GB * QB, stride=d)]
    per = GB // nb
    return [pl.ds(off + g * per + e + shift * QB * d, nb * QB, stride=d) for e in range(per)]


def _dil_load(ref, g, d, nb, off=0, shift=0):
    parts = [ref[sl, :] for sl in _dil_group(g, d, nb, off, shift)]
    rows = parts[0] if len(parts) == 1 else jnp.concatenate(parts, axis=0)
    return rows.reshape(GB, QB, HD)


def _dil_store(ref, g, d, nb, val, off=0, shift=0, add=False):
    rows = val.reshape(GB * QB, HD)
    slices = _dil_group(g, d, nb, off, shift)
    size = GB * QB // len(slices)
    for b, sl in enumerate(slices):
        piece = rows if len(slices) == 1 else rows[b * size:(b + 1) * size]
        if add:
            ref[sl, :] += piece
        else:
            ref[sl, :] = piece


def _bdot_nt(a, b):
    return lax.dot_general(a, b, (((2,), (2,)), ((0,), (0,))), preferred_element_type=F32)


def _bdot(a, b):
    return lax.dot_general(a, b, (((2,), (1,)), ((0,), (0,))), preferred_element_type=F32)


def _bdot_tn(a, b):
    return lax.dot_general(jnp.swapaxes(a, 1, 2).astype(BF), b, (((2,), (1,)), ((0,), (0,))), preferred_element_type=F32)


def _dil_masks(g, d, nb):
    row = lax.broadcasted_iota(jnp.int32, (GB, QB, QB), 1)
    col = lax.broadcasted_iota(jnp.int32, (GB, QB, QB), 2)
    blk = lax.broadcasted_iota(jnp.int32, (GB, QB, QB), 0)
    blk = blk + (g * GB) % nb if nb >= GB else blk % nb
    return col <= row, jnp.logical_and(col >= row, blk >= 1) if nb > 1 else None


def _dilated_fwd(qkv, name):
    s = qkv.shape[0]
    npat = len(DILATED_PATTERNS)
    chunk = 256

    def body(q_ref, k_ref, v_ref, out_ref, outt_ref, g_ref, qf, kf, vf, *per_pattern):
        o_s, l_s = per_pattern[:npat], per_pattern[npat:]
        qf[...] = q_ref[...].astype(F32)
        for dst, src in ((kf, k_ref), (vf, v_ref)):
            dst[0:DIL_PAD, :] = jnp.zeros((DIL_PAD, HD), F32)
            dst[DIL_PAD:, :] = src[...].astype(F32)
        for p, (_, d) in enumerate(DILATED_PATTERNS):
            nb = s // d // QB

            def grp(g, carry, p=p, d=d, nb=nb):
                mc, mp = _dil_masks(g, d, nb)
                q = _dil_load(qf, g, d, nb).astype(BF)
                zc = jnp.where(mc, _bdot_nt(q, _dil_load(kf, g, d, nb, DIL_PAD).astype(BF)) * SCALE, NEG_INF)
                m = jnp.max(zc, axis=2, keepdims=True)
                if nb > 1:
                    zp = jnp.where(mp, _bdot_nt(q, _dil_load(kf, g, d, nb, DIL_PAD, -1).astype(BF)) * SCALE, NEG_INF)
                    m = jnp.maximum(m, jnp.max(zp, axis=2, keepdims=True))
                ec = jnp.exp(zc - m)
                l = jnp.sum(ec, axis=2, keepdims=True)
                if nb > 1:
                    ep = jnp.where(mp, jnp.exp(zp - m), 0.0)
                    l = l + jnp.sum(ep, axis=2, keepdims=True)
                o = _bdot((ec / l).astype(BF), _dil_load(vf, g, d, nb, DIL_PAD).astype(BF))
                if nb > 1:
                    o = o + _bdot((ep / l).astype(BF), _dil_load(vf, g, d, nb, DIL_PAD, -1).astype(BF))
                _dil_store(o_s[p], g, d, nb, o)
                _dil_store(l_s[p], g, d, nb, jnp.broadcast_to(m + jnp.log(l), (GB, QB, HD)))
                return carry

            lax.fori_loop(0, s // (QB * GB), grp, 0)
        for c0 in range(0, s, chunk):
            rows = slice(c0, c0 + chunk)
            ls = [l_s[p][rows, :] for p in range(npat)]
            m = functools.reduce(jnp.maximum, ls)
            es = [jnp.exp(l - m) for l in ls]
            tot = functools.reduce(lambda a, b: a + b, es)
            out = functools.reduce(lambda a, b: a + b, [(e / tot) * o_s[p][rows, :] for p, e in enumerate(es)])
            out_ref[rows, :] = out.astype(BF)
            outt_ref[:, rows] = out.T.astype(BF)
            g_ref[rows, :] = m + jnp.log(tot)

    full = pl.BlockSpec((s, HD), lambda h: (0, h))
    return pl.pallas_call(
        body, name=name, grid=(N_HEADS,),
        in_specs=[full, pl.BlockSpec((s, HD), lambda h: (0, N_HEADS + h)), pl.BlockSpec((s, HD), lambda h: (0, 2 * N_HEADS + h))],
        out_specs=[full, pl.BlockSpec((HD, s), lambda h: (h, 0)), full],
        out_shape=[_sds((s, N_HEADS * HD), BF), _sds((N_HEADS * HD, s), BF), _sds((s, N_HEADS * HD), F32)],
        scratch_shapes=[pltpu.VMEM((s, HD), F32)] + [pltpu.VMEM((s + DIL_PAD, HD), F32)] * 2 + [pltpu.VMEM((s, HD), F32)] * (2 * npat),
        compiler_params=_cp("parallel"),
    )(qkv, qkv, qkv)


def _dilated_bwd(qkv, out, glse, do, tables, name):
    s = qkv.shape[0]
    chunk = 256

    def body(q_ref, k_ref, v_ref, out_ref, g_ref, do_ref, c_ref, sa_ref, sb_ref, dq_ref, dk_ref, dv_ref,
             qf, kf, vf, dof, dl_s, dq_a, dk_a, dv_a):
        qf[...] = q_ref[...].astype(F32)
        for dst, src in ((kf, k_ref), (vf, v_ref)):
            dst[0:DIL_PAD, :] = jnp.zeros((DIL_PAD, HD), F32)
            dst[DIL_PAD:, :] = src[...].astype(F32)
        for c0 in range(0, s, chunk):
            rows = slice(c0, c0 + chunk)
            dov = do_ref[rows, :].astype(F32)
            dof[rows, :] = dov
            dl_s[rows, :] = jnp.broadcast_to(jnp.sum(dov * out_ref[rows, :].astype(F32), axis=1, keepdims=True), (chunk, HD))
        dq_a[...] = jnp.zeros_like(dq_a)
        dk_a[...] = jnp.zeros_like(dk_a)
        dv_a[...] = jnp.zeros_like(dv_a)
        for _, d in DILATED_PATTERNS:
            nb = s // d // QB

            def grp(g, carry, d=d, nb=nb):
                mc, mp = _dil_masks(g, d, nb)
                q = _dil_load(qf, g, d, nb).astype(BF)
                kc = _dil_load(kf, g, d, nb, DIL_PAD).astype(BF)
                dov = _dil_load(dof, g, d, nb).astype(BF)
                lse = _dil_load(g_ref, g, d, nb)[:, :, 0:1]
                delta = _dil_load(dl_s, g, d, nb)[:, :, 0:1]
                pc = jnp.where(mc, jnp.exp(_bdot_nt(q, kc) * SCALE - lse), 0.0)
                dzc = pc * (_bdot_nt(dov, _dil_load(vf, g, d, nb, DIL_PAD).astype(BF)) - delta) * SCALE
                dq = _bdot(dzc.astype(BF), kc)
                if nb > 1:
                    kp = _dil_load(kf, g, d, nb, DIL_PAD, -1).astype(BF)
                    pp = jnp.where(mp, jnp.exp(_bdot_nt(q, kp) * SCALE - lse), 0.0)
                    dzp = pp * (_bdot_nt(dov, _dil_load(vf, g, d, nb, DIL_PAD, -1).astype(BF)) - delta) * SCALE
                    dq = dq + _bdot(dzp.astype(BF), kp)
                _dil_store(dq_a, g, d, nb, dq, add=True)
                _dil_store(dk_a, g, d, nb, _bdot_tn(dzc, q), DIL_PAD, add=True)
                _dil_store(dv_a, g, d, nb, _bdot_tn(pc, dov), DIL_PAD, add=True)
                if nb > 1:
                    _dil_store(dk_a, g, d, nb, _bdot_tn(dzp, q), DIL_PAD, -1, add=True)
                    _dil_store(dv_a, g, d, nb, _bdot_tn(pp, dov), DIL_PAD, -1, add=True)
                return carry

            lax.fori_loop(0, s // (QB * GB), grp, 0)
        for c0 in range(0, s, chunk):
            rows = slice(c0, c0 + chunk)
            padded = slice(DIL_PAD + c0, DIL_PAD + c0 + chunk)
            c, sa, sb = c_ref[rows, :], sa_ref[rows, :], sb_ref[rows, :]
            dq_ref[rows, :] = _rope(dq_a[rows, :], c, sa, sb).astype(BF)
            dk_ref[rows, :] = _rope(dk_a[padded, :], c, sa, sb).astype(BF)
            dv_ref[rows, :] = dv_a[padded, :].astype(BF)

    full = pl.BlockSpec((s, HD), lambda h: (0, h))
    tab = pl.BlockSpec((s, HD), lambda h: (0, 0))
    return pl.pallas_call(
        body, name=name, grid=(N_HEADS,),
        in_specs=[full, pl.BlockSpec((s, HD), lambda h: (0, N_HEADS + h)), pl.BlockSpec((s, HD), lambda h: (0, 2 * N_HEADS + h)),
                  full, full, full, tab, tab, tab],
        out_specs=[full, full, full], out_shape=[_sds((s, N_HEADS * HD), BF)] * 3,
        scratch_shapes=[pltpu.VMEM((s, HD), F32)] + [pltpu.VMEM((s + DIL_PAD, HD), F32)] * 2 + [pltpu.VMEM((s, HD), F32)] * 3
        + [pltpu.VMEM((s + DIL_PAD, HD), F32)] * 2,
        compiler_params=_cp("parallel"),
    )(qkv, qkv, qkv, out, glse, do, *tables)


def _swiglu_fwd(x, gnorm, w, tag, deps=()):
    h, ht = _rmsnorm_fwd(x, gnorm, f"norm_{tag}", deps)
    g, u, act, act_t = _ffn_up(h, w["gate"], w["up"], f"ffn_up_{tag}")
    if callable(w["down"]):
        w["down"] = w["down"](g)
    y = _ffn_down(act, w["down"], x, f"ffn_down_{tag}")
    return y, (x, ht, g, u, act_t)


def _swiglu_bwd(saved, gnorm, w, dy, dyb_half, out_scale, tag, deps=(), on_down=None, on_grads=None):
    x, ht, g, u, act_t = saved
    dg, du = _ffn_bwd_act(dyb_half, w["down"], g, u, f"ffn_bwd_act_{tag}", deps)
    if on_down:
        d_gate, d_up = _grad_cols(ht, [dg, du], [False, False], f"ffn_bwd_wgu_{tag}")
        tokens = list(on_grads({"gate": d_gate, "up": d_up}))
        d_down = _grad_rows(act_t, dyb_half, f"ffn_bwd_wd_{tag}", tokens)
        gw = {"gate": d_gate, "up": d_up, "down": d_down}
        tokens = list(on_down(d_down))
    else:
        d_down = _grad_rows(act_t, dyb_half, f"ffn_bwd_wd_{tag}")
        d_gate, d_up = _grad_cols(ht, [dg, du], [False, False], f"ffn_bwd_wgu_{tag}")
        gw = {"gate": d_gate, "up": d_up, "down": d_down}
        tokens = list(on_grads(gw)) if on_grads else []
    dh = _dh_cols([dg, du], [w["gate"], w["up"]], [False, False], f"ffn_bwd_dh_{tag}", tokens)
    dx, dxb, dxbt, dgn = _rmsnorm_bwd(x, gnorm, dh, dy, out_scale, f"norm_bwd_{tag}")
    return (dx, dxb, dxbt), dgn, gw


def kernel(x, norm_g, ffn1_w_gate, ffn1_w_up, ffn1_w_down, ffn2_w_gate, ffn2_w_up, ffn2_w_down, even_w_in, even_b_forget, even_w_out, odd_w_qkv, odd_w_out, final_norm_g, loss_target, m_norm_g, m_ffn1_w_gate, m_ffn1_w_up, m_ffn1_w_down, m_ffn2_w_gate, m_ffn2_w_up, m_ffn2_w_down, m_even_w_in, m_even_b_forget, m_even_w_out, m_odd_w_qkv, m_odd_w_out, m_final_norm_g, v_norm_g, v_ffn1_w_gate, v_ffn1_w_up, v_ffn1_w_down, v_ffn2_w_gate, v_ffn2_w_up, v_ffn2_w_down, v_even_w_in, v_even_b_forget, v_even_w_out, v_odd_w_qkv, v_odd_w_out, v_final_norm_g):
    s, d = x.shape[1], x.shape[2]
    nfox = N_HEADS - N_SB
    ax, ay, ac = lax.axis_index("x"), lax.axis_index("y"), lax.axis_index("c")
    me = 4 * ax + 2 * ay + ac
    slots = jnp.stack([4 * px + 2 * py + ac for px, py in [(ax, ay), (1 - ax, ay), (ax, 1 - ay), (1 - ax, 1 - ay)]]).astype(jnp.int32)
    x0 = x.reshape(s, d)
    target = loss_target.reshape(s, d)

    def bf(w):
        return w.astype(BF)

    groups = [
        [bf(ffn1_w_gate[0]), bf(ffn1_w_up[0]), norm_g.reshape(6, d // NDEV)],
        [bf(even_w_in[0]), bf(even_w_out[0])],
        [bf(ffn2_w_gate[0]), bf(ffn2_w_up[0]), bf(ffn2_w_down[0])],
        [bf(ffn1_w_gate[1]), bf(ffn1_w_up[1]), bf(ffn1_w_down[1])],
        [bf(odd_w_qkv[0]), bf(odd_w_out[0])],
        [bf(ffn2_w_gate[1]), bf(ffn2_w_up[1]), bf(ffn2_w_down[1])],
        [bf(ffn1_w_down[0])],
    ]
    started = [None] * len(groups)
    last_token = []
    for k in (0, 6, 1, 2, 3, 4, 5):
        started[k] = _gather_start(groups[k], me, last_token, f"gather_start_{k}")
        last_token = [started[k]["token"]]
    all_started = last_token

    def forward_early(k, after):
        started[k] = _gather_forward(started[k], after, f"gather_forward_{k}")
        return [started[k]["token"]]

    def gathered(k, after):
        st = started[k] if "send2" in started[k] else _gather_forward(started[k], after, f"gather_forward_{k}")
        return _gather_finish(st, after, f"gather_finish_{k}")

    def ffn_weights(ws_):
        return {"gate": ws_[0], "up": ws_[1], "down": ws_[2]}

    b_pad = jnp.pad(even_b_forget, ((0, 0), (0, HD - nfox)))
    gfin = final_norm_g.reshape(1, d)

    g0 = gathered(0, x0)
    gn = jnp.transpose(g0[2], (1, 0, 2)).reshape(6, 1, d)
    wf = [[{"gate": g0[0], "up": g0[1], "down": lambda after: gathered(6, after)[0]}, None], [None, None]]
    x1, sv_f1_0 = _swiglu_fwd(x0, gn[0], wf[0][0], "l0a", all_started)
    g1 = gathered(1, x1)
    w_in_nat = jnp.transpose(g1[0], (1, 0, 2)).reshape(d, -1)
    w_qkv_e = w_in_nat[:, :3 * d]
    w_f = jnp.pad(w_in_nat[:, 3 * d:], ((0, 0), (0, HD - nfox)))
    w_out_e = g1[1].reshape(d, d)
    h_e, ht_e = _rmsnorm_fwd(x1, gn[1], "norm_l0m")
    qkv_e = _mm_nn(h_e, w_qkv_e, 768, BF, "even_qkv")
    f_e = _mm_nn(h_e, w_f, HD, F32, "even_fgate")
    o_sb, ot_sb, tot_sb = _sb_fwd(qkv_e, "sb_fwd")
    cum_b, cum_t = _fgate_fwd(f_e, b_pad, "fgate_fwd")
    o_fox, ot_fox, lse_fox = _fox_fwd(qkv_e, cum_b, cum_t, "fox_fwd")
    o_e = jnp.concatenate([o_sb, o_fox], axis=1)
    ot_e = jnp.concatenate([ot_sb, ot_fox], axis=0)
    x2 = _mm_nn(o_e, w_out_e, 1024, F32, "even_out", res=x1, deps=forward_early(2, o_e))
    wf[0][1] = ffn_weights(gathered(2, x2))
    x3, sv_f2_0 = _swiglu_fwd(x2, gn[2], wf[0][1], "l0b")

    wf[1][0] = ffn_weights(gathered(3, x3))
    x4, sv_f1_1 = _swiglu_fwd(x3, gn[3], wf[1][0], "l1a")
    g4 = gathered(4, x4)
    w_qkv_o = g4[0]
    w_out_o = g4[1].reshape(d, d)
    h_o, ht_o = _rmsnorm_fwd(x4, gn[4], "norm_l1m")
    qkv_o = _qkv_rope(h_o, w_qkv_o, _rope_tables(s, 1.0), "odd_qkv")
    o_o, ot_o, glse = _dilated_fwd(qkv_o, "dilated_fwd")
    x5 = _mm_nn(o_o, w_out_o, 1024, F32, "odd_out", res=x4)
    wf[1][1] = ffn_weights(gathered(5, x5))
    x6, sv_f2_1 = _swiglu_fwd(x5, gn[5], wf[1][1], "l1b")

    def chip_sums(gs, a_s, tag):
        ps = _pair_sum(gs, a_s, slots, f"pair_sum_{tag}")
        return gs, a_s, _chip_start(ps, f"chip_start_{tag}")

    def as_slices(gs):
        return [g_ if g_.ndim == 3 else g_.reshape(NDEV, g_.shape[0] // NDEV, g_.shape[1]) for g_ in gs]

    def reduce_start(gs, tag):
        gs = as_slices(gs)
        return chip_sums(gs, _pair_exchange(gs, f"pair_exchange_{tag}"), tag)

    red, crossing = {}, {}

    def cross(gs, tag):
        crossing[tag] = _pair_start(as_slices(gs), f"pair_start_{tag}")
        return [crossing[tag]["token"]]

    def reduce_behind_dh(tag):
        return lambda gw: cross([gw["gate"], gw["up"], gw["down"]], tag)

    def reduce_after(tag, after):
        red[tag] = chip_sums(*_pair_finish(crossing[tag], after, f"pair_finish_{tag}"), tag)
        return [red[tag][2]["token"]]

    def reduce_now(tag, names):
        def hook(gw):
            red[tag] = reduce_start([gw[nm] for nm in names] if names else [gw], tag)
            return [red[tag][2]["token"]]
        return hook

    dx6, dx6b, _, d_gfin, loss_part = _loss_head(x6, gfin, target, "loss_head")

    (dx5, dx5b, dx5bt), dgn5, _ = _swiglu_bwd(sv_f2_1, gn[5], wf[1][1], dx6, dx6b, 1.0, "l1b", on_grads=reduce_behind_dh("l1b"))
    d_wout_o = _mm_nn(ot_o, dx5b, 1024, BF, "odd_out_dw")
    do_o = _mm_nt([(dx5b, w_out_o)], "odd_out_do", BF, deps=reduce_after("l1b", dx5))
    dqkv_o = jnp.concatenate(_dilated_bwd(qkv_o, o_o, glse, do_o, _rope_tables(s, -1.0), "dilated_bwd"), axis=1)
    (d_wqkv_o,) = _grad_cols(ht_o, [dqkv_o], [True], "odd_qkv_dw")
    dh_o = _dh_cols([dqkv_o], [w_qkv_o], [True], "odd_qkv_dh", cross([d_wqkv_o, d_wout_o], "l1m"))
    dx4, dx4b, _, dgn4 = _rmsnorm_bwd(x4, gn[4], dh_o, dx5, 0.5, "norm_bwd_l1m")
    (dx3, dx3b, _), dgn3, _ = _swiglu_bwd(sv_f1_1, gn[3], wf[1][0], dx4, dx4b, 0.5, "l1a", reduce_after("l1m", dx4),
                                         on_grads=reduce_behind_dh("l1a"))

    (dx2, dx2b, dx2bt), dgn2, _ = _swiglu_bwd(sv_f2_0, gn[2], wf[0][1], dx3, dx3b, 1.0, "l0b", reduce_after("l1a", dx3),
                                             on_grads=reduce_behind_dh("l0b"))
    d_wout_e = _mm_nn(ot_e, dx2b, 1024, BF, "even_out_dw")
    do_e = _mm_nt([(dx2b, w_out_e)], "even_out_do", BF, deps=reduce_after("l0b", dx2))
    dq_sb, dk_sb, dv_sb = _sb_bwd(qkv_e, do_e, tot_sb, "sb_bwd")
    dq_fx, dk_fx, dv_fx, dcq, dck = _fox_bwd(qkv_e, cum_b, cum_t, o_fox, lse_fox, do_e, "fox_bwd")
    df, db_part = _fgate_bwd(dcq, dck, f_e, b_pad, "fgate_bwd")
    dqkv_e = jnp.concatenate([dq_sb, dq_fx, dk_sb, dk_fx, dv_sb, dv_fx], axis=1)
    d_wqkv_e = _mm_nn(ht_e, dqkv_e, 768, BF, "even_qkv_dw")
    d_wf = _mm_nn(ht_e, df, HD, BF, "even_fgate_dw")
    d_win_nat = jnp.concatenate([d_wqkv_e, d_wf[:, :nfox]], axis=1)
    d_win = jnp.transpose(d_win_nat.reshape(d, NDEV, -1), (1, 0, 2))
    dh_e = _mm_nt([(dqkv_e, w_qkv_e), (df, w_f)], "even_in_dh", deps=cross([d_win, d_wout_e], "l0m"))
    dx1, dx1b, _, dgn1 = _rmsnorm_bwd(x1, gn[1], dh_e, dx2, 0.5, "norm_bwd_l0m")
    (dx0, _, _), dgn0, _ = _swiglu_bwd(sv_f1_0, gn[0], wf[0][0], dx1, dx1b, 1.0, "l0a", reduce_after("l0m", dx1),
                                      on_down=reduce_now("l0a_down", None), on_grads=reduce_now("l0a_gu", ["gate", "up"]))

    def reduce_finish(red, tag, after):
        gs, a_s, st = red
        return list(zip(gs, a_s, _chip_finish(st, after, f"chip_finish_{tag}")))

    f_l1b, f_l1m, f_l1a = (reduce_finish(red[t], t, dx0) for t in ("l1b", "l1m", "l1a"))
    f_l0b, f_l0m = (reduce_finish(red[t], t, dx0) for t in ("l0b", "l0m"))

    def update(w_, m_, v_, parts, nm):
        if w_.shape[2] % 128 == 0:
            return _adamw_sharded(w_, m_, v_, parts, slots, f"adamw_{nm}")
        outs = _adamw_sharded(jnp.swapaxes(w_, 1, 2), jnp.swapaxes(m_, 1, 2), jnp.swapaxes(v_, 1, 2), parts, slots,
                              f"adamw_{nm}", transposed=True)
        return [jnp.swapaxes(o, 1, 2) for o in outs]

    res = {}
    res["even_w_in"] = update(even_w_in, m_even_w_in, v_even_w_in, [f_l0m[0]], "even_w_in")
    res["even_w_out"] = _adamw_sharded(even_w_out, m_even_w_out, v_even_w_out, [f_l0m[1]], slots, "adamw_even_w_out")
    res["odd_w_qkv"] = _adamw_sharded(odd_w_qkv, m_odd_w_qkv, v_odd_w_qkv, [f_l1m[0]], slots, "adamw_odd_w_qkv")
    res["odd_w_out"] = _adamw_sharded(odd_w_out, m_odd_w_out, v_odd_w_out, [f_l1m[1]], slots, "adamw_odd_w_out")
    names = ["ffn2_w_gate", "ffn2_w_up", "ffn2_w_down", "ffn1_w_gate", "ffn1_w_up", "ffn1_w_down"]
    ws = [ffn2_w_gate, ffn2_w_up, ffn2_w_down, ffn1_w_gate, ffn1_w_up, ffn1_w_down]
    ms = [m_ffn2_w_gate, m_ffn2_w_up, m_ffn2_w_down, m_ffn1_w_gate, m_ffn1_w_up, m_ffn1_w_down]
    vs = [v_ffn2_w_gate, v_ffn2_w_up, v_ffn2_w_down, v_ffn1_w_gate, v_ffn1_w_up, v_ffn1_w_down]
    for k in range(3):
        res[names[k]] = update(ws[k], ms[k], vs[k], [f_l0b[k], f_l1b[k]], names[k])
    f_l0a = (reduce_finish(red["l0a_gu"], "l0a_gu", res["ffn2_w_down"][1])
             + reduce_finish(red["l0a_down"], "l0a_down", res["ffn2_w_down"][1]))
    for k in range(3, 6):
        res[names[k]] = update(ws[k], ms[k], vs[k], [f_l0a[k - 3], f_l1a[k - 3]], names[k])

    dnorm = jnp.concatenate([dgn0, dgn1, dgn2, dgn3, dgn4, dgn5], axis=0)
    nsm = d // NDEV
    small_rows = (6 * d + d + 2 * HD) // HD
    pad_rows = -small_rows % 8
    part = jnp.concatenate([dnorm.reshape(-1), d_gfin.reshape(-1), db_part.reshape(-1), loss_part.reshape(-1),
                            jnp.zeros((pad_rows * HD,), F32)]).reshape(small_rows + pad_rows, HD)
    (gathered,) = _all_gather([part], "gather_small")

    def pack(ng, bfg, fg):
        full = lax.dynamic_update_slice(jnp.zeros((6, d), F32), ng.reshape(6, nsm), (0, me * nsm))
        return jnp.concatenate([full.reshape(-1), fg.reshape(-1), jnp.pad(bfg.reshape(-1), (0, HD - nfox)),
                                jnp.zeros((HD + pad_rows * HD,), F32)]).reshape(small_rows + pad_rows, HD)

    sm = _adamw_small(pack(norm_g, even_b_forget, final_norm_g), pack(m_norm_g, m_even_b_forget, m_final_norm_g),
                      pack(v_norm_g, v_even_b_forget, v_final_norm_g), gathered, "adamw_small")

    def unpack(t):
        flat = t.reshape(-1)
        ng = lax.dynamic_slice(flat[:6 * d].reshape(6, d), (0, me * nsm), (6, nsm)).reshape(norm_g.shape)
        fg = flat[6 * d:7 * d].reshape(final_norm_g.shape)
        bfg = flat[7 * d:7 * d + nfox].reshape(even_b_forget.shape)
        return ng, bfg, fg

    sm_g, sm_d, sm_m, sm_v = [unpack(t) for t in sm]
    loss = sm[0].reshape(-1)[7 * d + HD]

    order = ["norm_g", "ffn1_w_gate", "ffn1_w_up", "ffn1_w_down", "ffn2_w_gate", "ffn2_w_up", "ffn2_w_down", "even_w_in",
             "even_b_forget", "even_w_out", "odd_w_qkv", "odd_w_out", "final_norm_g"]
    outs = [loss, dx0.reshape(x.shape)]
    for k in range(4):
        smk = [sm_g, sm_d, sm_m, sm_v][k]
        for nm in order:
            if nm == "norm_g":
                outs.append(smk[0])
            elif nm == "even_b_forget":
                outs.append(smk[1])
            elif nm == "final_norm_g":
                outs.append(smk[2])
            else:
                outs.append(res[nm][k])
    return tuple(outs)
```

```python
import functools

import jax
import jax.numpy as jnp
import numpy as np
from jax import lax
from jax.experimental import pallas as pl
from jax.experimental.pallas import tpu as pltpu

F32 = jnp.float32
BF = jnp.bfloat16
NDEV = 8
HD = 128
QB = 128
N_HEADS = 16
N_SB = 8
SCALE = HD ** -0.5
ROPE_THETA = 500000.0
ROPE_DIMS = HD // 4
DILATED_PATTERNS = ((128, 1), (512, 4), (2048, 16))
RMS_EPS = 1e-6
NEG_INF = -1e30
ADAM_LR = 0.001
ADAM_B1 = 0.9
ADAM_B2 = 0.999
ADAM_EPS = 1e-08
ADAM_WD = 0.01
ADAM_STEP = 10
VMEM_LIMIT_V7X = 56 * 1024 * 1024
MESH = pl.DeviceIdType.MESH
ANY = pl.BlockSpec(memory_space=pl.ANY)

NT_DIMS = (((1,), (1,)), ((), ()))


def _cp(*dims):
    return pltpu.CompilerParams(dimension_semantics=dims if dims else None, vmem_limit_bytes=VMEM_LIMIT_V7X)


def _dot(a, b):
    return jnp.dot(a, b, preferred_element_type=F32)


def _dot_nt(a, b):
    return lax.dot_general(a, b, NT_DIMS, preferred_element_type=F32)


def _sds(shape, dtype):
    return jax.ShapeDtypeStruct(shape, dtype)


def _place():
    x, y, c = lax.axis_index("x"), lax.axis_index("y"), lax.axis_index("c")
    chips = [(x, y), (1 - x, y), (x, 1 - y), (1 - x, 1 - y)]
    return x, y, c, chips


def _all_gather(xs, name):
    n = len(xs)

    def body(*refs):
        x_refs, out_refs = refs[:n], refs[n:2 * n]
        send_sems, recv_sems, local_sems = refs[2 * n:]
        x, y, c, chips = _place()
        me, sibling = (x, y, c), (x, y, 1 - c)
        others = chips[1:]

        def slot(a, px, py, pc):
            return out_refs[a].at[4 * px + 2 * py + pc]

        def copy(a, k, block, to, src=None):
            return pltpu.make_async_remote_copy(
                src_ref=slot(a, *block) if src is None else src, dst_ref=slot(a, *block),
                send_sem=send_sems.at[a, k], recv_sem=recv_sems.at[a, k], device_id=to, device_id_type=MESH)

        started = []
        for a in range(n):
            mine = pltpu.make_async_copy(x_refs[a], slot(a, *me), local_sems.at[a])
            mine.start()
            first = [copy(a, 0, me, sibling, src=x_refs[a])]
            first += [copy(a, 1 + j, me, (*chip, c), src=x_refs[a]) for j, chip in enumerate(others)]
            for cp in first:
                cp.start()
            started += [mine.wait] + [cp.wait_send for cp in first]
        for a in range(n):
            for j, chip in enumerate(others):
                copy(a, 1 + j, (*chip, c), me).wait_recv()
                passed = copy(a, 4 + j, (*chip, c), sibling)
                passed.start()
                started.append(passed.wait_send)
        for a in range(n):
            copy(a, 0, sibling, me).wait_recv()
            for j, chip in enumerate(others):
                copy(a, 4 + j, (*chip, 1 - c), me).wait_recv()
        for w in started:
            w()

    return pl.pallas_call(
        body, name=name,
        out_shape=[_sds((NDEV,) + x.shape, x.dtype) for x in xs],
        in_specs=[ANY] * n, out_specs=[ANY] * n,
        scratch_shapes=[pltpu.SemaphoreType.DMA((n, 7)), pltpu.SemaphoreType.DMA((n, 7)), pltpu.SemaphoreType.DMA((n,))],
    )(*xs)


def _pair_exchange(gs, name):
    n = len(gs)

    def body(*refs):
        g_refs, a_refs = refs[:n], refs[n:2 * n]
        send_sems, recv_sems = refs[2 * n:]
        x, y, c, chips = _place()
        copies = []
        for a in range(n):
            for j, (px, py) in enumerate(chips):
                copies.append(pltpu.make_async_remote_copy(
                    src_ref=g_refs[a].at[4 * px + 2 * py + (1 - c)], dst_ref=a_refs[a].at[j],
                    send_sem=send_sems.at[a, j], recv_sem=recv_sems.at[a, j],
                    device_id=(x, y, 1 - c), device_id_type=MESH))
        for cp in copies:
            cp.start()
        for cp in copies:
            cp.wait()

    return pl.pallas_call(
        body, name=name,
        out_shape=[_sds((4,) + g.shape[1:], g.dtype) for g in gs],
        in_specs=[ANY] * n, out_specs=[ANY] * n,
        scratch_shapes=[pltpu.SemaphoreType.DMA((n, 4)), pltpu.SemaphoreType.DMA((n, 4))],
    )(*gs)


HBM = pl.BlockSpec(memory_space=pltpu.HBM)
SEM = pl.BlockSpec(memory_space=pltpu.SEMAPHORE)
EFFECT = pltpu.SideEffectType.DATAFLOW_SIDE_EFFECTING
TOKEN = _sds((8, 128), F32)
TOKEN_SPEC = pl.BlockSpec((8, 128), lambda *_: (0, 0))


def _in_hbm(x):
    return pltpu.with_memory_space_constraint(x, pltpu.HBM)


def _ignore_deps(body, n_in, n_deps):
    if not n_deps:
        return body
    return lambda *refs: body(*refs[:n_in], *refs[n_in + n_deps:])


def _slot_of(px, py, pc):
    return 4 * px + 2 * py + pc


def _gather_start(xs, me, deps, name):
    n = len(xs)
    lands = [lax.dynamic_update_slice(lax.empty((NDEV,) + x.shape, x.dtype), x[None], (me,) + (0,) * x.ndim) for x in xs]

    def body(*refs):
        x_refs, land_refs = refs[:n], refs[n:2 * n]
        send, recv_ici, recv_sib = refs[2 * n:2 * n + 3]
        token = refs[4 * n + 3]
        x, y, c, chips = _place()
        for a in range(n):
            dst = land_refs[a].at[_slot_of(x, y, c)]
            pltpu.make_async_remote_copy(src_ref=x_refs[a], dst_ref=dst, send_sem=send.at[4 * a], recv_sem=recv_sib.at[a],
                                         device_id=(x, y, 1 - c), device_id_type=MESH).start()
            for j, chip in enumerate(chips[1:]):
                pltpu.make_async_remote_copy(src_ref=x_refs[a], dst_ref=dst, send_sem=send.at[4 * a + 1 + j], recv_sem=recv_ici.at[3 * a + j],
                                             device_id=(*chip, c), device_id_type=MESH).start()
        token[...] = jnp.zeros_like(token)

    outs = pl.pallas_call(
        _ignore_deps(body, 2 * n, len(deps)), name=name,
        out_shape=(pltpu.SemaphoreType.DMA((4 * n,)), pltpu.SemaphoreType.DMA((3 * n,)), pltpu.SemaphoreType.DMA((n,)),
                   *[pltpu.HBM(x.shape, x.dtype) for x in xs], *[pltpu.HBM(l.shape, l.dtype) for l in lands], TOKEN),
        in_specs=[HBM] * (2 * n) + [TOKEN_SPEC] * len(deps),
        out_specs=(SEM, SEM, SEM, *[HBM] * (2 * n), pl.BlockSpec(memory_space=pltpu.VMEM)),
        input_output_aliases={a: 3 + a for a in range(2 * n)},
        compiler_params=pltpu.CompilerParams(has_side_effects=EFFECT),
    )(*[_in_hbm(x) for x in xs], *[_in_hbm(l) for l in lands], *deps)
    send, recv_ici, recv_sib = outs[:3]
    return dict(send=send, recv_ici=recv_ici, recv_sib=recv_sib, xs=list(outs[3:3 + n]), lands=list(outs[3 + n:3 + 2 * n]), token=outs[-1])


def _gather_forward(st, after, name):
    n = len(st["lands"])

    def body(*refs):
        land_refs, recv_ici = refs[:n], refs[n]
        send2, recv2, token = refs[n + 2], refs[n + 3], refs[2 * n + 4]
        x, y, c, chips = _place()
        for a in range(n):
            for j, chip in enumerate(chips[1:]):
                blk = land_refs[a].at[_slot_of(*chip, c)]
                pltpu.make_async_remote_copy(src_ref=blk, dst_ref=blk, send_sem=send2.at[3 * a + j], recv_sem=recv_ici.at[3 * a + j],
                                             device_id=(*chip, c), device_id_type=MESH).wait_recv()
                pltpu.make_async_remote_copy(src_ref=blk, dst_ref=blk, send_sem=send2.at[3 * a + j], recv_sem=recv2.at[3 * a + j],
                                             device_id=(x, y, 1 - c), device_id_type=MESH).start()
        token[...] = jnp.zeros_like(token)

    outs = pl.pallas_call(
        body, name=name,
        out_shape=(pltpu.SemaphoreType.DMA((3 * n,)), pltpu.SemaphoreType.DMA((3 * n,)), *[pltpu.HBM(l.shape, l.dtype) for l in st["lands"]], TOKEN),
        in_specs=[HBM] * n + [SEM, pl.BlockSpec(memory_space=pl.ANY)],
        out_specs=(SEM, SEM, *[HBM] * n, pl.BlockSpec(memory_space=pltpu.VMEM)),
        input_output_aliases={a: 2 + a for a in range(n)},
        compiler_params=pltpu.CompilerParams(has_side_effects=EFFECT),
    )(*st["lands"], st["recv_ici"], after)
    return dict(st, send2=outs[0], recv2=outs[1], lands=list(outs[2:2 + n]), token=outs[-1])


def _gather_finish(st, after, name):
    n = len(st["lands"])

    def body(*refs):
        x_refs, land_refs = refs[:n], refs[n:2 * n]
        send, recv_sib, send2, recv2 = refs[2 * n:2 * n + 4]
        x, y, c, chips = _place()
        for a in range(n):
            mine = land_refs[a].at[_slot_of(x, y, c)]
            theirs = land_refs[a].at[_slot_of(x, y, 1 - c)]
            for k in range(4):
                pltpu.make_async_remote_copy(src_ref=x_refs[a], dst_ref=mine, send_sem=send.at[4 * a + k], recv_sem=recv_sib.at[a],
                                             device_id=(x, y, 1 - c), device_id_type=MESH).wait_send()
            pltpu.make_async_remote_copy(src_ref=x_refs[a], dst_ref=theirs, send_sem=send.at[4 * a], recv_sem=recv_sib.at[a],
                                         device_id=(x, y, 1 - c), device_id_type=MESH).wait_recv()
            for j, chip in enumerate(chips[1:]):
                sent = land_refs[a].at[_slot_of(*chip, c)]
                got = land_refs[a].at[_slot_of(*chip, 1 - c)]
                pltpu.make_async_remote_copy(src_ref=sent, dst_ref=sent, send_sem=send2.at[3 * a + j], recv_sem=recv2.at[3 * a + j],
                                             device_id=(x, y, 1 - c), device_id_type=MESH).wait_send()
                pltpu.make_async_remote_copy(src_ref=got, dst_ref=got, send_sem=send2.at[3 * a + j], recv_sem=recv2.at[3 * a + j],
                                             device_id=(x, y, 1 - c), device_id_type=MESH).wait_recv()

    outs = pl.pallas_call(
        body, name=name,
        out_shape=tuple(pltpu.HBM(v.shape, v.dtype) for v in st["xs"] + st["lands"]),
        in_specs=[HBM] * (2 * n) + [SEM] * 4 + [pl.BlockSpec(memory_space=pl.ANY)], out_specs=tuple([HBM] * (2 * n)),
        input_output_aliases={a: a for a in range(2 * n)},
        compiler_params=pltpu.CompilerParams(has_side_effects=EFFECT),
    )(*st["xs"], *st["lands"], st["send"], st["recv_sib"], st["send2"], st["recv2"], after)
    return list(outs[n:])


def _pair_start(gs, name):
    n = len(gs)
    lands = [lax.empty((4,) + g.shape[1:], g.dtype) for g in gs]

    def body(*refs):
        g_refs, a_refs = refs[:n], refs[n:2 * n]
        send, recv = refs[2 * n], refs[2 * n + 1]
        token = refs[4 * n + 2]
        x, y, c, chips = _place()
        for a in range(n):
            for j, (px, py) in enumerate(chips):
                pltpu.make_async_remote_copy(src_ref=g_refs[a].at[_slot_of(px, py, 1 - c)], dst_ref=a_refs[a].at[j],
                                             send_sem=send.at[4 * a + j], recv_sem=recv.at[4 * a + j],
                                             device_id=(x, y, 1 - c), device_id_type=MESH).start()
        token[...] = jnp.zeros_like(token)

    outs = pl.pallas_call(
        body, name=name,
        out_shape=(pltpu.SemaphoreType.DMA((4 * n,)), pltpu.SemaphoreType.DMA((4 * n,)),
                   *[pltpu.HBM(g.shape, g.dtype) for g in gs], *[pltpu.HBM(l.shape, l.dtype) for l in lands], TOKEN),
        in_specs=[HBM] * (2 * n), out_specs=(SEM, SEM, *[HBM] * (2 * n), pl.BlockSpec(memory_space=pltpu.VMEM)),
        input_output_aliases={a: 2 + a for a in range(2 * n)},
        compiler_params=pltpu.CompilerParams(has_side_effects=EFFECT),
    )(*[_in_hbm(g) for g in gs], *[_in_hbm(l) for l in lands])
    return dict(send=outs[0], recv=outs[1], gs=list(outs[2:2 + n]), lands=list(outs[2 + n:2 + 2 * n]), token=outs[-1])


def _pair_finish(st, after, name):
    n = len(st["gs"])

    def body(*refs):
        g_refs, a_refs = refs[:n], refs[n:2 * n]
        send, recv = refs[2 * n], refs[2 * n + 1]
        x, y, c, chips = _place()
        for a in range(n):
            for j, (px, py) in enumerate(chips):
                cp = pltpu.make_async_remote_copy(src_ref=g_refs[a].at[_slot_of(px, py, 1 - c)], dst_ref=a_refs[a].at[j],
                                                  send_sem=send.at[4 * a + j], recv_sem=recv.at[4 * a + j],
                                                  device_id=(x, y, 1 - c), device_id_type=MESH)
                cp.wait_send()
                cp.wait_recv()

    outs = pl.pallas_call(
        body, name=name,
        out_shape=tuple(pltpu.HBM(v.shape, v.dtype) for v in st["gs"] + st["lands"]),
        in_specs=[HBM] * (2 * n) + [SEM, SEM, pl.BlockSpec(memory_space=pl.ANY)], out_specs=tuple([HBM] * (2 * n)),
        input_output_aliases={a: a for a in range(2 * n)},
        compiler_params=pltpu.CompilerParams(has_side_effects=EFFECT),
    )(*st["gs"], *st["lands"], st["send"], st["recv"], after)
    return list(outs[:n]), list(outs[n:])


def _chip_start(ps, name):
    n = len(ps)
    lands = [lax.empty(p.shape, p.dtype) for p in ps]

    def body(*refs):
        p_refs, b_refs = refs[:n], refs[n:2 * n]
        send, recv = refs[2 * n], refs[2 * n + 1]
        token = refs[4 * n + 2]
        x, y, c, chips = _place()
        for a in range(n):
            for j, chip in enumerate(chips[1:]):
                pltpu.make_async_remote_copy(src_ref=p_refs[a].at[j], dst_ref=b_refs[a].at[j], send_sem=send.at[3 * a + j], recv_sem=recv.at[3 * a + j],
                                             device_id=(*chip, c), device_id_type=MESH).start()
        token[...] = jnp.zeros_like(token)

    outs = pl.pallas_call(
        body, name=name,
        out_shape=(pltpu.SemaphoreType.DMA((3 * n,)), pltpu.SemaphoreType.DMA((3 * n,)),
                   *[pltpu.HBM(p.shape, p.dtype) for p in ps], *[pltpu.HBM(p.shape, p.dtype) for p in ps], TOKEN),
        in_specs=[HBM] * (2 * n), out_specs=(SEM, SEM, *[HBM] * (2 * n), pl.BlockSpec(memory_space=pltpu.VMEM)),
        input_output_aliases={a: 2 + a for a in range(2 * n)},
        compiler_params=pltpu.CompilerParams(has_side_effects=EFFECT),
    )(*[_in_hbm(p) for p in ps], *[_in_hbm(l) for l in lands])
    return dict(send=outs[0], recv=outs[1], ps=list(outs[2:2 + n]), lands=list(outs[2 + n:2 + 2 * n]), token=outs[-1])


def _chip_finish(st, after, name):
    n = len(st["ps"])

    def body(*refs):
        p_refs, b_refs = refs[:n], refs[n:2 * n]
        send, recv = refs[2 * n], refs[2 * n + 1]
        x, y, c, chips = _place()
        for a in range(n):
            for j, chip in enumerate(chips[1:]):
                cp = pltpu.make_async_remote_copy(src_ref=p_refs[a].at[j], dst_ref=b_refs[a].at[j], send_sem=send.at[3 * a + j], recv_sem=recv.at[3 * a + j],
                                                  device_id=(*chip, c), device_id_type=MESH)
                cp.wait_send()
                cp.wait_recv()

    outs = pl.pallas_call(
        body, name=name,
        out_shape=tuple(pltpu.HBM(v.shape, v.dtype) for v in st["ps"] + st["lands"]),
        in_specs=[HBM] * (2 * n) + [SEM, SEM, pl.BlockSpec(memory_space=pl.ANY)], out_specs=tuple([HBM] * (2 * n)),
        input_output_aliases={a: a for a in range(2 * n)},
        compiler_params=pltpu.CompilerParams(has_side_effects=EFFECT),
    )(*st["ps"], *st["lands"], st["send"], st["recv"], after)
    return list(outs[n:])


def _rows_tile(r):
    for t in (512, 256, 128, 64, 32, 16):
        if r % t == 0:
            return t
    return r


PAIR_SUM_STEPS = 4


def _pair_sum(gs, a_s, slots, name):
    n = len(gs)
    trs = [g.shape[1] // PAIR_SUM_STEPS for g in gs]

    def body(slots_ref, *refs):
        for g_ref, a_ref, p_ref in zip(refs[:n], refs[n:2 * n], refs[2 * n:]):
            p_ref[...] = (g_ref[...].astype(F32) + a_ref[...].astype(F32)).astype(BF)

    def spec(g, tr, index):
        return pl.BlockSpec((None, tr, g.shape[2]), index)

    return pl.pallas_call(
        body, name=name,
        grid_spec=pltpu.PrefetchScalarGridSpec(
            num_scalar_prefetch=1, grid=(3, PAIR_SUM_STEPS),
            in_specs=[spec(g, tr, lambda j, i, s: (s[j + 1], i, 0)) for g, tr in zip(gs, trs)]
            + [spec(g, tr, lambda j, i, s: (j + 1, i, 0)) for g, tr in zip(gs, trs)],
            out_specs=[spec(g, tr, lambda j, i, s: (j, i, 0)) for g, tr in zip(gs, trs)]),
        out_shape=[_sds((3,) + g.shape[1:], BF) for g in gs], compiler_params=_cp("parallel", "parallel"),
    )(slots, *gs, *a_s)


def _adamw_math(w, g, m, v):
    m = ADAM_B1 * m + (1.0 - ADAM_B1) * g
    v = ADAM_B2 * v + (1.0 - ADAM_B2) * (g * g)
    m_hat = m / (1.0 - ADAM_B1 ** ADAM_STEP)
    v_hat = v / (1.0 - ADAM_B2 ** ADAM_STEP)
    delta = -ADAM_LR * (m_hat / (jnp.sqrt(v_hat) + ADAM_EPS) + ADAM_WD * w)
    return delta, m, v


def _adamw_sharded(w, m, v, parts, slots, name, transposed=False):
    nl = w.shape[0]
    r, c = parts[0][0].shape[1:]
    tr = _rows_tile(r)
    if c * tr * 4 > (1 << 21) and not transposed:
        tr = max(8, tr // 2)

    def body(slots_ref, w_ref, m_ref, v_ref, *rest):
        part_refs, (g_out, d_out, m_out, v_out) = rest[:5 * nl], rest[5 * nl:]
        layer = pl.program_id(0)
        g = None
        for l in range(nl):
            s = part_refs[5 * l][...].astype(F32)
            for ref in part_refs[5 * l + 1:5 * l + 5]:
                s = s + ref[...].astype(F32)
            g = s if g is None else jnp.where(layer == l, s, g)
        if transposed:
            g = g.T
        delta, mn, vn = _adamw_math(w_ref[...], g, m_ref[...], v_ref[...])
        g_out[...] = g
        d_out[...] = delta
        m_out[...] = mn
        v_out[...] = vn

    def own(l):
        return lambda L, i, s: (s[0], jnp.where(L == l, i, 0), 0)

    def fixed(l, k):
        return lambda L, i, s: (k, jnp.where(L == l, i, 0), 0)

    if transposed:
        wspec = pl.BlockSpec((None, c, tr), lambda L, i, s: (L, 0, i))
    else:
        wspec = pl.BlockSpec((None, tr, c), lambda L, i, s: (L, i, 0))
    in_specs = [wspec, wspec, wspec]
    args = [w, m, v]
    for l, (g, a, b) in enumerate(parts):
        in_specs += [pl.BlockSpec((None, tr, c), own(l)), pl.BlockSpec((None, tr, c), fixed(l, 0)),
                     pl.BlockSpec((None, tr, c), fixed(l, 0)), pl.BlockSpec((None, tr, c), fixed(l, 1)),
                     pl.BlockSpec((None, tr, c), fixed(l, 2))]
        args += [g, a, b, b, b]
    return pl.pallas_call(
        body, name=name,
        grid_spec=pltpu.PrefetchScalarGridSpec(
            num_scalar_prefetch=1, grid=(nl, r // tr), in_specs=in_specs, out_specs=[wspec] * 4),
        out_shape=[_sds(w.shape, F32)] * 4, compiler_params=_cp("arbitrary", "arbitrary"),
    )(slots, *args)


def _adamw_small(w, m, v, gathered, name):
    def body(w_ref, m_ref, v_ref, gg_ref, g_out, d_out, m_out, v_out):
        g = gg_ref[0]
        for k in range(1, NDEV):
            g = g + gg_ref[k]
        delta, mn, vn = _adamw_math(w_ref[...], g, m_ref[...], v_ref[...])
        g_out[...] = g
        d_out[...] = delta
        m_out[...] = mn
        v_out[...] = vn

    return pl.pallas_call(body, name=name, out_shape=[_sds(w.shape, F32)] * 4)(w, m, v, gathered)


def _rmsnorm_fwd(x, g, name, deps=()):
    s, d = x.shape
    tm = 256

    def body(x_ref, g_ref, h_ref, ht_ref):
        xf = x_ref[...]
        y = xf * lax.rsqrt(jnp.mean(xf * xf, axis=-1, keepdims=True) + RMS_EPS)
        h = y * g_ref[...]
        h_ref[...] = h.astype(BF)
        ht_ref[...] = h.T.astype(BF)

    return pl.pallas_call(
        _ignore_deps(body, 2, len(deps)), name=name, grid=(s // tm,),
        in_specs=[pl.BlockSpec((tm, d), lambda i: (i, 0)), pl.BlockSpec((1, d), lambda i: (0, 0))] + [TOKEN_SPEC] * len(deps),
        out_specs=[pl.BlockSpec((tm, d), lambda i: (i, 0)), pl.BlockSpec((d, tm), lambda i: (0, i))],
        out_shape=[_sds((s, d), BF), _sds((d, s), BF)], compiler_params=_cp("parallel"),
    )(x, g, *deps)


def _rmsnorm_bwd(x, g, dh, dres, out_scale, name):
    s, d = x.shape
    tm = 256

    def body(x_ref, g_ref, dh_ref, dres_ref, dx_ref, dxb_ref, dxbt_ref, dg_ref):
        xf = x_ref[...]
        r = lax.rsqrt(jnp.mean(xf * xf, axis=-1, keepdims=True) + RMS_EPS)
        xhat = xf * r
        dhv = dh_ref[...]
        dxhat = dhv * g_ref[...]
        dx = dres_ref[...] + r * (dxhat - xhat * jnp.mean(dxhat * xhat, axis=-1, keepdims=True))
        dx_ref[...] = dx
        scaled = dx * out_scale
        dxb_ref[...] = scaled.astype(BF)
        dxbt_ref[...] = scaled.T.astype(BF)

        @pl.when(pl.program_id(0) == 0)
        def _():
            dg_ref[...] = jnp.zeros_like(dg_ref)

        dg_ref[...] += jnp.sum(dhv * xhat, axis=0, keepdims=True)

    row = pl.BlockSpec((tm, d), lambda i: (i, 0))
    vec = pl.BlockSpec((1, d), lambda i: (0, 0))
    return pl.pallas_call(
        body, name=name, grid=(s // tm,),
        in_specs=[row, vec, row, row],
        out_specs=[row, row, pl.BlockSpec((d, tm), lambda i: (0, i)), vec],
        out_shape=[_sds((s, d), F32), _sds((s, d), BF), _sds((d, s), BF), _sds((1, d), F32)],
        compiler_params=_cp("arbitrary"),
    )(x, g, dh, dres)


def _loss_head(x, g, target, name):
    s, d = x.shape
    tm = 256

    def body(x_ref, g_ref, t_ref, dx_ref, dxb_ref, dxbt_ref, dg_ref, loss_ref):
        xf = x_ref[...]
        r = lax.rsqrt(jnp.mean(xf * xf, axis=-1, keepdims=True) + RMS_EPS)
        xhat = xf * r
        err = xhat * g_ref[...] - t_ref[...]
        dy = err * (1.0 / d)
        dxhat = dy * g_ref[...]
        dx = r * (dxhat - xhat * jnp.mean(dxhat * xhat, axis=-1, keepdims=True))
        dx_ref[...] = dx
        half = dx * 0.5
        dxb_ref[...] = half.astype(BF)
        dxbt_ref[...] = half.T.astype(BF)

        @pl.when(pl.program_id(0) == 0)
        def _():
            dg_ref[...] = jnp.zeros_like(dg_ref)
            loss_ref[...] = jnp.zeros_like(loss_ref)

        dg_ref[...] += jnp.sum(dy * xhat, axis=0, keepdims=True)
        part = 0.5 * jnp.sum(jnp.mean(err * err, axis=-1, keepdims=True), axis=0, keepdims=True)
        lane = lax.broadcasted_iota(jnp.int32, (1, 128), 1)
        loss_ref[...] += jnp.where(lane == 0, part, 0.0)

    row = pl.BlockSpec((tm, d), lambda i: (i, 0))
    vec = pl.BlockSpec((1, d), lambda i: (0, 0))
    return pl.pallas_call(
        body, name=name, grid=(s // tm,),
        in_specs=[row, vec, row],
        out_specs=[row, row, pl.BlockSpec((d, tm), lambda i: (0, i)), vec, pl.BlockSpec((1, 128), lambda i: (0, 0))],
        out_shape=[_sds((s, d), F32), _sds((s, d), BF), _sds((d, s), BF), _sds((1, d), F32), _sds((1, 128), F32)],
        compiler_params=_cp("arbitrary"),
    )(x, g, target)


def _act_spec(tm, n, natural, order):
    if natural:
        return pl.BlockSpec((tm, n), (lambda s, i: (i, s)) if order == "si" else (lambda i, s: (i, s)))
    return pl.BlockSpec((None, tm, n), (lambda s, i: (s, i, 0)) if order == "si" else (lambda i, s: (s, i, 0)))


def _act_shape(s, n, natural, dtype):
    return _sds((s, NDEV * n), dtype) if natural else _sds((NDEV, s, n), dtype)


def _ffn_up(h, wg, wu, name):
    s, d = h.shape
    n = wg.shape[2]
    tm = 1024

    def body(h_ref, wg_ref, wu_ref, g_ref, u_ref, a_ref, at_ref):
        hb = h_ref[...]
        g = _dot(hb, wg_ref[...])
        u = _dot(hb, wu_ref[...])
        sig = jax.nn.sigmoid(g)
        silu = g * sig
        g_ref[...] = (u * (sig * (1.0 + g * (1.0 - sig)))).astype(BF)
        u_ref[...] = silu.astype(BF)
        act = silu * u
        a_ref[...] = act.astype(BF)
        at_ref[...] = act.T.astype(BF)

    wsp = pl.BlockSpec((None, d, n), lambda s_, i: (s_, 0, 0))
    blk = _act_spec(tm, n, False, "si")
    return pl.pallas_call(
        body, name=name, grid=(NDEV, s // tm),
        in_specs=[pl.BlockSpec((tm, d), lambda s_, i: (i, 0)), wsp, wsp],
        out_specs=[blk] * 3 + [pl.BlockSpec((None, n, tm), lambda s_, i: (s_, 0, i))],
        out_shape=[_act_shape(s, n, False, BF)] * 3 + [_sds((NDEV, n, s), BF)],
        compiler_params=_cp("parallel", "parallel"),
    )(h, wg, wu)


def _ffn_down(act, wd, x, name):
    _, s, n = act.shape
    d = wd.shape[2]
    tm = 512

    def body(a_ref, w_ref, x_ref, o_ref, acc):
        k = pl.program_id(1)

        @pl.when(k == 0)
        def _():
            acc[...] = jnp.zeros_like(acc)

        acc[...] += _dot(a_ref[...], w_ref[...])

        @pl.when(k == NDEV - 1)
        def _():
            o_ref[...] = x_ref[...] + 0.5 * acc[...]

    row = pl.BlockSpec((tm, d), lambda i, k: (i, 0))
    return pl.pallas_call(
        body, name=name, grid=(s // tm, NDEV),
        in_specs=[_act_spec(tm, n, False, "is"), pl.BlockSpec((None, n, d), lambda i, k: (k, 0, 0)), row],
        out_specs=row, out_shape=_sds((s, d), F32),
        scratch_shapes=[pltpu.VMEM((tm, d), F32)], compiler_params=_cp("parallel", "arbitrary"),
    )(act, wd, x)


def _transpose_shards(w, after, name):
    _, a, b = w.shape
    tb = 512

    def body(w_ref, after_ref, o_ref):
        o_ref[...] = w_ref[...].astype(F32).T.astype(BF)

    return pl.pallas_call(
        body, name=name, grid=(NDEV, b // tb),
        in_specs=[pl.BlockSpec((None, a, tb), lambda k, j: (k, 0, j)), ANY],
        out_specs=pl.BlockSpec((None, tb, a), lambda k, j: (k, j, 0)), out_shape=_sds((NDEV, b, a), BF),
        compiler_params=_cp("parallel", "parallel"),
    )(w, after)


def _ffn_bwd_act(dyb, wd, g, u, name, deps=(), wd_t=None):
    s, d = dyb.shape
    n = wd.shape[1]
    tm = 1024

    def body(dy_ref, w_ref, g_ref, u_ref, dg_ref, du_ref):
        dact = _dot_nt(dy_ref[...], w_ref[...]) if wd_t is None else _dot(dy_ref[...], w_ref[...])
        dg_ref[...] = (dact * g_ref[...].astype(F32)).astype(BF)
        du_ref[...] = (dact * u_ref[...].astype(F32)).astype(BF)

    blk = _act_spec(tm, n, False, "si")
    wspec = pl.BlockSpec((None, n, d), lambda s_, i: (s_, 0, 0)) if wd_t is None else pl.BlockSpec((None, d, n), lambda s_, i: (s_, 0, 0))
    wd = wd if wd_t is None else wd_t
    return pl.pallas_call(
        _ignore_deps(body, 4, len(deps)), name=name, grid=(NDEV, s // tm),
        in_specs=[pl.BlockSpec((tm, d), lambda s_, i: (i, 0)), wspec, blk, blk]
        + [TOKEN_SPEC] * len(deps),
        out_specs=[blk, blk], out_shape=[_act_shape(s, n, False, BF)] * 2,
        compiler_params=_cp("parallel", "parallel"),
    )(dyb, wd, g, u, *deps)


def _grad_rows(act_t, dyb, name, deps=()):
    _, n, s = act_t.shape
    d = dyb.shape[1]
    tn = 2048

    def body(a_ref, dy_ref, o_ref):
        o_ref[...] = _dot(a_ref[...], dy_ref[...]).astype(BF)

    return pl.pallas_call(
        _ignore_deps(body, 2, len(deps)), name=name, grid=(NDEV, d // tn),
        in_specs=[pl.BlockSpec((None, n, s), lambda k, j: (k, 0, 0)), pl.BlockSpec((s, tn), lambda k, j: (0, j))]
        + [TOKEN_SPEC] * len(deps),
        out_specs=pl.BlockSpec((None, n, tn), lambda k, j: (k, 0, j)), out_shape=_sds((NDEV, n, d), BF),
        compiler_params=_cp("parallel", "parallel"),
    )(act_t, dyb, *deps)


def _grad_cols(ht, dxs, naturals, name, deps=()):
    d, s = ht.shape
    k = len(dxs)
    ns = [dx.shape[1] // NDEV if nat else dx.shape[2] for dx, nat in zip(dxs, naturals)]
    td = 1024

    def body(*refs):
        ht_ref, dx_refs, o_refs = refs[0], refs[1:1 + k], refs[1 + k:]
        hv = ht_ref[...]
        for dx_ref, o_ref in zip(dx_refs, o_refs):
            o_ref[...] = _dot(hv, dx_ref[...]).astype(BF)

    def dx_spec(n, nat):
        if nat:
            return pl.BlockSpec((s, n), lambda s_, j: (0, s_))
        return pl.BlockSpec((None, s, n), lambda s_, j: (s_, 0, 0))

    return pl.pallas_call(
        _ignore_deps(body, 1 + k, len(deps)), name=name, grid=(NDEV, d // td),
        in_specs=[pl.BlockSpec((td, s), lambda s_, j: (j, 0))] + [dx_spec(n, nat) for n, nat in zip(ns, naturals)]
        + [TOKEN_SPEC] * len(deps),
        out_specs=[pl.BlockSpec((None, td, n), lambda s_, j: (s_, j, 0)) for n in ns],
        out_shape=[_sds((NDEV, d, n), BF) for n in ns], compiler_params=_cp("parallel", "parallel"),
    )(ht, *dxs, *deps)


def _dh_cols(dxs, ws, naturals, name, deps=()):
    k = len(dxs)
    d = ws[0].shape[1]
    ns = [w.shape[2] for w in ws]
    s = dxs[0].shape[0] if naturals[0] else dxs[0].shape[1]
    tm = 512

    def body(*refs):
        dx_refs, w_refs, o_ref, acc = refs[:k], refs[k:2 * k], refs[2 * k], refs[2 * k + 1]
        j = pl.program_id(1)

        @pl.when(j == 0)
        def _():
            acc[...] = jnp.zeros_like(acc)

        t = _dot_nt(dx_refs[0][...], w_refs[0][...])
        for dx_ref, w_ref in zip(dx_refs[1:], w_refs[1:]):
            t = t + _dot_nt(dx_ref[...], w_ref[...])
        acc[...] += t

        @pl.when(j == NDEV - 1)
        def _():
            o_ref[...] = acc[...]

    return pl.pallas_call(
        _ignore_deps(body, 2 * k, len(deps)), name=name, grid=(s // tm, NDEV),
        in_specs=[_act_spec(tm, n, nat, "is") for n, nat in zip(ns, naturals)]
        + [pl.BlockSpec((None, d, n), lambda i, j: (j, 0, 0)) for n in ns] + [TOKEN_SPEC] * len(deps),
        out_specs=pl.BlockSpec((tm, d), lambda i, j: (i, 0)), out_shape=_sds((s, d), F32),
        scratch_shapes=[pltpu.VMEM((tm, d), F32)], compiler_params=_cp("parallel", "arbitrary"),
    )(*dxs, *ws, *deps)


def _mm_nn(a, b, tn, out_dtype, name, res=None, tm=1024, deps=()):
    m, k = a.shape
    nn = b.shape[1]

    def body(*refs):
        if res is None:
            a_ref, b_ref, o_ref = refs
            o_ref[...] = _dot(a_ref[...], b_ref[...]).astype(out_dtype)
        else:
            a_ref, b_ref, r_ref, o_ref = refs
            o_ref[...] = (r_ref[...] + _dot(a_ref[...], b_ref[...])).astype(out_dtype)

    osp = pl.BlockSpec((tm, tn), lambda j, i: (i, j))
    in_specs = [pl.BlockSpec((tm, k), lambda j, i: (i, 0)), pl.BlockSpec((k, tn), lambda j, i: (0, j))]
    args = [a, b]
    if res is not None:
        in_specs.append(osp)
        args.append(res)
    return pl.pallas_call(
        _ignore_deps(body, len(args), len(deps)), name=name, grid=(nn // tn, m // tm),
        in_specs=in_specs + [TOKEN_SPEC] * len(deps), out_specs=osp,
        out_shape=_sds((m, nn), out_dtype), compiler_params=_cp("parallel", "parallel"),
    )(*args, *deps)


def _mm_nt(pairs, name, out_dtype=F32, tm=512, tk=512, deps=()):
    m = pairs[0][0].shape[0]
    kk = pairs[0][1].shape[0]
    p = len(pairs)

    def body(*refs):
        o_ref = refs[2 * p]
        t = _dot_nt(refs[0][...], refs[1][...])
        for q in range(1, p):
            t = t + _dot_nt(refs[2 * q][...], refs[2 * q + 1][...])
        o_ref[...] = t.astype(out_dtype)

    in_specs, args = [], []
    for a, b in pairs:
        in_specs += [pl.BlockSpec((tm, a.shape[1]), lambda j, i: (i, 0)), pl.BlockSpec((tk, b.shape[1]), lambda j, i: (j, 0))]
        args += [a, b]
    return pl.pallas_call(
        _ignore_deps(body, 2 * p, len(deps)), name=name, grid=(kk // tk, m // tm), in_specs=in_specs + [TOKEN_SPEC] * len(deps),
        out_specs=pl.BlockSpec((tm, tk), lambda j, i: (i, j)), out_shape=_sds((m, kk), out_dtype),
        compiler_params=_cp("parallel", "parallel"),
    )(*args, *deps)


def _rope_tables(s, sign):
    half = ROPE_DIMS // 2
    f32 = np.float32
    freqs = f32(ROPE_THETA) ** (-np.arange(half, dtype=f32) / f32(half))
    ang = np.arange(s, dtype=f32)[:, None] * freqs[None, :]
    cos, sin = np.cos(ang).astype(f32), (sign * np.sin(ang)).astype(f32)
    one = np.ones((s, HD - ROPE_DIMS), f32)
    zero = np.zeros((s, HD - ROPE_DIMS), f32)
    zh = np.zeros((s, half), f32)
    c = np.concatenate([cos, cos, one], axis=1)
    sa = np.concatenate([-sin, zh, zero], axis=1)
    sb = np.concatenate([zh, sin, zero], axis=1)
    return jnp.asarray(c), jnp.asarray(sa), jnp.asarray(sb)


def _rope(xv, c, sa, sb):
    return xv * c + pltpu.roll(xv, HD - ROPE_DIMS // 2, 1) * sa + pltpu.roll(xv, ROPE_DIMS // 2, 1) * sb


def _qkv_rope(h, w, tables, name):
    s, d = h.shape
    n = w.shape[2]
    per = n // HD
    tm = 1024

    def body(h_ref, w_ref, c_ref, sa_ref, sb_ref, o_ref):
        shard = pl.program_id(0)
        y = _dot(h_ref[...], w_ref[...])
        c, sa, sb = c_ref[...], sa_ref[...], sb_ref[...]
        for j in range(per):
            blk = y[:, j * HD:(j + 1) * HD]
            rot = _rope(blk, c, sa, sb)
            is_qk = shard * per + j < 2 * N_HEADS
            o_ref[:, j * HD:(j + 1) * HD] = jnp.where(is_qk, rot, blk).astype(BF)

    tab = pl.BlockSpec((tm, HD), lambda s_, i: (i, 0))
    return pl.pallas_call(
        body, name=name, grid=(NDEV, s // tm),
        in_specs=[pl.BlockSpec((tm, d), lambda s_, i: (i, 0)), pl.BlockSpec((None, d, n), lambda s_, i: (s_, 0, 0)), tab, tab, tab],
        out_specs=pl.BlockSpec((tm, n), lambda s_, i: (i, s_)), out_shape=_sds((s, NDEV * n), BF),
        compiler_params=_cp("parallel", "parallel"),
    )(h, w, *tables)


def _iota2():
    return (lax.broadcasted_iota(jnp.int32, (QB, QB), 0), lax.broadcasted_iota(jnp.int32, (QB, QB), 1))


def _softplus(z):
    return jnp.maximum(z, 0.0) + jnp.log(1.0 + jnp.exp(-jnp.abs(z)))


def _tri_dot(xv, tri, left=False):
    hi = xv.astype(BF)
    r1 = xv - hi.astype(F32)
    mid = r1.astype(BF)
    lo = (r1 - mid.astype(F32)).astype(BF)
    if left:
        return _dot(tri, hi) + _dot(tri, mid) + _dot(tri, lo)
    return _dot(hi, tri) + _dot(mid, tri) + _dot(lo, tri)


def _col(ref_or_val):
    return ref_or_val[:, 0:1]


KT = 4 * QB
QQ = 4 * QB


def _iota_tile():
    return (lax.broadcasted_iota(jnp.int32, (QQ, KT), 0), lax.broadcasted_iota(jnp.int32, (QQ, KT), 1))


def _scan_matrix(keep):
    tri = keep(*_iota2()).astype(BF)
    return jnp.concatenate([tri, tri], axis=0)


def _scan_dot(xv, tri2):
    hi = xv.astype(BF)
    lo = (xv - hi.astype(F32)).astype(BF)
    return _dot(jnp.concatenate([hi, lo], axis=1), tri2)


def _blocks(xv):
    return [xv[:, b * QB:(b + 1) * QB] for b in range(KT // QB)]


def _sb_fwd(qkv, name):
    s = qkv.shape[0]
    nb = s // QB

    def body(q_ref, k_ref, v_ref, o_ref, ot_ref, t_ref):
        i = pl.program_id(1)
        q = q_ref[...]
        row, col = _iota_tile()
        later_keys = _scan_matrix(lambda j, s_: j > s_)
        last = (i * QQ + QQ - 1) // KT

        def step(tt, carry):
            acc, later = carry
            t = last - tt
            off = pl.multiple_of(t * KT, KT)
            k = k_ref[pl.ds(off, KT), :]
            v = v_ref[pl.ds(off, KT), :]
            z = _dot_nt(q, k) * SCALE
            strict = row + (i * QQ - t * KT) > col
            sp = _softplus(z)
            lnb = jnp.where(strict, -sp, 0.0)
            afters = []
            for xb in reversed(_blocks(lnb)):
                afters.append(later + _scan_dot(xb, later_keys))
                later = later + jnp.sum(xb, axis=1, keepdims=True)
            after = jnp.concatenate(afters[::-1], axis=1)
            w = jnp.where(strict, jnp.exp((z - sp) + after), 0.0)
            return acc + _dot(w.astype(BF), v), later

        acc, total = lax.fori_loop(0, last + 1, step, (jnp.zeros((QQ, HD), F32), jnp.zeros((QQ, 1), F32)))
        o_ref[...] = acc.astype(BF)
        ot_ref[...] = acc.T.astype(BF)
        t_ref[...] = jnp.broadcast_to(total, (QQ, HD))

    blk = pl.BlockSpec((QQ, HD), lambda h, i: (i, h))
    return pl.pallas_call(
        body, name=name, grid=(N_SB, s // QQ),
        in_specs=[blk, pl.BlockSpec((s, HD), lambda h, i: (0, N_HEADS + h)), pl.BlockSpec((s, HD), lambda h, i: (0, 2 * N_HEADS + h))],
        out_specs=[blk, pl.BlockSpec((HD, QQ), lambda h, i: (h, i)), blk],
        out_shape=[_sds((s, N_SB * HD), BF), _sds((N_SB * HD, s), BF), _sds((s, N_SB * HD), F32)],
        compiler_params=_cp("parallel", "parallel"),
    )(qkv, qkv, qkv)


def _sb_bwd(qkv, do, total, name):
    s = qkv.shape[0]
    nb = s // QB

    def body(q_ref, k_ref, v_ref, do_ref, t_ref, dq_ref, dk_ref, dv_ref, dk_acc, dv_acc):
        i = pl.program_id(1)

        @pl.when(i == 0)
        def _():
            dk_acc[...] = jnp.zeros_like(dk_acc)
            dv_acc[...] = jnp.zeros_like(dv_acc)

        q = q_ref[...]
        dov = do_ref[...]
        tot = _col(t_ref[...])
        row, col = _iota_tile()
        keys_upto = _scan_matrix(lambda j, s_: j <= s_)
        keys_before = _scan_matrix(lambda j, s_: j < s_)

        def step(t, carry):
            dq, lnb_before, dl_before = carry
            off = pl.multiple_of(t * KT, KT)
            k = k_ref[pl.ds(off, KT), :]
            v = v_ref[pl.ds(off, KT), :]
            z = _dot_nt(q, k) * SCALE
            strict = row + (i * QQ - t * KT) > col
            sp = _softplus(z)
            lnb = jnp.where(strict, -sp, 0.0)
            afters = []
            for xb in _blocks(lnb):
                afters.append(tot - (lnb_before + _scan_dot(xb, keys_upto)))
                lnb_before = lnb_before + jnp.sum(xb, axis=1, keepdims=True)
            a = jnp.where(strict, jnp.exp((z - sp) + jnp.concatenate(afters, axis=1)), 0.0)
            dl = a * _dot_nt(dov, v)
            befores = []
            for xb in _blocks(dl):
                befores.append(dl_before + _scan_dot(xb, keys_before))
                dl_before = dl_before + jnp.sum(xb, axis=1, keepdims=True)
            sig = jnp.exp(z - sp)
            dz = jnp.where(strict, dl * (1.0 - sig) - sig * jnp.concatenate(befores, axis=1), 0.0) * SCALE
            dq = dq + _dot(dz.astype(BF), k)
            dk_acc[pl.ds(off, KT), :] += _dot(dz.T.astype(BF), q)
            dv_acc[pl.ds(off, KT), :] += _dot(a.T.astype(BF), dov)
            return dq, lnb_before, dl_before

        zero = jnp.zeros((QQ, 1), F32)
        dq, _, _ = lax.fori_loop(0, (i * QQ + QQ - 1) // KT + 1, step, (jnp.zeros((QQ, HD), F32), zero, zero))
        dq_ref[...] = dq.astype(BF)

        @pl.when(i == s // QQ - 1)
        def _():
            dk_ref[...] = dk_acc[...].astype(BF)
            dv_ref[...] = dv_acc[...].astype(BF)

    blk = pl.BlockSpec((QQ, HD), lambda h, i: (i, h))
    full = pl.BlockSpec((s, HD), lambda h, i: (0, h))
    return pl.pallas_call(
        body, name=name, grid=(N_SB, s // QQ),
        in_specs=[blk, pl.BlockSpec((s, HD), lambda h, i: (0, N_HEADS + h)), pl.BlockSpec((s, HD), lambda h, i: (0, 2 * N_HEADS + h)), blk, blk],
        out_specs=[blk, full, full], out_shape=[_sds((s, N_SB * HD), BF)] * 3,
        scratch_shapes=[pltpu.VMEM((s, HD), F32), pltpu.VMEM((s, HD), F32)],
        compiler_params=_cp("parallel", "arbitrary"),
    )(qkv, qkv, qkv, do, total)


def _fgate_fwd(f, b, name):
    s = f.shape[0]
    nb = s // QB
    nfox = N_HEADS - N_SB

    def body(f_ref, b_ref, cb_ref, ct_ref):
        row, col = _iota2()
        upto = (row >= col).astype(BF)
        carry = jnp.zeros((1, HD), F32)
        for blk in range(nb):
            xv = f_ref[blk * QB:(blk + 1) * QB, :] + b_ref[...]
            logf = -_softplus(-xv)
            cum = _tri_dot(logf, upto, left=True) + carry
            carry = cum[QB - 1:QB, :]
            ct_ref[blk] = cum.T
            for h in range(nfox):
                cb_ref[blk * QB:(blk + 1) * QB, h * HD:(h + 1) * HD] = jnp.broadcast_to(cum[:, h:h + 1], (QB, HD))

    return pl.pallas_call(
        body, name=name, out_shape=[_sds((s, nfox * HD), F32), _sds((nb, HD, HD), F32)], compiler_params=_cp(),
    )(f, b)


def _fgate_bwd(dcq, dck, f, b, name):
    s = f.shape[0]
    nb = s // QB
    nfox = N_HEADS - N_SB

    def body(dcq_ref, dck_ref, f_ref, b_ref, df_ref, db_ref):
        row, col = _iota2()
        from_tri = (row <= col).astype(BF)
        lane = col
        carry = jnp.zeros((1, HD), F32)
        db = jnp.zeros((1, HD), F32)
        for blk in reversed(range(nb)):
            dcum = jnp.zeros((QB, HD), F32)
            for h in range(nfox):
                here = (slice(blk * QB, (blk + 1) * QB), slice(h * HD, (h + 1) * HD))
                dcum = jnp.where(lane == h, dcq_ref[here] - dck_ref[here], dcum)
            dlogf = _tri_dot(dcum, from_tri, left=True) + carry
            carry = dlogf[0:1, :]
            xv = f_ref[blk * QB:(blk + 1) * QB, :] + b_ref[...]
            sp = _softplus(xv)
            df = jnp.where(lane < nfox, dlogf * jnp.exp(-sp), 0.0)
            df_ref[blk * QB:(blk + 1) * QB, :] = df.astype(BF)
            db = db + jnp.sum(df, axis=0, keepdims=True)
        db_ref[...] = db

    return pl.pallas_call(
        body, name=name, out_shape=[_sds((s, HD), BF), _sds((1, HD), F32)], compiler_params=_cp(),
    )(dcq, dck, f, b)


def _fox_head_row(ct_ref, j, h):
    tile = ct_ref[j]
    sub = lax.broadcasted_iota(jnp.int32, tile.shape, 0)
    return jnp.sum(jnp.where(sub == h, tile, 0.0), axis=0, keepdims=True)


def _fox_tile_row(ct_ref, t, h):
    nsub = KT // QB
    return jnp.concatenate([_fox_head_row(ct_ref, t * nsub + b, h) for b in range(nsub)], axis=1)


def _fox_fwd(qkv, cum_b, cum_t, name):
    s = qkv.shape[0]
    nb = s // QB
    nfox = N_HEADS - N_SB

    def body(q_ref, k_ref, v_ref, cq_ref, ct_ref, o_ref, ot_ref, lse_ref):
        h, i = pl.program_id(0), pl.program_id(1)
        q = q_ref[...]
        cq = _col(cq_ref[...])
        row, col = _iota_tile()

        def step(t, carry):
            acc, m, l = carry
            off = pl.multiple_of(t * KT, KT)
            k = k_ref[pl.ds(off, KT), :]
            v = v_ref[pl.ds(off, KT), :]
            z = _dot_nt(q, k) * SCALE + cq - _fox_tile_row(ct_ref, t, h)
            z = jnp.where(row + (i * QQ - t * KT) >= col, z, NEG_INF)
            m_new = jnp.maximum(m, jnp.max(z, axis=1, keepdims=True))
            alpha = jnp.exp(m - m_new)
            p = jnp.exp(z - m_new)
            l = alpha * l + jnp.sum(p, axis=1, keepdims=True)
            acc = alpha * acc + _dot(p.astype(BF), v)
            return acc, m_new, l

        acc, m, l = lax.fori_loop(0, (i * QQ + QQ - 1) // KT + 1, step,
                                  (jnp.zeros((QQ, HD), F32), jnp.full((QQ, 1), NEG_INF, F32), jnp.zeros((QQ, 1), F32)))
        o = acc / l
        o_ref[...] = o.astype(BF)
        ot_ref[...] = o.T.astype(BF)
        lse_ref[...] = jnp.broadcast_to(m + jnp.log(l), (QQ, HD))

    blk = pl.BlockSpec((QQ, HD), lambda h, i: (i, h))
    return pl.pallas_call(
        body, name=name, grid=(nfox, s // QQ),
        in_specs=[pl.BlockSpec((QQ, HD), lambda h, i: (i, N_SB + h)),
                  pl.BlockSpec((s, HD), lambda h, i: (0, N_HEADS + N_SB + h)),
                  pl.BlockSpec((s, HD), lambda h, i: (0, 2 * N_HEADS + N_SB + h)),
                  blk, pl.BlockSpec((nb, 8, HD), lambda h, i: (0, 0, 0))],
        out_specs=[blk, pl.BlockSpec((HD, QQ), lambda h, i: (h, i)), blk],
        out_shape=[_sds((s, nfox * HD), BF), _sds((nfox * HD, s), BF), _sds((s, nfox * HD), F32)],
        compiler_params=_cp("parallel", "parallel"),
    )(qkv, qkv, qkv, cum_b, cum_t)


def _fox_bwd(qkv, cum_b, cum_t, o, lse, do, name):
    s = qkv.shape[0]
    nb = s // QB
    nfox = N_HEADS - N_SB

    def body(q_ref, k_ref, v_ref, cq_ref, ct_ref, o_ref, lse_ref, do_ref, dq_ref, dk_ref, dv_ref, dcq_ref, dc_ref, dk_acc, dv_acc, dc_acc):
        h, i = pl.program_id(0), pl.program_id(1)

        @pl.when(i == 0)
        def _():
            dk_acc[...] = jnp.zeros_like(dk_acc)
            dv_acc[...] = jnp.zeros_like(dv_acc)
            dc_acc[...] = jnp.zeros_like(dc_acc)

        q = q_ref[...]
        cq = _col(cq_ref[...])
        dov = do_ref[...]
        lse_c = _col(lse_ref[...])
        delta = jnp.sum(dov.astype(F32) * o_ref[...].astype(F32), axis=1, keepdims=True)
        row, col = _iota_tile()
        ones = jnp.ones((QQ, HD), BF)

        def step(t, carry):
            dq, over_keys = carry
            off = pl.multiple_of(t * KT, KT)
            k = k_ref[pl.ds(off, KT), :]
            v = v_ref[pl.ds(off, KT), :]
            z = _dot_nt(q, k) * SCALE + cq - _fox_tile_row(ct_ref, t, h)
            p = jnp.where(row + (i * QQ - t * KT) >= col, jnp.exp(z - lse_c), 0.0)
            dz = p * (_dot_nt(dov, v) - delta)
            dzt = dz.T
            dq = dq + _dot((dz * SCALE).astype(BF), k)
            dk_acc[pl.ds(off, KT), :] += _dot((dzt * SCALE).astype(BF), q)
            dv_acc[pl.ds(off, KT), :] += _dot(p.T.astype(BF), dov)
            dc_acc[pl.ds(off, KT), :] += _tri_dot(dzt, ones)
            return dq, over_keys + jnp.sum(dz, axis=1, keepdims=True)

        dq, over_keys = lax.fori_loop(0, (i * QQ + QQ - 1) // KT + 1, step, (jnp.zeros((QQ, HD), F32), jnp.zeros((QQ, 1), F32)))
        dq_ref[...] = dq.astype(BF)
        dcq_ref[...] = jnp.broadcast_to(over_keys, (QQ, HD))

        @pl.when(i == s // QQ - 1)
        def _():
            dk_ref[...] = dk_acc[...].astype(BF)
            dv_ref[...] = dv_acc[...].astype(BF)
            dc_ref[...] = dc_acc[...]

    blk = pl.BlockSpec((QQ, HD), lambda h, i: (i, h))
    full = pl.BlockSpec((s, HD), lambda h, i: (0, h))
    return pl.pallas_call(
        body, name=name, grid=(nfox, s // QQ),
        in_specs=[pl.BlockSpec((QQ, HD), lambda h, i: (i, N_SB + h)),
                  pl.BlockSpec((s, HD), lambda h, i: (0, N_HEADS + N_SB + h)),
                  pl.BlockSpec((s, HD), lambda h, i: (0, 2 * N_HEADS + N_SB + h)),
                  blk, pl.BlockSpec((nb, 8, HD), lambda h, i: (0, 0, 0)), blk, blk,
                  pl.BlockSpec((QQ, HD), lambda h, i: (i, N_SB + h))],
        out_specs=[blk, full, full, blk, full],
        out_shape=[_sds((s, nfox * HD), BF)] * 3 + [_sds((s, nfox * HD), F32)] * 2,
        scratch_shapes=[pltpu.VMEM((s, HD), F32)] * 3,
        compiler_params=_cp("parallel", "arbitrary"),
    )(qkv, qkv, qkv, cum_b, cum_t, o, lse, do)


GB = 8
DIL_PAD = QB * 16


def _dil_group(g, d, nb, off=0, shift=0):
    if nb >= GB:
        r, n0 = (g * GB) // nb, (g * GB) % nb
        start = off + r + (n0 + shift) * QB * d
        return [pl.ds(pl.multiple_of(start, QB), GB * QB)] if d == 1 else [pl.ds(start, GB * QB, stride=d)]
    per = GB // nb
    return [pl.ds(off + g * per + e + shift * QB * d, nb * QB, stride=d) for e in range(per)]


def _dil_load(ref, g, d, nb, off=0, shift=0):
    parts = [ref[sl, :] for sl in _dil_group(g, d, nb, off, shift)]
    rows = parts[0] if len(parts) == 1 else jnp.concatenate(parts, axis=0)
    return rows.reshape(GB, QB, HD)


def _dil_store(ref, g, d, nb, val, off=0, shift=0, add=False):
    rows = val.reshape(GB * QB, HD)
    slices = _dil_group(g, d, nb, off, shift)
    size = GB * QB // len(slices)
    for b, sl in enumerate(slices):
        piece = rows if len(slices) == 1 else rows[b * size:(b + 1) * size]
        if add:
            ref[sl, :] += piece
        else:
            ref[sl, :] = piece


def _bdot_nt(a, b):
    return lax.dot_general(a, b, (((2,), (2,)), ((0,), (0,))), preferred_element_type=F32)


def _bdot(a, b):
    return lax.dot_general(a, b, (((2,), (1,)), ((0,), (0,))), preferred_element_type=F32)


def _bdot_tn(a, b):
    return lax.dot_general(jnp.swapaxes(a, 1, 2).astype(BF), b, (((2,), (1,)), ((0,), (0,))), preferred_element_type=F32)


def _dil_masks(g, d, nb):
    row = lax.broadcasted_iota(jnp.int32, (GB, QB, QB), 1)
    col = lax.broadcasted_iota(jnp.int32, (GB, QB, QB), 2)
    blk = lax.broadcasted_iota(jnp.int32, (GB, QB, QB), 0)
    blk = blk + (g * GB) % nb if nb >= GB else blk % nb
    return col <= row, jnp.logical_and(col >= row, blk >= 1) if nb > 1 else None


def _dilated_fwd(qkv, name):
    s = qkv.shape[0]
    npat = len(DILATED_PATTERNS)
    chunk = 256

    def body(q_ref, k_ref, v_ref, out_ref, outt_ref, g_ref, qf, kf, vf, *per_pattern):
        o_s, l_s = per_pattern[:npat], per_pattern[npat:]
        qf[...] = q_ref[...].astype(F32)
        for dst, src in ((kf, k_ref), (vf, v_ref)):
            dst[0:DIL_PAD, :] = jnp.zeros((DIL_PAD, HD), F32)
            dst[DIL_PAD:, :] = src[...].astype(F32)
        for p, (_, d) in enumerate(DILATED_PATTERNS):
            nb = s // d // QB

            def grp(g, carry, p=p, d=d, nb=nb):
                mc, mp = _dil_masks(g, d, nb)
                q = _dil_load(qf, g, d, nb).astype(BF)
                zc = jnp.where(mc, _bdot_nt(q, _dil_load(kf, g, d, nb, DIL_PAD).astype(BF)) * SCALE, NEG_INF)
                m = jnp.max(zc, axis=2, keepdims=True)
                if nb > 1:
                    zp = jnp.where(mp, _bdot_nt(q, _dil_load(kf, g, d, nb, DIL_PAD, -1).astype(BF)) * SCALE, NEG_INF)
                    m = jnp.maximum(m, jnp.max(zp, axis=2, keepdims=True))
                ec = jnp.exp(zc - m)
                l = jnp.sum(ec, axis=2, keepdims=True)
                if nb > 1:
                    ep = jnp.where(mp, jnp.exp(zp - m), 0.0)
                    l = l + jnp.sum(ep, axis=2, keepdims=True)
                o = _bdot((ec / l).astype(BF), _dil_load(vf, g, d, nb, DIL_PAD).astype(BF))
                if nb > 1:
                    o = o + _bdot((ep / l).astype(BF), _dil_load(vf, g, d, nb, DIL_PAD, -1).astype(BF))
                _dil_store(o_s[p], g, d, nb, o)
                _dil_store(l_s[p], g, d, nb, jnp.broadcast_to(m + jnp.log(l), (GB, QB, HD)))
                return carry

            lax.fori_loop(0, s // (QB * GB), grp, 0)
        for c0 in range(0, s, chunk):
            rows = slice(c0, c0 + chunk)
            ls = [l_s[p][rows, :] for p in range(npat)]
            m = functools.reduce(jnp.maximum, ls)
            es = [jnp.exp(l - m) for l in ls]
            tot = functools.reduce(lambda a, b: a + b, es)
            out = functools.reduce(lambda a, b: a + b, [(e / tot) * o_s[p][rows, :] for p, e in enumerate(es)])
            out_ref[rows, :] = out.astype(BF)
            outt_ref[:, rows] = out.T.astype(BF)
            g_ref[rows, :] = m + jnp.log(tot)

    full = pl.BlockSpec((s, HD), lambda h: (0, h))
    return pl.pallas_call(
        body, name=name, grid=(N_HEADS,),
        in_specs=[full, pl.BlockSpec((s, HD), lambda h: (0, N_HEADS + h)), pl.BlockSpec((s, HD), lambda h: (0, 2 * N_HEADS + h))],
        out_specs=[full, pl.BlockSpec((HD, s), lambda h: (h, 0)), full],
        out_shape=[_sds((s, N_HEADS * HD), BF), _sds((N_HEADS * HD, s), BF), _sds((s, N_HEADS * HD), F32)],
        scratch_shapes=[pltpu.VMEM((s, HD), F32)] + [pltpu.VMEM((s + DIL_PAD, HD), F32)] * 2 + [pltpu.VMEM((s, HD), F32)] * (2 * npat),
        compiler_params=_cp("parallel"),
    )(qkv, qkv, qkv)


def _dilated_bwd(qkv, out, glse, do, tables, name):
    s = qkv.shape[0]
    chunk = 256

    def body(q_ref, k_ref, v_ref, out_ref, g_ref, do_ref, c_ref, sa_ref, sb_ref, dq_ref, dk_ref, dv_ref,
             qf, kf, vf, dof, dl_s, dq_a, dk_a, dv_a):
        qf[...] = q_ref[...].astype(F32)
        for dst, src in ((kf, k_ref), (vf, v_ref)):
            dst[0:DIL_PAD, :] = jnp.zeros((DIL_PAD, HD), F32)
            dst[DIL_PAD:, :] = src[...].astype(F32)
        for c0 in range(0, s, chunk):
            rows = slice(c0, c0 + chunk)
            dov = do_ref[rows, :].astype(F32)
            dof[rows, :] = dov
            dl_s[rows, :] = jnp.broadcast_to(jnp.sum(dov * out_ref[rows, :].astype(F32), axis=1, keepdims=True), (chunk, HD))
        dq_a[...] = jnp.zeros_like(dq_a)
        dk_a[...] = jnp.zeros_like(dk_a)
        dv_a[...] = jnp.zeros_like(dv_a)
        for _, d in DILATED_PATTERNS:
            nb = s // d // QB

            def grp(g, carry, d=d, nb=nb):
                mc, mp = _dil_masks(g, d, nb)
                q = _dil_load(qf, g, d, nb).astype(BF)
                kc = _dil_load(kf, g, d, nb, DIL_PAD).astype(BF)
                dov = _dil_load(dof, g, d, nb).astype(BF)
                lse = _dil_load(g_ref, g, d, nb)[:, :, 0:1]
                delta = _dil_load(dl_s, g, d, nb)[:, :, 0:1]
                pc = jnp.where(mc, jnp.exp(_bdot_nt(q, kc) * SCALE - lse), 0.0)
                dzc = pc * (_bdot_nt(dov, _dil_load(vf, g, d, nb, DIL_PAD).astype(BF)) - delta) * SCALE
                dq = _bdot(dzc.astype(BF), kc)
                if nb > 1:
                    kp = _dil_load(kf, g, d, nb, DIL_PAD, -1).astype(BF)
                    pp = jnp.where(mp, jnp.exp(_bdot_nt(q, kp) * SCALE - lse), 0.0)
                    dzp = pp * (_bdot_nt(dov, _dil_load(vf, g, d, nb, DIL_PAD, -1).astype(BF)) - delta) * SCALE
                    dq = dq + _bdot(dzp.astype(BF), kp)
                _dil_store(dq_a, g, d, nb, dq, add=True)
                _dil_store(dk_a, g, d, nb, _bdot_tn(dzc, q), DIL_PAD, add=True)
                _dil_store(dv_a, g, d, nb, _bdot_tn(pc, dov), DIL_PAD, add=True)
                if nb > 1:
                    _dil_store(dk_a, g, d, nb, _bdot_tn(dzp, q), DIL_PAD, -1, add=True)
                    _dil_store(dv_a, g, d, nb, _bdot_tn(pp, dov), DIL_PAD, -1, add=True)
                return carry

            lax.fori_loop(0, s // (QB * GB), grp, 0)
        for c0 in range(0, s, chunk):
            rows = slice(c0, c0 + chunk)
            padded = slice(DIL_PAD + c0, DIL_PAD + c0 + chunk)
            c, sa, sb = c_ref[rows, :], sa_ref[rows, :], sb_ref[rows, :]
            dq_ref[rows, :] = _rope(dq_a[rows, :], c, sa, sb).astype(BF)
            dk_ref[rows, :] = _rope(dk_a[padded, :], c, sa, sb).astype(BF)
            dv_ref[rows, :] = dv_a[padded, :].astype(BF)

    full = pl.BlockSpec((s, HD), lambda h: (0, h))
    tab = pl.BlockSpec((s, HD), lambda h: (0, 0))
    return pl.pallas_call(
        body, name=name, grid=(N_HEADS,),
        in_specs=[full, pl.BlockSpec((s, HD), lambda h: (0, N_HEADS + h)), pl.BlockSpec((s, HD), lambda h: (0, 2 * N_HEADS + h)),
                  full, full, full, tab, tab, tab],
        out_specs=[full, full, full], out_shape=[_sds((s, N_HEADS * HD), BF)] * 3,
        scratch_shapes=[pltpu.VMEM((s, HD), F32)] + [pltpu.VMEM((s + DIL_PAD, HD), F32)] * 2 + [pltpu.VMEM((s, HD), F32)] * 3
        + [pltpu.VMEM((s + DIL_PAD, HD), F32)] * 2,
        compiler_params=_cp("parallel"),
    )(qkv, qkv, qkv, out, glse, do, *tables)


def _swiglu_fwd(x, gnorm, w, tag, deps=()):
    h, ht = _rmsnorm_fwd(x, gnorm, f"norm_{tag}", deps)
    g, u, act, act_t = _ffn_up(h, w["gate"], w["up"], f"ffn_up_{tag}")
    if callable(w["down"]):
        w["down"] = w["down"](g)
    y = _ffn_down(act, w["down"], x, f"ffn_down_{tag}")
    return y, (x, ht, g, u, act_t)


def _swiglu_bwd(saved, gnorm, w, dy, dyb_half, out_scale, tag, deps=(), on_down=None, on_grads=None):
    x, ht, g, u, act_t = saved
    dg, du = _ffn_bwd_act(dyb_half, w["down"], g, u, f"ffn_bwd_act_{tag}", deps, w.get("down_t"))
    if on_down:
        d_gate, d_up = _grad_cols(ht, [dg, du], [False, False], f"ffn_bwd_wgu_{tag}")
        tokens = list(on_grads({"gate": d_gate, "up": d_up}))
        d_down = _grad_rows(act_t, dyb_half, f"ffn_bwd_wd_{tag}", tokens)
        gw = {"gate": d_gate, "up": d_up, "down": d_down}
        tokens = list(on_down(d_down))
    else:
        d_down = _grad_rows(act_t, dyb_half, f"ffn_bwd_wd_{tag}")
        d_gate, d_up = _grad_cols(ht, [dg, du], [False, False], f"ffn_bwd_wgu_{tag}")
        gw = {"gate": d_gate, "up": d_up, "down": d_down}
        tokens = list(on_grads(gw)) if on_grads else []
    dh = _dh_cols([dg, du], [w["gate"], w["up"]], [False, False], f"ffn_bwd_dh_{tag}", tokens)
    dx, dxb, dxbt, dgn = _rmsnorm_bwd(x, gnorm, dh, dy, out_scale, f"norm_bwd_{tag}")
    return (dx, dxb, dxbt), dgn, gw


def kernel(x, norm_g, ffn1_w_gate, ffn1_w_up, ffn1_w_down, ffn2_w_gate, ffn2_w_up, ffn2_w_down, even_w_in, even_b_forget, even_w_out, odd_w_qkv, odd_w_out, final_norm_g, loss_target, m_norm_g, m_ffn1_w_gate, m_ffn1_w_up, m_ffn1_w_down, m_ffn2_w_gate, m_ffn2_w_up, m_ffn2_w_down, m_even_w_in, m_even_b_forget, m_even_w_out, m_odd_w_qkv, m_odd_w_out, m_final_norm_g, v_norm_g, v_ffn1_w_gate, v_ffn1_w_up, v_ffn1_w_down, v_ffn2_w_gate, v_ffn2_w_up, v_ffn2_w_down, v_even_w_in, v_even_b_forget, v_even_w_out, v_odd_w_qkv, v_odd_w_out, v_final_norm_g):
    s, d = x.shape[1], x.shape[2]
    nfox = N_HEADS - N_SB
    ax, ay, ac = lax.axis_index("x"), lax.axis_index("y"), lax.axis_index("c")
    me = 4 * ax + 2 * ay + ac
    slots = jnp.stack([4 * px + 2 * py + ac for px, py in [(ax, ay), (1 - ax, ay), (ax, 1 - ay), (1 - ax, 1 - ay)]]).astype(jnp.int32)
    x0 = x.reshape(s, d)
    target = loss_target.reshape(s, d)

    def bf(w):
        return w.astype(BF)

    groups = [
        [bf(ffn1_w_gate[0]), bf(ffn1_w_up[0]), norm_g.reshape(6, d // NDEV)],
        [bf(even_w_in[0]), bf(even_w_out[0])],
        [bf(ffn2_w_gate[0]), bf(ffn2_w_up[0]), bf(ffn2_w_down[0])],
        [bf(ffn1_w_gate[1]), bf(ffn1_w_up[1]), bf(ffn1_w_down[1])],
        [bf(odd_w_qkv[0]), bf(odd_w_out[0])],
        [bf(ffn2_w_gate[1]), bf(ffn2_w_up[1]), bf(ffn2_w_down[1])],
        [bf(ffn1_w_down[0])],
    ]
    started = [None] * len(groups)
    last_token = []
    for k in (0, 6, 1, 2, 3, 4, 5):
        started[k] = _gather_start(groups[k], me, last_token, f"gather_start_{k}")
        last_token = [started[k]["token"]]
    all_started = last_token

    def forward_early(k, after):
        started[k] = _gather_forward(started[k], after, f"gather_forward_{k}")
        return [started[k]["token"]]

    def gathered(k, after, fill=None):
        wait_after = after
        if fill is not None:
            fill["down_t"] = wait_after = _transpose_shards(fill["down"], after, f"down_t_{k}")
        st = started[k] if "send2" in started[k] else _gather_forward(started[k], wait_after, f"gather_forward_{k}")
        return _gather_finish(st, after, f"gather_finish_{k}")

    def ffn_weights(ws_):
        return {"gate": ws_[0], "up": ws_[1], "down": ws_[2]}

    b_pad = jnp.pad(even_b_forget, ((0, 0), (0, HD - nfox)))
    gfin = final_norm_g.reshape(1, d)

    g0 = gathered(0, x0)
    gn = jnp.transpose(g0[2], (1, 0, 2)).reshape(6, 1, d)
    wf = [[{"gate": g0[0], "up": g0[1], "down": lambda after: gathered(6, after)[0]}, None], [None, None]]
    x1, sv_f1_0 = _swiglu_fwd(x0, gn[0], wf[0][0], "l0a", all_started)
    g1 = gathered(1, x1, fill=wf[0][0])
    w_in_nat = jnp.transpose(g1[0], (1, 0, 2)).reshape(d, -1)
    w_qkv_e = w_in_nat[:, :3 * d]
    w_f = jnp.pad(w_in_nat[:, 3 * d:], ((0, 0), (0, HD - nfox)))
    w_out_e = g1[1].reshape(d, d)
    h_e, ht_e = _rmsnorm_fwd(x1, gn[1], "norm_l0m")
    qkv_e = _mm_nn(h_e, w_qkv_e, 768, BF, "even_qkv")
    f_e = _mm_nn(h_e, w_f, HD, F32, "even_fgate")
    o_sb, ot_sb, tot_sb = _sb_fwd(qkv_e, "sb_fwd")
    cum_b, cum_t = _fgate_fwd(f_e, b_pad, "fgate_fwd")
    o_fox, ot_fox, lse_fox = _fox_fwd(qkv_e, cum_b, cum_t, "fox_fwd")
    o_e = jnp.concatenate([o_sb, o_fox], axis=1)
    ot_e = jnp.concatenate([ot_sb, ot_fox], axis=0)
    x2 = _mm_nn(o_e, w_out_e, 1024, F32, "even_out", res=x1, deps=forward_early(2, o_e))
    wf[0][1] = ffn_weights(gathered(2, x2))
    x3, sv_f2_0 = _swiglu_fwd(x2, gn[2], wf[0][1], "l0b")

    wf[1][0] = ffn_weights(gathered(3, x3, fill=wf[0][1]))
    x4, sv_f1_1 = _swiglu_fwd(x3, gn[3], wf[1][0], "l1a")
    g4 = gathered(4, x4)
    w_qkv_o = g4[0]
    w_out_o = g4[1].reshape(d, d)
    h_o, ht_o = _rmsnorm_fwd(x4, gn[4], "norm_l1m")
    qkv_o = _qkv_rope(h_o, w_qkv_o, _rope_tables(s, 1.0), "odd_qkv")
    o_o, ot_o, glse = _dilated_fwd(qkv_o, "dilated_fwd")
    x5 = _mm_nn(o_o, w_out_o, 1024, F32, "odd_out", res=x4)
    wf[1][1] = ffn_weights(gathered(5, x5, fill=wf[1][0]))
    x6, sv_f2_1 = _swiglu_fwd(x5, gn[5], wf[1][1], "l1b")

    def chip_sums(gs, a_s, tag):
        ps = _pair_sum(gs, a_s, slots, f"pair_sum_{tag}")
        return gs, a_s, _chip_start(ps, f"chip_start_{tag}")

    def as_slices(gs):
        return [g_ if g_.ndim == 3 else g_.reshape(NDEV, g_.shape[0] // NDEV, g_.shape[1]) for g_ in gs]

    def reduce_start(gs, tag):
        gs = as_slices(gs)
        return chip_sums(gs, _pair_exchange(gs, f"pair_exchange_{tag}"), tag)

    red, crossing = {}, {}

    def cross(gs, tag):
        crossing[tag] = _pair_start(as_slices(gs), f"pair_start_{tag}")
        return [crossing[tag]["token"]]

    def reduce_behind_dh(tag):
        return lambda gw: cross([gw["gate"], gw["up"], gw["down"]], tag)

    def reduce_after(tag, after):
        red[tag] = chip_sums(*_pair_finish(crossing[tag], after, f"pair_finish_{tag}"), tag)
        return [red[tag][2]["token"]]

    def reduce_now(tag, names):
        def hook(gw):
            red[tag] = reduce_start([gw[nm] for nm in names] if names else [gw], tag)
            return [red[tag][2]["token"]]
        return hook

    dx6, dx6b, _, d_gfin, loss_part = _loss_head(x6, gfin, target, "loss_head")

    (dx5, dx5b, dx5bt), dgn5, _ = _swiglu_bwd(sv_f2_1, gn[5], wf[1][1], dx6, dx6b, 1.0, "l1b", on_grads=reduce_behind_dh("l1b"))
    d_wout_o = _mm_nn(ot_o, dx5b, 1024, BF, "odd_out_dw")
    do_o = _mm_nt([(dx5b, w_out_o)], "odd_out_do", BF, deps=reduce_after("l1b", dx5))
    dqkv_o = jnp.concatenate(_dilated_bwd(qkv_o, o_o, glse, do_o, _rope_tables(s, -1.0), "dilated_bwd"), axis=1)
    (d_wqkv_o,) = _grad_cols(ht_o, [dqkv_o], [True], "odd_qkv_dw")
    dh_o = _dh_cols([dqkv_o], [w_qkv_o], [True], "odd_qkv_dh", cross([d_wqkv_o, d_wout_o], "l1m"))
    dx4, dx4b, _, dgn4 = _rmsnorm_bwd(x4, gn[4], dh_o, dx5, 0.5, "norm_bwd_l1m")
    (dx3, dx3b, _), dgn3, _ = _swiglu_bwd(sv_f1_1, gn[3], wf[1][0], dx4, dx4b, 0.5, "l1a", reduce_after("l1m", dx4),
                                         on_grads=reduce_behind_dh("l1a"))

    (dx2, dx2b, dx2bt), dgn2, _ = _swiglu_bwd(sv_f2_0, gn[2], wf[0][1], dx3, dx3b, 1.0, "l0b", reduce_after("l1a", dx3),
                                             on_grads=reduce_behind_dh("l0b"))
    d_wout_e = _mm_nn(ot_e, dx2b, 1024, BF, "even_out_dw")
    do_e = _mm_nt([(dx2b, w_out_e)], "even_out_do", BF, deps=reduce_after("l0b", dx2))
    dq_sb, dk_sb, dv_sb = _sb_bwd(qkv_e, do_e, tot_sb, "sb_bwd")
    dq_fx, dk_fx, dv_fx, dcq, dck = _fox_bwd(qkv_e, cum_b, cum_t, o_fox, lse_fox, do_e, "fox_bwd")
    df, db_part = _fgate_bwd(dcq, dck, f_e, b_pad, "fgate_bwd")
    dqkv_e = jnp.concatenate([dq_sb, dq_fx, dk_sb, dk_fx, dv_sb, dv_fx], axis=1)
    d_wqkv_e = _mm_nn(ht_e, dqkv_e, 768, BF, "even_qkv_dw")
    d_wf = _mm_nn(ht_e, df, HD, BF, "even_fgate_dw")
    d_win_nat = jnp.concatenate([d_wqkv_e, d_wf[:, :nfox]], axis=1)
    d_win = jnp.transpose(d_win_nat.reshape(d, NDEV, -1), (1, 0, 2))
    dh_e = _mm_nt([(dqkv_e, w_qkv_e), (df, w_f)], "even_in_dh", deps=cross([d_win, d_wout_e], "l0m"))
    dx1, dx1b, _, dgn1 = _rmsnorm_bwd(x1, gn[1], dh_e, dx2, 0.5, "norm_bwd_l0m")
    (dx0, _, _), dgn0, _ = _swiglu_bwd(sv_f1_0, gn[0], wf[0][0], dx1, dx1b, 1.0, "l0a", reduce_after("l0m", dx1),
                                      on_down=reduce_now("l0a_down", None), on_grads=reduce_now("l0a_gu", ["gate", "up"]))

    def reduce_finish(red, tag, after):
        gs, a_s, st = red
        return list(zip(gs, a_s, _chip_finish(st, after, f"chip_finish_{tag}")))

    f_l1b, f_l1m, f_l1a = (reduce_finish(red[t], t, dx0) for t in ("l1b", "l1m", "l1a"))
    f_l0b, f_l0m = (reduce_finish(red[t], t, dx0) for t in ("l0b", "l0m"))

    def update(w_, m_, v_, parts, nm):
        if w_.shape[2] % 128 == 0:
            return _adamw_sharded(w_, m_, v_, parts, slots, f"adamw_{nm}")
        outs = _adamw_sharded(jnp.swapaxes(w_, 1, 2), jnp.swapaxes(m_, 1, 2), jnp.swapaxes(v_, 1, 2), parts, slots,
                              f"adamw_{nm}", transposed=True)
        return [jnp.swapaxes(o, 1, 2) for o in outs]

    res = {}
    res["even_w_in"] = update(even_w_in, m_even_w_in, v_even_w_in, [f_l0m[0]], "even_w_in")
    res["even_w_out"] = _adamw_sharded(even_w_out, m_even_w_out, v_even_w_out, [f_l0m[1]], slots, "adamw_even_w_out")
    res["odd_w_qkv"] = _adamw_sharded(odd_w_qkv, m_odd_w_qkv, v_odd_w_qkv, [f_l1m[0]], slots, "adamw_odd_w_qkv")
    res["odd_w_out"] = _adamw_sharded(odd_w_out, m_odd_w_out, v_odd_w_out, [f_l1m[1]], slots, "adamw_odd_w_out")
    names = ["ffn2_w_gate", "ffn2_w_up", "ffn2_w_down", "ffn1_w_gate", "ffn1_w_up", "ffn1_w_down"]
    ws = [ffn2_w_gate, ffn2_w_up, ffn2_w_down, ffn1_w_gate, ffn1_w_up, ffn1_w_down]
    ms = [m_ffn2_w_gate, m_ffn2_w_up, m_ffn2_w_down, m_ffn1_w_gate, m_ffn1_w_up, m_ffn1_w_down]
    vs = [v_ffn2_w_gate, v_ffn2_w_up, v_ffn2_w_down, v_ffn1_w_gate, v_ffn1_w_up, v_ffn1_w_down]
    for k in range(3):
        res[names[k]] = update(ws[k], ms[k], vs[k], [f_l0b[k], f_l1b[k]], names[k])
    f_l0a = (reduce_finish(red["l0a_gu"], "l0a_gu", res["ffn2_w_down"][1])
             + reduce_finish(red["l0a_down"], "l0a_down", res["ffn2_w_down"][1]))
    for k in range(3, 6):
        res[names[k]] = update(ws[k], ms[k], vs[k], [f_l0a[k - 3], f_l1a[k - 3]], names[k])

    dnorm = jnp.concatenate([dgn0, dgn1, dgn2, dgn3, dgn4, dgn5], axis=0)
    nsm = d // NDEV
    small_rows = (6 * d + d + 2 * HD) // HD
    pad_rows = -small_rows % 8
    part = jnp.concatenate([dnorm.reshape(-1), d_gfin.reshape(-1), db_part.reshape(-1), loss_part.reshape(-1),
                            jnp.zeros((pad_rows * HD,), F32)]).reshape(small_rows + pad_rows, HD)
    (gathered,) = _all_gather([part], "gather_small")

    def pack(ng, bfg, fg):
        full = lax.dynamic_update_slice(jnp.zeros((6, d), F32), ng.reshape(6, nsm), (0, me * nsm))
        return jnp.concatenate([full.reshape(-1), fg.reshape(-1), jnp.pad(bfg.reshape(-1), (0, HD - nfox)),
                                jnp.zeros((HD + pad_rows * HD,), F32)]).reshape(small_rows + pad_rows, HD)

    sm = _adamw_small(pack(norm_g, even_b_forget, final_norm_g), pack(m_norm_g, m_even_b_forget, m_final_norm_g),
                      pack(v_norm_g, v_even_b_forget, v_final_norm_g), gathered, "adamw_small")

    def unpack(t):
        flat = t.reshape(-1)
        ng = lax.dynamic_slice(flat[:6 * d].reshape(6, d), (0, me * nsm), (6, nsm)).reshape(norm_g.shape)
        fg = flat[6 * d:7 * d].reshape(final_norm_g.shape)
        bfg = flat[7 * d:7 * d + nfox].reshape(even_b_forget.shape)
        return ng, bfg, fg

    sm_g, sm_d, sm_m, sm_v = [unpack(t) for t in sm]
    loss = sm[0].reshape(-1)[7 * d + HD]

    order = ["norm_g", "ffn1_w_gate", "ffn1_w_up", "ffn1_w_down", "ffn2_w_gate", "ffn2_w_up", "ffn2_w_down", "even_w_in",
             "even_b_forget", "even_w_out", "odd_w_qkv", "odd_w_out", "final_norm_g"]
    outs = [loss, dx0.reshape(x.shape)]
    for k in range(4):
        smk = [sm_g, sm_d, sm_m, sm_v][k]
        for nm in order:
            if nm == "norm_g":
                outs.append(smk[0])
            elif nm == "even_b_forget":
                outs.append(smk[1])
            elif nm == "final_norm_g":
                outs.append(smk[2])
            else:
                outs.append(res[nm][k])
    return tuple(outs)
```

```python
import functools

import jax
import jax.numpy as jnp
import numpy as np
from jax import lax
from jax.experimental import pallas as pl
from jax.experimental.pallas import tpu as pltpu

F32 = jnp.float32
BF = jnp.bfloat16
NDEV = 8
HD = 128
QB = 128
N_HEADS = 16
N_SB = 8
SCALE = HD ** -0.5
ROPE_THETA = 500000.0
ROPE_DIMS = HD // 4
DILATED_PATTERNS = ((128, 1), (512, 4), (2048, 16))
RMS_EPS = 1e-6
NEG_INF = -1e30
ADAM_LR = 0.001
ADAM_B1 = 0.9
ADAM_B2 = 0.999
ADAM_EPS = 1e-08
ADAM_WD = 0.01
ADAM_STEP = 10
VMEM_LIMIT_V7X = 56 * 1024 * 1024
MESH = pl.DeviceIdType.MESH
ANY = pl.BlockSpec(memory_space=pl.ANY)

NT_DIMS = (((1,), (1,)), ((), ()))


def _cp(*dims):
    return pltpu.CompilerParams(dimension_semantics=dims if dims else None, vmem_limit_bytes=VMEM_LIMIT_V7X)


def _dot(a, b):
    return jnp.dot(a, b, preferred_element_type=F32)


def _dot_nt(a, b):
    return lax.dot_general(a, b, NT_DIMS, preferred_element_type=F32)


def _sds(shape, dtype):
    return jax.ShapeDtypeStruct(shape, dtype)


def _place():
    x, y, c = lax.axis_index("x"), lax.axis_index("y"), lax.axis_index("c")
    chips = [(x, y), (1 - x, y), (x, 1 - y), (1 - x, 1 - y)]
    return x, y, c, chips


def _all_gather(xs, name):
    n = len(xs)

    def body(*refs):
        x_refs, out_refs = refs[:n], refs[n:2 * n]
        send_sems, recv_sems, local_sems = refs[2 * n:]
        x, y, c, chips = _place()
        me, sibling = (x, y, c), (x, y, 1 - c)
        others = chips[1:]

        def slot(a, px, py, pc):
            return out_refs[a].at[4 * px + 2 * py + pc]

        def copy(a, k, block, to, src=None):
            return pltpu.make_async_remote_copy(
                src_ref=slot(a, *block) if src is None else src, dst_ref=slot(a, *block),
                send_sem=send_sems.at[a, k], recv_sem=recv_sems.at[a, k], device_id=to, device_id_type=MESH)

        started = []
        for a in range(n):
            mine = pltpu.make_async_copy(x_refs[a], slot(a, *me), local_sems.at[a])
            mine.start()
            first = [copy(a, 0, me, sibling, src=x_refs[a])]
            first += [copy(a, 1 + j, me, (*chip, c), src=x_refs[a]) for j, chip in enumerate(others)]
            for cp in first:
                cp.start()
            started += [mine.wait] + [cp.wait_send for cp in first]
        for a in range(n):
            for j, chip in enumerate(others):
                copy(a, 1 + j, (*chip, c), me).wait_recv()
                passed = copy(a, 4 + j, (*chip, c), sibling)
                passed.start()
                started.append(passed.wait_send)
        for a in range(n):
            copy(a, 0, sibling, me).wait_recv()
            for j, chip in enumerate(others):
                copy(a, 4 + j, (*chip, 1 - c), me).wait_recv()
        for w in started:
            w()

    return pl.pallas_call(
        body, name=name,
        out_shape=[_sds((NDEV,) + x.shape, x.dtype) for x in xs],
        in_specs=[ANY] * n, out_specs=[ANY] * n,
        scratch_shapes=[pltpu.SemaphoreType.DMA((n, 7)), pltpu.SemaphoreType.DMA((n, 7)), pltpu.SemaphoreType.DMA((n,))],
    )(*xs)


def _pair_exchange(gs, name):
    n = len(gs)

    def body(*refs):
        g_refs, a_refs = refs[:n], refs[n:2 * n]
        send_sems, recv_sems = refs[2 * n:]
        x, y, c, chips = _place()
        copies = []
        for a in range(n):
            for j, (px, py) in enumerate(chips):
                copies.append(pltpu.make_async_remote_copy(
                    src_ref=g_refs[a].at[4 * px + 2 * py + (1 - c)], dst_ref=a_refs[a].at[j],
                    send_sem=send_sems.at[a, j], recv_sem=recv_sems.at[a, j],
                    device_id=(x, y, 1 - c), device_id_type=MESH))
        for cp in copies:
            cp.start()
        for cp in copies:
            cp.wait()

    return pl.pallas_call(
        body, name=name,
        out_shape=[_sds((4,) + g.shape[1:], g.dtype) for g in gs],
        in_specs=[ANY] * n, out_specs=[ANY] * n,
        scratch_shapes=[pltpu.SemaphoreType.DMA((n, 4)), pltpu.SemaphoreType.DMA((n, 4))],
    )(*gs)


HBM = pl.BlockSpec(memory_space=pltpu.HBM)
SEM = pl.BlockSpec(memory_space=pltpu.SEMAPHORE)
EFFECT = pltpu.SideEffectType.DATAFLOW_SIDE_EFFECTING
TOKEN = _sds((8, 128), F32)
TOKEN_SPEC = pl.BlockSpec((8, 128), lambda *_: (0, 0))


def _in_hbm(x):
    return pltpu.with_memory_space_constraint(x, pltpu.HBM)


def _ignore_deps(body, n_in, n_deps):
    if not n_deps:
        return body
    return lambda *refs: body(*refs[:n_in], *refs[n_in + n_deps:])


def _slot_of(px, py, pc):
    return 4 * px + 2 * py + pc


def _gather_start(xs, me, deps, name):
    n = len(xs)
    lands = [lax.dynamic_update_slice(lax.empty((NDEV,) + x.shape, x.dtype), x[None], (me,) + (0,) * x.ndim) for x in xs]

    def body(*refs):
        x_refs, land_refs = refs[:n], refs[n:2 * n]
        send, recv_ici, recv_sib = refs[2 * n:2 * n + 3]
        token = refs[4 * n + 3]
        x, y, c, chips = _place()
        for a in range(n):
            dst = land_refs[a].at[_slot_of(x, y, c)]
            pltpu.make_async_remote_copy(src_ref=x_refs[a], dst_ref=dst, send_sem=send.at[4 * a], recv_sem=recv_sib.at[a],
                                         device_id=(x, y, 1 - c), device_id_type=MESH).start()
            for j, chip in enumerate(chips[1:]):
                pltpu.make_async_remote_copy(src_ref=x_refs[a], dst_ref=dst, send_sem=send.at[4 * a + 1 + j], recv_sem=recv_ici.at[3 * a + j],
                                             device_id=(*chip, c), device_id_type=MESH).start()
        token[...] = jnp.zeros_like(token)

    outs = pl.pallas_call(
        _ignore_deps(body, 2 * n, len(deps)), name=name,
        out_shape=(pltpu.SemaphoreType.DMA((4 * n,)), pltpu.SemaphoreType.DMA((3 * n,)), pltpu.SemaphoreType.DMA((n,)),
                   *[pltpu.HBM(x.shape, x.dtype) for x in xs], *[pltpu.HBM(l.shape, l.dtype) for l in lands], TOKEN),
        in_specs=[HBM] * (2 * n) + [TOKEN_SPEC] * len(deps),
        out_specs=(SEM, SEM, SEM, *[HBM] * (2 * n), pl.BlockSpec(memory_space=pltpu.VMEM)),
        input_output_aliases={a: 3 + a for a in range(2 * n)},
        compiler_params=pltpu.CompilerParams(has_side_effects=EFFECT),
    )(*[_in_hbm(x) for x in xs], *[_in_hbm(l) for l in lands], *deps)
    send, recv_ici, recv_sib = outs[:3]
    return dict(send=send, recv_ici=recv_ici, recv_sib=recv_sib, xs=list(outs[3:3 + n]), lands=list(outs[3 + n:3 + 2 * n]), token=outs[-1])


def _gather_forward(st, after, name):
    n = len(st["lands"])

    def body(*refs):
        land_refs, recv_ici = refs[:n], refs[n]
        send2, recv2, token = refs[n + 2], refs[n + 3], refs[2 * n + 4]
        x, y, c, chips = _place()
        for a in range(n):
            for j, chip in enumerate(chips[1:]):
                blk = land_refs[a].at[_slot_of(*chip, c)]
                pltpu.make_async_remote_copy(src_ref=blk, dst_ref=blk, send_sem=send2.at[3 * a + j], recv_sem=recv_ici.at[3 * a + j],
                                             device_id=(*chip, c), device_id_type=MESH).wait_recv()
                pltpu.make_async_remote_copy(src_ref=blk, dst_ref=blk, send_sem=send2.at[3 * a + j], recv_sem=recv2.at[3 * a + j],
                                             device_id=(x, y, 1 - c), device_id_type=MESH).start()
        token[...] = jnp.zeros_like(token)

    outs = pl.pallas_call(
        body, name=name,
        out_shape=(pltpu.SemaphoreType.DMA((3 * n,)), pltpu.SemaphoreType.DMA((3 * n,)), *[pltpu.HBM(l.shape, l.dtype) for l in st["lands"]], TOKEN),
        in_specs=[HBM] * n + [SEM, pl.BlockSpec(memory_space=pl.ANY)],
        out_specs=(SEM, SEM, *[HBM] * n, pl.BlockSpec(memory_space=pltpu.VMEM)),
        input_output_aliases={a: 2 + a for a in range(n)},
        compiler_params=pltpu.CompilerParams(has_side_effects=EFFECT),
    )(*st["lands"], st["recv_ici"], after)
    return dict(st, send2=outs[0], recv2=outs[1], lands=list(outs[2:2 + n]), token=outs[-1])


def _gather_finish(st, after, name):
    n = len(st["lands"])

    def body(*refs):
        x_refs, land_refs = refs[:n], refs[n:2 * n]
        send, recv_sib, send2, recv2 = refs[2 * n:2 * n + 4]
        x, y, c, chips = _place()
        for a in range(n):
            mine = land_refs[a].at[_slot_of(x, y, c)]
            theirs = land_refs[a].at[_slot_of(x, y, 1 - c)]
            for k in range(4):
                pltpu.make_async_remote_copy(src_ref=x_refs[a], dst_ref=mine, send_sem=send.at[4 * a + k], recv_sem=recv_sib.at[a],
                                             device_id=(x, y, 1 - c), device_id_type=MESH).wait_send()
            pltpu.make_async_remote_copy(src_ref=x_refs[a], dst_ref=theirs, send_sem=send.at[4 * a], recv_sem=recv_sib.at[a],
                                         device_id=(x, y, 1 - c), device_id_type=MESH).wait_recv()
            for j, chip in enumerate(chips[1:]):
                sent = land_refs[a].at[_slot_of(*chip, c)]
                got = land_refs[a].at[_slot_of(*chip, 1 - c)]
                pltpu.make_async_remote_copy(src_ref=sent, dst_ref=sent, send_sem=send2.at[3 * a + j], recv_sem=recv2.at[3 * a + j],
                                             device_id=(x, y, 1 - c), device_id_type=MESH).wait_send()
                pltpu.make_async_remote_copy(src_ref=got, dst_ref=got, send_sem=send2.at[3 * a + j], recv_sem=recv2.at[3 * a + j],
                                             device_id=(x, y, 1 - c), device_id_type=MESH).wait_recv()

    outs = pl.pallas_call(
        body, name=name,
        out_shape=tuple(pltpu.HBM(v.shape, v.dtype) for v in st["xs"] + st["lands"]),
        in_specs=[HBM] * (2 * n) + [SEM] * 4 + [pl.BlockSpec(memory_space=pl.ANY)], out_specs=tuple([HBM] * (2 * n)),
        input_output_aliases={a: a for a in range(2 * n)},
        compiler_params=pltpu.CompilerParams(has_side_effects=EFFECT),
    )(*st["xs"], *st["lands"], st["send"], st["recv_sib"], st["send2"], st["recv2"], after)
    return list(outs[n:])


def _pair_start(gs, name):
    n = len(gs)
    lands = [lax.empty((4,) + g.shape[1:], g.dtype) for g in gs]

    def body(*refs):
        g_refs, a_refs = refs[:n], refs[n:2 * n]
        send, recv = refs[2 * n], refs[2 * n + 1]
        token = refs[4 * n + 2]
        x, y, c, chips = _place()
        for a in range(n):
            for j, (px, py) in enumerate(chips):
                pltpu.make_async_remote_copy(src_ref=g_refs[a].at[_slot_of(px, py, 1 - c)], dst_ref=a_refs[a].at[j],
                                             send_sem=send.at[4 * a + j], recv_sem=recv.at[4 * a + j],
                                             device_id=(x, y, 1 - c), device_id_type=MESH).start()
        token[...] = jnp.zeros_like(token)

    outs = pl.pallas_call(
        body, name=name,
        out_shape=(pltpu.SemaphoreType.DMA((4 * n,)), pltpu.SemaphoreType.DMA((4 * n,)),
                   *[pltpu.HBM(g.shape, g.dtype) for g in gs], *[pltpu.HBM(l.shape, l.dtype) for l in lands], TOKEN),
        in_specs=[HBM] * (2 * n), out_specs=(SEM, SEM, *[HBM] * (2 * n), pl.BlockSpec(memory_space=pltpu.VMEM)),
        input_output_aliases={a: 2 + a for a in range(2 * n)},
        compiler_params=pltpu.CompilerParams(has_side_effects=EFFECT),
    )(*[_in_hbm(g) for g in gs], *[_in_hbm(l) for l in lands])
    return dict(send=outs[0], recv=outs[1], gs=list(outs[2:2 + n]), lands=list(outs[2 + n:2 + 2 * n]), token=outs[-1])


def _pair_finish(st, after, name):
    n = len(st["gs"])

    def body(*refs):
        g_refs, a_refs = refs[:n], refs[n:2 * n]
        send, recv = refs[2 * n], refs[2 * n + 1]
        x, y, c, chips = _place()
        for a in range(n):
            for j, (px, py) in enumerate(chips):
                cp = pltpu.make_async_remote_copy(src_ref=g_refs[a].at[_slot_of(px, py, 1 - c)], dst_ref=a_refs[a].at[j],
                                                  send_sem=send.at[4 * a + j], recv_sem=recv.at[4 * a + j],
                                                  device_id=(x, y, 1 - c), device_id_type=MESH)
                cp.wait_send()
                cp.wait_recv()

    outs = pl.pallas_call(
        body, name=name,
        out_shape=tuple(pltpu.HBM(v.shape, v.dtype) for v in st["gs"] + st["lands"]),
        in_specs=[HBM] * (2 * n) + [SEM, SEM, pl.BlockSpec(memory_space=pl.ANY)], out_specs=tuple([HBM] * (2 * n)),
        input_output_aliases={a: a for a in range(2 * n)},
        compiler_params=pltpu.CompilerParams(has_side_effects=EFFECT),
    )(*st["gs"], *st["lands"], st["send"], st["recv"], after)
    return list(outs[:n]), list(outs[n:])


def _chip_start(ps, name):
    n = len(ps)
    lands = [lax.empty(p.shape, p.dtype) for p in ps]

    def body(*refs):
        p_refs, b_refs = refs[:n], refs[n:2 * n]
        send, recv = refs[2 * n], refs[2 * n + 1]
        token = refs[4 * n + 2]
        x, y, c, chips = _place()
        for a in range(n):
            for j, chip in enumerate(chips[1:]):
                pltpu.make_async_remote_copy(src_ref=p_refs[a].at[j], dst_ref=b_refs[a].at[j], send_sem=send.at[3 * a + j], recv_sem=recv.at[3 * a + j],
                                             device_id=(*chip, c), device_id_type=MESH).start()
        token[...] = jnp.zeros_like(token)

    outs = pl.pallas_call(
        body, name=name,
        out_shape=(pltpu.SemaphoreType.DMA((3 * n,)), pltpu.SemaphoreType.DMA((3 * n,)),
                   *[pltpu.HBM(p.shape, p.dtype) for p in ps], *[pltpu.HBM(p.shape, p.dtype) for p in ps], TOKEN),
        in_specs=[HBM] * (2 * n), out_specs=(SEM, SEM, *[HBM] * (2 * n), pl.BlockSpec(memory_space=pltpu.VMEM)),
        input_output_aliases={a: 2 + a for a in range(2 * n)},
        compiler_params=pltpu.CompilerParams(has_side_effects=EFFECT),
    )(*[_in_hbm(p) for p in ps], *[_in_hbm(l) for l in lands])
    return dict(send=outs[0], recv=outs[1], ps=list(outs[2:2 + n]), lands=list(outs[2 + n:2 + 2 * n]), token=outs[-1])


def _chip_finish(st, after, name):
    n = len(st["ps"])

    def body(*refs):
        p_refs, b_refs = refs[:n], refs[n:2 * n]
        send, recv = refs[2 * n], refs[2 * n + 1]
        x, y, c, chips = _place()
        for a in range(n):
            for j, chip in enumerate(chips[1:]):
                cp = pltpu.make_async_remote_copy(src_ref=p_refs[a].at[j], dst_ref=b_refs[a].at[j], send_sem=send.at[3 * a + j], recv_sem=recv.at[3 * a + j],
                                                  device_id=(*chip, c), device_id_type=MESH)
                cp.wait_send()
                cp.wait_recv()

    outs = pl.pallas_call(
        body, name=name,
        out_shape=tuple(pltpu.HBM(v.shape, v.dtype) for v in st["ps"] + st["lands"]),
        in_specs=[HBM] * (2 * n) + [SEM, SEM, pl.BlockSpec(memory_space=pl.ANY)], out_specs=tuple([HBM] * (2 * n)),
        input_output_aliases={a: a for a in range(2 * n)},
        compiler_params=pltpu.CompilerParams(has_side_effects=EFFECT),
    )(*st["ps"], *st["lands"], st["send"], st["recv"], after)
    return list(outs[n:])


def _rows_tile(r):
    for t in (512, 256, 128, 64, 32, 16):
        if r % t == 0:
            return t
    return r


PAIR_SUM_STEPS = 4


def _pair_sum(gs, a_s, slots, name):
    n = len(gs)
    trs = [g.shape[1] // PAIR_SUM_STEPS for g in gs]

    def body(slots_ref, *refs):
        for g_ref, a_ref, p_ref in zip(refs[:n], refs[n:2 * n], refs[2 * n:]):
            p_ref[...] = (g_ref[...].astype(F32) + a_ref[...].astype(F32)).astype(BF)

    def spec(g, tr, index):
        return pl.BlockSpec((None, tr, g.shape[2]), index)

    return pl.pallas_call(
        body, name=name,
        grid_spec=pltpu.PrefetchScalarGridSpec(
            num_scalar_prefetch=1, grid=(3, PAIR_SUM_STEPS),
            in_specs=[spec(g, tr, lambda j, i, s: (s[j + 1], i, 0)) for g, tr in zip(gs, trs)]
            + [spec(g, tr, lambda j, i, s: (j + 1, i, 0)) for g, tr in zip(gs, trs)],
            out_specs=[spec(g, tr, lambda j, i, s: (j, i, 0)) for g, tr in zip(gs, trs)]),
        out_shape=[_sds((3,) + g.shape[1:], BF) for g in gs], compiler_params=_cp("parallel", "parallel"),
    )(slots, *gs, *a_s)


def _adamw_math(w, g, m, v):
    m = ADAM_B1 * m + (1.0 - ADAM_B1) * g
    v = ADAM_B2 * v + (1.0 - ADAM_B2) * (g * g)
    m_hat = m / (1.0 - ADAM_B1 ** ADAM_STEP)
    v_hat = v / (1.0 - ADAM_B2 ** ADAM_STEP)
    delta = -ADAM_LR * (m_hat / (jnp.sqrt(v_hat) + ADAM_EPS) + ADAM_WD * w)
    return delta, m, v


def _adamw_sharded(w, m, v, parts, slots, name, transposed=False):
    nl = w.shape[0]
    r, c = parts[0][0].shape[1:]
    tr = _rows_tile(r)
    if c * tr * 4 > (1 << 21) and not transposed:
        tr = max(8, tr // 2)

    def body(slots_ref, w_ref, m_ref, v_ref, *rest):
        part_refs, (g_out, d_out, m_out, v_out) = rest[:5 * nl], rest[5 * nl:]
        layer = pl.program_id(0)
        g = None
        for l in range(nl):
            s = part_refs[5 * l][...].astype(F32)
            for ref in part_refs[5 * l + 1:5 * l + 5]:
                s = s + ref[...].astype(F32)
            g = s if g is None else jnp.where(layer == l, s, g)
        if transposed:
            g = g.T
        delta, mn, vn = _adamw_math(w_ref[...], g, m_ref[...], v_ref[...])
        g_out[...] = g
        d_out[...] = delta
        m_out[...] = mn
        v_out[...] = vn

    def own(l):
        return lambda L, i, s: (s[0], jnp.where(L == l, i, 0), 0)

    def fixed(l, k):
        return lambda L, i, s: (k, jnp.where(L == l, i, 0), 0)

    if transposed:
        wspec = pl.BlockSpec((None, c, tr), lambda L, i, s: (L, 0, i))
    else:
        wspec = pl.BlockSpec((None, tr, c), lambda L, i, s: (L, i, 0))
    in_specs = [wspec, wspec, wspec]
    args = [w, m, v]
    for l, (g, a, b) in enumerate(parts):
        in_specs += [pl.BlockSpec((None, tr, c), own(l)), pl.BlockSpec((None, tr, c), fixed(l, 0)),
                     pl.BlockSpec((None, tr, c), fixed(l, 0)), pl.BlockSpec((None, tr, c), fixed(l, 1)),
                     pl.BlockSpec((None, tr, c), fixed(l, 2))]
        args += [g, a, b, b, b]
    return pl.pallas_call(
        body, name=name,
        grid_spec=pltpu.PrefetchScalarGridSpec(
            num_scalar_prefetch=1, grid=(nl, r // tr), in_specs=in_specs, out_specs=[wspec] * 4),
        out_shape=[_sds(w.shape, F32)] * 4, compiler_params=_cp("arbitrary", "arbitrary"),
    )(slots, *args)


def _adamw_small(w, m, v, gathered, name):
    def body(w_ref, m_ref, v_ref, gg_ref, g_out, d_out, m_out, v_out):
        g = gg_ref[0]
        for k in range(1, NDEV):
            g = g + gg_ref[k]
        delta, mn, vn = _adamw_math(w_ref[...], g, m_ref[...], v_ref[...])
        g_out[...] = g
        d_out[...] = delta
        m_out[...] = mn
        v_out[...] = vn

    return pl.pallas_call(body, name=name, out_shape=[_sds(w.shape, F32)] * 4)(w, m, v, gathered)


def _rmsnorm_fwd(x, g, name, deps=()):
    s, d = x.shape
    tm = 256

    def body(x_ref, g_ref, h_ref, ht_ref):
        xf = x_ref[...]
        y = xf * lax.rsqrt(jnp.mean(xf * xf, axis=-1, keepdims=True) + RMS_EPS)
        h = y * g_ref[...]
        h_ref[...] = h.astype(BF)
        ht_ref[...] = h.T.astype(BF)

    return pl.pallas_call(
        _ignore_deps(body, 2, len(deps)), name=name, grid=(s // tm,),
        in_specs=[pl.BlockSpec((tm, d), lambda i: (i, 0)), pl.BlockSpec((1, d), lambda i: (0, 0))] + [TOKEN_SPEC] * len(deps),
        out_specs=[pl.BlockSpec((tm, d), lambda i: (i, 0)), pl.BlockSpec((d, tm), lambda i: (0, i))],
        out_shape=[_sds((s, d), BF), _sds((d, s), BF)], compiler_params=_cp("parallel"),
    )(x, g, *deps)


def _rmsnorm_bwd(x, g, dh, dres, out_scale, name):
    s, d = x.shape
    tm = 256

    def body(x_ref, g_ref, dh_ref, dres_ref, dx_ref, dxb_ref, dxbt_ref, dg_ref):
        xf = x_ref[...]
        r = lax.rsqrt(jnp.mean(xf * xf, axis=-1, keepdims=True) + RMS_EPS)
        xhat = xf * r
        dhv = dh_ref[...]
        dxhat = dhv * g_ref[...]
        dx = dres_ref[...] + r * (dxhat - xhat * jnp.mean(dxhat * xhat, axis=-1, keepdims=True))
        dx_ref[...] = dx
        scaled = dx * out_scale
        dxb_ref[...] = scaled.astype(BF)
        dxbt_ref[...] = scaled.T.astype(BF)

        @pl.when(pl.program_id(0) == 0)
        def _():
            dg_ref[...] = jnp.zeros_like(dg_ref)

        dg_ref[...] += jnp.sum(dhv * xhat, axis=0, keepdims=True)

    row = pl.BlockSpec((tm, d), lambda i: (i, 0))
    vec = pl.BlockSpec((1, d), lambda i: (0, 0))
    return pl.pallas_call(
        body, name=name, grid=(s // tm,),
        in_specs=[row, vec, row, row],
        out_specs=[row, row, pl.BlockSpec((d, tm), lambda i: (0, i)), vec],
        out_shape=[_sds((s, d), F32), _sds((s, d), BF), _sds((d, s), BF), _sds((1, d), F32)],
        compiler_params=_cp("arbitrary"),
    )(x, g, dh, dres)


def _loss_head(x, g, target, name):
    s, d = x.shape
    tm = 256

    def body(x_ref, g_ref, t_ref, dx_ref, dxb_ref, dxbt_ref, dg_ref, loss_ref):
        xf = x_ref[...]
        r = lax.rsqrt(jnp.mean(xf * xf, axis=-1, keepdims=True) + RMS_EPS)
        xhat = xf * r
        err = xhat * g_ref[...] - t_ref[...]
        dy = err * (1.0 / d)
        dxhat = dy * g_ref[...]
        dx = r * (dxhat - xhat * jnp.mean(dxhat * xhat, axis=-1, keepdims=True))
        dx_ref[...] = dx
        half = dx * 0.5
        dxb_ref[...] = half.astype(BF)
        dxbt_ref[...] = half.T.astype(BF)

        @pl.when(pl.program_id(0) == 0)
        def _():
            dg_ref[...] = jnp.zeros_like(dg_ref)
            loss_ref[...] = jnp.zeros_like(loss_ref)

        dg_ref[...] += jnp.sum(dy * xhat, axis=0, keepdims=True)
        part = 0.5 * jnp.sum(jnp.mean(err * err, axis=-1, keepdims=True), axis=0, keepdims=True)
        lane = lax.broadcasted_iota(jnp.int32, (1, 128), 1)
        loss_ref[...] += jnp.where(lane == 0, part, 0.0)

    row = pl.BlockSpec((tm, d), lambda i: (i, 0))
    vec = pl.BlockSpec((1, d), lambda i: (0, 0))
    return pl.pallas_call(
        body, name=name, grid=(s // tm,),
        in_specs=[row, vec, row],
        out_specs=[row, row, pl.BlockSpec((d, tm), lambda i: (0, i)), vec, pl.BlockSpec((1, 128), lambda i: (0, 0))],
        out_shape=[_sds((s, d), F32), _sds((s, d), BF), _sds((d, s), BF), _sds((1, d), F32), _sds((1, 128), F32)],
        compiler_params=_cp("arbitrary"),
    )(x, g, target)


def _act_spec(tm, n, natural, order):
    if natural:
        return pl.BlockSpec((tm, n), (lambda s, i: (i, s)) if order == "si" else (lambda i, s: (i, s)))
    return pl.BlockSpec((None, tm, n), (lambda s, i: (s, i, 0)) if order == "si" else (lambda i, s: (s, i, 0)))


def _act_shape(s, n, natural, dtype):
    return _sds((s, NDEV * n), dtype) if natural else _sds((NDEV, s, n), dtype)


def _ffn_up(h, wg, wu, name):
    s, d = h.shape
    n = wg.shape[2]
    tm = 1024

    def body(h_ref, wg_ref, wu_ref, g_ref, u_ref, a_ref, at_ref):
        hb = h_ref[...]
        g = _dot(hb, wg_ref[...])
        u = _dot(hb, wu_ref[...])
        sig = jax.nn.sigmoid(g)
        silu = g * sig
        g_ref[...] = (u * (sig * (1.0 + g * (1.0 - sig)))).astype(BF)
        u_ref[...] = silu.astype(BF)
        act = silu * u
        a_ref[...] = act.astype(BF)
        at_ref[...] = act.T.astype(BF)

    wsp = pl.BlockSpec((None, d, n), lambda s_, i: (s_, 0, 0))
    blk = _act_spec(tm, n, False, "si")
    return pl.pallas_call(
        body, name=name, grid=(NDEV, s // tm),
        in_specs=[pl.BlockSpec((tm, d), lambda s_, i: (i, 0)), wsp, wsp],
        out_specs=[blk] * 3 + [pl.BlockSpec((None, n, tm), lambda s_, i: (s_, 0, i))],
        out_shape=[_act_shape(s, n, False, BF)] * 3 + [_sds((NDEV, n, s), BF)],
        compiler_params=_cp("parallel", "parallel"),
    )(h, wg, wu)


def _ffn_down(act, wd, x, name):
    _, s, n = act.shape
    d = wd.shape[2]
    tm = 512

    def body(a_ref, w_ref, x_ref, o_ref, acc):
        k = pl.program_id(1)

        @pl.when(k == 0)
        def _():
            acc[...] = jnp.zeros_like(acc)

        acc[...] += _dot(a_ref[...], w_ref[...])

        @pl.when(k == NDEV - 1)
        def _():
            o_ref[...] = x_ref[...] + 0.5 * acc[...]

    row = pl.BlockSpec((tm, d), lambda i, k: (i, 0))
    return pl.pallas_call(
        body, name=name, grid=(s // tm, NDEV),
        in_specs=[_act_spec(tm, n, False, "is"), pl.BlockSpec((None, n, d), lambda i, k: (k, 0, 0)), row],
        out_specs=row, out_shape=_sds((s, d), F32),
        scratch_shapes=[pltpu.VMEM((tm, d), F32)], compiler_params=_cp("parallel", "arbitrary"),
    )(act, wd, x)


def _ffn_bwd_act(dyb, wd, g, u, name, deps=()):
    s, d = dyb.shape
    n = wd.shape[1]
    tm = 1024

    def body(dy_ref, w_ref, g_ref, u_ref, dg_ref, du_ref):
        dact = _dot_nt(dy_ref[...], w_ref[...])
        dg_ref[...] = (dact * g_ref[...].astype(F32)).astype(BF)
        du_ref[...] = (dact * u_ref[...].astype(F32)).astype(BF)

    blk = _act_spec(tm, n, False, "is")
    return pl.pallas_call(
        _ignore_deps(body, 4, len(deps)), name=name, grid=(s // tm, NDEV),
        in_specs=[pl.BlockSpec((tm, d), lambda i, s_: (i, 0)), pl.BlockSpec((None, n, d), lambda i, s_: (s_, 0, 0)), blk, blk]
        + [TOKEN_SPEC] * len(deps),
        out_specs=[blk, blk], out_shape=[_act_shape(s, n, False, BF)] * 2,
        compiler_params=_cp("parallel", "parallel"),
    )(dyb, wd, g, u, *deps)


def _grad_rows(act_t, dyb, name, deps=()):
    _, n, s = act_t.shape
    d = dyb.shape[1]
    tn = 2048

    def body(a_ref, dy_ref, o_ref):
        o_ref[...] = _dot(a_ref[...], dy_ref[...]).astype(BF)

    return pl.pallas_call(
        _ignore_deps(body, 2, len(deps)), name=name, grid=(NDEV, d // tn),
        in_specs=[pl.BlockSpec((None, n, s), lambda k, j: (k, 0, 0)), pl.BlockSpec((s, tn), lambda k, j: (0, j))]
        + [TOKEN_SPEC] * len(deps),
        out_specs=pl.BlockSpec((None, n, tn), lambda k, j: (k, 0, j)), out_shape=_sds((NDEV, n, d), BF),
        compiler_params=_cp("parallel", "parallel"),
    )(act_t, dyb, *deps)


def _grad_cols(ht, dxs, naturals, name, deps=()):
    d, s = ht.shape
    k = len(dxs)
    ns = [dx.shape[1] // NDEV if nat else dx.shape[2] for dx, nat in zip(dxs, naturals)]
    td = 1024

    def body(*refs):
        ht_ref, dx_refs, o_refs = refs[0], refs[1:1 + k], refs[1 + k:]
        hv = ht_ref[...]
        for dx_ref, o_ref in zip(dx_refs, o_refs):
            o_ref[...] = _dot(hv, dx_ref[...]).astype(BF)

    def dx_spec(n, nat):
        if nat:
            return pl.BlockSpec((s, n), lambda s_, j: (0, s_))
        return pl.BlockSpec((None, s, n), lambda s_, j: (s_, 0, 0))

    return pl.pallas_call(
        _ignore_deps(body, 1 + k, len(deps)), name=name, grid=(NDEV, d // td),
        in_specs=[pl.BlockSpec((td, s), lambda s_, j: (j, 0))] + [dx_spec(n, nat) for n, nat in zip(ns, naturals)]
        + [TOKEN_SPEC] * len(deps),
        out_specs=[pl.BlockSpec((None, td, n), lambda s_, j: (s_, j, 0)) for n in ns],
        out_shape=[_sds((NDEV, d, n), BF) for n in ns], compiler_params=_cp("parallel", "parallel"),
    )(ht, *dxs, *deps)


def _dh_cols(dxs, ws, naturals, name, deps=()):
    k = len(dxs)
    d = ws[0].shape[1]
    ns = [w.shape[2] for w in ws]
    s = dxs[0].shape[0] if naturals[0] else dxs[0].shape[1]
    tm = 512

    def body(*refs):
        dx_refs, w_refs, o_ref, acc = refs[:k], refs[k:2 * k], refs[2 * k], refs[2 * k + 1]
        j = pl.program_id(1)

        @pl.when(j == 0)
        def _():
            acc[...] = jnp.zeros_like(acc)

        t = _dot_nt(dx_refs[0][...], w_refs[0][...])
        for dx_ref, w_ref in zip(dx_refs[1:], w_refs[1:]):
            t = t + _dot_nt(dx_ref[...], w_ref[...])
        acc[...] += t

        @pl.when(j == NDEV - 1)
        def _():
            o_ref[...] = acc[...]

    return pl.pallas_call(
        _ignore_deps(body, 2 * k, len(deps)), name=name, grid=(s // tm, NDEV),
        in_specs=[_act_spec(tm, n, nat, "is") for n, nat in zip(ns, naturals)]
        + [pl.BlockSpec((None, d, n), lambda i, j: (j, 0, 0)) for n in ns] + [TOKEN_SPEC] * len(deps),
        out_specs=pl.BlockSpec((tm, d), lambda i, j: (i, 0)), out_shape=_sds((s, d), F32),
        scratch_shapes=[pltpu.VMEM((tm, d), F32)], compiler_params=_cp("parallel", "arbitrary"),
    )(*dxs, *ws, *deps)


def _mm_nn(a, b, tn, out_dtype, name, res=None, tm=1024, deps=()):
    m, k = a.shape
    nn = b.shape[1]

    def body(*refs):
        if res is None:
            a_ref, b_ref, o_ref = refs
            o_ref[...] = _dot(a_ref[...], b_ref[...]).astype(out_dtype)
        else:
            a_ref, b_ref, r_ref, o_ref = refs
            o_ref[...] = (r_ref[...] + _dot(a_ref[...], b_ref[...])).astype(out_dtype)

    osp = pl.BlockSpec((tm, tn), lambda j, i: (i, j))
    in_specs = [pl.BlockSpec((tm, k), lambda j, i: (i, 0)), pl.BlockSpec((k, tn), lambda j, i: (0, j))]
    args = [a, b]
    if res is not None:
        in_specs.append(osp)
        args.append(res)
    return pl.pallas_call(
        _ignore_deps(body, len(args), len(deps)), name=name, grid=(nn // tn, m // tm),
        in_specs=in_specs + [TOKEN_SPEC] * len(deps), out_specs=osp,
        out_shape=_sds((m, nn), out_dtype), compiler_params=_cp("parallel", "parallel"),
    )(*args, *deps)


def _mm_nt(pairs, name, out_dtype=F32, tm=512, tk=512, deps=()):
    m = pairs[0][0].shape[0]
    kk = pairs[0][1].shape[0]
    p = len(pairs)

    def body(*refs):
        o_ref = refs[2 * p]
        t = _dot_nt(refs[0][...], refs[1][...])
        for q in range(1, p):
            t = t + _dot_nt(refs[2 * q][...], refs[2 * q + 1][...])
        o_ref[...] = t.astype(out_dtype)

    in_specs, args = [], []
    for a, b in pairs:
        in_specs += [pl.BlockSpec((tm, a.shape[1]), lambda j, i: (i, 0)), pl.BlockSpec((tk, b.shape[1]), lambda j, i: (j, 0))]
        args += [a, b]
    return pl.pallas_call(
        _ignore_deps(body, 2 * p, len(deps)), name=name, grid=(kk // tk, m // tm), in_specs=in_specs + [TOKEN_SPEC] * len(deps),
        out_specs=pl.BlockSpec((tm, tk), lambda j, i: (i, j)), out_shape=_sds((m, kk), out_dtype),
        compiler_params=_cp("parallel", "parallel"),
    )(*args, *deps)


def _rope_tables(s, sign):
    half = ROPE_DIMS // 2
    f32 = np.float32
    freqs = f32(ROPE_THETA) ** (-np.arange(half, dtype=f32) / f32(half))
    ang = np.arange(s, dtype=f32)[:, None] * freqs[None, :]
    cos, sin = np.cos(ang).astype(f32), (sign * np.sin(ang)).astype(f32)
    one = np.ones((s, HD - ROPE_DIMS), f32)
    zero = np.zeros((s, HD - ROPE_DIMS), f32)
    zh = np.zeros((s, half), f32)
    c = np.concatenate([cos, cos, one], axis=1)
    sa = np.concatenate([-sin, zh, zero], axis=1)
    sb = np.concatenate([zh, sin, zero], axis=1)
    return jnp.asarray(c), jnp.asarray(sa), jnp.asarray(sb)


def _rope(xv, c, sa, sb):
    return xv * c + pltpu.roll(xv, HD - ROPE_DIMS // 2, 1) * sa + pltpu.roll(xv, ROPE_DIMS // 2, 1) * sb


def _qkv_rope(h, w, tables, name):
    s, d = h.shape
    n = w.shape[2]
    per = n // HD
    tm = 1024

    def body(h_ref, w_ref, c_ref, sa_ref, sb_ref, o_ref):
        shard = pl.program_id(0)
        y = _dot(h_ref[...], w_ref[...])
        c, sa, sb = c_ref[...], sa_ref[...], sb_ref[...]
        for j in range(per):
            blk = y[:, j * HD:(j + 1) * HD]
            rot = _rope(blk, c, sa, sb)
            is_qk = shard * per + j < 2 * N_HEADS
            o_ref[:, j * HD:(j + 1) * HD] = jnp.where(is_qk, rot, blk).astype(BF)

    tab = pl.BlockSpec((tm, HD), lambda s_, i: (i, 0))
    return pl.pallas_call(
        body, name=name, grid=(NDEV, s // tm),
        in_specs=[pl.BlockSpec((tm, d), lambda s_, i: (i, 0)), pl.BlockSpec((None, d, n), lambda s_, i: (s_, 0, 0)), tab, tab, tab],
        out_specs=pl.BlockSpec((tm, n), lambda s_, i: (i, s_)), out_shape=_sds((s, NDEV * n), BF),
        compiler_params=_cp("parallel", "parallel"),
    )(h, w, *tables)


def _iota2():
    return (lax.broadcasted_iota(jnp.int32, (QB, QB), 0), lax.broadcasted_iota(jnp.int32, (QB, QB), 1))


def _softplus(z):
    return jnp.maximum(z, 0.0) + jnp.log(1.0 + jnp.exp(-jnp.abs(z)))


def _tri_dot(xv, tri, left=False):
    hi = xv.astype(BF)
    r1 = xv - hi.astype(F32)
    mid = r1.astype(BF)
    lo = (r1 - mid.astype(F32)).astype(BF)
    if left:
        return _dot(tri, hi) + _dot(tri, mid) + _dot(tri, lo)
    return _dot(hi, tri) + _dot(mid, tri) + _dot(lo, tri)


def _col(ref_or_val):
    return ref_or_val[:, 0:1]


KT = 4 * QB
QQ = 4 * QB


def _iota_tile():
    return (lax.broadcasted_iota(jnp.int32, (QQ, KT), 0), lax.broadcasted_iota(jnp.int32, (QQ, KT), 1))


def _scan_matrix(keep):
    tri = keep(*_iota2()).astype(BF)
    return jnp.concatenate([tri, tri], axis=0)


def _scan_dot(xv, tri2):
    hi = xv.astype(BF)
    lo = (xv - hi.astype(F32)).astype(BF)
    return _dot(jnp.concatenate([hi, lo], axis=1), tri2)


def _blocks(xv):
    return [xv[:, b * QB:(b + 1) * QB] for b in range(KT // QB)]


def _sb_fwd(qkv, name):
    s = qkv.shape[0]
    nb = s // QB

    def body(q_ref, k_ref, v_ref, o_ref, ot_ref, t_ref):
        i = pl.program_id(1)
        q = q_ref[...]
        row, col = _iota_tile()
        later_keys = _scan_matrix(lambda j, s_: j > s_)
        last = (i * QQ + QQ - 1) // KT

        def step(tt, carry):
            acc, later = carry
            t = last - tt
            off = pl.multiple_of(t * KT, KT)
            k = k_ref[pl.ds(off, KT), :]
            v = v_ref[pl.ds(off, KT), :]
            z = _dot_nt(q, k) * SCALE
            strict = row + (i * QQ - t * KT) > col
            sp = _softplus(z)
            lnb = jnp.where(strict, -sp, 0.0)
            afters = []
            for xb in reversed(_blocks(lnb)):
                afters.append(later + _scan_dot(xb, later_keys))
                later = later + jnp.sum(xb, axis=1, keepdims=True)
            after = jnp.concatenate(afters[::-1], axis=1)
            w = jnp.where(strict, jnp.exp((z - sp) + after), 0.0)
            return acc + _dot(w.astype(BF), v), later

        acc, total = lax.fori_loop(0, last + 1, step, (jnp.zeros((QQ, HD), F32), jnp.zeros((QQ, 1), F32)))
        o_ref[...] = acc.astype(BF)
        ot_ref[...] = acc.T.astype(BF)
        t_ref[...] = jnp.broadcast_to(total, (QQ, HD))

    blk = pl.BlockSpec((QQ, HD), lambda h, i: (i, h))
    return pl.pallas_call(
        body, name=name, grid=(N_SB, s // QQ),
        in_specs=[blk, pl.BlockSpec((s, HD), lambda h, i: (0, N_HEADS + h)), pl.BlockSpec((s, HD), lambda h, i: (0, 2 * N_HEADS + h))],
        out_specs=[blk, pl.BlockSpec((HD, QQ), lambda h, i: (h, i)), blk],
        out_shape=[_sds((s, N_SB * HD), BF), _sds((N_SB * HD, s), BF), _sds((s, N_SB * HD), F32)],
        compiler_params=_cp("parallel", "parallel"),
    )(qkv, qkv, qkv)


def _sb_bwd(qkv, do, total, name):
    s = qkv.shape[0]
    nb = s // QB

    def body(q_ref, k_ref, v_ref, do_ref, t_ref, dq_ref, dk_ref, dv_ref, dk_acc, dv_acc):
        i = pl.program_id(1)

        @pl.when(i == 0)
        def _():
            dk_acc[...] = jnp.zeros_like(dk_acc)
            dv_acc[...] = jnp.zeros_like(dv_acc)

        q = q_ref[...]
        dov = do_ref[...]
        tot = _col(t_ref[...])
        row, col = _iota_tile()
        keys_upto = _scan_matrix(lambda j, s_: j <= s_)
        keys_before = _scan_matrix(lambda j, s_: j < s_)

        def step(t, carry):
            dq, lnb_before, dl_before = carry
            off = pl.multiple_of(t * KT, KT)
            k = k_ref[pl.ds(off, KT), :]
            v = v_ref[pl.ds(off, KT), :]
            z = _dot_nt(q, k) * SCALE
            strict = row + (i * QQ - t * KT) > col
            sp = _softplus(z)
            lnb = jnp.where(strict, -sp, 0.0)
            afters = []
            for xb in _blocks(lnb):
                afters.append(tot - (lnb_before + _scan_dot(xb, keys_upto)))
                lnb_before = lnb_before + jnp.sum(xb, axis=1, keepdims=True)
            a = jnp.where(strict, jnp.exp((z - sp) + jnp.concatenate(afters, axis=1)), 0.0)
            dl = a * _dot_nt(dov, v)
            befores = []
            for xb in _blocks(dl):
                befores.append(dl_before + _scan_dot(xb, keys_before))
                dl_before = dl_before + jnp.sum(xb, axis=1, keepdims=True)
            sig = jnp.exp(z - sp)
            dz = jnp.where(strict, dl * (1.0 - sig) - sig * jnp.concatenate(befores, axis=1), 0.0) * SCALE
            dq = dq + _dot(dz.astype(BF), k)
            dk_acc[pl.ds(off, KT), :] += _dot(dz.T.astype(BF), q)
            dv_acc[pl.ds(off, KT), :] += _dot(a.T.astype(BF), dov)
            return dq, lnb_before, dl_before

        zero = jnp.zeros((QQ, 1), F32)
        dq, _, _ = lax.fori_loop(0, (i * QQ + QQ - 1) // KT + 1, step, (jnp.zeros((QQ, HD), F32), zero, zero))
        dq_ref[...] = dq.astype(BF)

        @pl.when(i == s // QQ - 1)
        def _():
            dk_ref[...] = dk_acc[...].astype(BF)
            dv_ref[...] = dv_acc[...].astype(BF)

    blk = pl.BlockSpec((QQ, HD), lambda h, i: (i, h))
    full = pl.BlockSpec((s, HD), lambda h, i: (0, h))
    return pl.pallas_call(
        body, name=name, grid=(N_SB, s // QQ),
        in_specs=[blk, pl.BlockSpec((s, HD), lambda h, i: (0, N_HEADS + h)), pl.BlockSpec((s, HD), lambda h, i: (0, 2 * N_HEADS + h)), blk, blk],
        out_specs=[blk, full, full], out_shape=[_sds((s, N_SB * HD), BF)] * 3,
        scratch_shapes=[pltpu.VMEM((s, HD), F32), pltpu.VMEM((s, HD), F32)],
        compiler_params=_cp("parallel", "arbitrary"),
    )(qkv, qkv, qkv, do, total)


def _fgate_fwd(f, b, name):
    s = f.shape[0]
    nb = s // QB
    nfox = N_HEADS - N_SB

    def body(f_ref, b_ref, cb_ref, ct_ref):
        row, col = _iota2()
        upto = (row >= col).astype(BF)
        carry = jnp.zeros((1, HD), F32)
        for blk in range(nb):
            xv = f_ref[blk * QB:(blk + 1) * QB, :] + b_ref[...]
            logf = -_softplus(-xv)
            cum = _tri_dot(logf, upto, left=True) + carry
            carry = cum[QB - 1:QB, :]
            ct_ref[blk] = cum.T
            for h in range(nfox):
                cb_ref[blk * QB:(blk + 1) * QB, h * HD:(h + 1) * HD] = jnp.broadcast_to(cum[:, h:h + 1], (QB, HD))

    return pl.pallas_call(
        body, name=name, out_shape=[_sds((s, nfox * HD), F32), _sds((nb, HD, HD), F32)], compiler_params=_cp(),
    )(f, b)


def _fgate_bwd(dcq, dck, f, b, name):
    s = f.shape[0]
    nb = s // QB
    nfox = N_HEADS - N_SB

    def body(dcq_ref, dck_ref, f_ref, b_ref, df_ref, db_ref):
        row, col = _iota2()
        from_tri = (row <= col).astype(BF)
        lane = col
        carry = jnp.zeros((1, HD), F32)
        db = jnp.zeros((1, HD), F32)
        for blk in reversed(range(nb)):
            dcum = jnp.zeros((QB, HD), F32)
            for h in range(nfox):
                here = (slice(blk * QB, (blk + 1) * QB), slice(h * HD, (h + 1) * HD))
                dcum = jnp.where(lane == h, dcq_ref[here] - dck_ref[here], dcum)
            dlogf = _tri_dot(dcum, from_tri, left=True) + carry
            carry = dlogf[0:1, :]
            xv = f_ref[blk * QB:(blk + 1) * QB, :] + b_ref[...]
            sp = _softplus(xv)
            df = jnp.where(lane < nfox, dlogf * jnp.exp(-sp), 0.0)
            df_ref[blk * QB:(blk + 1) * QB, :] = df.astype(BF)
            db = db + jnp.sum(df, axis=0, keepdims=True)
        db_ref[...] = db

    return pl.pallas_call(
        body, name=name, out_shape=[_sds((s, HD), BF), _sds((1, HD), F32)], compiler_params=_cp(),
    )(dcq, dck, f, b)


def _fox_head_row(ct_ref, j, h):
    tile = ct_ref[j]
    sub = lax.broadcasted_iota(jnp.int32, tile.shape, 0)
    return jnp.sum(jnp.where(sub == h, tile, 0.0), axis=0, keepdims=True)


def _fox_tile_row(ct_ref, t, h):
    nsub = KT // QB
    return jnp.concatenate([_fox_head_row(ct_ref, t * nsub + b, h) for b in range(nsub)], axis=1)


def _fox_fwd(qkv, cum_b, cum_t, name):
    s = qkv.shape[0]
    nb = s // QB
    nfox = N_HEADS - N_SB

    def body(q_ref, k_ref, v_ref, cq_ref, ct_ref, o_ref, ot_ref, lse_ref):
        h, i = pl.program_id(0), pl.program_id(1)
        q = q_ref[...]
        cq = _col(cq_ref[...])
        row, col = _iota_tile()

        def step(t, carry):
            acc, m, l = carry
            off = pl.multiple_of(t * KT, KT)
            k = k_ref[pl.ds(off, KT), :]
            v = v_ref[pl.ds(off, KT), :]
            z = _dot_nt(q, k) * SCALE + cq - _fox_tile_row(ct_ref, t, h)
            z = jnp.where(row + (i * QQ - t * KT) >= col, z, NEG_INF)
            m_new = jnp.maximum(m, jnp.max(z, axis=1, keepdims=True))
            alpha = jnp.exp(m - m_new)
            p = jnp.exp(z - m_new)
            l = alpha * l + jnp.sum(p, axis=1, keepdims=True)
            acc = alpha * acc + _dot(p.astype(BF), v)
            return acc, m_new, l

        acc, m, l = lax.fori_loop(0, (i * QQ + QQ - 1) // KT + 1, step,
                                  (jnp.zeros((QQ, HD), F32), jnp.full((QQ, 1), NEG_INF, F32), jnp.zeros((QQ, 1), F32)))
        o = acc / l
        o_ref[...] = o.astype(BF)
        ot_ref[...] = o.T.astype(BF)
        lse_ref[...] = jnp.broadcast_to(m + jnp.log(l), (QQ, HD))

    blk = pl.BlockSpec((QQ, HD), lambda h, i: (i, h))
    return pl.pallas_call(
        body, name=name, grid=(nfox, s // QQ),
        in_specs=[pl.BlockSpec((QQ, HD), lambda h, i: (i, N_SB + h)),
                  pl.BlockSpec((s, HD), lambda h, i: (0, N_HEADS + N_SB + h)),
                  pl.BlockSpec((s, HD), lambda h, i: (0, 2 * N_HEADS + N_SB + h)),
                  blk, pl.BlockSpec((nb, 8, HD), lambda h, i: (0, 0, 0))],
        out_specs=[blk, pl.BlockSpec((HD, QQ), lambda h, i: (h, i)), blk],
        out_shape=[_sds((s, nfox * HD), BF), _sds((nfox * HD, s), BF), _sds((s, nfox * HD), F32)],
        compiler_params=_cp("parallel", "parallel"),
    )(qkv, qkv, qkv, cum_b, cum_t)


def _fox_bwd(qkv, cum_b, cum_t, o, lse, do, name):
    s = qkv.shape[0]
    nb = s // QB
    nfox = N_HEADS - N_SB

    def body(q_ref, k_ref, v_ref, cq_ref, ct_ref, o_ref, lse_ref, do_ref, dq_ref, dk_ref, dv_ref, dcq_ref, dc_ref, dk_acc, dv_acc, dc_acc):
        h, i = pl.program_id(0), pl.program_id(1)

        @pl.when(i == 0)
        def _():
            dk_acc[...] = jnp.zeros_like(dk_acc)
            dv_acc[...] = jnp.zeros_like(dv_acc)
            dc_acc[...] = jnp.zeros_like(dc_acc)

        q = q_ref[...]
        cq = _col(cq_ref[...])
        dov = do_ref[...]
        lse_c = _col(lse_ref[...])
        delta = jnp.sum(dov.astype(F32) * o_ref[...].astype(F32), axis=1, keepdims=True)
        row, col = _iota_tile()
        ones = jnp.ones((QQ, HD), BF)

        def step(t, carry):
            dq, over_keys = carry
            off = pl.multiple_of(t * KT, KT)
            k = k_ref[pl.ds(off, KT), :]
            v = v_ref[pl.ds(off, KT), :]
            z = _dot_nt(q, k) * SCALE + cq - _fox_tile_row(ct_ref, t, h)
            p = jnp.where(row + (i * QQ - t * KT) >= col, jnp.exp(z - lse_c), 0.0)
            dz = p * (_dot_nt(dov, v) - delta)
            dzt = dz.T
            dq = dq + _dot((dz * SCALE).astype(BF), k)
            dk_acc[pl.ds(off, KT), :] += _dot((dzt * SCALE).astype(BF), q)
            dv_acc[pl.ds(off, KT), :] += _dot(p.T.astype(BF), dov)
            dc_acc[pl.ds(off, KT), :] += _tri_dot(dzt, ones)
            return dq, over_keys + jnp.sum(dz, axis=1, keepdims=True)

        dq, over_keys = lax.fori_loop(0, (i * QQ + QQ - 1) // KT + 1, step, (jnp.zeros((QQ, HD), F32), jnp.zeros((QQ, 1), F32)))
        dq_ref[...] = dq.astype(BF)
        dcq_ref[...] = jnp.broadcast_to(over_keys, (QQ, HD))

        @pl.when(i == s // QQ - 1)
        def _():
            dk_ref[...] = dk_acc[...].astype(BF)
            dv_ref[...] = dv_acc[...].astype(BF)
            dc_ref[...] = dc_acc[...]

    blk = pl.BlockSpec((QQ, HD), lambda h, i: (i, h))
    full = pl.BlockSpec((s, HD), lambda h, i: (0, h))
    return pl.pallas_call(
        body, name=name, grid=(nfox, s // QQ),
        in_specs=[pl.BlockSpec((QQ, HD), lambda h, i: (i, N_SB + h)),
                  pl.BlockSpec((s, HD), lambda h, i: (0, N_HEADS + N_SB + h)),
                  pl.BlockSpec((s, HD), lambda h, i: (0, 2 * N_HEADS + N_SB + h)),
                  blk, pl.BlockSpec((nb, 8, HD), lambda h, i: (0, 0, 0)), blk, blk,
                  pl.BlockSpec((QQ, HD), lambda h, i: (i, N_SB + h))],
        out_specs=[blk, full, full, blk, full],
        out_shape=[_sds((s, nfox * HD), BF)] * 3 + [_sds((s, nfox * HD), F32)] * 2,
        scratch_shapes=[pltpu.VMEM((s, HD), F32)] * 3,
        compiler_params=_cp("parallel", "arbitrary"),
    )(qkv, qkv, qkv, cum_b, cum_t, o, lse, do)


GB = 8
DIL_PAD = QB * 16


def _dil_group(g, d, nb, off=0, shift=0):
    if nb >= GB:
        r, n0 = (g * GB) // nb, (g * GB) % nb
        start = off + r + (n0 + shift) * QB * d
        return [pl.ds(pl.multiple_of(start, QB), GB * QB)] if d == 1 else [pl.ds(start, GB * QB, stride=d)]
    per = GB // nb
    return [pl.ds(off + g * per + e + shift * QB * d, nb * QB, stride=d) for e in range(per)]


def _dil_load(ref, g, d, nb, off=0, shift=0):
    parts = [ref[sl, :] for sl in _dil_group(g, d, nb, off, shift)]
    rows = parts[0] if len(parts) == 1 else jnp.concatenate(parts, axis=0)
    return rows.reshape(GB, QB, HD)


def _dil_store(ref, g, d, nb, val, off=0, shift=0, add=False):
    rows = val.reshape(GB * QB, HD)
    slices = _dil_group(g, d, nb, off, shift)
    size = GB * QB // len(slices)
    for b, sl in enumerate(slices):
        piece = rows if len(slices) == 1 else rows[b * size:(b + 1) * size]
        if add:
            ref[sl, :] += piece
        else:
            ref[sl, :] = piece


def _bdot_nt(a, b):
    return lax.dot_general(a, b, (((2,), (2,)), ((0,), (0,))), preferred_element_type=F32)


def _bdot(a, b):
    return lax.dot_general(a, b, (((2,), (1,)), ((0,), (0,))), preferred_element_type=F32)


def _bdot_tn(a, b):
    return lax.dot_general(jnp.swapaxes(a, 1, 2).astype(BF), b, (((2,), (1,)), ((0,), (0,))), preferred_element_type=F32)


def _dil_masks(g, d, nb):
    row = lax.broadcasted_iota(jnp.int32, (GB, QB, QB), 1)
    col = lax.broadcasted_iota(jnp.int32, (GB, QB, QB), 2)
    blk = lax.broadcasted_iota(jnp.int32, (GB, QB, QB), 0)
    blk = blk + (g * GB) % nb if nb >= GB else blk % nb
    return col <= row, jnp.logical_and(col >= row, blk >= 1) if nb > 1 else None


def _dilated_fwd(qkv, name):
    s = qkv.shape[0]
    npat = len(DILATED_PATTERNS)
    chunk = 256

    def body(q_ref, k_ref, v_ref, out_ref, outt_ref, g_ref, qf, kf, vf, *per_pattern):
        o_s, l_s = per_pattern[:npat], per_pattern[npat:]
        qf[...] = q_ref[...].astype(F32)
        for dst, src in ((kf, k_ref), (vf, v_ref)):
            dst[0:DIL_PAD, :] = jnp.zeros((DIL_PAD, HD), F32)
            dst[DIL_PAD:, :] = src[...].astype(F32)
        for p, (_, d) in enumerate(DILATED_PATTERNS):
            nb = s // d // QB

            def grp(g, carry, p=p, d=d, nb=nb):
                mc, mp = _dil_masks(g, d, nb)
                q = _dil_load(qf, g, d, nb).astype(BF)
                zc = jnp.where(mc, _bdot_nt(q, _dil_load(kf, g, d, nb, DIL_PAD).astype(BF)) * SCALE, NEG_INF)
                m = jnp.max(zc, axis=2, keepdims=True)
                if nb > 1:
                    zp = jnp.where(mp, _bdot_nt(q, _dil_load(kf, g, d, nb, DIL_PAD, -1).astype(BF)) * SCALE, NEG_INF)
                    m = jnp.maximum(m, jnp.max(zp, axis=2, keepdims=True))
                ec = jnp.exp(zc - m)
                l = jnp.sum(ec, axis=2, keepdims=True)
                if nb > 1:
                    ep = jnp.where(mp, jnp.exp(zp - m), 0.0)
                    l = l + jnp.sum(ep, axis=2, keepdims=True)
                o = _bdot((ec / l).astype(BF), _dil_load(vf, g, d, nb, DIL_PAD).astype(BF))
                if nb > 1:
                    o = o + _bdot((ep / l).astype(BF), _dil_load(vf, g, d, nb, DIL_PAD, -1).astype(BF))
                _dil_store(o_s[p], g, d, nb, o)
                _dil_store(l_s[p], g, d, nb, jnp.broadcast_to(m + jnp.log(l), (GB, QB, HD)))
                return carry

            lax.fori_loop(0, s // (QB * GB), grp, 0)
        for c0 in range(0, s, chunk):
            rows = slice(c0, c0 + chunk)
            ls = [l_s[p][rows, :] for p in range(npat)]
            m = functools.reduce(jnp.maximum, ls)
            es = [jnp.exp(l - m) for l in ls]
            tot = functools.reduce(lambda a, b: a + b, es)
            out = functools.reduce(lambda a, b: a + b, [(e / tot) * o_s[p][rows, :] for p, e in enumerate(es)])
            out_ref[rows, :] = out.astype(BF)
            outt_ref[:, rows] = out.T.astype(BF)
            g_ref[rows, :] = m + jnp.log(tot)

    full = pl.BlockSpec((s, HD), lambda h: (0, h))
    return pl.pallas_call(
        body, name=name, grid=(N_HEADS,),
        in_specs=[full, pl.BlockSpec((s, HD), lambda h: (0, N_HEADS + h)), pl.BlockSpec((s, HD), lambda h: (0, 2 * N_HEADS + h))],
        out_specs=[full, pl.BlockSpec((HD, s), lambda h: (h, 0)), full],
        out_shape=[_sds((s, N_HEADS * HD), BF), _sds((N_HEADS * HD, s), BF), _sds((s, N_HEADS * HD), F32)],
        scratch_shapes=[pltpu.VMEM((s, HD), F32)] + [pltpu.VMEM((s + DIL_PAD, HD), F32)] * 2 + [pltpu.VMEM((s, HD), F32)] * (2 * npat),
        compiler_params=_cp("parallel"),
    )(qkv, qkv, qkv)


def _dilated_bwd(qkv, out, glse, do, tables, name):
    s = qkv.shape[0]
    chunk = 256

    def body(q_ref, k_ref, v_ref, out_ref, g_ref, do_ref, c_ref, sa_ref, sb_ref, dq_ref, dk_ref, dv_ref,
             qf, kf, vf, dof, dl_s, dq_a, dk_a, dv_a):
        qf[...] = q_ref[...].astype(F32)
        for dst, src in ((kf, k_ref), (vf, v_ref)):
            dst[0:DIL_PAD, :] = jnp.zeros((DIL_PAD, HD), F32)
            dst[DIL_PAD:, :] = src[...].astype(F32)
        for c0 in range(0, s, chunk):
            rows = slice(c0, c0 + chunk)
            dov = do_ref[rows, :].astype(F32)
            dof[rows, :] = dov
            dl_s[rows, :] = jnp.broadcast_to(jnp.sum(dov * out_ref[rows, :].astype(F32), axis=1, keepdims=True), (chunk, HD))
        dq_a[...] = jnp.zeros_like(dq_a)
        dk_a[...] = jnp.zeros_like(dk_a)
        dv_a[...] = jnp.zeros_like(dv_a)
        for _, d in DILATED_PATTERNS:
            nb = s // d // QB

            def grp(g, carry, d=d, nb=nb):
                mc, mp = _dil_masks(g, d, nb)
                q = _dil_load(qf, g, d, nb).astype(BF)
                kc = _dil_load(kf, g, d, nb, DIL_PAD).astype(BF)
                dov = _dil_load(dof, g, d, nb).astype(BF)
                lse = _dil_load(g_ref, g, d, nb)[:, :, 0:1]
                delta = _dil_load(dl_s, g, d, nb)[:, :, 0:1]
                pc = jnp.where(mc, jnp.exp(_bdot_nt(q, kc) * SCALE - lse), 0.0)
                dzc = pc * (_bdot_nt(dov, _dil_load(vf, g, d, nb, DIL_PAD).astype(BF)) - delta) * SCALE
                dq = _bdot(dzc.astype(BF), kc)
                if nb > 1:
                    kp = _dil_load(kf, g, d, nb, DIL_PAD, -1).astype(BF)
                    pp = jnp.where(mp, jnp.exp(_bdot_nt(q, kp) * SCALE - lse), 0.0)
                    dzp = pp * (_bdot_nt(dov, _dil_load(vf, g, d, nb, DIL_PAD, -1).astype(BF)) - delta) * SCALE
                    dq = dq + _bdot(dzp.astype(BF), kp)
                _dil_store(dq_a, g, d, nb, dq, add=True)
                _dil_store(dk_a, g, d, nb, _bdot_tn(dzc, q), DIL_PAD, add=True)
                _dil_store(dv_a, g, d, nb, _bdot_tn(pc, dov), DIL_PAD, add=True)
                if nb > 1:
                    _dil_store(dk_a, g, d, nb, _bdot_tn(dzp, q), DIL_PAD, -1, add=True)
                    _dil_store(dv_a, g, d, nb, _bdot_tn(pp, dov), DIL_PAD, -1, add=True)
                return carry

            lax.fori_loop(0, s // (QB * GB), grp, 0)
        for c0 in range(0, s, chunk):
            rows = slice(c0, c0 + chunk)
            padded = slice(DIL_PAD + c0, DIL_PAD + c0 + chunk)
            c, sa, sb = c_ref[rows, :], sa_ref[rows, :], sb_ref[rows, :]
            dq_ref[rows, :] = _rope(dq_a[rows, :], c, sa, sb).astype(BF)
            dk_ref[rows, :] = _rope(dk_a[padded, :], c, sa, sb).astype(BF)
            dv_ref[rows, :] = dv_a[padded, :].astype(BF)

    full = pl.BlockSpec((s, HD), lambda h: (0, h))
    tab = pl.BlockSpec((s, HD), lambda h: (0, 0))
    return pl.pallas_call(
        body, name=name, grid=(N_HEADS,),
        in_specs=[full, pl.BlockSpec((s, HD), lambda h: (0, N_HEADS + h)), pl.BlockSpec((s, HD), lambda h: (0, 2 * N_HEADS + h)),
                  full, full, full, tab, tab, tab],
        out_specs=[full, full, full], out_shape=[_sds((s, N_HEADS * HD), BF)] * 3,
        scratch_shapes=[pltpu.VMEM((s, HD), F32)] + [pltpu.VMEM((s + DIL_PAD, HD), F32)] * 2 + [pltpu.VMEM((s, HD), F32)] * 3
        + [pltpu.VMEM((s + DIL_PAD, HD), F32)] * 2,
        compiler_params=_cp("parallel"),
    )(qkv, qkv, qkv, out, glse, do, *tables)


def _swiglu_fwd(x, gnorm, w, tag, deps=()):
    h, ht = _rmsnorm_fwd(x, gnorm, f"norm_{tag}", deps)
    g, u, act, act_t = _ffn_up(h, w["gate"], w["up"], f"ffn_up_{tag}")
    if callable(w["down"]):
        w["down"] = w["down"](g)
    y = _ffn_down(act, w["down"], x, f"ffn_down_{tag}")
    return y, (x, ht, g, u, act_t)


def _swiglu_bwd(saved, gnorm, w, dy, dyb_half, out_scale, tag, deps=(), on_down=None, on_grads=None):
    x, ht, g, u, act_t = saved
    dg, du = _ffn_bwd_act(dyb_half, w["down"], g, u, f"ffn_bwd_act_{tag}", deps)
    if on_down:
        d_gate, d_up = _grad_cols(ht, [dg, du], [False, False], f"ffn_bwd_wgu_{tag}")
        tokens = list(on_grads({"gate": d_gate, "up": d_up}))
        d_down = _grad_rows(act_t, dyb_half, f"ffn_bwd_wd_{tag}", tokens)
        gw = {"gate": d_gate, "up": d_up, "down": d_down}
        tokens = list(on_down(d_down))
    else:
        d_down = _grad_rows(act_t, dyb_half, f"ffn_bwd_wd_{tag}")
        d_gate, d_up = _grad_cols(ht, [dg, du], [False, False], f"ffn_bwd_wgu_{tag}")
        gw = {"gate": d_gate, "up": d_up, "down": d_down}
        tokens = list(on_grads(gw)) if on_grads else []
    dh = _dh_cols([dg, du], [w["gate"], w["up"]], [False, False], f"ffn_bwd_dh_{tag}", tokens)
    dx, dxb, dxbt, dgn = _rmsnorm_bwd(x, gnorm, dh, dy, out_scale, f"norm_bwd_{tag}")
    return (dx, dxb, dxbt), dgn, gw


def kernel(x, norm_g, ffn1_w_gate, ffn1_w_up, ffn1_w_down, ffn2_w_gate, ffn2_w_up, ffn2_w_down, even_w_in, even_b_forget, even_w_out, odd_w_qkv, odd_w_out, final_norm_g, loss_target, m_norm_g, m_ffn1_w_gate, m_ffn1_w_up, m_ffn1_w_down, m_ffn2_w_gate, m_ffn2_w_up, m_ffn2_w_down, m_even_w_in, m_even_b_forget, m_even_w_out, m_odd_w_qkv, m_odd_w_out, m_final_norm_g, v_norm_g, v_ffn1_w_gate, v_ffn1_w_up, v_ffn1_w_down, v_ffn2_w_gate, v_ffn2_w_up, v_ffn2_w_down, v_even_w_in, v_even_b_forget, v_even_w_out, v_odd_w_qkv, v_odd_w_out, v_final_norm_g):
    s, d = x.shape[1], x.shape[2]
    nfox = N_HEADS - N_SB
    ax, ay, ac = lax.axis_index("x"), lax.axis_index("y"), lax.axis_index("c")
    me = 4 * ax + 2 * ay + ac
    slots = jnp.stack([4 * px + 2 * py + ac for px, py in [(ax, ay), (1 - ax, ay), (ax, 1 - ay), (1 - ax, 1 - ay)]]).astype(jnp.int32)
    x0 = x.reshape(s, d)
    target = loss_target.reshape(s, d)

    def bf(w):
        return w.astype(BF)

    groups = [
        [bf(ffn1_w_gate[0]), bf(ffn1_w_up[0]), norm_g.reshape(6, d // NDEV)],
        [bf(even_w_in[0]), bf(even_w_out[0])],
        [bf(ffn2_w_gate[0]), bf(ffn2_w_up[0]), bf(ffn2_w_down[0])],
        [bf(ffn1_w_gate[1]), bf(ffn1_w_up[1]), bf(ffn1_w_down[1])],
        [bf(odd_w_qkv[0]), bf(odd_w_out[0])],
        [bf(ffn2_w_gate[1]), bf(ffn2_w_up[1]), bf(ffn2_w_down[1])],
        [bf(ffn1_w_down[0])],
    ]
    started = [None] * len(groups)
    last_token = []
    for k in (0, 6, 1, 2, 3, 4, 5):
        started[k] = _gather_start(groups[k], me, last_token, f"gather_start_{k}")
        last_token = [started[k]["token"]]
    all_started = last_token

    def forward_early(k, after):
        started[k] = _gather_forward(started[k], after, f"gather_forward_{k}")
        return [started[k]["token"]]

    def gathered(k, after):
        st = started[k] if "send2" in started[k] else _gather_forward(started[k], after, f"gather_forward_{k}")
        return _gather_finish(st, after, f"gather_finish_{k}")

    def ffn_weights(ws_):
        return {"gate": ws_[0], "up": ws_[1], "down": ws_[2]}

    b_pad = jnp.pad(even_b_forget, ((0, 0), (0, HD - nfox)))
    gfin = final_norm_g.reshape(1, d)

    g0 = gathered(0, x0)
    gn = jnp.transpose(g0[2], (1, 0, 2)).reshape(6, 1, d)
    wf = [[{"gate": g0[0], "up": g0[1], "down": lambda after: gathered(6, after)[0]}, None], [None, None]]
    x1, sv_f1_0 = _swiglu_fwd(x0, gn[0], wf[0][0], "l0a", all_started)
    g1 = gathered(1, x1)
    w_in_nat = jnp.transpose(g1[0], (1, 0, 2)).reshape(d, -1)
    w_qkv_e = w_in_nat[:, :3 * d]
    w_f = jnp.pad(w_in_nat[:, 3 * d:], ((0, 0), (0, HD - nfox)))
    w_out_e = g1[1].reshape(d, d)
    h_e, ht_e = _rmsnorm_fwd(x1, gn[1], "norm_l0m")
    qkv_e = _mm_nn(h_e, w_qkv_e, 768, BF, "even_qkv")
    f_e = _mm_nn(h_e, w_f, HD, F32, "even_fgate")
    o_sb, ot_sb, tot_sb = _sb_fwd(qkv_e, "sb_fwd")
    cum_b, cum_t = _fgate_fwd(f_e, b_pad, "fgate_fwd")
    o_fox, ot_fox, lse_fox = _fox_fwd(qkv_e, cum_b, cum_t, "fox_fwd")
    o_e = jnp.concatenate([o_sb, o_fox], axis=1)
    ot_e = jnp.concatenate([ot_sb, ot_fox], axis=0)
    x2 = _mm_nn(o_e, w_out_e, 1024, F32, "even_out", res=x1, deps=forward_early(2, o_e))
    wf[0][1] = ffn_weights(gathered(2, x2))
    x3, sv_f2_0 = _swiglu_fwd(x2, gn[2], wf[0][1], "l0b")

    wf[1][0] = ffn_weights(gathered(3, x3))
    x4, sv_f1_1 = _swiglu_fwd(x3, gn[3], wf[1][0], "l1a")
    g4 = gathered(4, x4)
    w_qkv_o = g4[0]
    w_out_o = g4[1].reshape(d, d)
    h_o, ht_o = _rmsnorm_fwd(x4, gn[4], "norm_l1m")
    qkv_o = _qkv_rope(h_o, w_qkv_o, _rope_tables(s, 1.0), "odd_qkv")
    o_o, ot_o, glse = _dilated_fwd(qkv_o, "dilated_fwd")
    x5 = _mm_nn(o_o, w_out_o, 1024, F32, "odd_out", res=x4)
    wf[1][1] = ffn_weights(gathered(5, x5))
    x6, sv_f2_1 = _swiglu_fwd(x5, gn[5], wf[1][1], "l1b")

    def chip_sums(gs, a_s, tag):
        ps = _pair_sum(gs, a_s, slots, f"pair_sum_{tag}")
        return gs, a_s, _chip_start(ps, f"chip_start_{tag}")

    def as_slices(gs):
        return [g_ if g_.ndim == 3 else g_.reshape(NDEV, g_.shape[0] // NDEV, g_.shape[1]) for g_ in gs]

    def reduce_start(gs, tag):
        gs = as_slices(gs)
        return chip_sums(gs, _pair_exchange(gs, f"pair_exchange_{tag}"), tag)

    red, crossing = {}, {}

    def cross(gs, tag):
        crossing[tag] = _pair_start(as_slices(gs), f"pair_start_{tag}")
        return [crossing[tag]["token"]]

    def reduce_behind_dh(tag):
        return lambda gw: cross([gw["gate"], gw["up"], gw["down"]], tag)

    def reduce_after(tag, after):
        red[tag] = chip_sums(*_pair_finish(crossing[tag], after, f"pair_finish_{tag}"), tag)
        return [red[tag][2]["token"]]

    def reduce_now(tag, names):
        def hook(gw):
            red[tag] = reduce_start([gw[nm] for nm in names] if names else [gw], tag)
            return [red[tag][2]["token"]]
        return hook

    dx6, dx6b, _, d_gfin, loss_part = _loss_head(x6, gfin, target, "loss_head")

    (dx5, dx5b, dx5bt), dgn5, _ = _swiglu_bwd(sv_f2_1, gn[5], wf[1][1], dx6, dx6b, 1.0, "l1b", on_grads=reduce_behind_dh("l1b"))
    d_wout_o = _mm_nn(ot_o, dx5b, 1024, BF, "odd_out_dw")
    do_o = _mm_nt([(dx5b, w_out_o)], "odd_out_do", BF, deps=reduce_after("l1b", dx5))
    dqkv_o = jnp.concatenate(_dilated_bwd(qkv_o, o_o, glse, do_o, _rope_tables(s, -1.0), "dilated_bwd"), axis=1)
    (d_wqkv_o,) = _grad_cols(ht_o, [dqkv_o], [True], "odd_qkv_dw")
    dh_o = _dh_cols([dqkv_o], [w_qkv_o], [True], "odd_qkv_dh", cross([d_wqkv_o, d_wout_o], "l1m"))
    dx4, dx4b, _, dgn4 = _rmsnorm_bwd(x4, gn[4], dh_o, dx5, 0.5, "norm_bwd_l1m")
    (dx3, dx3b, _), dgn3, _ = _swiglu_bwd(sv_f1_1, gn[3], wf[1][0], dx4, dx4b, 0.5, "l1a", reduce_after("l1m", dx4),
                                         on_grads=reduce_behind_dh("l1a"))

    (dx2, dx2b, dx2bt), dgn2, _ = _swiglu_bwd(sv_f2_0, gn[2], wf[0][1], dx3, dx3b, 1.0, "l0b", reduce_after("l1a", dx3),
                                             on_grads=reduce_behind_dh("l0b"))
    d_wout_e = _mm_nn(ot_e, dx2b, 1024, BF, "even_out_dw")
    do_e = _mm_nt([(dx2b, w_out_e)], "even_out_do", BF, deps=reduce_after("l0b", dx2))
    dq_sb, dk_sb, dv_sb = _sb_bwd(qkv_e, do_e, tot_sb, "sb_bwd")
    dq_fx, dk_fx, dv_fx, dcq, dck = _fox_bwd(qkv_e, cum_b, cum_t, o_fox, lse_fox, do_e, "fox_bwd")
    df, db_part = _fgate_bwd(dcq, dck, f_e, b_pad, "fgate_bwd")
    dqkv_e = jnp.concatenate([dq_sb, dq_fx, dk_sb, dk_fx, dv_sb, dv_fx], axis=1)
    d_wqkv_e = _mm_nn(ht_e, dqkv_e, 768, BF, "even_qkv_dw")
    d_wf = _mm_nn(ht_e, df, HD, BF, "even_fgate_dw")
    d_win_nat = jnp.concatenate([d_wqkv_e, d_wf[:, :nfox]], axis=1)
    d_win = jnp.transpose(d_win_nat.reshape(d, NDEV, -1), (1, 0, 2))
    dh_e = _mm_nt([(dqkv_e, w_qkv_e), (df, w_f)], "even_in_dh", deps=cross([d_win, d_wout_e], "l0m"))
    dx1, dx1b, _, dgn1 = _rmsnorm_bwd(x1, gn[1], dh_e, dx2, 0.5, "norm_bwd_l0m")
    (dx0, _, _), dgn0, _ = _swiglu_bwd(sv_f1_0, gn[0], wf[0][0], dx1, dx1b, 1.0, "l0a", reduce_after("l0m", dx1),
                                      on_down=reduce_now("l0a_down", None), on_grads=reduce_now("l0a_gu", ["gate", "up"]))

    def reduce_finish(red, tag, after):
        gs, a_s, st = red
        return list(zip(gs, a_s, _chip_finish(st, after, f"chip_finish_{tag}")))

    f_l1b, f_l1m, f_l1a = (reduce_finish(red[t], t, dx0) for t in ("l1b", "l1m", "l1a"))
    f_l0b, f_l0m = (reduce_finish(red[t], t, dx0) for t in ("l0b", "l0m"))

    def update(w_, m_, v_, parts, nm):
        if w_.shape[2] % 128 == 0:
            return _adamw_sharded(w_, m_, v_, parts, slots, f"adamw_{nm}")
        outs = _adamw_sharded(jnp.swapaxes(w_, 1, 2), jnp.swapaxes(m_, 1, 2), jnp.swapaxes(v_, 1, 2), parts, slots,
                              f"adamw_{nm}", transposed=True)
        return [jnp.swapaxes(o, 1, 2) for o in outs]

    res = {}
    res["even_w_in"] = update(even_w_in, m_even_w_in, v_even_w_in, [f_l0m[0]], "even_w_in")
    res["even_w_out"] = _adamw_sharded(even_w_out, m_even_w_out, v_even_w_out, [f_l0m[1]], slots, "adamw_even_w_out")
    res["odd_w_qkv"] = _adamw_sharded(odd_w_qkv, m_odd_w_qkv, v_odd_w_qkv, [f_l1m[0]], slots, "adamw_odd_w_qkv")
    res["odd_w_out"] = _adamw_sharded(odd_w_out, m_odd_w_out, v_odd_w_out, [f_l1m[1]], slots, "adamw_odd_w_out")
    names = ["ffn2_w_gate", "ffn2_w_up", "ffn2_w_down", "ffn1_w_gate", "ffn1_w_up", "ffn1_w_down"]
    ws = [ffn2_w_gate, ffn2_w_up, ffn2_w_down, ffn1_w_gate, ffn1_w_up, ffn1_w_down]
    ms = [m_ffn2_w_gate, m_ffn2_w_up, m_ffn2_w_down, m_ffn1_w_gate, m_ffn1_w_up, m_ffn1_w_down]
    vs = [v_ffn2_w_gate, v_ffn2_w_up, v_ffn2_w_down, v_ffn1_w_gate, v_ffn1_w_up, v_ffn1_w_down]
    for k in range(3):
        res[names[k]] = update(ws[k], ms[k], vs[k], [f_l0b[k], f_l1b[k]], names[k])
    f_l0a = (reduce_finish(red["l0a_gu"], "l0a_gu", res["ffn2_w_down"][1])
             + reduce_finish(red["l0a_down"], "l0a_down", res["ffn2_w_down"][1]))
    for k in range(3, 6):
        res[names[k]] = update(ws[k], ms[k], vs[k], [f_l0a[k - 3], f_l1a[k - 3]], names[k])

    dnorm = jnp.concatenate([dgn0, dgn1, dgn2, dgn3, dgn4, dgn5], axis=0)
    nsm = d // NDEV
    small_rows = (6 * d + d + 2 * HD) // HD
    pad_rows = -small_rows % 8
    part = jnp.concatenate([dnorm.reshape(-1), d_gfin.reshape(-1), db_part.reshape(-1), loss_part.reshape(-1),
                            jnp.zeros((pad_rows * HD,), F32)]).reshape(small_rows + pad_rows, HD)
    (gathered,) = _all_gather([part], "gather_small")

    def pack(ng, bfg, fg):
        full = lax.dynamic_update_slice(jnp.zeros((6, d), F32), ng.reshape(6, nsm), (0, me * nsm))
        return jnp.concatenate([full.reshape(-1), fg.reshape(-1), jnp.pad(bfg.reshape(-1), (0, HD - nfox)),
                                jnp.zeros((HD + pad_rows * HD,), F32)]).reshape(small_rows + pad_rows, HD)

    sm = _adamw_small(pack(norm_g, even_b_forget, final_norm_g), pack(m_norm_g, m_even_b_forget, m_final_norm_g),
                      pack(v_norm_g, v_even_b_forget, v_final_norm_g), gathered, "adamw_small")

    def unpack(t):
        flat = t.reshape(-1)
        ng = lax.dynamic_slice(flat[:6 * d].reshape(6, d), (0, me * nsm), (6, nsm)).reshape(norm_g.shape)
        fg = flat[6 * d:7 * d].reshape(final_norm_g.shape)
        bfg = flat[7 * d:7 * d + nfox].reshape(even_b_forget.shape)
        return ng, bfg, fg

    sm_g, sm_d, sm_m, sm_v = [unpack(t) for t in sm]
    loss = sm[0].reshape(-1)[7 * d + HD]

    order = ["norm_g", "ffn1_w_gate", "ffn1_w_up", "ffn1_w_down", "ffn2_w_gate", "ffn2_w_up", "ffn2_w_down", "even_w_in",
             "even_b_forget", "even_w_out", "odd_w_qkv", "odd_w_out", "final_norm_g"]
    outs = [loss, dx0.reshape(x.shape)]
    for k in range(4):
        smk = [sm_g, sm_d, sm_m, sm_v][k]
        for nm in order:
            if nm == "norm_g":
                outs.append(smk[0])
            elif nm == "even_b_forget":
                outs.append(smk[1])
            elif nm == "final_norm_g":
                outs.append(smk[2])
            else:
                outs.append(res[nm][k])
    return tuple(outs)
```

```python
import functools

import jax
import jax.numpy as jnp
import numpy as np
from jax import lax
from jax.experimental import pallas as pl
from jax.experimental.pallas import tpu as pltpu

F32 = jnp.float32
BF = jnp.bfloat16
NDEV = 8
HD = 128
QB = 128
N_HEADS = 16
N_SB = 8
SCALE = HD ** -0.5
ROPE_THETA = 500000.0
ROPE_DIMS = HD // 4
DILATED_PATTERNS = ((128, 1), (512, 4), (2048, 16))
RMS_EPS = 1e-6
NEG_INF = -1e30
ADAM_LR = 0.001
ADAM_B1 = 0.9
ADAM_B2 = 0.999
ADAM_EPS = 1e-08
ADAM_WD = 0.01
ADAM_STEP = 10
VMEM_LIMIT_V7X = 56 * 1024 * 1024
MESH = pl.DeviceIdType.MESH
ANY = pl.BlockSpec(memory_space=pl.ANY)

NT_DIMS = (((1,), (1,)), ((), ()))


def _cp(*dims):
    return pltpu.CompilerParams(dimension_semantics=dims if dims else None, vmem_limit_bytes=VMEM_LIMIT_V7X)


def _dot(a, b):
    return jnp.dot(a, b, preferred_element_type=F32)


def _dot_nt(a, b):
    return lax.dot_general(a, b, NT_DIMS, preferred_element_type=F32)


def _sds(shape, dtype):
    return jax.ShapeDtypeStruct(shape, dtype)


def _place():
    x, y, c = lax.axis_index("x"), lax.axis_index("y"), lax.axis_index("c")
    chips = [(x, y), (1 - x, y), (x, 1 - y), (1 - x, 1 - y)]
    return x, y, c, chips


def _all_gather(xs, name):
    n = len(xs)

    def body(*refs):
        x_refs, out_refs = refs[:n], refs[n:2 * n]
        send_sems, recv_sems, local_sems = refs[2 * n:]
        x, y, c, chips = _place()
        me, sibling = (x, y, c), (x, y, 1 - c)
        others = chips[1:]

        def slot(a, px, py, pc):
            return out_refs[a].at[4 * px + 2 * py + pc]

        def copy(a, k, block, to, src=None):
            return pltpu.make_async_remote_copy(
                src_ref=slot(a, *block) if src is None else src, dst_ref=slot(a, *block),
                send_sem=send_sems.at[a, k], recv_sem=recv_sems.at[a, k], device_id=to, device_id_type=MESH)

        started = []
        for a in range(n):
            mine = pltpu.make_async_copy(x_refs[a], slot(a, *me), local_sems.at[a])
            mine.start()
            first = [copy(a, 0, me, sibling, src=x_refs[a])]
            first += [copy(a, 1 + j, me, (*chip, c), src=x_refs[a]) for j, chip in enumerate(others)]
            for cp in first:
                cp.start()
            started += [mine.wait] + [cp.wait_send for cp in first]
        for a in range(n):
            for j, chip in enumerate(others):
                copy(a, 1 + j, (*chip, c), me).wait_recv()
                passed = copy(a, 4 + j, (*chip, c), sibling)
                passed.start()
                started.append(passed.wait_send)
        for a in range(n):
            copy(a, 0, sibling, me).wait_recv()
            for j, chip in enumerate(others):
                copy(a, 4 + j, (*chip, 1 - c), me).wait_recv()
        for w in started:
            w()

    return pl.pallas_call(
        body, name=name,
        out_shape=[_sds((NDEV,) + x.shape, x.dtype) for x in xs],
        in_specs=[ANY] * n, out_specs=[ANY] * n,
        scratch_shapes=[pltpu.SemaphoreType.DMA((n, 7)), pltpu.SemaphoreType.DMA((n, 7)), pltpu.SemaphoreType.DMA((n,))],
    )(*xs)


def _pair_exchange(gs, name):
    n = len(gs)

    def body(*refs):
        g_refs, a_refs = refs[:n], refs[n:2 * n]
        send_sems, recv_sems = refs[2 * n:]
        x, y, c, chips = _place()
        copies = []
        for a in range(n):
            for j, (px, py) in enumerate(chips):
                copies.append(pltpu.make_async_remote_copy(
                    src_ref=g_refs[a].at[4 * px + 2 * py + (1 - c)], dst_ref=a_refs[a].at[j],
                    send_sem=send_sems.at[a, j], recv_sem=recv_sems.at[a, j],
                    device_id=(x, y, 1 - c), device_id_type=MESH))
        for cp in copies:
            cp.start()
        for cp in copies:
            cp.wait()

    return pl.pallas_call(
        body, name=name,
        out_shape=[_sds((4,) + g.shape[1:], g.dtype) for g in gs],
        in_specs=[ANY] * n, out_specs=[ANY] * n,
        scratch_shapes=[pltpu.SemaphoreType.DMA((n, 4)), pltpu.SemaphoreType.DMA((n, 4))],
    )(*gs)


HBM = pl.BlockSpec(memory_space=pltpu.HBM)
SEM = pl.BlockSpec(memory_space=pltpu.SEMAPHORE)
EFFECT = pltpu.SideEffectType.DATAFLOW_SIDE_EFFECTING
TOKEN = _sds((8, 128), F32)
TOKEN_SPEC = pl.BlockSpec((8, 128), lambda *_: (0, 0))


def _in_hbm(x):
    return pltpu.with_memory_space_constraint(x, pltpu.HBM)


def _ignore_deps(body, n_in, n_deps):
    if not n_deps:
        return body
    return lambda *refs: body(*refs[:n_in], *refs[n_in + n_deps:])


def _slot_of(px, py, pc):
    return 4 * px + 2 * py + pc


def _gather_start(xs, me, deps, name):
    n = len(xs)
    lands = [lax.dynamic_update_slice(lax.empty((NDEV,) + x.shape, x.dtype), x[None], (me,) + (0,) * x.ndim) for x in xs]

    def body(*refs):
        x_refs, land_refs = refs[:n], refs[n:2 * n]
        send, recv_ici, recv_sib = refs[2 * n:2 * n + 3]
        token = refs[4 * n + 3]
        x, y, c, chips = _place()
        for a in range(n):
            dst = land_refs[a].at[_slot_of(x, y, c)]
            pltpu.make_async_remote_copy(src_ref=x_refs[a], dst_ref=dst, send_sem=send.at[4 * a], recv_sem=recv_sib.at[a],
                                         device_id=(x, y, 1 - c), device_id_type=MESH).start()
            for j, chip in enumerate(chips[1:]):
                pltpu.make_async_remote_copy(src_ref=x_refs[a], dst_ref=dst, send_sem=send.at[4 * a + 1 + j], recv_sem=recv_ici.at[3 * a + j],
                                             device_id=(*chip, c), device_id_type=MESH).start()
        token[...] = jnp.zeros_like(token)

    outs = pl.pallas_call(
        _ignore_deps(body, 2 * n, len(deps)), name=name,
        out_shape=(pltpu.SemaphoreType.DMA((4 * n,)), pltpu.SemaphoreType.DMA((3 * n,)), pltpu.SemaphoreType.DMA((n,)),
                   *[pltpu.HBM(x.shape, x.dtype) for x in xs], *[pltpu.HBM(l.shape, l.dtype) for l in lands], TOKEN),
        in_specs=[HBM] * (2 * n) + [TOKEN_SPEC] * len(deps),
        out_specs=(SEM, SEM, SEM, *[HBM] * (2 * n), pl.BlockSpec(memory_space=pltpu.VMEM)),
        input_output_aliases={a: 3 + a for a in range(2 * n)},
        compiler_params=pltpu.CompilerParams(has_side_effects=EFFECT),
    )(*[_in_hbm(x) for x in xs], *[_in_hbm(l) for l in lands], *deps)
    send, recv_ici, recv_sib = outs[:3]
    return dict(send=send, recv_ici=recv_ici, recv_sib=recv_sib, xs=list(outs[3:3 + n]), lands=list(outs[3 + n:3 + 2 * n]), token=outs[-1])


def _gather_forward(st, after, name):
    n = len(st["lands"])

    def body(*refs):
        land_refs, recv_ici = refs[:n], refs[n]
        send2, recv2, token = refs[n + 2], refs[n + 3], refs[2 * n + 4]
        x, y, c, chips = _place()
        for a in range(n):
            for j, chip in enumerate(chips[1:]):
                blk = land_refs[a].at[_slot_of(*chip, c)]
                pltpu.make_async_remote_copy(src_ref=blk, dst_ref=blk, send_sem=send2.at[3 * a + j], recv_sem=recv_ici.at[3 * a + j],
                                             device_id=(*chip, c), device_id_type=MESH).wait_recv()
                pltpu.make_async_remote_copy(src_ref=blk, dst_ref=blk, send_sem=send2.at[3 * a + j], recv_sem=recv2.at[3 * a + j],
                                             device_id=(x, y, 1 - c), device_id_type=MESH).start()
        token[...] = jnp.zeros_like(token)

    outs = pl.pallas_call(
        body, name=name,
        out_shape=(pltpu.SemaphoreType.DMA((3 * n,)), pltpu.SemaphoreType.DMA((3 * n,)), *[pltpu.HBM(l.shape, l.dtype) for l in st["lands"]], TOKEN),
        in_specs=[HBM] * n + [SEM, pl.BlockSpec(memory_space=pl.ANY)],
        out_specs=(SEM, SEM, *[HBM] * n, pl.BlockSpec(memory_space=pltpu.VMEM)),
        input_output_aliases={a: 2 + a for a in range(n)},
        compiler_params=pltpu.CompilerParams(has_side_effects=EFFECT),
    )(*st["lands"], st["recv_ici"], after)
    return dict(st, send2=outs[0], recv2=outs[1], lands=list(outs[2:2 + n]), token=outs[-1])


def _gather_finish(st, after, name):
    n = len(st["lands"])

    def body(*refs):
        x_refs, land_refs = refs[:n], refs[n:2 * n]
        send, recv_sib, send2, recv2 = refs[2 * n:2 * n + 4]
        x, y, c, chips = _place()
        for a in range(n):
            mine = land_refs[a].at[_slot_of(x, y, c)]
            theirs = land_refs[a].at[_slot_of(x, y, 1 - c)]
            for k in range(4):
                pltpu.make_async_remote_copy(src_ref=x_refs[a], dst_ref=mine, send_sem=send.at[4 * a + k], recv_sem=recv_sib.at[a],
                                             device_id=(x, y, 1 - c), device_id_type=MESH).wait_send()
            pltpu.make_async_remote_copy(src_ref=x_refs[a], dst_ref=theirs, send_sem=send.at[4 * a], recv_sem=recv_sib.at[a],
                                         device_id=(x, y, 1 - c), device_id_type=MESH).wait_recv()
            for j, chip in enumerate(chips[1:]):
                sent = land_refs[a].at[_slot_of(*chip, c)]
                got = land_refs[a].at[_slot_of(*chip, 1 - c)]
                pltpu.make_async_remote_copy(src_ref=sent, dst_ref=sent, send_sem=send2.at[3 * a + j], recv_sem=recv2.at[3 * a + j],
                                             device_id=(x, y, 1 - c), device_id_type=MESH).wait_send()
                pltpu.make_async_remote_copy(src_ref=got, dst_ref=got, send_sem=send2.at[3 * a + j], recv_sem=recv2.at[3 * a + j],
                                             device_id=(x, y, 1 - c), device_id_type=MESH).wait_recv()

    outs = pl.pallas_call(
        body, name=name,
        out_shape=tuple(pltpu.HBM(v.shape, v.dtype) for v in st["xs"] + st["lands"]),
        in_specs=[HBM] * (2 * n) + [SEM] * 4 + [pl.BlockSpec(memory_space=pl.ANY)], out_specs=tuple([HBM] * (2 * n)),
        input_output_aliases={a: a for a in range(2 * n)},
        compiler_params=pltpu.CompilerParams(has_side_effects=EFFECT),
    )(*st["xs"], *st["lands"], st["send"], st["recv_sib"], st["send2"], st["recv2"], after)
    return list(outs[n:])


def _pair_start(gs, name):
    n = len(gs)
    lands = [lax.empty((4,) + g.shape[1:], g.dtype) for g in gs]

    def body(*refs):
        g_refs, a_refs = refs[:n], refs[n:2 * n]
        send, recv = refs[2 * n], refs[2 * n + 1]
        token = refs[4 * n + 2]
        x, y, c, chips = _place()
        for a in range(n):
            for j, (px, py) in enumerate(chips):
                pltpu.make_async_remote_copy(src_ref=g_refs[a].at[_slot_of(px, py, 1 - c)], dst_ref=a_refs[a].at[j],
                                             send_sem=send.at[4 * a + j], recv_sem=recv.at[4 * a + j],
                                             device_id=(x, y, 1 - c), device_id_type=MESH).start()
        token[...] = jnp.zeros_like(token)

    outs = pl.pallas_call(
        body, name=name,
        out_shape=(pltpu.SemaphoreType.DMA((4 * n,)), pltpu.SemaphoreType.DMA((4 * n,)),
                   *[pltpu.HBM(g.shape, g.dtype) for g in gs], *[pltpu.HBM(l.shape, l.dtype) for l in lands], TOKEN),
        in_specs=[HBM] * (2 * n), out_specs=(SEM, SEM, *[HBM] * (2 * n), pl.BlockSpec(memory_space=pltpu.VMEM)),
        input_output_aliases={a: 2 + a for a in range(2 * n)},
        compiler_params=pltpu.CompilerParams(has_side_effects=EFFECT),
    )(*[_in_hbm(g) for g in gs], *[_in_hbm(l) for l in lands])
    return dict(send=outs[0], recv=outs[1], gs=list(outs[2:2 + n]), lands=list(outs[2 + n:2 + 2 * n]), token=outs[-1])


def _pair_finish(st, after, name):
    n = len(st["gs"])

    def body(*refs):
        g_refs, a_refs = refs[:n], refs[n:2 * n]
        send, recv = refs[2 * n], refs[2 * n + 1]
        x, y, c, chips = _place()
        for a in range(n):
            for j, (px, py) in enumerate(chips):
                cp = pltpu.make_async_remote_copy(src_ref=g_refs[a].at[_slot_of(px, py, 1 - c)], dst_ref=a_refs[a].at[j],
                                                  send_sem=send.at[4 * a + j], recv_sem=recv.at[4 * a + j],
                                                  device_id=(x, y, 1 - c), device_id_type=MESH)
                cp.wait_send()
                cp.wait_recv()

    outs = pl.pallas_call(
        body, name=name,
        out_shape=tuple(pltpu.HBM(v.shape, v.dtype) for v in st["gs"] + st["lands"]),
        in_specs=[HBM] * (2 * n) + [SEM, SEM, pl.BlockSpec(memory_space=pl.ANY)], out_specs=tuple([HBM] * (2 * n)),
        input_output_aliases={a: a for a in range(2 * n)},
        compiler_params=pltpu.CompilerParams(has_side_effects=EFFECT),
    )(*st["gs"], *st["lands"], st["send"], st["recv"], after)
    return list(outs[:n]), list(outs[n:])


def _chip_start(ps, name):
    n = len(ps)
    lands = [lax.empty(p.shape, p.dtype) for p in ps]

    def body(*refs):
        p_refs, b_refs = refs[:n], refs[n:2 * n]
        send, recv = refs[2 * n], refs[2 * n + 1]
        token = refs[4 * n + 2]
        x, y, c, chips = _place()
        for a in range(n):
            for j, chip in enumerate(chips[1:]):
                pltpu.make_async_remote_copy(src_ref=p_refs[a].at[j], dst_ref=b_refs[a].at[j], send_sem=send.at[3 * a + j], recv_sem=recv.at[3 * a + j],
                                             device_id=(*chip, c), device_id_type=MESH).start()
        token[...] = jnp.zeros_like(token)

    outs = pl.pallas_call(
        body, name=name,
        out_shape=(pltpu.SemaphoreType.DMA((3 * n,)), pltpu.SemaphoreType.DMA((3 * n,)),
                   *[pltpu.HBM(p.shape, p.dtype) for p in ps], *[pltpu.HBM(p.shape, p.dtype) for p in ps], TOKEN),
        in_specs=[HBM] * (2 * n), out_specs=(SEM, SEM, *[HBM] * (2 * n), pl.BlockSpec(memory_space=pltpu.VMEM)),
        input_output_aliases={a: 2 + a for a in range(2 * n)},
        compiler_params=pltpu.CompilerParams(has_side_effects=EFFECT),
    )(*[_in_hbm(p) for p in ps], *[_in_hbm(l) for l in lands])
    return dict(send=outs[0], recv=outs[1], ps=list(outs[2:2 + n]), lands=list(outs[2 + n:2 + 2 * n]), token=outs[-1])


def _chip_finish(st, after, name):
    n = len(st["ps"])

    def body(*refs):
        p_refs, b_refs = refs[:n], refs[n:2 * n]
        send, recv = refs[2 * n], refs[2 * n + 1]
        x, y, c, chips = _place()
        for a in range(n):
            for j, chip in enumerate(chips[1:]):
                cp = pltpu.make_async_remote_copy(src_ref=p_refs[a].at[j], dst_ref=b_refs[a].at[j], send_sem=send.at[3 * a + j], recv_sem=recv.at[3 * a + j],
                                                  device_id=(*chip, c), device_id_type=MESH)
                cp.wait_send()
                cp.wait_recv()

    outs = pl.pallas_call(
        body, name=name,
        out_shape=tuple(pltpu.HBM(v.shape, v.dtype) for v in st["ps"] + st["lands"]),
        in_specs=[HBM] * (2 * n) + [SEM, SEM, pl.BlockSpec(memory_space=pl.ANY)], out_specs=tuple([HBM] * (2 * n)),
        input_output_aliases={a: a for a in range(2 * n)},
        compiler_params=pltpu.CompilerParams(has_side_effects=EFFECT),
    )(*st["ps"], *st["lands"], st["send"], st["recv"], after)
    return list(outs[n:])


def _rows_tile(r):
    for t in (512, 256, 128, 64, 32, 16):
        if r % t == 0:
            return t
    return r


PAIR_SUM_STEPS = 4


def _pair_sum(gs, a_s, slots, name):
    n = len(gs)
    trs = [g.shape[1] // PAIR_SUM_STEPS for g in gs]

    def body(slots_ref, *refs):
        for g_ref, a_ref, p_ref in zip(refs[:n], refs[n:2 * n], refs[2 * n:]):
            p_ref[...] = (g_ref[...].astype(F32) + a_ref[...].astype(F32)).astype(BF)

    def spec(g, tr, index):
        return pl.BlockSpec((None, tr, g.shape[2]), index)

    return pl.pallas_call(
        body, name=name,
        grid_spec=pltpu.PrefetchScalarGridSpec(
            num_scalar_prefetch=1, grid=(3, PAIR_SUM_STEPS),
            in_specs=[spec(g, tr, lambda j, i, s: (s[j + 1], i, 0)) for g, tr in zip(gs, trs)]
            + [spec(g, tr, lambda j, i, s: (j + 1, i, 0)) for g, tr in zip(gs, trs)],
            out_specs=[spec(g, tr, lambda j, i, s: (j, i, 0)) for g, tr in zip(gs, trs)]),
        out_shape=[_sds((3,) + g.shape[1:], BF) for g in gs], compiler_params=_cp("parallel", "parallel"),
    )(slots, *gs, *a_s)


def _adamw_math(w, g, m, v):
    m = ADAM_B1 * m + (1.0 - ADAM_B1) * g
    v = ADAM_B2 * v + (1.0 - ADAM_B2) * (g * g)
    m_hat = m / (1.0 - ADAM_B1 ** ADAM_STEP)
    v_hat = v / (1.0 - ADAM_B2 ** ADAM_STEP)
    delta = -ADAM_LR * (m_hat / (jnp.sqrt(v_hat) + ADAM_EPS) + ADAM_WD * w)
    return delta, m, v


def _adamw_sharded(w, m, v, parts, slots, name, transposed=False):
    nl = w.shape[0]
    r, c = parts[0][0].shape[1:]
    tr = _rows_tile(r)
    if c * tr * 4 > (1 << 21) and not transposed:
        tr = max(8, tr // 2)

    def body(slots_ref, w_ref, m_ref, v_ref, *rest):
        part_refs, (g_out, d_out, m_out, v_out) = rest[:5 * nl], rest[5 * nl:]
        layer = pl.program_id(0)
        g = None
        for l in range(nl):
            s = part_refs[5 * l][...].astype(F32)
            for ref in part_refs[5 * l + 1:5 * l + 5]:
                s = s + ref[...].astype(F32)
            g = s if g is None else jnp.where(layer == l, s, g)
        if transposed:
            g = g.T
        delta, mn, vn = _adamw_math(w_ref[...], g, m_ref[...], v_ref[...])
        g_out[...] = g
        d_out[...] = delta
        m_out[...] = mn
        v_out[...] = vn

    def own(l):
        return lambda L, i, s: (s[0], jnp.where(L == l, i, 0), 0)

    def fixed(l, k):
        return lambda L, i, s: (k, jnp.where(L == l, i, 0), 0)

    if transposed:
        wspec = pl.BlockSpec((None, c, tr), lambda L, i, s: (L, 0, i))
    else:
        wspec = pl.BlockSpec((None, tr, c), lambda L, i, s: (L, i, 0))
    in_specs = [wspec, wspec, wspec]
    args = [w, m, v]
    for l, (g, a, b) in enumerate(parts):
        in_specs += [pl.BlockSpec((None, tr, c), own(l)), pl.BlockSpec((None, tr, c), fixed(l, 0)),
                     pl.BlockSpec((None, tr, c), fixed(l, 0)), pl.BlockSpec((None, tr, c), fixed(l, 1)),
                     pl.BlockSpec((None, tr, c), fixed(l, 2))]
        args += [g, a, b, b, b]
    return pl.pallas_call(
        body, name=name,
        grid_spec=pltpu.PrefetchScalarGridSpec(
            num_scalar_prefetch=1, grid=(nl, r // tr), in_specs=in_specs, out_specs=[wspec] * 4),
        out_shape=[_sds(w.shape, F32)] * 4, compiler_params=_cp("arbitrary", "arbitrary"),
    )(slots, *args)


def _adamw_small(w, m, v, gathered, name):
    def body(w_ref, m_ref, v_ref, gg_ref, g_out, d_out, m_out, v_out):
        g = gg_ref[0]
        for k in range(1, NDEV):
            g = g + gg_ref[k]
        delta, mn, vn = _adamw_math(w_ref[...], g, m_ref[...], v_ref[...])
        g_out[...] = g
        d_out[...] = delta
        m_out[...] = mn
        v_out[...] = vn

    return pl.pallas_call(body, name=name, out_shape=[_sds(w.shape, F32)] * 4)(w, m, v, gathered)


def _rmsnorm_fwd(x, g, name, deps=()):
    s, d = x.shape
    tm = 256

    def body(x_ref, g_ref, h_ref, ht_ref):
        xf = x_ref[...]
        y = xf * lax.rsqrt(jnp.mean(xf * xf, axis=-1, keepdims=True) + RMS_EPS)
        h = y * g_ref[...]
        h_ref[...] = h.astype(BF)
        ht_ref[...] = h.T.astype(BF)

    return pl.pallas_call(
        _ignore_deps(body, 2, len(deps)), name=name, grid=(s // tm,),
        in_specs=[pl.BlockSpec((tm, d), lambda i: (i, 0)), pl.BlockSpec((1, d), lambda i: (0, 0))] + [TOKEN_SPEC] * len(deps),
        out_specs=[pl.BlockSpec((tm, d), lambda i: (i, 0)), pl.BlockSpec((d, tm), lambda i: (0, i))],
        out_shape=[_sds((s, d), BF), _sds((d, s), BF)], compiler_params=_cp("parallel"),
    )(x, g, *deps)


def _rmsnorm_bwd(x, g, dh, dres, out_scale, name):
    s, d = x.shape
    tm = 256

    def body(x_ref, g_ref, dh_ref, dres_ref, dx_ref, dxb_ref, dxbt_ref, dg_ref):
        xf = x_ref[...]
        r = lax.rsqrt(jnp.mean(xf * xf, axis=-1, keepdims=True) + RMS_EPS)
        xhat = xf * r
        dhv = dh_ref[...]
        dxhat = dhv * g_ref[...]
        dx = dres_ref[...] + r * (dxhat - xhat * jnp.mean(dxhat * xhat, axis=-1, keepdims=True))
        dx_ref[...] = dx
        scaled = dx * out_scale
        dxb_ref[...] = scaled.astype(BF)
        dxbt_ref[...] = scaled.T.astype(BF)

        @pl.when(pl.program_id(0) == 0)
        def _():
            dg_ref[...] = jnp.zeros_like(dg_ref)

        dg_ref[...] += jnp.sum(dhv * xhat, axis=0, keepdims=True)

    row = pl.BlockSpec((tm, d), lambda i: (i, 0))
    vec = pl.BlockSpec((1, d), lambda i: (0, 0))
    return pl.pallas_call(
        body, name=name, grid=(s // tm,),
        in_specs=[row, vec, row, row],
        out_specs=[row, row, pl.BlockSpec((d, tm), lambda i: (0, i)), vec],
        out_shape=[_sds((s, d), F32), _sds((s, d), BF), _sds((d, s), BF), _sds((1, d), F32)],
        compiler_params=_cp("arbitrary"),
    )(x, g, dh, dres)


def _loss_head(x, g, target, name):
    s, d = x.shape
    tm = 256

    def body(x_ref, g_ref, t_ref, dx_ref, dxb_ref, dxbt_ref, dg_ref, loss_ref):
        xf = x_ref[...]
        r = lax.rsqrt(jnp.mean(xf * xf, axis=-1, keepdims=True) + RMS_EPS)
        xhat = xf * r
        err = xhat * g_ref[...] - t_ref[...]
        dy = err * (1.0 / d)
        dxhat = dy * g_ref[...]
        dx = r * (dxhat - xhat * jnp.mean(dxhat * xhat, axis=-1, keepdims=True))
        dx_ref[...] = dx
        half = dx * 0.5
        dxb_ref[...] = half.astype(BF)
        dxbt_ref[...] = half.T.astype(BF)

        @pl.when(pl.program_id(0) == 0)
        def _():
            dg_ref[...] = jnp.zeros_like(dg_ref)
            loss_ref[...] = jnp.zeros_like(loss_ref)

        dg_ref[...] += jnp.sum(dy * xhat, axis=0, keepdims=True)
        part = 0.5 * jnp.sum(jnp.mean(err * err, axis=-1, keepdims=True), axis=0, keepdims=True)
        lane = lax.broadcasted_iota(jnp.int32, (1, 128), 1)
        loss_ref[...] += jnp.where(lane == 0, part, 0.0)

    row = pl.BlockSpec((tm, d), lambda i: (i, 0))
    vec = pl.BlockSpec((1, d), lambda i: (0, 0))
    return pl.pallas_call(
        body, name=name, grid=(s // tm,),
        in_specs=[row, vec, row],
        out_specs=[row, row, pl.BlockSpec((d, tm), lambda i: (0, i)), vec, pl.BlockSpec((1, 128), lambda i: (0, 0))],
        out_shape=[_sds((s, d), F32), _sds((s, d), BF), _sds((d, s), BF), _sds((1, d), F32), _sds((1, 128), F32)],
        compiler_params=_cp("arbitrary"),
    )(x, g, target)


def _act_spec(tm, n, natural, order):
    if natural:
        return pl.BlockSpec((tm, n), (lambda s, i: (i, s)) if order == "si" else (lambda i, s: (i, s)))
    return pl.BlockSpec((None, tm, n), (lambda s, i: (s, i, 0)) if order == "si" else (lambda i, s: (s, i, 0)))


def _act_shape(s, n, natural, dtype):
    return _sds((s, NDEV * n), dtype) if natural else _sds((NDEV, s, n), dtype)


def _ffn_up(h, wg, wu, name):
    s, d = h.shape
    n = wg.shape[2]
    tm = 1024

    def body(h_ref, wg_ref, wu_ref, g_ref, u_ref, a_ref, at_ref):
        hb = h_ref[...]
        g = _dot(hb, wg_ref[...])
        u = _dot(hb, wu_ref[...])
        sig = jax.nn.sigmoid(g)
        silu = g * sig
        g_ref[...] = (u * (sig * (1.0 + g * (1.0 - sig)))).astype(BF)
        u_ref[...] = silu.astype(BF)
        act = silu * u
        a_ref[...] = act.astype(BF)
        at_ref[...] = act.T.astype(BF)

    wsp = pl.BlockSpec((None, d, n), lambda s_, i: (s_, 0, 0))
    blk = _act_spec(tm, n, False, "si")
    return pl.pallas_call(
        body, name=name, grid=(NDEV, s // tm),
        in_specs=[pl.BlockSpec((tm, d), lambda s_, i: (i, 0)), wsp, wsp],
        out_specs=[blk] * 3 + [pl.BlockSpec((None, n, tm), lambda s_, i: (s_, 0, i))],
        out_shape=[_act_shape(s, n, False, BF)] * 3 + [_sds((NDEV, n, s), BF)],
        compiler_params=_cp("parallel", "parallel"),
    )(h, wg, wu)


def _ffn_down(act, wd, x, name):
    _, s, n = act.shape
    d = wd.shape[2]
    tm = 512

    def body(a_ref, w_ref, x_ref, o_ref, acc):
        k = pl.program_id(1)

        @pl.when(k == 0)
        def _():
            acc[...] = jnp.zeros_like(acc)

        acc[...] += _dot(a_ref[...], w_ref[...])

        @pl.when(k == NDEV - 1)
        def _():
            o_ref[...] = x_ref[...] + 0.5 * acc[...]

    row = pl.BlockSpec((tm, d), lambda i, k: (i, 0))
    return pl.pallas_call(
        body, name=name, grid=(s // tm, NDEV),
        in_specs=[_act_spec(tm, n, False, "is"), pl.BlockSpec((None, n, d), lambda i, k: (k, 0, 0)), row],
        out_specs=row, out_shape=_sds((s, d), F32),
        scratch_shapes=[pltpu.VMEM((tm, d), F32)], compiler_params=_cp("parallel", "arbitrary"),
    )(act, wd, x)


def _ffn_bwd_act(dyb, wd, g, u, name, deps=()):
    s, d = dyb.shape
    n = wd.shape[1]
    tm = 1024

    def body(dy_ref, w_ref, g_ref, u_ref, dg_ref, du_ref):
        dact = _dot_nt(dy_ref[...], w_ref[...])
        dg_ref[...] = (dact * g_ref[...].astype(F32)).astype(BF)
        du_ref[...] = (dact * u_ref[...].astype(F32)).astype(BF)

    blk = _act_spec(tm, n, False, "is")
    return pl.pallas_call(
        _ignore_deps(body, 4, len(deps)), name=name, grid=(s // tm, NDEV),
        in_specs=[pl.BlockSpec((tm, d), lambda i, s_: (i, 0)), pl.BlockSpec((None, n, d), lambda i, s_: (s_, 0, 0)), blk, blk]
        + [TOKEN_SPEC] * len(deps),
        out_specs=[blk, blk], out_shape=[_act_shape(s, n, False, BF)] * 2,
        compiler_params=_cp("parallel", "parallel"),
    )(dyb, wd, g, u, *deps)


def _grad_rows(act_t, dyb, name, deps=()):
    _, n, s = act_t.shape
    d = dyb.shape[1]
    tn = 2048

    def body(a_ref, dy_ref, o_ref):
        o_ref[...] = _dot(a_ref[...], dy_ref[...]).astype(BF)

    return pl.pallas_call(
        _ignore_deps(body, 2, len(deps)), name=name, grid=(NDEV, d // tn),
        in_specs=[pl.BlockSpec((None, n, s), lambda k, j: (k, 0, 0)), pl.BlockSpec((s, tn), lambda k, j: (0, j))]
        + [TOKEN_SPEC] * len(deps),
        out_specs=pl.BlockSpec((None, n, tn), lambda k, j: (k, 0, j)), out_shape=_sds((NDEV, n, d), BF),
        compiler_params=_cp("parallel", "parallel"),
    )(act_t, dyb, *deps)


def _grad_cols(ht, dxs, naturals, name, deps=()):
    d, s = ht.shape
    k = len(dxs)
    ns = [dx.shape[1] // NDEV if nat else dx.shape[2] for dx, nat in zip(dxs, naturals)]
    td = 1024

    def body(*refs):
        ht_ref, dx_refs, o_refs = refs[0], refs[1:1 + k], refs[1 + k:]
        hv = ht_ref[...]
        for dx_ref, o_ref in zip(dx_refs, o_refs):
            o_ref[...] = _dot(hv, dx_ref[...]).astype(BF)

    def dx_spec(n, nat):
        if nat:
            return pl.BlockSpec((s, n), lambda s_, j: (0, s_))
        return pl.BlockSpec((None, s, n), lambda s_, j: (s_, 0, 0))

    return pl.pallas_call(
        _ignore_deps(body, 1 + k, len(deps)), name=name, grid=(NDEV, d // td),
        in_specs=[pl.BlockSpec((td, s), lambda s_, j: (j, 0))] + [dx_spec(n, nat) for n, nat in zip(ns, naturals)]
        + [TOKEN_SPEC] * len(deps),
        out_specs=[pl.BlockSpec((None, td, n), lambda s_, j: (s_, j, 0)) for n in ns],
        out_shape=[_sds((NDEV, d, n), BF) for n in ns], compiler_params=_cp("parallel", "parallel"),
    )(ht, *dxs, *deps)


def _dh_cols(dxs, ws, naturals, name, deps=()):
    k = len(dxs)
    d = ws[0].shape[1]
    ns = [w.shape[2] for w in ws]
    s = dxs[0].shape[0] if naturals[0] else dxs[0].shape[1]
    tm = 1024

    def body(*refs):
        dx_refs, w_refs, o_ref, acc = refs[:k], refs[k:2 * k], refs[2 * k], refs[2 * k + 1]
        j = pl.program_id(1)

        @pl.when(j == 0)
        def _():
            acc[...] = jnp.zeros_like(acc)

        for dx_ref, w_ref in zip(dx_refs, w_refs):
            acc[...] += _dot_nt(dx_ref[...], w_ref[...])

        @pl.when(j == NDEV - 1)
        def _():
            o_ref[...] = acc[...]

    return pl.pallas_call(
        _ignore_deps(body, 2 * k, len(deps)), name=name, grid=(s // tm, NDEV),
        in_specs=[_act_spec(tm, n, nat, "is") for n, nat in zip(ns, naturals)]
        + [pl.BlockSpec((None, d, n), lambda i, j: (j, 0, 0)) for n in ns] + [TOKEN_SPEC] * len(deps),
        out_specs=pl.BlockSpec((tm, d), lambda i, j: (i, 0)), out_shape=_sds((s, d), F32),
        scratch_shapes=[pltpu.VMEM((tm, d), F32)], compiler_params=_cp("parallel", "arbitrary"),
    )(*dxs, *ws, *deps)


def _mm_nn(a, b, tn, out_dtype, name, res=None, tm=1024, deps=()):
    m, k = a.shape
    nn = b.shape[1]

    def body(*refs):
        if res is None:
            a_ref, b_ref, o_ref = refs
            o_ref[...] = _dot(a_ref[...], b_ref[...]).astype(out_dtype)
        else:
            a_ref, b_ref, r_ref, o_ref = refs
            o_ref[...] = (r_ref[...] + _dot(a_ref[...], b_ref[...])).astype(out_dtype)

    osp = pl.BlockSpec((tm, tn), lambda j, i: (i, j))
    in_specs = [pl.BlockSpec((tm, k), lambda j, i: (i, 0)), pl.BlockSpec((k, tn), lambda j, i: (0, j))]
    args = [a, b]
    if res is not None:
        in_specs.append(osp)
        args.append(res)
    return pl.pallas_call(
        _ignore_deps(body, len(args), len(deps)), name=name, grid=(nn // tn, m // tm),
        in_specs=in_specs + [TOKEN_SPEC] * len(deps), out_specs=osp,
        out_shape=_sds((m, nn), out_dtype), compiler_params=_cp("parallel", "parallel"),
    )(*args, *deps)


def _mm_nt(pairs, name, out_dtype=F32, tm=512, tk=512, deps=()):
    m = pairs[0][0].shape[0]
    kk = pairs[0][1].shape[0]
    p = len(pairs)

    def body(*refs):
        o_ref = refs[2 * p]
        t = _dot_nt(refs[0][...], refs[1][...])
        for q in range(1, p):
            t = t + _dot_nt(refs[2 * q][...], refs[2 * q + 1][...])
        o_ref[...] = t.astype(out_dtype)

    in_specs, args = [], []
    for a, b in pairs:
        in_specs += [pl.BlockSpec((tm, a.shape[1]), lambda j, i: (i, 0)), pl.BlockSpec((tk, b.shape[1]), lambda j, i: (j, 0))]
        args += [a, b]
    return pl.pallas_call(
        _ignore_deps(body, 2 * p, len(deps)), name=name, grid=(kk // tk, m // tm), in_specs=in_specs + [TOKEN_SPEC] * len(deps),
        out_specs=pl.BlockSpec((tm, tk), lambda j, i: (i, j)), out_shape=_sds((m, kk), out_dtype),
        compiler_params=_cp("parallel", "parallel"),
    )(*args, *deps)


def _rope_tables(s, sign):
    half = ROPE_DIMS // 2
    f32 = np.float32
    freqs = f32(ROPE_THETA) ** (-np.arange(half, dtype=f32) / f32(half))
    ang = np.arange(s, dtype=f32)[:, None] * freqs[None, :]
    cos, sin = np.cos(ang).astype(f32), (sign * np.sin(ang)).astype(f32)
    one = np.ones((s, HD - ROPE_DIMS), f32)
    zero = np.zeros((s, HD - ROPE_DIMS), f32)
    zh = np.zeros((s, half), f32)
    c = np.concatenate([cos, cos, one], axis=1)
    sa = np.concatenate([-sin, zh, zero], axis=1)
    sb = np.concatenate([zh, sin, zero], axis=1)
    return jnp.asarray(c), jnp.asarray(sa), jnp.asarray(sb)


def _rope(xv, c, sa, sb):
    return xv * c + pltpu.roll(xv, HD - ROPE_DIMS // 2, 1) * sa + pltpu.roll(xv, ROPE_DIMS // 2, 1) * sb


def _qkv_rope(h, w, tables, name):
    s, d = h.shape
    n = w.shape[2]
    per = n // HD
    tm = 1024

    def body(h_ref, w_ref, c_ref, sa_ref, sb_ref, o_ref):
        shard = pl.program_id(0)
        y = _dot(h_ref[...], w_ref[...])
        c, sa, sb = c_ref[...], sa_ref[...], sb_ref[...]
        for j in range(per):
            blk = y[:, j * HD:(j + 1) * HD]
            rot = _rope(blk, c, sa, sb)
            is_qk = shard * per + j < 2 * N_HEADS
            o_ref[:, j * HD:(j + 1) * HD] = jnp.where(is_qk, rot, blk).astype(BF)

    tab = pl.BlockSpec((tm, HD), lambda s_, i: (i, 0))
    return pl.pallas_call(
        body, name=name, grid=(NDEV, s // tm),
        in_specs=[pl.BlockSpec((tm, d), lambda s_, i: (i, 0)), pl.BlockSpec((None, d, n), lambda s_, i: (s_, 0, 0)), tab, tab, tab],
        out_specs=pl.BlockSpec((tm, n), lambda s_, i: (i, s_)), out_shape=_sds((s, NDEV * n), BF),
        compiler_params=_cp("parallel", "parallel"),
    )(h, w, *tables)


def _iota2():
    return (lax.broadcasted_iota(jnp.int32, (QB, QB), 0), lax.broadcasted_iota(jnp.int32, (QB, QB), 1))


def _softplus(z):
    return jnp.maximum(z, 0.0) + jnp.log(1.0 + jnp.exp(-jnp.abs(z)))


def _tri_dot(xv, tri, left=False):
    hi = xv.astype(BF)
    r1 = xv - hi.astype(F32)
    mid = r1.astype(BF)
    lo = (r1 - mid.astype(F32)).astype(BF)
    if left:
        return _dot(tri, hi) + _dot(tri, mid) + _dot(tri, lo)
    return _dot(hi, tri) + _dot(mid, tri) + _dot(lo, tri)


def _col(ref_or_val):
    return ref_or_val[:, 0:1]


KT = 4 * QB
QQ = 4 * QB


def _iota_tile():
    return (lax.broadcasted_iota(jnp.int32, (QQ, KT), 0), lax.broadcasted_iota(jnp.int32, (QQ, KT), 1))


def _scan_matrix(keep):
    tri = keep(*_iota2()).astype(BF)
    return jnp.concatenate([tri, tri], axis=0)


def _scan_dot(xv, tri2):
    hi = xv.astype(BF)
    lo = (xv - hi.astype(F32)).astype(BF)
    return _dot(jnp.concatenate([hi, lo], axis=1), tri2)


def _blocks(xv):
    return [xv[:, b * QB:(b + 1) * QB] for b in range(KT // QB)]


def _sb_fwd(qkv, name):
    s = qkv.shape[0]
    nb = s // QB

    def body(q_ref, k_ref, v_ref, o_ref, ot_ref, t_ref):
        i = pl.program_id(1)
        q = q_ref[...]
        row, col = _iota_tile()
        later_keys = _scan_matrix(lambda j, s_: j > s_)
        last = (i * QQ + QQ - 1) // KT

        def step(tt, carry):
            acc, later = carry
            t = last - tt
            off = pl.multiple_of(t * KT, KT)
            k = k_ref[pl.ds(off, KT), :]
            v = v_ref[pl.ds(off, KT), :]
            z = _dot_nt(q, k) * SCALE
            strict = row + (i * QQ - t * KT) > col
            sp = _softplus(z)
            lnb = jnp.where(strict, -sp, 0.0)
            afters = []
            for xb in reversed(_blocks(lnb)):
                afters.append(later + _scan_dot(xb, later_keys))
                later = later + jnp.sum(xb, axis=1, keepdims=True)
            after = jnp.concatenate(afters[::-1], axis=1)
            w = jnp.where(strict, jnp.exp((z - sp) + after), 0.0)
            return acc + _dot(w.astype(BF), v), later

        acc, total = lax.fori_loop(0, last + 1, step, (jnp.zeros((QQ, HD), F32), jnp.zeros((QQ, 1), F32)))
        o_ref[...] = acc.astype(BF)
        ot_ref[...] = acc.T.astype(BF)
        t_ref[...] = jnp.broadcast_to(total, (QQ, HD))

    blk = pl.BlockSpec((QQ, HD), lambda h, i: (i, h))
    return pl.pallas_call(
        body, name=name, grid=(N_SB, s // QQ),
        in_specs=[blk, pl.BlockSpec((s, HD), lambda h, i: (0, N_HEADS + h)), pl.BlockSpec((s, HD), lambda h, i: (0, 2 * N_HEADS + h))],
        out_specs=[blk, pl.BlockSpec((HD, QQ), lambda h, i: (h, i)), blk],
        out_shape=[_sds((s, N_SB * HD), BF), _sds((N_SB * HD, s), BF), _sds((s, N_SB * HD), F32)],
        compiler_params=_cp("parallel", "parallel"),
    )(qkv, qkv, qkv)


def _sb_bwd(qkv, do, total, name):
    s = qkv.shape[0]
    nb = s // QB

    def body(q_ref, k_ref, v_ref, do_ref, t_ref, dq_ref, dk_ref, dv_ref, dk_acc, dv_acc):
        i = pl.program_id(1)

        @pl.when(i == 0)
        def _():
            dk_acc[...] = jnp.zeros_like(dk_acc)
            dv_acc[...] = jnp.zeros_like(dv_acc)

        q = q_ref[...]
        dov = do_ref[...]
        tot = _col(t_ref[...])
        row, col = _iota_tile()
        keys_upto = _scan_matrix(lambda j, s_: j <= s_)
        keys_before = _scan_matrix(lambda j, s_: j < s_)

        def step(t, carry):
            dq, lnb_before, dl_before = carry
            off = pl.multiple_of(t * KT, KT)
            k = k_ref[pl.ds(off, KT), :]
            v = v_ref[pl.ds(off, KT), :]
            z = _dot_nt(q, k) * SCALE
            strict = row + (i * QQ - t * KT) > col
            sp = _softplus(z)
            lnb = jnp.where(strict, -sp, 0.0)
            afters = []
            for xb in _blocks(lnb):
                afters.append(tot - (lnb_before + _scan_dot(xb, keys_upto)))
                lnb_before = lnb_before + jnp.sum(xb, axis=1, keepdims=True)
            a = jnp.where(strict, jnp.exp((z - sp) + jnp.concatenate(afters, axis=1)), 0.0)
            dl = a * _dot_nt(dov, v)
            befores = []
            for xb in _blocks(dl):
                befores.append(dl_before + _scan_dot(xb, keys_before))
                dl_before = dl_before + jnp.sum(xb, axis=1, keepdims=True)
            sig = jnp.exp(z - sp)
            dz = jnp.where(strict, dl * (1.0 - sig) - sig * jnp.concatenate(befores, axis=1), 0.0) * SCALE
            dq = dq + _dot(dz.astype(BF), k)
            dk_acc[pl.ds(off, KT), :] += _dot(dz.T.astype(BF), q)
            dv_acc[pl.ds(off, KT), :] += _dot(a.T.astype(BF), dov)
            return dq, lnb_before, dl_before

        zero = jnp.zeros((QQ, 1), F32)
        dq, _, _ = lax.fori_loop(0, (i * QQ + QQ - 1) // KT + 1, step, (jnp.zeros((QQ, HD), F32), zero, zero))
        dq_ref[...] = dq.astype(BF)

        @pl.when(i == s // QQ - 1)
        def _():
            dk_ref[...] = dk_acc[...].astype(BF)
            dv_ref[...] = dv_acc[...].astype(BF)

    blk = pl.BlockSpec((QQ, HD), lambda h, i: (i, h))
    full = pl.BlockSpec((s, HD), lambda h, i: (0, h))
    return pl.pallas_call(
        body, name=name, grid=(N_SB, s // QQ),
        in_specs=[blk, pl.BlockSpec((s, HD), lambda h, i: (0, N_HEADS + h)), pl.BlockSpec((s, HD), lambda h, i: (0, 2 * N_HEADS + h)), blk, blk],
        out_specs=[blk, full, full], out_shape=[_sds((s, N_SB * HD), BF)] * 3,
        scratch_shapes=[pltpu.VMEM((s, HD), F32), pltpu.VMEM((s, HD), F32)],
        compiler_params=_cp("parallel", "arbitrary"),
    )(qkv, qkv, qkv, do, total)


def _fgate_fwd(f, b, name):
    s = f.shape[0]
    nb = s // QB
    nfox = N_HEADS - N_SB

    def body(f_ref, b_ref, cb_ref, ct_ref):
        row, col = _iota2()
        upto = (row >= col).astype(BF)
        carry = jnp.zeros((1, HD), F32)
        for blk in range(nb):
            xv = f_ref[blk * QB:(blk + 1) * QB, :] + b_ref[...]
            logf = -_softplus(-xv)
            cum = _tri_dot(logf, upto, left=True) + carry
            carry = cum[QB - 1:QB, :]
            ct_ref[blk] = cum.T
            for h in range(nfox):
                cb_ref[blk * QB:(blk + 1) * QB, h * HD:(h + 1) * HD] = jnp.broadcast_to(cum[:, h:h + 1], (QB, HD))

    return pl.pallas_call(
        body, name=name, out_shape=[_sds((s, nfox * HD), F32), _sds((nb, HD, HD), F32)], compiler_params=_cp(),
    )(f, b)


def _fgate_bwd(dcq, dck, f, b, name):
    s = f.shape[0]
    nb = s // QB
    nfox = N_HEADS - N_SB

    def body(dcq_ref, dck_ref, f_ref, b_ref, df_ref, db_ref):
        row, col = _iota2()
        from_tri = (row <= col).astype(BF)
        lane = col
        carry = jnp.zeros((1, HD), F32)
        db = jnp.zeros((1, HD), F32)
        for blk in reversed(range(nb)):
            dcum = jnp.zeros((QB, HD), F32)
            for h in range(nfox):
                here = (slice(blk * QB, (blk + 1) * QB), slice(h * HD, (h + 1) * HD))
                dcum = jnp.where(lane == h, dcq_ref[here] - dck_ref[here], dcum)
            dlogf = _tri_dot(dcum, from_tri, left=True) + carry
            carry = dlogf[0:1, :]
            xv = f_ref[blk * QB:(blk + 1) * QB, :] + b_ref[...]
            sp = _softplus(xv)
            df = jnp.where(lane < nfox, dlogf * jnp.exp(-sp), 0.0)
            df_ref[blk * QB:(blk + 1) * QB, :] = df.astype(BF)
            db = db + jnp.sum(df, axis=0, keepdims=True)
        db_ref[...] = db

    return pl.pallas_call(
        body, name=name, out_shape=[_sds((s, HD), BF), _sds((1, HD), F32)], compiler_params=_cp(),
    )(dcq, dck, f, b)


def _fox_head_row(ct_ref, j, h):
    tile = ct_ref[j]
    sub = lax.broadcasted_iota(jnp.int32, tile.shape, 0)
    return jnp.sum(jnp.where(sub == h, tile, 0.0), axis=0, keepdims=True)


def _fox_tile_row(ct_ref, t, h):
    nsub = KT // QB
    return jnp.concatenate([_fox_head_row(ct_ref, t * nsub + b, h) for b in range(nsub)], axis=1)


def _fox_fwd(qkv, cum_b, cum_t, name):
    s = qkv.shape[0]
    nb = s // QB
    nfox = N_HEADS - N_SB

    def body(q_ref, k_ref, v_ref, cq_ref, ct_ref, o_ref, ot_ref, lse_ref):
        h, i = pl.program_id(0), pl.program_id(1)
        q = q_ref[...]
        cq = _col(cq_ref[...])
        row, col = _iota_tile()

        def step(t, carry):
            acc, m, l = carry
            off = pl.multiple_of(t * KT, KT)
            k = k_ref[pl.ds(off, KT), :]
            v = v_ref[pl.ds(off, KT), :]
            z = _dot_nt(q, k) * SCALE + cq - _fox_tile_row(ct_ref, t, h)
            z = jnp.where(row + (i * QQ - t * KT) >= col, z, NEG_INF)
            m_new = jnp.maximum(m, jnp.max(z, axis=1, keepdims=True))
            alpha = jnp.exp(m - m_new)
            p = jnp.exp(z - m_new)
            l = alpha * l + jnp.sum(p, axis=1, keepdims=True)
            acc = alpha * acc + _dot(p.astype(BF), v)
            return acc, m_new, l

        acc, m, l = lax.fori_loop(0, (i * QQ + QQ - 1) // KT + 1, step,
                                  (jnp.zeros((QQ, HD), F32), jnp.full((QQ, 1), NEG_INF, F32), jnp.zeros((QQ, 1), F32)))
        o = acc / l
        o_ref[...] = o.astype(BF)
        ot_ref[...] = o.T.astype(BF)
        lse_ref[...] = jnp.broadcast_to(m + jnp.log(l), (QQ, HD))

    blk = pl.BlockSpec((QQ, HD), lambda h, i: (i, h))
    return pl.pallas_call(
        body, name=name, grid=(nfox, s // QQ),
        in_specs=[pl.BlockSpec((QQ, HD), lambda h, i: (i, N_SB + h)),
                  pl.BlockSpec((s, HD), lambda h, i: (0, N_HEADS + N_SB + h)),
                  pl.BlockSpec((s, HD), lambda h, i: (0, 2 * N_HEADS + N_SB + h)),
                  blk, pl.BlockSpec((nb, 8, HD), lambda h, i: (0, 0, 0))],
        out_specs=[blk, pl.BlockSpec((HD, QQ), lambda h, i: (h, i)), blk],
        out_shape=[_sds((s, nfox * HD), BF), _sds((nfox * HD, s), BF), _sds((s, nfox * HD), F32)],
        compiler_params=_cp("parallel", "parallel"),
    )(qkv, qkv, qkv, cum_b, cum_t)


def _fox_bwd(qkv, cum_b, cum_t, o, lse, do, name):
    s = qkv.shape[0]
    nb = s // QB
    nfox = N_HEADS - N_SB

    def body(q_ref, k_ref, v_ref, cq_ref, ct_ref, o_ref, lse_ref, do_ref, dq_ref, dk_ref, dv_ref, dcq_ref, dc_ref, dk_acc, dv_acc, dc_acc):
        h, i = pl.program_id(0), pl.program_id(1)

        @pl.when(i == 0)
        def _():
            dk_acc[...] = jnp.zeros_like(dk_acc)
            dv_acc[...] = jnp.zeros_like(dv_acc)
            dc_acc[...] = jnp.zeros_like(dc_acc)

        q = q_ref[...]
        cq = _col(cq_ref[...])
        dov = do_ref[...]
        lse_c = _col(lse_ref[...])
        delta = jnp.sum(dov.astype(F32) * o_ref[...].astype(F32), axis=1, keepdims=True)
        row, col = _iota_tile()
        ones = jnp.ones((QQ, HD), BF)

        def step(t, carry):
            dq, over_keys = carry
            off = pl.multiple_of(t * KT, KT)
            k = k_ref[pl.ds(off, KT), :]
            v = v_ref[pl.ds(off, KT), :]
            z = _dot_nt(q, k) * SCALE + cq - _fox_tile_row(ct_ref, t, h)
            p = jnp.where(row + (i * QQ - t * KT) >= col, jnp.exp(z - lse_c), 0.0)
            dz = p * (_dot_nt(dov, v) - delta)
            dzt = dz.T
            dq = dq + _dot((dz * SCALE).astype(BF), k)
            dk_acc[pl.ds(off, KT), :] += _dot((dzt * SCALE).astype(BF), q)
            dv_acc[pl.ds(off, KT), :] += _dot(p.T.astype(BF), dov)
            dc_acc[pl.ds(off, KT), :] += _tri_dot(dzt, ones)
            return dq, over_keys + jnp.sum(dz, axis=1, keepdims=True)

        dq, over_keys = lax.fori_loop(0, (i * QQ + QQ - 1) // KT + 1, step, (jnp.zeros((QQ, HD), F32), jnp.zeros((QQ, 1), F32)))
        dq_ref[...] = dq.astype(BF)
        dcq_ref[...] = jnp.broadcast_to(over_keys, (QQ, HD))

        @pl.when(i == s // QQ - 1)
        def _():
            dk_ref[...] = dk_acc[...].astype(BF)
            dv_ref[...] = dv_acc[...].astype(BF)
            dc_ref[...] = dc_acc[...]

    blk = pl.BlockSpec((QQ, HD), lambda h, i: (i, h))
    full = pl.BlockSpec((s, HD), lambda h, i: (0, h))
    return pl.pallas_call(
        body, name=name, grid=(nfox, s // QQ),
        in_specs=[pl.BlockSpec((QQ, HD), lambda h, i: (i, N_SB + h)),
                  pl.BlockSpec((s, HD), lambda h, i: (0, N_HEADS + N_SB + h)),
                  pl.BlockSpec((s, HD), lambda h, i: (0, 2 * N_HEADS + N_SB + h)),
                  blk, pl.BlockSpec((nb, 8, HD), lambda h, i: (0, 0, 0)), blk, blk,
                  pl.BlockSpec((QQ, HD), lambda h, i: (i, N_SB + h))],
        out_specs=[blk, full, full, blk, full],
        out_shape=[_sds((s, nfox * HD), BF)] * 3 + [_sds((s, nfox * HD), F32)] * 2,
        scratch_shapes=[pltpu.VMEM((s, HD), F32)] * 3,
        compiler_params=_cp("parallel", "arbitrary"),
    )(qkv, qkv, qkv, cum_b, cum_t, o, lse, do)


GB = 8
DIL_PAD = QB * 16


def _dil_group(g, d, nb, off=0, shift=0):
    if nb >= GB:
        r, n0 = (g * GB) // nb, (g * GB) % nb
        start = off + r + (n0 + shift) * QB * d
        return [pl.ds(pl.multiple_of(start, QB), GB * QB)] if d == 1 else [pl.ds(start, GB * QB, stride=d)]
    per = GB // nb
    return [pl.ds(off + g * per + e + shift * QB * d, nb * QB, stride=d) for e in range(per)]


def _dil_load(ref, g, d, nb, off=0, shift=0):
    parts = [ref[sl, :] for sl in _dil_group(g, d, nb, off, shift)]
    rows = parts[0] if len(parts) == 1 else jnp.concatenate(parts, axis=0)
    return rows.reshape(GB, QB, HD)


def _dil_store(ref, g, d, nb, val, off=0, shift=0, add=False):
    rows = val.reshape(GB * QB, HD)
    slices = _dil_group(g, d, nb, off, shift)
    size = GB * QB // len(slices)
    for b, sl in enumerate(slices):
        piece = rows if len(slices) == 1 else rows[b * size:(b + 1) * size]
        if add:
            ref[sl, :] += piece
        else:
            ref[sl, :] = piece


def _bdot_nt(a, b):
    return lax.dot_general(a, b, (((2,), (2,)), ((0,), (0,))), preferred_element_type=F32)


def _bdot(a, b):
    return lax.dot_general(a, b, (((2,), (1,)), ((0,), (0,))), preferred_element_type=F32)


def _bdot_tn(a, b):
    return lax.dot_general(jnp.swapaxes(a, 1, 2).astype(BF), b, (((2,), (1,)), ((0,), (0,))), preferred_element_type=F32)


def _dil_masks(g, d, nb):
    row = lax.broadcasted_iota(jnp.int32, (GB, QB, QB), 1)
    col = lax.broadcasted_iota(jnp.int32, (GB, QB, QB), 2)
    blk = lax.broadcasted_iota(jnp.int32, (GB, QB, QB), 0)
    blk = blk + (g * GB) % nb if nb >= GB else blk % nb
    return col <= row, jnp.logical_and(col >= row, blk >= 1) if nb > 1 else None


def _dilated_fwd(qkv, name):
    s = qkv.shape[0]
    npat = len(DILATED_PATTERNS)
    chunk = 256

    def body(q_ref, k_ref, v_ref, out_ref, outt_ref, g_ref, qf, kf, vf, *per_pattern):
        o_s, l_s = per_pattern[:npat], per_pattern[npat:]
        qf[...] = q_ref[...].astype(F32)
        for dst, src in ((kf, k_ref), (vf, v_ref)):
            dst[0:DIL_PAD, :] = jnp.zeros((DIL_PAD, HD), F32)
            dst[DIL_PAD:, :] = src[...].astype(F32)
        for p, (_, d) in enumerate(DILATED_PATTERNS):
            nb = s // d // QB

            def grp(g, carry, p=p, d=d, nb=nb):
                mc, mp = _dil_masks(g, d, nb)
                q = _dil_load(qf, g, d, nb).astype(BF)
                zc = jnp.where(mc, _bdot_nt(q, _dil_load(kf, g, d, nb, DIL_PAD).astype(BF)) * SCALE, NEG_INF)
                m = jnp.max(zc, axis=2, keepdims=True)
                if nb > 1:
                    zp = jnp.where(mp, _bdot_nt(q, _dil_load(kf, g, d, nb, DIL_PAD, -1).astype(BF)) * SCALE, NEG_INF)
                    m = jnp.maximum(m, jnp.max(zp, axis=2, keepdims=True))
                ec = jnp.exp(zc - m)
                l = jnp.sum(ec, axis=2, keepdims=True)
                if nb > 1:
                    ep = jnp.where(mp, jnp.exp(zp - m), 0.0)
                    l = l + jnp.sum(ep, axis=2, keepdims=True)
                o = _bdot((ec / l).astype(BF), _dil_load(vf, g, d, nb, DIL_PAD).astype(BF))
                if nb > 1:
                    o = o + _bdot((ep / l).astype(BF), _dil_load(vf, g, d, nb, DIL_PAD, -1).astype(BF))
                _dil_store(o_s[p], g, d, nb, o)
                _dil_store(l_s[p], g, d, nb, jnp.broadcast_to(m + jnp.log(l), (GB, QB, HD)))
                return carry

            lax.fori_loop(0, s // (QB * GB), grp, 0)
        for c0 in range(0, s, chunk):
            rows = slice(c0, c0 + chunk)
            ls = [l_s[p][rows, :] for p in range(npat)]
            m = functools.reduce(jnp.maximum, ls)
            es = [jnp.exp(l - m) for l in ls]
            tot = functools.reduce(lambda a, b: a + b, es)
            out = functools.reduce(lambda a, b: a + b, [(e / tot) * o_s[p][rows, :] for p, e in enumerate(es)])
            out_ref[rows, :] = out.astype(BF)
            outt_ref[:, rows] = out.T.astype(BF)
            g_ref[rows, :] = m + jnp.log(tot)

    full = pl.BlockSpec((s, HD), lambda h: (0, h))
    return pl.pallas_call(
        body, name=name, grid=(N_HEADS,),
        in_specs=[full, pl.BlockSpec((s, HD), lambda h: (0, N_HEADS + h)), pl.BlockSpec((s, HD), lambda h: (0, 2 * N_HEADS + h))],
        out_specs=[full, pl.BlockSpec((HD, s), lambda h: (h, 0)), full],
        out_shape=[_sds((s, N_HEADS * HD), BF), _sds((N_HEADS * HD, s), BF), _sds((s, N_HEADS * HD), F32)],
        scratch_shapes=[pltpu.VMEM((s, HD), F32)] + [pltpu.VMEM((s + DIL_PAD, HD), F32)] * 2 + [pltpu.VMEM((s, HD), F32)] * (2 * npat),
        compiler_params=_cp("parallel"),
    )(qkv, qkv, qkv)


def _dilated_bwd(qkv, out, glse, do, tables, name):
    s = qkv.shape[0]
    chunk = 256

    def body(q_ref, k_ref, v_ref, out_ref, g_ref, do_ref, c_ref, sa_ref, sb_ref, dq_ref, dk_ref, dv_ref,
             qf, kf, vf, dof, dl_s, dq_a, dk_a, dv_a):
        qf[...] = q_ref[...].astype(F32)
        for dst, src in ((kf, k_ref), (vf, v_ref)):
            dst[0:DIL_PAD, :] = jnp.zeros((DIL_PAD, HD), F32)
            dst[DIL_PAD:, :] = src[...].astype(F32)
        for c0 in range(0, s, chunk):
            rows = slice(c0, c0 + chunk)
            dov = do_ref[rows, :].astype(F32)
            dof[rows, :] = dov
            dl_s[rows, :] = jnp.broadcast_to(jnp.sum(dov * out_ref[rows, :].astype(F32), axis=1, keepdims=True), (chunk, HD))
        dq_a[...] = jnp.zeros_like(dq_a)
        dk_a[...] = jnp.zeros_like(dk_a)
        dv_a[...] = jnp.zeros_like(dv_a)
        for _, d in DILATED_PATTERNS:
            nb = s // d // QB

            def grp(g, carry, d=d, nb=nb):
                mc, mp = _dil_masks(g, d, nb)
                q = _dil_load(qf, g, d, nb).astype(BF)
                kc = _dil_load(kf, g, d, nb, DIL_PAD).astype(BF)
                dov = _dil_load(dof, g, d, nb).astype(BF)
                lse = _dil_load(g_ref, g, d, nb)[:, :, 0:1]
                delta = _dil_load(dl_s, g, d, nb)[:, :, 0:1]
                pc = jnp.where(mc, jnp.exp(_bdot_nt(q, kc) * SCALE - lse), 0.0)
                dzc = pc * (_bdot_nt(dov, _dil_load(vf, g, d, nb, DIL_PAD).astype(BF)) - delta) * SCALE
                dq = _bdot(dzc.astype(BF), kc)
                if nb > 1:
                    kp = _dil_load(kf, g, d, nb, DIL_PAD, -1).astype(BF)
                    pp = jnp.where(mp, jnp.exp(_bdot_nt(q, kp) * SCALE - lse), 0.0)
                    dzp = pp * (_bdot_nt(dov, _dil_load(vf, g, d, nb, DIL_PAD, -1).astype(BF)) - delta) * SCALE
                    dq = dq + _bdot(dzp.astype(BF), kp)
                _dil_store(dq_a, g, d, nb, dq, add=True)
                _dil_store(dk_a, g, d, nb, _bdot_tn(dzc, q), DIL_PAD, add=True)
                _dil_store(dv_a, g, d, nb, _bdot_tn(pc, dov), DIL_PAD, add=True)
                if nb > 1:
                    _dil_store(dk_a, g, d, nb, _bdot_tn(dzp, q), DIL_PAD, -1, add=True)
                    _dil_store(dv_a, g, d, nb, _bdot_tn(pp, dov), DIL_PAD, -1, add=True)
                return carry

            lax.fori_loop(0, s // (QB * GB), grp, 0)
        for c0 in range(0, s, chunk):
            rows = slice(c0, c0 + chunk)
            padded = slice(DIL_PAD + c0, DIL_PAD + c0 + chunk)
            c, sa, sb = c_ref[rows, :], sa_ref[rows, :], sb_ref[rows, :]
            dq_ref[rows, :] = _rope(dq_a[rows, :], c, sa, sb).astype(BF)
            dk_ref[rows, :] = _rope(dk_a[padded, :], c, sa, sb).astype(BF)
            dv_ref[rows, :] = dv_a[padded, :].astype(BF)

    full = pl.BlockSpec((s, HD), lambda h: (0, h))
    tab = pl.BlockSpec((s, HD), lambda h: (0, 0))
    return pl.pallas_call(
        body, name=name, grid=(N_HEADS,),
        in_specs=[full, pl.BlockSpec((s, HD), lambda h: (0, N_HEADS + h)), pl.BlockSpec((s, HD), lambda h: (0, 2 * N_HEADS + h)),
                  full, full, full, tab, tab, tab],
        out_specs=[full, full, full], out_shape=[_sds((s, N_HEADS * HD), BF)] * 3,
        scratch_shapes=[pltpu.VMEM((s, HD), F32)] + [pltpu.VMEM((s + DIL_PAD, HD), F32)] * 2 + [pltpu.VMEM((s, HD), F32)] * 3
        + [pltpu.VMEM((s + DIL_PAD, HD), F32)] * 2,
        compiler_params=_cp("parallel"),
    )(qkv, qkv, qkv, out, glse, do, *tables)


def _swiglu_fwd(x, gnorm, w, tag, deps=()):
    h, ht = _rmsnorm_fwd(x, gnorm, f"norm_{tag}", deps)
    g, u, act, act_t = _ffn_up(h, w["gate"], w["up"], f"ffn_up_{tag}")
    if callable(w["down"]):
        w["down"] = w["down"](g)
    y = _ffn_down(act, w["down"], x, f"ffn_down_{tag}")
    return y, (x, ht, g, u, act_t)


def _swiglu_bwd(saved, gnorm, w, dy, dyb_half, out_scale, tag, deps=(), on_down=None, on_grads=None):
    x, ht, g, u, act_t = saved
    dg, du = _ffn_bwd_act(dyb_half, w["down"], g, u, f"ffn_bwd_act_{tag}", deps)
    if on_down:
        d_gate, d_up = _grad_cols(ht, [dg, du], [False, False], f"ffn_bwd_wgu_{tag}")
        tokens = list(on_grads({"gate": d_gate, "up": d_up}))
        d_down = _grad_rows(act_t, dyb_half, f"ffn_bwd_wd_{tag}", tokens)
        gw = {"gate": d_gate, "up": d_up, "down": d_down}
        tokens = list(on_down(d_down))
    else:
        d_down = _grad_rows(act_t, dyb_half, f"ffn_bwd_wd_{tag}")
        d_gate, d_up = _grad_cols(ht, [dg, du], [False, False], f"ffn_bwd_wgu_{tag}")
        gw = {"gate": d_gate, "up": d_up, "down": d_down}
        tokens = list(on_grads(gw)) if on_grads else []
    dh = _dh_cols([dg, du], [w["gate"], w["up"]], [False, False], f"ffn_bwd_dh_{tag}", tokens)
    dx, dxb, dxbt, dgn = _rmsnorm_bwd(x, gnorm, dh, dy, out_scale, f"norm_bwd_{tag}")
    return (dx, dxb, dxbt), dgn, gw


def kernel(x, norm_g, ffn1_w_gate, ffn1_w_up, ffn1_w_down, ffn2_w_gate, ffn2_w_up, ffn2_w_down, even_w_in, even_b_forget, even_w_out, odd_w_qkv, odd_w_out, final_norm_g, loss_target, m_norm_g, m_ffn1_w_gate, m_ffn1_w_up, m_ffn1_w_down, m_ffn2_w_gate, m_ffn2_w_up, m_ffn2_w_down, m_even_w_in, m_even_b_forget, m_even_w_out, m_odd_w_qkv, m_odd_w_out, m_final_norm_g, v_norm_g, v_ffn1_w_gate, v_ffn1_w_up, v_ffn1_w_down, v_ffn2_w_gate, v_ffn2_w_up, v_ffn2_w_down, v_even_w_in, v_even_b_forget, v_even_w_out, v_odd_w_qkv, v_odd_w_out, v_final_norm_g):
    s, d = x.shape[1], x.shape[2]
    nfox = N_HEADS - N_SB
    ax, ay, ac = lax.axis_index("x"), lax.axis_index("y"), lax.axis_index("c")
    me = 4 * ax + 2 * ay + ac
    slots = jnp.stack([4 * px + 2 * py + ac for px, py in [(ax, ay), (1 - ax, ay), (ax, 1 - ay), (1 - ax, 1 - ay)]]).astype(jnp.int32)
    x0 = x.reshape(s, d)
    target = loss_target.reshape(s, d)

    def bf(w):
        return w.astype(BF)

    groups = [
        [bf(ffn1_w_gate[0]), bf(ffn1_w_up[0]), norm_g.reshape(6, d // NDEV)],
        [bf(even_w_in[0]), bf(even_w_out[0])],
        [bf(ffn2_w_gate[0]), bf(ffn2_w_up[0]), bf(ffn2_w_down[0])],
        [bf(ffn1_w_gate[1]), bf(ffn1_w_up[1]), bf(ffn1_w_down[1])],
        [bf(odd_w_qkv[0]), bf(odd_w_out[0])],
        [bf(ffn2_w_gate[1]), bf(ffn2_w_up[1]), bf(ffn2_w_down[1])],
        [bf(ffn1_w_down[0])],
    ]
    started = [None] * len(groups)
    last_token = []
    for k in (0, 6, 1, 2, 3, 4, 5):
        started[k] = _gather_start(groups[k], me, last_token, f"gather_start_{k}")
        last_token = [started[k]["token"]]
    all_started = last_token

    def forward_early(k, after):
        started[k] = _gather_forward(started[k], after, f"gather_forward_{k}")
        return [started[k]["token"]]

    def gathered(k, after):
        st = started[k] if "send2" in started[k] else _gather_forward(started[k], after, f"gather_forward_{k}")
        return _gather_finish(st, after, f"gather_finish_{k}")

    def ffn_weights(ws_):
        return {"gate": ws_[0], "up": ws_[1], "down": ws_[2]}

    b_pad = jnp.pad(even_b_forget, ((0, 0), (0, HD - nfox)))
    gfin = final_norm_g.reshape(1, d)

    g0 = gathered(0, x0)
    gn = jnp.transpose(g0[2], (1, 0, 2)).reshape(6, 1, d)
    wf = [[{"gate": g0[0], "up": g0[1], "down": lambda after: gathered(6, after)[0]}, None], [None, None]]
    x1, sv_f1_0 = _swiglu_fwd(x0, gn[0], wf[0][0], "l0a", all_started)
    g1 = gathered(1, x1)
    w_in_nat = jnp.transpose(g1[0], (1, 0, 2)).reshape(d, -1)
    w_qkv_e = w_in_nat[:, :3 * d]
    w_f = jnp.pad(w_in_nat[:, 3 * d:], ((0, 0), (0, HD - nfox)))
    w_out_e = g1[1].reshape(d, d)
    h_e, ht_e = _rmsnorm_fwd(x1, gn[1], "norm_l0m")
    qkv_e = _mm_nn(h_e, w_qkv_e, 768, BF, "even_qkv")
    f_e = _mm_nn(h_e, w_f, HD, F32, "even_fgate")
    o_sb, ot_sb, tot_sb = _sb_fwd(qkv_e, "sb_fwd")
    cum_b, cum_t = _fgate_fwd(f_e, b_pad, "fgate_fwd")
    o_fox, ot_fox, lse_fox = _fox_fwd(qkv_e, cum_b, cum_t, "fox_fwd")
    o_e = jnp.concatenate([o_sb, o_fox], axis=1)
    ot_e = jnp.concatenate([ot_sb, ot_fox], axis=0)
    x2 = _mm_nn(o_e, w_out_e, 1024, F32, "even_out", res=x1, deps=forward_early(2, o_e))
    wf[0][1] = ffn_weights(gathered(2, x2))
    x3, sv_f2_0 = _swiglu_fwd(x2, gn[2], wf[0][1], "l0b")

    wf[1][0] = ffn_weights(gathered(3, x3))
    x4, sv_f1_1 = _swiglu_fwd(x3, gn[3], wf[1][0], "l1a")
    g4 = gathered(4, x4)
    w_qkv_o = g4[0]
    w_out_o = g4[1].reshape(d, d)
    h_o, ht_o = _rmsnorm_fwd(x4, gn[4], "norm_l1m")
    qkv_o = _qkv_rope(h_o, w_qkv_o, _rope_tables(s, 1.0), "odd_qkv")
    o_o, ot_o, glse = _dilated_fwd(qkv_o, "dilated_fwd")
    x5 = _mm_nn(o_o, w_out_o, 1024, F32, "odd_out", res=x4)
    wf[1][1] = ffn_weights(gathered(5, x5))
    x6, sv_f2_1 = _swiglu_fwd(x5, gn[5], wf[1][1], "l1b")

    def chip_sums(gs, a_s, tag):
        ps = _pair_sum(gs, a_s, slots, f"pair_sum_{tag}")
        return gs, a_s, _chip_start(ps, f"chip_start_{tag}")

    def as_slices(gs):
        return [g_ if g_.ndim == 3 else g_.reshape(NDEV, g_.shape[0] // NDEV, g_.shape[1]) for g_ in gs]

    def reduce_start(gs, tag):
        gs = as_slices(gs)
        return chip_sums(gs, _pair_exchange(gs, f"pair_exchange_{tag}"), tag)

    red, crossing = {}, {}

    def cross(gs, tag):
        crossing[tag] = _pair_start(as_slices(gs), f"pair_start_{tag}")
        return [crossing[tag]["token"]]

    def reduce_behind_dh(tag):
        return lambda gw: cross([gw["gate"], gw["up"], gw["down"]], tag)

    def reduce_after(tag, after):
        red[tag] = chip_sums(*_pair_finish(crossing[tag], after, f"pair_finish_{tag}"), tag)
        return [red[tag][2]["token"]]

    def reduce_now(tag, names):
        def hook(gw):
            red[tag] = reduce_start([gw[nm] for nm in names] if names else [gw], tag)
            return [red[tag][2]["token"]]
        return hook

    dx6, dx6b, _, d_gfin, loss_part = _loss_head(x6, gfin, target, "loss_head")

    (dx5, dx5b, dx5bt), dgn5, _ = _swiglu_bwd(sv_f2_1, gn[5], wf[1][1], dx6, dx6b, 1.0, "l1b", on_grads=reduce_behind_dh("l1b"))
    d_wout_o = _mm_nn(ot_o, dx5b, 1024, BF, "odd_out_dw")
    do_o = _mm_nt([(dx5b, w_out_o)], "odd_out_do", BF, deps=reduce_after("l1b", dx5))
    dqkv_o = jnp.concatenate(_dilated_bwd(qkv_o, o_o, glse, do_o, _rope_tables(s, -1.0), "dilated_bwd"), axis=1)
    (d_wqkv_o,) = _grad_cols(ht_o, [dqkv_o], [True], "odd_qkv_dw")
    dh_o = _dh_cols([dqkv_o], [w_qkv_o], [True], "odd_qkv_dh", cross([d_wqkv_o, d_wout_o], "l1m"))
    dx4, dx4b, _, dgn4 = _rmsnorm_bwd(x4, gn[4], dh_o, dx5, 0.5, "norm_bwd_l1m")
    (dx3, dx3b, _), dgn3, _ = _swiglu_bwd(sv_f1_1, gn[3], wf[1][0], dx4, dx4b, 0.5, "l1a", reduce_after("l1m", dx4),
                                         on_grads=reduce_behind_dh("l1a"))

    (dx2, dx2b, dx2bt), dgn2, _ = _swiglu_bwd(sv_f2_0, gn[2], wf[0][1], dx3, dx3b, 1.0, "l0b", reduce_after("l1a", dx3),
                                             on_grads=reduce_behind_dh("l0b"))
    d_wout_e = _mm_nn(ot_e, dx2b, 1024, BF, "even_out_dw")
    do_e = _mm_nt([(dx2b, w_out_e)], "even_out_do", BF, deps=reduce_after("l0b", dx2))
    dq_sb, dk_sb, dv_sb = _sb_bwd(qkv_e, do_e, tot_sb, "sb_bwd")
    dq_fx, dk_fx, dv_fx, dcq, dck = _fox_bwd(qkv_e, cum_b, cum_t, o_fox, lse_fox, do_e, "fox_bwd")
    df, db_part = _fgate_bwd(dcq, dck, f_e, b_pad, "fgate_bwd")
    dqkv_e = jnp.concatenate([dq_sb, dq_fx, dk_sb, dk_fx, dv_sb, dv_fx], axis=1)
    d_wqkv_e = _mm_nn(ht_e, dqkv_e, 768, BF, "even_qkv_dw")
    d_wf = _mm_nn(ht_e, df, HD, BF, "even_fgate_dw")
    d_win_nat = jnp.concatenate([d_wqkv_e, d_wf[:, :nfox]], axis=1)
    d_win = jnp.transpose(d_win_nat.reshape(d, NDEV, -1), (1, 0, 2))
    dh_e = _mm_nt([(dqkv_e, w_qkv_e), (df, w_f)], "even_in_dh", deps=cross([d_win, d_wout_e], "l0m"))
    dx1, dx1b, _, dgn1 = _rmsnorm_bwd(x1, gn[1], dh_e, dx2, 0.5, "norm_bwd_l0m")
    (dx0, _, _), dgn0, _ = _swiglu_bwd(sv_f1_0, gn[0], wf[0][0], dx1, dx1b, 1.0, "l0a", reduce_after("l0m", dx1),
                                      on_down=reduce_now("l0a_down", None), on_grads=reduce_now("l0a_gu", ["gate", "up"]))

    def reduce_finish(red, tag, after):
        gs, a_s, st = red
        return list(zip(gs, a_s, _chip_finish(st, after, f"chip_finish_{tag}")))

    f_l1b, f_l1m, f_l1a = (reduce_finish(red[t], t, dx0) for t in ("l1b", "l1m", "l1a"))
    f_l0b, f_l0m = (reduce_finish(red[t], t, dx0) for t in ("l0b", "l0m"))

    def update(w_, m_, v_, parts, nm):
        if w_.shape[2] % 128 == 0:
            return _adamw_sharded(w_, m_, v_, parts, slots, f"adamw_{nm}")
        outs = _adamw_sharded(jnp.swapaxes(w_, 1, 2), jnp.swapaxes(m_, 1, 2), jnp.swapaxes(v_, 1, 2), parts, slots,
                              f"adamw_{nm}", transposed=True)
        return [jnp.swapaxes(o, 1, 2) for o in outs]

    res = {}
    res["even_w_in"] = update(even_w_in, m_even_w_in, v_even_w_in, [f_l0m[0]], "even_w_in")
    res["even_w_out"] = _adamw_sharded(even_w_out, m_even_w_out, v_even_w_out, [f_l0m[1]], slots, "adamw_even_w_out")
    res["odd_w_qkv"] = _adamw_sharded(odd_w_qkv, m_odd_w_qkv, v_odd_w_qkv, [f_l1m[0]], slots, "adamw_odd_w_qkv")
    res["odd_w_out"] = _adamw_sharded(odd_w_out, m_odd_w_out, v_odd_w_out, [f_l1m[1]], slots, "adamw_odd_w_out")
    names = ["ffn2_w_gate", "ffn2_w_up", "ffn2_w_down", "ffn1_w_gate", "ffn1_w_up", "ffn1_w_down"]
    ws = [ffn2_w_gate, ffn2_w_up, ffn2_w_down, ffn1_w_gate, ffn1_w_up, ffn1_w_down]
    ms = [m_ffn2_w_gate, m_ffn2_w_up, m_ffn2_w_down, m_ffn1_w_gate, m_ffn1_w_up, m_ffn1_w_down]
    vs = [v_ffn2_w_gate, v_ffn2_w_up, v_ffn2_w_down, v_ffn1_w_gate, v_ffn1_w_up, v_ffn1_w_down]
    for k in range(3):
        res[names[k]] = update(ws[k], ms[k], vs[k], [f_l0b[k], f_l1b[k]], names[k])
    f_l0a = (reduce_finish(red["l0a_gu"], "l0a_gu", res["ffn2_w_down"][1])
             + reduce_finish(red["l0a_down"], "l0a_down", res["ffn2_w_down"][1]))
    for k in range(3, 6):
        res[names[k]] = update(ws[k], ms[k], vs[k], [f_l0a[k - 3], f_l1a[k - 3]], names[k])

    dnorm = jnp.concatenate([dgn0, dgn1, dgn2, dgn3, dgn4, dgn5], axis=0)
    nsm = d // NDEV
    small_rows = (6 * d + d + 2 * HD) // HD
    pad_rows = -small_rows % 8
    part = jnp.concatenate([dnorm.reshape(-1), d_gfin.reshape(-1), db_part.reshape(-1), loss_part.reshape(-1),
                            jnp.zeros((pad_rows * HD,), F32)]).reshape(small_rows + pad_rows, HD)
    (gathered,) = _all_gather([part], "gather_small")

    def pack(ng, bfg, fg):
        full = lax.dynamic_update_slice(jnp.zeros((6, d), F32), ng.reshape(6, nsm), (0, me * nsm))
        return jnp.concatenate([full.reshape(-1), fg.reshape(-1), jnp.pad(bfg.reshape(-1), (0, HD - nfox)),
                                jnp.zeros((HD + pad_rows * HD,), F32)]).reshape(small_rows + pad_rows, HD)

    sm = _adamw_small(pack(norm_g, even_b_forget, final_norm_g), pack(m_norm_g, m_even_b_forget, m_final_norm_g),
                      pack(v_norm_g, v_even_b_forget, v_final_norm_g), gathered, "adamw_small")

    def unpack(t):
        flat = t.reshape(-1)
        ng = lax.dynamic_slice(flat[:6 * d].reshape(6, d), (0, me * nsm), (6, nsm)).reshape(norm_g.shape)
        fg = flat[6 * d:7 * d].reshape(final_norm_g.shape)
        bfg = flat[7 * d:7 * d + nfox].reshape(even_b_forget.shape)
        return ng, bfg, fg

    sm_g, sm_d, sm_m, sm_v = [unpack(t) for t in sm]
    loss = sm[0].reshape(-1)[7 * d + HD]

    order = ["norm_g", "ffn1_w_gate", "ffn1_w_up", "ffn1_w_down", "ffn2_w_gate", "ffn2_w_up", "ffn2_w_down", "even_w_in",
             "even_b_forget", "even_w_out", "odd_w_qkv", "odd_w_out", "final_norm_g"]
    outs = [loss, dx0.reshape(x.shape)]
    for k in range(4):
        smk = [sm_g, sm_d, sm_m, sm_v][k]
        for nm in order:
            if nm == "norm_g":
                outs.append(smk[0])
            elif nm == "even_b_forget":
                outs.append(smk[1])
            elif nm == "final_norm_g":
                outs.append(smk[2])
            else:
                outs.append(res[nm][k])
    return tuple(outs)
```
